```python
import jax, jax.numpy as jnp
from jax import lax
import numpy as np

D_MODEL = 1024
BATCH = 8
SEQ = 8192
DEPTH = 4

CHUNK = 64
N_A = DEPTH // 2
N_B = DEPTH - N_A
N_MEM = 256
MEM_HEADS = 4
MEM_HEAD_DIM = D_MODEL // 16
MEM_W = MEM_HEADS * MEM_HEAD_DIM
GM_W = D_MODEL - MEM_W
GM_GROUPS = 6
GM_GC = GM_W // GM_GROUPS
GM_CHUNK = 128
SB_W = D_MODEL - MEM_W
SB_HEAD_DIM = 64
SB_HEADS = SB_W // SB_HEAD_DIM
SB_BLOCK = 128
D_FF = 2816
EPS = 1e-6

kernel_name = "yoco_gmlp_stickbreaking_macaron_memory"


def rms_norm(x, g):
    xf = x.astype(jnp.float32)
    r = lax.rsqrt(jnp.mean(xf * xf, axis=-1, keepdims=True) + EPS)
    return (xf * r).astype(x.dtype) * g


def half_swiglu(x, g, w_gate, w_up, w_down):
    h = rms_norm(x, g)
    return (jax.nn.silu(h @ w_gate) * (h @ w_up)) @ w_down


def memory_attention(q, mem_kv):
    b, s, _ = q.shape
    k, v = jnp.split(mem_kv, 2, axis=-1)
    q = q.reshape(b, s, MEM_HEADS, MEM_HEAD_DIM)
    k = k.reshape(b, N_MEM, MEM_HEADS, MEM_HEAD_DIM)
    v = v.reshape(b, N_MEM, MEM_HEADS, MEM_HEAD_DIM)
    scores = jnp.einsum('bshd,bmhd->bhsm', q, k).astype(jnp.float32) * (MEM_HEAD_DIM ** -0.5)
    p = jax.nn.softmax(scores, axis=-1).astype(v.dtype)
    return jnp.einsum('bhsm,bmhd->bshd', p, v).reshape(b, s, MEM_W)


def gmlp_chunk_mask():
    pos = np.arange(GM_CHUNK)
    return jnp.asarray((pos[None, :] // CHUNK) <= (pos[:, None] // CHUNK))


def gmlp_spatial_gating(u, v, v_gain, w_s, b_s):
    b, s, _ = u.shape
    v = rms_norm(v, v_gain)
    vb = v.reshape(b, s // GM_CHUNK, GM_CHUNK, GM_GROUPS, GM_GC)
    w = jnp.where(gmlp_chunk_mask()[None], w_s, 0.0).astype(v.dtype)
    mixed = jnp.einsum('gts,bnsgc->bntgc', w, vb) + b_s.T[None, None, :, :, None]
    return u * mixed.reshape(b, s, GM_W)


def stick_breaking_attention(q, k, v):
    b, s, h, d = q.shape
    nblk = s // SB_BLOCK
    qb = q.reshape(b, nblk, SB_BLOCK, h, d).transpose(1, 0, 2, 3, 4)
    starts = jnp.arange(nblk, dtype=jnp.int32) * SB_BLOCK
    key_pos = jnp.arange(s, dtype=jnp.int32)
    scale = d ** -0.5

    def block(args):
        qi, start = args
        z = jnp.einsum('bqhd,bkhd->bhqk', qi, k).astype(jnp.float32) * scale
        qpos = start + jnp.arange(SB_BLOCK, dtype=jnp.int32)
        causal = key_pos[None, :] < qpos[:, None]
        log_beta = jax.nn.log_sigmoid(z)
        log_one_minus = jnp.where(causal, jax.nn.log_sigmoid(-z), 0.0)
        suffix = lax.cumsum(log_one_minus, axis=3, reverse=True) - log_one_minus
        a = jnp.where(causal, jnp.exp(log_beta + suffix), 0.0).astype(v.dtype)
        return jnp.einsum('bhqk,bkhd->bqhd', a, v)

    out = lax.map(block, (qb, starts))
    return out.transpose(1, 0, 2, 3, 4).reshape(b, s, h * d)


def _fwd_setup_inputs(seed: int = 0) -> dict:
    key = jax.random.key(seed)
    ks = iter(jax.random.split(key, 32))
    f32 = jnp.float32

    def w(shape, fan_in):
        return jax.random.normal(next(ks), shape, f32) * (fan_in ** -0.5)

    def gain(shape):
        return 1.0 + 0.02 * jax.random.normal(next(ks), shape, f32)

    return {
        "x": jax.random.normal(next(ks), (BATCH, SEQ, D_MODEL), f32),
        "mem": jax.random.normal(next(ks), (BATCH, N_MEM, D_MODEL), f32),
        "ffn1_norm": gain((DEPTH, D_MODEL)),
        "ffn1_w_gate": w((DEPTH, D_MODEL, D_FF), D_MODEL),
        "ffn1_w_up": w((DEPTH, D_MODEL, D_FF), D_MODEL),
        "ffn1_w_down": w((DEPTH, D_FF, D_MODEL), D_FF),
        "mix_norm": gain((DEPTH, D_MODEL)),
        "ffn2_norm": gain((DEPTH, D_MODEL)),
        "ffn2_w_gate": w((DEPTH, D_MODEL, D_FF), D_MODEL),
        "ffn2_w_up": w((DEPTH, D_MODEL, D_FF), D_MODEL),
        "ffn2_w_down": w((DEPTH, D_FF, D_MODEL), D_FF),
        "mem_norm": gain((D_MODEL,)),
        "w_mem_kv": w((DEPTH, D_MODEL, 2 * MEM_W), D_MODEL),
        "a_w_in": w((N_A, D_MODEL, 2 * GM_W + MEM_W), D_MODEL),
        "a_v_norm": gain((N_A, GM_W)),
        "a_w_spatial": w((N_A, GM_GROUPS, GM_CHUNK, GM_CHUNK), GM_CHUNK),
        "a_b_spatial": gain((N_A, GM_GROUPS, GM_CHUNK)),
        "a_w_out": w((N_A, GM_W + MEM_W, D_MODEL), GM_W + MEM_W),
        "kv_norm": gain((D_MODEL,)),
        "w_kv": w((D_MODEL, 2 * SB_W), D_MODEL),
        "b_w_in": w((N_B, D_MODEL, SB_W + MEM_W), D_MODEL),
        "b_w_out": w((N_B, SB_W + MEM_W, D_MODEL), SB_W + MEM_W),
        "final_norm": gain((D_MODEL,)),
    }


def _fwd_reference(x, mem, ffn1_norm, ffn1_w_gate, ffn1_w_up, ffn1_w_down, mix_norm,
              ffn2_norm, ffn2_w_gate, ffn2_w_up, ffn2_w_down, mem_norm, w_mem_kv,
              a_w_in, a_v_norm, a_w_spatial, a_b_spatial, a_w_out,
              kv_norm, w_kv, b_w_in, b_w_out, final_norm):
    b, s, _ = x.shape
    mem_h = rms_norm(mem, mem_norm)
    shared_k = shared_v = None
    for l in range(DEPTH):
        if l == N_A:
            kv = rms_norm(x, kv_norm) @ w_kv
            shared_k, shared_v = jnp.split(kv, 2, axis=-1)
            shared_k = shared_k.reshape(b, s, SB_HEADS, SB_HEAD_DIM)
            shared_v = shared_v.reshape(b, s, SB_HEADS, SB_HEAD_DIM)

        x = x + 0.5 * half_swiglu(x, ffn1_norm[l], ffn1_w_gate[l], ffn1_w_up[l], ffn1_w_down[l])

        h = rms_norm(x, mix_norm[l])
        mem_kv = mem_h @ w_mem_kv[l]
        if l < N_A:
            i = l
            proj = h @ a_w_in[i]
            uv = jax.nn.gelu(proj[..., :2 * GM_W])
            u, v = uv[..., :GM_W], uv[..., GM_W:]
            q_mem = proj[..., 2 * GM_W:]
            y_tok = gmlp_spatial_gating(u, v, a_v_norm[i], a_w_spatial[i], a_b_spatial[i])
            y = jnp.concatenate([y_tok, memory_attention(q_mem, mem_kv)], axis=-1)
            x = x + y @ a_w_out[i]
        else:
            j = l - N_A
            proj = h @ b_w_in[j]
            q_sb = proj[..., :SB_W].reshape(b, s, SB_HEADS, SB_HEAD_DIM)
            q_mem = proj[..., SB_W:]
            y_tok = stick_breaking_attention(q_sb, shared_k, shared_v)
            y = jnp.concatenate([y_tok, memory_attention(q_mem, mem_kv)], axis=-1)
            x = x + y @ b_w_out[j]

        x = x + 0.5 * half_swiglu(x, ffn2_norm[l], ffn2_w_gate[l], ffn2_w_up[l], ffn2_w_down[l])
    return rms_norm(x, final_norm)


import jax as _jax
import jax.numpy as _jnp

TWIN_FORMAT = 'train_step'
FWD_PARAMS = ['x', 'mem', 'ffn1_norm', 'ffn1_w_gate', 'ffn1_w_up', 'ffn1_w_down', 'mix_norm', 'ffn2_norm', 'ffn2_w_gate', 'ffn2_w_up', 'ffn2_w_down', 'mem_norm', 'w_mem_kv', 'a_w_in', 'a_v_norm', 'a_w_spatial', 'a_b_spatial', 'a_w_out', 'kv_norm', 'w_kv', 'b_w_in', 'b_w_out', 'final_norm']
TWIN_WEIGHTS = ['ffn1_norm', 'ffn1_w_gate', 'ffn1_w_up', 'ffn1_w_down', 'mix_norm', 'ffn2_norm', 'ffn2_w_gate', 'ffn2_w_up', 'ffn2_w_down', 'mem_norm', 'w_mem_kv', 'a_w_in', 'a_v_norm', 'a_w_spatial', 'a_b_spatial', 'a_w_out', 'kv_norm', 'w_kv', 'b_w_in', 'b_w_out', 'final_norm']
TWIN_DIFF_INPUT = 'x'
TWIN_INPUTS = ['x', 'mem', 'ffn1_norm', 'ffn1_w_gate', 'ffn1_w_up', 'ffn1_w_down', 'mix_norm', 'ffn2_norm', 'ffn2_w_gate', 'ffn2_w_up', 'ffn2_w_down', 'mem_norm', 'w_mem_kv', 'a_w_in', 'a_v_norm', 'a_w_spatial', 'a_b_spatial', 'a_w_out', 'kv_norm', 'w_kv', 'b_w_in', 'b_w_out', 'final_norm', 'loss_target', 'm_ffn1_norm', 'm_ffn1_w_gate', 'm_ffn1_w_up', 'm_ffn1_w_down', 'm_mix_norm', 'm_ffn2_norm', 'm_ffn2_w_gate', 'm_ffn2_w_up', 'm_ffn2_w_down', 'm_mem_norm', 'm_w_mem_kv', 'm_a_w_in', 'm_a_v_norm', 'm_a_w_spatial', 'm_a_b_spatial', 'm_a_w_out', 'm_kv_norm', 'm_w_kv', 'm_b_w_in', 'm_b_w_out', 'm_final_norm', 'v_ffn1_norm', 'v_ffn1_w_gate', 'v_ffn1_w_up', 'v_ffn1_w_down', 'v_mix_norm', 'v_ffn2_norm', 'v_ffn2_w_gate', 'v_ffn2_w_up', 'v_ffn2_w_down', 'v_mem_norm', 'v_w_mem_kv', 'v_a_w_in', 'v_a_v_norm', 'v_a_w_spatial', 'v_a_b_spatial', 'v_a_w_out', 'v_kv_norm', 'v_w_kv', 'v_b_w_in', 'v_b_w_out', 'v_final_norm']
TWIN_OUTPUTS = ['loss', 'grad_x', 'grad_ffn1_norm', 'grad_ffn1_w_gate', 'grad_ffn1_w_up', 'grad_ffn1_w_down', 'grad_mix_norm', 'grad_ffn2_norm', 'grad_ffn2_w_gate', 'grad_ffn2_w_up', 'grad_ffn2_w_down', 'grad_mem_norm', 'grad_w_mem_kv', 'grad_a_w_in', 'grad_a_v_norm', 'grad_a_w_spatial', 'grad_a_b_spatial', 'grad_a_w_out', 'grad_kv_norm', 'grad_w_kv', 'grad_b_w_in', 'grad_b_w_out', 'grad_final_norm', 'delta_ffn1_norm', 'delta_ffn1_w_gate', 'delta_ffn1_w_up', 'delta_ffn1_w_down', 'delta_mix_norm', 'delta_ffn2_norm', 'delta_ffn2_w_gate', 'delta_ffn2_w_up', 'delta_ffn2_w_down', 'delta_mem_norm', 'delta_w_mem_kv', 'delta_a_w_in', 'delta_a_v_norm', 'delta_a_w_spatial', 'delta_a_b_spatial', 'delta_a_w_out', 'delta_kv_norm', 'delta_w_kv', 'delta_b_w_in', 'delta_b_w_out', 'delta_final_norm', 'new_m_ffn1_norm', 'new_m_ffn1_w_gate', 'new_m_ffn1_w_up', 'new_m_ffn1_w_down', 'new_m_mix_norm', 'new_m_ffn2_norm', 'new_m_ffn2_w_gate', 'new_m_ffn2_w_up', 'new_m_ffn2_w_down', 'new_m_mem_norm', 'new_m_w_mem_kv', 'new_m_a_w_in', 'new_m_a_v_norm', 'new_m_a_w_spatial', 'new_m_a_b_spatial', 'new_m_a_w_out', 'new_m_kv_norm', 'new_m_w_kv', 'new_m_b_w_in', 'new_m_b_w_out', 'new_m_final_norm', 'new_v_ffn1_norm', 'new_v_ffn1_w_gate', 'new_v_ffn1_w_up', 'new_v_ffn1_w_down', 'new_v_mix_norm', 'new_v_ffn2_norm', 'new_v_ffn2_w_gate', 'new_v_ffn2_w_up', 'new_v_ffn2_w_down', 'new_v_mem_norm', 'new_v_w_mem_kv', 'new_v_a_w_in', 'new_v_a_v_norm', 'new_v_a_w_spatial', 'new_v_a_b_spatial', 'new_v_a_w_out', 'new_v_kv_norm', 'new_v_w_kv', 'new_v_b_w_in', 'new_v_b_w_out', 'new_v_final_norm']
TWIN_LEAF_KINDS = {'loss': 'loss', 'grad_x': 'grad_x', 'grad_ffn1_norm': 'grad_w', 'grad_ffn1_w_gate': 'grad_w', 'grad_ffn1_w_up': 'grad_w', 'grad_ffn1_w_down': 'grad_w', 'grad_mix_norm': 'grad_w', 'grad_ffn2_norm': 'grad_w', 'grad_ffn2_w_gate': 'grad_w', 'grad_ffn2_w_up': 'grad_w', 'grad_ffn2_w_down': 'grad_w', 'grad_mem_norm': 'grad_w', 'grad_w_mem_kv': 'grad_w', 'grad_a_w_in': 'grad_w', 'grad_a_v_norm': 'grad_w', 'grad_a_w_spatial': 'grad_w', 'grad_a_b_spatial': 'grad_w', 'grad_a_w_out': 'grad_w', 'grad_kv_norm': 'grad_w', 'grad_w_kv': 'grad_w', 'grad_b_w_in': 'grad_w', 'grad_b_w_out': 'grad_w', 'grad_final_norm': 'grad_w', 'delta_ffn1_norm': 'delta_w', 'delta_ffn1_w_gate': 'delta_w', 'delta_ffn1_w_up': 'delta_w', 'delta_ffn1_w_down': 'delta_w', 'delta_mix_norm': 'delta_w', 'delta_ffn2_norm': 'delta_w', 'delta_ffn2_w_gate': 'delta_w', 'delta_ffn2_w_up': 'delta_w', 'delta_ffn2_w_down': 'delta_w', 'delta_mem_norm': 'delta_w', 'delta_w_mem_kv': 'delta_w', 'delta_a_w_in': 'delta_w', 'delta_a_v_norm': 'delta_w', 'delta_a_w_spatial': 'delta_w', 'delta_a_b_spatial': 'delta_w', 'delta_a_w_out': 'delta_w', 'delta_kv_norm': 'delta_w', 'delta_w_kv': 'delta_w', 'delta_b_w_in': 'delta_w', 'delta_b_w_out': 'delta_w', 'delta_final_norm': 'delta_w', 'new_m_ffn1_norm': 'new_m', 'new_m_ffn1_w_gate': 'new_m', 'new_m_ffn1_w_up': 'new_m', 'new_m_ffn1_w_down': 'new_m', 'new_m_mix_norm': 'new_m', 'new_m_ffn2_norm': 'new_m', 'new_m_ffn2_w_gate': 'new_m', 'new_m_ffn2_w_up': 'new_m', 'new_m_ffn2_w_down': 'new_m', 'new_m_mem_norm': 'new_m', 'new_m_w_mem_kv': 'new_m', 'new_m_a_w_in': 'new_m', 'new_m_a_v_norm': 'new_m', 'new_m_a_w_spatial': 'new_m', 'new_m_a_b_spatial': 'new_m', 'new_m_a_w_out': 'new_m', 'new_m_kv_norm': 'new_m', 'new_m_w_kv': 'new_m', 'new_m_b_w_in': 'new_m', 'new_m_b_w_out': 'new_m', 'new_m_final_norm': 'new_m', 'new_v_ffn1_norm': 'new_v', 'new_v_ffn1_w_gate': 'new_v', 'new_v_ffn1_w_up': 'new_v', 'new_v_ffn1_w_down': 'new_v', 'new_v_mix_norm': 'new_v', 'new_v_ffn2_norm': 'new_v', 'new_v_ffn2_w_gate': 'new_v', 'new_v_ffn2_w_up': 'new_v', 'new_v_ffn2_w_down': 'new_v', 'new_v_mem_norm': 'new_v', 'new_v_w_mem_kv': 'new_v', 'new_v_a_w_in': 'new_v', 'new_v_a_v_norm': 'new_v', 'new_v_a_w_spatial': 'new_v', 'new_v_a_b_spatial': 'new_v', 'new_v_a_w_out': 'new_v', 'new_v_kv_norm': 'new_v', 'new_v_w_kv': 'new_v', 'new_v_b_w_in': 'new_v', 'new_v_b_w_out': 'new_v', 'new_v_final_norm': 'new_v'}


def _forward(args):
    return _fwd_reference(*[args[k] for k in FWD_PARAMS])


def _output_shape():
    def fwd():
        inp = _fwd_setup_inputs(0)
        return _fwd_reference(*[inp[k] for k in FWD_PARAMS])
    out = _jax.eval_shape(fwd)
    return out.shape, out.dtype

N_MICROBATCH = 1
ADAM_LR = 0.001
ADAM_B1 = 0.9
ADAM_B2 = 0.999
ADAM_EPS = 1e-08
ADAM_WD = 0.01
ADAM_STEP = 10
PER_EXAMPLE_BATCH_AXIS = {'x': 0, 'mem': 0, 'loss_target': 0}
SHARED_INPUTS = []
_WEIGHT_DTYPES = {'ffn1_norm': _jnp.float32, 'ffn1_w_gate': _jnp.float32, 'ffn1_w_up': _jnp.float32, 'ffn1_w_down': _jnp.float32, 'mix_norm': _jnp.float32, 'ffn2_norm': _jnp.float32, 'ffn2_w_gate': _jnp.float32, 'ffn2_w_up': _jnp.float32, 'ffn2_w_down': _jnp.float32, 'mem_norm': _jnp.float32, 'w_mem_kv': _jnp.float32, 'a_w_in': _jnp.float32, 'a_v_norm': _jnp.float32, 'a_w_spatial': _jnp.float32, 'a_b_spatial': _jnp.float32, 'a_w_out': _jnp.float32, 'kv_norm': _jnp.float32, 'w_kv': _jnp.float32, 'b_w_in': _jnp.float32, 'b_w_out': _jnp.float32, 'final_norm': _jnp.float32}
MOMENT_SCALE = {'ffn1_norm': 9.553506e-02, 'ffn1_w_gate': 4.098637e-02, 'ffn1_w_up': 3.978642e-02, 'ffn1_w_down': 6.600693e-02, 'mix_norm': 1.391988e-01, 'ffn2_norm': 7.557528e-02, 'ffn2_w_gate': 3.246465e-02, 'ffn2_w_up': 3.163287e-02, 'ffn2_w_down': 5.251755e-02, 'mem_norm': 2.956717e-02, 'w_mem_kv': 1.959307e-02, 'a_w_in': 1.481900e-01, 'a_v_norm': 1.188041e-01, 'a_w_spatial': 1.202544e-01, 'a_b_spatial': 1.455804e-01, 'a_w_out': 1.712739e-01, 'kv_norm': 1.306426e-01, 'w_kv': 1.108069e-01, 'b_w_in': 3.684100e-02, 'b_w_out': 9.105945e-02, 'final_norm': 6.414025e+01}


def _to_microbatches(a, axis):
    t = _jnp.moveaxis(a, axis, 0)
    t = t.reshape((N_MICROBATCH, t.shape[0] // N_MICROBATCH) + t.shape[1:])
    return _jnp.moveaxis(t, 1, axis + 1)


def setup_inputs(seed: int = 0) -> dict:
    inp = _fwd_setup_inputs(seed)
    key = _jax.random.fold_in(_jax.random.key(seed), 7919)
    shape, _ = _output_shape()
    out = dict(inp)
    out["loss_target"] = _jax.random.normal(_jax.random.fold_in(key, 0), shape, _jnp.float32)
    for i, name in enumerate(TWIN_WEIGHTS):
        w = inp[name].astype(_jnp.float32)
        if MOMENT_SCALE is None:
            s = _jnp.sqrt(_jnp.mean(_jnp.square(w)) + 1e-30)
        else:
            s = MOMENT_SCALE[name]
        km, kv = _jax.random.split(_jax.random.fold_in(key, i + 1))
        out[name] = w
        out["m_" + name] = s * _jax.random.normal(km, w.shape, _jnp.float32)
        out["v_" + name] = (s * s) * _jax.random.uniform(kv, w.shape, _jnp.float32, 0.5, 1.5)
    if N_MICROBATCH > 1:
        for name, axis in PER_EXAMPLE_BATCH_AXIS.items():
            out[name] = _to_microbatches(out[name], axis)
    return {'x': out['x'], 'mem': out['mem'], 'ffn1_norm': out['ffn1_norm'], 'ffn1_w_gate': out['ffn1_w_gate'], 'ffn1_w_up': out['ffn1_w_up'], 'ffn1_w_down': out['ffn1_w_down'], 'mix_norm': out['mix_norm'], 'ffn2_norm': out['ffn2_norm'], 'ffn2_w_gate': out['ffn2_w_gate'], 'ffn2_w_up': out['ffn2_w_up'], 'ffn2_w_down': out['ffn2_w_down'], 'mem_norm': out['mem_norm'], 'w_mem_kv': out['w_mem_kv'], 'a_w_in': out['a_w_in'], 'a_v_norm': out['a_v_norm'], 'a_w_spatial': out['a_w_spatial'], 'a_b_spatial': out['a_b_spatial'], 'a_w_out': out['a_w_out'], 'kv_norm': out['kv_norm'], 'w_kv': out['w_kv'], 'b_w_in': out['b_w_in'], 'b_w_out': out['b_w_out'], 'final_norm': out['final_norm'], 'loss_target': out['loss_target'], 'm_ffn1_norm': out['m_ffn1_norm'], 'm_ffn1_w_gate': out['m_ffn1_w_gate'], 'm_ffn1_w_up': out['m_ffn1_w_up'], 'm_ffn1_w_down': out['m_ffn1_w_down'], 'm_mix_norm': out['m_mix_norm'], 'm_ffn2_norm': out['m_ffn2_norm'], 'm_ffn2_w_gate': out['m_ffn2_w_gate'], 'm_ffn2_w_up': out['m_ffn2_w_up'], 'm_ffn2_w_down': out['m_ffn2_w_down'], 'm_mem_norm': out['m_mem_norm'], 'm_w_mem_kv': out['m_w_mem_kv'], 'm_a_w_in': out['m_a_w_in'], 'm_a_v_norm': out['m_a_v_norm'], 'm_a_w_spatial': out['m_a_w_spatial'], 'm_a_b_spatial': out['m_a_b_spatial'], 'm_a_w_out': out['m_a_w_out'], 'm_kv_norm': out['m_kv_norm'], 'm_w_kv': out['m_w_kv'], 'm_b_w_in': out['m_b_w_in'], 'm_b_w_out': out['m_b_w_out'], 'm_final_norm': out['m_final_norm'], 'v_ffn1_norm': out['v_ffn1_norm'], 'v_ffn1_w_gate': out['v_ffn1_w_gate'], 'v_ffn1_w_up': out['v_ffn1_w_up'], 'v_ffn1_w_down': out['v_ffn1_w_down'], 'v_mix_norm': out['v_mix_norm'], 'v_ffn2_norm': out['v_ffn2_norm'], 'v_ffn2_w_gate': out['v_ffn2_w_gate'], 'v_ffn2_w_up': out['v_ffn2_w_up'], 'v_ffn2_w_down': out['v_ffn2_w_down'], 'v_mem_norm': out['v_mem_norm'], 'v_w_mem_kv': out['v_w_mem_kv'], 'v_a_w_in': out['v_a_w_in'], 'v_a_v_norm': out['v_a_v_norm'], 'v_a_w_spatial': out['v_a_w_spatial'], 'v_a_b_spatial': out['v_a_b_spatial'], 'v_a_w_out': out['v_a_w_out'], 'v_kv_norm': out['v_kv_norm'], 'v_w_kv': out['v_w_kv'], 'v_b_w_in': out['v_b_w_in'], 'v_b_w_out': out['v_b_w_out'], 'v_final_norm': out['v_final_norm']}


def _loss(weights, diff, rest, loss_target):
    with _jax.named_scope("forward"):
        args = {**rest, TWIN_DIFF_INPUT: diff, **{k: w.astype(_WEIGHT_DTYPES[k]) for k, w in weights.items()}}
        y = _forward(args)
    with _jax.named_scope("loss_head"):
        err = _jnp.square(y.astype(_jnp.float32) - loss_target)
        return 0.5 * _jnp.sum(_jnp.mean(err, axis=-1)) if err.ndim else 0.5 * err


def _adamw(w, g, m, v):
    m = ADAM_B1 * m + (1.0 - ADAM_B1) * g
    v = ADAM_B2 * v + (1.0 - ADAM_B2) * _jnp.square(g)
    m_hat = m / (1.0 - ADAM_B1 ** ADAM_STEP)
    v_hat = v / (1.0 - ADAM_B2 ** ADAM_STEP)
    delta = -ADAM_LR * (m_hat / (_jnp.sqrt(v_hat) + ADAM_EPS) + ADAM_WD * w)
    return delta, m, v


def reference(x, mem, ffn1_norm, ffn1_w_gate, ffn1_w_up, ffn1_w_down, mix_norm, ffn2_norm, ffn2_w_gate, ffn2_w_up, ffn2_w_down, mem_norm, w_mem_kv, a_w_in, a_v_norm, a_w_spatial, a_b_spatial, a_w_out, kv_norm, w_kv, b_w_in, b_w_out, final_norm, loss_target, m_ffn1_norm, m_ffn1_w_gate, m_ffn1_w_up, m_ffn1_w_down, m_mix_norm, m_ffn2_norm, m_ffn2_w_gate, m_ffn2_w_up, m_ffn2_w_down, m_mem_norm, m_w_mem_kv, m_a_w_in, m_a_v_norm, m_a_w_spatial, m_a_b_spatial, m_a_w_out, m_kv_norm, m_w_kv, m_b_w_in, m_b_w_out, m_final_norm, v_ffn1_norm, v_ffn1_w_gate, v_ffn1_w_up, v_ffn1_w_down, v_mix_norm, v_ffn2_norm, v_ffn2_w_gate, v_ffn2_w_up, v_ffn2_w_down, v_mem_norm, v_w_mem_kv, v_a_w_in, v_a_v_norm, v_a_w_spatial, v_a_b_spatial, v_a_w_out, v_kv_norm, v_w_kv, v_b_w_in, v_b_w_out, v_final_norm):
    given = dict(x=x, mem=mem, ffn1_norm=ffn1_norm, ffn1_w_gate=ffn1_w_gate, ffn1_w_up=ffn1_w_up, ffn1_w_down=ffn1_w_down, mix_norm=mix_norm, ffn2_norm=ffn2_norm, ffn2_w_gate=ffn2_w_gate, ffn2_w_up=ffn2_w_up, ffn2_w_down=ffn2_w_down, mem_norm=mem_norm, w_mem_kv=w_mem_kv, a_w_in=a_w_in, a_v_norm=a_v_norm, a_w_spatial=a_w_spatial, a_b_spatial=a_b_spatial, a_w_out=a_w_out, kv_norm=kv_norm, w_kv=w_kv, b_w_in=b_w_in, b_w_out=b_w_out, final_norm=final_norm, loss_target=loss_target, m_ffn1_norm=m_ffn1_norm, m_ffn1_w_gate=m_ffn1_w_gate, m_ffn1_w_up=m_ffn1_w_up, m_ffn1_w_down=m_ffn1_w_down, m_mix_norm=m_mix_norm, m_ffn2_norm=m_ffn2_norm, m_ffn2_w_gate=m_ffn2_w_gate, m_ffn2_w_up=m_ffn2_w_up, m_ffn2_w_down=m_ffn2_w_down, m_mem_norm=m_mem_norm, m_w_mem_kv=m_w_mem_kv, m_a_w_in=m_a_w_in, m_a_v_norm=m_a_v_norm, m_a_w_spatial=m_a_w_spatial, m_a_b_spatial=m_a_b_spatial, m_a_w_out=m_a_w_out, m_kv_norm=m_kv_norm, m_w_kv=m_w_kv, m_b_w_in=m_b_w_in, m_b_w_out=m_b_w_out, m_final_norm=m_final_norm, v_ffn1_norm=v_ffn1_norm, v_ffn1_w_gate=v_ffn1_w_gate, v_ffn1_w_up=v_ffn1_w_up, v_ffn1_w_down=v_ffn1_w_down, v_mix_norm=v_mix_norm, v_ffn2_norm=v_ffn2_norm, v_ffn2_w_gate=v_ffn2_w_gate, v_ffn2_w_up=v_ffn2_w_up, v_ffn2_w_down=v_ffn2_w_down, v_mem_norm=v_mem_norm, v_w_mem_kv=v_w_mem_kv, v_a_w_in=v_a_w_in, v_a_v_norm=v_a_v_norm, v_a_w_spatial=v_a_w_spatial, v_a_b_spatial=v_a_b_spatial, v_a_w_out=v_a_w_out, v_kv_norm=v_kv_norm, v_w_kv=v_w_kv, v_b_w_in=v_b_w_in, v_b_w_out=v_b_w_out, v_final_norm=v_final_norm)
    weights = {n: given[n] for n in TWIN_WEIGHTS}
    shared = {n: given[n] for n in SHARED_INPUTS}
    per_example = {n: given[n] for n in ['x', 'mem']}
    grad_fn = _jax.value_and_grad(_loss, argnums=(0, 1))

    def one_microbatch(ex, loss_target):
        ex = dict(ex)
        diff = ex.pop(TWIN_DIFF_INPUT)
        return grad_fn(weights, diff, {**shared, **ex}, loss_target)

    if N_MICROBATCH == 1:
        loss, (grad_w, grad_x) = one_microbatch(per_example, given["loss_target"])
    else:
        def body(carry, xs):
            loss_sum, grad_sum = carry
            l_k, (gw_k, gx_k) = one_microbatch(xs[0], xs[1])
            with _jax.named_scope("update"):
                return (loss_sum + l_k, _jax.tree.map(_jnp.add, grad_sum, gw_k)), gx_k

        init = (_jnp.zeros((), _jnp.float32), _jax.tree.map(_jnp.zeros_like, weights))
        (loss, grad_w), grad_x = _jax.lax.scan(body, init, (per_example, given["loss_target"]))
    with _jax.named_scope("update"):
        delta_w, new_m, new_v = {}, {}, {}
        for n in TWIN_WEIGHTS:
            delta_w[n], new_m[n], new_v[n] = _adamw(weights[n], grad_w[n], given["m_" + n], given["v_" + n])
    return (loss, grad_x, *[grad_w[n] for n in TWIN_WEIGHTS], *[delta_w[n] for n in TWIN_WEIGHTS],
            *[new_m[n] for n in TWIN_WEIGHTS], *[new_v[n] for n in TWIN_WEIGHTS])
```

```python
import functools
import math

import jax
import jax.numpy as jnp
from jax import lax
from jax.experimental import pallas as pl
from jax.experimental.pallas import tpu as pltpu

F32 = jnp.float32
BF16 = jnp.bfloat16
EPS = 1e-6
LANES = 128
ROW_TILE = 512
COL_TILE = 1408
PACK_ROWS = 1024
PART_ROWS = 16
VMEM_LIMIT = 56 * 1024 * 1024

W_NAMES = ['ffn1_norm', 'ffn1_w_gate', 'ffn1_w_up', 'ffn1_w_down', 'mix_norm', 'ffn2_norm', 'ffn2_w_gate',
           'ffn2_w_up', 'ffn2_w_down', 'mem_norm', 'w_mem_kv', 'a_w_in', 'a_v_norm', 'a_w_spatial',
           'a_b_spatial', 'a_w_out', 'kv_norm', 'w_kv', 'b_w_in', 'b_w_out', 'final_norm']
SHARD_AXIS = {'ffn1_w_gate': 2, 'ffn1_w_up': 2, 'ffn1_w_down': 1, 'ffn2_w_gate': 2, 'ffn2_w_up': 2,
              'ffn2_w_down': 1, 'w_mem_kv': 1, 'a_w_in': 2, 'a_v_norm': 1, 'a_w_out': 1, 'w_kv': 1,
              'b_w_in': 1, 'b_w_out': 1}
N_CHIPS = 4
PACK_W = 1024

MEM_HEADS = 4
MEM_W = 256
HEAD_DIM = 64
GM_W = 768
GM_GROUPS = 6
GM_CHUNK = 128
CHUNK = 64
SB_BLOCK = 128
SB_Q_CHUNK = 1024

ADAM_LR, ADAM_B1, ADAM_B2, ADAM_EPS, ADAM_WD, ADAM_STEP = 0.001, 0.9, 0.999, 1e-08, 0.01, 10

MESH = pl.DeviceIdType.MESH


def _cparams(sem=None):
    return pltpu.CompilerParams(dimension_semantics=sem, vmem_limit_bytes=VMEM_LIMIT)


def _tile(n, target, mult=LANES):
    best = None
    for t in range(mult, min(n, target) + 1, mult):
        if n % t == 0:
            best = t
    return best if best is not None else n


def _dot(a, b):
    return jnp.dot(a, b, preferred_element_type=F32)


def _dot_nt(a, b):
    return lax.dot_general(a, b, (((1,), (1,)), ((), ())), preferred_element_type=F32)


def _dot_tn(a, b):
    return lax.dot_general(a, b, (((0,), (0,)), ((), ())), preferred_element_type=F32)


def _bf(v):
    return v.astype(BF16)


def rms_fwd(x, gain, name):
    S, D = x.shape
    tm = _tile(S, ROW_TILE, 8)

    def body(x_ref, g_ref, o_ref):
        xf = x_ref[...]
        r = lax.rsqrt(jnp.mean(xf * xf, axis=-1, keepdims=True) + EPS)
        o_ref[...] = ((xf * r) * g_ref[...]).astype(o_ref.dtype)

    return pl.pallas_call(
        body, grid=(S // tm,),
        in_specs=[pl.BlockSpec((tm, D), lambda i: (i, 0)), pl.BlockSpec((1, D), lambda i: (0, 0))],
        out_specs=pl.BlockSpec((tm, D), lambda i: (i, 0)),
        out_shape=jax.ShapeDtypeStruct((S, D), BF16),
        compiler_params=_cparams(("parallel",)), name=name)(x, gain.reshape(1, D))


def rms_bwd(x, gain, dh, dres, name):
    S, D = x.shape
    tm = _tile(S, ROW_TILE, 8)
    has_res = dres is not None

    def body(*refs):
        if has_res:
            x_ref, g_ref, dh_ref, dres_ref, dx_ref, dg_ref = refs
        else:
            x_ref, g_ref, dh_ref, dx_ref, dg_ref = refs
        xf = x_ref[...]
        r = lax.rsqrt(jnp.mean(xf * xf, axis=-1, keepdims=True) + EPS)
        xhat = xf * r
        dhv = dh_ref[...].astype(F32)
        dxhat = dhv * g_ref[...]
        m = jnp.mean(dxhat * xhat, axis=-1, keepdims=True)
        dx = r * (dxhat - xhat * m)
        if has_res:
            dx = dx + dres_ref[...]
        dx_ref[...] = dx

        @pl.when(pl.program_id(0) == 0)
        def _():
            dg_ref[...] = jnp.zeros_like(dg_ref)

        dg_ref[...] += jnp.sum(dhv * xhat, axis=0, keepdims=True)

    row = pl.BlockSpec((tm, D), lambda i: (i, 0))
    vec = pl.BlockSpec((1, D), lambda i: (0, 0))
    ins = [x, gain.reshape(1, D), dh] + ([dres] if has_res else [])
    in_specs = [row, vec, row] + ([row] if has_res else [])
    dx, dg = pl.pallas_call(
        body, grid=(S // tm,), in_specs=in_specs, out_specs=[row, vec],
        out_shape=[jax.ShapeDtypeStruct((S, D), F32), jax.ShapeDtypeStruct((1, D), F32)],
        compiler_params=_cparams(("arbitrary",)), name=name)(*ins)
    return dx, dg


def final_loss(x, gain, target, name):
    S, D = x.shape
    tm = _tile(S, ROW_TILE, 8)

    def body(x_ref, g_ref, t_ref, loss_ref, dx_ref, dg_ref):
        xf = x_ref[...]
        r = lax.rsqrt(jnp.mean(xf * xf, axis=-1, keepdims=True) + EPS)
        xhat = xf * r
        g = g_ref[...]
        diff = xhat * g - t_ref[...]
        dy = diff * (1.0 / D)
        dxhat = dy * g
        m = jnp.mean(dxhat * xhat, axis=-1, keepdims=True)
        dx_ref[...] = r * (dxhat - xhat * m)

        @pl.when(pl.program_id(0) == 0)
        def _():
            dg_ref[...] = jnp.zeros_like(dg_ref)
            loss_ref[...] = jnp.zeros_like(loss_ref)

        dg_ref[...] += jnp.sum(dy * xhat, axis=0, keepdims=True)
        per_tok = jnp.sum(diff * diff, axis=-1, keepdims=True) * (0.5 / D)
        loss_ref[...] += jnp.sum(per_tok, axis=0, keepdims=True)

    row = pl.BlockSpec((tm, D), lambda i: (i, 0))
    vec = pl.BlockSpec((1, D), lambda i: (0, 0))
    one = pl.BlockSpec((1, 1), lambda i: (0, 0))
    return pl.pallas_call(
        body, grid=(S // tm,), in_specs=[row, vec, row], out_specs=[one, row, vec],
        out_shape=[jax.ShapeDtypeStruct((1, 1), F32), jax.ShapeDtypeStruct((S, D), F32),
                   jax.ShapeDtypeStruct((1, D), F32)],
        compiler_params=_cparams(("arbitrary",)), name=name)(x, gain.reshape(1, D), target)


def mm(pairs, *, tb=False, res=None, scale=1.0, out_dtype=F32, name):
    M = pairs[0][0].shape[0]
    N = pairs[0][1].shape[0] if tb else pairs[0][1].shape[1]
    tm = _tile(M, ROW_TILE, 8)
    tn = _tile(N, COL_TILE)
    n_p = len(pairs)
    has_res = res is not None

    def body(*refs):
        o_ref = refs[-1]
        acc = None
        for p in range(n_p):
            a = _bf(refs[2 * p][...])
            b = _bf(refs[2 * p + 1][...])
            d = _dot_nt(a, b) if tb else _dot(a, b)
            acc = d if acc is None else acc + d
        if scale != 1.0:
            acc = acc * scale
        if has_res:
            acc = acc + refs[2 * n_p][...]
        o_ref[...] = acc.astype(o_ref.dtype)

    ins, in_specs = [], []
    for a, b, cb in pairs:
        K = b.shape[1] if tb else b.shape[0]
        ins += [a, b]
        in_specs.append(pl.BlockSpec((tm, K), functools.partial(lambda i, j, cb: (i, cb), cb=cb)))
        if tb:
            in_specs.append(pl.BlockSpec((tn, K), lambda i, j: (j, 0)))
        else:
            in_specs.append(pl.BlockSpec((K, tn), lambda i, j: (0, j)))
    if has_res:
        ins.append(res)
        in_specs.append(pl.BlockSpec((tm, tn), lambda i, j: (i, j)))
    return pl.pallas_call(
        body, grid=(M // tm, N // tn), in_specs=in_specs,
        out_specs=pl.BlockSpec((tm, tn), lambda i, j: (i, j)),
        out_shape=jax.ShapeDtypeStruct((M, N), out_dtype),
        compiler_params=_cparams(("parallel", "parallel")), name=name)(*ins)


def mm_tn(a, b, *, a_cb=0, a_w=None, b_cb=0, b_w=None, scale=1.0, name):
    S = a.shape[0]
    Ka = a.shape[1] if a_w is None else a_w
    Nb = b.shape[1] if b_w is None else b_w
    ts = _tile(S, ROW_TILE, 8)
    tk = _tile(Ka, 1024)
    tn = _tile(Nb, COL_TILE)
    a_off, b_off = a_cb * (Ka // tk), b_cb * (Nb // tn)

    def body(a_ref, b_ref, o_ref):
        s = pl.program_id(2)

        @pl.when(s == 0)
        def _():
            o_ref[...] = jnp.zeros_like(o_ref)

        o_ref[...] += _dot_tn(_bf(a_ref[...]), _bf(b_ref[...]))
        if scale != 1.0:
            @pl.when(s == pl.num_programs(2) - 1)
            def _():
                o_ref[...] = o_ref[...] * scale

    return pl.pallas_call(
        body, grid=(Ka // tk, Nb // tn, S // ts),
        in_specs=[pl.BlockSpec((ts, tk), lambda k, n, s: (s, a_off + k)),
                  pl.BlockSpec((ts, tn), lambda k, n, s: (s, b_off + n))],
        out_specs=pl.BlockSpec((tk, tn), lambda k, n, s: (k, n)),
        out_shape=jax.ShapeDtypeStruct((Ka, Nb), F32),
        compiler_params=_cparams(("parallel", "parallel", "arbitrary")), name=name)(a, b)


def ffn_up(h, wg, wu, name):
    S, D = h.shape
    Fd = wg.shape[1]
    tm = _tile(S, ROW_TILE, 8)
    tn = _tile(Fd, COL_TILE)

    def body(h_ref, wg_ref, wu_ref, g_ref, u_ref, a_ref):
        hv = h_ref[...]
        g = _dot(hv, wg_ref[...])
        u = _dot(hv, wu_ref[...])
        g_ref[...] = _bf(g)
        u_ref[...] = _bf(u)
        a_ref[...] = _bf(g * jax.nn.sigmoid(g) * u)

    blk = pl.BlockSpec((tm, tn), lambda i, j: (i, j))
    wspec = pl.BlockSpec((D, tn), lambda i, j: (0, j))
    sh = jax.ShapeDtypeStruct((S, Fd), BF16)
    return pl.pallas_call(
        body, grid=(S // tm, Fd // tn),
        in_specs=[pl.BlockSpec((tm, D), lambda i, j: (i, 0)), wspec, wspec],
        out_specs=[blk, blk, blk], out_shape=[sh, sh, sh],
        compiler_params=_cparams(("parallel", "parallel")), name=name)(h, wg, wu)


def ffn_down_bwd(dxo, wd, g, u, name):
    S, D = dxo.shape
    Fd = wd.shape[0]
    tm = _tile(S, ROW_TILE, 8)
    tn = _tile(Fd, COL_TILE)

    def body(dx_ref, wd_ref, g_ref, u_ref, dg_ref, du_ref):
        da = _dot_nt(_bf(dx_ref[...]), wd_ref[...]) * 0.5
        g = g_ref[...].astype(F32)
        u = u_ref[...].astype(F32)
        sg = jax.nn.sigmoid(g)
        dg_ref[...] = _bf(da * u * (sg * (1.0 + g * (1.0 - sg))))
        du_ref[...] = _bf(da * (g * sg))

    blk = pl.BlockSpec((tm, tn), lambda i, j: (i, j))
    sh = jax.ShapeDtypeStruct((S, Fd), BF16)
    return pl.pallas_call(
        body, grid=(S // tm, Fd // tn),
        in_specs=[pl.BlockSpec((tm, D), lambda i, j: (i, 0)), pl.BlockSpec((tn, D), lambda i, j: (j, 0)),
                  blk, blk],
        out_specs=[blk, blk], out_shape=[sh, sh],
        compiler_params=_cparams(("parallel", "parallel")), name=name)(dxo, wd, g, u)


_GELU_C = math.sqrt(2.0 / math.pi)


def _gelu(p):
    return 0.5 * p * (1.0 + jnp.tanh(_GELU_C * (p + 0.044715 * (p * p * p))))


def _gelu_grad(p):
    th = jnp.tanh(_GELU_C * (p + 0.044715 * (p * p * p)))
    return 0.5 * (1.0 + th) + 0.5 * p * (1.0 - th * th) * (_GELU_C * (1.0 + 3.0 * 0.044715 * (p * p)))


def _chunk_mask():
    t = lax.broadcasted_iota(jnp.int32, (GM_CHUNK, GM_CHUNK), 0)
    s = lax.broadcasted_iota(jnp.int32, (GM_CHUNK, GM_CHUNK), 1)
    return (s // CHUNK) <= (t // CHUNK)


def gmlp_fwd(proj, v_gain, w_s, b_mat, name):
    S = proj.shape[0]
    tw = _tile(S, ROW_TILE, GM_CHUNK)
    n_win = tw // GM_CHUNK

    def body(p_ref, gain_ref, w_ref, b_ref, y_ref):
        mask = _chunk_mask()
        v = _gelu(p_ref[:, GM_W:])
        r = lax.rsqrt(jnp.mean(v * v, axis=-1, keepdims=True) + EPS)
        vn = _bf(v * r * gain_ref[...])
        for g in range(GM_GROUPS):
            wm = _bf(jnp.where(mask, w_ref[g], 0.0))
            cs = slice(g * GM_CHUNK, (g + 1) * GM_CHUNK)
            for w in range(n_win):
                rs = slice(w * GM_CHUNK, (w + 1) * GM_CHUNK)
                mixed = _dot(wm, vn[rs, cs]) + b_ref[g]
                y_ref[rs, cs] = _bf(_gelu(p_ref[rs, cs]) * mixed)

    return pl.pallas_call(
        body, grid=(S // tw,),
        in_specs=[pl.BlockSpec((tw, 2 * GM_W), lambda i: (i, 0)), pl.BlockSpec((1, GM_W), lambda i: (0, 0)),
                  pl.BlockSpec((GM_GROUPS, GM_CHUNK, GM_CHUNK), lambda i: (0, 0, 0)),
                  pl.BlockSpec((GM_GROUPS, GM_CHUNK, GM_CHUNK), lambda i: (0, 0, 0))],
        out_specs=pl.BlockSpec((tw, GM_W), lambda i: (i, 0)),
        out_shape=jax.ShapeDtypeStruct((S, GM_W), BF16),
        compiler_params=_cparams(("parallel",)), name=name)(proj, v_gain.reshape(1, GM_W), w_s, b_mat)


def gmlp_bwd(proj, dy, v_gain, w_s, b_mat, name):
    S = proj.shape[0]
    tw = _tile(S, ROW_TILE, GM_CHUNK)
    n_win = tw // GM_CHUNK

    def body(p_ref, dy_ref, gain_ref, w_ref, b_ref, dp_ref, dw_ref, db_ref, dgain_ref, dvn_ref):
        step = pl.program_id(0)

        @pl.when(step == 0)
        def _():
            dw_ref[...] = jnp.zeros_like(dw_ref)
            db_ref[...] = jnp.zeros_like(db_ref)
            dgain_ref[...] = jnp.zeros_like(dgain_ref)

        mask = _chunk_mask()
        pv = p_ref[:, GM_W:]
        v = _gelu(pv)
        r = lax.rsqrt(jnp.mean(v * v, axis=-1, keepdims=True) + EPS)
        vhat = v * r
        gain = gain_ref[...]
        vn = _bf(vhat * gain)
        for g in range(GM_GROUPS):
            wm = _bf(jnp.where(mask, w_ref[g], 0.0))
            cs = slice(g * GM_CHUNK, (g + 1) * GM_CHUNK)
            dw_acc = jnp.zeros((GM_CHUNK, GM_CHUNK), F32)
            db_acc = jnp.zeros((GM_CHUNK, GM_CHUNK), F32)
            for w in range(n_win):
                rs = slice(w * GM_CHUNK, (w + 1) * GM_CHUNK)
                pu = p_ref[rs, cs]
                u = _gelu(pu)
                vn_blk = vn[rs, cs]
                mixed = _dot(wm, vn_blk) + b_ref[g]
                dyb = dy_ref[rs, cs]
                dp_ref[rs, cs] = _bf(dyb * mixed * _gelu_grad(pu))
                dmix = dyb * u
                db_acc = db_acc + dmix
                dmix_b = _bf(dmix)
                dw_acc = dw_acc + _dot_nt(dmix_b, vn_blk)
                dvn_ref[rs, cs] = _dot_tn(wm, dmix_b)
            dw_ref[g] += jnp.where(mask, dw_acc, 0.0)
            db_ref[g] += jnp.broadcast_to(jnp.sum(db_acc, axis=-1, keepdims=True), (GM_CHUNK, GM_CHUNK))
        dvn = dvn_ref[...]
        dgain_ref[...] += jnp.sum(dvn * vhat, axis=0, keepdims=True)
        dvhat = dvn * gain
        m = jnp.mean(dvhat * vhat, axis=-1, keepdims=True)
        dv = r * (dvhat - vhat * m)
        dp_ref[:, GM_W:] = _bf(dv * _gelu_grad(pv))

    sq = pl.BlockSpec((GM_GROUPS, GM_CHUNK, GM_CHUNK), lambda i: (0, 0, 0))
    sq_sh = jax.ShapeDtypeStruct((GM_GROUPS, GM_CHUNK, GM_CHUNK), F32)
    return pl.pallas_call(
        body, grid=(S // tw,),
        in_specs=[pl.BlockSpec((tw, 2 * GM_W), lambda i: (i, 0)), pl.BlockSpec((tw, GM_W), lambda i: (i, 0)),
                  pl.BlockSpec((1, GM_W), lambda i: (0, 0)), sq, sq],
        out_specs=[pl.BlockSpec((tw, 2 * GM_W), lambda i: (i, 0)), sq, sq,
                   pl.BlockSpec((1, GM_W), lambda i: (0, 0))],
        out_shape=[jax.ShapeDtypeStruct((S, 2 * GM_W), BF16), sq_sh, sq_sh,
                   jax.ShapeDtypeStruct((1, GM_W), F32)],
        scratch_shapes=[pltpu.VMEM((tw, GM_W), F32)],
        compiler_params=_cparams(("arbitrary",)), name=name)(proj, dy, v_gain.reshape(1, GM_W), w_s, b_mat)


def _head_mask(h, width):
    lane = lax.broadcasted_iota(jnp.int32, (1, width), 1)
    return (lane >= HEAD_DIM * h) & (lane < HEAD_DIM * (h + 1))


def _mem_probs(q, k, h):
    kh = jnp.where(_head_mask(h, MEM_W), k, jnp.zeros_like(k))
    s = _dot_nt(q, kh) * (HEAD_DIM ** -0.5)
    s = s - jnp.max(s, axis=-1, keepdims=True)
    e = jnp.exp(s)
    return e / jnp.sum(e, axis=-1, keepdims=True), kh


def memattn_fwd(proj, q_cb, mem_kv, name):
    S = proj.shape[0]
    NM = mem_kv.shape[0]
    tm = _tile(S, ROW_TILE, 8)

    def body(q_ref, kv_ref, o_ref):
        q = _bf(q_ref[...])
        k = _bf(kv_ref[:, :MEM_W])
        v = _bf(kv_ref[:, MEM_W:])
        acc = jnp.zeros((tm, MEM_W), F32)
        for h in range(MEM_HEADS):
            p, _ = _mem_probs(q, k, h)
            vh = jnp.where(_head_mask(h, MEM_W), v, jnp.zeros_like(v))
            acc = acc + _dot(_bf(p), vh)
        o_ref[...] = _bf(acc)

    return pl.pallas_call(
        body, grid=(S // tm,),
        in_specs=[pl.BlockSpec((tm, MEM_W), lambda i: (i, q_cb)), pl.BlockSpec((NM, 2 * MEM_W), lambda i: (0, 0))],
        out_specs=pl.BlockSpec((tm, MEM_W), lambda i: (i, 0)),
        out_shape=jax.ShapeDtypeStruct((S, MEM_W), BF16),
        compiler_params=_cparams(("parallel",)), name=name)(proj, mem_kv)


def memattn_bwd(proj, q_cb, mem_kv, dy, dy_cb, name):
    S = proj.shape[0]
    NM = mem_kv.shape[0]
    tm = _tile(S, ROW_TILE, 8)

    def body(q_ref, kv_ref, do_ref, dq_ref, dkv_ref):
        @pl.when(pl.program_id(0) == 0)
        def _():
            dkv_ref[...] = jnp.zeros_like(dkv_ref)

        q = _bf(q_ref[...])
        k = _bf(kv_ref[:, :MEM_W])
        v = _bf(kv_ref[:, MEM_W:])
        do = _bf(do_ref[...])
        dq = jnp.zeros((tm, MEM_W), F32)
        dk = jnp.zeros((NM, MEM_W), F32)
        dv = jnp.zeros((NM, MEM_W), F32)
        for h in range(MEM_HEADS):
            hm = _head_mask(h, MEM_W)
            p, kh = _mem_probs(q, k, h)
            vh = jnp.where(hm, v, jnp.zeros_like(v))
            dp = _dot_nt(do, vh)
            ds = _bf(p * (dp - jnp.sum(p * dp, axis=-1, keepdims=True)) * (HEAD_DIM ** -0.5))
            dq = dq + _dot(ds, kh)
            dk = dk + jnp.where(hm, _dot_tn(ds, q), 0.0)
            dv = dv + jnp.where(hm, _dot_tn(_bf(p), do), 0.0)
        dq_ref[...] = _bf(dq)
        dkv_ref[:, :MEM_W] += dk
        dkv_ref[:, MEM_W:] += dv

    return pl.pallas_call(
        body, grid=(S // tm,),
        in_specs=[pl.BlockSpec((tm, MEM_W), lambda i: (i, q_cb)), pl.BlockSpec((NM, 2 * MEM_W), lambda i: (0, 0)),
                  pl.BlockSpec((tm, MEM_W), lambda i: (i, dy_cb))],
        out_specs=[pl.BlockSpec((tm, MEM_W), lambda i: (i, 0)), pl.BlockSpec((NM, 2 * MEM_W), lambda i: (0, 0))],
        out_shape=[jax.ShapeDtypeStruct((S, MEM_W), BF16), jax.ShapeDtypeStruct((NM, 2 * MEM_W), F32)],
        compiler_params=_cparams(("arbitrary",)), name=name)(proj, mem_kv, dy)


def _split_bf(v):
    hi = _bf(v)
    return hi, _bf(v - hi.astype(F32))


def _sb_weights(qh, k2, run, dpos, row, col, tri):
    z = _dot_nt(qh, k2) * (HEAD_DIM ** -0.5)
    t = jnp.log(1.0 + jnp.exp(-jnp.abs(z)))
    lb = jnp.minimum(z, 0.0) - t
    mask = (col - row) < dpos
    lom = jnp.where(mask, -jnp.maximum(z, 0.0) - t, 0.0)
    l_hi, l_lo = _split_bf(lom)
    insuf = _dot(l_hi, tri) + _dot(l_lo, tri)
    a = jnp.where(mask, jnp.exp(lb + run + insuf), 0.0)
    return a, lb, lom, mask


def _sb_consts():
    row = lax.broadcasted_iota(jnp.int32, (SB_BLOCK, SB_BLOCK), 0)
    col = lax.broadcasted_iota(jnp.int32, (SB_BLOCK, SB_BLOCK), 1)
    return row, col


def sb_fwd(proj, kv, name):
    S = proj.shape[0]
    n_hp = GM_W // LANES
    tq = _tile(S, SB_Q_CHUNK, SB_BLOCK)
    nqb = tq // SB_BLOCK

    def body(q_ref, k_ref, v_ref, o_ref):
        chunk = pl.program_id(1)
        row, col = _sb_consts()
        tri = _bf(jnp.where(row > col, 1.0, 0.0))
        for h in range(2):
            hm = _head_mask(h, LANES)

            def q_loop(qi, _):
                i = chunk * nqb + qi
                rows = pl.ds(pl.multiple_of(qi * SB_BLOCK, SB_BLOCK), SB_BLOCK)
                qh = _bf(jnp.where(hm, q_ref[rows, :], 0.0))

                def k_loop(t, carry):
                    acc, run = carry
                    j = i - t
                    keys = pl.ds(pl.multiple_of(j * SB_BLOCK, SB_BLOCK), SB_BLOCK)
                    a, _, lom, _ = _sb_weights(qh, k_ref[keys, :], run, t * SB_BLOCK, row, col, tri)
                    v2 = v_ref[keys, :]
                    vh = jnp.where(hm, v2, jnp.zeros_like(v2))
                    a_hi, a_lo = _split_bf(a)
                    acc = acc + _dot(a_hi, vh) + _dot(a_lo, vh)
                    return acc, run + jnp.sum(lom, axis=1, keepdims=True)

                acc, _ = lax.fori_loop(0, i + 1, k_loop, (jnp.zeros((SB_BLOCK, LANES), F32),
                                                           jnp.zeros((SB_BLOCK, 1), F32)))
                if h == 0:
                    o_ref[rows, :] = acc
                else:
                    o_ref[rows, :] += acc
                return 0

            lax.fori_loop(0, nqb, q_loop, 0)

    return pl.pallas_call(
        body, grid=(n_hp, S // tq),
        in_specs=[pl.BlockSpec((tq, LANES), lambda hp, c: (c, hp)),
                  pl.BlockSpec((S, LANES), lambda hp, c: (0, hp)),
                  pl.BlockSpec((S, LANES), lambda hp, c: (0, n_hp + hp))],
        out_specs=pl.BlockSpec((tq, LANES), lambda hp, c: (c, hp)),
        out_shape=jax.ShapeDtypeStruct((S, GM_W), F32),
        compiler_params=_cparams(("parallel", "parallel")), name=name)(proj, kv, kv)


def sb_bwd(proj, kv, dy, out, dk_init, dv_init, name):
    S = proj.shape[0]
    n_hp = GM_W // LANES
    tq = _tile(S, SB_Q_CHUNK, SB_BLOCK)
    nqb = tq // SB_BLOCK
    has_init = dk_init is not None
    scale = HEAD_DIM ** -0.5

    def body(*refs):
        if has_init:
            q_ref, k_ref, v_ref, do_ref, out_ref, dki_ref, dvi_ref, dq_ref, dk_ref, dv_ref = refs
        else:
            q_ref, k_ref, v_ref, do_ref, out_ref, dq_ref, dk_ref, dv_ref = refs
        hp = pl.program_id(0)
        chunk = pl.program_id(1)

        @pl.when(chunk == 0)
        def _():
            if has_init:
                cols = pl.ds(pl.multiple_of(hp * LANES, LANES), LANES)
                pltpu.sync_copy(dki_ref.at[:, cols], dk_ref)
                pltpu.sync_copy(dvi_ref.at[:, cols], dv_ref)
            else:
                dk_ref[...] = jnp.zeros_like(dk_ref)
                dv_ref[...] = jnp.zeros_like(dv_ref)

        row, col = _sb_consts()
        tri = _bf(jnp.where(row > col, 1.0, 0.0))
        tri_inc = _bf(jnp.where(row >= col, 1.0, 0.0))
        for h in range(2):
            hm = _head_mask(h, LANES)

            def q_loop(qi, _):
                i = chunk * nqb + qi
                rows = pl.ds(pl.multiple_of(qi * SB_BLOCK, SB_BLOCK), SB_BLOCK)
                qh = _bf(jnp.where(hm, q_ref[rows, :], 0.0))
                doh = _bf(jnp.where(hm, do_ref[rows, :], 0.0))
                e_tot = jnp.sum(doh.astype(F32) * out_ref[rows, :], axis=1, keepdims=True)

                def k_loop(t, carry):
                    dq, run, e_run = carry
                    j = i - t
                    keys = pl.ds(pl.multiple_of(j * SB_BLOCK, SB_BLOCK), SB_BLOCK)
                    k2 = k_ref[keys, :]
                    a, lb, lom, mask = _sb_weights(qh, k2, run, t * SB_BLOCK, row, col, tri)
                    v2 = v_ref[keys, :]
                    vh = jnp.where(hm, v2, jnp.zeros_like(v2))
                    e = a * _dot_nt(doh, vh)
                    e_hi, e_lo = _split_bf(e)
                    before = e_tot - e_run - (_dot(e_hi, tri_inc) + _dot(e_lo, tri_inc))
                    beta = jnp.exp(lb)
                    dz = _bf(jnp.where(mask, e * (1.0 - beta) - before * beta, 0.0) * scale)
                    kh = jnp.where(hm, k2, jnp.zeros_like(k2))
                    dq = dq + _dot(dz, kh)
                    dk_ref[keys, :] += _dot_tn(dz, qh)
                    dv_ref[keys, :] += _dot_tn(_bf(a), doh)
                    return (dq, run + jnp.sum(lom, axis=1, keepdims=True),
                            e_run + jnp.sum(e, axis=1, keepdims=True))

                zero = jnp.zeros((SB_BLOCK, 1), F32)
                dq, _, _ = lax.fori_loop(0, i + 1, k_loop, (jnp.zeros((SB_BLOCK, LANES), F32), zero, zero))
                if h == 0:
                    dq_ref[rows, :] = dq
                else:
                    dq_ref[rows, :] += dq
                return 0

            lax.fori_loop(0, nqb, q_loop, 0)

    qspec = pl.BlockSpec((tq, LANES), lambda hp, c: (c, hp))
    kspec = pl.BlockSpec((S, LANES), lambda hp, c: (0, hp))
    in_specs = [qspec, kspec, pl.BlockSpec((S, LANES), lambda hp, c: (0, n_hp + hp)), qspec, qspec]
    ins = [proj, kv, kv, dy, out]
    if has_init:
        in_specs += [pl.BlockSpec(memory_space=pl.ANY), pl.BlockSpec(memory_space=pl.ANY)]
        ins += [dk_init, dv_init]
    sh = jax.ShapeDtypeStruct((S, GM_W), F32)
    return pl.pallas_call(
        body, grid=(n_hp, S // tq), in_specs=in_specs, out_specs=[qspec, kspec, kspec],
        out_shape=[sh, sh, sh],
        compiler_params=_cparams(("parallel", "arbitrary")), name=name)(*ins)


def adamw(w, g, m, v, name):
    shape = w.shape
    C = shape[-1] if w.ndim > 1 else shape[0]
    R = w.size // C
    tr = _tile(R, ROW_TILE, 8)

    def body(w_ref, g_ref, m_ref, v_ref, d_ref, nm_ref, nv_ref):
        gv = g_ref[...]
        m2 = ADAM_B1 * m_ref[...] + (1.0 - ADAM_B1) * gv
        v2 = ADAM_B2 * v_ref[...] + (1.0 - ADAM_B2) * (gv * gv)
        m_hat = m2 / (1.0 - ADAM_B1 ** ADAM_STEP)
        v_hat = v2 / (1.0 - ADAM_B2 ** ADAM_STEP)
        d_ref[...] = -ADAM_LR * (m_hat / (jnp.sqrt(v_hat) + ADAM_EPS) + ADAM_WD * w_ref[...])
        nm_ref[...] = m2
        nv_ref[...] = v2

    blk = pl.BlockSpec((tr, C), lambda i: (i, 0))
    sh = jax.ShapeDtypeStruct((R, C), F32)
    outs = pl.pallas_call(
        body, grid=(R // tr,), in_specs=[blk] * 4, out_specs=[blk] * 3, out_shape=[sh] * 3,
        compiler_params=_cparams(("parallel",)), name=name)(
            w.reshape(R, C), g.reshape(R, C), m.reshape(R, C), v.reshape(R, C))
    return tuple(o.reshape(shape) for o in outs)


def add_n(parts, name):
    parts = [p if isinstance(p, tuple) else (p[None], 0) for p in parts]
    _, R, C = parts[0][0].shape
    tr = _tile(R, ROW_TILE, 8)
    n = len(parts)

    def body(*refs):
        acc = refs[0][...]
        for p in range(1, n):
            acc = acc + refs[p][...]
        refs[n][...] = acc

    in_specs = [pl.BlockSpec((None, tr, C), functools.partial(lambda i, k: (k, i, 0), k=k)) for _, k in parts]
    return pl.pallas_call(
        body, grid=(R // tr,), in_specs=in_specs, out_specs=pl.BlockSpec((tr, C), lambda i: (i, 0)),
        out_shape=jax.ShapeDtypeStruct((R, C), F32),
        compiler_params=_cparams(("parallel",)), name=name)(*[a for a, _ in parts])


def _place():
    return lax.axis_index("x"), lax.axis_index("y"), lax.axis_index("c")


def _other_chips(x, y):
    return [(1 - x, y), (x, 1 - y), (1 - x, 1 - y)]


_ANY = pl.BlockSpec(memory_space=pl.ANY)


def all_gather_chips(buf, name):
    P, Wd = buf.shape
    Ph = P // 2

    def body(in_ref, out_ref, send_sems, recv_sems, local_sem):
        x, y, c = _place()
        sibling = (x, y, 1 - c)
        chips = _other_chips(x, y)

        def half(px, py, pc):
            return out_ref.at[2 * px + py, pl.ds(pc * Ph, Ph), :]

        def copy(k, src, dst, to):
            return pltpu.make_async_remote_copy(src_ref=src, dst_ref=dst, send_sem=send_sems.at[k],
                                                recv_sem=recv_sems.at[k], device_id=to, device_id_type=MESH)

        mine = pltpu.make_async_copy(in_ref, out_ref.at[2 * x + y], local_sem)
        mine.start()
        my_half = in_ref.at[pl.ds(c * Ph, Ph), :]
        first = [copy(j, my_half, half(x, y, c), (*chip, c)) for j, chip in enumerate(chips)]
        for cp in first:
            cp.start()
        passed = [copy(3 + j, half(*chip, c), half(*chip, c), sibling) for j, chip in enumerate(chips)]
        for j, chip in enumerate(chips):
            copy(j, my_half, half(*chip, c), (*chip, c)).wait_recv()
            passed[j].start()
        for j, chip in enumerate(chips):
            copy(3 + j, my_half, half(*chip, 1 - c), sibling).wait_recv()
        for cp in first + passed:
            cp.wait_send()
        mine.wait()

    return pl.pallas_call(
        body, in_specs=[_ANY], out_specs=_ANY,
        out_shape=jax.ShapeDtypeStruct((N_CHIPS, P, Wd), buf.dtype),
        scratch_shapes=[pltpu.SemaphoreType.DMA((6,)), pltpu.SemaphoreType.DMA((6,)), pltpu.SemaphoreType.DMA],
        name=name)(buf)


def swap_halves(g, name):
    n, P, Wd = g.shape
    Ph = P // 2

    def body(g_ref, mine_ref, theirs_ref, send_sem, recv_sem, local_sem):
        x, y, c = _place()
        keep = pltpu.make_async_copy(g_ref.at[:, pl.ds(c * Ph, Ph), :], mine_ref, local_sem)
        keep.start()
        give = pltpu.make_async_remote_copy(
            src_ref=g_ref.at[:, pl.ds((1 - c) * Ph, Ph), :], dst_ref=theirs_ref, send_sem=send_sem,
            recv_sem=recv_sem, device_id=(x, y, 1 - c), device_id_type=MESH)
        give.start()
        give.wait()
        keep.wait()

    sh = jax.ShapeDtypeStruct((n, Ph, Wd), g.dtype)
    return pl.pallas_call(
        body, in_specs=[_ANY], out_specs=[_ANY, _ANY], out_shape=[sh, sh],
        scratch_shapes=[pltpu.SemaphoreType.DMA, pltpu.SemaphoreType.DMA, pltpu.SemaphoreType.DMA],
        name=name)(g)


def scatter_to_chips(h, name):
    n, Ph, Wd = h.shape

    def body(h_ref, out_ref, send_sems, recv_sems, local_sem):
        x, y, c = _place()
        me = 2 * x + y
        chips = _other_chips(x, y)
        keep = pltpu.make_async_copy(h_ref.at[me], out_ref.at[me], local_sem)
        keep.start()
        sends = [pltpu.make_async_remote_copy(
            src_ref=h_ref.at[2 * cx + cy], dst_ref=out_ref.at[me], send_sem=send_sems.at[j],
            recv_sem=recv_sems.at[j], device_id=(cx, cy, c), device_id_type=MESH)
            for j, (cx, cy) in enumerate(chips)]
        for cp in sends:
            cp.start()
        for j, (cx, cy) in enumerate(chips):
            pltpu.make_async_remote_copy(
                src_ref=h_ref.at[me], dst_ref=out_ref.at[2 * cx + cy], send_sem=send_sems.at[j],
                recv_sem=recv_sems.at[j], device_id=(cx, cy, c), device_id_type=MESH).wait_recv()
        for cp in sends:
            cp.wait_send()
        keep.wait()

    return pl.pallas_call(
        body, in_specs=[_ANY], out_specs=_ANY, out_shape=jax.ShapeDtypeStruct((n, Ph, Wd), h.dtype),
        scratch_shapes=[pltpu.SemaphoreType.DMA((3,)), pltpu.SemaphoreType.DMA((3,)), pltpu.SemaphoreType.DMA],
        name=name)(h)


def join_halves(t, name):
    Ph, Wd = t.shape

    def body(t_ref, out_ref, send_sem, recv_sem, local_sem):
        x, y, c = _place()
        keep = pltpu.make_async_copy(t_ref, out_ref.at[pl.ds(c * Ph, Ph), :], local_sem)
        keep.start()
        give = pltpu.make_async_remote_copy(
            src_ref=t_ref, dst_ref=out_ref.at[pl.ds(c * Ph, Ph), :], send_sem=send_sem, recv_sem=recv_sem,
            device_id=(x, y, 1 - c), device_id_type=MESH)
        give.start()
        give.wait()
        keep.wait()

    return pl.pallas_call(
        body, in_specs=[_ANY], out_specs=_ANY, out_shape=jax.ShapeDtypeStruct((2 * Ph, Wd), t.dtype),
        scratch_shapes=[pltpu.SemaphoreType.DMA, pltpu.SemaphoreType.DMA, pltpu.SemaphoreType.DMA],
        name=name)(t)


def _pad_rows(flat, mult_rows=1):
    n = flat.shape[0]
    rows = -(-n // PACK_W)
    rows = -(-rows // mult_rows) * mult_rows
    return jnp.pad(flat, (0, rows * PACK_W - n)).reshape(rows, PACK_W)


def _pack(parts, row_mult):
    blocks = [_pad_rows(p.reshape(-1), PART_ROWS) for p in parts]
    buf = jnp.concatenate(blocks, axis=0)
    rows = buf.shape[0]
    total = -(-rows // row_mult) * row_mult
    return jnp.pad(buf, ((0, total - rows), (0, 0)))


def _unpack(buf, shapes):
    outs, r = [], 0
    lead = buf.shape[:-2]
    for shp in shapes:
        n = math.prod(shp)
        rows = -(-n // PACK_W)
        blk = buf[..., r:r + rows, :].reshape(lead + (rows * PACK_W,))[..., :n]
        outs.append(blk.reshape(lead + tuple(shp)))
        r += -(-rows // PART_ROWS) * PART_ROWS
    return outs


def gather_weights(shards):
    names = [n for n in W_NAMES if n in SHARD_AXIS]
    parts = []
    for n in names:
        w = shards[n]
        if n == 'a_v_norm':
            parts.append(lax.bitcast_convert_type(w, BF16))
        else:
            parts.append(w.astype(BF16))
    buf = _pack(parts, PACK_ROWS)
    full = all_gather_chips(buf, "all_gather_weights")
    got = _unpack(full, [p.shape for p in parts])
    out = {}
    for n, g in zip(names, got):
        if n == 'a_v_norm':
            g = lax.bitcast_convert_type(g, F32)
        out[n] = jnp.concatenate([g[k] for k in range(N_CHIPS)], axis=SHARD_AXIS[n])
    return out


def reduce_grads(grads, shard_shapes):
    sharded = [n for n in W_NAMES if n in SHARD_AXIS]
    repl = [n for n in W_NAMES if n not in SHARD_AXIS]
    blocks = []
    for k in range(N_CHIPS):
        parts = [jnp.split(grads[n], N_CHIPS, axis=SHARD_AXIS[n])[k] for n in sharded]
        parts += [grads[n] for n in repl]
        blocks.append(_pack(parts, PACK_ROWS))
    g = jnp.stack(blocks, axis=0)
    mine, theirs = swap_halves(g, "grads_swap_halves")
    n, Ph, Wd = mine.shape
    chip_sum = add_n([mine.reshape(n * Ph, Wd), theirs.reshape(n * Ph, Wd)], "grads_add_cores").reshape(n, Ph, Wd)
    slots = scatter_to_chips(chip_sum, "grads_scatter")
    total_half = add_n([(slots, k) for k in range(N_CHIPS)], "grads_add_chips")
    total = join_halves(total_half, "grads_join_halves")
    shapes = [shard_shapes[n] for n in sharded] + [grads[n].shape for n in repl]
    return dict(zip(sharded + repl, _unpack(total, shapes)))


def _ffn_fwd(x, gain, wg, wu, wd, tag):
    h = rms_fwd(x, gain, tag + "_norm")
    g, u, a = ffn_up(h, wg, wu, tag + "_up")
    xo = mm([(a, wd, 0)], res=x, scale=0.5, name=tag + "_down")
    return xo, (x, h, g, u, a)


def _ffn_bwd(dxo, saved, gain, wg, wu, wd, tag):
    x, h, g, u, a = saved
    dgp, du = ffn_down_bwd(dxo, wd, g, u, tag + "_down_bwd")
    d_wd = mm_tn(a, dxo, scale=0.5, name=tag + "_dwd")
    d_wg = mm_tn(h, dgp, name=tag + "_dwg")
    d_wu = mm_tn(h, du, name=tag + "_dwu")
    dh = mm([(dgp, wg, 0), (du, wu, 0)], tb=True, name=tag + "_up_bwd")
    dx, d_gain = rms_bwd(x, gain, dh, dxo, tag + "_norm_bwd")
    return dx, d_gain, d_wg, d_wu, d_wd


def local_step(x, mem, W, target):
    depth = W['ffn1_norm'].shape[0]
    n_a = W['a_w_in'].shape[0]
    G = {}
    mem_h = rms_fwd(mem, W['mem_norm'], "mem_norm")
    b_mats = jnp.broadcast_to(W['a_b_spatial'][..., None], W['a_b_spatial'].shape + (GM_CHUNK,))
    saved = []
    kv = kvn = x_kv = None
    for l in range(depth):
        if l == n_a:
            x_kv = x
            kvn = rms_fwd(x, W['kv_norm'], "kv_norm")
            kv = mm([(kvn, W['w_kv'], 0)], out_dtype=BF16, name="kv_proj")
        x, s1 = _ffn_fwd(x, W['ffn1_norm'][l], W['ffn1_w_gate'][l], W['ffn1_w_up'][l], W['ffn1_w_down'][l],
                         f"l{l}_ffn1")
        hm = rms_fwd(x, W['mix_norm'][l], f"l{l}_mix_norm")
        mem_kv = mm([(mem_h, W['w_mem_kv'][l], 0)], name=f"l{l}_mem_kv")
        if l < n_a:
            w_in, w_out = W['a_w_in'][l], W['a_w_out'][l]
            proj = mm([(hm, w_in, 0)], name=f"l{l}_mix_in")
            y_tok = gmlp_fwd(proj, W['a_v_norm'][l], W['a_w_spatial'][l], b_mats[l], f"l{l}_gmlp")
            q_cb = 2 * GM_W // MEM_W
            extra = None
        else:
            w_in, w_out = W['b_w_in'][l - n_a], W['b_w_out'][l - n_a]
            proj = mm([(hm, w_in, 0)], name=f"l{l}_mix_in")
            y_tok = sb_fwd(proj, kv, f"l{l}_sb")
            q_cb = GM_W // MEM_W
            extra = None
        y_mem = memattn_fwd(proj, q_cb, mem_kv, f"l{l}_memattn")
        x_mid = x
        x = mm([(y_tok, w_out[:GM_W], 0), (y_mem, w_out[GM_W:], 0)], res=x, name=f"l{l}_mix_out")
        sm = (x_mid, hm, mem_kv, proj, y_tok, y_mem, q_cb, extra)
        x, s2 = _ffn_fwd(x, W['ffn2_norm'][l], W['ffn2_w_gate'][l], W['ffn2_w_up'][l], W['ffn2_w_down'][l],
                         f"l{l}_ffn2")
        saved.append((s1, sm, s2))

    loss, dx, d_final = final_loss(x, W['final_norm'], target, "loss_head")
    G['final_norm'] = d_final.reshape(-1)

    per_layer = {n: [None] * depth for n in ['ffn1_norm', 'ffn1_w_gate', 'ffn1_w_up', 'ffn1_w_down', 'mix_norm',
                                             'ffn2_norm', 'ffn2_w_gate', 'ffn2_w_up', 'ffn2_w_down', 'w_mem_kv']}
    per_a = {n: [None] * n_a for n in ['a_w_in', 'a_v_norm', 'a_w_spatial', 'a_b_spatial', 'a_w_out']}
    per_b = {n: [None] * (depth - n_a) for n in ['b_w_in', 'b_w_out']}
    d_mem_h = None
    dk = dv = None
    for l in reversed(range(depth)):
        s1, sm, s2 = saved[l]
        dx, dg, dwg, dwu, dwd = _ffn_bwd(dx, s2, W['ffn2_norm'][l], W['ffn2_w_gate'][l], W['ffn2_w_up'][l],
                                         W['ffn2_w_down'][l], f"l{l}_ffn2")
        per_layer['ffn2_norm'][l], per_layer['ffn2_w_gate'][l] = dg.reshape(-1), dwg
        per_layer['ffn2_w_up'][l], per_layer['ffn2_w_down'][l] = dwu, dwd

        x_mid, hm, mem_kv, proj, y_tok, y_mem, q_cb, _ = sm
        is_a = l < n_a
        w_in = W['a_w_in'][l] if is_a else W['b_w_in'][l - n_a]
        w_out = W['a_w_out'][l] if is_a else W['b_w_out'][l - n_a]
        dy = mm([(dx, w_out, 0)], tb=True, name=f"l{l}_mix_out_bwd")
        d_w_out = jnp.concatenate([mm_tn(y_tok, dx, name=f"l{l}_dwout_tok"),
                                   mm_tn(y_mem, dx, name=f"l{l}_dwout_mem")], axis=0)
        dq_mem, d_mem_kv = memattn_bwd(proj, q_cb, mem_kv, dy, GM_W // MEM_W, f"l{l}_memattn_bwd")
        if is_a:
            dp_uv, d_ws, d_bs, d_vg = gmlp_bwd(proj, dy, W['a_v_norm'][l], W['a_w_spatial'][l],
                                               jnp.broadcast_to(W['a_b_spatial'][l][..., None],
                                                                (GM_GROUPS, GM_CHUNK, GM_CHUNK)), f"l{l}_gmlp_bwd")
            per_a['a_w_spatial'][l], per_a['a_b_spatial'][l] = d_ws, d_bs[:, :, 0]
            per_a['a_v_norm'][l], per_a['a_w_out'][l] = d_vg.reshape(-1), d_w_out
            d_tok = dp_uv
            tok_w = 2 * GM_W
        else:
            d_tok, dk, dv = sb_bwd(proj, kv, dy, y_tok, dk, dv, f"l{l}_sb_bwd")
            per_b['b_w_out'][l - n_a] = d_w_out
            tok_w = GM_W
        d_w_in = jnp.concatenate([mm_tn(hm, d_tok, name=f"l{l}_dwin_tok"),
                                  mm_tn(hm, dq_mem, name=f"l{l}_dwin_mem")], axis=1)
        (per_a['a_w_in'] if is_a else per_b['b_w_in'])[l if is_a else l - n_a] = d_w_in
        dh = mm([(d_tok, w_in[:, :tok_w], 0), (dq_mem, w_in[:, tok_w:], 0)], tb=True, name=f"l{l}_mix_in_bwd")
        dx, dg = rms_bwd(x_mid, W['mix_norm'][l], dh, dx, f"l{l}_mix_norm_bwd")
        per_layer['mix_norm'][l] = dg.reshape(-1)
        per_layer['w_mem_kv'][l] = mm_tn(mem_h, d_mem_kv, name=f"l{l}_dw_mem_kv")
        res = mm([(d_mem_kv, W['w_mem_kv'][l], 0)], tb=True, res=d_mem_h, name=f"l{l}_mem_kv_bwd")
        d_mem_h = res

        dx, dg, dwg, dwu, dwd = _ffn_bwd(dx, s1, W['ffn1_norm'][l], W['ffn1_w_gate'][l], W['ffn1_w_up'][l],
                                         W['ffn1_w_down'][l], f"l{l}_ffn1")
        per_layer['ffn1_norm'][l], per_layer['ffn1_w_gate'][l] = dg.reshape(-1), dwg
        per_layer['ffn1_w_up'][l], per_layer['ffn1_w_down'][l] = dwu, dwd
        if l == n_a:
            w_kv = W['w_kv']
            G['w_kv'] = jnp.concatenate([mm_tn(kvn, dk, name="dw_k"), mm_tn(kvn, dv, name="dw_v")], axis=1)
            d_kvn = mm([(dk, w_kv[:, :GM_W], 0), (dv, w_kv[:, GM_W:], 0)], tb=True, name="kv_proj_bwd")
            dx, dg = rms_bwd(x_kv, W['kv_norm'], d_kvn, dx, "kv_norm_bwd")
            G['kv_norm'] = dg.reshape(-1)

    _, dg = rms_bwd(mem, W['mem_norm'], d_mem_h, None, "mem_norm_bwd")
    G['mem_norm'] = dg.reshape(-1)
    for d in (per_layer, per_a, per_b):
        for n, v in d.items():
            G[n] = jnp.stack(v, axis=0)
    return loss, dx, G


def kernel(x, mem, ffn1_norm, ffn1_w_gate, ffn1_w_up, ffn1_w_down, mix_norm, ffn2_norm, ffn2_w_gate, ffn2_w_up, ffn2_w_down, mem_norm, w_mem_kv, a_w_in, a_v_norm, a_w_spatial, a_b_spatial, a_w_out, kv_norm, w_kv, b_w_in, b_w_out, final_norm, loss_target, m_ffn1_norm, m_ffn1_w_gate, m_ffn1_w_up, m_ffn1_w_down, m_mix_norm, m_ffn2_norm, m_ffn2_w_gate, m_ffn2_w_up, m_ffn2_w_down, m_mem_norm, m_w_mem_kv, m_a_w_in, m_a_v_norm, m_a_w_spatial, m_a_b_spatial, m_a_w_out, m_kv_norm, m_w_kv, m_b_w_in, m_b_w_out, m_final_norm, v_ffn1_norm, v_ffn1_w_gate, v_ffn1_w_up, v_ffn1_w_down, v_mix_norm, v_ffn2_norm, v_ffn2_w_gate, v_ffn2_w_up, v_ffn2_w_down, v_mem_norm, v_w_mem_kv, v_a_w_in, v_a_v_norm, v_a_w_spatial, v_a_b_spatial, v_a_w_out, v_kv_norm, v_w_kv, v_b_w_in, v_b_w_out, v_final_norm):
    weights = dict(zip(W_NAMES, [ffn1_norm, ffn1_w_gate, ffn1_w_up, ffn1_w_down, mix_norm, ffn2_norm, ffn2_w_gate,
                                 ffn2_w_up, ffn2_w_down, mem_norm, w_mem_kv, a_w_in, a_v_norm, a_w_spatial,
                                 a_b_spatial, a_w_out, kv_norm, w_kv, b_w_in, b_w_out, final_norm]))
    m_in = dict(zip(W_NAMES, [m_ffn1_norm, m_ffn1_w_gate, m_ffn1_w_up, m_ffn1_w_down, m_mix_norm, m_ffn2_norm,
                              m_ffn2_w_gate, m_ffn2_w_up, m_ffn2_w_down, m_mem_norm, m_w_mem_kv, m_a_w_in,
                              m_a_v_norm, m_a_w_spatial, m_a_b_spatial, m_a_w_out, m_kv_norm, m_w_kv, m_b_w_in,
                              m_b_w_out, m_final_norm]))
    v_in = dict(zip(W_NAMES, [v_ffn1_norm, v_ffn1_w_gate, v_ffn1_w_up, v_ffn1_w_down, v_mix_norm, v_ffn2_norm,
                              v_ffn2_w_gate, v_ffn2_w_up, v_ffn2_w_down, v_mem_norm, v_w_mem_kv, v_a_w_in,
                              v_a_v_norm, v_a_w_spatial, v_a_b_spatial, v_a_w_out, v_kv_norm, v_w_kv, v_b_w_in,
                              v_b_w_out, v_final_norm]))

    full = gather_weights({n: weights[n] for n in SHARD_AXIS})
    W = {n: (full[n] if n in SHARD_AXIS else weights[n]) for n in W_NAMES}
    loss, dx, grads = local_step(x[0], mem[0], W, loss_target[0])
    total = reduce_grads(grads, {n: weights[n].shape for n in SHARD_AXIS})
    loss = lax.psum(loss[0, 0], ("x", "y", "c"))

    deltas, new_m, new_v = {}, {}, {}
    for n in W_NAMES:
        deltas[n], new_m[n], new_v[n] = adamw(weights[n], total[n], m_in[n], v_in[n], "adamw_" + n)
    return (loss, dx[None], *[total[n] for n in W_NAMES], *[deltas[n] for n in W_NAMES],
            *[new_m[n] for n in W_NAMES], *[new_v[n] for n in W_NAMES])
```

```python
import functools
import math

import jax
import jax.numpy as jnp
from jax import lax
from jax.experimental import pallas as pl
from jax.experimental.pallas import tpu as pltpu

F32 = jnp.float32
BF16 = jnp.bfloat16
EPS = 1e-6
LANES = 128
ROW_TILE = 512
COL_TILE = 1408
PACK_ROWS = 1024
PART_ROWS = 16
VMEM_LIMIT = 56 * 1024 * 1024

W_NAMES = ['ffn1_norm', 'ffn1_w_gate', 'ffn1_w_up', 'ffn1_w_down', 'mix_norm', 'ffn2_norm', 'ffn2_w_gate',
           'ffn2_w_up', 'ffn2_w_down', 'mem_norm', 'w_mem_kv', 'a_w_in', 'a_v_norm', 'a_w_spatial',
           'a_b_spatial', 'a_w_out', 'kv_norm', 'w_kv', 'b_w_in', 'b_w_out', 'final_norm']
SHARD_AXIS = {'ffn1_w_gate': 2, 'ffn1_w_up': 2, 'ffn1_w_down': 1, 'ffn2_w_gate': 2, 'ffn2_w_up': 2,
              'ffn2_w_down': 1, 'w_mem_kv': 1, 'a_w_in': 2, 'a_v_norm': 1, 'a_w_out': 1, 'w_kv': 1,
              'b_w_in': 1, 'b_w_out': 1}
N_CHIPS = 4
PACK_W = 1024

MEM_HEADS = 4
MEM_W = 256
HEAD_DIM = 64
GM_W = 768
GM_GROUPS = 6
GM_CHUNK = 128
CHUNK = 64
SB_BLOCK = 128
SB_Q_CHUNK = 1024
SB_DEAD = -110.0

ADAM_LR, ADAM_B1, ADAM_B2, ADAM_EPS, ADAM_WD, ADAM_STEP = 0.001, 0.9, 0.999, 1e-08, 0.01, 10

MESH = pl.DeviceIdType.MESH


def _cparams(sem=None):
    return pltpu.CompilerParams(dimension_semantics=sem, vmem_limit_bytes=VMEM_LIMIT)


def _tile(n, target, mult=LANES):
    best = None
    for t in range(mult, min(n, target) + 1, mult):
        if n % t == 0:
            best = t
    return best if best is not None else n


def _dot(a, b):
    return jnp.dot(a, b, preferred_element_type=F32)


def _dot_nt(a, b):
    return lax.dot_general(a, b, (((1,), (1,)), ((), ())), preferred_element_type=F32)


def _dot_tn(a, b):
    return lax.dot_general(a, b, (((0,), (0,)), ((), ())), preferred_element_type=F32)


def _bf(v):
    return v.astype(BF16)


def rms_fwd(x, gain, name):
    S, D = x.shape
    tm = _tile(S, ROW_TILE, 8)

    def body(x_ref, g_ref, o_ref):
        xf = x_ref[...]
        r = lax.rsqrt(jnp.mean(xf * xf, axis=-1, keepdims=True) + EPS)
        o_ref[...] = ((xf * r) * g_ref[...]).astype(o_ref.dtype)

    return pl.pallas_call(
        body, grid=(S // tm,),
        in_specs=[pl.BlockSpec((tm, D), lambda i: (i, 0)), pl.BlockSpec((1, D), lambda i: (0, 0))],
        out_specs=pl.BlockSpec((tm, D), lambda i: (i, 0)),
        out_shape=jax.ShapeDtypeStruct((S, D), BF16),
        compiler_params=_cparams(("parallel",)), name=name)(x, gain.reshape(1, D))


def rms_bwd(x, gain, dh, dres, name):
    S, D = x.shape
    tm = _tile(S, ROW_TILE, 8)
    has_res = dres is not None

    def body(*refs):
        if has_res:
            x_ref, g_ref, dh_ref, dres_ref, dx_ref, dg_ref = refs
        else:
            x_ref, g_ref, dh_ref, dx_ref, dg_ref = refs
        xf = x_ref[...]
        r = lax.rsqrt(jnp.mean(xf * xf, axis=-1, keepdims=True) + EPS)
        xhat = xf * r
        dhv = dh_ref[...].astype(F32)
        dxhat = dhv * g_ref[...]
        m = jnp.mean(dxhat * xhat, axis=-1, keepdims=True)
        dx = r * (dxhat - xhat * m)
        if has_res:
            dx = dx + dres_ref[...]
        dx_ref[...] = dx

        @pl.when(pl.program_id(0) == 0)
        def _():
            dg_ref[...] = jnp.zeros_like(dg_ref)

        dg_ref[...] += jnp.sum(dhv * xhat, axis=0, keepdims=True)

    row = pl.BlockSpec((tm, D), lambda i: (i, 0))
    vec = pl.BlockSpec((1, D), lambda i: (0, 0))
    ins = [x, gain.reshape(1, D), dh] + ([dres] if has_res else [])
    in_specs = [row, vec, row] + ([row] if has_res else [])
    dx, dg = pl.pallas_call(
        body, grid=(S // tm,), in_specs=in_specs, out_specs=[row, vec],
        out_shape=[jax.ShapeDtypeStruct((S, D), F32), jax.ShapeDtypeStruct((1, D), F32)],
        compiler_params=_cparams(("arbitrary",)), name=name)(*ins)
    return dx, dg


def final_loss(x, gain, target, name):
    S, D = x.shape
    tm = _tile(S, ROW_TILE, 8)

    def body(x_ref, g_ref, t_ref, loss_ref, dx_ref, dg_ref):
        xf = x_ref[...]
        r = lax.rsqrt(jnp.mean(xf * xf, axis=-1, keepdims=True) + EPS)
        xhat = xf * r
        g = g_ref[...]
        diff = xhat * g - t_ref[...]
        dy = diff * (1.0 / D)
        dxhat = dy * g
        m = jnp.mean(dxhat * xhat, axis=-1, keepdims=True)
        dx_ref[...] = r * (dxhat - xhat * m)

        @pl.when(pl.program_id(0) == 0)
        def _():
            dg_ref[...] = jnp.zeros_like(dg_ref)
            loss_ref[...] = jnp.zeros_like(loss_ref)

        dg_ref[...] += jnp.sum(dy * xhat, axis=0, keepdims=True)
        per_tok = jnp.sum(diff * diff, axis=-1, keepdims=True) * (0.5 / D)
        loss_ref[...] += jnp.sum(per_tok, axis=0, keepdims=True)

    row = pl.BlockSpec((tm, D), lambda i: (i, 0))
    vec = pl.BlockSpec((1, D), lambda i: (0, 0))
    one = pl.BlockSpec((1, 1), lambda i: (0, 0))
    return pl.pallas_call(
        body, grid=(S // tm,), in_specs=[row, vec, row], out_specs=[one, row, vec],
        out_shape=[jax.ShapeDtypeStruct((1, 1), F32), jax.ShapeDtypeStruct((S, D), F32),
                   jax.ShapeDtypeStruct((1, D), F32)],
        compiler_params=_cparams(("arbitrary",)), name=name)(x, gain.reshape(1, D), target)


def mm(pairs, *, tb=False, res=None, scale=1.0, out_dtype=F32, name):
    M = pairs[0][0].shape[0]
    N = pairs[0][1].shape[0] if tb else pairs[0][1].shape[1]
    tm = _tile(M, ROW_TILE, 8)
    tn = _tile(N, COL_TILE)
    n_p = len(pairs)
    has_res = res is not None

    def body(*refs):
        o_ref = refs[-1]
        acc = None
        for p in range(n_p):
            a = _bf(refs[2 * p][...])
            b = _bf(refs[2 * p + 1][...])
            d = _dot_nt(a, b) if tb else _dot(a, b)
            acc = d if acc is None else acc + d
        if scale != 1.0:
            acc = acc * scale
        if has_res:
            acc = acc + refs[2 * n_p][...]
        o_ref[...] = acc.astype(o_ref.dtype)

    ins, in_specs = [], []
    for a, b, cb in pairs:
        K = b.shape[1] if tb else b.shape[0]
        ins += [a, b]
        in_specs.append(pl.BlockSpec((tm, K), functools.partial(lambda i, j, cb: (i, cb), cb=cb)))
        if tb:
            in_specs.append(pl.BlockSpec((tn, K), lambda i, j: (j, 0)))
        else:
            in_specs.append(pl.BlockSpec((K, tn), lambda i, j: (0, j)))
    if has_res:
        ins.append(res)
        in_specs.append(pl.BlockSpec((tm, tn), lambda i, j: (i, j)))
    return pl.pallas_call(
        body, grid=(M // tm, N // tn), in_specs=in_specs,
        out_specs=pl.BlockSpec((tm, tn), lambda i, j: (i, j)),
        out_shape=jax.ShapeDtypeStruct((M, N), out_dtype),
        compiler_params=_cparams(("parallel", "parallel")), name=name)(*ins)


def mm_tn(a, b, *, a_cb=0, a_w=None, b_cb=0, b_w=None, scale=1.0, name):
    S = a.shape[0]
    Ka = a.shape[1] if a_w is None else a_w
    Nb = b.shape[1] if b_w is None else b_w
    ts = _tile(S, ROW_TILE, 8)
    tk = _tile(Ka, COL_TILE)
    tn = _tile(Nb, COL_TILE)
    a_off, b_off = a_cb * (Ka // tk), b_cb * (Nb // tn)

    def body(a_ref, b_ref, o_ref):
        s = pl.program_id(2)

        @pl.when(s == 0)
        def _():
            o_ref[...] = jnp.zeros_like(o_ref)

        o_ref[...] += _dot_tn(_bf(a_ref[...]), _bf(b_ref[...]))
        if scale != 1.0:
            @pl.when(s == pl.num_programs(2) - 1)
            def _():
                o_ref[...] = o_ref[...] * scale

    return pl.pallas_call(
        body, grid=(Ka // tk, Nb // tn, S // ts),
        in_specs=[pl.BlockSpec((ts, tk), lambda k, n, s: (s, a_off + k)),
                  pl.BlockSpec((ts, tn), lambda k, n, s: (s, b_off + n))],
        out_specs=pl.BlockSpec((tk, tn), lambda k, n, s: (k, n)),
        out_shape=jax.ShapeDtypeStruct((Ka, Nb), F32),
        compiler_params=_cparams(("parallel", "parallel", "arbitrary")), name=name)(a, b)


def ffn_up(h, wg, wu, name):
    S, D = h.shape
    Fd = wg.shape[1]
    tm = _tile(S, ROW_TILE, 8)
    tn = _tile(Fd, COL_TILE)

    def body(h_ref, wg_ref, wu_ref, g_ref, u_ref, a_ref):
        hv = h_ref[...]
        g = _dot(hv, wg_ref[...])
        u = _dot(hv, wu_ref[...])
        g_ref[...] = _bf(g)
        u_ref[...] = _bf(u)
        a_ref[...] = _bf(g * jax.nn.sigmoid(g) * u)

    blk = pl.BlockSpec((tm, tn), lambda i, j: (i, j))
    wspec = pl.BlockSpec((D, tn), lambda i, j: (0, j))
    sh = jax.ShapeDtypeStruct((S, Fd), BF16)
    return pl.pallas_call(
        body, grid=(S // tm, Fd // tn),
        in_specs=[pl.BlockSpec((tm, D), lambda i, j: (i, 0)), wspec, wspec],
        out_specs=[blk, blk, blk], out_shape=[sh, sh, sh],
        compiler_params=_cparams(("parallel", "parallel")), name=name)(h, wg, wu)


def ffn_down_bwd(dxo, wd, g, u, name):
    S, D = dxo.shape
    Fd = wd.shape[0]
    tm = _tile(S, ROW_TILE, 8)
    tn = _tile(Fd, COL_TILE)

    def body(dx_ref, wd_ref, g_ref, u_ref, dg_ref, du_ref):
        da = _dot_nt(_bf(dx_ref[...]), wd_ref[...]) * 0.5
        g = g_ref[...].astype(F32)
        u = u_ref[...].astype(F32)
        sg = jax.nn.sigmoid(g)
        dg_ref[...] = _bf(da * u * (sg * (1.0 + g * (1.0 - sg))))
        du_ref[...] = _bf(da * (g * sg))

    blk = pl.BlockSpec((tm, tn), lambda i, j: (i, j))
    sh = jax.ShapeDtypeStruct((S, Fd), BF16)
    return pl.pallas_call(
        body, grid=(S // tm, Fd // tn),
        in_specs=[pl.BlockSpec((tm, D), lambda i, j: (i, 0)), pl.BlockSpec((tn, D), lambda i, j: (j, 0)),
                  blk, blk],
        out_specs=[blk, blk], out_shape=[sh, sh],
        compiler_params=_cparams(("parallel", "parallel")), name=name)(dxo, wd, g, u)


_GELU_C = math.sqrt(2.0 / math.pi)


def _gelu(p):
    return 0.5 * p * (1.0 + jnp.tanh(_GELU_C * (p + 0.044715 * (p * p * p))))


def _gelu_grad(p):
    th = jnp.tanh(_GELU_C * (p + 0.044715 * (p * p * p)))
    return 0.5 * (1.0 + th) + 0.5 * p * (1.0 - th * th) * (_GELU_C * (1.0 + 3.0 * 0.044715 * (p * p)))


def _chunk_mask():
    t = lax.broadcasted_iota(jnp.int32, (GM_CHUNK, GM_CHUNK), 0)
    s = lax.broadcasted_iota(jnp.int32, (GM_CHUNK, GM_CHUNK), 1)
    return (s // CHUNK) <= (t // CHUNK)


def gmlp_fwd(proj, v_gain, w_s, b_mat, name):
    S = proj.shape[0]
    tw = _tile(S, ROW_TILE, GM_CHUNK)
    n_win = tw // GM_CHUNK

    def body(p_ref, gain_ref, w_ref, b_ref, y_ref):
        mask = _chunk_mask()
        v = _gelu(p_ref[:, GM_W:])
        r = lax.rsqrt(jnp.mean(v * v, axis=-1, keepdims=True) + EPS)
        vn = _bf(v * r * gain_ref[...])
        for g in range(GM_GROUPS):
            wm = _bf(jnp.where(mask, w_ref[g], 0.0))
            cs = slice(g * GM_CHUNK, (g + 1) * GM_CHUNK)
            for w in range(n_win):
                rs = slice(w * GM_CHUNK, (w + 1) * GM_CHUNK)
                mixed = _dot(wm, vn[rs, cs]) + b_ref[g]
                y_ref[rs, cs] = _bf(_gelu(p_ref[rs, cs]) * mixed)

    return pl.pallas_call(
        body, grid=(S // tw,),
        in_specs=[pl.BlockSpec((tw, 2 * GM_W), lambda i: (i, 0)), pl.BlockSpec((1, GM_W), lambda i: (0, 0)),
                  pl.BlockSpec((GM_GROUPS, GM_CHUNK, GM_CHUNK), lambda i: (0, 0, 0)),
                  pl.BlockSpec((GM_GROUPS, GM_CHUNK, GM_CHUNK), lambda i: (0, 0, 0))],
        out_specs=pl.BlockSpec((tw, GM_W), lambda i: (i, 0)),
        out_shape=jax.ShapeDtypeStruct((S, GM_W), BF16),
        compiler_params=_cparams(("parallel",)), name=name)(proj, v_gain.reshape(1, GM_W), w_s, b_mat)


def gmlp_bwd(proj, dy, v_gain, w_s, b_mat, name):
    S = proj.shape[0]
    tw = _tile(S, ROW_TILE, GM_CHUNK)
    n_win = tw // GM_CHUNK

    def body(p_ref, dy_ref, gain_ref, w_ref, b_ref, dp_ref, dw_ref, db_ref, dgain_ref, dvn_ref):
        step = pl.program_id(0)

        @pl.when(step == 0)
        def _():
            dw_ref[...] = jnp.zeros_like(dw_ref)
            db_ref[...] = jnp.zeros_like(db_ref)
            dgain_ref[...] = jnp.zeros_like(dgain_ref)

        mask = _chunk_mask()
        pv = p_ref[:, GM_W:]
        v = _gelu(pv)
        r = lax.rsqrt(jnp.mean(v * v, axis=-1, keepdims=True) + EPS)
        vhat = v * r
        gain = gain_ref[...]
        vn = _bf(vhat * gain)
        for g in range(GM_GROUPS):
            wm = _bf(jnp.where(mask, w_ref[g], 0.0))
            cs = slice(g * GM_CHUNK, (g + 1) * GM_CHUNK)
            dw_acc = jnp.zeros((GM_CHUNK, GM_CHUNK), F32)
            db_acc = jnp.zeros((GM_CHUNK, GM_CHUNK), F32)
            for w in range(n_win):
                rs = slice(w * GM_CHUNK, (w + 1) * GM_CHUNK)
                pu = p_ref[rs, cs]
                u = _gelu(pu)
                vn_blk = vn[rs, cs]
                mixed = _dot(wm, vn_blk) + b_ref[g]
                dyb = dy_ref[rs, cs]
                dp_ref[rs, cs] = _bf(dyb * mixed * _gelu_grad(pu))
                dmix = dyb * u
                db_acc = db_acc + dmix
                dmix_b = _bf(dmix)
                dw_acc = dw_acc + _dot_nt(dmix_b, vn_blk)
                dvn_ref[rs, cs] = _dot_tn(wm, dmix_b)
            dw_ref[g] += jnp.where(mask, dw_acc, 0.0)
            db_ref[g] += jnp.broadcast_to(jnp.sum(db_acc, axis=-1, keepdims=True), (GM_CHUNK, GM_CHUNK))
        dvn = dvn_ref[...]
        dgain_ref[...] += jnp.sum(dvn * vhat, axis=0, keepdims=True)
        dvhat = dvn * gain
        m = jnp.mean(dvhat * vhat, axis=-1, keepdims=True)
        dv = r * (dvhat - vhat * m)
        dp_ref[:, GM_W:] = _bf(dv * _gelu_grad(pv))

    sq = pl.BlockSpec((GM_GROUPS, GM_CHUNK, GM_CHUNK), lambda i: (0, 0, 0))
    sq_sh = jax.ShapeDtypeStruct((GM_GROUPS, GM_CHUNK, GM_CHUNK), F32)
    return pl.pallas_call(
        body, grid=(S // tw,),
        in_specs=[pl.BlockSpec((tw, 2 * GM_W), lambda i: (i, 0)), pl.BlockSpec((tw, GM_W), lambda i: (i, 0)),
                  pl.BlockSpec((1, GM_W), lambda i: (0, 0)), sq, sq],
        out_specs=[pl.BlockSpec((tw, 2 * GM_W), lambda i: (i, 0)), sq, sq,
                   pl.BlockSpec((1, GM_W), lambda i: (0, 0))],
        out_shape=[jax.ShapeDtypeStruct((S, 2 * GM_W), BF16), sq_sh, sq_sh,
                   jax.ShapeDtypeStruct((1, GM_W), F32)],
        scratch_shapes=[pltpu.VMEM((tw, GM_W), F32)],
        compiler_params=_cparams(("arbitrary",)), name=name)(proj, dy, v_gain.reshape(1, GM_W), w_s, b_mat)


def _head_mask(h, width):
    lane = lax.broadcasted_iota(jnp.int32, (1, width), 1)
    return (lane >= HEAD_DIM * h) & (lane < HEAD_DIM * (h + 1))


def _mem_probs(q, k, h):
    kh = jnp.where(_head_mask(h, MEM_W), k, jnp.zeros_like(k))
    s = _dot_nt(q, kh) * (HEAD_DIM ** -0.5)
    s = s - jnp.max(s, axis=-1, keepdims=True)
    e = jnp.exp(s)
    return e / jnp.sum(e, axis=-1, keepdims=True), kh


def memattn_fwd(proj, q_cb, mem_kv, name):
    S = proj.shape[0]
    NM = mem_kv.shape[0]
    tm = _tile(S, ROW_TILE, 8)

    def body(q_ref, kv_ref, o_ref):
        q = _bf(q_ref[...])
        k = _bf(kv_ref[:, :MEM_W])
        v = _bf(kv_ref[:, MEM_W:])
        acc = jnp.zeros((tm, MEM_W), F32)
        for h in range(MEM_HEADS):
            p, _ = _mem_probs(q, k, h)
            vh = jnp.where(_head_mask(h, MEM_W), v, jnp.zeros_like(v))
            acc = acc + _dot(_bf(p), vh)
        o_ref[...] = _bf(acc)

    return pl.pallas_call(
        body, grid=(S // tm,),
        in_specs=[pl.BlockSpec((tm, MEM_W), lambda i: (i, q_cb)), pl.BlockSpec((NM, 2 * MEM_W), lambda i: (0, 0))],
        out_specs=pl.BlockSpec((tm, MEM_W), lambda i: (i, 0)),
        out_shape=jax.ShapeDtypeStruct((S, MEM_W), BF16),
        compiler_params=_cparams(("parallel",)), name=name)(proj, mem_kv)


def memattn_bwd(proj, q_cb, mem_kv, dy, dy_cb, name):
    S = proj.shape[0]
    NM = mem_kv.shape[0]
    tm = _tile(S, ROW_TILE, 8)

    def body(q_ref, kv_ref, do_ref, dq_ref, dkv_ref):
        @pl.when(pl.program_id(0) == 0)
        def _():
            dkv_ref[...] = jnp.zeros_like(dkv_ref)

        q = _bf(q_ref[...])
        k = _bf(kv_ref[:, :MEM_W])
        v = _bf(kv_ref[:, MEM_W:])
        do = _bf(do_ref[...])
        dq = jnp.zeros((tm, MEM_W), F32)
        dk = jnp.zeros((NM, MEM_W), F32)
        dv = jnp.zeros((NM, MEM_W), F32)
        for h in range(MEM_HEADS):
            hm = _head_mask(h, MEM_W)
            p, kh = _mem_probs(q, k, h)
            vh = jnp.where(hm, v, jnp.zeros_like(v))
            dp = _dot_nt(do, vh)
            ds = _bf(p * (dp - jnp.sum(p * dp, axis=-1, keepdims=True)) * (HEAD_DIM ** -0.5))
            dq = dq + _dot(ds, kh)
            dk = dk + jnp.where(hm, _dot_tn(ds, q), 0.0)
            dv = dv + jnp.where(hm, _dot_tn(_bf(p), do), 0.0)
        dq_ref[...] = _bf(dq)
        dkv_ref[:, :MEM_W] += dk
        dkv_ref[:, MEM_W:] += dv

    return pl.pallas_call(
        body, grid=(S // tm,),
        in_specs=[pl.BlockSpec((tm, MEM_W), lambda i: (i, q_cb)), pl.BlockSpec((NM, 2 * MEM_W), lambda i: (0, 0)),
                  pl.BlockSpec((tm, MEM_W), lambda i: (i, dy_cb))],
        out_specs=[pl.BlockSpec((tm, MEM_W), lambda i: (i, 0)), pl.BlockSpec((NM, 2 * MEM_W), lambda i: (0, 0))],
        out_shape=[jax.ShapeDtypeStruct((S, MEM_W), BF16), jax.ShapeDtypeStruct((NM, 2 * MEM_W), F32)],
        compiler_params=_cparams(("arbitrary",)), name=name)(proj, mem_kv, dy)


def _split_bf(v):
    hi = _bf(v)
    return hi, _bf(v - hi.astype(F32))


def _sb_weights(qh, k2, run, dpos, row, col, tri):
    z = _dot_nt(qh, k2) * (HEAD_DIM ** -0.5)
    t = jnp.log(1.0 + jnp.exp(-jnp.abs(z)))
    lb = jnp.minimum(z, 0.0) - t
    mask = (col - row) < dpos
    lom = jnp.where(mask, -jnp.maximum(z, 0.0) - t, 0.0)
    l_hi, l_lo = _split_bf(lom)
    insuf = _dot(l_hi, tri) + _dot(l_lo, tri)
    a = jnp.where(mask, jnp.exp(lb + run + insuf), 0.0)
    return a, lb, lom, mask


def _sb_alive(runs):
    top = jnp.maximum(jnp.max(runs[0]), jnp.max(runs[1]))
    return top > SB_DEAD


def _sb_consts():
    row = lax.broadcasted_iota(jnp.int32, (SB_BLOCK, SB_BLOCK), 0)
    col = lax.broadcasted_iota(jnp.int32, (SB_BLOCK, SB_BLOCK), 1)
    return row, col


def sb_fwd(proj, kv, name):
    S = proj.shape[0]
    n_hp = GM_W // LANES
    tq = _tile(S, SB_Q_CHUNK, SB_BLOCK)
    nqb = tq // SB_BLOCK

    def body(q_ref, k_ref, v_ref, o_ref):
        chunk = pl.program_id(1)
        row, col = _sb_consts()
        tri = _bf(jnp.where(row > col, 1.0, 0.0))
        masks = [_head_mask(h, LANES) for h in range(2)]

        def q_loop(qi, _):
            i = chunk * nqb + qi
            rows = pl.ds(pl.multiple_of(qi * SB_BLOCK, SB_BLOCK), SB_BLOCK)
            q2 = q_ref[rows, :]
            qhs = [_bf(jnp.where(hm, q2, 0.0)) for hm in masks]

            def k_step(carry):
                t, acc, runs, _ = carry
                keys = pl.ds(pl.multiple_of((i - t) * SB_BLOCK, SB_BLOCK), SB_BLOCK)
                k2, v2 = k_ref[keys, :], v_ref[keys, :]
                new_runs = []
                for h in range(2):
                    a, _, lom, _ = _sb_weights(qhs[h], k2, runs[h], t * SB_BLOCK, row, col, tri)
                    vh = jnp.where(masks[h], v2, jnp.zeros_like(v2))
                    a_hi, a_lo = _split_bf(a)
                    acc = acc + _dot(a_hi, vh) + _dot(a_lo, vh)
                    new_runs.append(runs[h] + jnp.sum(lom, axis=1, keepdims=True))
                return t + 1, acc, tuple(new_runs), _sb_alive(new_runs)

            zero = jnp.zeros((SB_BLOCK, 1), F32)
            _, acc, _, _ = lax.while_loop(
                lambda carry: jnp.logical_and(carry[0] <= i, carry[3]), k_step,
                (jnp.int32(0), jnp.zeros((SB_BLOCK, LANES), F32), (zero, zero), jnp.bool_(True)))
            o_ref[rows, :] = acc
            return 0

        lax.fori_loop(0, nqb, q_loop, 0)

    return pl.pallas_call(
        body, grid=(n_hp, S // tq),
        in_specs=[pl.BlockSpec((tq, LANES), lambda hp, c: (c, hp)),
                  pl.BlockSpec((S, LANES), lambda hp, c: (0, hp)),
                  pl.BlockSpec((S, LANES), lambda hp, c: (0, n_hp + hp))],
        out_specs=pl.BlockSpec((tq, LANES), lambda hp, c: (c, hp)),
        out_shape=jax.ShapeDtypeStruct((S, GM_W), F32),
        compiler_params=_cparams(("parallel", "parallel")), name=name)(proj, kv, kv)


def sb_bwd(proj, kv, dy, out, dk_init, dv_init, name):
    S = proj.shape[0]
    n_hp = GM_W // LANES
    tq = _tile(S, SB_Q_CHUNK, SB_BLOCK)
    nqb = tq // SB_BLOCK
    has_init = dk_init is not None
    scale = HEAD_DIM ** -0.5

    def body(*refs):
        if has_init:
            q_ref, k_ref, v_ref, do_ref, out_ref, dki_ref, dvi_ref, dq_ref, dk_ref, dv_ref = refs
        else:
            q_ref, k_ref, v_ref, do_ref, out_ref, dq_ref, dk_ref, dv_ref = refs
        hp = pl.program_id(0)
        chunk = pl.program_id(1)

        @pl.when(chunk == 0)
        def _():
            if has_init:
                cols = pl.ds(pl.multiple_of(hp * LANES, LANES), LANES)
                pltpu.sync_copy(dki_ref.at[:, cols], dk_ref)
                pltpu.sync_copy(dvi_ref.at[:, cols], dv_ref)
            else:
                dk_ref[...] = jnp.zeros_like(dk_ref)
                dv_ref[...] = jnp.zeros_like(dv_ref)

        row, col = _sb_consts()
        tri = _bf(jnp.where(row > col, 1.0, 0.0))
        tri_inc = _bf(jnp.where(row >= col, 1.0, 0.0))
        masks = [_head_mask(h, LANES) for h in range(2)]

        def q_loop(qi, _):
            i = chunk * nqb + qi
            rows = pl.ds(pl.multiple_of(qi * SB_BLOCK, SB_BLOCK), SB_BLOCK)
            q2, do2, out2 = q_ref[rows, :], do_ref[rows, :], out_ref[rows, :]
            qhs = [_bf(jnp.where(hm, q2, 0.0)) for hm in masks]
            dohs = [_bf(jnp.where(hm, do2, 0.0)) for hm in masks]
            e_tots = [jnp.sum(doh.astype(F32) * out2, axis=1, keepdims=True) for doh in dohs]

            def k_step(carry):
                t, dq, runs, e_runs, _ = carry
                keys = pl.ds(pl.multiple_of((i - t) * SB_BLOCK, SB_BLOCK), SB_BLOCK)
                k2, v2 = k_ref[keys, :], v_ref[keys, :]
                dk_blk = jnp.zeros((SB_BLOCK, LANES), F32)
                dv_blk = jnp.zeros((SB_BLOCK, LANES), F32)
                new_runs, new_e_runs = [], []
                for h in range(2):
                    a, lb, lom, mask = _sb_weights(qhs[h], k2, runs[h], t * SB_BLOCK, row, col, tri)
                    vh = jnp.where(masks[h], v2, jnp.zeros_like(v2))
                    e = a * _dot_nt(dohs[h], vh)
                    e_hi, e_lo = _split_bf(e)
                    before = e_tots[h] - e_runs[h] - (_dot(e_hi, tri_inc) + _dot(e_lo, tri_inc))
                    beta = jnp.exp(lb)
                    dz = _bf(jnp.where(mask, e * (1.0 - beta) - before * beta, 0.0) * scale)
                    kh = jnp.where(masks[h], k2, jnp.zeros_like(k2))
                    dq = dq + _dot(dz, kh)
                    dk_blk = dk_blk + _dot_tn(dz, qhs[h])
                    dv_blk = dv_blk + _dot_tn(_bf(a), dohs[h])
                    new_runs.append(runs[h] + jnp.sum(lom, axis=1, keepdims=True))
                    new_e_runs.append(e_runs[h] + jnp.sum(e, axis=1, keepdims=True))
                dk_ref[keys, :] += dk_blk
                dv_ref[keys, :] += dv_blk
                return t + 1, dq, tuple(new_runs), tuple(new_e_runs), _sb_alive(new_runs)

            zero = jnp.zeros((SB_BLOCK, 1), F32)
            _, dq, _, _, _ = lax.while_loop(
                lambda carry: jnp.logical_and(carry[0] <= i, carry[4]), k_step,
                (jnp.int32(0), jnp.zeros((SB_BLOCK, LANES), F32), (zero, zero), (zero, zero), jnp.bool_(True)))
            dq_ref[rows, :] = dq
            return 0

        lax.fori_loop(0, nqb, q_loop, 0)

    qspec = pl.BlockSpec((tq, LANES), lambda hp, c: (c, hp))
    kspec = pl.BlockSpec((S, LANES), lambda hp, c: (0, hp))
    in_specs = [qspec, kspec, pl.BlockSpec((S, LANES), lambda hp, c: (0, n_hp + hp)), qspec, qspec]
    ins = [proj, kv, kv, dy, out]
    if has_init:
        in_specs += [pl.BlockSpec(memory_space=pl.ANY), pl.BlockSpec(memory_space=pl.ANY)]
        ins += [dk_init, dv_init]
    sh = jax.ShapeDtypeStruct((S, GM_W), F32)
    return pl.pallas_call(
        body, grid=(n_hp, S // tq), in_specs=in_specs, out_specs=[qspec, kspec, kspec],
        out_shape=[sh, sh, sh],
        compiler_params=_cparams(("parallel", "arbitrary")), name=name)(*ins)


def adamw(w, g, m, v, name):
    shape = w.shape
    C = shape[-1] if w.ndim > 1 else shape[0]
    R = w.size // C
    tr = _tile(R, ROW_TILE, 8)

    def body(w_ref, g_ref, m_ref, v_ref, d_ref, nm_ref, nv_ref):
        gv = g_ref[...]
        m2 = ADAM_B1 * m_ref[...] + (1.0 - ADAM_B1) * gv
        v2 = ADAM_B2 * v_ref[...] + (1.0 - ADAM_B2) * (gv * gv)
        m_hat = m2 / (1.0 - ADAM_B1 ** ADAM_STEP)
        v_hat = v2 / (1.0 - ADAM_B2 ** ADAM_STEP)
        d_ref[...] = -ADAM_LR * (m_hat / (jnp.sqrt(v_hat) + ADAM_EPS) + ADAM_WD * w_ref[...])
        nm_ref[...] = m2
        nv_ref[...] = v2

    blk = pl.BlockSpec((tr, C), lambda i: (i, 0))
    sh = jax.ShapeDtypeStruct((R, C), F32)
    outs = pl.pallas_call(
        body, grid=(R // tr,), in_specs=[blk] * 4, out_specs=[blk] * 3, out_shape=[sh] * 3,
        compiler_params=_cparams(("parallel",)), name=name)(
            w.reshape(R, C), g.reshape(R, C), m.reshape(R, C), v.reshape(R, C))
    return tuple(o.reshape(shape) for o in outs)


def _place():
    return lax.axis_index("x"), lax.axis_index("y"), lax.axis_index("c")


def _other_chips(x, y):
    return [(1 - x, y), (x, 1 - y), (1 - x, 1 - y)]


_ANY = pl.BlockSpec(memory_space=pl.ANY)
LOCAL_CHUNKS = 8


def _local_copies(src_of, dst_of, n_rows, sems, src_base=0, dst_base=0):
    rows = n_rows // LOCAL_CHUNKS
    copies = []
    for k in range(LOCAL_CHUNKS):
        cp = pltpu.make_async_copy(src_of(pl.ds(src_base + k * rows, rows)),
                                   dst_of(pl.ds(dst_base + k * rows, rows)), sems.at[k])
        cp.start()
        copies.append(cp)
    return copies


def all_gather_chips(buf, name):
    P, Wd = buf.shape
    Ph = P // 2

    def body(in_ref, out_ref, send_sems, recv_sems, local_sem):
        x, y, c = _place()
        sibling = (x, y, 1 - c)
        chips = _other_chips(x, y)

        def half(px, py, pc):
            return out_ref.at[2 * px + py, pl.ds(pc * Ph, Ph), :]

        def copy(k, src, dst, to):
            return pltpu.make_async_remote_copy(src_ref=src, dst_ref=dst, send_sem=send_sems.at[k],
                                                recv_sem=recv_sems.at[k], device_id=to, device_id_type=MESH)

        mine = _local_copies(lambda r: in_ref.at[r, :], lambda r: out_ref.at[2 * x + y, r, :], P, local_sem)
        my_half = in_ref.at[pl.ds(c * Ph, Ph), :]
        first = [copy(j, my_half, half(x, y, c), (*chip, c)) for j, chip in enumerate(chips)]
        for cp in first:
            cp.start()
        passed = [copy(3 + j, half(*chip, c), half(*chip, c), sibling) for j, chip in enumerate(chips)]
        for j, chip in enumerate(chips):
            copy(j, my_half, half(*chip, c), (*chip, c)).wait_recv()
            passed[j].start()
        for j, chip in enumerate(chips):
            copy(3 + j, my_half, half(*chip, 1 - c), sibling).wait_recv()
        for cp in first + passed:
            cp.wait_send()
        for cp in mine:
            cp.wait()

    return pl.pallas_call(
        body, in_specs=[_ANY], out_specs=_ANY,
        out_shape=jax.ShapeDtypeStruct((N_CHIPS, P, Wd), buf.dtype),
        scratch_shapes=[pltpu.SemaphoreType.DMA((6,)), pltpu.SemaphoreType.DMA((6,)),
                        pltpu.SemaphoreType.DMA((LOCAL_CHUNKS,))],
        name=name)(buf)


def swap_halves(g, name):
    n, P, Wd = g.shape
    Ph = P // 2

    def body(g_ref, theirs_ref, send_sems, recv_sems):
        x, y, c = _place()
        gives = [pltpu.make_async_remote_copy(
            src_ref=g_ref.at[k, pl.ds((1 - c) * Ph, Ph), :], dst_ref=theirs_ref.at[k], send_sem=send_sems.at[k],
            recv_sem=recv_sems.at[k], device_id=(x, y, 1 - c), device_id_type=MESH) for k in range(n)]
        for cp in gives:
            cp.start()
        for cp in gives:
            cp.wait()

    return pl.pallas_call(
        body, in_specs=[_ANY], out_specs=_ANY, out_shape=jax.ShapeDtypeStruct((n, Ph, Wd), g.dtype),
        scratch_shapes=[pltpu.SemaphoreType.DMA((n,)), pltpu.SemaphoreType.DMA((n,))],
        name=name)(g)


def _core_index():
    return lax.axis_index("c").astype(jnp.int32).reshape(1)


def add_cores(g, theirs, name):
    n, Ph, Wd = theirs.shape
    tr = _tile(Ph, ROW_TILE, 8)
    steps = Ph // tr

    def body(c_ref, g_ref, t_ref, o_ref):
        o_ref[...] = g_ref[...] + t_ref[...]

    blk = pl.BlockSpec((None, tr, Wd), lambda k, i, c_ref: (k, i, 0))
    return pl.pallas_call(
        body,
        grid_spec=pltpu.PrefetchScalarGridSpec(
            num_scalar_prefetch=1, grid=(n, steps),
            in_specs=[pl.BlockSpec((None, tr, Wd), lambda k, i, c_ref: (k, c_ref[0] * steps + i, 0)), blk],
            out_specs=blk),
        out_shape=jax.ShapeDtypeStruct((n, Ph, Wd), F32),
        compiler_params=_cparams(("parallel", "parallel")), name=name)(_core_index(), g, theirs)


def add_chips(slots, name):
    n, Ph, Wd = slots.shape
    tr = _tile(Ph, ROW_TILE, 8)
    steps = Ph // tr

    def body(c_ref, *refs):
        acc = refs[0][...]
        for k in range(1, n):
            acc = acc + refs[k][...]
        refs[n][...] = acc

    in_specs = [pl.BlockSpec((None, tr, Wd), functools.partial(lambda i, c_ref, k: (k, i, 0), k=k)) for k in range(n)]
    return pl.pallas_call(
        body,
        grid_spec=pltpu.PrefetchScalarGridSpec(
            num_scalar_prefetch=1, grid=(steps,), in_specs=in_specs,
            out_specs=pl.BlockSpec((tr, Wd), lambda i, c_ref: (c_ref[0] * steps + i, 0))),
        out_shape=jax.ShapeDtypeStruct((2 * Ph, Wd), F32),
        compiler_params=_cparams(("parallel",)), name=name)(_core_index(), *([slots] * n))


def scatter_to_chips(h, name):
    n, Ph, Wd = h.shape

    def body(h_ref, out_ref, send_sems, recv_sems, local_sem):
        x, y, c = _place()
        me = 2 * x + y
        chips = _other_chips(x, y)
        keep = _local_copies(lambda r: h_ref.at[me, r, :], lambda r: out_ref.at[me, r, :], Ph, local_sem)
        sends = [pltpu.make_async_remote_copy(
            src_ref=h_ref.at[2 * cx + cy], dst_ref=out_ref.at[me], send_sem=send_sems.at[j],
            recv_sem=recv_sems.at[j], device_id=(cx, cy, c), device_id_type=MESH)
            for j, (cx, cy) in enumerate(chips)]
        for cp in sends:
            cp.start()
        for j, (cx, cy) in enumerate(chips):
            pltpu.make_async_remote_copy(
                src_ref=h_ref.at[me], dst_ref=out_ref.at[2 * cx + cy], send_sem=send_sems.at[j],
                recv_sem=recv_sems.at[j], device_id=(cx, cy, c), device_id_type=MESH).wait_recv()
        for cp in sends:
            cp.wait_send()
        for cp in keep:
            cp.wait()

    return pl.pallas_call(
        body, in_specs=[_ANY], out_specs=_ANY, out_shape=jax.ShapeDtypeStruct((n, Ph, Wd), h.dtype),
        scratch_shapes=[pltpu.SemaphoreType.DMA((3,)), pltpu.SemaphoreType.DMA((3,)),
                        pltpu.SemaphoreType.DMA((LOCAL_CHUNKS,))],
        name=name)(h)


def join_halves(t, name):
    P, Wd = t.shape
    Ph = P // 2
    n = 4
    rows = Ph // n

    def body(t_ref, out_ref, send_sems, recv_sems):
        x, y, c = _place()
        gives = []
        for k in range(n):
            part = out_ref.at[pl.ds(c * Ph + k * rows, rows), :]
            gives.append(pltpu.make_async_remote_copy(
                src_ref=part, dst_ref=part, send_sem=send_sems.at[k], recv_sem=recv_sems.at[k],
                device_id=(x, y, 1 - c), device_id_type=MESH))
        for cp in gives:
            cp.start()
        for k in range(n):
            landed = out_ref.at[pl.ds((1 - c) * Ph + k * rows, rows), :]
            pltpu.make_async_remote_copy(
                src_ref=landed, dst_ref=landed, send_sem=send_sems.at[k], recv_sem=recv_sems.at[k],
                device_id=(x, y, 1 - c), device_id_type=MESH).wait_recv()
        for cp in gives:
            cp.wait_send()

    return pl.pallas_call(
        body, in_specs=[_ANY], out_specs=_ANY, out_shape=jax.ShapeDtypeStruct((P, Wd), t.dtype),
        scratch_shapes=[pltpu.SemaphoreType.DMA((n,)), pltpu.SemaphoreType.DMA((n,))],
        input_output_aliases={0: 0}, name=name)(t)


def _pad_rows(flat, mult_rows=1):
    n = flat.shape[0]
    rows = -(-n // PACK_W)
    rows = -(-rows // mult_rows) * mult_rows
    return jnp.pad(flat, (0, rows * PACK_W - n)).reshape(rows, PACK_W)


def _pack(parts, row_mult):
    blocks = [_pad_rows(p.reshape(-1), PART_ROWS) for p in parts]
    buf = jnp.concatenate(blocks, axis=0)
    rows = buf.shape[0]
    total = -(-rows // row_mult) * row_mult
    return jnp.pad(buf, ((0, total - rows), (0, 0)))


def _unpack(buf, shapes):
    outs, r = [], 0
    lead = buf.shape[:-2]
    for shp in shapes:
        n = math.prod(shp)
        rows = -(-n // PACK_W)
        blk = buf[..., r:r + rows, :].reshape(lead + (rows * PACK_W,))[..., :n]
        outs.append(blk.reshape(lead + tuple(shp)))
        r += -(-rows // PART_ROWS) * PART_ROWS
    return outs


def gather_weights(shards):
    names = [n for n in W_NAMES if n in SHARD_AXIS]
    parts = []
    for n in names:
        w = shards[n]
        if n == 'a_v_norm':
            parts.append(lax.bitcast_convert_type(w, BF16))
        else:
            parts.append(w.astype(BF16))
    buf = _pack(parts, PACK_ROWS)
    full = all_gather_chips(buf, "all_gather_weights")
    got = _unpack(full, [p.shape for p in parts])
    out = {}
    for n, g in zip(names, got):
        if n == 'a_v_norm':
            g = lax.bitcast_convert_type(g, F32)
        out[n] = jnp.concatenate([g[k] for k in range(N_CHIPS)], axis=SHARD_AXIS[n])
    return out


def reduce_grads(grads, shard_shapes):
    sharded = [n for n in W_NAMES if n in SHARD_AXIS]
    repl = [n for n in W_NAMES if n not in SHARD_AXIS]
    blocks = []
    for k in range(N_CHIPS):
        parts = [jnp.split(grads[n], N_CHIPS, axis=SHARD_AXIS[n])[k] for n in sharded]
        parts += [grads[n] for n in repl]
        blocks.append(_pack(parts, PACK_ROWS))
    g = jnp.stack(blocks, axis=0)
    theirs = swap_halves(g, "grads_swap_halves")
    chip_sum = add_cores(g, theirs, "grads_add_cores")
    slots = scatter_to_chips(chip_sum, "grads_scatter")
    total = join_halves(add_chips(slots, "grads_add_chips"), "grads_join_halves")
    shapes = [shard_shapes[n] for n in sharded] + [grads[n].shape for n in repl]
    return dict(zip(sharded + repl, _unpack(total, shapes)))


def _ffn_fwd(x, gain, wg, wu, wd, tag):
    h = rms_fwd(x, gain, tag + "_norm")
    g, u, a = ffn_up(h, wg, wu, tag + "_up")
    xo = mm([(a, wd, 0)], res=x, scale=0.5, name=tag + "_down")
    return xo, (x, h, g, u, a)


def _ffn_bwd(dxo, saved, gain, wg, wu, wd, tag):
    x, h, g, u, a = saved
    dgp, du = ffn_down_bwd(dxo, wd, g, u, tag + "_down_bwd")
    d_wd = mm_tn(a, dxo, scale=0.5, name=tag + "_dwd")
    d_wg = mm_tn(h, dgp, name=tag + "_dwg")
    d_wu = mm_tn(h, du, name=tag + "_dwu")
    dh = mm([(dgp, wg, 0), (du, wu, 0)], tb=True, name=tag + "_up_bwd")
    dx, d_gain = rms_bwd(x, gain, dh, dxo, tag + "_norm_bwd")
    return dx, d_gain, d_wg, d_wu, d_wd


def local_step(x, mem, W, target):
    depth = W['ffn1_norm'].shape[0]
    n_a = W['a_w_in'].shape[0]
    G = {}
    mem_h = rms_fwd(mem, W['mem_norm'], "mem_norm")
    b_mats = jnp.broadcast_to(W['a_b_spatial'][..., None], W['a_b_spatial'].shape + (GM_CHUNK,))
    saved = []
    kv = kvn = x_kv = None
    for l in range(depth):
        if l == n_a:
            x_kv = x
            kvn = rms_fwd(x, W['kv_norm'], "kv_norm")
            kv = mm([(kvn, W['w_kv'], 0)], out_dtype=BF16, name="kv_proj")
        x, s1 = _ffn_fwd(x, W['ffn1_norm'][l], W['ffn1_w_gate'][l], W['ffn1_w_up'][l], W['ffn1_w_down'][l],
                         f"l{l}_ffn1")
        hm = rms_fwd(x, W['mix_norm'][l], f"l{l}_mix_norm")
        mem_kv = mm([(mem_h, W['w_mem_kv'][l], 0)], name=f"l{l}_mem_kv")
        if l < n_a:
            w_in, w_out = W['a_w_in'][l], W['a_w_out'][l]
            proj = mm([(hm, w_in, 0)], name=f"l{l}_mix_in")
            y_tok = gmlp_fwd(proj, W['a_v_norm'][l], W['a_w_spatial'][l], b_mats[l], f"l{l}_gmlp")
            q_cb = 2 * GM_W // MEM_W
            extra = None
        else:
            w_in, w_out = W['b_w_in'][l - n_a], W['b_w_out'][l - n_a]
            proj = mm([(hm, w_in, 0)], name=f"l{l}_mix_in")
            y_tok = sb_fwd(proj, kv, f"l{l}_sb")
            q_cb = GM_W // MEM_W
            extra = None
        y_mem = memattn_fwd(proj, q_cb, mem_kv, f"l{l}_memattn")
        x_mid = x
        x = mm([(y_tok, w_out[:GM_W], 0), (y_mem, w_out[GM_W:], 0)], res=x, name=f"l{l}_mix_out")
        sm = (x_mid, hm, mem_kv, proj, y_tok, y_mem, q_cb, extra)
        x, s2 = _ffn_fwd(x, W['ffn2_norm'][l], W['ffn2_w_gate'][l], W['ffn2_w_up'][l], W['ffn2_w_down'][l],
                         f"l{l}_ffn2")
        saved.append((s1, sm, s2))

    loss, dx, d_final = final_loss(x, W['final_norm'], target, "loss_head")
    G['final_norm'] = d_final.reshape(-1)

    per_layer = {n: [None] * depth for n in ['ffn1_norm', 'ffn1_w_gate', 'ffn1_w_up', 'ffn1_w_down', 'mix_norm',
                                             'ffn2_norm', 'ffn2_w_gate', 'ffn2_w_up', 'ffn2_w_down', 'w_mem_kv']}
    per_a = {n: [None] * n_a for n in ['a_w_in', 'a_v_norm', 'a_w_spatial', 'a_b_spatial', 'a_w_out']}
    per_b = {n: [None] * (depth - n_a) for n in ['b_w_in', 'b_w_out']}
    d_mem_h = None
    dk = dv = None
    for l in reversed(range(depth)):
        s1, sm, s2 = saved[l]
        dx, dg, dwg, dwu, dwd = _ffn_bwd(dx, s2, W['ffn2_norm'][l], W['ffn2_w_gate'][l], W['ffn2_w_up'][l],
                                         W['ffn2_w_down'][l], f"l{l}_ffn2")
        per_layer['ffn2_norm'][l], per_layer['ffn2_w_gate'][l] = dg.reshape(-1), dwg
        per_layer['ffn2_w_up'][l], per_layer['ffn2_w_down'][l] = dwu, dwd

        x_mid, hm, mem_kv, proj, y_tok, y_mem, q_cb, _ = sm
        is_a = l < n_a
        w_in = W['a_w_in'][l] if is_a else W['b_w_in'][l - n_a]
        w_out = W['a_w_out'][l] if is_a else W['b_w_out'][l - n_a]
        dy = mm([(dx, w_out, 0)], tb=True, name=f"l{l}_mix_out_bwd")
        d_w_out = jnp.concatenate([mm_tn(y_tok, dx, name=f"l{l}_dwout_tok"),
                                   mm_tn(y_mem, dx, name=f"l{l}_dwout_mem")], axis=0)
        dq_mem, d_mem_kv = memattn_bwd(proj, q_cb, mem_kv, dy, GM_W // MEM_W, f"l{l}_memattn_bwd")
        if is_a:
            dp_uv, d_ws, d_bs, d_vg = gmlp_bwd(proj, dy, W['a_v_norm'][l], W['a_w_spatial'][l],
                                               jnp.broadcast_to(W['a_b_spatial'][l][..., None],
                                                                (GM_GROUPS, GM_CHUNK, GM_CHUNK)), f"l{l}_gmlp_bwd")
            per_a['a_w_spatial'][l], per_a['a_b_spatial'][l] = d_ws, d_bs[:, :, 0]
            per_a['a_v_norm'][l], per_a['a_w_out'][l] = d_vg.reshape(-1), d_w_out
            d_tok = dp_uv
            tok_w = 2 * GM_W
        else:
            d_tok, dk, dv = sb_bwd(proj, kv, dy, y_tok, dk, dv, f"l{l}_sb_bwd")
            per_b['b_w_out'][l - n_a] = d_w_out
            tok_w = GM_W
        d_w_in = jnp.concatenate([mm_tn(hm, d_tok, name=f"l{l}_dwin_tok"),
                                  mm_tn(hm, dq_mem, name=f"l{l}_dwin_mem")], axis=1)
        (per_a['a_w_in'] if is_a else per_b['b_w_in'])[l if is_a else l - n_a] = d_w_in
        dh = mm([(d_tok, w_in[:, :tok_w], 0), (dq_mem, w_in[:, tok_w:], 0)], tb=True, name=f"l{l}_mix_in_bwd")
        dx, dg = rms_bwd(x_mid, W['mix_norm'][l], dh, dx, f"l{l}_mix_norm_bwd")
        per_layer['mix_norm'][l] = dg.reshape(-1)
        per_layer['w_mem_kv'][l] = mm_tn(mem_h, d_mem_kv, name=f"l{l}_dw_mem_kv")
        res = mm([(d_mem_kv, W['w_mem_kv'][l], 0)], tb=True, res=d_mem_h, name=f"l{l}_mem_kv_bwd")
        d_mem_h = res

        dx, dg, dwg, dwu, dwd = _ffn_bwd(dx, s1, W['ffn1_norm'][l], W['ffn1_w_gate'][l], W['ffn1_w_up'][l],
                                         W['ffn1_w_down'][l], f"l{l}_ffn1")
        per_layer['ffn1_norm'][l], per_layer['ffn1_w_gate'][l] = dg.reshape(-1), dwg
        per_layer['ffn1_w_up'][l], per_layer['ffn1_w_down'][l] = dwu, dwd
        if l == n_a:
            w_kv = W['w_kv']
            G['w_kv'] = jnp.concatenate([mm_tn(kvn, dk, name="dw_k"), mm_tn(kvn, dv, name="dw_v")], axis=1)
            d_kvn = mm([(dk, w_kv[:, :GM_W], 0), (dv, w_kv[:, GM_W:], 0)], tb=True, name="kv_proj_bwd")
            dx, dg = rms_bwd(x_kv, W['kv_norm'], d_kvn, dx, "kv_norm_bwd")
            G['kv_norm'] = dg.reshape(-1)

    _, dg = rms_bwd(mem, W['mem_norm'], d_mem_h, None, "mem_norm_bwd")
    G['mem_norm'] = dg.reshape(-1)
    for d in (per_layer, per_a, per_b):
        for n, v in d.items():
            G[n] = jnp.stack(v, axis=0)
    return loss, dx, G


def kernel(x, mem, ffn1_norm, ffn1_w_gate, ffn1_w_up, ffn1_w_down, mix_norm, ffn2_norm, ffn2_w_gate, ffn2_w_up, ffn2_w_down, mem_norm, w_mem_kv, a_w_in, a_v_norm, a_w_spatial, a_b_spatial, a_w_out, kv_norm, w_kv, b_w_in, b_w_out, final_norm, loss_target, m_ffn1_norm, m_ffn1_w_gate, m_ffn1_w_up, m_ffn1_w_down, m_mix_norm, m_ffn2_norm, m_ffn2_w_gate, m_ffn2_w_up, m_ffn2_w_down, m_mem_norm, m_w_mem_kv, m_a_w_in, m_a_v_norm, m_a_w_spatial, m_a_b_spatial, m_a_w_out, m_kv_norm, m_w_kv, m_b_w_in, m_b_w_out, m_final_norm, v_ffn1_norm, v_ffn1_w_gate, v_ffn1_w_up, v_ffn1_w_down, v_mix_norm, v_ffn2_norm, v_ffn2_w_gate, v_ffn2_w_up, v_ffn2_w_down, v_mem_norm, v_w_mem_kv, v_a_w_in, v_a_v_norm, v_a_w_spatial, v_a_b_spatial, v_a_w_out, v_kv_norm, v_w_kv, v_b_w_in, v_b_w_out, v_final_norm):
    weights = dict(zip(W_NAMES, [ffn1_norm, ffn1_w_gate, ffn1_w_up, ffn1_w_down, mix_norm, ffn2_norm, ffn2_w_gate,
                                 ffn2_w_up, ffn2_w_down, mem_norm, w_mem_kv, a_w_in, a_v_norm, a_w_spatial,
                                 a_b_spatial, a_w_out, kv_norm, w_kv, b_w_in, b_w_out, final_norm]))
    m_in = dict(zip(W_NAMES, [m_ffn1_norm, m_ffn1_w_gate, m_ffn1_w_up, m_ffn1_w_down, m_mix_norm, m_ffn2_norm,
                              m_ffn2_w_gate, m_ffn2_w_up, m_ffn2_w_down, m_mem_norm, m_w_mem_kv, m_a_w_in,
                              m_a_v_norm, m_a_w_spatial, m_a_b_spatial, m_a_w_out, m_kv_norm, m_w_kv, m_b_w_in,
                              m_b_w_out, m_final_norm]))
    v_in = dict(zip(W_NAMES, [v_ffn1_norm, v_ffn1_w_gate, v_ffn1_w_up, v_ffn1_w_down, v_mix_norm, v_ffn2_norm,
                              v_ffn2_w_gate, v_ffn2_w_up, v_ffn2_w_down, v_mem_norm, v_w_mem_kv, v_a_w_in,
                              v_a_v_norm, v_a_w_spatial, v_a_b_spatial, v_a_w_out, v_kv_norm, v_w_kv, v_b_w_in,
                              v_b_w_out, v_final_norm]))

    full = gather_weights({n: weights[n] for n in SHARD_AXIS})
    W = {n: (full[n] if n in SHARD_AXIS else weights[n]) for n in W_NAMES}
    loss, dx, grads = local_step(x[0], mem[0], W, loss_target[0])
    total = reduce_grads(grads, {n: weights[n].shape for n in SHARD_AXIS})
    loss = lax.psum(loss[0, 0], ("x", "y", "c"))

    deltas, new_m, new_v = {}, {}, {}
    for n in W_NAMES:
        deltas[n], new_m[n], new_v[n] = adamw(weights[n], total[n], m_in[n], v_in[n], "adamw_" + n)
    return (loss, dx[None], *[total[n] for n in W_NAMES], *[deltas[n] for n in W_NAMES],
            *[new_m[n] for n in W_NAMES], *[new_v[n] for n in W_NAMES])
```

```python
import functools
import math

import jax
import jax.numpy as jnp
from jax import lax
from jax.experimental import pallas as pl
from jax.experimental.pallas import tpu as pltpu

F32 = jnp.float32
BF16 = jnp.bfloat16
EPS = 1e-6
LANES = 128
ROW_TILE = 512
COL_TILE = 1408
PACK_ROWS = 1024
PART_ROWS = 16
VMEM_LIMIT = 56 * 1024 * 1024

W_NAMES = ['ffn1_norm', 'ffn1_w_gate', 'ffn1_w_up', 'ffn1_w_down', 'mix_norm', 'ffn2_norm', 'ffn2_w_gate',
           'ffn2_w_up', 'ffn2_w_down', 'mem_norm', 'w_mem_kv', 'a_w_in', 'a_v_norm', 'a_w_spatial',
           'a_b_spatial', 'a_w_out', 'kv_norm', 'w_kv', 'b_w_in', 'b_w_out', 'final_norm']
SHARD_AXIS = {'ffn1_w_gate': 2, 'ffn1_w_up': 2, 'ffn1_w_down': 1, 'ffn2_w_gate': 2, 'ffn2_w_up': 2,
              'ffn2_w_down': 1, 'w_mem_kv': 1, 'a_w_in': 2, 'a_v_norm': 1, 'a_w_out': 1, 'w_kv': 1,
              'b_w_in': 1, 'b_w_out': 1}
N_CHIPS = 4
PACK_W = 1024

MEM_HEADS = 4
MEM_W = 256
HEAD_DIM = 64
GM_W = 768
GM_GROUPS = 6
GM_CHUNK = 128
CHUNK = 64
SB_BLOCK = 128
SB_Q_CHUNK = 1024
SB_DEAD = -110.0
SB_PAIRS = 2
SB_COLS = SB_PAIRS * LANES

ADAM_LR, ADAM_B1, ADAM_B2, ADAM_EPS, ADAM_WD, ADAM_STEP = 0.001, 0.9, 0.999, 1e-08, 0.01, 10

MESH = pl.DeviceIdType.MESH


def _cparams(sem=None):
    return pltpu.CompilerParams(dimension_semantics=sem, vmem_limit_bytes=VMEM_LIMIT)


def _tile(n, target, mult=LANES):
    best = None
    for t in range(mult, min(n, target) + 1, mult):
        if n % t == 0:
            best = t
    return best if best is not None else n


def _dot(a, b):
    return jnp.dot(a, b, preferred_element_type=F32)


def _dot_nt(a, b):
    return lax.dot_general(a, b, (((1,), (1,)), ((), ())), preferred_element_type=F32)


def _dot_tn(a, b):
    return lax.dot_general(a, b, (((0,), (0,)), ((), ())), preferred_element_type=F32)


def _bf(v):
    return v.astype(BF16)


def rms_fwd(x, gain, name):
    S, D = x.shape
    tm = _tile(S, ROW_TILE, 8)

    def body(x_ref, g_ref, o_ref):
        xf = x_ref[...]
        r = lax.rsqrt(jnp.mean(xf * xf, axis=-1, keepdims=True) + EPS)
        o_ref[...] = ((xf * r) * g_ref[...]).astype(o_ref.dtype)

    return pl.pallas_call(
        body, grid=(S // tm,),
        in_specs=[pl.BlockSpec((tm, D), lambda i: (i, 0)), pl.BlockSpec((1, D), lambda i: (0, 0))],
        out_specs=pl.BlockSpec((tm, D), lambda i: (i, 0)),
        out_shape=jax.ShapeDtypeStruct((S, D), BF16),
        compiler_params=_cparams(("parallel",)), name=name)(x, gain.reshape(1, D))


def rms_bwd(x, gain, dh, dres, name):
    S, D = x.shape
    tm = _tile(S, ROW_TILE, 8)
    has_res = dres is not None

    def body(*refs):
        if has_res:
            x_ref, g_ref, dh_ref, dres_ref, dx_ref, dg_ref = refs
        else:
            x_ref, g_ref, dh_ref, dx_ref, dg_ref = refs
        xf = x_ref[...]
        r = lax.rsqrt(jnp.mean(xf * xf, axis=-1, keepdims=True) + EPS)
        xhat = xf * r
        dhv = dh_ref[...].astype(F32)
        dxhat = dhv * g_ref[...]
        m = jnp.mean(dxhat * xhat, axis=-1, keepdims=True)
        dx = r * (dxhat - xhat * m)
        if has_res:
            dx = dx + dres_ref[...]
        dx_ref[...] = dx

        @pl.when(pl.program_id(0) == 0)
        def _():
            dg_ref[...] = jnp.zeros_like(dg_ref)

        dg_ref[...] += jnp.sum(dhv * xhat, axis=0, keepdims=True)

    row = pl.BlockSpec((tm, D), lambda i: (i, 0))
    vec = pl.BlockSpec((1, D), lambda i: (0, 0))
    ins = [x, gain.reshape(1, D), dh] + ([dres] if has_res else [])
    in_specs = [row, vec, row] + ([row] if has_res else [])
    dx, dg = pl.pallas_call(
        body, grid=(S // tm,), in_specs=in_specs, out_specs=[row, vec],
        out_shape=[jax.ShapeDtypeStruct((S, D), F32), jax.ShapeDtypeStruct((1, D), F32)],
        compiler_params=_cparams(("arbitrary",)), name=name)(*ins)
    return dx, dg


def final_loss(x, gain, target, name):
    S, D = x.shape
    tm = _tile(S, ROW_TILE, 8)

    def body(x_ref, g_ref, t_ref, loss_ref, dx_ref, dg_ref):
        xf = x_ref[...]
        r = lax.rsqrt(jnp.mean(xf * xf, axis=-1, keepdims=True) + EPS)
        xhat = xf * r
        g = g_ref[...]
        diff = xhat * g - t_ref[...]
        dy = diff * (1.0 / D)
        dxhat = dy * g
        m = jnp.mean(dxhat * xhat, axis=-1, keepdims=True)
        dx_ref[...] = r * (dxhat - xhat * m)

        @pl.when(pl.program_id(0) == 0)
        def _():
            dg_ref[...] = jnp.zeros_like(dg_ref)
            loss_ref[...] = jnp.zeros_like(loss_ref)

        dg_ref[...] += jnp.sum(dy * xhat, axis=0, keepdims=True)
        per_tok = jnp.sum(diff * diff, axis=-1, keepdims=True) * (0.5 / D)
        loss_ref[...] += jnp.sum(per_tok, axis=0, keepdims=True)

    row = pl.BlockSpec((tm, D), lambda i: (i, 0))
    vec = pl.BlockSpec((1, D), lambda i: (0, 0))
    one = pl.BlockSpec((1, 1), lambda i: (0, 0))
    return pl.pallas_call(
        body, grid=(S // tm,), in_specs=[row, vec, row], out_specs=[one, row, vec],
        out_shape=[jax.ShapeDtypeStruct((1, 1), F32), jax.ShapeDtypeStruct((S, D), F32),
                   jax.ShapeDtypeStruct((1, D), F32)],
        compiler_params=_cparams(("arbitrary",)), name=name)(x, gain.reshape(1, D), target)


def _wspec(w, blk, idx):
    _, l = w
    if l is None:
        return pl.BlockSpec(blk, idx)
    return pl.BlockSpec((None,) + blk, lambda *g: (l,) + idx(*g))


def mm(pairs, *, tb=False, res=None, scale=1.0, out_dtype=F32, name):
    M = pairs[0][0].shape[0]
    N = pairs[0][2][0].shape[-2 if tb else -1]
    tm = _tile(M, ROW_TILE, 8)
    tn = _tile(N, COL_TILE)
    n_p = len(pairs)
    has_res = res is not None

    def body(*refs):
        o_ref = refs[-1]
        acc = None
        for p in range(n_p):
            a = _bf(refs[2 * p][...])
            b = _bf(refs[2 * p + 1][...])
            d = _dot_nt(a, b) if tb else _dot(a, b)
            acc = d if acc is None else acc + d
        if scale != 1.0:
            acc = acc * scale
        if has_res:
            acc = acc + refs[2 * n_p][...]
        o_ref[...] = acc.astype(o_ref.dtype)

    ins, in_specs = [], []
    for a, cb, w, K, kb in pairs:
        ins += [a, w[0]]
        in_specs.append(pl.BlockSpec((tm, K), functools.partial(lambda i, j, cb: (i, cb), cb=cb)))
        if tb:
            in_specs.append(_wspec(w, (tn, K), functools.partial(lambda i, j, kb: (j, kb), kb=kb)))
        else:
            in_specs.append(_wspec(w, (K, tn), functools.partial(lambda i, j, kb: (kb, j), kb=kb)))
    if has_res:
        ins.append(res)
        in_specs.append(pl.BlockSpec((tm, tn), lambda i, j: (i, j)))
    return pl.pallas_call(
        body, grid=(M // tm, N // tn), in_specs=in_specs,
        out_specs=pl.BlockSpec((tm, tn), lambda i, j: (i, j)),
        out_shape=jax.ShapeDtypeStruct((M, N), out_dtype),
        compiler_params=_cparams(("parallel", "parallel")), name=name)(*ins)


def mm_tn(a, b, *, a_cb=0, a_w=None, b_cb=0, b_w=None, scale=1.0, name):
    S = a.shape[0]
    Ka = a.shape[1] if a_w is None else a_w
    Nb = b.shape[1] if b_w is None else b_w
    ts = _tile(S, ROW_TILE, 8)
    tk = _tile(Ka, COL_TILE)
    tn = _tile(Nb, COL_TILE)
    a_off, b_off = a_cb * (Ka // tk), b_cb * (Nb // tn)

    def body(a_ref, b_ref, o_ref):
        s = pl.program_id(2)

        @pl.when(s == 0)
        def _():
            o_ref[...] = jnp.zeros_like(o_ref)

        o_ref[...] += _dot_tn(_bf(a_ref[...]), _bf(b_ref[...]))
        if scale != 1.0:
            @pl.when(s == pl.num_programs(2) - 1)
            def _():
                o_ref[...] = o_ref[...] * scale

    return pl.pallas_call(
        body, grid=(Ka // tk, Nb // tn, S // ts),
        in_specs=[pl.BlockSpec((ts, tk), lambda k, n, s: (s, a_off + k)),
                  pl.BlockSpec((ts, tn), lambda k, n, s: (s, b_off + n))],
        out_specs=pl.BlockSpec((tk, tn), lambda k, n, s: (k, n)),
        out_shape=jax.ShapeDtypeStruct((Ka, Nb), F32),
        compiler_params=_cparams(("parallel", "parallel", "arbitrary")), name=name)(a, b)


def ffn_up(h, wg, wu, name):
    S, D = h.shape
    Fd = wg[0].shape[-1]
    tm = _tile(S, ROW_TILE, 8)
    tn = _tile(Fd, COL_TILE)

    def body(h_ref, wg_ref, wu_ref, g_ref, u_ref, a_ref):
        hv = h_ref[...]
        g = _dot(hv, wg_ref[...])
        u = _dot(hv, wu_ref[...])
        g_ref[...] = _bf(g)
        u_ref[...] = _bf(u)
        a_ref[...] = _bf(g * jax.nn.sigmoid(g) * u)

    blk = pl.BlockSpec((tm, tn), lambda i, j: (i, j))
    sh = jax.ShapeDtypeStruct((S, Fd), BF16)
    return pl.pallas_call(
        body, grid=(S // tm, Fd // tn),
        in_specs=[pl.BlockSpec((tm, D), lambda i, j: (i, 0)), _wspec(wg, (D, tn), lambda i, j: (0, j)),
                  _wspec(wu, (D, tn), lambda i, j: (0, j))],
        out_specs=[blk, blk, blk], out_shape=[sh, sh, sh],
        compiler_params=_cparams(("parallel", "parallel")), name=name)(h, wg[0], wu[0])


def ffn_down_bwd(dxo, wd, g, u, name):
    S, D = dxo.shape
    Fd = wd[0].shape[-2]
    tm = _tile(S, ROW_TILE, 8)
    tn = _tile(Fd, COL_TILE)

    def body(dx_ref, wd_ref, g_ref, u_ref, dg_ref, du_ref):
        da = _dot_nt(_bf(dx_ref[...]), wd_ref[...]) * 0.5
        g = g_ref[...].astype(F32)
        u = u_ref[...].astype(F32)
        sg = jax.nn.sigmoid(g)
        dg_ref[...] = _bf(da * u * (sg * (1.0 + g * (1.0 - sg))))
        du_ref[...] = _bf(da * (g * sg))

    blk = pl.BlockSpec((tm, tn), lambda i, j: (i, j))
    sh = jax.ShapeDtypeStruct((S, Fd), BF16)
    return pl.pallas_call(
        body, grid=(S // tm, Fd // tn),
        in_specs=[pl.BlockSpec((tm, D), lambda i, j: (i, 0)), _wspec(wd, (tn, D), lambda i, j: (j, 0)),
                  blk, blk],
        out_specs=[blk, blk], out_shape=[sh, sh],
        compiler_params=_cparams(("parallel", "parallel")), name=name)(dxo, wd[0], g, u)


_GELU_C = math.sqrt(2.0 / math.pi)


def _gelu(p):
    return 0.5 * p * (1.0 + jnp.tanh(_GELU_C * (p + 0.044715 * (p * p * p))))


def _gelu_grad(p):
    th = jnp.tanh(_GELU_C * (p + 0.044715 * (p * p * p)))
    return 0.5 * (1.0 + th) + 0.5 * p * (1.0 - th * th) * (_GELU_C * (1.0 + 3.0 * 0.044715 * (p * p)))


def _chunk_mask():
    t = lax.broadcasted_iota(jnp.int32, (GM_CHUNK, GM_CHUNK), 0)
    s = lax.broadcasted_iota(jnp.int32, (GM_CHUNK, GM_CHUNK), 1)
    return (s // CHUNK) <= (t // CHUNK)


def gmlp_fwd(proj, v_gain, w_s, b_mat, name):
    S = proj.shape[0]
    tw = _tile(S, ROW_TILE, GM_CHUNK)
    n_win = tw // GM_CHUNK

    def body(p_ref, gain_ref, w_ref, b_ref, y_ref):
        mask = _chunk_mask()
        v = _gelu(p_ref[:, GM_W:])
        r = lax.rsqrt(jnp.mean(v * v, axis=-1, keepdims=True) + EPS)
        vn = _bf(v * r * gain_ref[...])
        for g in range(GM_GROUPS):
            wm = _bf(jnp.where(mask, w_ref[g], 0.0))
            cs = slice(g * GM_CHUNK, (g + 1) * GM_CHUNK)
            for w in range(n_win):
                rs = slice(w * GM_CHUNK, (w + 1) * GM_CHUNK)
                mixed = _dot(wm, vn[rs, cs]) + b_ref[g]
                y_ref[rs, cs] = _bf(_gelu(p_ref[rs, cs]) * mixed)

    return pl.pallas_call(
        body, grid=(S // tw,),
        in_specs=[pl.BlockSpec((tw, 2 * GM_W), lambda i: (i, 0)), pl.BlockSpec((1, GM_W), lambda i: (0, 0)),
                  pl.BlockSpec((GM_GROUPS, GM_CHUNK, GM_CHUNK), lambda i: (0, 0, 0)),
                  pl.BlockSpec((GM_GROUPS, GM_CHUNK, GM_CHUNK), lambda i: (0, 0, 0))],
        out_specs=pl.BlockSpec((tw, GM_W), lambda i: (i, 0)),
        out_shape=jax.ShapeDtypeStruct((S, GM_W), BF16),
        compiler_params=_cparams(("parallel",)), name=name)(proj, v_gain.reshape(1, GM_W), w_s, b_mat)


def gmlp_bwd(proj, dy, v_gain, w_s, b_mat, name):
    S = proj.shape[0]
    tw = _tile(S, ROW_TILE, GM_CHUNK)
    n_win = tw // GM_CHUNK

    def body(p_ref, dy_ref, gain_ref, w_ref, b_ref, dp_ref, dw_ref, db_ref, dgain_ref, dvn_ref):
        step = pl.program_id(0)

        @pl.when(step == 0)
        def _():
            dw_ref[...] = jnp.zeros_like(dw_ref)
            db_ref[...] = jnp.zeros_like(db_ref)
            dgain_ref[...] = jnp.zeros_like(dgain_ref)

        mask = _chunk_mask()
        pv = p_ref[:, GM_W:]
        v = _gelu(pv)
        r = lax.rsqrt(jnp.mean(v * v, axis=-1, keepdims=True) + EPS)
        vhat = v * r
        gain = gain_ref[...]
        vn = _bf(vhat * gain)
        for g in range(GM_GROUPS):
            wm = _bf(jnp.where(mask, w_ref[g], 0.0))
            cs = slice(g * GM_CHUNK, (g + 1) * GM_CHUNK)
            dw_acc = jnp.zeros((GM_CHUNK, GM_CHUNK), F32)
            db_acc = jnp.zeros((GM_CHUNK, GM_CHUNK), F32)
            for w in range(n_win):
                rs = slice(w * GM_CHUNK, (w + 1) * GM_CHUNK)
                pu = p_ref[rs, cs]
                u = _gelu(pu)
                vn_blk = vn[rs, cs]
                mixed = _dot(wm, vn_blk) + b_ref[g]
                dyb = dy_ref[rs, cs]
                dp_ref[rs, cs] = _bf(dyb * mixed * _gelu_grad(pu))
                dmix = dyb * u
                db_acc = db_acc + dmix
                dmix_b = _bf(dmix)
                dw_acc = dw_acc + _dot_nt(dmix_b, vn_blk)
                dvn_ref[rs, cs] = _dot_tn(wm, dmix_b)
            dw_ref[g] += jnp.where(mask, dw_acc, 0.0)
            db_ref[g] += jnp.broadcast_to(jnp.sum(db_acc, axis=-1, keepdims=True), (GM_CHUNK, GM_CHUNK))
        dvn = dvn_ref[...]
        dgain_ref[...] += jnp.sum(dvn * vhat, axis=0, keepdims=True)
        dvhat = dvn * gain
        m = jnp.mean(dvhat * vhat, axis=-1, keepdims=True)
        dv = r * (dvhat - vhat * m)
        dp_ref[:, GM_W:] = _bf(dv * _gelu_grad(pv))

    sq = pl.BlockSpec((GM_GROUPS, GM_CHUNK, GM_CHUNK), lambda i: (0, 0, 0))
    sq_sh = jax.ShapeDtypeStruct((GM_GROUPS, GM_CHUNK, GM_CHUNK), F32)
    return pl.pallas_call(
        body, grid=(S // tw,),
        in_specs=[pl.BlockSpec((tw, 2 * GM_W), lambda i: (i, 0)), pl.BlockSpec((tw, GM_W), lambda i: (i, 0)),
                  pl.BlockSpec((1, GM_W), lambda i: (0, 0)), sq, sq],
        out_specs=[pl.BlockSpec((tw, 2 * GM_W), lambda i: (i, 0)), sq, sq,
                   pl.BlockSpec((1, GM_W), lambda i: (0, 0))],
        out_shape=[jax.ShapeDtypeStruct((S, 2 * GM_W), BF16), sq_sh, sq_sh,
                   jax.ShapeDtypeStruct((1, GM_W), F32)],
        scratch_shapes=[pltpu.VMEM((tw, GM_W), F32)],
        compiler_params=_cparams(("arbitrary",)), name=name)(proj, dy, v_gain.reshape(1, GM_W), w_s, b_mat)


def _head_mask(h, width):
    lane = lax.broadcasted_iota(jnp.int32, (1, width), 1)
    return (lane >= HEAD_DIM * h) & (lane < HEAD_DIM * (h + 1))


def _mem_probs(q, k, h):
    kh = jnp.where(_head_mask(h, MEM_W), k, jnp.zeros_like(k))
    s = _dot_nt(q, kh) * (HEAD_DIM ** -0.5)
    s = s - jnp.max(s, axis=-1, keepdims=True)
    e = jnp.exp(s)
    return e / jnp.sum(e, axis=-1, keepdims=True), kh


def memattn_fwd(proj, q_cb, mem_kv, name):
    S = proj.shape[0]
    NM = mem_kv.shape[0]
    tm = _tile(S, ROW_TILE, 8)

    def body(q_ref, kv_ref, o_ref):
        q = _bf(q_ref[...])
        k = _bf(kv_ref[:, :MEM_W])
        v = _bf(kv_ref[:, MEM_W:])
        acc = jnp.zeros((tm, MEM_W), F32)
        for h in range(MEM_HEADS):
            p, _ = _mem_probs(q, k, h)
            vh = jnp.where(_head_mask(h, MEM_W), v, jnp.zeros_like(v))
            acc = acc + _dot(_bf(p), vh)
        o_ref[...] = _bf(acc)

    return pl.pallas_call(
        body, grid=(S // tm,),
        in_specs=[pl.BlockSpec((tm, MEM_W), lambda i: (i, q_cb)), pl.BlockSpec((NM, 2 * MEM_W), lambda i: (0, 0))],
        out_specs=pl.BlockSpec((tm, MEM_W), lambda i: (i, 0)),
        out_shape=jax.ShapeDtypeStruct((S, MEM_W), BF16),
        compiler_params=_cparams(("parallel",)), name=name)(proj, mem_kv)


def memattn_bwd(proj, q_cb, mem_kv, dy, dy_cb, name):
    S = proj.shape[0]
    NM = mem_kv.shape[0]
    tm = _tile(S, ROW_TILE, 8)

    def body(q_ref, kv_ref, do_ref, dq_ref, dkv_ref):
        @pl.when(pl.program_id(0) == 0)
        def _():
            dkv_ref[...] = jnp.zeros_like(dkv_ref)

        q = _bf(q_ref[...])
        k = _bf(kv_ref[:, :MEM_W])
        v = _bf(kv_ref[:, MEM_W:])
        do = _bf(do_ref[...])
        dq = jnp.zeros((tm, MEM_W), F32)
        dk = jnp.zeros((NM, MEM_W), F32)
        dv = jnp.zeros((NM, MEM_W), F32)
        for h in range(MEM_HEADS):
            hm = _head_mask(h, MEM_W)
            p, kh = _mem_probs(q, k, h)
            vh = jnp.where(hm, v, jnp.zeros_like(v))
            dp = _dot_nt(do, vh)
            ds = _bf(p * (dp - jnp.sum(p * dp, axis=-1, keepdims=True)) * (HEAD_DIM ** -0.5))
            dq = dq + _dot(ds, kh)
            dk = dk + jnp.where(hm, _dot_tn(ds, q), 0.0)
            dv = dv + jnp.where(hm, _dot_tn(_bf(p), do), 0.0)
        dq_ref[...] = _bf(dq)
        dkv_ref[:, :MEM_W] += dk
        dkv_ref[:, MEM_W:] += dv

    return pl.pallas_call(
        body, grid=(S // tm,),
        in_specs=[pl.BlockSpec((tm, MEM_W), lambda i: (i, q_cb)), pl.BlockSpec((NM, 2 * MEM_W), lambda i: (0, 0)),
                  pl.BlockSpec((tm, MEM_W), lambda i: (i, dy_cb))],
        out_specs=[pl.BlockSpec((tm, MEM_W), lambda i: (i, 0)), pl.BlockSpec((NM, 2 * MEM_W), lambda i: (0, 0))],
        out_shape=[jax.ShapeDtypeStruct((S, MEM_W), BF16), jax.ShapeDtypeStruct((NM, 2 * MEM_W), F32)],
        compiler_params=_cparams(("arbitrary",)), name=name)(proj, mem_kv, dy)


def _split_bf(v):
    hi = _bf(v)
    return hi, _bf(v - hi.astype(F32))


def _sb_weights(qh, k2, run, dpos, row, col, tri):
    z = _dot_nt(qh, k2) * (HEAD_DIM ** -0.5)
    t = jnp.log(1.0 + jnp.exp(-jnp.abs(z)))
    lb = jnp.minimum(z, 0.0) - t
    mask = (col - row) < dpos
    lom = jnp.where(mask, -jnp.maximum(z, 0.0) - t, 0.0)
    l_hi, l_lo = _split_bf(lom)
    insuf = _dot(l_hi, tri) + _dot(l_lo, tri)
    a = jnp.where(mask, jnp.exp(lb + run + insuf), 0.0)
    return a, lb, lom, mask


def _sb_alive(runs):
    top = runs[0]
    for r in runs[1:]:
        top = jnp.maximum(top, r)
    return jnp.max(top) > SB_DEAD


def _sb_consts():
    row = lax.broadcasted_iota(jnp.int32, (SB_BLOCK, SB_BLOCK), 0)
    col = lax.broadcasted_iota(jnp.int32, (SB_BLOCK, SB_BLOCK), 1)
    return row, col


def sb_fwd(proj, kv, name):
    S = proj.shape[0]
    n_g = GM_W // SB_COLS
    tq = _tile(S, SB_Q_CHUNK, SB_BLOCK)
    nqb = tq // SB_BLOCK

    def body(q_ref, k_ref, v_ref, o_ref):
        chunk = pl.program_id(1)
        row, col = _sb_consts()
        tri = _bf(jnp.where(row > col, 1.0, 0.0))
        masks = [_head_mask(h, LANES) for h in range(2)]
        pairs = [slice(p * LANES, (p + 1) * LANES) for p in range(SB_PAIRS)]

        def q_loop(qi, _):
            i = chunk * nqb + qi
            rows = pl.ds(pl.multiple_of(qi * SB_BLOCK, SB_BLOCK), SB_BLOCK)
            qhs = [[_bf(jnp.where(hm, q_ref[rows, cs], 0.0)) for hm in masks] for cs in pairs]

            def k_step(carry):
                t, accs, runs, _ = carry
                keys = pl.ds(pl.multiple_of((i - t) * SB_BLOCK, SB_BLOCK), SB_BLOCK)
                new_accs, new_runs = [], []
                for p, cs in enumerate(pairs):
                    k2, v2 = k_ref[keys, cs], v_ref[keys, cs]
                    acc = accs[p]
                    for h in range(2):
                        a, _, lom, _ = _sb_weights(qhs[p][h], k2, runs[2 * p + h], t * SB_BLOCK, row, col, tri)
                        vh = jnp.where(masks[h], v2, jnp.zeros_like(v2))
                        a_hi, a_lo = _split_bf(a)
                        acc = acc + _dot(a_hi, vh) + _dot(a_lo, vh)
                        new_runs.append(runs[2 * p + h] + jnp.sum(lom, axis=1, keepdims=True))
                    new_accs.append(acc)
                return t + 1, tuple(new_accs), tuple(new_runs), _sb_alive(new_runs)

            zero = jnp.zeros((SB_BLOCK, 1), F32)
            _, accs, _, _ = lax.while_loop(
                lambda carry: jnp.logical_and(carry[0] <= i, carry[3]), k_step,
                (jnp.int32(0), (jnp.zeros((SB_BLOCK, LANES), F32),) * SB_PAIRS, (zero,) * (2 * SB_PAIRS),
                 jnp.bool_(True)))
            for p, cs in enumerate(pairs):
                o_ref[rows, cs] = accs[p]
            return 0

        lax.fori_loop(0, nqb, q_loop, 0)

    return pl.pallas_call(
        body, grid=(n_g, S // tq),
        in_specs=[pl.BlockSpec((tq, SB_COLS), lambda g, c: (c, g)),
                  pl.BlockSpec((S, SB_COLS), lambda g, c: (0, g)),
                  pl.BlockSpec((S, SB_COLS), lambda g, c: (0, n_g + g))],
        out_specs=pl.BlockSpec((tq, SB_COLS), lambda g, c: (c, g)),
        out_shape=jax.ShapeDtypeStruct((S, GM_W), F32),
        compiler_params=_cparams(("parallel", "parallel")), name=name)(proj, kv, kv)


def sb_bwd(proj, kv, dy, out, dk_init, dv_init, name):
    S = proj.shape[0]
    n_g = GM_W // SB_COLS
    tq = _tile(S, SB_Q_CHUNK, SB_BLOCK)
    nqb = tq // SB_BLOCK
    n_chunks = S // tq
    has_init = dk_init is not None
    scale = HEAD_DIM ** -0.5

    def body(*refs):
        if has_init:
            q_ref, k_ref, v_ref, do_ref, out_ref, dki_ref, dvi_ref, dq_ref, dko_ref, dvo_ref, dk_acc, dv_acc = refs
        else:
            q_ref, k_ref, v_ref, do_ref, out_ref, dq_ref, dko_ref, dvo_ref, dk_acc, dv_acc = refs
        chunk = pl.program_id(1)
        cols = pl.ds(pl.multiple_of(pl.program_id(0) * SB_COLS, SB_COLS), SB_COLS)

        @pl.when(chunk == 0)
        def _():
            if has_init:
                pltpu.sync_copy(dki_ref.at[:, cols], dk_acc)
                pltpu.sync_copy(dvi_ref.at[:, cols], dv_acc)
            else:
                dk_acc[...] = jnp.zeros_like(dk_acc)
                dv_acc[...] = jnp.zeros_like(dv_acc)

        row, col = _sb_consts()
        tri = _bf(jnp.where(row > col, 1.0, 0.0))
        tri_inc = _bf(jnp.where(row >= col, 1.0, 0.0))
        masks = [_head_mask(h, LANES) for h in range(2)]
        pairs = [slice(p * LANES, (p + 1) * LANES) for p in range(SB_PAIRS)]

        def q_loop(qi, _):
            i = chunk * nqb + qi
            rows = pl.ds(pl.multiple_of(qi * SB_BLOCK, SB_BLOCK), SB_BLOCK)
            qhs, dohs, e_tots = [], [], []
            for cs in pairs:
                q2, do2, out2 = q_ref[rows, cs], do_ref[rows, cs], out_ref[rows, cs]
                qhs.append([_bf(jnp.where(hm, q2, 0.0)) for hm in masks])
                dohs.append([_bf(jnp.where(hm, do2, 0.0)) for hm in masks])
                e_tots.append([jnp.sum(doh.astype(F32) * out2, axis=1, keepdims=True) for doh in dohs[-1]])

            def k_step(carry):
                t, dqs, runs, e_runs, _ = carry
                keys = pl.ds(pl.multiple_of((i - t) * SB_BLOCK, SB_BLOCK), SB_BLOCK)
                new_dqs, new_runs, new_e_runs = [], [], []
                for p, cs in enumerate(pairs):
                    k2, v2 = k_ref[keys, cs], v_ref[keys, cs]
                    dq = dqs[p]
                    dk_blk = jnp.zeros((SB_BLOCK, LANES), F32)
                    dv_blk = jnp.zeros((SB_BLOCK, LANES), F32)
                    for h in range(2):
                        c = 2 * p + h
                        a, lb, lom, mask = _sb_weights(qhs[p][h], k2, runs[c], t * SB_BLOCK, row, col, tri)
                        vh = jnp.where(masks[h], v2, jnp.zeros_like(v2))
                        e = a * _dot_nt(dohs[p][h], vh)
                        e_hi, e_lo = _split_bf(e)
                        before = e_tots[p][h] - e_runs[c] - (_dot(e_hi, tri_inc) + _dot(e_lo, tri_inc))
                        beta = jnp.exp(lb)
                        dz = _bf(jnp.where(mask, e * (1.0 - beta) - before * beta, 0.0) * scale)
                        kh = jnp.where(masks[h], k2, jnp.zeros_like(k2))
                        dq = dq + _dot(dz, kh)
                        dk_blk = dk_blk + _dot_tn(dz, qhs[p][h])
                        dv_blk = dv_blk + _dot_tn(_bf(a), dohs[p][h])
                        new_runs.append(runs[c] + jnp.sum(lom, axis=1, keepdims=True))
                        new_e_runs.append(e_runs[c] + jnp.sum(e, axis=1, keepdims=True))
                    dk_acc[keys, cs] += dk_blk
                    dv_acc[keys, cs] += dv_blk
                    new_dqs.append(dq)
                return t + 1, tuple(new_dqs), tuple(new_runs), tuple(new_e_runs), _sb_alive(new_runs)

            zero = jnp.zeros((SB_BLOCK, 1), F32)
            zeros = (zero,) * (2 * SB_PAIRS)
            _, dqs, _, _, _ = lax.while_loop(
                lambda carry: jnp.logical_and(carry[0] <= i, carry[4]), k_step,
                (jnp.int32(0), (jnp.zeros((SB_BLOCK, LANES), F32),) * SB_PAIRS, zeros, zeros, jnp.bool_(True)))
            for p, cs in enumerate(pairs):
                dq_ref[rows, cs] = dqs[p]
            return 0

        lax.fori_loop(0, nqb, q_loop, 0)

        @pl.when(chunk == n_chunks - 1)
        def _():
            pltpu.sync_copy(dk_acc, dko_ref.at[:, cols])
            pltpu.sync_copy(dv_acc, dvo_ref.at[:, cols])

    qspec = pl.BlockSpec((tq, SB_COLS), lambda g, c: (c, g))
    kspec = pl.BlockSpec((S, SB_COLS), lambda g, c: (0, g))
    in_specs = [qspec, kspec, pl.BlockSpec((S, SB_COLS), lambda g, c: (0, n_g + g)), qspec, qspec]
    ins = [proj, kv, kv, dy, out]
    if has_init:
        in_specs += [_ANY, _ANY]
        ins += [dk_init, dv_init]
    sh = jax.ShapeDtypeStruct((S, GM_W), F32)
    return pl.pallas_call(
        body, grid=(n_g, n_chunks), in_specs=in_specs, out_specs=[qspec, _ANY, _ANY],
        out_shape=[sh, sh, sh],
        scratch_shapes=[pltpu.VMEM((S, SB_COLS), F32), pltpu.VMEM((S, SB_COLS), F32)],
        compiler_params=_cparams(("arbitrary", "arbitrary")), name=name)(*ins)


def adamw(w, g, m, v, name):
    shape = w.shape
    C = shape[-1] if w.ndim > 1 else shape[0]
    R = w.size // C
    tr = _tile(R, ROW_TILE, 8)

    def body(w_ref, g_ref, m_ref, v_ref, d_ref, nm_ref, nv_ref):
        gv = g_ref[...]
        m2 = ADAM_B1 * m_ref[...] + (1.0 - ADAM_B1) * gv
        v2 = ADAM_B2 * v_ref[...] + (1.0 - ADAM_B2) * (gv * gv)
        m_hat = m2 / (1.0 - ADAM_B1 ** ADAM_STEP)
        v_hat = v2 / (1.0 - ADAM_B2 ** ADAM_STEP)
        d_ref[...] = -ADAM_LR * (m_hat / (jnp.sqrt(v_hat) + ADAM_EPS) + ADAM_WD * w_ref[...])
        nm_ref[...] = m2
        nv_ref[...] = v2

    blk = pl.BlockSpec((tr, C), lambda i: (i, 0))
    sh = jax.ShapeDtypeStruct((R, C), F32)
    outs = pl.pallas_call(
        body, grid=(R // tr,), in_specs=[blk] * 4, out_specs=[blk] * 3, out_shape=[sh] * 3,
        compiler_params=_cparams(("parallel",)), name=name)(
            w.reshape(R, C), g.reshape(R, C), m.reshape(R, C), v.reshape(R, C))
    return tuple(o.reshape(shape) for o in outs)


def _place():
    return lax.axis_index("x"), lax.axis_index("y"), lax.axis_index("c")


def _other_chips(x, y):
    return [(1 - x, y), (x, 1 - y), (1 - x, 1 - y)]


_ANY = pl.BlockSpec(memory_space=pl.ANY)
LOCAL_CHUNKS = 8


def _local_copies(src_of, dst_of, n_rows, sems, src_base=0, dst_base=0):
    rows = n_rows // LOCAL_CHUNKS
    copies = []
    for k in range(LOCAL_CHUNKS):
        cp = pltpu.make_async_copy(src_of(pl.ds(src_base + k * rows, rows)),
                                   dst_of(pl.ds(dst_base + k * rows, rows)), sems.at[k])
        cp.start()
        copies.append(cp)
    return copies


def all_gather_chips(buf, name):
    P, Wd = buf.shape
    Ph = P // 2

    def body(in_ref, out_ref, send_sems, recv_sems, local_sem):
        x, y, c = _place()
        sibling = (x, y, 1 - c)
        chips = _other_chips(x, y)

        def half(px, py, pc):
            return out_ref.at[2 * px + py, pl.ds(pc * Ph, Ph), :]

        def copy(k, src, dst, to):
            return pltpu.make_async_remote_copy(src_ref=src, dst_ref=dst, send_sem=send_sems.at[k],
                                                recv_sem=recv_sems.at[k], device_id=to, device_id_type=MESH)

        mine = _local_copies(lambda r: in_ref.at[r, :], lambda r: out_ref.at[2 * x + y, r, :], P, local_sem)
        my_half = in_ref.at[pl.ds(c * Ph, Ph), :]
        first = [copy(j, my_half, half(x, y, c), (*chip, c)) for j, chip in enumerate(chips)]
        for cp in first:
            cp.start()
        passed = [copy(3 + j, half(*chip, c), half(*chip, c), sibling) for j, chip in enumerate(chips)]
        for j, chip in enumerate(chips):
            copy(j, my_half, half(*chip, c), (*chip, c)).wait_recv()
            passed[j].start()
        for j, chip in enumerate(chips):
            copy(3 + j, my_half, half(*chip, 1 - c), sibling).wait_recv()
        for cp in first + passed:
            cp.wait_send()
        for cp in mine:
            cp.wait()

    return pl.pallas_call(
        body, in_specs=[_ANY], out_specs=_ANY,
        out_shape=jax.ShapeDtypeStruct((N_CHIPS, P, Wd), buf.dtype),
        scratch_shapes=[pltpu.SemaphoreType.DMA((6,)), pltpu.SemaphoreType.DMA((6,)),
                        pltpu.SemaphoreType.DMA((LOCAL_CHUNKS,))],
        name=name)(buf)


def swap_halves(g, name):
    n, P, Wd = g.shape
    Ph = P // 2

    def body(g_ref, theirs_ref, send_sems, recv_sems):
        x, y, c = _place()
        gives = [pltpu.make_async_remote_copy(
            src_ref=g_ref.at[k, pl.ds((1 - c) * Ph, Ph), :], dst_ref=theirs_ref.at[k], send_sem=send_sems.at[k],
            recv_sem=recv_sems.at[k], device_id=(x, y, 1 - c), device_id_type=MESH) for k in range(n)]
        for cp in gives:
            cp.start()
        for cp in gives:
            cp.wait()

    return pl.pallas_call(
        body, in_specs=[_ANY], out_specs=_ANY, out_shape=jax.ShapeDtypeStruct((n, Ph, Wd), g.dtype),
        scratch_shapes=[pltpu.SemaphoreType.DMA((n,)), pltpu.SemaphoreType.DMA((n,))],
        name=name)(g)


def _core_index():
    return lax.axis_index("c").astype(jnp.int32).reshape(1)


def add_cores(g, theirs, name):
    n, Ph, Wd = theirs.shape
    tr = _tile(Ph, ROW_TILE, 16)
    steps = Ph // tr

    def body(c_ref, g_ref, t_ref, o_ref):
        o_ref[...] = _bf(g_ref[...] + t_ref[...])

    blk = pl.BlockSpec((None, tr, Wd), lambda k, i, c_ref: (k, i, 0))
    return pl.pallas_call(
        body,
        grid_spec=pltpu.PrefetchScalarGridSpec(
            num_scalar_prefetch=1, grid=(n, steps),
            in_specs=[pl.BlockSpec((None, tr, Wd), lambda k, i, c_ref: (k, c_ref[0] * steps + i, 0)), blk],
            out_specs=blk),
        out_shape=jax.ShapeDtypeStruct((n, Ph, Wd), BF16),
        compiler_params=_cparams(("parallel", "parallel")), name=name)(_core_index(), g, theirs)


def add_chips(slots, name):
    n, Ph, Wd = slots.shape
    tr = _tile(Ph, ROW_TILE, 16)
    steps = Ph // tr

    def body(c_ref, *refs):
        acc = refs[0][...].astype(F32)
        for k in range(1, n):
            acc = acc + refs[k][...].astype(F32)
        refs[n][...] = acc

    in_specs = [pl.BlockSpec((None, tr, Wd), functools.partial(lambda i, c_ref, k: (k, i, 0), k=k)) for k in range(n)]
    return pl.pallas_call(
        body,
        grid_spec=pltpu.PrefetchScalarGridSpec(
            num_scalar_prefetch=1, grid=(steps,), in_specs=in_specs,
            out_specs=pl.BlockSpec((tr, Wd), lambda i, c_ref: (c_ref[0] * steps + i, 0))),
        out_shape=jax.ShapeDtypeStruct((2 * Ph, Wd), F32),
        compiler_params=_cparams(("parallel",)), name=name)(_core_index(), *([slots] * n))


def scatter_to_chips(h, name):
    n, Ph, Wd = h.shape

    def body(h_ref, out_ref, send_sems, recv_sems, local_sem):
        x, y, c = _place()
        me = 2 * x + y
        chips = _other_chips(x, y)
        keep = _local_copies(lambda r: h_ref.at[me, r, :], lambda r: out_ref.at[me, r, :], Ph, local_sem)
        sends = [pltpu.make_async_remote_copy(
            src_ref=h_ref.at[2 * cx + cy], dst_ref=out_ref.at[me], send_sem=send_sems.at[j],
            recv_sem=recv_sems.at[j], device_id=(cx, cy, c), device_id_type=MESH)
            for j, (cx, cy) in enumerate(chips)]
        for cp in sends:
            cp.start()
        for j, (cx, cy) in enumerate(chips):
            pltpu.make_async_remote_copy(
                src_ref=h_ref.at[me], dst_ref=out_ref.at[2 * cx + cy], send_sem=send_sems.at[j],
                recv_sem=recv_sems.at[j], device_id=(cx, cy, c), device_id_type=MESH).wait_recv()
        for cp in sends:
            cp.wait_send()
        for cp in keep:
            cp.wait()

    return pl.pallas_call(
        body, in_specs=[_ANY], out_specs=_ANY, out_shape=jax.ShapeDtypeStruct((n, Ph, Wd), h.dtype),
        scratch_shapes=[pltpu.SemaphoreType.DMA((3,)), pltpu.SemaphoreType.DMA((3,)),
                        pltpu.SemaphoreType.DMA((LOCAL_CHUNKS,))],
        name=name)(h)


def join_halves(t, name):
    P, Wd = t.shape
    Ph = P // 2
    n = 4
    rows = Ph // n

    def body(t_ref, out_ref, send_sems, recv_sems):
        x, y, c = _place()
        gives = []
        for k in range(n):
            part = out_ref.at[pl.ds(c * Ph + k * rows, rows), :]
            gives.append(pltpu.make_async_remote_copy(
                src_ref=part, dst_ref=part, send_sem=send_sems.at[k], recv_sem=recv_sems.at[k],
                device_id=(x, y, 1 - c), device_id_type=MESH))
        for cp in gives:
            cp.start()
        for k in range(n):
            landed = out_ref.at[pl.ds((1 - c) * Ph + k * rows, rows), :]
            pltpu.make_async_remote_copy(
                src_ref=landed, dst_ref=landed, send_sem=send_sems.at[k], recv_sem=recv_sems.at[k],
                device_id=(x, y, 1 - c), device_id_type=MESH).wait_recv()
        for cp in gives:
            cp.wait_send()

    return pl.pallas_call(
        body, in_specs=[_ANY], out_specs=_ANY, out_shape=jax.ShapeDtypeStruct((P, Wd), t.dtype),
        scratch_shapes=[pltpu.SemaphoreType.DMA((n,)), pltpu.SemaphoreType.DMA((n,))],
        input_output_aliases={0: 0}, name=name)(t)


def _pad_rows(flat, mult_rows=1):
    n = flat.shape[0]
    rows = -(-n // PACK_W)
    rows = -(-rows // mult_rows) * mult_rows
    return jnp.pad(flat, (0, rows * PACK_W - n)).reshape(rows, PACK_W)


def _pack(parts, row_mult):
    blocks = [_pad_rows(p.reshape(-1), PART_ROWS) for p in parts]
    buf = jnp.concatenate(blocks, axis=0)
    rows = buf.shape[0]
    total = -(-rows // row_mult) * row_mult
    return jnp.pad(buf, ((0, total - rows), (0, 0)))


def _unpack(buf, shapes):
    outs, r = [], 0
    lead = buf.shape[:-2]
    for shp in shapes:
        n = math.prod(shp)
        rows = -(-n // PACK_W)
        blk = buf[..., r:r + rows, :].reshape(lead + (rows * PACK_W,))[..., :n]
        outs.append(blk.reshape(lead + tuple(shp)))
        r += -(-rows // PART_ROWS) * PART_ROWS
    return outs


def gather_weights(shards):
    names = [n for n in W_NAMES if n in SHARD_AXIS]
    parts = []
    for n in names:
        w = shards[n]
        if n == 'a_v_norm':
            parts.append(lax.bitcast_convert_type(w, BF16))
        else:
            parts.append(w.astype(BF16))
    buf = _pack(parts, PACK_ROWS)
    full = all_gather_chips(buf, "all_gather_weights")
    got = _unpack(full, [p.shape for p in parts])
    out = {}
    for n, g in zip(names, got):
        if n == 'a_v_norm':
            g = lax.bitcast_convert_type(g, F32)
        out[n] = jnp.concatenate([g[k] for k in range(N_CHIPS)], axis=SHARD_AXIS[n])
    return out


def reduce_grads(grads, shard_shapes):
    sharded = [n for n in W_NAMES if n in SHARD_AXIS]
    repl = [n for n in W_NAMES if n not in SHARD_AXIS]
    blocks = []
    for k in range(N_CHIPS):
        parts = [jnp.split(grads[n], N_CHIPS, axis=SHARD_AXIS[n])[k] for n in sharded]
        parts += [grads[n] for n in repl]
        blocks.append(_pack(parts, PACK_ROWS))
    g = jnp.stack(blocks, axis=0)
    theirs = swap_halves(g, "grads_swap_halves")
    chip_sum = add_cores(g, theirs, "grads_add_cores")
    slots = scatter_to_chips(chip_sum, "grads_scatter")
    total = join_halves(add_chips(slots, "grads_add_chips"), "grads_join_halves")
    shapes = [shard_shapes[n] for n in sharded] + [grads[n].shape for n in repl]
    return dict(zip(sharded + repl, _unpack(total, shapes)))


def _ffn_fwd(x, gain, wg, wu, wd, tag):
    h = rms_fwd(x, gain, tag + "_norm")
    g, u, a = ffn_up(h, wg, wu, tag + "_up")
    xo = mm([(a, 0, wd, a.shape[1], 0)], res=x, scale=0.5, name=tag + "_down")
    return xo, (x, h, g, u, a)


def _ffn_bwd(dxo, saved, gain, wg, wu, wd, tag):
    x, h, g, u, a = saved
    Fd = a.shape[1]
    dgp, du = ffn_down_bwd(dxo, wd, g, u, tag + "_down_bwd")
    d_wd = mm_tn(a, dxo, scale=0.5, name=tag + "_dwd")
    d_wg = mm_tn(h, dgp, name=tag + "_dwg")
    d_wu = mm_tn(h, du, name=tag + "_dwu")
    dh = mm([(dgp, 0, wg, Fd, 0), (du, 0, wu, Fd, 0)], tb=True, name=tag + "_up_bwd")
    dx, d_gain = rms_bwd(x, gain, dh, dxo, tag + "_norm_bwd")
    return dx, d_gain, d_wg, d_wu, d_wd


def local_step(x, mem, W, target):
    D = x.shape[1]
    depth = W['ffn1_norm'].shape[0]
    n_a = W['a_w_in'].shape[0]
    G = {}
    mem_h = rms_fwd(mem, W['mem_norm'], "mem_norm")
    b_mats = jnp.broadcast_to(W['a_b_spatial'][..., None], W['a_b_spatial'].shape + (GM_CHUNK,))
    w_kv = (W['w_kv'], None)

    def ffn_w(which, l):
        return [(W[f'{which}_w_{part}'], l) for part in ('gate', 'up', 'down')]

    def mixer_w(l):
        if l < n_a:
            return (W['a_w_in'], l), (W['a_w_out'], l), 2 * GM_W
        return (W['b_w_in'], l - n_a), (W['b_w_out'], l - n_a), GM_W

    saved = []
    kv = kvn = x_kv = None
    for l in range(depth):
        if l == n_a:
            x_kv = x
            kvn = rms_fwd(x, W['kv_norm'], "kv_norm")
            kv = mm([(kvn, 0, w_kv, D, 0)], out_dtype=BF16, name="kv_proj")
        x, s1 = _ffn_fwd(x, W['ffn1_norm'][l], *ffn_w('ffn1', l), f"l{l}_ffn1")
        hm = rms_fwd(x, W['mix_norm'][l], f"l{l}_mix_norm")
        mem_kv = mm([(mem_h, 0, (W['w_mem_kv'], l), D, 0)], name=f"l{l}_mem_kv")
        w_in, w_out, tok_w = mixer_w(l)
        proj = mm([(hm, 0, w_in, D, 0)], name=f"l{l}_mix_in")
        if l < n_a:
            y_tok = gmlp_fwd(proj, W['a_v_norm'][l], W['a_w_spatial'][l], b_mats[l], f"l{l}_gmlp")
        else:
            y_tok = sb_fwd(proj, kv, f"l{l}_sb")
        q_cb = tok_w // MEM_W
        y_mem = memattn_fwd(proj, q_cb, mem_kv, f"l{l}_memattn")
        x_mid = x
        x = mm([(y_tok, 0, w_out, GM_W, 0), (y_mem, 0, w_out, MEM_W, GM_W // MEM_W)], res=x, name=f"l{l}_mix_out")
        sm = (x_mid, hm, mem_kv, proj, y_tok, y_mem, q_cb)
        x, s2 = _ffn_fwd(x, W['ffn2_norm'][l], *ffn_w('ffn2', l), f"l{l}_ffn2")
        saved.append((s1, sm, s2))

    loss, dx, d_final = final_loss(x, W['final_norm'], target, "loss_head")
    G['final_norm'] = d_final.reshape(-1)

    per_layer = {n: [None] * depth for n in ['ffn1_norm', 'ffn1_w_gate', 'ffn1_w_up', 'ffn1_w_down', 'mix_norm',
                                             'ffn2_norm', 'ffn2_w_gate', 'ffn2_w_up', 'ffn2_w_down', 'w_mem_kv']}
    per_a = {n: [None] * n_a for n in ['a_w_in', 'a_v_norm', 'a_w_spatial', 'a_b_spatial', 'a_w_out']}
    per_b = {n: [None] * (depth - n_a) for n in ['b_w_in', 'b_w_out']}
    d_mem_h = None
    dk = dv = None
    for l in reversed(range(depth)):
        s1, sm, s2 = saved[l]
        dx, dg, dwg, dwu, dwd = _ffn_bwd(dx, s2, W['ffn2_norm'][l], *ffn_w('ffn2', l), f"l{l}_ffn2")
        per_layer['ffn2_norm'][l], per_layer['ffn2_w_gate'][l] = dg.reshape(-1), dwg
        per_layer['ffn2_w_up'][l], per_layer['ffn2_w_down'][l] = dwu, dwd

        x_mid, hm, mem_kv, proj, y_tok, y_mem, q_cb = sm
        is_a = l < n_a
        w_in, w_out, tok_w = mixer_w(l)
        dy = mm([(dx, 0, w_out, D, 0)], tb=True, name=f"l{l}_mix_out_bwd")
        d_w_out = jnp.concatenate([mm_tn(y_tok, dx, name=f"l{l}_dwout_tok"),
                                   mm_tn(y_mem, dx, name=f"l{l}_dwout_mem")], axis=0)
        dq_mem, d_mem_kv = memattn_bwd(proj, q_cb, mem_kv, dy, GM_W // MEM_W, f"l{l}_memattn_bwd")
        if is_a:
            d_tok, d_ws, d_bs, d_vg = gmlp_bwd(proj, dy, W['a_v_norm'][l], W['a_w_spatial'][l], b_mats[l],
                                               f"l{l}_gmlp_bwd")
            per_a['a_w_spatial'][l], per_a['a_b_spatial'][l] = d_ws, d_bs[:, :, 0]
            per_a['a_v_norm'][l], per_a['a_w_out'][l] = d_vg.reshape(-1), d_w_out
        else:
            d_tok, dk, dv = sb_bwd(proj, kv, dy, y_tok, dk, dv, f"l{l}_sb_bwd")
            per_b['b_w_out'][l - n_a] = d_w_out
        d_w_in = jnp.concatenate([mm_tn(hm, d_tok, name=f"l{l}_dwin_tok"),
                                  mm_tn(hm, dq_mem, name=f"l{l}_dwin_mem")], axis=1)
        (per_a['a_w_in'] if is_a else per_b['b_w_in'])[l if is_a else l - n_a] = d_w_in
        dh = mm([(d_tok, 0, w_in, tok_w, 0), (dq_mem, 0, w_in, MEM_W, tok_w // MEM_W)], tb=True,
                name=f"l{l}_mix_in_bwd")
        dx, dg = rms_bwd(x_mid, W['mix_norm'][l], dh, dx, f"l{l}_mix_norm_bwd")
        per_layer['mix_norm'][l] = dg.reshape(-1)
        per_layer['w_mem_kv'][l] = mm_tn(mem_h, d_mem_kv, name=f"l{l}_dw_mem_kv")
        d_mem_h = mm([(d_mem_kv, 0, (W['w_mem_kv'], l), 2 * MEM_W, 0)], tb=True, res=d_mem_h,
                     name=f"l{l}_mem_kv_bwd")

        dx, dg, dwg, dwu, dwd = _ffn_bwd(dx, s1, W['ffn1_norm'][l], *ffn_w('ffn1', l), f"l{l}_ffn1")
        per_layer['ffn1_norm'][l], per_layer['ffn1_w_gate'][l] = dg.reshape(-1), dwg
        per_layer['ffn1_w_up'][l], per_layer['ffn1_w_down'][l] = dwu, dwd
        if l == n_a:
            G['w_kv'] = jnp.concatenate([mm_tn(kvn, dk, name="dw_k"), mm_tn(kvn, dv, name="dw_v")], axis=1)
            d_kvn = mm([(dk, 0, w_kv, GM_W, 0), (dv, 0, w_kv, GM_W, 1)], tb=True, name="kv_proj_bwd")
            dx, dg = rms_bwd(x_kv, W['kv_norm'], d_kvn, dx, "kv_norm_bwd")
            G['kv_norm'] = dg.reshape(-1)

    _, dg = rms_bwd(mem, W['mem_norm'], d_mem_h, None, "mem_norm_bwd")
    G['mem_norm'] = dg.reshape(-1)
    for d in (per_layer, per_a, per_b):
        for n, v in d.items():
            G[n] = jnp.stack(v, axis=0)
    return loss, dx, G


def kernel(x, mem, ffn1_norm, ffn1_w_gate, ffn1_w_up, ffn1_w_down, mix_norm, ffn2_norm, ffn2_w_gate, ffn2_w_up, ffn2_w_down, mem_norm, w_mem_kv, a_w_in, a_v_norm, a_w_spatial, a_b_spatial, a_w_out, kv_norm, w_kv, b_w_in, b_w_out, final_norm, loss_target, m_ffn1_norm, m_ffn1_w_gate, m_ffn1_w_up, m_ffn1_w_down, m_mix_norm, m_ffn2_norm, m_ffn2_w_gate, m_ffn2_w_up, m_ffn2_w_down, m_mem_norm, m_w_mem_kv, m_a_w_in, m_a_v_norm, m_a_w_spatial, m_a_b_spatial, m_a_w_out, m_kv_norm, m_w_kv, m_b_w_in, m_b_w_out, m_final_norm, v_ffn1_norm, v_ffn1_w_gate, v_ffn1_w_up, v_ffn1_w_down, v_mix_norm, v_ffn2_norm, v_ffn2_w_gate, v_ffn2_w_up, v_ffn2_w_down, v_mem_norm, v_w_mem_kv, v_a_w_in, v_a_v_norm, v_a_w_spatial, v_a_b_spatial, v_a_w_out, v_kv_norm, v_w_kv, v_b_w_in, v_b_w_out, v_final_norm):
    weights = dict(zip(W_NAMES, [ffn1_norm, ffn1_w_gate, ffn1_w_up, ffn1_w_down, mix_norm, ffn2_norm, ffn2_w_gate,
                                 ffn2_w_up, ffn2_w_down, mem_norm, w_mem_kv, a_w_in, a_v_norm, a_w_spatial,
                                 a_b_spatial, a_w_out, kv_norm, w_kv, b_w_in, b_w_out, final_norm]))
    m_in = dict(zip(W_NAMES, [m_ffn1_norm, m_ffn1_w_gate, m_ffn1_w_up, m_ffn1_w_down, m_mix_norm, m_ffn2_norm,
                              m_ffn2_w_gate, m_ffn2_w_up, m_ffn2_w_down, m_mem_norm, m_w_mem_kv, m_a_w_in,
                              m_a_v_norm, m_a_w_spatial, m_a_b_spatial, m_a_w_out, m_kv_norm, m_w_kv, m_b_w_in,
                              m_b_w_out, m_final_norm]))
    v_in = dict(zip(W_NAMES, [v_ffn1_norm, v_ffn1_w_gate, v_ffn1_w_up, v_ffn1_w_down, v_mix_norm, v_ffn2_norm,
                              v_ffn2_w_gate, v_ffn2_w_up, v_ffn2_w_down, v_mem_norm, v_w_mem_kv, v_a_w_in,
                              v_a_v_norm, v_a_w_spatial, v_a_b_spatial, v_a_w_out, v_kv_norm, v_w_kv, v_b_w_in,
                              v_b_w_out, v_final_norm]))

    full = gather_weights({n: weights[n] for n in SHARD_AXIS})
    W = {n: (full[n] if n in SHARD_AXIS else weights[n]) for n in W_NAMES}
    loss, dx, grads = local_step(x[0], mem[0], W, loss_target[0])
    total = reduce_grads(grads, {n: weights[n].shape for n in SHARD_AXIS})
    loss = lax.psum(loss[0, 0], ("x", "y", "c"))

    deltas, new_m, new_v = {}, {}, {}
    for n in W_NAMES:
        deltas[n], new_m[n], new_v[n] = adamw(weights[n], total[n], m_in[n], v_in[n], "adamw_" + n)
    return (loss, dx[None], *[total[n] for n in W_NAMES], *[deltas[n] for n in W_NAMES],
            *[new_m[n] for n in W_NAMES], *[new_v[n] for n in W_NAMES])
```

```python
import functools
import math

import jax
import jax.numpy as jnp
from jax import lax
from jax.experimental import pallas as pl
from jax.experimental.pallas import tpu as pltpu

F32 = jnp.float32
BF16 = jnp.bfloat16
EPS = 1e-6
LANES = 128
ROW_TILE = 512
COL_TILE = 1408
PACK_ROWS = 256
PART_ROWS = 16
VMEM_LIMIT = 56 * 1024 * 1024

W_NAMES = ['ffn1_norm', 'ffn1_w_gate', 'ffn1_w_up', 'ffn1_w_down', 'mix_norm', 'ffn2_norm', 'ffn2_w_gate',
           'ffn2_w_up', 'ffn2_w_down', 'mem_norm', 'w_mem_kv', 'a_w_in', 'a_v_norm', 'a_w_spatial',
           'a_b_spatial', 'a_w_out', 'kv_norm', 'w_kv', 'b_w_in', 'b_w_out', 'final_norm']
SHARD_AXIS = {'ffn1_w_gate': 2, 'ffn1_w_up': 2, 'ffn1_w_down': 1, 'ffn2_w_gate': 2, 'ffn2_w_up': 2,
              'ffn2_w_down': 1, 'w_mem_kv': 1, 'a_w_in': 2, 'a_v_norm': 1, 'a_w_out': 1, 'w_kv': 1,
              'b_w_in': 1, 'b_w_out': 1}
N_CHIPS = 4
PACK_W = 1024

MEM_HEADS = 4
MEM_W = 256
HEAD_DIM = 64
GM_W = 768
GM_GROUPS = 6
GM_CHUNK = 128
CHUNK = 64
SB_BLOCK = 128
SB_Q_CHUNK = 1024
SB_DEAD = -110.0
SB_PAIRS = 2
SB_COLS = SB_PAIRS * LANES

ADAM_LR, ADAM_B1, ADAM_B2, ADAM_EPS, ADAM_WD, ADAM_STEP = 0.001, 0.9, 0.999, 1e-08, 0.01, 10

MESH = pl.DeviceIdType.MESH


def _cparams(sem=None):
    return pltpu.CompilerParams(dimension_semantics=sem, vmem_limit_bytes=VMEM_LIMIT)


def _tile(n, target, mult=LANES):
    best = None
    for t in range(mult, min(n, target) + 1, mult):
        if n % t == 0:
            best = t
    return best if best is not None else n


def _dot(a, b):
    return jnp.dot(a, b, preferred_element_type=F32)


def _dot_nt(a, b):
    return lax.dot_general(a, b, (((1,), (1,)), ((), ())), preferred_element_type=F32)


def _dot_tn(a, b):
    return lax.dot_general(a, b, (((0,), (0,)), ((), ())), preferred_element_type=F32)


def _bf(v):
    return v.astype(BF16)


def rms_fwd(x, gain, name):
    S, D = x.shape
    tm = _tile(S, ROW_TILE, 8)

    def body(x_ref, g_ref, o_ref):
        xf = x_ref[...]
        r = lax.rsqrt(jnp.mean(xf * xf, axis=-1, keepdims=True) + EPS)
        o_ref[...] = ((xf * r) * g_ref[...]).astype(o_ref.dtype)

    return pl.pallas_call(
        body, grid=(S // tm,),
        in_specs=[pl.BlockSpec((tm, D), lambda i: (i, 0)), pl.BlockSpec((1, D), lambda i: (0, 0))],
        out_specs=pl.BlockSpec((tm, D), lambda i: (i, 0)),
        out_shape=jax.ShapeDtypeStruct((S, D), BF16),
        compiler_params=_cparams(("parallel",)), name=name)(x, gain.reshape(1, D))


def rms_bwd(x, gain, dh, dres, name):
    S, D = x.shape
    tm = _tile(S, ROW_TILE, 16)
    has_res = dres is not None

    def body(*refs):
        if has_res:
            x_ref, g_ref, dh_ref, dres_ref, dx_ref, dxb_ref, dg_ref = refs
        else:
            x_ref, g_ref, dh_ref, dx_ref, dxb_ref, dg_ref = refs
        xf = x_ref[...]
        r = lax.rsqrt(jnp.mean(xf * xf, axis=-1, keepdims=True) + EPS)
        xhat = xf * r
        dhv = dh_ref[...].astype(F32)
        dxhat = dhv * g_ref[...]
        m = jnp.mean(dxhat * xhat, axis=-1, keepdims=True)
        dx = r * (dxhat - xhat * m)
        if has_res:
            dx = dx + dres_ref[...]
        dx_ref[...] = dx
        dxb_ref[...] = _bf(dx)

        @pl.when(pl.program_id(0) == 0)
        def _():
            dg_ref[...] = jnp.zeros_like(dg_ref)

        dg_ref[...] += jnp.sum(dhv * xhat, axis=0, keepdims=True)

    row = pl.BlockSpec((tm, D), lambda i: (i, 0))
    vec = pl.BlockSpec((1, D), lambda i: (0, 0))
    ins = [x, gain.reshape(1, D), dh] + ([dres] if has_res else [])
    in_specs = [row, vec, row] + ([row] if has_res else [])
    dx, dxb, dg = pl.pallas_call(
        body, grid=(S // tm,), in_specs=in_specs, out_specs=[row, row, vec],
        out_shape=[jax.ShapeDtypeStruct((S, D), F32), jax.ShapeDtypeStruct((S, D), BF16),
                   jax.ShapeDtypeStruct((1, D), F32)],
        compiler_params=_cparams(("arbitrary",)), name=name)(*ins)
    return (dx, dxb), dg


def final_loss(x, gain, target, name):
    S, D = x.shape
    tm = _tile(S, ROW_TILE, 16)

    def body(x_ref, g_ref, t_ref, loss_ref, dx_ref, dxb_ref, dg_ref):
        xf = x_ref[...]
        r = lax.rsqrt(jnp.mean(xf * xf, axis=-1, keepdims=True) + EPS)
        xhat = xf * r
        g = g_ref[...]
        diff = xhat * g - t_ref[...]
        dy = diff * (1.0 / D)
        dxhat = dy * g
        m = jnp.mean(dxhat * xhat, axis=-1, keepdims=True)
        dx = r * (dxhat - xhat * m)
        dx_ref[...] = dx
        dxb_ref[...] = _bf(dx)

        @pl.when(pl.program_id(0) == 0)
        def _():
            dg_ref[...] = jnp.zeros_like(dg_ref)
            loss_ref[...] = jnp.zeros_like(loss_ref)

        dg_ref[...] += jnp.sum(dy * xhat, axis=0, keepdims=True)
        per_tok = jnp.sum(diff * diff, axis=-1, keepdims=True) * (0.5 / D)
        loss_ref[...] += jnp.sum(per_tok, axis=0, keepdims=True)

    row = pl.BlockSpec((tm, D), lambda i: (i, 0))
    vec = pl.BlockSpec((1, D), lambda i: (0, 0))
    one = pl.BlockSpec((1, 1), lambda i: (0, 0))
    loss, dx, dxb, dg = pl.pallas_call(
        body, grid=(S // tm,), in_specs=[row, vec, row], out_specs=[one, row, row, vec],
        out_shape=[jax.ShapeDtypeStruct((1, 1), F32), jax.ShapeDtypeStruct((S, D), F32),
                   jax.ShapeDtypeStruct((S, D), BF16), jax.ShapeDtypeStruct((1, D), F32)],
        compiler_params=_cparams(("arbitrary",)), name=name)(x, gain.reshape(1, D), target)
    return loss, (dx, dxb), dg


def _wspec(w, blk, idx):
    _, l = w
    if l is None:
        return pl.BlockSpec(blk, idx)
    return pl.BlockSpec((None,) + blk, lambda *g: (l,) + idx(*g))


def mm(pairs, *, tb=False, res=None, scale=1.0, out_dtype=F32, name):
    M = pairs[0][0].shape[0]
    N = pairs[0][2][0].shape[-2 if tb else -1]
    tm = _tile(M, ROW_TILE, 8)
    tn = _tile(N, COL_TILE)
    n_p = len(pairs)
    has_res = res is not None

    def body(*refs):
        o_ref = refs[-1]
        acc = None
        for p in range(n_p):
            a = _bf(refs[2 * p][...])
            b = _bf(refs[2 * p + 1][...])
            d = _dot_nt(a, b) if tb else _dot(a, b)
            acc = d if acc is None else acc + d
        if scale != 1.0:
            acc = acc * scale
        if has_res:
            acc = acc + refs[2 * n_p][...]
        o_ref[...] = acc.astype(o_ref.dtype)

    ins, in_specs = [], []
    for a, cb, w, K, kb in pairs:
        ins += [a, w[0]]
        in_specs.append(pl.BlockSpec((tm, K), functools.partial(lambda i, j, cb: (i, cb), cb=cb)))
        if tb:
            in_specs.append(_wspec(w, (tn, K), functools.partial(lambda i, j, kb: (j, kb), kb=kb)))
        else:
            in_specs.append(_wspec(w, (K, tn), functools.partial(lambda i, j, kb: (kb, j), kb=kb)))
    if has_res:
        ins.append(res)
        in_specs.append(pl.BlockSpec((tm, tn), lambda i, j: (i, j)))
    return pl.pallas_call(
        body, grid=(M // tm, N // tn), in_specs=in_specs,
        out_specs=pl.BlockSpec((tm, tn), lambda i, j: (i, j)),
        out_shape=jax.ShapeDtypeStruct((M, N), out_dtype),
        compiler_params=_cparams(("parallel", "parallel")), name=name)(*ins)


def mm_tn(a, b, *, a_cb=0, a_w=None, b_cb=0, b_w=None, scale=1.0, name):
    S = a.shape[0]
    Ka = a.shape[1] if a_w is None else a_w
    Nb = b.shape[1] if b_w is None else b_w
    ts = _tile(S, ROW_TILE, 8)
    tk = _tile(Ka, COL_TILE)
    tn = _tile(Nb, COL_TILE)
    a_off, b_off = a_cb * (Ka // tk), b_cb * (Nb // tn)

    def body(a_ref, b_ref, o_ref):
        s = pl.program_id(2)

        @pl.when(s == 0)
        def _():
            o_ref[...] = jnp.zeros_like(o_ref)

        o_ref[...] += _dot_tn(_bf(a_ref[...]), _bf(b_ref[...]))
        if scale != 1.0:
            @pl.when(s == pl.num_programs(2) - 1)
            def _():
                o_ref[...] = o_ref[...] * scale

    return pl.pallas_call(
        body, grid=(Ka // tk, Nb // tn, S // ts),
        in_specs=[pl.BlockSpec((ts, tk), lambda k, n, s: (s, a_off + k)),
                  pl.BlockSpec((ts, tn), lambda k, n, s: (s, b_off + n))],
        out_specs=pl.BlockSpec((tk, tn), lambda k, n, s: (k, n)),
        out_shape=jax.ShapeDtypeStruct((Ka, Nb), F32),
        compiler_params=_cparams(("parallel", "parallel", "arbitrary")), name=name)(a, b)


def _sigmoid(x):
    return 0.5 * jnp.tanh(0.5 * x) + 0.5


def ffn_up(h, wg, wu, name):
    S, D = h.shape
    Fd = wg[0].shape[-1]
    tm = _tile(S, ROW_TILE, 8)
    tn = _tile(Fd, COL_TILE)

    def body(h_ref, wg_ref, wu_ref, g_ref, u_ref, a_ref):
        hv = h_ref[...]
        g = _dot(hv, wg_ref[...])
        u = _dot(hv, wu_ref[...])
        g_ref[...] = _bf(g)
        u_ref[...] = _bf(u)
        a_ref[...] = _bf(g * _sigmoid(g) * u)

    blk = pl.BlockSpec((tm, tn), lambda i, j: (i, j))
    sh = jax.ShapeDtypeStruct((S, Fd), BF16)
    return pl.pallas_call(
        body, grid=(S // tm, Fd // tn),
        in_specs=[pl.BlockSpec((tm, D), lambda i, j: (i, 0)), _wspec(wg, (D, tn), lambda i, j: (0, j)),
                  _wspec(wu, (D, tn), lambda i, j: (0, j))],
        out_specs=[blk, blk, blk], out_shape=[sh, sh, sh],
        compiler_params=_cparams(("parallel", "parallel")), name=name)(h, wg[0], wu[0])


def ffn_down_bwd(dxo, wd, g, u, name):
    S, D = dxo.shape
    Fd = wd[0].shape[-2]
    tm = _tile(S, ROW_TILE, 8)
    tn = _tile(Fd, COL_TILE)

    def body(dx_ref, wd_ref, g_ref, u_ref, dg_ref, du_ref):
        da = _dot_nt(_bf(dx_ref[...]), wd_ref[...]) * 0.5
        g = g_ref[...].astype(F32)
        u = u_ref[...].astype(F32)
        sg = _sigmoid(g)
        dg_ref[...] = _bf(da * u * (sg * (1.0 + g * (1.0 - sg))))
        du_ref[...] = _bf(da * (g * sg))

    blk = pl.BlockSpec((tm, tn), lambda i, j: (i, j))
    sh = jax.ShapeDtypeStruct((S, Fd), BF16)
    return pl.pallas_call(
        body, grid=(S // tm, Fd // tn),
        in_specs=[pl.BlockSpec((tm, D), lambda i, j: (i, 0)), _wspec(wd, (tn, D), lambda i, j: (j, 0)),
                  blk, blk],
        out_specs=[blk, blk], out_shape=[sh, sh],
        compiler_params=_cparams(("parallel", "parallel")), name=name)(dxo, wd[0], g, u)


_GELU_C = math.sqrt(2.0 / math.pi)


def _gelu(p):
    return 0.5 * p * (1.0 + jnp.tanh(_GELU_C * (p + 0.044715 * (p * p * p))))


def _gelu_grad(p):
    th = jnp.tanh(_GELU_C * (p + 0.044715 * (p * p * p)))
    return 0.5 * (1.0 + th) + 0.5 * p * (1.0 - th * th) * (_GELU_C * (1.0 + 3.0 * 0.044715 * (p * p)))


def _chunk_mask():
    t = lax.broadcasted_iota(jnp.int32, (GM_CHUNK, GM_CHUNK), 0)
    s = lax.broadcasted_iota(jnp.int32, (GM_CHUNK, GM_CHUNK), 1)
    return (s // CHUNK) <= (t // CHUNK)


def gmlp_fwd(proj, v_gain, w_s, b_mat, name):
    S = proj.shape[0]
    tw = _tile(S, ROW_TILE, GM_CHUNK)
    n_win = tw // GM_CHUNK

    def body(p_ref, gain_ref, w_ref, b_ref, y_ref):
        mask = _chunk_mask()
        v = _gelu(p_ref[:, GM_W:])
        r = lax.rsqrt(jnp.mean(v * v, axis=-1, keepdims=True) + EPS)
        vn = _bf(v * r * gain_ref[...])
        for g in range(GM_GROUPS):
            wm = _bf(jnp.where(mask, w_ref[g], 0.0))
            cs = slice(g * GM_CHUNK, (g + 1) * GM_CHUNK)
            for w in range(n_win):
                rs = slice(w * GM_CHUNK, (w + 1) * GM_CHUNK)
                mixed = _dot(wm, vn[rs, cs]) + b_ref[g]
                y_ref[rs, cs] = _bf(_gelu(p_ref[rs, cs]) * mixed)

    return pl.pallas_call(
        body, grid=(S // tw,),
        in_specs=[pl.BlockSpec((tw, 2 * GM_W), lambda i: (i, 0)), pl.BlockSpec((1, GM_W), lambda i: (0, 0)),
                  pl.BlockSpec((GM_GROUPS, GM_CHUNK, GM_CHUNK), lambda i: (0, 0, 0)),
                  pl.BlockSpec((GM_GROUPS, GM_CHUNK, GM_CHUNK), lambda i: (0, 0, 0))],
        out_specs=pl.BlockSpec((tw, GM_W), lambda i: (i, 0)),
        out_shape=jax.ShapeDtypeStruct((S, GM_W), BF16),
        compiler_params=_cparams(("parallel",)), name=name)(proj, v_gain.reshape(1, GM_W), w_s, b_mat)


def gmlp_bwd(proj, dy, v_gain, w_s, b_mat, name):
    S = proj.shape[0]
    tw = _tile(S, ROW_TILE, GM_CHUNK)
    n_win = tw // GM_CHUNK

    def body(p_ref, dy_ref, gain_ref, w_ref, b_ref, dp_ref, dw_ref, db_ref, dgain_ref, dvn_ref):
        step = pl.program_id(0)

        @pl.when(step == 0)
        def _():
            dw_ref[...] = jnp.zeros_like(dw_ref)
            db_ref[...] = jnp.zeros_like(db_ref)
            dgain_ref[...] = jnp.zeros_like(dgain_ref)

        mask = _chunk_mask()
        pv = p_ref[:, GM_W:]
        v = _gelu(pv)
        r = lax.rsqrt(jnp.mean(v * v, axis=-1, keepdims=True) + EPS)
        vhat = v * r
        gain = gain_ref[...]
        vn = _bf(vhat * gain)
        for g in range(GM_GROUPS):
            wm = _bf(jnp.where(mask, w_ref[g], 0.0))
            cs = slice(g * GM_CHUNK, (g + 1) * GM_CHUNK)
            dw_acc = jnp.zeros((GM_CHUNK, GM_CHUNK), F32)
            db_acc = jnp.zeros((GM_CHUNK, GM_CHUNK), F32)
            for w in range(n_win):
                rs = slice(w * GM_CHUNK, (w + 1) * GM_CHUNK)
                pu = p_ref[rs, cs]
                u = _gelu(pu)
                vn_blk = vn[rs, cs]
                mixed = _dot(wm, vn_blk) + b_ref[g]
                dyb = dy_ref[rs, cs]
                dp_ref[rs, cs] = _bf(dyb * mixed * _gelu_grad(pu))
                dmix = dyb * u
                db_acc = db_acc + dmix
                dmix_b = _bf(dmix)
                dw_acc = dw_acc + _dot_nt(dmix_b, vn_blk)
                dvn_ref[rs, cs] = _dot_tn(wm, dmix_b)
            dw_ref[g] += jnp.where(mask, dw_acc, 0.0)
            db_ref[g] += jnp.broadcast_to(jnp.sum(db_acc, axis=-1, keepdims=True), (GM_CHUNK, GM_CHUNK))
        dvn = dvn_ref[...]
        dgain_ref[...] += jnp.sum(dvn * vhat, axis=0, keepdims=True)
        dvhat = dvn * gain
        m = jnp.mean(dvhat * vhat, axis=-1, keepdims=True)
        dv = r * (dvhat - vhat * m)
        dp_ref[:, GM_W:] = _bf(dv * _gelu_grad(pv))

    sq = pl.BlockSpec((GM_GROUPS, GM_CHUNK, GM_CHUNK), lambda i: (0, 0, 0))
    sq_sh = jax.ShapeDtypeStruct((GM_GROUPS, GM_CHUNK, GM_CHUNK), F32)
    return pl.pallas_call(
        body, grid=(S // tw,),
        in_specs=[pl.BlockSpec((tw, 2 * GM_W), lambda i: (i, 0)), pl.BlockSpec((tw, GM_W), lambda i: (i, 0)),
                  pl.BlockSpec((1, GM_W), lambda i: (0, 0)), sq, sq],
        out_specs=[pl.BlockSpec((tw, 2 * GM_W), lambda i: (i, 0)), sq, sq,
                   pl.BlockSpec((1, GM_W), lambda i: (0, 0))],
        out_shape=[jax.ShapeDtypeStruct((S, 2 * GM_W), BF16), sq_sh, sq_sh,
                   jax.ShapeDtypeStruct((1, GM_W), F32)],
        scratch_shapes=[pltpu.VMEM((tw, GM_W), F32)],
        compiler_params=_cparams(("arbitrary",)), name=name)(proj, dy, v_gain.reshape(1, GM_W), w_s, b_mat)


def _head_mask(h, width):
    lane = lax.broadcasted_iota(jnp.int32, (1, width), 1)
    return (lane >= HEAD_DIM * h) & (lane < HEAD_DIM * (h + 1))


def _mem_probs(q, k, h):
    kh = jnp.where(_head_mask(h, MEM_W), k, jnp.zeros_like(k))
    s = _dot_nt(q, kh) * (HEAD_DIM ** -0.5)
    s = s - jnp.max(s, axis=-1, keepdims=True)
    e = jnp.exp(s)
    return e * (1.0 / jnp.sum(e, axis=-1, keepdims=True)), kh


def memattn_fwd(proj, q_cb, mem_kv, name):
    S = proj.shape[0]
    NM = mem_kv.shape[0]
    tm = _tile(S, ROW_TILE, 8)

    def body(q_ref, kv_ref, o_ref):
        q = _bf(q_ref[...])
        k = _bf(kv_ref[:, :MEM_W])
        v = _bf(kv_ref[:, MEM_W:])
        acc = jnp.zeros((tm, MEM_W), F32)
        for h in range(MEM_HEADS):
            p, _ = _mem_probs(q, k, h)
            vh = jnp.where(_head_mask(h, MEM_W), v, jnp.zeros_like(v))
            acc = acc + _dot(_bf(p), vh)
        o_ref[...] = _bf(acc)

    return pl.pallas_call(
        body, grid=(S // tm,),
        in_specs=[pl.BlockSpec((tm, MEM_W), lambda i: (i, q_cb)), pl.BlockSpec((NM, 2 * MEM_W), lambda i: (0, 0))],
        out_specs=pl.BlockSpec((tm, MEM_W), lambda i: (i, 0)),
        out_shape=jax.ShapeDtypeStruct((S, MEM_W), BF16),
        compiler_params=_cparams(("parallel",)), name=name)(proj, mem_kv)


def memattn_bwd(proj, q_cb, mem_kv, dy, dy_cb, name):
    S = proj.shape[0]
    NM = mem_kv.shape[0]
    tm = _tile(S, ROW_TILE, 8)

    def body(q_ref, kv_ref, do_ref, dq_ref, dkv_ref):
        @pl.when(pl.program_id(0) == 0)
        def _():
            dkv_ref[...] = jnp.zeros_like(dkv_ref)

        q = _bf(q_ref[...])
        k = _bf(kv_ref[:, :MEM_W])
        v = _bf(kv_ref[:, MEM_W:])
        do = _bf(do_ref[...])
        dq = jnp.zeros((tm, MEM_W), F32)
        dk = jnp.zeros((NM, MEM_W), F32)
        dv = jnp.zeros((NM, MEM_W), F32)
        for h in range(MEM_HEADS):
            hm = _head_mask(h, MEM_W)
            p, kh = _mem_probs(q, k, h)
            vh = jnp.where(hm, v, jnp.zeros_like(v))
            dp = _dot_nt(do, vh)
            ds = _bf(p * (dp - jnp.sum(p * dp, axis=-1, keepdims=True)) * (HEAD_DIM ** -0.5))
            dq = dq + _dot(ds, kh)
            dk = dk + jnp.where(hm, _dot_tn(ds, q), 0.0)
            dv = dv + jnp.where(hm, _dot_tn(_bf(p), do), 0.0)
        dq_ref[...] = _bf(dq)
        dkv_ref[:, :MEM_W] += dk
        dkv_ref[:, MEM_W:] += dv

    return pl.pallas_call(
        body, grid=(S // tm,),
        in_specs=[pl.BlockSpec((tm, MEM_W), lambda i: (i, q_cb)), pl.BlockSpec((NM, 2 * MEM_W), lambda i: (0, 0)),
                  pl.BlockSpec((tm, MEM_W), lambda i: (i, dy_cb))],
        out_specs=[pl.BlockSpec((tm, MEM_W), lambda i: (i, 0)), pl.BlockSpec((NM, 2 * MEM_W), lambda i: (0, 0))],
        out_shape=[jax.ShapeDtypeStruct((S, MEM_W), BF16), jax.ShapeDtypeStruct((NM, 2 * MEM_W), F32)],
        compiler_params=_cparams(("arbitrary",)), name=name)(proj, mem_kv, dy)


def _split_bf(v):
    hi = _bf(v)
    return hi, _bf(v - hi.astype(F32))


def _sb_weights(qh, k2, run, dpos, row, col, tri):
    z = _dot_nt(qh, k2) * (HEAD_DIM ** -0.5)
    t = jnp.log(1.0 + jnp.exp(-jnp.abs(z)))
    lb = jnp.minimum(z, 0.0) - t
    mask = (col - row) < dpos
    lom = jnp.where(mask, -jnp.maximum(z, 0.0) - t, 0.0)
    l_hi, l_lo = _split_bf(lom)
    insuf = _dot(l_hi, tri) + _dot(l_lo, tri)
    a = jnp.where(mask, jnp.exp(lb + run + insuf), 0.0)
    return a, lb, lom, mask


def _sb_alive(runs):
    top = runs[0]
    for r in runs[1:]:
        top = jnp.maximum(top, r)
    return jnp.max(top) > SB_DEAD


def _sb_consts():
    row = lax.broadcasted_iota(jnp.int32, (2 * SB_BLOCK, SB_BLOCK), 0) % SB_BLOCK
    col = lax.broadcasted_iota(jnp.int32, (2 * SB_BLOCK, SB_BLOCK), 1)
    r = lax.broadcasted_iota(jnp.int32, (SB_BLOCK, SB_BLOCK), 0)
    c = lax.broadcasted_iota(jnp.int32, (SB_BLOCK, SB_BLOCK), 1)
    return row, col, _bf(jnp.where(r > c, 1.0, 0.0)), _bf(jnp.where(r >= c, 1.0, 0.0))


def _stack_heads(x2, masks):
    return jnp.concatenate([_bf(jnp.where(hm, x2, jnp.zeros_like(x2))) for hm in masks], axis=0)


def _side_by_side(x):
    return jnp.concatenate([x[:SB_BLOCK], x[SB_BLOCK:]], axis=1)


def sb_fwd(proj, kv, name):
    S = proj.shape[0]
    n_g = GM_W // SB_COLS
    tq = _tile(S, SB_Q_CHUNK, SB_BLOCK)
    nqb = tq // SB_BLOCK

    def body(q_ref, k_ref, v_ref, o_ref):
        chunk = pl.program_id(1)
        row, col, tri, _ = _sb_consts()
        masks = [_head_mask(h, LANES) for h in range(2)]
        pairs = [slice(p * LANES, (p + 1) * LANES) for p in range(SB_PAIRS)]

        def q_loop(qi, _):
            i = chunk * nqb + qi
            rows = pl.ds(pl.multiple_of(qi * SB_BLOCK, SB_BLOCK), SB_BLOCK)
            qps = [_stack_heads(q_ref[rows, cs], masks) for cs in pairs]

            def k_step(carry):
                t, accs, runs, _ = carry
                keys = pl.ds(pl.multiple_of((i - t) * SB_BLOCK, SB_BLOCK), SB_BLOCK)
                new_accs, new_runs = [], []
                for p, cs in enumerate(pairs):
                    a, _, lom, _ = _sb_weights(qps[p], k_ref[keys, cs], runs[p], t * SB_BLOCK, row, col, tri)
                    a_hi, a_lo = _split_bf(_side_by_side(a))
                    vp = _stack_heads(v_ref[keys, cs], masks)
                    new_accs.append(accs[p] + _dot(a_hi, vp) + _dot(a_lo, vp))
                    new_runs.append(runs[p] + jnp.sum(lom, axis=1, keepdims=True))
                return t + 1, tuple(new_accs), tuple(new_runs), _sb_alive(new_runs)

            zero = jnp.zeros((2 * SB_BLOCK, 1), F32)
            _, accs, _, _ = lax.while_loop(
                lambda carry: jnp.logical_and(carry[0] <= i, carry[3]), k_step,
                (jnp.int32(0), (jnp.zeros((SB_BLOCK, LANES), F32),) * SB_PAIRS, (zero,) * SB_PAIRS,
                 jnp.bool_(True)))
            for p, cs in enumerate(pairs):
                o_ref[rows, cs] = accs[p]
            return 0

        lax.fori_loop(0, nqb, q_loop, 0)

    return pl.pallas_call(
        body, grid=(n_g, S // tq),
        in_specs=[pl.BlockSpec((tq, SB_COLS), lambda g, c: (c, g)),
                  pl.BlockSpec((S, SB_COLS), lambda g, c: (0, g)),
                  pl.BlockSpec((S, SB_COLS), lambda g, c: (0, n_g + g))],
        out_specs=pl.BlockSpec((tq, SB_COLS), lambda g, c: (c, g)),
        out_shape=jax.ShapeDtypeStruct((S, GM_W), F32),
        compiler_params=_cparams(("parallel", "parallel")), name=name)(proj, kv, kv)


def sb_bwd(proj, kv, dy, out, dk_init, dv_init, name):
    S = proj.shape[0]
    n_g = GM_W // SB_COLS
    tq = _tile(S, SB_Q_CHUNK, SB_BLOCK)
    nqb = tq // SB_BLOCK
    n_chunks = S // tq
    has_init = dk_init is not None
    scale = HEAD_DIM ** -0.5

    def body(*refs):
        if has_init:
            q_ref, k_ref, v_ref, do_ref, out_ref, dki_ref, dvi_ref, dq_ref, dko_ref, dvo_ref, dk_acc, dv_acc = refs
        else:
            q_ref, k_ref, v_ref, do_ref, out_ref, dq_ref, dko_ref, dvo_ref, dk_acc, dv_acc = refs
        chunk = pl.program_id(1)
        cols = pl.ds(pl.multiple_of(pl.program_id(0) * SB_COLS, SB_COLS), SB_COLS)

        @pl.when(chunk == 0)
        def _():
            if has_init:
                pltpu.sync_copy(dki_ref.at[:, cols], dk_acc)
                pltpu.sync_copy(dvi_ref.at[:, cols], dv_acc)
            else:
                dk_acc[...] = jnp.zeros_like(dk_acc)
                dv_acc[...] = jnp.zeros_like(dv_acc)

        row, col, tri, tri_inc = _sb_consts()
        masks = [_head_mask(h, LANES) for h in range(2)]
        pairs = [slice(p * LANES, (p + 1) * LANES) for p in range(SB_PAIRS)]

        def q_loop(qi, _):
            i = chunk * nqb + qi
            rows = pl.ds(pl.multiple_of(qi * SB_BLOCK, SB_BLOCK), SB_BLOCK)
            qps = [_stack_heads(q_ref[rows, cs], masks) for cs in pairs]
            dops = [_stack_heads(do_ref[rows, cs], masks) for cs in pairs]
            e_tots = [jnp.sum(dop.astype(F32) * jnp.concatenate([out_ref[rows, cs]] * 2, axis=0), axis=1,
                              keepdims=True) for dop, cs in zip(dops, pairs)]

            def k_step(carry):
                t, dqs, runs, e_runs, _ = carry
                keys = pl.ds(pl.multiple_of((i - t) * SB_BLOCK, SB_BLOCK), SB_BLOCK)
                new_dqs, new_runs, new_e_runs = [], [], []
                for p, cs in enumerate(pairs):
                    k2, v2 = k_ref[keys, cs], v_ref[keys, cs]
                    a, lb, lom, mask = _sb_weights(qps[p], k2, runs[p], t * SB_BLOCK, row, col, tri)
                    e = a * _dot_nt(dops[p], v2)
                    e_hi, e_lo = _split_bf(e)
                    before = e_tots[p] - e_runs[p] - (_dot(e_hi, tri_inc) + _dot(e_lo, tri_inc))
                    beta = jnp.exp(lb)
                    dz = _bf(jnp.where(mask, e * (1.0 - beta) - before * beta, 0.0) * scale)
                    new_dqs.append(dqs[p] + _dot(_side_by_side(dz), _stack_heads(k2, masks)))
                    dk_acc[keys, cs] += _dot_tn(dz, qps[p])
                    dv_acc[keys, cs] += _dot_tn(_bf(a), dops[p])
                    new_runs.append(runs[p] + jnp.sum(lom, axis=1, keepdims=True))
                    new_e_runs.append(e_runs[p] + jnp.sum(e, axis=1, keepdims=True))
                return t + 1, tuple(new_dqs), tuple(new_runs), tuple(new_e_runs), _sb_alive(new_runs)

            zero = jnp.zeros((2 * SB_BLOCK, 1), F32)
            zeros = (zero,) * SB_PAIRS
            _, dqs, _, _, _ = lax.while_loop(
                lambda carry: jnp.logical_and(carry[0] <= i, carry[4]), k_step,
                (jnp.int32(0), (jnp.zeros((SB_BLOCK, LANES), F32),) * SB_PAIRS, zeros, zeros, jnp.bool_(True)))
            for p, cs in enumerate(pairs):
                dq_ref[rows, cs] = dqs[p]
            return 0

        lax.fori_loop(0, nqb, q_loop, 0)

        @pl.when(chunk == n_chunks - 1)
        def _():
            pltpu.sync_copy(dk_acc, dko_ref.at[:, cols])
            pltpu.sync_copy(dv_acc, dvo_ref.at[:, cols])

    qspec = pl.BlockSpec((tq, SB_COLS), lambda g, c: (c, g))
    kspec = pl.BlockSpec((S, SB_COLS), lambda g, c: (0, g))
    in_specs = [qspec, kspec, pl.BlockSpec((S, SB_COLS), lambda g, c: (0, n_g + g)), qspec, qspec]
    ins = [proj, kv, kv, dy, out]
    if has_init:
        in_specs += [_ANY, _ANY]
        ins += [dk_init, dv_init]
    sh = jax.ShapeDtypeStruct((S, GM_W), F32)
    return pl.pallas_call(
        body, grid=(n_g, n_chunks), in_specs=in_specs, out_specs=[qspec, _ANY, _ANY],
        out_shape=[sh, sh, sh],
        scratch_shapes=[pltpu.VMEM((S, SB_COLS), F32), pltpu.VMEM((S, SB_COLS), F32)],
        compiler_params=_cparams(("arbitrary", "arbitrary")), name=name)(*ins)


def adamw(w, g, m, v, name):
    shape = w.shape
    C = shape[-1] if w.ndim > 1 else shape[0]
    R = w.size // C
    tr = _tile(R, ROW_TILE, 8)

    def body(w_ref, g_ref, m_ref, v_ref, d_ref, nm_ref, nv_ref):
        gv = g_ref[...]
        m2 = ADAM_B1 * m_ref[...] + (1.0 - ADAM_B1) * gv
        v2 = ADAM_B2 * v_ref[...] + (1.0 - ADAM_B2) * (gv * gv)
        m_hat = m2 / (1.0 - ADAM_B1 ** ADAM_STEP)
        v_hat = v2 / (1.0 - ADAM_B2 ** ADAM_STEP)
        d_ref[...] = -ADAM_LR * (m_hat / (jnp.sqrt(v_hat) + ADAM_EPS) + ADAM_WD * w_ref[...])
        nm_ref[...] = m2
        nv_ref[...] = v2

    blk = pl.BlockSpec((tr, C), lambda i: (i, 0))
    sh = jax.ShapeDtypeStruct((R, C), F32)
    outs = pl.pallas_call(
        body, grid=(R // tr,), in_specs=[blk] * 4, out_specs=[blk] * 3, out_shape=[sh] * 3,
        compiler_params=_cparams(("parallel",)), name=name)(
            w.reshape(R, C), g.reshape(R, C), m.reshape(R, C), v.reshape(R, C))
    return tuple(o.reshape(shape) for o in outs)


def _place():
    return lax.axis_index("x"), lax.axis_index("y"), lax.axis_index("c")


def _other_chips(x, y):
    return [(1 - x, y), (x, 1 - y), (1 - x, 1 - y)]


_ANY = pl.BlockSpec(memory_space=pl.ANY)
LOCAL_CHUNKS = 8


def _local_copies(src_of, dst_of, n_rows, sems, sem_base):
    rows = n_rows // LOCAL_CHUNKS
    copies = []
    for k in range(LOCAL_CHUNKS):
        r = pl.ds(k * rows, rows)
        cp = pltpu.make_async_copy(src_of(r), dst_of(r), sems.at[sem_base + k])
        cp.start()
        copies.append(cp)
    return copies


def _remote(src, dst, send_sems, recv_sems, k, to):
    return pltpu.make_async_remote_copy(src_ref=src, dst_ref=dst, send_sem=send_sems.at[k], recv_sem=recv_sems.at[k],
                                        device_id=to, device_id_type=MESH)


def all_gather_chips(bufs, name):
    nb = len(bufs)

    def body(*refs):
        in_refs, out_refs = refs[:nb], refs[nb:2 * nb]
        send_sems, recv_sems, local_sems = refs[2 * nb:]
        x, y, c = _place()
        sibling = (x, y, 1 - c)
        chips = _other_chips(x, y)

        def half(b, px, py, pc):
            Ph = bufs[b].shape[0] // 2
            return out_refs[b].at[2 * px + py, pl.ds(pc * Ph, Ph), :]

        def my_half(b):
            Ph = bufs[b].shape[0] // 2
            return in_refs[b].at[pl.ds(c * Ph, Ph), :]

        mine, first, passed = [], [], []
        for b in range(nb):
            mine += _local_copies(lambda r, b=b: in_refs[b].at[r, :], lambda r, b=b: out_refs[b].at[2 * x + y, r, :],
                                  bufs[b].shape[0], local_sems, b * LOCAL_CHUNKS)
            for j, chip in enumerate(chips):
                cp = _remote(my_half(b), half(b, x, y, c), send_sems, recv_sems, 6 * b + j, (*chip, c))
                cp.start()
                first.append(cp)
        for j, chip in enumerate(chips):
            for b in range(nb):
                _remote(my_half(b), half(b, *chip, c), send_sems, recv_sems, 6 * b + j, (*chip, c)).wait_recv()
                cp = _remote(half(b, *chip, c), half(b, *chip, c), send_sems, recv_sems, 6 * b + 3 + j, sibling)
                cp.start()
                passed.append(cp)
        for j, chip in enumerate(chips):
            for b in range(nb):
                _remote(my_half(b), half(b, *chip, 1 - c), send_sems, recv_sems, 6 * b + 3 + j, sibling).wait_recv()
        for cp in first + passed:
            cp.wait_send()
        for cp in mine:
            cp.wait()

    return pl.pallas_call(
        body, in_specs=[_ANY] * nb, out_specs=[_ANY] * nb,
        out_shape=[jax.ShapeDtypeStruct((N_CHIPS,) + b.shape, b.dtype) for b in bufs],
        scratch_shapes=[pltpu.SemaphoreType.DMA((6 * nb,)), pltpu.SemaphoreType.DMA((6 * nb,)),
                        pltpu.SemaphoreType.DMA((LOCAL_CHUNKS * nb,))],
        name=name)(*bufs)


def swap_halves(gs, name):
    nb = len(gs)
    n = N_CHIPS

    def body(*refs):
        g_refs, t_refs = refs[:nb], refs[nb:2 * nb]
        send_sems, recv_sems = refs[2 * nb:]
        x, y, c = _place()
        gives = []
        for b in range(nb):
            Ph = gs[b].shape[1] // 2
            for k in range(n):
                gives.append(_remote(g_refs[b].at[k, pl.ds((1 - c) * Ph, Ph), :], t_refs[b].at[k], send_sems,
                                     recv_sems, n * b + k, (x, y, 1 - c)))
        for cp in gives:
            cp.start()
        for cp in gives:
            cp.wait()

    return pl.pallas_call(
        body, in_specs=[_ANY] * nb, out_specs=[_ANY] * nb,
        out_shape=[jax.ShapeDtypeStruct((n, g.shape[1] // 2, g.shape[2]), g.dtype) for g in gs],
        scratch_shapes=[pltpu.SemaphoreType.DMA((n * nb,)), pltpu.SemaphoreType.DMA((n * nb,))],
        name=name)(*gs)


def _core_index():
    return lax.axis_index("c").astype(jnp.int32).reshape(1)


def add_cores(g, theirs, name):
    n, Ph, Wd = theirs.shape
    tr = _tile(Ph, ROW_TILE, 16)
    steps = Ph // tr

    def body(c_ref, g_ref, t_ref, o_ref):
        o_ref[...] = _bf(g_ref[...] + t_ref[...])

    blk = pl.BlockSpec((None, tr, Wd), lambda k, i, c_ref: (k, i, 0))
    return pl.pallas_call(
        body,
        grid_spec=pltpu.PrefetchScalarGridSpec(
            num_scalar_prefetch=1, grid=(n, steps),
            in_specs=[pl.BlockSpec((None, tr, Wd), lambda k, i, c_ref: (k, c_ref[0] * steps + i, 0)), blk],
            out_specs=blk),
        out_shape=jax.ShapeDtypeStruct((n, Ph, Wd), BF16),
        compiler_params=_cparams(("parallel", "parallel")), name=name)(_core_index(), g, theirs)


def add_chips(slots, name):
    n, Ph, Wd = slots.shape
    tr = _tile(Ph, ROW_TILE, 16)
    steps = Ph // tr

    def body(c_ref, *refs):
        acc = refs[0][...].astype(F32)
        for k in range(1, n):
            acc = acc + refs[k][...].astype(F32)
        refs[n][...] = acc

    in_specs = [pl.BlockSpec((None, tr, Wd), functools.partial(lambda i, c_ref, k: (k, i, 0), k=k)) for k in range(n)]
    return pl.pallas_call(
        body,
        grid_spec=pltpu.PrefetchScalarGridSpec(
            num_scalar_prefetch=1, grid=(steps,), in_specs=in_specs,
            out_specs=pl.BlockSpec((tr, Wd), lambda i, c_ref: (c_ref[0] * steps + i, 0))),
        out_shape=jax.ShapeDtypeStruct((2 * Ph, Wd), F32),
        compiler_params=_cparams(("parallel",)), name=name)(_core_index(), *([slots] * n))


def scatter_to_chips(hs, name):
    nb = len(hs)

    def body(*refs):
        h_refs, out_refs = refs[:nb], refs[nb:2 * nb]
        send_sems, recv_sems, local_sems = refs[2 * nb:]
        x, y, c = _place()
        me = 2 * x + y
        chips = _other_chips(x, y)
        keep, sends = [], []
        for b in range(nb):
            keep += _local_copies(lambda r, b=b: h_refs[b].at[me, r, :], lambda r, b=b: out_refs[b].at[me, r, :],
                                  hs[b].shape[1], local_sems, b * LOCAL_CHUNKS)
            for j, (cx, cy) in enumerate(chips):
                cp = _remote(h_refs[b].at[2 * cx + cy], out_refs[b].at[me], send_sems, recv_sems, 3 * b + j,
                             (cx, cy, c))
                cp.start()
                sends.append(cp)
        for b in range(nb):
            for j, (cx, cy) in enumerate(chips):
                _remote(h_refs[b].at[me], out_refs[b].at[2 * cx + cy], send_sems, recv_sems, 3 * b + j,
                        (cx, cy, c)).wait_recv()
        for cp in sends:
            cp.wait_send()
        for cp in keep:
            cp.wait()

    return pl.pallas_call(
        body, in_specs=[_ANY] * nb, out_specs=[_ANY] * nb,
        out_shape=[jax.ShapeDtypeStruct(h.shape, h.dtype) for h in hs],
        scratch_shapes=[pltpu.SemaphoreType.DMA((3 * nb,)), pltpu.SemaphoreType.DMA((3 * nb,)),
                        pltpu.SemaphoreType.DMA((LOCAL_CHUNKS * nb,))],
        name=name)(*hs)


JOIN_CHUNKS = 4


def join_halves(ts, name):
    nb = len(ts)
    n = JOIN_CHUNKS

    def body(*refs):
        out_refs = refs[nb:2 * nb]
        send_sems, recv_sems = refs[2 * nb:]
        x, y, c = _place()
        gives = []
        for b in range(nb):
            Ph = ts[b].shape[0] // 2
            rows = Ph // n
            for k in range(n):
                part = out_refs[b].at[pl.ds(c * Ph + k * rows, rows), :]
                gives.append(_remote(part, part, send_sems, recv_sems, n * b + k, (x, y, 1 - c)))
        for cp in gives:
            cp.start()
        for b in range(nb):
            Ph = ts[b].shape[0] // 2
            rows = Ph // n
            for k in range(n):
                landed = out_refs[b].at[pl.ds((1 - c) * Ph + k * rows, rows), :]
                _remote(landed, landed, send_sems, recv_sems, n * b + k, (x, y, 1 - c)).wait_recv()
        for cp in gives:
            cp.wait_send()

    return pl.pallas_call(
        body, in_specs=[_ANY] * nb, out_specs=[_ANY] * nb,
        out_shape=[jax.ShapeDtypeStruct(t.shape, t.dtype) for t in ts],
        scratch_shapes=[pltpu.SemaphoreType.DMA((n * nb,)), pltpu.SemaphoreType.DMA((n * nb,))],
        input_output_aliases={b: b for b in range(nb)}, name=name)(*ts)


WIDE = ['ffn1_w_gate', 'ffn1_w_up', 'ffn2_w_gate', 'ffn2_w_up']


def _rows_of(p):
    if p.shape[-1] == PACK_W and (p.size // PACK_W) % PART_ROWS == 0:
        return p.reshape(-1, PACK_W)
    flat = p.reshape(-1)
    rows = -(-flat.shape[0] // (PACK_W * PART_ROWS)) * PART_ROWS
    return jnp.pad(flat, (0, rows * PACK_W - flat.shape[0])).reshape(rows, PACK_W)


def _pack(parts):
    buf = jnp.concatenate([_rows_of(p) for p in parts], axis=0)
    rows = buf.shape[0]
    total = -(-rows // PACK_ROWS) * PACK_ROWS
    return jnp.pad(buf, ((0, total - rows), (0, 0)))


def _unpack(buf, shapes):
    outs, r = [], 0
    lead = buf.shape[:-2]
    for shp in shapes:
        n = math.prod(shp)
        rows = -(-n // (PACK_W * PART_ROWS)) * PART_ROWS
        blk = buf[..., r:r + rows, :]
        if n != rows * PACK_W:
            blk = blk.reshape(lead + (rows * PACK_W,))[..., :n]
        outs.append(blk.reshape(lead + tuple(shp)))
        r += rows
    return outs


def gather_weights(shards):
    narrow = [n for n in W_NAMES if n in SHARD_AXIS and n not in WIDE]
    parts = [lax.bitcast_convert_type(shards[n], BF16) if n == 'a_v_norm' else shards[n].astype(BF16)
             for n in narrow]
    wide = [shards[n].astype(BF16) for n in WIDE]
    cols = wide[0].shape[-1]
    wide_buf = jnp.concatenate([w.reshape(-1, cols) for w in wide], axis=0)
    full_wide, full_narrow = all_gather_chips([wide_buf, _pack(parts)], "all_gather_weights")
    out = {}
    rows = wide[0].size // cols
    for i, n in enumerate(WIDE):
        g = full_wide[:, i * rows:(i + 1) * rows, :].reshape((N_CHIPS,) + wide[i].shape)
        out[n] = jnp.concatenate([g[k] for k in range(N_CHIPS)], axis=SHARD_AXIS[n])
    for n, g in zip(narrow, _unpack(full_narrow, [p.shape for p in parts])):
        if n == 'a_v_norm':
            g = lax.bitcast_convert_type(g, F32)
        out[n] = jnp.concatenate([g[k] for k in range(N_CHIPS)], axis=SHARD_AXIS[n])
    return out


def reduce_grads(grads, shard_shapes):
    narrow = [n for n in W_NAMES if n in SHARD_AXIS and n not in WIDE]
    repl = [n for n in W_NAMES if n not in SHARD_AXIS]
    full = {n: (jnp.stack(g, axis=0) if isinstance(g, list) else g) for n, g in grads.items() if n not in WIDE}
    cols = shard_shapes[WIDE[0]][-1]
    wide_parts = []
    for n in WIDE:
        for g in grads[n]:
            wide_parts.append(g.reshape(g.shape[0], N_CHIPS, cols).transpose(1, 0, 2))
    g_wide = jnp.concatenate(wide_parts, axis=1)
    blocks = []
    for k in range(N_CHIPS):
        parts = [jnp.split(full[n], N_CHIPS, axis=SHARD_AXIS[n])[k] for n in narrow]
        parts += [full[n] for n in repl]
        blocks.append(_pack(parts))
    gs = [g_wide, jnp.stack(blocks, axis=0)]
    theirs = swap_halves(gs, "grads_swap_halves")
    chip_sums = [add_cores(g, t, f"grads_add_cores_{i}") for i, (g, t) in enumerate(zip(gs, theirs))]
    slots = scatter_to_chips(chip_sums, "grads_scatter")
    totals = join_halves([add_chips(s, f"grads_add_chips_{i}") for i, s in enumerate(slots)], "grads_join_halves")
    out = {}
    rows = math.prod(shard_shapes[WIDE[0]][:-1])
    for i, n in enumerate(WIDE):
        out[n] = totals[0][i * rows:(i + 1) * rows, :].reshape(shard_shapes[n])
    shapes = [shard_shapes[n] for n in narrow] + [full[n].shape for n in repl]
    out.update(zip(narrow + repl, _unpack(totals[1], shapes)))
    return out


def _ffn_fwd(x, gain, wg, wu, wd, tag):
    h = rms_fwd(x, gain, tag + "_norm")
    g, u, a = ffn_up(h, wg, wu, tag + "_up")
    xo = mm([(a, 0, wd, a.shape[1], 0)], res=x, scale=0.5, name=tag + "_down")
    return xo, (x, h, g, u, a)


def _ffn_bwd(dxo, saved, gain, wg, wu, wd, tag):
    x, h, g, u, a = saved
    Fd = a.shape[1]
    dxo, dxo_b = dxo
    dgp, du = ffn_down_bwd(dxo_b, wd, g, u, tag + "_down_bwd")
    d_wd = mm_tn(a, dxo_b, scale=0.5, name=tag + "_dwd")
    d_wg = mm_tn(h, dgp, name=tag + "_dwg")
    d_wu = mm_tn(h, du, name=tag + "_dwu")
    dh = mm([(dgp, 0, wg, Fd, 0), (du, 0, wu, Fd, 0)], tb=True, name=tag + "_up_bwd")
    dx, d_gain = rms_bwd(x, gain, dh, dxo, tag + "_norm_bwd")
    return dx, d_gain, d_wg, d_wu, d_wd


def local_step(x, mem, W, target):
    D = x.shape[1]
    depth = W['ffn1_norm'].shape[0]
    n_a = W['a_w_in'].shape[0]
    G = {}
    mem_h = rms_fwd(mem, W['mem_norm'], "mem_norm")
    b_mats = jnp.broadcast_to(W['a_b_spatial'][..., None], W['a_b_spatial'].shape + (GM_CHUNK,))
    w_kv = (W['w_kv'], None)

    def ffn_w(which, l):
        return [(W[f'{which}_w_{part}'], l) for part in ('gate', 'up', 'down')]

    def mixer_w(l):
        if l < n_a:
            return (W['a_w_in'], l), (W['a_w_out'], l), 2 * GM_W
        return (W['b_w_in'], l - n_a), (W['b_w_out'], l - n_a), GM_W

    saved = []
    kv = kvn = x_kv = None
    for l in range(depth):
        if l == n_a:
            x_kv = x
            kvn = rms_fwd(x, W['kv_norm'], "kv_norm")
            kv = mm([(kvn, 0, w_kv, D, 0)], out_dtype=BF16, name="kv_proj")
        x, s1 = _ffn_fwd(x, W['ffn1_norm'][l], *ffn_w('ffn1', l), f"l{l}_ffn1")
        hm = rms_fwd(x, W['mix_norm'][l], f"l{l}_mix_norm")
        mem_kv = mm([(mem_h, 0, (W['w_mem_kv'], l), D, 0)], name=f"l{l}_mem_kv")
        w_in, w_out, tok_w = mixer_w(l)
        proj = mm([(hm, 0, w_in, D, 0)], name=f"l{l}_mix_in")
        if l < n_a:
            y_tok = gmlp_fwd(proj, W['a_v_norm'][l], W['a_w_spatial'][l], b_mats[l], f"l{l}_gmlp")
        else:
            y_tok = sb_fwd(proj, kv, f"l{l}_sb")
        q_cb = tok_w // MEM_W
        y_mem = memattn_fwd(proj, q_cb, mem_kv, f"l{l}_memattn")
        x_mid = x
        x = mm([(y_tok, 0, w_out, GM_W, 0), (y_mem, 0, w_out, MEM_W, GM_W // MEM_W)], res=x, name=f"l{l}_mix_out")
        sm = (x_mid, hm, mem_kv, proj, y_tok, y_mem, q_cb)
        x, s2 = _ffn_fwd(x, W['ffn2_norm'][l], *ffn_w('ffn2', l), f"l{l}_ffn2")
        saved.append((s1, sm, s2))

    loss, dx, d_final = final_loss(x, W['final_norm'], target, "loss_head")
    G['final_norm'] = d_final.reshape(-1)

    per_layer = {n: [None] * depth for n in ['ffn1_norm', 'ffn1_w_gate', 'ffn1_w_up', 'ffn1_w_down', 'mix_norm',
                                             'ffn2_norm', 'ffn2_w_gate', 'ffn2_w_up', 'ffn2_w_down', 'w_mem_kv']}
    per_a = {n: [None] * n_a for n in ['a_w_in', 'a_v_norm', 'a_w_spatial', 'a_b_spatial', 'a_w_out']}
    per_b = {n: [None] * (depth - n_a) for n in ['b_w_in', 'b_w_out']}
    d_mem_h = None
    dk = dv = None
    for l in reversed(range(depth)):
        s1, sm, s2 = saved[l]
        dx, dg, dwg, dwu, dwd = _ffn_bwd(dx, s2, W['ffn2_norm'][l], *ffn_w('ffn2', l), f"l{l}_ffn2")
        per_layer['ffn2_norm'][l], per_layer['ffn2_w_gate'][l] = dg.reshape(-1), dwg
        per_layer['ffn2_w_up'][l], per_layer['ffn2_w_down'][l] = dwu, dwd

        x_mid, hm, mem_kv, proj, y_tok, y_mem, q_cb = sm
        is_a = l < n_a
        w_in, w_out, tok_w = mixer_w(l)
        dy = mm([(dx[1], 0, w_out, D, 0)], tb=True, name=f"l{l}_mix_out_bwd")
        d_w_out = jnp.concatenate([mm_tn(y_tok, dx[1], name=f"l{l}_dwout_tok"),
                                   mm_tn(y_mem, dx[1], name=f"l{l}_dwout_mem")], axis=0)
        dq_mem, d_mem_kv = memattn_bwd(proj, q_cb, mem_kv, dy, GM_W // MEM_W, f"l{l}_memattn_bwd")
        if is_a:
            d_tok, d_ws, d_bs, d_vg = gmlp_bwd(proj, dy, W['a_v_norm'][l], W['a_w_spatial'][l], b_mats[l],
                                               f"l{l}_gmlp_bwd")
            per_a['a_w_spatial'][l], per_a['a_b_spatial'][l] = d_ws, d_bs[:, :, 0]
            per_a['a_v_norm'][l], per_a['a_w_out'][l] = d_vg.reshape(-1), d_w_out
        else:
            d_tok, dk, dv = sb_bwd(proj, kv, dy, y_tok, dk, dv, f"l{l}_sb_bwd")
            per_b['b_w_out'][l - n_a] = d_w_out
        d_w_in = jnp.concatenate([mm_tn(hm, d_tok, name=f"l{l}_dwin_tok"),
                                  mm_tn(hm, dq_mem, name=f"l{l}_dwin_mem")], axis=1)
        (per_a['a_w_in'] if is_a else per_b['b_w_in'])[l if is_a else l - n_a] = d_w_in
        dh = mm([(d_tok, 0, w_in, tok_w, 0), (dq_mem, 0, w_in, MEM_W, tok_w // MEM_W)], tb=True,
                name=f"l{l}_mix_in_bwd")
        dx, dg = rms_bwd(x_mid, W['mix_norm'][l], dh, dx[0], f"l{l}_mix_norm_bwd")
        per_layer['mix_norm'][l] = dg.reshape(-1)
        per_layer['w_mem_kv'][l] = mm_tn(mem_h, d_mem_kv, name=f"l{l}_dw_mem_kv")
        d_mem_h = mm([(d_mem_kv, 0, (W['w_mem_kv'], l), 2 * MEM_W, 0)], tb=True, res=d_mem_h,
                     name=f"l{l}_mem_kv_bwd")

        dx, dg, dwg, dwu, dwd = _ffn_bwd(dx, s1, W['ffn1_norm'][l], *ffn_w('ffn1', l), f"l{l}_ffn1")
        per_layer['ffn1_norm'][l], per_layer['ffn1_w_gate'][l] = dg.reshape(-1), dwg
        per_layer['ffn1_w_up'][l], per_layer['ffn1_w_down'][l] = dwu, dwd
        if l == n_a:
            G['w_kv'] = jnp.concatenate([mm_tn(kvn, dk, name="dw_k"), mm_tn(kvn, dv, name="dw_v")], axis=1)
            d_kvn = mm([(dk, 0, w_kv, GM_W, 0), (dv, 0, w_kv, GM_W, 1)], tb=True, name="kv_proj_bwd")
            dx, dg = rms_bwd(x_kv, W['kv_norm'], d_kvn, dx[0], "kv_norm_bwd")
            G['kv_norm'] = dg.reshape(-1)

    _, dg = rms_bwd(mem, W['mem_norm'], d_mem_h, None, "mem_norm_bwd")
    G['mem_norm'] = dg.reshape(-1)
    for d in (per_layer, per_a, per_b):
        G.update(d)
    return loss, dx[0], G


def kernel(x, mem, ffn1_norm, ffn1_w_gate, ffn1_w_up, ffn1_w_down, mix_norm, ffn2_norm, ffn2_w_gate, ffn2_w_up, ffn2_w_down, mem_norm, w_mem_kv, a_w_in, a_v_norm, a_w_spatial, a_b_spatial, a_w_out, kv_norm, w_kv, b_w_in, b_w_out, final_norm, loss_target, m_ffn1_norm, m_ffn1_w_gate, m_ffn1_w_up, m_ffn1_w_down, m_mix_norm, m_ffn2_norm, m_ffn2_w_gate, m_ffn2_w_up, m_ffn2_w_down, m_mem_norm, m_w_mem_kv, m_a_w_in, m_a_v_norm, m_a_w_spatial, m_a_b_spatial, m_a_w_out, m_kv_norm, m_w_kv, m_b_w_in, m_b_w_out, m_final_norm, v_ffn1_norm, v_ffn1_w_gate, v_ffn1_w_up, v_ffn1_w_down, v_mix_norm, v_ffn2_norm, v_ffn2_w_gate, v_ffn2_w_up, v_ffn2_w_down, v_mem_norm, v_w_mem_kv, v_a_w_in, v_a_v_norm, v_a_w_spatial, v_a_b_spatial, v_a_w_out, v_kv_norm, v_w_kv, v_b_w_in, v_b_w_out, v_final_norm):
    weights = dict(zip(W_NAMES, [ffn1_norm, ffn1_w_gate, ffn1_w_up, ffn1_w_down, mix_norm, ffn2_norm, ffn2_w_gate,
                                 ffn2_w_up, ffn2_w_down, mem_norm, w_mem_kv, a_w_in, a_v_norm, a_w_spatial,
                                 a_b_spatial, a_w_out, kv_norm, w_kv, b_w_in, b_w_out, final_norm]))
    m_in = dict(zip(W_NAMES, [m_ffn1_norm, m_ffn1_w_gate, m_ffn1_w_up, m_ffn1_w_down, m_mix_norm, m_ffn2_norm,
                              m_ffn2_w_gate, m_ffn2_w_up, m_ffn2_w_down, m_mem_norm, m_w_mem_kv, m_a_w_in,
                              m_a_v_norm, m_a_w_spatial, m_a_b_spatial, m_a_w_out, m_kv_norm, m_w_kv, m_b_w_in,
                              m_b_w_out, m_final_norm]))
    v_in = dict(zip(W_NAMES, [v_ffn1_norm, v_ffn1_w_gate, v_ffn1_w_up, v_ffn1_w_down, v_mix_norm, v_ffn2_norm,
                              v_ffn2_w_gate, v_ffn2_w_up, v_ffn2_w_down, v_mem_norm, v_w_mem_kv, v_a_w_in,
                              v_a_v_norm, v_a_w_spatial, v_a_b_spatial, v_a_w_out, v_kv_norm, v_w_kv, v_b_w_in,
                              v_b_w_out, v_final_norm]))

    full = gather_weights({n: weights[n] for n in SHARD_AXIS})
    W = {n: (full[n] if n in SHARD_AXIS else weights[n]) for n in W_NAMES}
    loss, dx, grads = local_step(x[0], mem[0], W, loss_target[0])
    total = reduce_grads(grads, {n: weights[n].shape for n in SHARD_AXIS})
    loss = lax.psum(loss[0, 0], ("x", "y", "c"))

    deltas, new_m, new_v = {}, {}, {}
    for n in W_NAMES:
        deltas[n], new_m[n], new_v[n] = adamw(weights[n], total[n], m_in[n], v_in[n], "adamw_" + n)
    return (loss, dx[None], *[total[n] for n in W_NAMES], *[deltas[n] for n in W_NAMES],
            *[new_m[n] for n in W_NAMES], *[new_v[n] for n in W_NAMES])
```

```python
import functools
import math

import jax
import jax.numpy as jnp
from jax import lax
from jax.experimental import pallas as pl
from jax.experimental.pallas import tpu as pltpu

F32 = jnp.float32
BF16 = jnp.bfloat16
EPS = 1e-6
LANES = 128
ROW_TILE = 512
COL_TILE = 1408
PACK_ROWS = 256
PART_ROWS = 16
VMEM_LIMIT = 56 * 1024 * 1024

W_NAMES = ['ffn1_norm', 'ffn1_w_gate', 'ffn1_w_up', 'ffn1_w_down', 'mix_norm', 'ffn2_norm', 'ffn2_w_gate',
           'ffn2_w_up', 'ffn2_w_down', 'mem_norm', 'w_mem_kv', 'a_w_in', 'a_v_norm', 'a_w_spatial',
           'a_b_spatial', 'a_w_out', 'kv_norm', 'w_kv', 'b_w_in', 'b_w_out', 'final_norm']
SHARD_AXIS = {'ffn1_w_gate': 2, 'ffn1_w_up': 2, 'ffn1_w_down': 1, 'ffn2_w_gate': 2, 'ffn2_w_up': 2,
              'ffn2_w_down': 1, 'w_mem_kv': 1, 'a_w_in': 2, 'a_v_norm': 1, 'a_w_out': 1, 'w_kv': 1,
              'b_w_in': 1, 'b_w_out': 1}
N_CHIPS = 4
PACK_W = 1024

MEM_HEADS = 4
MEM_W = 256
HEAD_DIM = 64
GM_W = 768
GM_GROUPS = 6
GM_CHUNK = 128
CHUNK = 64
SB_BLOCK = 128
SB_Q_CHUNK = 1024
SB_DEAD = -110.0
SB_PAIRS = 2
SB_COLS = SB_PAIRS * LANES

ADAM_LR, ADAM_B1, ADAM_B2, ADAM_EPS, ADAM_WD, ADAM_STEP = 0.001, 0.9, 0.999, 1e-08, 0.01, 10

MESH = pl.DeviceIdType.MESH


def _cparams(sem=None):
    return pltpu.CompilerParams(dimension_semantics=sem, vmem_limit_bytes=VMEM_LIMIT)


def _tile(n, target, mult=LANES):
    best = None
    for t in range(mult, min(n, target) + 1, mult):
        if n % t == 0:
            best = t
    return best if best is not None else n


def _dot(a, b):
    return jnp.dot(a, b, preferred_element_type=F32)


def _dot_nt(a, b):
    return lax.dot_general(a, b, (((1,), (1,)), ((), ())), preferred_element_type=F32)


def _dot_tn(a, b):
    return lax.dot_general(a, b, (((0,), (0,)), ((), ())), preferred_element_type=F32)


def _bf(v):
    return v.astype(BF16)


def rms_fwd(x, gain, name):
    S, D = x.shape
    tm = _tile(S, ROW_TILE, 8)

    def body(x_ref, g_ref, o_ref):
        xf = x_ref[...]
        r = lax.rsqrt(jnp.mean(xf * xf, axis=-1, keepdims=True) + EPS)
        o_ref[...] = ((xf * r) * g_ref[...]).astype(o_ref.dtype)

    return pl.pallas_call(
        body, grid=(S // tm,),
        in_specs=[pl.BlockSpec((tm, D), lambda i: (i, 0)), pl.BlockSpec((1, D), lambda i: (0, 0))],
        out_specs=pl.BlockSpec((tm, D), lambda i: (i, 0)),
        out_shape=jax.ShapeDtypeStruct((S, D), BF16),
        compiler_params=_cparams(("parallel",)), name=name)(x, gain.reshape(1, D))


def rms_bwd(x, gain, dh, dres, name):
    S, D = x.shape
    tm = _tile(S, ROW_TILE, 16)
    has_res = dres is not None

    def body(*refs):
        if has_res:
            x_ref, g_ref, dh_ref, dres_ref, dx_ref, dxb_ref, dg_ref = refs
        else:
            x_ref, g_ref, dh_ref, dx_ref, dxb_ref, dg_ref = refs
        xf = x_ref[...]
        r = lax.rsqrt(jnp.mean(xf * xf, axis=-1, keepdims=True) + EPS)
        xhat = xf * r
        dhv = dh_ref[...].astype(F32)
        dxhat = dhv * g_ref[...]
        m = jnp.mean(dxhat * xhat, axis=-1, keepdims=True)
        dx = r * (dxhat - xhat * m)
        if has_res:
            dx = dx + dres_ref[...]
        dx_ref[...] = dx
        dxb_ref[...] = _bf(dx)

        @pl.when(pl.program_id(0) == 0)
        def _():
            dg_ref[...] = jnp.zeros_like(dg_ref)

        dg_ref[...] += jnp.sum(dhv * xhat, axis=0, keepdims=True)

    row = pl.BlockSpec((tm, D), lambda i: (i, 0))
    vec = pl.BlockSpec((1, D), lambda i: (0, 0))
    ins = [x, gain.reshape(1, D), dh] + ([dres] if has_res else [])
    in_specs = [row, vec, row] + ([row] if has_res else [])
    dx, dxb, dg = pl.pallas_call(
        body, grid=(S // tm,), in_specs=in_specs, out_specs=[row, row, vec],
        out_shape=[jax.ShapeDtypeStruct((S, D), F32), jax.ShapeDtypeStruct((S, D), BF16),
                   jax.ShapeDtypeStruct((1, D), F32)],
        compiler_params=_cparams(("arbitrary",)), name=name)(*ins)
    return (dx, dxb), dg


def final_loss(x, gain, target, name):
    S, D = x.shape
    tm = _tile(S, ROW_TILE, 16)

    def body(x_ref, g_ref, t_ref, loss_ref, dx_ref, dxb_ref, dg_ref):
        xf = x_ref[...]
        r = lax.rsqrt(jnp.mean(xf * xf, axis=-1, keepdims=True) + EPS)
        xhat = xf * r
        g = g_ref[...]
        diff = xhat * g - t_ref[...]
        dy = diff * (1.0 / D)
        dxhat = dy * g
        m = jnp.mean(dxhat * xhat, axis=-1, keepdims=True)
        dx = r * (dxhat - xhat * m)
        dx_ref[...] = dx
        dxb_ref[...] = _bf(dx)

        @pl.when(pl.program_id(0) == 0)
        def _():
            dg_ref[...] = jnp.zeros_like(dg_ref)
            loss_ref[...] = jnp.zeros_like(loss_ref)

        dg_ref[...] += jnp.sum(dy * xhat, axis=0, keepdims=True)
        per_tok = jnp.sum(diff * diff, axis=-1, keepdims=True) * (0.5 / D)
        loss_ref[...] += jnp.sum(per_tok, axis=0, keepdims=True)

    row = pl.BlockSpec((tm, D), lambda i: (i, 0))
    vec = pl.BlockSpec((1, D), lambda i: (0, 0))
    one = pl.BlockSpec((1, 1), lambda i: (0, 0))
    loss, dx, dxb, dg = pl.pallas_call(
        body, grid=(S // tm,), in_specs=[row, vec, row], out_specs=[one, row, row, vec],
        out_shape=[jax.ShapeDtypeStruct((1, 1), F32), jax.ShapeDtypeStruct((S, D), F32),
                   jax.ShapeDtypeStruct((S, D), BF16), jax.ShapeDtypeStruct((1, D), F32)],
        compiler_params=_cparams(("arbitrary",)), name=name)(x, gain.reshape(1, D), target)
    return loss, (dx, dxb), dg


def _wspec(w, blk, idx):
    _, l = w
    if l is None:
        return pl.BlockSpec(blk, idx)
    return pl.BlockSpec((None,) + blk, lambda *g: (l,) + idx(*g))


def mm(pairs, *, tb=False, res=None, scale=1.0, out_dtype=F32, name):
    M = pairs[0][0].shape[0]
    N = pairs[0][2][0].shape[-2 if tb else -1]
    tm = _tile(M, ROW_TILE, 8)
    tn = _tile(N, COL_TILE)
    n_p = len(pairs)
    has_res = res is not None

    def body(*refs):
        o_ref = refs[-1]
        acc = None
        for p in range(n_p):
            a = _bf(refs[2 * p][...])
            b = _bf(refs[2 * p + 1][...])
            d = _dot_nt(a, b) if tb else _dot(a, b)
            acc = d if acc is None else acc + d
        if scale != 1.0:
            acc = acc * scale
        if has_res:
            acc = acc + refs[2 * n_p][...]
        o_ref[...] = acc.astype(o_ref.dtype)

    ins, in_specs = [], []
    for a, cb, w, K, kb in pairs:
        ins += [a, w[0]]
        in_specs.append(pl.BlockSpec((tm, K), functools.partial(lambda j, i, cb: (i, cb), cb=cb)))
        if tb:
            in_specs.append(_wspec(w, (tn, K), functools.partial(lambda j, i, kb: (j, kb), kb=kb)))
        else:
            in_specs.append(_wspec(w, (K, tn), functools.partial(lambda j, i, kb: (kb, j), kb=kb)))
    if has_res:
        ins.append(res)
        in_specs.append(pl.BlockSpec((tm, tn), lambda j, i: (i, j)))
    return pl.pallas_call(
        body, grid=(N // tn, M // tm), in_specs=in_specs,
        out_specs=pl.BlockSpec((tm, tn), lambda j, i: (i, j)),
        out_shape=jax.ShapeDtypeStruct((M, N), out_dtype),
        compiler_params=_cparams(("parallel", "parallel")), name=name)(*ins)


def mm_tn(a, b, *, a_cb=0, a_w=None, b_cb=0, b_w=None, scale=1.0, name):
    S = a.shape[0]
    Ka = a.shape[1] if a_w is None else a_w
    Nb = b.shape[1] if b_w is None else b_w
    ts = _tile(S, ROW_TILE, 8)
    tk = _tile(Ka, COL_TILE)
    tn = _tile(Nb, COL_TILE)
    a_off, b_off = a_cb * (Ka // tk), b_cb * (Nb // tn)

    def body(a_ref, b_ref, o_ref):
        s = pl.program_id(2)

        @pl.when(s == 0)
        def _():
            o_ref[...] = jnp.zeros_like(o_ref)

        o_ref[...] += _dot_tn(_bf(a_ref[...]), _bf(b_ref[...]))
        if scale != 1.0:
            @pl.when(s == pl.num_programs(2) - 1)
            def _():
                o_ref[...] = o_ref[...] * scale

    return pl.pallas_call(
        body, grid=(Ka // tk, Nb // tn, S // ts),
        in_specs=[pl.BlockSpec((ts, tk), lambda k, n, s: (s, a_off + k)),
                  pl.BlockSpec((ts, tn), lambda k, n, s: (s, b_off + n))],
        out_specs=pl.BlockSpec((tk, tn), lambda k, n, s: (k, n)),
        out_shape=jax.ShapeDtypeStruct((Ka, Nb), F32),
        compiler_params=_cparams(("parallel", "parallel", "arbitrary")), name=name)(a, b)


def _sigmoid(x):
    return 0.5 * jnp.tanh(0.5 * x) + 0.5


def ffn_up(h, wg, wu, name):
    S, D = h.shape
    Fd = wg[0].shape[-1]
    tm = _tile(S, ROW_TILE, 8)
    tn = _tile(Fd, COL_TILE)

    def body(h_ref, wg_ref, wu_ref, g_ref, u_ref, a_ref):
        hv = h_ref[...]
        g = _dot(hv, wg_ref[...])
        u = _dot(hv, wu_ref[...])
        g_ref[...] = _bf(g)
        u_ref[...] = _bf(u)
        a_ref[...] = _bf(g * _sigmoid(g) * u)

    blk = pl.BlockSpec((tm, tn), lambda j, i: (i, j))
    sh = jax.ShapeDtypeStruct((S, Fd), BF16)
    return pl.pallas_call(
        body, grid=(Fd // tn, S // tm),
        in_specs=[pl.BlockSpec((tm, D), lambda j, i: (i, 0)), _wspec(wg, (D, tn), lambda j, i: (0, j)),
                  _wspec(wu, (D, tn), lambda j, i: (0, j))],
        out_specs=[blk, blk, blk], out_shape=[sh, sh, sh],
        compiler_params=_cparams(("parallel", "parallel")), name=name)(h, wg[0], wu[0])


def ffn_down_bwd(dxo, wd, g, u, name):
    S, D = dxo.shape
    Fd = wd[0].shape[-2]
    tm = _tile(S, ROW_TILE, 8)
    tn = _tile(Fd, COL_TILE)

    def body(dx_ref, wd_ref, g_ref, u_ref, dg_ref, du_ref):
        da = _dot_nt(_bf(dx_ref[...]), wd_ref[...]) * 0.5
        g = g_ref[...].astype(F32)
        u = u_ref[...].astype(F32)
        sg = _sigmoid(g)
        dg_ref[...] = _bf(da * u * (sg * (1.0 + g * (1.0 - sg))))
        du_ref[...] = _bf(da * (g * sg))

    blk = pl.BlockSpec((tm, tn), lambda j, i: (i, j))
    sh = jax.ShapeDtypeStruct((S, Fd), BF16)
    return pl.pallas_call(
        body, grid=(Fd // tn, S // tm),
        in_specs=[pl.BlockSpec((tm, D), lambda j, i: (i, 0)), _wspec(wd, (tn, D), lambda j, i: (j, 0)),
                  blk, blk],
        out_specs=[blk, blk], out_shape=[sh, sh],
        compiler_params=_cparams(("parallel", "parallel")), name=name)(dxo, wd[0], g, u)


_GELU_C = math.sqrt(2.0 / math.pi)


def _gelu(p):
    return 0.5 * p * (1.0 + jnp.tanh(_GELU_C * (p + 0.044715 * (p * p * p))))


def _gelu_grad(p):
    th = jnp.tanh(_GELU_C * (p + 0.044715 * (p * p * p)))
    return 0.5 * (1.0 + th) + 0.5 * p * (1.0 - th * th) * (_GELU_C * (1.0 + 3.0 * 0.044715 * (p * p)))


def _chunk_mask():
    t = lax.broadcasted_iota(jnp.int32, (GM_CHUNK, GM_CHUNK), 0)
    s = lax.broadcasted_iota(jnp.int32, (GM_CHUNK, GM_CHUNK), 1)
    return (s // CHUNK) <= (t // CHUNK)


def gmlp_fwd(proj, v_gain, w_s, b_mat, name):
    S = proj.shape[0]
    tw = _tile(S, ROW_TILE, GM_CHUNK)
    n_win = tw // GM_CHUNK

    def body(p_ref, gain_ref, w_ref, b_ref, y_ref):
        mask = _chunk_mask()
        v = _gelu(p_ref[:, GM_W:])
        r = lax.rsqrt(jnp.mean(v * v, axis=-1, keepdims=True) + EPS)
        vn = _bf(v * r * gain_ref[...])
        for g in range(GM_GROUPS):
            wm = _bf(jnp.where(mask, w_ref[g], 0.0))
            cs = slice(g * GM_CHUNK, (g + 1) * GM_CHUNK)
            for w in range(n_win):
                rs = slice(w * GM_CHUNK, (w + 1) * GM_CHUNK)
                mixed = _dot(wm, vn[rs, cs]) + b_ref[g]
                y_ref[rs, cs] = _bf(_gelu(p_ref[rs, cs]) * mixed)

    return pl.pallas_call(
        body, grid=(S // tw,),
        in_specs=[pl.BlockSpec((tw, 2 * GM_W), lambda i: (i, 0)), pl.BlockSpec((1, GM_W), lambda i: (0, 0)),
                  pl.BlockSpec((GM_GROUPS, GM_CHUNK, GM_CHUNK), lambda i: (0, 0, 0)),
                  pl.BlockSpec((GM_GROUPS, GM_CHUNK, GM_CHUNK), lambda i: (0, 0, 0))],
        out_specs=pl.BlockSpec((tw, GM_W), lambda i: (i, 0)),
        out_shape=jax.ShapeDtypeStruct((S, GM_W), BF16),
        compiler_params=_cparams(("parallel",)), name=name)(proj, v_gain.reshape(1, GM_W), w_s, b_mat)


def gmlp_bwd(proj, dy, v_gain, w_s, b_mat, name):
    S = proj.shape[0]
    tw = _tile(S, ROW_TILE, GM_CHUNK)
    n_win = tw // GM_CHUNK

    def body(p_ref, dy_ref, gain_ref, w_ref, b_ref, dp_ref, dw_ref, db_ref, dgain_ref, dvn_ref):
        step = pl.program_id(0)

        @pl.when(step == 0)
        def _():
            dw_ref[...] = jnp.zeros_like(dw_ref)
            db_ref[...] = jnp.zeros_like(db_ref)
            dgain_ref[...] = jnp.zeros_like(dgain_ref)

        mask = _chunk_mask()
        pv = p_ref[:, GM_W:]
        v = _gelu(pv)
        r = lax.rsqrt(jnp.mean(v * v, axis=-1, keepdims=True) + EPS)
        vhat = v * r
        gain = gain_ref[...]
        vn = _bf(vhat * gain)
        for g in range(GM_GROUPS):
            wm = _bf(jnp.where(mask, w_ref[g], 0.0))
            cs = slice(g * GM_CHUNK, (g + 1) * GM_CHUNK)
            dw_acc = jnp.zeros((GM_CHUNK, GM_CHUNK), F32)
            db_acc = jnp.zeros((GM_CHUNK, GM_CHUNK), F32)
            for w in range(n_win):
                rs = slice(w * GM_CHUNK, (w + 1) * GM_CHUNK)
                pu = p_ref[rs, cs]
                u = _gelu(pu)
                vn_blk = vn[rs, cs]
                mixed = _dot(wm, vn_blk) + b_ref[g]
                dyb = dy_ref[rs, cs]
                dp_ref[rs, cs] = _bf(dyb * mixed * _gelu_grad(pu))
                dmix = dyb * u
                db_acc = db_acc + dmix
                dmix_b = _bf(dmix)
                dw_acc = dw_acc + _dot_nt(dmix_b, vn_blk)
                dvn_ref[rs, cs] = _dot_tn(wm, dmix_b)
            dw_ref[g] += jnp.where(mask, dw_acc, 0.0)
            db_ref[g] += jnp.broadcast_to(jnp.sum(db_acc, axis=-1, keepdims=True), (GM_CHUNK, GM_CHUNK))
        dvn = dvn_ref[...]
        dgain_ref[...] += jnp.sum(dvn * vhat, axis=0, keepdims=True)
        dvhat = dvn * gain
        m = jnp.mean(dvhat * vhat, axis=-1, keepdims=True)
        dv = r * (dvhat - vhat * m)
        dp_ref[:, GM_W:] = _bf(dv * _gelu_grad(pv))

    sq = pl.BlockSpec((GM_GROUPS, GM_CHUNK, GM_CHUNK), lambda i: (0, 0, 0))
    sq_sh = jax.ShapeDtypeStruct((GM_GROUPS, GM_CHUNK, GM_CHUNK), F32)
    return pl.pallas_call(
        body, grid=(S // tw,),
        in_specs=[pl.BlockSpec((tw, 2 * GM_W), lambda i: (i, 0)), pl.BlockSpec((tw, GM_W), lambda i: (i, 0)),
                  pl.BlockSpec((1, GM_W), lambda i: (0, 0)), sq, sq],
        out_specs=[pl.BlockSpec((tw, 2 * GM_W), lambda i: (i, 0)), sq, sq,
                   pl.BlockSpec((1, GM_W), lambda i: (0, 0))],
        out_shape=[jax.ShapeDtypeStruct((S, 2 * GM_W), BF16), sq_sh, sq_sh,
                   jax.ShapeDtypeStruct((1, GM_W), F32)],
        scratch_shapes=[pltpu.VMEM((tw, GM_W), F32)],
        compiler_params=_cparams(("arbitrary",)), name=name)(proj, dy, v_gain.reshape(1, GM_W), w_s, b_mat)


def _head_mask(h, width):
    lane = lax.broadcasted_iota(jnp.int32, (1, width), 1)
    return (lane >= HEAD_DIM * h) & (lane < HEAD_DIM * (h + 1))


def _mem_probs(q, k, h):
    kh = jnp.where(_head_mask(h, MEM_W), k, jnp.zeros_like(k))
    s = _dot_nt(q, kh) * (HEAD_DIM ** -0.5)
    s = s - jnp.max(s, axis=-1, keepdims=True)
    e = jnp.exp(s)
    return e * (1.0 / jnp.sum(e, axis=-1, keepdims=True)), kh


def memattn_fwd(proj, q_cb, mem_kv, name):
    S = proj.shape[0]
    NM = mem_kv.shape[0]
    tm = _tile(S, ROW_TILE, 8)

    def body(q_ref, kv_ref, o_ref):
        q = _bf(q_ref[...])
        k = _bf(kv_ref[:, :MEM_W])
        v = _bf(kv_ref[:, MEM_W:])
        acc = jnp.zeros((tm, MEM_W), F32)
        for h in range(MEM_HEADS):
            p, _ = _mem_probs(q, k, h)
            vh = jnp.where(_head_mask(h, MEM_W), v, jnp.zeros_like(v))
            acc = acc + _dot(_bf(p), vh)
        o_ref[...] = _bf(acc)

    return pl.pallas_call(
        body, grid=(S // tm,),
        in_specs=[pl.BlockSpec((tm, MEM_W), lambda i: (i, q_cb)), pl.BlockSpec((NM, 2 * MEM_W), lambda i: (0, 0))],
        out_specs=pl.BlockSpec((tm, MEM_W), lambda i: (i, 0)),
        out_shape=jax.ShapeDtypeStruct((S, MEM_W), BF16),
        compiler_params=_cparams(("parallel",)), name=name)(proj, mem_kv)


def memattn_bwd(proj, q_cb, mem_kv, dy, dy_cb, name):
    S = proj.shape[0]
    NM = mem_kv.shape[0]
    tm = _tile(S, ROW_TILE, 8)

    def body(q_ref, kv_ref, do_ref, dq_ref, dkv_ref):
        @pl.when(pl.program_id(0) == 0)
        def _():
            dkv_ref[...] = jnp.zeros_like(dkv_ref)

        q = _bf(q_ref[...])
        k = _bf(kv_ref[:, :MEM_W])
        v = _bf(kv_ref[:, MEM_W:])
        do = _bf(do_ref[...])
        dq = jnp.zeros((tm, MEM_W), F32)
        dk = jnp.zeros((NM, MEM_W), F32)
        dv = jnp.zeros((NM, MEM_W), F32)
        for h in range(MEM_HEADS):
            hm = _head_mask(h, MEM_W)
            p, kh = _mem_probs(q, k, h)
            vh = jnp.where(hm, v, jnp.zeros_like(v))
            dp = _dot_nt(do, vh)
            ds = _bf(p * (dp - jnp.sum(p * dp, axis=-1, keepdims=True)) * (HEAD_DIM ** -0.5))
            dq = dq + _dot(ds, kh)
            dk = dk + jnp.where(hm, _dot_tn(ds, q), 0.0)
            dv = dv + jnp.where(hm, _dot_tn(_bf(p), do), 0.0)
        dq_ref[...] = _bf(dq)
        dkv_ref[:, :MEM_W] += dk
        dkv_ref[:, MEM_W:] += dv

    return pl.pallas_call(
        body, grid=(S // tm,),
        in_specs=[pl.BlockSpec((tm, MEM_W), lambda i: (i, q_cb)), pl.BlockSpec((NM, 2 * MEM_W), lambda i: (0, 0)),
                  pl.BlockSpec((tm, MEM_W), lambda i: (i, dy_cb))],
        out_specs=[pl.BlockSpec((tm, MEM_W), lambda i: (i, 0)), pl.BlockSpec((NM, 2 * MEM_W), lambda i: (0, 0))],
        out_shape=[jax.ShapeDtypeStruct((S, MEM_W), BF16), jax.ShapeDtypeStruct((NM, 2 * MEM_W), F32)],
        compiler_params=_cparams(("arbitrary",)), name=name)(proj, mem_kv, dy)


def _split_bf(v):
    hi = _bf(v)
    return hi, _bf(v - hi.astype(F32))


def _sb_weights(qh, k2, run, dpos, row, col, tri):
    z = _dot_nt(qh, k2) * (HEAD_DIM ** -0.5)
    t = jnp.log(1.0 + jnp.exp(-jnp.abs(z)))
    lb = jnp.minimum(z, 0.0) - t
    mask = (col - row) < dpos
    lom = jnp.where(mask, -jnp.maximum(z, 0.0) - t, 0.0)
    l_hi, l_lo = _split_bf(lom)
    insuf = _dot(l_hi, tri) + _dot(l_lo, tri)
    a = jnp.where(mask, jnp.exp(lb + run + insuf), 0.0)
    return a, lb, lom, mask


def _sb_alive(runs):
    top = runs[0]
    for r in runs[1:]:
        top = jnp.maximum(top, r)
    return jnp.max(top) > SB_DEAD


def _sb_consts():
    row = lax.broadcasted_iota(jnp.int32, (2 * SB_BLOCK, SB_BLOCK), 0) % SB_BLOCK
    col = lax.broadcasted_iota(jnp.int32, (2 * SB_BLOCK, SB_BLOCK), 1)
    r = lax.broadcasted_iota(jnp.int32, (SB_BLOCK, SB_BLOCK), 0)
    c = lax.broadcasted_iota(jnp.int32, (SB_BLOCK, SB_BLOCK), 1)
    return row, col, _bf(jnp.where(r > c, 1.0, 0.0)), _bf(jnp.where(r >= c, 1.0, 0.0))


def _stack_heads(x2, masks):
    return jnp.concatenate([_bf(jnp.where(hm, x2, jnp.zeros_like(x2))) for hm in masks], axis=0)


def _side_by_side(x):
    return jnp.concatenate([x[:SB_BLOCK], x[SB_BLOCK:]], axis=1)


def sb_fwd(proj, kv, name):
    S = proj.shape[0]
    n_g = GM_W // SB_COLS
    tq = _tile(S, SB_Q_CHUNK, SB_BLOCK)
    nqb = tq // SB_BLOCK

    def body(q_ref, k_ref, v_ref, o_ref):
        chunk = pl.program_id(1)
        row, col, tri, _ = _sb_consts()
        masks = [_head_mask(h, LANES) for h in range(2)]
        pairs = [slice(p * LANES, (p + 1) * LANES) for p in range(SB_PAIRS)]

        def q_loop(qi, _):
            i = chunk * nqb + qi
            rows = pl.ds(pl.multiple_of(qi * SB_BLOCK, SB_BLOCK), SB_BLOCK)
            qps = [_stack_heads(q_ref[rows, cs], masks) for cs in pairs]

            def k_step(carry):
                t, accs, runs, _ = carry
                keys = pl.ds(pl.multiple_of((i - t) * SB_BLOCK, SB_BLOCK), SB_BLOCK)
                new_accs, new_runs = [], []
                for p, cs in enumerate(pairs):
                    a, _, lom, _ = _sb_weights(qps[p], k_ref[keys, cs], runs[p], t * SB_BLOCK, row, col, tri)
                    a_hi, a_lo = _split_bf(_side_by_side(a))
                    vp = _stack_heads(v_ref[keys, cs], masks)
                    new_accs.append(accs[p] + _dot(a_hi, vp) + _dot(a_lo, vp))
                    new_runs.append(runs[p] + jnp.sum(lom, axis=1, keepdims=True))
                return t + 1, tuple(new_accs), tuple(new_runs), _sb_alive(new_runs)

            zero = jnp.zeros((2 * SB_BLOCK, 1), F32)
            _, accs, _, _ = lax.while_loop(
                lambda carry: jnp.logical_and(carry[0] <= i, carry[3]), k_step,
                (jnp.int32(0), (jnp.zeros((SB_BLOCK, LANES), F32),) * SB_PAIRS, (zero,) * SB_PAIRS,
                 jnp.bool_(True)))
            for p, cs in enumerate(pairs):
                o_ref[rows, cs] = accs[p]
            return 0

        lax.fori_loop(0, nqb, q_loop, 0)

    return pl.pallas_call(
        body, grid=(n_g, S // tq),
        in_specs=[pl.BlockSpec((tq, SB_COLS), lambda g, c: (c, g)),
                  pl.BlockSpec((S, SB_COLS), lambda g, c: (0, g)),
                  pl.BlockSpec((S, SB_COLS), lambda g, c: (0, n_g + g))],
        out_specs=pl.BlockSpec((tq, SB_COLS), lambda g, c: (c, g)),
        out_shape=jax.ShapeDtypeStruct((S, GM_W), F32),
        compiler_params=_cparams(("parallel", "parallel")), name=name)(proj, kv, kv)


def sb_bwd(proj, kv, dy, out, dk_init, dv_init, name):
    S = proj.shape[0]
    n_g = GM_W // SB_COLS
    tq = _tile(S, SB_Q_CHUNK, SB_BLOCK)
    nqb = tq // SB_BLOCK
    n_chunks = S // tq
    has_init = dk_init is not None
    scale = HEAD_DIM ** -0.5

    def body(*refs):
        if has_init:
            q_ref, k_ref, v_ref, do_ref, out_ref, dki_ref, dvi_ref, dq_ref, dko_ref, dvo_ref, dk_acc, dv_acc = refs
        else:
            q_ref, k_ref, v_ref, do_ref, out_ref, dq_ref, dko_ref, dvo_ref, dk_acc, dv_acc = refs
        chunk = pl.program_id(1)
        cols = pl.ds(pl.multiple_of(pl.program_id(0) * SB_COLS, SB_COLS), SB_COLS)

        @pl.when(chunk == 0)
        def _():
            if has_init:
                pltpu.sync_copy(dki_ref.at[:, cols], dk_acc)
                pltpu.sync_copy(dvi_ref.at[:, cols], dv_acc)
            else:
                dk_acc[...] = jnp.zeros_like(dk_acc)
                dv_acc[...] = jnp.zeros_like(dv_acc)

        row, col, tri, tri_inc = _sb_consts()
        masks = [_head_mask(h, LANES) for h in range(2)]
        pairs = [slice(p * LANES, (p + 1) * LANES) for p in range(SB_PAIRS)]

        def q_loop(qi, _):
            i = chunk * nqb + qi
            rows = pl.ds(pl.multiple_of(qi * SB_BLOCK, SB_BLOCK), SB_BLOCK)
            qps = [_stack_heads(q_ref[rows, cs], masks) for cs in pairs]
            dops = [_stack_heads(do_ref[rows, cs], masks) for cs in pairs]
            e_tots = [jnp.sum(dop.astype(F32) * jnp.concatenate([out_ref[rows, cs]] * 2, axis=0), axis=1,
                              keepdims=True) for dop, cs in zip(dops, pairs)]

            def k_step(carry):
                t, dqs, runs, e_runs, _ = carry
                keys = pl.ds(pl.multiple_of((i - t) * SB_BLOCK, SB_BLOCK), SB_BLOCK)
                new_dqs, new_runs, new_e_runs = [], [], []
                for p, cs in enumerate(pairs):
                    k2, v2 = k_ref[keys, cs], v_ref[keys, cs]
                    a, lb, lom, mask = _sb_weights(qps[p], k2, runs[p], t * SB_BLOCK, row, col, tri)
                    e = a * _dot_nt(dops[p], v2)
                    e_hi, e_lo = _split_bf(e)
                    before = e_tots[p] - e_runs[p] - (_dot(e_hi, tri_inc) + _dot(e_lo, tri_inc))
                    beta = jnp.exp(lb)
                    dz = _bf(jnp.where(mask, e * (1.0 - beta) - before * beta, 0.0) * scale)
                    new_dqs.append(dqs[p] + _dot(_side_by_side(dz), _stack_heads(k2, masks)))
                    dk_acc[keys, cs] += _dot_tn(dz, qps[p])
                    dv_acc[keys, cs] += _dot_tn(_bf(a), dops[p])
                    new_runs.append(runs[p] + jnp.sum(lom, axis=1, keepdims=True))
                    new_e_runs.append(e_runs[p] + jnp.sum(e, axis=1, keepdims=True))
                return t + 1, tuple(new_dqs), tuple(new_runs), tuple(new_e_runs), _sb_alive(new_runs)

            zero = jnp.zeros((2 * SB_BLOCK, 1), F32)
            zeros = (zero,) * SB_PAIRS
            _, dqs, _, _, _ = lax.while_loop(
                lambda carry: jnp.logical_and(carry[0] <= i, carry[4]), k_step,
                (jnp.int32(0), (jnp.zeros((SB_BLOCK, LANES), F32),) * SB_PAIRS, zeros, zeros, jnp.bool_(True)))
            for p, cs in enumerate(pairs):
                dq_ref[rows, cs] = dqs[p]
            return 0

        lax.fori_loop(0, nqb, q_loop, 0)

        @pl.when(chunk == n_chunks - 1)
        def _():
            pltpu.sync_copy(dk_acc, dko_ref.at[:, cols])
            pltpu.sync_copy(dv_acc, dvo_ref.at[:, cols])

    qspec = pl.BlockSpec((tq, SB_COLS), lambda g, c: (c, g))
    kspec = pl.BlockSpec((S, SB_COLS), lambda g, c: (0, g))
    in_specs = [qspec, kspec, pl.BlockSpec((S, SB_COLS), lambda g, c: (0, n_g + g)), qspec, qspec]
    ins = [proj, kv, kv, dy, out]
    if has_init:
        in_specs += [_ANY, _ANY]
        ins += [dk_init, dv_init]
    sh = jax.ShapeDtypeStruct((S, GM_W), F32)
    return pl.pallas_call(
        body, grid=(n_g, n_chunks), in_specs=in_specs, out_specs=[qspec, _ANY, _ANY],
        out_shape=[sh, sh, sh],
        scratch_shapes=[pltpu.VMEM((S, SB_COLS), F32), pltpu.VMEM((S, SB_COLS), F32)],
        compiler_params=_cparams(("arbitrary", "arbitrary")), name=name)(*ins)


def adamw(w, g, m, v, name):
    shape = w.shape
    C = shape[-1] if w.ndim > 1 else shape[0]
    R = w.size // C
    tr = _tile(R, ROW_TILE, 8)

    def body(w_ref, g_ref, m_ref, v_ref, d_ref, nm_ref, nv_ref):
        gv = g_ref[...]
        m2 = ADAM_B1 * m_ref[...] + (1.0 - ADAM_B1) * gv
        v2 = ADAM_B2 * v_ref[...] + (1.0 - ADAM_B2) * (gv * gv)
        m_hat = m2 / (1.0 - ADAM_B1 ** ADAM_STEP)
        v_hat = v2 / (1.0 - ADAM_B2 ** ADAM_STEP)
        d_ref[...] = -ADAM_LR * (m_hat / (jnp.sqrt(v_hat) + ADAM_EPS) + ADAM_WD * w_ref[...])
        nm_ref[...] = m2
        nv_ref[...] = v2

    blk = pl.BlockSpec((tr, C), lambda i: (i, 0))
    sh = jax.ShapeDtypeStruct((R, C), F32)
    outs = pl.pallas_call(
        body, grid=(R // tr,), in_specs=[blk] * 4, out_specs=[blk] * 3, out_shape=[sh] * 3,
        compiler_params=_cparams(("parallel",)), name=name)(
            w.reshape(R, C), g.reshape(R, C), m.reshape(R, C), v.reshape(R, C))
    return tuple(o.reshape(shape) for o in outs)


def _place():
    return lax.axis_index("x"), lax.axis_index("y"), lax.axis_index("c")


def _other_chips(x, y):
    return [(1 - x, y), (x, 1 - y), (1 - x, 1 - y)]


_ANY = pl.BlockSpec(memory_space=pl.ANY)
LOCAL_CHUNKS = 8
AG_CHUNKS = 4


def _local_copies(src_of, dst_of, n_rows, sems, sem_base):
    rows = n_rows // LOCAL_CHUNKS
    copies = []
    for k in range(LOCAL_CHUNKS):
        r = pl.ds(k * rows, rows)
        cp = pltpu.make_async_copy(src_of(r), dst_of(r), sems.at[sem_base + k])
        cp.start()
        copies.append(cp)
    return copies


def _remote(src, dst, send_sems, recv_sems, k, to):
    return pltpu.make_async_remote_copy(src_ref=src, dst_ref=dst, send_sem=send_sems.at[k], recv_sem=recv_sems.at[k],
                                        device_id=to, device_id_type=MESH)


def all_gather_chips(bufs, name):
    nb = len(bufs)

    def body(*refs):
        in_refs, out_refs = refs[:nb], refs[nb:2 * nb]
        send_sems, recv_sems, local_sems = refs[2 * nb:]
        x, y, c = _place()
        sibling = (x, y, 1 - c)
        chips = _other_chips(x, y)

        n_sem = 2 * 3 * AG_CHUNKS

        def piece(b, px, py, pc, q):
            rows = bufs[b].shape[0] // (2 * AG_CHUNKS)
            return out_refs[b].at[2 * px + py, pl.ds((pc * AG_CHUNKS + q) * rows, rows), :]

        def my_piece(b, q):
            rows = bufs[b].shape[0] // (2 * AG_CHUNKS)
            return in_refs[b].at[pl.ds((c * AG_CHUNKS + q) * rows, rows), :]

        def sem(b, stage, j, q):
            return n_sem * b + (stage * 3 + j) * AG_CHUNKS + q

        mine, first, passed = [], [], []
        for b in range(nb):
            mine += _local_copies(lambda r, b=b: in_refs[b].at[r, :], lambda r, b=b: out_refs[b].at[2 * x + y, r, :],
                                  bufs[b].shape[0], local_sems, b * LOCAL_CHUNKS)
        for q in range(AG_CHUNKS):
            for b in range(nb):
                for j, chip in enumerate(chips):
                    cp = _remote(my_piece(b, q), piece(b, x, y, c, q), send_sems, recv_sems, sem(b, 0, j, q),
                                 (*chip, c))
                    cp.start()
                    first.append(cp)
        for q in range(AG_CHUNKS):
            for j, chip in enumerate(chips):
                for b in range(nb):
                    landed = piece(b, *chip, c, q)
                    _remote(landed, landed, send_sems, recv_sems, sem(b, 0, j, q), (*chip, c)).wait_recv()
                    cp = _remote(landed, landed, send_sems, recv_sems, sem(b, 1, j, q), sibling)
                    cp.start()
                    passed.append(cp)
        for q in range(AG_CHUNKS):
            for j, chip in enumerate(chips):
                for b in range(nb):
                    landed = piece(b, *chip, 1 - c, q)
                    _remote(landed, landed, send_sems, recv_sems, sem(b, 1, j, q), sibling).wait_recv()
        for cp in first + passed:
            cp.wait_send()
        for cp in mine:
            cp.wait()

    n_sems = 2 * 3 * AG_CHUNKS * nb
    return pl.pallas_call(
        body, in_specs=[_ANY] * nb, out_specs=[_ANY] * nb,
        out_shape=[jax.ShapeDtypeStruct((N_CHIPS,) + b.shape, b.dtype) for b in bufs],
        scratch_shapes=[pltpu.SemaphoreType.DMA((n_sems,)), pltpu.SemaphoreType.DMA((n_sems,)),
                        pltpu.SemaphoreType.DMA((LOCAL_CHUNKS * nb,))],
        name=name)(*bufs)


def swap_halves(gs, name):
    nb = len(gs)
    n = N_CHIPS

    def body(*refs):
        g_refs, t_refs = refs[:nb], refs[nb:2 * nb]
        send_sems, recv_sems = refs[2 * nb:]
        x, y, c = _place()
        gives = []
        for b in range(nb):
            Ph = gs[b].shape[1] // 2
            for k in range(n):
                gives.append(_remote(g_refs[b].at[k, pl.ds((1 - c) * Ph, Ph), :], t_refs[b].at[k], send_sems,
                                     recv_sems, n * b + k, (x, y, 1 - c)))
        for cp in gives:
            cp.start()
        for cp in gives:
            cp.wait()

    return pl.pallas_call(
        body, in_specs=[_ANY] * nb, out_specs=[_ANY] * nb,
        out_shape=[jax.ShapeDtypeStruct((n, g.shape[1] // 2, g.shape[2]), g.dtype) for g in gs],
        scratch_shapes=[pltpu.SemaphoreType.DMA((n * nb,)), pltpu.SemaphoreType.DMA((n * nb,))],
        name=name)(*gs)


def _core_index():
    return lax.axis_index("c").astype(jnp.int32).reshape(1)


def add_cores(g, theirs, name):
    n, Ph, Wd = theirs.shape
    tr = _tile(Ph, ROW_TILE, 16)
    steps = Ph // tr

    def body(c_ref, g_ref, t_ref, o_ref):
        o_ref[...] = _bf(g_ref[...] + t_ref[...])

    blk = pl.BlockSpec((None, tr, Wd), lambda k, i, c_ref: (k, i, 0))
    return pl.pallas_call(
        body,
        grid_spec=pltpu.PrefetchScalarGridSpec(
            num_scalar_prefetch=1, grid=(n, steps),
            in_specs=[pl.BlockSpec((None, tr, Wd), lambda k, i, c_ref: (k, c_ref[0] * steps + i, 0)), blk],
            out_specs=blk),
        out_shape=jax.ShapeDtypeStruct((n, Ph, Wd), BF16),
        compiler_params=_cparams(("parallel", "parallel")), name=name)(_core_index(), g, theirs)


def add_chips(slots, name):
    n, Ph, Wd = slots.shape
    tr = _tile(Ph, ROW_TILE, 16)
    steps = Ph // tr

    def body(c_ref, *refs):
        acc = refs[0][...].astype(F32)
        for k in range(1, n):
            acc = acc + refs[k][...].astype(F32)
        refs[n][...] = acc

    in_specs = [pl.BlockSpec((None, tr, Wd), functools.partial(lambda i, c_ref, k: (k, i, 0), k=k)) for k in range(n)]
    return pl.pallas_call(
        body,
        grid_spec=pltpu.PrefetchScalarGridSpec(
            num_scalar_prefetch=1, grid=(steps,), in_specs=in_specs,
            out_specs=pl.BlockSpec((tr, Wd), lambda i, c_ref: (c_ref[0] * steps + i, 0))),
        out_shape=jax.ShapeDtypeStruct((2 * Ph, Wd), F32),
        compiler_params=_cparams(("parallel",)), name=name)(_core_index(), *([slots] * n))


def scatter_to_chips(hs, name):
    nb = len(hs)

    def body(*refs):
        h_refs, out_refs = refs[:nb], refs[nb:2 * nb]
        send_sems, recv_sems, local_sems = refs[2 * nb:]
        x, y, c = _place()
        me = 2 * x + y
        chips = _other_chips(x, y)
        keep, sends = [], []
        for b in range(nb):
            keep += _local_copies(lambda r, b=b: h_refs[b].at[me, r, :], lambda r, b=b: out_refs[b].at[me, r, :],
                                  hs[b].shape[1], local_sems, b * LOCAL_CHUNKS)
            for j, (cx, cy) in enumerate(chips):
                cp = _remote(h_refs[b].at[2 * cx + cy], out_refs[b].at[me], send_sems, recv_sems, 3 * b + j,
                             (cx, cy, c))
                cp.start()
                sends.append(cp)
        for b in range(nb):
            for j, (cx, cy) in enumerate(chips):
                _remote(h_refs[b].at[me], out_refs[b].at[2 * cx + cy], send_sems, recv_sems, 3 * b + j,
                        (cx, cy, c)).wait_recv()
        for cp in sends:
            cp.wait_send()
        for cp in keep:
            cp.wait()

    return pl.pallas_call(
        body, in_specs=[_ANY] * nb, out_specs=[_ANY] * nb,
        out_shape=[jax.ShapeDtypeStruct(h.shape, h.dtype) for h in hs],
        scratch_shapes=[pltpu.SemaphoreType.DMA((3 * nb,)), pltpu.SemaphoreType.DMA((3 * nb,)),
                        pltpu.SemaphoreType.DMA((LOCAL_CHUNKS * nb,))],
        name=name)(*hs)


JOIN_CHUNKS = 4


def join_halves(ts, name):
    nb = len(ts)
    n = JOIN_CHUNKS

    def body(*refs):
        out_refs = refs[nb:2 * nb]
        send_sems, recv_sems = refs[2 * nb:]
        x, y, c = _place()
        gives = []
        for b in range(nb):
            Ph = ts[b].shape[0] // 2
            rows = Ph // n
            for k in range(n):
                part = out_refs[b].at[pl.ds(c * Ph + k * rows, rows), :]
                gives.append(_remote(part, part, send_sems, recv_sems, n * b + k, (x, y, 1 - c)))
        for cp in gives:
            cp.start()
        for b in range(nb):
            Ph = ts[b].shape[0] // 2
            rows = Ph // n
            for k in range(n):
                landed = out_refs[b].at[pl.ds((1 - c) * Ph + k * rows, rows), :]
                _remote(landed, landed, send_sems, recv_sems, n * b + k, (x, y, 1 - c)).wait_recv()
        for cp in gives:
            cp.wait_send()

    return pl.pallas_call(
        body, in_specs=[_ANY] * nb, out_specs=[_ANY] * nb,
        out_shape=[jax.ShapeDtypeStruct(t.shape, t.dtype) for t in ts],
        scratch_shapes=[pltpu.SemaphoreType.DMA((n * nb,)), pltpu.SemaphoreType.DMA((n * nb,))],
        input_output_aliases={b: b for b in range(nb)}, name=name)(*ts)


WIDE = ['ffn1_w_gate', 'ffn1_w_up', 'ffn2_w_gate', 'ffn2_w_up']


def _rows_of(p):
    if p.shape[-1] == PACK_W and (p.size // PACK_W) % PART_ROWS == 0:
        return p.reshape(-1, PACK_W)
    flat = p.reshape(-1)
    rows = -(-flat.shape[0] // (PACK_W * PART_ROWS)) * PART_ROWS
    return jnp.pad(flat, (0, rows * PACK_W - flat.shape[0])).reshape(rows, PACK_W)


def _pack(parts):
    buf = jnp.concatenate([_rows_of(p) for p in parts], axis=0)
    rows = buf.shape[0]
    total = -(-rows // PACK_ROWS) * PACK_ROWS
    return jnp.pad(buf, ((0, total - rows), (0, 0)))


def _unpack(buf, shapes):
    outs, r = [], 0
    lead = buf.shape[:-2]
    for shp in shapes:
        n = math.prod(shp)
        rows = -(-n // (PACK_W * PART_ROWS)) * PART_ROWS
        blk = buf[..., r:r + rows, :]
        if n != rows * PACK_W:
            blk = blk.reshape(lead + (rows * PACK_W,))[..., :n]
        outs.append(blk.reshape(lead + tuple(shp)))
        r += rows
    return outs


def gather_weights(shards):
    narrow = [n for n in W_NAMES if n in SHARD_AXIS and n not in WIDE]
    parts = [lax.bitcast_convert_type(shards[n], BF16) if n == 'a_v_norm' else shards[n].astype(BF16)
             for n in narrow]
    wide = [shards[n].astype(BF16) for n in WIDE]
    cols = wide[0].shape[-1]
    wide_buf = jnp.concatenate([w.reshape(-1, cols) for w in wide], axis=0)
    full_wide, full_narrow = all_gather_chips([wide_buf, _pack(parts)], "all_gather_weights")
    out = {}
    rows = wide[0].size // cols
    for i, n in enumerate(WIDE):
        g = full_wide[:, i * rows:(i + 1) * rows, :].reshape((N_CHIPS,) + wide[i].shape)
        out[n] = jnp.concatenate([g[k] for k in range(N_CHIPS)], axis=SHARD_AXIS[n])
    for n, g in zip(narrow, _unpack(full_narrow, [p.shape for p in parts])):
        if n == 'a_v_norm':
            g = lax.bitcast_convert_type(g, F32)
        out[n] = jnp.concatenate([g[k] for k in range(N_CHIPS)], axis=SHARD_AXIS[n])
    return out


def reduce_grads(grads, shard_shapes):
    narrow = [n for n in W_NAMES if n in SHARD_AXIS and n not in WIDE]
    repl = [n for n in W_NAMES if n not in SHARD_AXIS]
    full = {n: (jnp.stack(g, axis=0) if isinstance(g, list) else g) for n, g in grads.items() if n not in WIDE}
    cols = shard_shapes[WIDE[0]][-1]
    wide_parts = []
    for n in WIDE:
        for g in grads[n]:
            wide_parts.append(g.reshape(g.shape[0], N_CHIPS, cols).transpose(1, 0, 2))
    g_wide = jnp.concatenate(wide_parts, axis=1)
    blocks = []
    for k in range(N_CHIPS):
        parts = [jnp.split(full[n], N_CHIPS, axis=SHARD_AXIS[n])[k] for n in narrow]
        parts += [full[n] for n in repl]
        blocks.append(_pack(parts))
    gs = [g_wide, jnp.stack(blocks, axis=0)]
    theirs = swap_halves(gs, "grads_swap_halves")
    chip_sums = [add_cores(g, t, f"grads_add_cores_{i}") for i, (g, t) in enumerate(zip(gs, theirs))]
    slots = scatter_to_chips(chip_sums, "grads_scatter")
    totals = join_halves([add_chips(s, f"grads_add_chips_{i}") for i, s in enumerate(slots)], "grads_join_halves")
    out = {}
    rows = math.prod(shard_shapes[WIDE[0]][:-1])
    for i, n in enumerate(WIDE):
        out[n] = totals[0][i * rows:(i + 1) * rows, :].reshape(shard_shapes[n])
    shapes = [shard_shapes[n] for n in narrow] + [full[n].shape for n in repl]
    out.update(zip(narrow + repl, _unpack(totals[1], shapes)))
    return out


def _ffn_fwd(x, gain, wg, wu, wd, tag):
    h = rms_fwd(x, gain, tag + "_norm")
    g, u, a = ffn_up(h, wg, wu, tag + "_up")
    xo = mm([(a, 0, wd, a.shape[1], 0)], res=x, scale=0.5, name=tag + "_down")
    return xo, (x, h, g, u, a)


def _ffn_bwd(dxo, saved, gain, wg, wu, wd, tag):
    x, h, g, u, a = saved
    Fd = a.shape[1]
    dxo, dxo_b = dxo
    dgp, du = ffn_down_bwd(dxo_b, wd, g, u, tag + "_down_bwd")
    d_wd = mm_tn(a, dxo_b, scale=0.5, name=tag + "_dwd")
    d_wg = mm_tn(h, dgp, name=tag + "_dwg")
    d_wu = mm_tn(h, du, name=tag + "_dwu")
    dh = mm([(dgp, 0, wg, Fd, 0), (du, 0, wu, Fd, 0)], tb=True, name=tag + "_up_bwd")
    dx, d_gain = rms_bwd(x, gain, dh, dxo, tag + "_norm_bwd")
    return dx, d_gain, d_wg, d_wu, d_wd


def local_step(x, mem, W, target):
    D = x.shape[1]
    depth = W['ffn1_norm'].shape[0]
    n_a = W['a_w_in'].shape[0]
    G = {}
    mem_h = rms_fwd(mem, W['mem_norm'], "mem_norm")
    b_mats = jnp.broadcast_to(W['a_b_spatial'][..., None], W['a_b_spatial'].shape + (GM_CHUNK,))
    w_kv = (W['w_kv'], None)

    def ffn_w(which, l):
        return [(W[f'{which}_w_{part}'], l) for part in ('gate', 'up', 'down')]

    def mixer_w(l):
        if l < n_a:
            return (W['a_w_in'], l), (W['a_w_out'], l), 2 * GM_W
        return (W['b_w_in'], l - n_a), (W['b_w_out'], l - n_a), GM_W

    saved = []
    kv = kvn = x_kv = None
    for l in range(depth):
        if l == n_a:
            x_kv = x
            kvn = rms_fwd(x, W['kv_norm'], "kv_norm")
            kv = mm([(kvn, 0, w_kv, D, 0)], out_dtype=BF16, name="kv_proj")
        x, s1 = _ffn_fwd(x, W['ffn1_norm'][l], *ffn_w('ffn1', l), f"l{l}_ffn1")
        hm = rms_fwd(x, W['mix_norm'][l], f"l{l}_mix_norm")
        mem_kv = mm([(mem_h, 0, (W['w_mem_kv'], l), D, 0)], name=f"l{l}_mem_kv")
        w_in, w_out, tok_w = mixer_w(l)
        proj = mm([(hm, 0, w_in, D, 0)], name=f"l{l}_mix_in")
        if l < n_a:
            y_tok = gmlp_fwd(proj, W['a_v_norm'][l], W['a_w_spatial'][l], b_mats[l], f"l{l}_gmlp")
        else:
            y_tok = sb_fwd(proj, kv, f"l{l}_sb")
        q_cb = tok_w // MEM_W
        y_mem = memattn_fwd(proj, q_cb, mem_kv, f"l{l}_memattn")
        x_mid = x
        x = mm([(y_tok, 0, w_out, GM_W, 0), (y_mem, 0, w_out, MEM_W, GM_W // MEM_W)], res=x, name=f"l{l}_mix_out")
        sm = (x_mid, hm, mem_kv, proj, y_tok, y_mem, q_cb)
        x, s2 = _ffn_fwd(x, W['ffn2_norm'][l], *ffn_w('ffn2', l), f"l{l}_ffn2")
        saved.append((s1, sm, s2))

    loss, dx, d_final = final_loss(x, W['final_norm'], target, "loss_head")
    G['final_norm'] = d_final.reshape(-1)

    per_layer = {n: [None] * depth for n in ['ffn1_norm', 'ffn1_w_gate', 'ffn1_w_up', 'ffn1_w_down', 'mix_norm',
                                             'ffn2_norm', 'ffn2_w_gate', 'ffn2_w_up', 'ffn2_w_down', 'w_mem_kv']}
    per_a = {n: [None] * n_a for n in ['a_w_in', 'a_v_norm', 'a_w_spatial', 'a_b_spatial', 'a_w_out']}
    per_b = {n: [None] * (depth - n_a) for n in ['b_w_in', 'b_w_out']}
    d_mem_h = None
    dk = dv = None
    for l in reversed(range(depth)):
        s1, sm, s2 = saved[l]
        dx, dg, dwg, dwu, dwd = _ffn_bwd(dx, s2, W['ffn2_norm'][l], *ffn_w('ffn2', l), f"l{l}_ffn2")
        per_layer['ffn2_norm'][l], per_layer['ffn2_w_gate'][l] = dg.reshape(-1), dwg
        per_layer['ffn2_w_up'][l], per_layer['ffn2_w_down'][l] = dwu, dwd

        x_mid, hm, mem_kv, proj, y_tok, y_mem, q_cb = sm
        is_a = l < n_a
        w_in, w_out, tok_w = mixer_w(l)
        dy = mm([(dx[1], 0, w_out, D, 0)], tb=True, name=f"l{l}_mix_out_bwd")
        d_w_out = jnp.concatenate([mm_tn(y_tok, dx[1], name=f"l{l}_dwout_tok"),
                                   mm_tn(y_mem, dx[1], name=f"l{l}_dwout_mem")], axis=0)
        dq_mem, d_mem_kv = memattn_bwd(proj, q_cb, mem_kv, dy, GM_W // MEM_W, f"l{l}_memattn_bwd")
        if is_a:
            d_tok, d_ws, d_bs, d_vg = gmlp_bwd(proj, dy, W['a_v_norm'][l], W['a_w_spatial'][l], b_mats[l],
                                               f"l{l}_gmlp_bwd")
            per_a['a_w_spatial'][l], per_a['a_b_spatial'][l] = d_ws, d_bs[:, :, 0]
            per_a['a_v_norm'][l], per_a['a_w_out'][l] = d_vg.reshape(-1), d_w_out
        else:
            d_tok, dk, dv = sb_bwd(proj, kv, dy, y_tok, dk, dv, f"l{l}_sb_bwd")
            per_b['b_w_out'][l - n_a] = d_w_out
        d_w_in = jnp.concatenate([mm_tn(hm, d_tok, name=f"l{l}_dwin_tok"),
                                  mm_tn(hm, dq_mem, name=f"l{l}_dwin_mem")], axis=1)
        (per_a['a_w_in'] if is_a else per_b['b_w_in'])[l if is_a else l - n_a] = d_w_in
        dh = mm([(d_tok, 0, w_in, tok_w, 0), (dq_mem, 0, w_in, MEM_W, tok_w // MEM_W)], tb=True,
                name=f"l{l}_mix_in_bwd")
        dx, dg = rms_bwd(x_mid, W['mix_norm'][l], dh, dx[0], f"l{l}_mix_norm_bwd")
        per_layer['mix_norm'][l] = dg.reshape(-1)
        per_layer['w_mem_kv'][l] = mm_tn(mem_h, d_mem_kv, name=f"l{l}_dw_mem_kv")
        d_mem_h = mm([(d_mem_kv, 0, (W['w_mem_kv'], l), 2 * MEM_W, 0)], tb=True, res=d_mem_h,
                     name=f"l{l}_mem_kv_bwd")

        dx, dg, dwg, dwu, dwd = _ffn_bwd(dx, s1, W['ffn1_norm'][l], *ffn_w('ffn1', l), f"l{l}_ffn1")
        per_layer['ffn1_norm'][l], per_layer['ffn1_w_gate'][l] = dg.reshape(-1), dwg
        per_layer['ffn1_w_up'][l], per_layer['ffn1_w_down'][l] = dwu, dwd
        if l == n_a:
            G['w_kv'] = jnp.concatenate([mm_tn(kvn, dk, name="dw_k"), mm_tn(kvn, dv, name="dw_v")], axis=1)
            d_kvn = mm([(dk, 0, w_kv, GM_W, 0), (dv, 0, w_kv, GM_W, 1)], tb=True, name="kv_proj_bwd")
            dx, dg = rms_bwd(x_kv, W['kv_norm'], d_kvn, dx[0], "kv_norm_bwd")
            G['kv_norm'] = dg.reshape(-1)

    _, dg = rms_bwd(mem, W['mem_norm'], d_mem_h, None, "mem_norm_bwd")
    G['mem_norm'] = dg.reshape(-1)
    for d in (per_layer, per_a, per_b):
        G.update(d)
    return loss, dx[0], G


def kernel(x, mem, ffn1_norm, ffn1_w_gate, ffn1_w_up, ffn1_w_down, mix_norm, ffn2_norm, ffn2_w_gate, ffn2_w_up, ffn2_w_down, mem_norm, w_mem_kv, a_w_in, a_v_norm, a_w_spatial, a_b_spatial, a_w_out, kv_norm, w_kv, b_w_in, b_w_out, final_norm, loss_target, m_ffn1_norm, m_ffn1_w_gate, m_ffn1_w_up, m_ffn1_w_down, m_mix_norm, m_ffn2_norm, m_ffn2_w_gate, m_ffn2_w_up, m_ffn2_w_down, m_mem_norm, m_w_mem_kv, m_a_w_in, m_a_v_norm, m_a_w_spatial, m_a_b_spatial, m_a_w_out, m_kv_norm, m_w_kv, m_b_w_in, m_b_w_out, m_final_norm, v_ffn1_norm, v_ffn1_w_gate, v_ffn1_w_up, v_ffn1_w_down, v_mix_norm, v_ffn2_norm, v_ffn2_w_gate, v_ffn2_w_up, v_ffn2_w_down, v_mem_norm, v_w_mem_kv, v_a_w_in, v_a_v_norm, v_a_w_spatial, v_a_b_spatial, v_a_w_out, v_kv_norm, v_w_kv, v_b_w_in, v_b_w_out, v_final_norm):
    weights = dict(zip(W_NAMES, [ffn1_norm, ffn1_w_gate, ffn1_w_up, ffn1_w_down, mix_norm, ffn2_norm, ffn2_w_gate,
                                 ffn2_w_up, ffn2_w_down, mem_norm, w_mem_kv, a_w_in, a_v_norm, a_w_spatial,
                                 a_b_spatial, a_w_out, kv_norm, w_kv, b_w_in, b_w_out, final_norm]))
    m_in = dict(zip(W_NAMES, [m_ffn1_norm, m_ffn1_w_gate, m_ffn1_w_up, m_ffn1_w_down, m_mix_norm, m_ffn2_norm,
                              m_ffn2_w_gate, m_ffn2_w_up, m_ffn2_w_down, m_mem_norm, m_w_mem_kv, m_a_w_in,
                              m_a_v_norm, m_a_w_spatial, m_a_b_spatial, m_a_w_out, m_kv_norm, m_w_kv, m_b_w_in,
                              m_b_w_out, m_final_norm]))
    v_in = dict(zip(W_NAMES, [v_ffn1_norm, v_ffn1_w_gate, v_ffn1_w_up, v_ffn1_w_down, v_mix_norm, v_ffn2_norm,
                              v_ffn2_w_gate, v_ffn2_w_up, v_ffn2_w_down, v_mem_norm, v_w_mem_kv, v_a_w_in,
                              v_a_v_norm, v_a_w_spatial, v_a_b_spatial, v_a_w_out, v_kv_norm, v_w_kv, v_b_w_in,
                              v_b_w_out, v_final_norm]))

    full = gather_weights({n: weights[n] for n in SHARD_AXIS})
    W = {n: (full[n] if n in SHARD_AXIS else weights[n]) for n in W_NAMES}
    loss, dx, grads = local_step(x[0], mem[0], W, loss_target[0])
    total = reduce_grads(grads, {n: weights[n].shape for n in SHARD_AXIS})
    loss = lax.psum(loss[0, 0], ("x", "y", "c"))

    deltas, new_m, new_v = {}, {}, {}
    for n in W_NAMES:
        deltas[n], new_m[n], new_v[n] = adamw(weights[n], total[n], m_in[n], v_in[n], "adamw_" + n)
    return (loss, dx[None], *[total[n] for n in W_NAMES], *[deltas[n] for n in W_NAMES],
            *[new_m[n] for n in W_NAMES], *[new_v[n] for n in W_NAMES])
```

```python
import functools
import math

import jax
import jax.numpy as jnp
from jax import lax
from jax.experimental import pallas as pl
from jax.experimental.pallas import tpu as pltpu

F32 = jnp.float32
BF16 = jnp.bfloat16
EPS = 1e-6
LANES = 128
ROW_TILE = 512
COL_TILE = 1408
PACK_ROWS = 256
PART_ROWS = 16
VMEM_LIMIT = 56 * 1024 * 1024

W_NAMES = ['ffn1_norm', 'ffn1_w_gate', 'ffn1_w_up', 'ffn1_w_down', 'mix_norm', 'ffn2_norm', 'ffn2_w_gate',
           'ffn2_w_up', 'ffn2_w_down', 'mem_norm', 'w_mem_kv', 'a_w_in', 'a_v_norm', 'a_w_spatial',
           'a_b_spatial', 'a_w_out', 'kv_norm', 'w_kv', 'b_w_in', 'b_w_out', 'final_norm']
SHARD_AXIS = {'ffn1_w_gate': 2, 'ffn1_w_up': 2, 'ffn1_w_down': 1, 'ffn2_w_gate': 2, 'ffn2_w_up': 2,
              'ffn2_w_down': 1, 'w_mem_kv': 1, 'a_w_in': 2, 'a_v_norm': 1, 'a_w_out': 1, 'w_kv': 1,
              'b_w_in': 1, 'b_w_out': 1}
N_CHIPS = 4
PACK_W = 1024

MEM_HEADS = 4
MEM_W = 256
HEAD_DIM = 64
GM_W = 768
GM_GROUPS = 6
GM_CHUNK = 128
CHUNK = 64
SB_BLOCK = 128
SB_Q_CHUNK = 1024
SB_DEAD = -110.0
SB_PAIRS = 2
SB_COLS = SB_PAIRS * LANES

ADAM_LR, ADAM_B1, ADAM_B2, ADAM_EPS, ADAM_WD, ADAM_STEP = 0.001, 0.9, 0.999, 1e-08, 0.01, 10

MESH = pl.DeviceIdType.MESH


def _cparams(sem=None):
    return pltpu.CompilerParams(dimension_semantics=sem, vmem_limit_bytes=VMEM_LIMIT)


def _tile(n, target, mult=LANES):
    best = None
    for t in range(mult, min(n, target) + 1, mult):
        if n % t == 0:
            best = t
    return best if best is not None else n


def _dot(a, b):
    return jnp.dot(a, b, preferred_element_type=F32)


def _dot_nt(a, b):
    return lax.dot_general(a, b, (((1,), (1,)), ((), ())), preferred_element_type=F32)


def _dot_tn(a, b):
    return lax.dot_general(a, b, (((0,), (0,)), ((), ())), preferred_element_type=F32)


def _bf(v):
    return v.astype(BF16)


def rms_fwd(x, gain, name):
    S, D = x.shape
    tm = _tile(S, ROW_TILE, 8)

    def body(x_ref, g_ref, o_ref):
        xf = x_ref[...]
        r = lax.rsqrt(jnp.mean(xf * xf, axis=-1, keepdims=True) + EPS)
        o_ref[...] = ((xf * r) * g_ref[...]).astype(o_ref.dtype)

    return pl.pallas_call(
        body, grid=(S // tm,),
        in_specs=[pl.BlockSpec((tm, D), lambda i: (i, 0)), pl.BlockSpec((1, D), lambda i: (0, 0))],
        out_specs=pl.BlockSpec((tm, D), lambda i: (i, 0)),
        out_shape=jax.ShapeDtypeStruct((S, D), BF16),
        compiler_params=_cparams(("parallel",)), name=name)(x, gain.reshape(1, D))


def rms_bwd(x, gain, dh, dres, name):
    S, D = x.shape
    tm = _tile(S, ROW_TILE, 16)
    has_res = dres is not None

    def body(*refs):
        if has_res:
            x_ref, g_ref, dh_ref, dres_ref, dx_ref, dxb_ref, dg_ref = refs
        else:
            x_ref, g_ref, dh_ref, dx_ref, dxb_ref, dg_ref = refs
        xf = x_ref[...]
        r = lax.rsqrt(jnp.mean(xf * xf, axis=-1, keepdims=True) + EPS)
        xhat = xf * r
        dhv = dh_ref[...].astype(F32)
        dxhat = dhv * g_ref[...]
        m = jnp.mean(dxhat * xhat, axis=-1, keepdims=True)
        dx = r * (dxhat - xhat * m)
        if has_res:
            dx = dx + dres_ref[...]
        dx_ref[...] = dx
        dxb_ref[...] = _bf(dx)

        @pl.when(pl.program_id(0) == 0)
        def _():
            dg_ref[...] = jnp.zeros_like(dg_ref)

        dg_ref[...] += jnp.sum(dhv * xhat, axis=0, keepdims=True)

    row = pl.BlockSpec((tm, D), lambda i: (i, 0))
    vec = pl.BlockSpec((1, D), lambda i: (0, 0))
    ins = [x, gain.reshape(1, D), dh] + ([dres] if has_res else [])
    in_specs = [row, vec, row] + ([row] if has_res else [])
    dx, dxb, dg = pl.pallas_call(
        body, grid=(S // tm,), in_specs=in_specs, out_specs=[row, row, vec],
        out_shape=[jax.ShapeDtypeStruct((S, D), F32), jax.ShapeDtypeStruct((S, D), BF16),
                   jax.ShapeDtypeStruct((1, D), F32)],
        compiler_params=_cparams(("arbitrary",)), name=name)(*ins)
    return (dx, dxb), dg


def final_loss(x, gain, target, name):
    S, D = x.shape
    tm = _tile(S, ROW_TILE, 16)

    def body(x_ref, g_ref, t_ref, loss_ref, dx_ref, dxb_ref, dg_ref):
        xf = x_ref[...]
        r = lax.rsqrt(jnp.mean(xf * xf, axis=-1, keepdims=True) + EPS)
        xhat = xf * r
        g = g_ref[...]
        diff = xhat * g - t_ref[...]
        dy = diff * (1.0 / D)
        dxhat = dy * g
        m = jnp.mean(dxhat * xhat, axis=-1, keepdims=True)
        dx = r * (dxhat - xhat * m)
        dx_ref[...] = dx
        dxb_ref[...] = _bf(dx)

        @pl.when(pl.program_id(0) == 0)
        def _():
            dg_ref[...] = jnp.zeros_like(dg_ref)
            loss_ref[...] = jnp.zeros_like(loss_ref)

        dg_ref[...] += jnp.sum(dy * xhat, axis=0, keepdims=True)
        per_tok = jnp.sum(diff * diff, axis=-1, keepdims=True) * (0.5 / D)
        loss_ref[...] += jnp.sum(per_tok, axis=0, keepdims=True)

    row = pl.BlockSpec((tm, D), lambda i: (i, 0))
    vec = pl.BlockSpec((1, D), lambda i: (0, 0))
    one = pl.BlockSpec((1, 1), lambda i: (0, 0))
    loss, dx, dxb, dg = pl.pallas_call(
        body, grid=(S // tm,), in_specs=[row, vec, row], out_specs=[one, row, row, vec],
        out_shape=[jax.ShapeDtypeStruct((1, 1), F32), jax.ShapeDtypeStruct((S, D), F32),
                   jax.ShapeDtypeStruct((S, D), BF16), jax.ShapeDtypeStruct((1, D), F32)],
        compiler_params=_cparams(("arbitrary",)), name=name)(x, gain.reshape(1, D), target)
    return loss, (dx, dxb), dg


def _wspec(w, blk, idx):
    _, l = w
    if l is None:
        return pl.BlockSpec(blk, idx)
    return pl.BlockSpec((None,) + blk, lambda *g: (l,) + idx(*g))


def mm(pairs, *, tb=False, res=None, scale=1.0, out_dtype=F32, name):
    M = pairs[0][0].shape[0]
    N = pairs[0][2][0].shape[-2 if tb else -1]
    tm = _tile(M, ROW_TILE, 8)
    tn = _tile(N, COL_TILE)
    n_p = len(pairs)
    has_res = res is not None

    def body(*refs):
        o_ref = refs[-1]
        acc = None
        for p in range(n_p):
            a = _bf(refs[2 * p][...])
            b = _bf(refs[2 * p + 1][...])
            d = _dot_nt(a, b) if tb else _dot(a, b)
            acc = d if acc is None else acc + d
        if scale != 1.0:
            acc = acc * scale
        if has_res:
            acc = acc + refs[2 * n_p][...]
        o_ref[...] = acc.astype(o_ref.dtype)

    ins, in_specs = [], []
    for a, cb, w, K, kb in pairs:
        ins += [a, w[0]]
        in_specs.append(pl.BlockSpec((tm, K), functools.partial(lambda j, i, cb: (i, cb), cb=cb)))
        if tb:
            in_specs.append(_wspec(w, (tn, K), functools.partial(lambda j, i, kb: (j, kb), kb=kb)))
        else:
            in_specs.append(_wspec(w, (K, tn), functools.partial(lambda j, i, kb: (kb, j), kb=kb)))
    if has_res:
        ins.append(res)
        in_specs.append(pl.BlockSpec((tm, tn), lambda j, i: (i, j)))
    return pl.pallas_call(
        body, grid=(N // tn, M // tm), in_specs=in_specs,
        out_specs=pl.BlockSpec((tm, tn), lambda j, i: (i, j)),
        out_shape=jax.ShapeDtypeStruct((M, N), out_dtype),
        compiler_params=_cparams(("parallel", "parallel")), name=name)(*ins)


def mm_tn(a, b, *, a_cb=0, a_w=None, b_cb=0, b_w=None, scale=1.0, name):
    S = a.shape[0]
    Ka = a.shape[1] if a_w is None else a_w
    Nb = b.shape[1] if b_w is None else b_w
    ts = _tile(S, 2 * ROW_TILE, 8)
    tk = _tile(Ka, COL_TILE)
    tn = _tile(Nb, COL_TILE)
    a_off, b_off = a_cb * (Ka // tk), b_cb * (Nb // tn)

    def body(a_ref, b_ref, o_ref):
        s = pl.program_id(2)

        @pl.when(s == 0)
        def _():
            o_ref[...] = jnp.zeros_like(o_ref)

        o_ref[...] += _dot_tn(_bf(a_ref[...]), _bf(b_ref[...]))
        if scale != 1.0:
            @pl.when(s == pl.num_programs(2) - 1)
            def _():
                o_ref[...] = o_ref[...] * scale

    return pl.pallas_call(
        body, grid=(Ka // tk, Nb // tn, S // ts),
        in_specs=[pl.BlockSpec((ts, tk), lambda k, n, s: (s, a_off + k)),
                  pl.BlockSpec((ts, tn), lambda k, n, s: (s, b_off + n))],
        out_specs=pl.BlockSpec((tk, tn), lambda k, n, s: (k, n)),
        out_shape=jax.ShapeDtypeStruct((Ka, Nb), F32),
        compiler_params=_cparams(("parallel", "parallel", "arbitrary")), name=name)(a, b)


def _sigmoid(x):
    return 0.5 * jnp.tanh(0.5 * x) + 0.5


def ffn_up(h, wg, wu, name):
    S, D = h.shape
    Fd = wg[0].shape[-1]
    tm = _tile(S, ROW_TILE, 8)
    tn = _tile(Fd, COL_TILE)

    def body(h_ref, wg_ref, wu_ref, g_ref, u_ref, a_ref):
        hv = h_ref[...]
        g = _dot(hv, wg_ref[...])
        u = _dot(hv, wu_ref[...])
        g_ref[...] = _bf(g)
        u_ref[...] = _bf(u)
        a_ref[...] = _bf(g * _sigmoid(g) * u)

    blk = pl.BlockSpec((tm, tn), lambda j, i: (i, j))
    sh = jax.ShapeDtypeStruct((S, Fd), BF16)
    return pl.pallas_call(
        body, grid=(Fd // tn, S // tm),
        in_specs=[pl.BlockSpec((tm, D), lambda j, i: (i, 0)), _wspec(wg, (D, tn), lambda j, i: (0, j)),
                  _wspec(wu, (D, tn), lambda j, i: (0, j))],
        out_specs=[blk, blk, blk], out_shape=[sh, sh, sh],
        compiler_params=_cparams(("parallel", "parallel")), name=name)(h, wg[0], wu[0])


def ffn_down_bwd(dxo, wd, g, u, name):
    S, D = dxo.shape
    Fd = wd[0].shape[-2]
    tm = _tile(S, ROW_TILE, 8)
    tn = _tile(Fd, COL_TILE)

    def body(dx_ref, wd_ref, g_ref, u_ref, dg_ref, du_ref):
        da = _dot_nt(_bf(dx_ref[...]), wd_ref[...]) * 0.5
        g = g_ref[...].astype(F32)
        u = u_ref[...].astype(F32)
        sg = _sigmoid(g)
        dg_ref[...] = _bf(da * u * (sg * (1.0 + g * (1.0 - sg))))
        du_ref[...] = _bf(da * (g * sg))

    blk = pl.BlockSpec((tm, tn), lambda j, i: (i, j))
    sh = jax.ShapeDtypeStruct((S, Fd), BF16)
    return pl.pallas_call(
        body, grid=(Fd // tn, S // tm),
        in_specs=[pl.BlockSpec((tm, D), lambda j, i: (i, 0)), _wspec(wd, (tn, D), lambda j, i: (j, 0)),
                  blk, blk],
        out_specs=[blk, blk], out_shape=[sh, sh],
        compiler_params=_cparams(("parallel", "parallel")), name=name)(dxo, wd[0], g, u)


_GELU_C = math.sqrt(2.0 / math.pi)


def _gelu(p):
    return 0.5 * p * (1.0 + jnp.tanh(_GELU_C * (p + 0.044715 * (p * p * p))))


def _gelu_grad(p):
    th = jnp.tanh(_GELU_C * (p + 0.044715 * (p * p * p)))
    return 0.5 * (1.0 + th) + 0.5 * p * (1.0 - th * th) * (_GELU_C * (1.0 + 3.0 * 0.044715 * (p * p)))


def _chunk_mask():
    t = lax.broadcasted_iota(jnp.int32, (GM_CHUNK, GM_CHUNK), 0)
    s = lax.broadcasted_iota(jnp.int32, (GM_CHUNK, GM_CHUNK), 1)
    return (s // CHUNK) <= (t // CHUNK)


def gmlp_fwd(proj, v_gain, w_s, b_mat, name):
    S = proj.shape[0]
    tw = _tile(S, ROW_TILE, GM_CHUNK)
    n_win = tw // GM_CHUNK

    def body(p_ref, gain_ref, w_ref, b_ref, y_ref):
        mask = _chunk_mask()
        v = _gelu(p_ref[:, GM_W:])
        r = lax.rsqrt(jnp.mean(v * v, axis=-1, keepdims=True) + EPS)
        vn = _bf(v * r * gain_ref[...])
        for g in range(GM_GROUPS):
            wm = _bf(jnp.where(mask, w_ref[g], 0.0))
            cs = slice(g * GM_CHUNK, (g + 1) * GM_CHUNK)
            for w in range(n_win):
                rs = slice(w * GM_CHUNK, (w + 1) * GM_CHUNK)
                mixed = _dot(wm, vn[rs, cs]) + b_ref[g]
                y_ref[rs, cs] = _bf(_gelu(p_ref[rs, cs]) * mixed)

    return pl.pallas_call(
        body, grid=(S // tw,),
        in_specs=[pl.BlockSpec((tw, 2 * GM_W), lambda i: (i, 0)), pl.BlockSpec((1, GM_W), lambda i: (0, 0)),
                  pl.BlockSpec((GM_GROUPS, GM_CHUNK, GM_CHUNK), lambda i: (0, 0, 0)),
                  pl.BlockSpec((GM_GROUPS, GM_CHUNK, GM_CHUNK), lambda i: (0, 0, 0))],
        out_specs=pl.BlockSpec((tw, GM_W), lambda i: (i, 0)),
        out_shape=jax.ShapeDtypeStruct((S, GM_W), BF16),
        compiler_params=_cparams(("parallel",)), name=name)(proj, v_gain.reshape(1, GM_W), w_s, b_mat)


def gmlp_bwd(proj, dy, v_gain, w_s, b_mat, name):
    S = proj.shape[0]
    tw = _tile(S, ROW_TILE, GM_CHUNK)
    n_win = tw // GM_CHUNK

    def body(p_ref, dy_ref, gain_ref, w_ref, b_ref, dp_ref, dw_ref, db_ref, dgain_ref, dvn_ref):
        step = pl.program_id(0)

        @pl.when(step == 0)
        def _():
            dw_ref[...] = jnp.zeros_like(dw_ref)
            db_ref[...] = jnp.zeros_like(db_ref)
            dgain_ref[...] = jnp.zeros_like(dgain_ref)

        mask = _chunk_mask()
        pv = p_ref[:, GM_W:]
        v = _gelu(pv)
        r = lax.rsqrt(jnp.mean(v * v, axis=-1, keepdims=True) + EPS)
        vhat = v * r
        gain = gain_ref[...]
        vn = _bf(vhat * gain)
        for g in range(GM_GROUPS):
            wm = _bf(jnp.where(mask, w_ref[g], 0.0))
            cs = slice(g * GM_CHUNK, (g + 1) * GM_CHUNK)
            dw_acc = jnp.zeros((GM_CHUNK, GM_CHUNK), F32)
            db_acc = jnp.zeros((GM_CHUNK, GM_CHUNK), F32)
            for w in range(n_win):
                rs = slice(w * GM_CHUNK, (w + 1) * GM_CHUNK)
                pu = p_ref[rs, cs]
                u = _gelu(pu)
                vn_blk = vn[rs, cs]
                mixed = _dot(wm, vn_blk) + b_ref[g]
                dyb = dy_ref[rs, cs]
                dp_ref[rs, cs] = _bf(dyb * mixed * _gelu_grad(pu))
                dmix = dyb * u
                db_acc = db_acc + dmix
                dmix_b = _bf(dmix)
                dw_acc = dw_acc + _dot_nt(dmix_b, vn_blk)
                dvn_ref[rs, cs] = _dot_tn(wm, dmix_b)
            dw_ref[g] += jnp.where(mask, dw_acc, 0.0)
            db_ref[g] += jnp.broadcast_to(jnp.sum(db_acc, axis=-1, keepdims=True), (GM_CHUNK, GM_CHUNK))
        dvn = dvn_ref[...]
        dgain_ref[...] += jnp.sum(dvn * vhat, axis=0, keepdims=True)
        dvhat = dvn * gain
        m = jnp.mean(dvhat * vhat, axis=-1, keepdims=True)
        dv = r * (dvhat - vhat * m)
        dp_ref[:, GM_W:] = _bf(dv * _gelu_grad(pv))

    sq = pl.BlockSpec((GM_GROUPS, GM_CHUNK, GM_CHUNK), lambda i: (0, 0, 0))
    sq_sh = jax.ShapeDtypeStruct((GM_GROUPS, GM_CHUNK, GM_CHUNK), F32)
    return pl.pallas_call(
        body, grid=(S // tw,),
        in_specs=[pl.BlockSpec((tw, 2 * GM_W), lambda i: (i, 0)), pl.BlockSpec((tw, GM_W), lambda i: (i, 0)),
                  pl.BlockSpec((1, GM_W), lambda i: (0, 0)), sq, sq],
        out_specs=[pl.BlockSpec((tw, 2 * GM_W), lambda i: (i, 0)), sq, sq,
                   pl.BlockSpec((1, GM_W), lambda i: (0, 0))],
        out_shape=[jax.ShapeDtypeStruct((S, 2 * GM_W), BF16), sq_sh, sq_sh,
                   jax.ShapeDtypeStruct((1, GM_W), F32)],
        scratch_shapes=[pltpu.VMEM((tw, GM_W), F32)],
        compiler_params=_cparams(("arbitrary",)), name=name)(proj, dy, v_gain.reshape(1, GM_W), w_s, b_mat)


def _head_mask(h, width):
    lane = lax.broadcasted_iota(jnp.int32, (1, width), 1)
    return (lane >= HEAD_DIM * h) & (lane < HEAD_DIM * (h + 1))


def _mem_probs(q, k, h):
    kh = jnp.where(_head_mask(h, MEM_W), k, jnp.zeros_like(k))
    s = _dot_nt(q, kh) * (HEAD_DIM ** -0.5)
    s = s - jnp.max(s, axis=-1, keepdims=True)
    e = jnp.exp(s)
    return e * (1.0 / jnp.sum(e, axis=-1, keepdims=True)), kh


def memattn_fwd(proj, q_cb, mem_kv, name):
    S = proj.shape[0]
    NM = mem_kv.shape[0]
    tm = _tile(S, ROW_TILE, 8)

    def body(q_ref, kv_ref, o_ref):
        q = _bf(q_ref[...])
        k = _bf(kv_ref[:, :MEM_W])
        v = _bf(kv_ref[:, MEM_W:])
        acc = jnp.zeros((tm, MEM_W), F32)
        for h in range(MEM_HEADS):
            p, _ = _mem_probs(q, k, h)
            vh = jnp.where(_head_mask(h, MEM_W), v, jnp.zeros_like(v))
            acc = acc + _dot(_bf(p), vh)
        o_ref[...] = _bf(acc)

    return pl.pallas_call(
        body, grid=(S // tm,),
        in_specs=[pl.BlockSpec((tm, MEM_W), lambda i: (i, q_cb)), pl.BlockSpec((NM, 2 * MEM_W), lambda i: (0, 0))],
        out_specs=pl.BlockSpec((tm, MEM_W), lambda i: (i, 0)),
        out_shape=jax.ShapeDtypeStruct((S, MEM_W), BF16),
        compiler_params=_cparams(("parallel",)), name=name)(proj, mem_kv)


def memattn_bwd(proj, q_cb, mem_kv, dy, dy_cb, name):
    S = proj.shape[0]
    NM = mem_kv.shape[0]
    tm = _tile(S, ROW_TILE, 8)

    def body(q_ref, kv_ref, do_ref, dq_ref, dkv_ref):
        @pl.when(pl.program_id(0) == 0)
        def _():
            dkv_ref[...] = jnp.zeros_like(dkv_ref)

        q = _bf(q_ref[...])
        k = _bf(kv_ref[:, :MEM_W])
        v = _bf(kv_ref[:, MEM_W:])
        do = _bf(do_ref[...])
        dq = jnp.zeros((tm, MEM_W), F32)
        dk = jnp.zeros((NM, MEM_W), F32)
        dv = jnp.zeros((NM, MEM_W), F32)
        for h in range(MEM_HEADS):
            hm = _head_mask(h, MEM_W)
            p, kh = _mem_probs(q, k, h)
            vh = jnp.where(hm, v, jnp.zeros_like(v))
            dp = _dot_nt(do, vh)
            ds = _bf(p * (dp - jnp.sum(p * dp, axis=-1, keepdims=True)) * (HEAD_DIM ** -0.5))
            dq = dq + _dot(ds, kh)
            dk = dk + jnp.where(hm, _dot_tn(ds, q), 0.0)
            dv = dv + jnp.where(hm, _dot_tn(_bf(p), do), 0.0)
        dq_ref[...] = _bf(dq)
        dkv_ref[:, :MEM_W] += dk
        dkv_ref[:, MEM_W:] += dv

    return pl.pallas_call(
        body, grid=(S // tm,),
        in_specs=[pl.BlockSpec((tm, MEM_W), lambda i: (i, q_cb)), pl.BlockSpec((NM, 2 * MEM_W), lambda i: (0, 0)),
                  pl.BlockSpec((tm, MEM_W), lambda i: (i, dy_cb))],
        out_specs=[pl.BlockSpec((tm, MEM_W), lambda i: (i, 0)), pl.BlockSpec((NM, 2 * MEM_W), lambda i: (0, 0))],
        out_shape=[jax.ShapeDtypeStruct((S, MEM_W), BF16), jax.ShapeDtypeStruct((NM, 2 * MEM_W), F32)],
        compiler_params=_cparams(("arbitrary",)), name=name)(proj, mem_kv, dy)


def _split_bf(v):
    hi = _bf(v)
    return hi, _bf(v - hi.astype(F32))


def _sb_scores(qp, k2, dpos, row, col, tri):
    z = _dot_nt(qp, k2) * (HEAD_DIM ** -0.5)
    t = jnp.log(1.0 + jnp.exp(-jnp.abs(z)))
    lb = jnp.minimum(z, 0.0) - t
    lom = jnp.where(_sb_mask(dpos, row, col), -jnp.maximum(z, 0.0) - t, 0.0)
    l_hi, l_lo = _split_bf(lom)
    insuf = _dot(l_hi, tri) + _dot(l_lo, tri)
    return lb + insuf, jnp.sum(lom, axis=1, keepdims=True), lb


def _sb_mask(dpos, row, col):
    return (col - row) < dpos


def _sb_weights(s, run, dpos, row, col):
    return jnp.where(_sb_mask(dpos, row, col), jnp.exp(s + run), 0.0)


def _sb_alive(runs):
    top = runs[0]
    for r in runs[1:]:
        top = jnp.maximum(top, r)
    return jnp.max(top) > SB_DEAD


def _sb_consts():
    row = lax.broadcasted_iota(jnp.int32, (2 * SB_BLOCK, SB_BLOCK), 0) % SB_BLOCK
    col = lax.broadcasted_iota(jnp.int32, (2 * SB_BLOCK, SB_BLOCK), 1)
    r = lax.broadcasted_iota(jnp.int32, (SB_BLOCK, SB_BLOCK), 0)
    c = lax.broadcasted_iota(jnp.int32, (SB_BLOCK, SB_BLOCK), 1)
    return row, col, _bf(jnp.where(r > c, 1.0, 0.0)), _bf(jnp.where(r >= c, 1.0, 0.0))


def _stack_heads(x2, masks):
    return jnp.concatenate([_bf(jnp.where(hm, x2, jnp.zeros_like(x2))) for hm in masks], axis=0)


def _side_by_side(x):
    return jnp.concatenate([x[:SB_BLOCK], x[SB_BLOCK:]], axis=1)


def sb_fwd(proj, kv, name):
    S = proj.shape[0]
    n_g = GM_W // SB_COLS
    tq = _tile(S, SB_Q_CHUNK, SB_BLOCK)
    nqb = tq // SB_BLOCK

    def body(q_ref, k_ref, v_ref, o_ref, s_ref):
        chunk = pl.program_id(1)
        row, col, tri, _ = _sb_consts()
        masks = [_head_mask(h, LANES) for h in range(2)]
        pairs = [slice(p * LANES, (p + 1) * LANES) for p in range(SB_PAIRS)]

        def q_loop(qi, _):
            i = chunk * nqb + qi
            rows = pl.ds(pl.multiple_of(qi * SB_BLOCK, SB_BLOCK), SB_BLOCK)
            qps = [_stack_heads(q_ref[rows, cs], masks) for cs in pairs]

            def look_ahead(t):
                keys = pl.ds(pl.multiple_of(jnp.maximum(i - t, 0) * SB_BLOCK, SB_BLOCK), SB_BLOCK)
                sums = []
                for p, cs in enumerate(pairs):
                    s, lom_sum, _ = _sb_scores(qps[p], k_ref[keys, cs], t * SB_BLOCK, row, col, tri)
                    s_ref[p] = s
                    sums.append(lom_sum)
                return tuple(sums)

            def k_step(carry):
                t, accs, runs, sums, _ = carry
                keys = pl.ds(pl.multiple_of((i - t) * SB_BLOCK, SB_BLOCK), SB_BLOCK)
                new_accs, new_runs = [], []
                for p, cs in enumerate(pairs):
                    a = _sb_weights(s_ref[p], runs[p], t * SB_BLOCK, row, col)
                    a_hi, a_lo = _split_bf(_side_by_side(a))
                    vp = _stack_heads(v_ref[keys, cs], masks)
                    new_accs.append(accs[p] + _dot(a_hi, vp) + _dot(a_lo, vp))
                    new_runs.append(runs[p] + sums[p])
                return t + 1, tuple(new_accs), tuple(new_runs), look_ahead(t + 1), _sb_alive(new_runs)

            zero = jnp.zeros((2 * SB_BLOCK, 1), F32)
            _, accs, _, _, _ = lax.while_loop(
                lambda carry: jnp.logical_and(carry[0] <= i, carry[4]), k_step,
                (jnp.int32(0), (jnp.zeros((SB_BLOCK, LANES), F32),) * SB_PAIRS, (zero,) * SB_PAIRS,
                 look_ahead(jnp.int32(0)), jnp.bool_(True)))
            for p, cs in enumerate(pairs):
                o_ref[rows, cs] = accs[p]
            return 0

        lax.fori_loop(0, nqb, q_loop, 0)

    return pl.pallas_call(
        body, grid=(n_g, S // tq),
        in_specs=[pl.BlockSpec((tq, SB_COLS), lambda g, c: (c, g)),
                  pl.BlockSpec((S, SB_COLS), lambda g, c: (0, g)),
                  pl.BlockSpec((S, SB_COLS), lambda g, c: (0, n_g + g))],
        out_specs=pl.BlockSpec((tq, SB_COLS), lambda g, c: (c, g)),
        out_shape=jax.ShapeDtypeStruct((S, GM_W), F32),
        scratch_shapes=[pltpu.VMEM((SB_PAIRS, 2 * SB_BLOCK, LANES), F32)],
        compiler_params=_cparams(("parallel", "parallel")), name=name)(proj, kv, kv)


def sb_bwd(proj, kv, dy, out, dk_init, dv_init, name):
    S = proj.shape[0]
    n_g = GM_W // SB_COLS
    tq = _tile(S, SB_Q_CHUNK, SB_BLOCK)
    nqb = tq // SB_BLOCK
    n_chunks = S // tq
    has_init = dk_init is not None
    scale = HEAD_DIM ** -0.5

    def body(*refs):
        s_ref, beta_ref, da_ref = refs[-3:]
        if has_init:
            q_ref, k_ref, v_ref, do_ref, out_ref, dki_ref, dvi_ref, dq_ref, dko_ref, dvo_ref, dk_acc, dv_acc = refs[:-3]
        else:
            q_ref, k_ref, v_ref, do_ref, out_ref, dq_ref, dko_ref, dvo_ref, dk_acc, dv_acc = refs[:-3]
        chunk = pl.program_id(1)
        cols = pl.ds(pl.multiple_of(pl.program_id(0) * SB_COLS, SB_COLS), SB_COLS)

        @pl.when(chunk == 0)
        def _():
            if has_init:
                pltpu.sync_copy(dki_ref.at[:, cols], dk_acc)
                pltpu.sync_copy(dvi_ref.at[:, cols], dv_acc)
            else:
                dk_acc[...] = jnp.zeros_like(dk_acc)
                dv_acc[...] = jnp.zeros_like(dv_acc)

        row, col, tri, tri_inc = _sb_consts()
        masks = [_head_mask(h, LANES) for h in range(2)]
        pairs = [slice(p * LANES, (p + 1) * LANES) for p in range(SB_PAIRS)]

        def q_loop(qi, _):
            i = chunk * nqb + qi
            rows = pl.ds(pl.multiple_of(qi * SB_BLOCK, SB_BLOCK), SB_BLOCK)
            qps = [_stack_heads(q_ref[rows, cs], masks) for cs in pairs]
            dops = [_stack_heads(do_ref[rows, cs], masks) for cs in pairs]
            e_tots = [jnp.sum(dop.astype(F32) * jnp.concatenate([out_ref[rows, cs]] * 2, axis=0), axis=1,
                              keepdims=True) for dop, cs in zip(dops, pairs)]

            def look_ahead(t):
                keys = pl.ds(pl.multiple_of(jnp.maximum(i - t, 0) * SB_BLOCK, SB_BLOCK), SB_BLOCK)
                sums = []
                for p, cs in enumerate(pairs):
                    s, lom_sum, lb = _sb_scores(qps[p], k_ref[keys, cs], t * SB_BLOCK, row, col, tri)
                    s_ref[p] = s
                    beta_ref[p] = jnp.exp(lb)
                    da_ref[p] = _dot_nt(dops[p], v_ref[keys, cs])
                    sums.append(lom_sum)
                return tuple(sums)

            def k_step(carry):
                t, dqs, runs, e_runs, sums, _ = carry
                keys = pl.ds(pl.multiple_of((i - t) * SB_BLOCK, SB_BLOCK), SB_BLOCK)
                new_dqs, new_runs, new_e_runs = [], [], []
                for p, cs in enumerate(pairs):
                    a = _sb_weights(s_ref[p], runs[p], t * SB_BLOCK, row, col)
                    e = a * da_ref[p]
                    e_hi, e_lo = _split_bf(e)
                    before = e_tots[p] - e_runs[p] - (_dot(e_hi, tri_inc) + _dot(e_lo, tri_inc))
                    beta = beta_ref[p]
                    dz = jnp.where(_sb_mask(t * SB_BLOCK, row, col), e * (1.0 - beta) - before * beta, 0.0)
                    dz = _bf(dz * scale)
                    new_dqs.append(dqs[p] + _dot(_side_by_side(dz), _stack_heads(k_ref[keys, cs], masks)))
                    dk_acc[keys, cs] += _dot_tn(dz, qps[p])
                    dv_acc[keys, cs] += _dot_tn(_bf(a), dops[p])
                    new_runs.append(runs[p] + sums[p])
                    new_e_runs.append(e_runs[p] + jnp.sum(e, axis=1, keepdims=True))
                return (t + 1, tuple(new_dqs), tuple(new_runs), tuple(new_e_runs), look_ahead(t + 1),
                        _sb_alive(new_runs))

            zero = jnp.zeros((2 * SB_BLOCK, 1), F32)
            zeros = (zero,) * SB_PAIRS
            _, dqs, _, _, _, _ = lax.while_loop(
                lambda carry: jnp.logical_and(carry[0] <= i, carry[5]), k_step,
                (jnp.int32(0), (jnp.zeros((SB_BLOCK, LANES), F32),) * SB_PAIRS, zeros, zeros,
                 look_ahead(jnp.int32(0)), jnp.bool_(True)))
            for p, cs in enumerate(pairs):
                dq_ref[rows, cs] = dqs[p]
            return 0

        lax.fori_loop(0, nqb, q_loop, 0)

        @pl.when(chunk == n_chunks - 1)
        def _():
            pltpu.sync_copy(dk_acc, dko_ref.at[:, cols])
            pltpu.sync_copy(dv_acc, dvo_ref.at[:, cols])

    qspec = pl.BlockSpec((tq, SB_COLS), lambda g, c: (c, g))
    kspec = pl.BlockSpec((S, SB_COLS), lambda g, c: (0, g))
    in_specs = [qspec, kspec, pl.BlockSpec((S, SB_COLS), lambda g, c: (0, n_g + g)), qspec, qspec]
    ins = [proj, kv, kv, dy, out]
    if has_init:
        in_specs += [_ANY, _ANY]
        ins += [dk_init, dv_init]
    sh = jax.ShapeDtypeStruct((S, GM_W), F32)
    return pl.pallas_call(
        body, grid=(n_g, n_chunks), in_specs=in_specs, out_specs=[qspec, _ANY, _ANY],
        out_shape=[sh, sh, sh],
        scratch_shapes=[pltpu.VMEM((S, SB_COLS), F32), pltpu.VMEM((S, SB_COLS), F32)]
        + [pltpu.VMEM((SB_PAIRS, 2 * SB_BLOCK, LANES), F32)] * 3,
        compiler_params=_cparams(("arbitrary", "arbitrary")), name=name)(*ins)


def adamw(w, g, m, v, name):
    shape = w.shape
    C = shape[-1] if w.ndim > 1 else shape[0]
    R = w.size // C
    tr = _tile(R, ROW_TILE, 8)

    def body(w_ref, g_ref, m_ref, v_ref, d_ref, nm_ref, nv_ref):
        gv = g_ref[...]
        m2 = ADAM_B1 * m_ref[...] + (1.0 - ADAM_B1) * gv
        v2 = ADAM_B2 * v_ref[...] + (1.0 - ADAM_B2) * (gv * gv)
        m_hat = m2 / (1.0 - ADAM_B1 ** ADAM_STEP)
        v_hat = v2 / (1.0 - ADAM_B2 ** ADAM_STEP)
        d_ref[...] = -ADAM_LR * (m_hat / (jnp.sqrt(v_hat) + ADAM_EPS) + ADAM_WD * w_ref[...])
        nm_ref[...] = m2
        nv_ref[...] = v2

    blk = pl.BlockSpec((tr, C), lambda i: (i, 0))
    sh = jax.ShapeDtypeStruct((R, C), F32)
    outs = pl.pallas_call(
        body, grid=(R // tr,), in_specs=[blk] * 4, out_specs=[blk] * 3, out_shape=[sh] * 3,
        compiler_params=_cparams(("parallel",)), name=name)(
            w.reshape(R, C), g.reshape(R, C), m.reshape(R, C), v.reshape(R, C))
    return tuple(o.reshape(shape) for o in outs)


def _place():
    return lax.axis_index("x"), lax.axis_index("y"), lax.axis_index("c")


def _other_chips(x, y):
    return [(1 - x, y), (x, 1 - y), (1 - x, 1 - y)]


_ANY = pl.BlockSpec(memory_space=pl.ANY)
LOCAL_CHUNKS = 8
AG_CHUNKS = 4


def _local_copies(src_of, dst_of, n_rows, sems, sem_base):
    rows = n_rows // LOCAL_CHUNKS
    copies = []
    for k in range(LOCAL_CHUNKS):
        r = pl.ds(k * rows, rows)
        cp = pltpu.make_async_copy(src_of(r), dst_of(r), sems.at[sem_base + k])
        cp.start()
        copies.append(cp)
    return copies


def _remote(src, dst, send_sems, recv_sems, k, to):
    return pltpu.make_async_remote_copy(src_ref=src, dst_ref=dst, send_sem=send_sems.at[k], recv_sem=recv_sems.at[k],
                                        device_id=to, device_id_type=MESH)


def all_gather_chips(bufs, name):
    nb = len(bufs)

    def body(*refs):
        in_refs, out_refs = refs[:nb], refs[nb:2 * nb]
        send_sems, recv_sems, local_sems = refs[2 * nb:]
        x, y, c = _place()
        sibling = (x, y, 1 - c)
        chips = _other_chips(x, y)

        n_sem = 2 * 3 * AG_CHUNKS

        def piece(b, px, py, pc, q):
            rows = bufs[b].shape[0] // (2 * AG_CHUNKS)
            return out_refs[b].at[2 * px + py, pl.ds((pc * AG_CHUNKS + q) * rows, rows), :]

        def my_piece(b, q):
            rows = bufs[b].shape[0] // (2 * AG_CHUNKS)
            return in_refs[b].at[pl.ds((c * AG_CHUNKS + q) * rows, rows), :]

        def sem(b, stage, j, q):
            return n_sem * b + (stage * 3 + j) * AG_CHUNKS + q

        mine, first, passed = [], [], []
        for b in range(nb):
            mine += _local_copies(lambda r, b=b: in_refs[b].at[r, :], lambda r, b=b: out_refs[b].at[2 * x + y, r, :],
                                  bufs[b].shape[0], local_sems, b * LOCAL_CHUNKS)
        for q in range(AG_CHUNKS):
            for b in range(nb):
                for j, chip in enumerate(chips):
                    cp = _remote(my_piece(b, q), piece(b, x, y, c, q), send_sems, recv_sems, sem(b, 0, j, q),
                                 (*chip, c))
                    cp.start()
                    first.append(cp)
        for q in range(AG_CHUNKS):
            for j, chip in enumerate(chips):
                for b in range(nb):
                    landed = piece(b, *chip, c, q)
                    _remote(landed, landed, send_sems, recv_sems, sem(b, 0, j, q), (*chip, c)).wait_recv()
                    cp = _remote(landed, landed, send_sems, recv_sems, sem(b, 1, j, q), sibling)
                    cp.start()
                    passed.append(cp)
        for q in range(AG_CHUNKS):
            for j, chip in enumerate(chips):
                for b in range(nb):
                    landed = piece(b, *chip, 1 - c, q)
                    _remote(landed, landed, send_sems, recv_sems, sem(b, 1, j, q), sibling).wait_recv()
        for cp in first + passed:
            cp.wait_send()
        for cp in mine:
            cp.wait()

    n_sems = 2 * 3 * AG_CHUNKS * nb
    return pl.pallas_call(
        body, in_specs=[_ANY] * nb, out_specs=[_ANY] * nb,
        out_shape=[jax.ShapeDtypeStruct((N_CHIPS,) + b.shape, b.dtype) for b in bufs],
        scratch_shapes=[pltpu.SemaphoreType.DMA((n_sems,)), pltpu.SemaphoreType.DMA((n_sems,)),
                        pltpu.SemaphoreType.DMA((LOCAL_CHUNKS * nb,))],
        name=name)(*bufs)


def swap_halves(gs, name):
    nb = len(gs)
    n = N_CHIPS

    def body(*refs):
        g_refs, t_refs = refs[:nb], refs[nb:2 * nb]
        send_sems, recv_sems = refs[2 * nb:]
        x, y, c = _place()
        gives = []
        for b in range(nb):
            Ph = gs[b].shape[1] // 2
            for k in range(n):
                gives.append(_remote(g_refs[b].at[k, pl.ds((1 - c) * Ph, Ph), :], t_refs[b].at[k], send_sems,
                                     recv_sems, n * b + k, (x, y, 1 - c)))
        for cp in gives:
            cp.start()
        for cp in gives:
            cp.wait()

    return pl.pallas_call(
        body, in_specs=[_ANY] * nb, out_specs=[_ANY] * nb,
        out_shape=[jax.ShapeDtypeStruct((n, g.shape[1] // 2, g.shape[2]), g.dtype) for g in gs],
        scratch_shapes=[pltpu.SemaphoreType.DMA((n * nb,)), pltpu.SemaphoreType.DMA((n * nb,))],
        name=name)(*gs)


def _core_index():
    return lax.axis_index("c").astype(jnp.int32).reshape(1)


def add_cores(g, theirs, name):
    n, Ph, Wd = theirs.shape
    tr = _tile(Ph, ROW_TILE, 16)
    steps = Ph // tr

    def body(c_ref, g_ref, t_ref, o_ref):
        o_ref[...] = _bf(g_ref[...] + t_ref[...])

    blk = pl.BlockSpec((None, tr, Wd), lambda k, i, c_ref: (k, i, 0))
    return pl.pallas_call(
        body,
        grid_spec=pltpu.PrefetchScalarGridSpec(
            num_scalar_prefetch=1, grid=(n, steps),
            in_specs=[pl.BlockSpec((None, tr, Wd), lambda k, i, c_ref: (k, c_ref[0] * steps + i, 0)), blk],
            out_specs=blk),
        out_shape=jax.ShapeDtypeStruct((n, Ph, Wd), BF16),
        compiler_params=_cparams(("parallel", "parallel")), name=name)(_core_index(), g, theirs)


def add_chips(slots, name):
    n, Ph, Wd = slots.shape
    tr = _tile(Ph, ROW_TILE, 16)
    steps = Ph // tr

    def body(c_ref, *refs):
        acc = refs[0][...].astype(F32)
        for k in range(1, n):
            acc = acc + refs[k][...].astype(F32)
        refs[n][...] = acc

    in_specs = [pl.BlockSpec((None, tr, Wd), functools.partial(lambda i, c_ref, k: (k, i, 0), k=k)) for k in range(n)]
    return pl.pallas_call(
        body,
        grid_spec=pltpu.PrefetchScalarGridSpec(
            num_scalar_prefetch=1, grid=(steps,), in_specs=in_specs,
            out_specs=pl.BlockSpec((tr, Wd), lambda i, c_ref: (c_ref[0] * steps + i, 0))),
        out_shape=jax.ShapeDtypeStruct((2 * Ph, Wd), F32),
        compiler_params=_cparams(("parallel",)), name=name)(_core_index(), *([slots] * n))


def scatter_to_chips(hs, name):
    nb = len(hs)

    def body(*refs):
        h_refs, out_refs = refs[:nb], refs[nb:2 * nb]
        send_sems, recv_sems, local_sems = refs[2 * nb:]
        x, y, c = _place()
        me = 2 * x + y
        chips = _other_chips(x, y)
        keep, sends = [], []
        for b in range(nb):
            keep += _local_copies(lambda r, b=b: h_refs[b].at[me, r, :], lambda r, b=b: out_refs[b].at[me, r, :],
                                  hs[b].shape[1], local_sems, b * LOCAL_CHUNKS)
            for j, (cx, cy) in enumerate(chips):
                cp = _remote(h_refs[b].at[2 * cx + cy], out_refs[b].at[me], send_sems, recv_sems, 3 * b + j,
                             (cx, cy, c))
                cp.start()
                sends.append(cp)
        for b in range(nb):
            for j, (cx, cy) in enumerate(chips):
                _remote(h_refs[b].at[me], out_refs[b].at[2 * cx + cy], send_sems, recv_sems, 3 * b + j,
                        (cx, cy, c)).wait_recv()
        for cp in sends:
            cp.wait_send()
        for cp in keep:
            cp.wait()

    return pl.pallas_call(
        body, in_specs=[_ANY] * nb, out_specs=[_ANY] * nb,
        out_shape=[jax.ShapeDtypeStruct(h.shape, h.dtype) for h in hs],
        scratch_shapes=[pltpu.SemaphoreType.DMA((3 * nb,)), pltpu.SemaphoreType.DMA((3 * nb,)),
                        pltpu.SemaphoreType.DMA((LOCAL_CHUNKS * nb,))],
        name=name)(*hs)


JOIN_CHUNKS = 4


def join_halves(ts, name):
    nb = len(ts)
    n = JOIN_CHUNKS

    def body(*refs):
        out_refs = refs[nb:2 * nb]
        send_sems, recv_sems = refs[2 * nb:]
        x, y, c = _place()
        gives = []
        for b in range(nb):
            Ph = ts[b].shape[0] // 2
            rows = Ph // n
            for k in range(n):
                part = out_refs[b].at[pl.ds(c * Ph + k * rows, rows), :]
                gives.append(_remote(part, part, send_sems, recv_sems, n * b + k, (x, y, 1 - c)))
        for cp in gives:
            cp.start()
        for b in range(nb):
            Ph = ts[b].shape[0] // 2
            rows = Ph // n
            for k in range(n):
                landed = out_refs[b].at[pl.ds((1 - c) * Ph + k * rows, rows), :]
                _remote(landed, landed, send_sems, recv_sems, n * b + k, (x, y, 1 - c)).wait_recv()
        for cp in gives:
            cp.wait_send()

    return pl.pallas_call(
        body, in_specs=[_ANY] * nb, out_specs=[_ANY] * nb,
        out_shape=[jax.ShapeDtypeStruct(t.shape, t.dtype) for t in ts],
        scratch_shapes=[pltpu.SemaphoreType.DMA((n * nb,)), pltpu.SemaphoreType.DMA((n * nb,))],
        input_output_aliases={b: b for b in range(nb)}, name=name)(*ts)


WIDE = ['ffn1_w_gate', 'ffn1_w_up', 'ffn2_w_gate', 'ffn2_w_up']


def _rows_of(p):
    if p.shape[-1] == PACK_W and (p.size // PACK_W) % PART_ROWS == 0:
        return p.reshape(-1, PACK_W)
    flat = p.reshape(-1)
    rows = -(-flat.shape[0] // (PACK_W * PART_ROWS)) * PART_ROWS
    return jnp.pad(flat, (0, rows * PACK_W - flat.shape[0])).reshape(rows, PACK_W)


def _pack(parts):
    buf = jnp.concatenate([_rows_of(p) for p in parts], axis=0)
    rows = buf.shape[0]
    total = -(-rows // PACK_ROWS) * PACK_ROWS
    return jnp.pad(buf, ((0, total - rows), (0, 0)))


def _unpack(buf, shapes):
    outs, r = [], 0
    lead = buf.shape[:-2]
    for shp in shapes:
        n = math.prod(shp)
        rows = -(-n // (PACK_W * PART_ROWS)) * PART_ROWS
        blk = buf[..., r:r + rows, :]
        if n != rows * PACK_W:
            blk = blk.reshape(lead + (rows * PACK_W,))[..., :n]
        outs.append(blk.reshape(lead + tuple(shp)))
        r += rows
    return outs


def gather_weights(shards):
    narrow = [n for n in W_NAMES if n in SHARD_AXIS and n not in WIDE]
    parts = [lax.bitcast_convert_type(shards[n], BF16) if n == 'a_v_norm' else shards[n].astype(BF16)
             for n in narrow]
    wide = [shards[n].astype(BF16) for n in WIDE]
    cols = wide[0].shape[-1]
    wide_buf = jnp.concatenate([w.reshape(-1, cols) for w in wide], axis=0)
    full_wide, full_narrow = all_gather_chips([wide_buf, _pack(parts)], "all_gather_weights")
    out = {}
    rows = wide[0].size // cols
    for i, n in enumerate(WIDE):
        g = full_wide[:, i * rows:(i + 1) * rows, :].reshape((N_CHIPS,) + wide[i].shape)
        out[n] = jnp.concatenate([g[k] for k in range(N_CHIPS)], axis=SHARD_AXIS[n])
    for n, g in zip(narrow, _unpack(full_narrow, [p.shape for p in parts])):
        if n == 'a_v_norm':
            g = lax.bitcast_convert_type(g, F32)
        out[n] = jnp.concatenate([g[k] for k in range(N_CHIPS)], axis=SHARD_AXIS[n])
    return out


def reduce_grads(grads, shard_shapes):
    narrow = [n for n in W_NAMES if n in SHARD_AXIS and n not in WIDE]
    repl = [n for n in W_NAMES if n not in SHARD_AXIS]
    full = {n: (jnp.stack(g, axis=0) if isinstance(g, list) else g) for n, g in grads.items() if n not in WIDE}
    cols = shard_shapes[WIDE[0]][-1]
    wide_parts = []
    for n in WIDE:
        for g in grads[n]:
            wide_parts.append(g.reshape(g.shape[0], N_CHIPS, cols).transpose(1, 0, 2))
    g_wide = jnp.concatenate(wide_parts, axis=1)
    blocks = []
    for k in range(N_CHIPS):
        parts = [jnp.split(full[n], N_CHIPS, axis=SHARD_AXIS[n])[k] for n in narrow]
        parts += [full[n] for n in repl]
        blocks.append(_pack(parts))
    gs = [g_wide, jnp.stack(blocks, axis=0)]
    theirs = swap_halves(gs, "grads_swap_halves")
    chip_sums = [add_cores(g, t, f"grads_add_cores_{i}") for i, (g, t) in enumerate(zip(gs, theirs))]
    slots = scatter_to_chips(chip_sums, "grads_scatter")
    totals = join_halves([add_chips(s, f"grads_add_chips_{i}") for i, s in enumerate(slots)], "grads_join_halves")
    out = {}
    rows = math.prod(shard_shapes[WIDE[0]][:-1])
    for i, n in enumerate(WIDE):
        out[n] = totals[0][i * rows:(i + 1) * rows, :].reshape(shard_shapes[n])
    shapes = [shard_shapes[n] for n in narrow] + [full[n].shape for n in repl]
    out.update(zip(narrow + repl, _unpack(totals[1], shapes)))
    return out


def _ffn_fwd(x, gain, wg, wu, wd, tag):
    h = rms_fwd(x, gain, tag + "_norm")
    g, u, a = ffn_up(h, wg, wu, tag + "_up")
    xo = mm([(a, 0, wd, a.shape[1], 0)], res=x, scale=0.5, name=tag + "_down")
    return xo, (x, h, g, u, a)


def _ffn_bwd(dxo, saved, gain, wg, wu, wd, tag):
    x, h, g, u, a = saved
    Fd = a.shape[1]
    dxo, dxo_b = dxo
    dgp, du = ffn_down_bwd(dxo_b, wd, g, u, tag + "_down_bwd")
    d_wd = mm_tn(a, dxo_b, scale=0.5, name=tag + "_dwd")
    d_wg = mm_tn(h, dgp, name=tag + "_dwg")
    d_wu = mm_tn(h, du, name=tag + "_dwu")
    dh = mm([(dgp, 0, wg, Fd, 0), (du, 0, wu, Fd, 0)], tb=True, name=tag + "_up_bwd")
    dx, d_gain = rms_bwd(x, gain, dh, dxo, tag + "_norm_bwd")
    return dx, d_gain, d_wg, d_wu, d_wd


def local_step(x, mem, W, target):
    D = x.shape[1]
    depth = W['ffn1_norm'].shape[0]
    n_a = W['a_w_in'].shape[0]
    G = {}
    mem_h = rms_fwd(mem, W['mem_norm'], "mem_norm")
    b_mats = jnp.broadcast_to(W['a_b_spatial'][..., None], W['a_b_spatial'].shape + (GM_CHUNK,))
    w_kv = (W['w_kv'], None)

    def ffn_w(which, l):
        return [(W[f'{which}_w_{part}'], l) for part in ('gate', 'up', 'down')]

    def mixer_w(l):
        if l < n_a:
            return (W['a_w_in'], l), (W['a_w_out'], l), 2 * GM_W
        return (W['b_w_in'], l - n_a), (W['b_w_out'], l - n_a), GM_W

    saved = []
    kv = kvn = x_kv = None
    for l in range(depth):
        if l == n_a:
            x_kv = x
            kvn = rms_fwd(x, W['kv_norm'], "kv_norm")
            kv = mm([(kvn, 0, w_kv, D, 0)], out_dtype=BF16, name="kv_proj")
        x, s1 = _ffn_fwd(x, W['ffn1_norm'][l], *ffn_w('ffn1', l), f"l{l}_ffn1")
        hm = rms_fwd(x, W['mix_norm'][l], f"l{l}_mix_norm")
        mem_kv = mm([(mem_h, 0, (W['w_mem_kv'], l), D, 0)], name=f"l{l}_mem_kv")
        w_in, w_out, tok_w = mixer_w(l)
        proj = mm([(hm, 0, w_in, D, 0)], name=f"l{l}_mix_in")
        if l < n_a:
            y_tok = gmlp_fwd(proj, W['a_v_norm'][l], W['a_w_spatial'][l], b_mats[l], f"l{l}_gmlp")
        else:
            y_tok = sb_fwd(proj, kv, f"l{l}_sb")
        q_cb = tok_w // MEM_W
        y_mem = memattn_fwd(proj, q_cb, mem_kv, f"l{l}_memattn")
        x_mid = x
        x = mm([(y_tok, 0, w_out, GM_W, 0), (y_mem, 0, w_out, MEM_W, GM_W // MEM_W)], res=x, name=f"l{l}_mix_out")
        sm = (x_mid, hm, mem_kv, proj, y_tok, y_mem, q_cb)
        x, s2 = _ffn_fwd(x, W['ffn2_norm'][l], *ffn_w('ffn2', l), f"l{l}_ffn2")
        saved.append((s1, sm, s2))

    loss, dx, d_final = final_loss(x, W['final_norm'], target, "loss_head")
    G['final_norm'] = d_final.reshape(-1)

    per_layer = {n: [None] * depth for n in ['ffn1_norm', 'ffn1_w_gate', 'ffn1_w_up', 'ffn1_w_down', 'mix_norm',
                                             'ffn2_norm', 'ffn2_w_gate', 'ffn2_w_up', 'ffn2_w_down', 'w_mem_kv']}
    per_a = {n: [None] * n_a for n in ['a_w_in', 'a_v_norm', 'a_w_spatial', 'a_b_spatial', 'a_w_out']}
    per_b = {n: [None] * (depth - n_a) for n in ['b_w_in', 'b_w_out']}
    d_mem_h = None
    dk = dv = None
    for l in reversed(range(depth)):
        s1, sm, s2 = saved[l]
        dx, dg, dwg, dwu, dwd = _ffn_bwd(dx, s2, W['ffn2_norm'][l], *ffn_w('ffn2', l), f"l{l}_ffn2")
        per_layer['ffn2_norm'][l], per_layer['ffn2_w_gate'][l] = dg.reshape(-1), dwg
        per_layer['ffn2_w_up'][l], per_layer['ffn2_w_down'][l] = dwu, dwd

        x_mid, hm, mem_kv, proj, y_tok, y_mem, q_cb = sm
        is_a = l < n_a
        w_in, w_out, tok_w = mixer_w(l)
        dy = mm([(dx[1], 0, w_out, D, 0)], tb=True, name=f"l{l}_mix_out_bwd")
        d_w_out = jnp.concatenate([mm_tn(y_tok, dx[1], name=f"l{l}_dwout_tok"),
                                   mm_tn(y_mem, dx[1], name=f"l{l}_dwout_mem")], axis=0)
        dq_mem, d_mem_kv = memattn_bwd(proj, q_cb, mem_kv, dy, GM_W // MEM_W, f"l{l}_memattn_bwd")
        if is_a:
            d_tok, d_ws, d_bs, d_vg = gmlp_bwd(proj, dy, W['a_v_norm'][l], W['a_w_spatial'][l], b_mats[l],
                                               f"l{l}_gmlp_bwd")
            per_a['a_w_spatial'][l], per_a['a_b_spatial'][l] = d_ws, d_bs[:, :, 0]
            per_a['a_v_norm'][l], per_a['a_w_out'][l] = d_vg.reshape(-1), d_w_out
        else:
            d_tok, dk, dv = sb_bwd(proj, kv, dy, y_tok, dk, dv, f"l{l}_sb_bwd")
            per_b['b_w_out'][l - n_a] = d_w_out
        d_w_in = jnp.concatenate([mm_tn(hm, d_tok, name=f"l{l}_dwin_tok"),
                                  mm_tn(hm, dq_mem, name=f"l{l}_dwin_mem")], axis=1)
        (per_a['a_w_in'] if is_a else per_b['b_w_in'])[l if is_a else l - n_a] = d_w_in
        dh = mm([(d_tok, 0, w_in, tok_w, 0), (dq_mem, 0, w_in, MEM_W, tok_w // MEM_W)], tb=True,
                name=f"l{l}_mix_in_bwd")
        dx, dg = rms_bwd(x_mid, W['mix_norm'][l], dh, dx[0], f"l{l}_mix_norm_bwd")
        per_layer['mix_norm'][l] = dg.reshape(-1)
        per_layer['w_mem_kv'][l] = mm_tn(mem_h, d_mem_kv, name=f"l{l}_dw_mem_kv")
        d_mem_h = mm([(d_mem_kv, 0, (W['w_mem_kv'], l), 2 * MEM_W, 0)], tb=True, res=d_mem_h,
                     name=f"l{l}_mem_kv_bwd")

        dx, dg, dwg, dwu, dwd = _ffn_bwd(dx, s1, W['ffn1_norm'][l], *ffn_w('ffn1', l), f"l{l}_ffn1")
        per_layer['ffn1_norm'][l], per_layer['ffn1_w_gate'][l] = dg.reshape(-1), dwg
        per_layer['ffn1_w_up'][l], per_layer['ffn1_w_down'][l] = dwu, dwd
        if l == n_a:
            G['w_kv'] = jnp.concatenate([mm_tn(kvn, dk, name="dw_k"), mm_tn(kvn, dv, name="dw_v")], axis=1)
            d_kvn = mm([(dk, 0, w_kv, GM_W, 0), (dv, 0, w_kv, GM_W, 1)], tb=True, name="kv_proj_bwd")
            dx, dg = rms_bwd(x_kv, W['kv_norm'], d_kvn, dx[0], "kv_norm_bwd")
            G['kv_norm'] = dg.reshape(-1)

    _, dg = rms_bwd(mem, W['mem_norm'], d_mem_h, None, "mem_norm_bwd")
    G['mem_norm'] = dg.reshape(-1)
    for d in (per_layer, per_a, per_b):
        G.update(d)
    return loss, dx[0], G


def kernel(x, mem, ffn1_norm, ffn1_w_gate, ffn1_w_up, ffn1_w_down, mix_norm, ffn2_norm, ffn2_w_gate, ffn2_w_up, ffn2_w_down, mem_norm, w_mem_kv, a_w_in, a_v_norm, a_w_spatial, a_b_spatial, a_w_out, kv_norm, w_kv, b_w_in, b_w_out, final_norm, loss_target, m_ffn1_norm, m_ffn1_w_gate, m_ffn1_w_up, m_ffn1_w_down, m_mix_norm, m_ffn2_norm, m_ffn2_w_gate, m_ffn2_w_up, m_ffn2_w_down, m_mem_norm, m_w_mem_kv, m_a_w_in, m_a_v_norm, m_a_w_spatial, m_a_b_spatial, m_a_w_out, m_kv_norm, m_w_kv, m_b_w_in, m_b_w_out, m_final_norm, v_ffn1_norm, v_ffn1_w_gate, v_ffn1_w_up, v_ffn1_w_down, v_mix_norm, v_ffn2_norm, v_ffn2_w_gate, v_ffn2_w_up, v_ffn2_w_down, v_mem_norm, v_w_mem_kv, v_a_w_in, v_a_v_norm, v_a_w_spatial, v_a_b_spatial, v_a_w_out, v_kv_norm, v_w_kv, v_b_w_in, v_b_w_out, v_final_norm):
    weights = dict(zip(W_NAMES, [ffn1_norm, ffn1_w_gate, ffn1_w_up, ffn1_w_down, mix_norm, ffn2_norm, ffn2_w_gate,
                                 ffn2_w_up, ffn2_w_down, mem_norm, w_mem_kv, a_w_in, a_v_norm, a_w_spatial,
                                 a_b_spatial, a_w_out, kv_norm, w_kv, b_w_in, b_w_out, final_norm]))
    m_in = dict(zip(W_NAMES, [m_ffn1_norm, m_ffn1_w_gate, m_ffn1_w_up, m_ffn1_w_down, m_mix_norm, m_ffn2_norm,
                              m_ffn2_w_gate, m_ffn2_w_up, m_ffn2_w_down, m_mem_norm, m_w_mem_kv, m_a_w_in,
                              m_a_v_norm, m_a_w_spatial, m_a_b_spatial, m_a_w_out, m_kv_norm, m_w_kv, m_b_w_in,
                              m_b_w_out, m_final_norm]))
    v_in = dict(zip(W_NAMES, [v_ffn1_norm, v_ffn1_w_gate, v_ffn1_w_up, v_ffn1_w_down, v_mix_norm, v_ffn2_norm,
                              v_ffn2_w_gate, v_ffn2_w_up, v_ffn2_w_down, v_mem_norm, v_w_mem_kv, v_a_w_in,
                              v_a_v_norm, v_a_w_spatial, v_a_b_spatial, v_a_w_out, v_kv_norm, v_w_kv, v_b_w_in,
                              v_b_w_out, v_final_norm]))

    full = gather_weights({n: weights[n] for n in SHARD_AXIS})
    W = {n: (full[n] if n in SHARD_AXIS else weights[n]) for n in W_NAMES}
    loss, dx, grads = local_step(x[0], mem[0], W, loss_target[0])
    total = reduce_grads(grads, {n: weights[n].shape for n in SHARD_AXIS})
    loss = lax.psum(loss[0, 0], ("x", "y", "c"))

    deltas, new_m, new_v = {}, {}, {}
    for n in W_NAMES:
        deltas[n], new_m[n], new_v[n] = adamw(weights[n], total[n], m_in[n], v_in[n], "adamw_" + n)
    return (loss, dx[None], *[total[n] for n in W_NAMES], *[deltas[n] for n in W_NAMES],
            *[new_m[n] for n in W_NAMES], *[new_v[n] for n in W_NAMES])
```

```python
import functools
import math

import jax
import jax.numpy as jnp
from jax import lax
from jax.experimental import pallas as pl
from jax.experimental.pallas import tpu as pltpu

F32 = jnp.float32
BF16 = jnp.bfloat16
EPS = 1e-6
LANES = 128
ROW_TILE = 512
COL_TILE = 1408
PACK_ROWS = 256
PART_ROWS = 16
VMEM_LIMIT = 56 * 1024 * 1024

W_NAMES = ['ffn1_norm', 'ffn1_w_gate', 'ffn1_w_up', 'ffn1_w_down', 'mix_norm', 'ffn2_norm', 'ffn2_w_gate',
           'ffn2_w_up', 'ffn2_w_down', 'mem_norm', 'w_mem_kv', 'a_w_in', 'a_v_norm', 'a_w_spatial',
           'a_b_spatial', 'a_w_out', 'kv_norm', 'w_kv', 'b_w_in', 'b_w_out', 'final_norm']
SHARD_AXIS = {'ffn1_w_gate': 2, 'ffn1_w_up': 2, 'ffn1_w_down': 1, 'ffn2_w_gate': 2, 'ffn2_w_up': 2,
              'ffn2_w_down': 1, 'w_mem_kv': 1, 'a_w_in': 2, 'a_v_norm': 1, 'a_w_out': 1, 'w_kv': 1,
              'b_w_in': 1, 'b_w_out': 1}
N_CHIPS = 4
PACK_W = 1024

MEM_HEADS = 4
MEM_W = 256
HEAD_DIM = 64
GM_W = 768
GM_GROUPS = 6
GM_CHUNK = 128
CHUNK = 64
SB_BLOCK = 128
SB_Q_CHUNK = 1024
SB_DEAD = -110.0
SB_PAIRS = 2
SB_COLS = SB_PAIRS * LANES

ADAM_LR, ADAM_B1, ADAM_B2, ADAM_EPS, ADAM_WD, ADAM_STEP = 0.001, 0.9, 0.999, 1e-08, 0.01, 10

MESH = pl.DeviceIdType.MESH


def _cparams(sem=None):
    return pltpu.CompilerParams(dimension_semantics=sem, vmem_limit_bytes=VMEM_LIMIT)


def _tile(n, target, mult=LANES):
    best = None
    for t in range(mult, min(n, target) + 1, mult):
        if n % t == 0:
            best = t
    return best if best is not None else n


def _dot(a, b):
    return jnp.dot(a, b, preferred_element_type=F32)


def _dot_nt(a, b):
    return lax.dot_general(a, b, (((1,), (1,)), ((), ())), preferred_element_type=F32)


def _dot_tn(a, b):
    return lax.dot_general(a, b, (((0,), (0,)), ((), ())), preferred_element_type=F32)


def _bf(v):
    return v.astype(BF16)


def rms_fwd(x, gain, name):
    S, D = x.shape
    tm = _tile(S, ROW_TILE, 8)

    def body(x_ref, g_ref, o_ref):
        xf = x_ref[...]
        r = lax.rsqrt(jnp.mean(xf * xf, axis=-1, keepdims=True) + EPS)
        o_ref[...] = ((xf * r) * g_ref[...]).astype(o_ref.dtype)

    return pl.pallas_call(
        body, grid=(S // tm,),
        in_specs=[pl.BlockSpec((tm, D), lambda i: (i, 0)), pl.BlockSpec((1, D), lambda i: (0, 0))],
        out_specs=pl.BlockSpec((tm, D), lambda i: (i, 0)),
        out_shape=jax.ShapeDtypeStruct((S, D), BF16),
        compiler_params=_cparams(("parallel",)), name=name)(x, gain.reshape(1, D))


def rms_gain_grad(x, dh, name):
    S, D = x.shape
    tm = _tile(S, ROW_TILE, 8)

    def body(x_ref, dh_ref, dg_ref):
        xf = x_ref[...]
        r = lax.rsqrt(jnp.mean(xf * xf, axis=-1, keepdims=True) + EPS)

        @pl.when(pl.program_id(0) == 0)
        def _():
            dg_ref[...] = jnp.zeros_like(dg_ref)

        dg_ref[...] += jnp.sum(dh_ref[...] * (xf * r), axis=0, keepdims=True)

    row = pl.BlockSpec((tm, D), lambda i: (i, 0))
    return pl.pallas_call(
        body, grid=(S // tm,), in_specs=[row, row], out_specs=pl.BlockSpec((1, D), lambda i: (0, 0)),
        out_shape=jax.ShapeDtypeStruct((1, D), F32),
        compiler_params=_cparams(("arbitrary",)), name=name)(x, dh)


def final_loss(x, gain, target, name):
    S, D = x.shape
    tm = _tile(S, ROW_TILE, 16)

    def body(x_ref, g_ref, t_ref, loss_ref, dx_ref, dxb_ref, dg_ref):
        xf = x_ref[...]
        r = lax.rsqrt(jnp.mean(xf * xf, axis=-1, keepdims=True) + EPS)
        xhat = xf * r
        g = g_ref[...]
        diff = xhat * g - t_ref[...]
        dy = diff * (1.0 / D)
        dxhat = dy * g
        m = jnp.mean(dxhat * xhat, axis=-1, keepdims=True)
        dx = r * (dxhat - xhat * m)
        dx_ref[...] = dx
        dxb_ref[...] = _bf(dx)

        @pl.when(pl.program_id(0) == 0)
        def _():
            dg_ref[...] = jnp.zeros_like(dg_ref)
            loss_ref[...] = jnp.zeros_like(loss_ref)

        dg_ref[...] += jnp.sum(dy * xhat, axis=0, keepdims=True)
        per_tok = jnp.sum(diff * diff, axis=-1, keepdims=True) * (0.5 / D)
        loss_ref[...] += jnp.sum(per_tok, axis=0, keepdims=True)

    row = pl.BlockSpec((tm, D), lambda i: (i, 0))
    vec = pl.BlockSpec((1, D), lambda i: (0, 0))
    one = pl.BlockSpec((1, 1), lambda i: (0, 0))
    loss, dx, dxb, dg = pl.pallas_call(
        body, grid=(S // tm,), in_specs=[row, vec, row], out_specs=[one, row, row, vec],
        out_shape=[jax.ShapeDtypeStruct((1, 1), F32), jax.ShapeDtypeStruct((S, D), F32),
                   jax.ShapeDtypeStruct((S, D), BF16), jax.ShapeDtypeStruct((1, D), F32)],
        compiler_params=_cparams(("arbitrary",)), name=name)(x, gain.reshape(1, D), target)
    return loss, (dx, dxb), dg


def _wspec(w, blk, idx):
    _, l = w
    if l is None:
        return pl.BlockSpec(blk, idx)
    return pl.BlockSpec((None,) + blk, lambda *g: (l,) + idx(*g))


def mm(pairs, *, tb=False, res=None, scale=1.0, out_dtype=F32, norm_bwd=None, name):
    M = pairs[0][0].shape[0]
    N = pairs[0][2][0].shape[-2 if tb else -1]
    tm = _tile(M, ROW_TILE, 16)
    tn = N if norm_bwd is not None else _tile(N, COL_TILE)
    n_p = len(pairs)
    has_res = res is not None
    n_in = 2 * n_p + has_res + (2 if norm_bwd is not None else 0)

    def body(*refs):
        acc = None
        for p in range(n_p):
            a = _bf(refs[2 * p][...])
            b = _bf(refs[2 * p + 1][...])
            d = _dot_nt(a, b) if tb else _dot(a, b)
            acc = d if acc is None else acc + d
        if scale != 1.0:
            acc = acc * scale
        if norm_bwd is None:
            if has_res:
                acc = acc + refs[2 * n_p][...]
            refs[n_in][...] = acc.astype(refs[n_in].dtype)
            return
        x_ref, g_ref = refs[n_in - 2], refs[n_in - 1]
        dx_ref, dxb_ref, dg_ref = refs[n_in:]
        xf = x_ref[...]
        r = lax.rsqrt(jnp.mean(xf * xf, axis=-1, keepdims=True) + EPS)
        xhat = xf * r
        dxhat = acc * g_ref[...]
        m = jnp.mean(dxhat * xhat, axis=-1, keepdims=True)
        dx = r * (dxhat - xhat * m)
        if has_res:
            dx = dx + refs[2 * n_p][...]
        dx_ref[...] = dx
        dxb_ref[...] = _bf(dx)

        @pl.when(pl.program_id(1) == 0)
        def _():
            dg_ref[...] = jnp.zeros_like(dg_ref)

        dg_ref[...] += jnp.sum(acc * xhat, axis=0, keepdims=True)

    ins, in_specs = [], []
    for a, cb, w, K, kb in pairs:
        ins += [a, w[0]]
        in_specs.append(pl.BlockSpec((tm, K), functools.partial(lambda j, i, cb: (i, cb), cb=cb)))
        if tb:
            in_specs.append(_wspec(w, (tn, K), functools.partial(lambda j, i, kb: (j, kb), kb=kb)))
        else:
            in_specs.append(_wspec(w, (K, tn), functools.partial(lambda j, i, kb: (kb, j), kb=kb)))
    tile = pl.BlockSpec((tm, tn), lambda j, i: (i, j))
    if has_res:
        ins.append(res)
        in_specs.append(tile)
    if norm_bwd is None:
        return pl.pallas_call(
            body, grid=(N // tn, M // tm), in_specs=in_specs, out_specs=tile,
            out_shape=jax.ShapeDtypeStruct((M, N), out_dtype),
            compiler_params=_cparams(("parallel", "parallel")), name=name)(*ins)
    x, gain = norm_bwd
    vec = pl.BlockSpec((1, N), lambda j, i: (0, 0))
    dx, dxb, dg = pl.pallas_call(
        body, grid=(1, M // tm), in_specs=in_specs + [tile, vec], out_specs=[tile, tile, vec],
        out_shape=[jax.ShapeDtypeStruct((M, N), F32), jax.ShapeDtypeStruct((M, N), BF16),
                   jax.ShapeDtypeStruct((1, N), F32)],
        compiler_params=_cparams(("arbitrary", "arbitrary")), name=name)(*ins, x, gain.reshape(1, N))
    return (dx, dxb), dg


def mm_tn(a, b, *, a_cb=0, a_w=None, b_cb=0, b_w=None, scale=1.0, name):
    S = a.shape[0]
    Ka = a.shape[1] if a_w is None else a_w
    Nb = b.shape[1] if b_w is None else b_w
    ts = _tile(S, 2 * ROW_TILE, 8)
    tk = _tile(Ka, COL_TILE)
    tn = _tile(Nb, COL_TILE)
    a_off, b_off = a_cb * (Ka // tk), b_cb * (Nb // tn)

    def body(a_ref, b_ref, o_ref):
        s = pl.program_id(2)

        @pl.when(s == 0)
        def _():
            o_ref[...] = jnp.zeros_like(o_ref)

        o_ref[...] += _dot_tn(_bf(a_ref[...]), _bf(b_ref[...]))
        if scale != 1.0:
            @pl.when(s == pl.num_programs(2) - 1)
            def _():
                o_ref[...] = o_ref[...] * scale

    return pl.pallas_call(
        body, grid=(Ka // tk, Nb // tn, S // ts),
        in_specs=[pl.BlockSpec((ts, tk), lambda k, n, s: (s, a_off + k)),
                  pl.BlockSpec((ts, tn), lambda k, n, s: (s, b_off + n))],
        out_specs=pl.BlockSpec((tk, tn), lambda k, n, s: (k, n)),
        out_shape=jax.ShapeDtypeStruct((Ka, Nb), F32),
        compiler_params=_cparams(("parallel", "parallel", "arbitrary")), name=name)(a, b)


def _sigmoid(x):
    return 0.5 * jnp.tanh(0.5 * x) + 0.5


def ffn_up(h, wg, wu, name):
    S, D = h.shape
    Fd = wg[0].shape[-1]
    tm = _tile(S, ROW_TILE, 8)
    tn = _tile(Fd, COL_TILE)

    def body(h_ref, wg_ref, wu_ref, g_ref, u_ref, a_ref):
        hv = h_ref[...]
        g = _dot(hv, wg_ref[...])
        u = _dot(hv, wu_ref[...])
        g_ref[...] = _bf(g)
        u_ref[...] = _bf(u)
        a_ref[...] = _bf(g * _sigmoid(g) * u)

    blk = pl.BlockSpec((tm, tn), lambda j, i: (i, j))
    sh = jax.ShapeDtypeStruct((S, Fd), BF16)
    return pl.pallas_call(
        body, grid=(Fd // tn, S // tm),
        in_specs=[pl.BlockSpec((tm, D), lambda j, i: (i, 0)), _wspec(wg, (D, tn), lambda j, i: (0, j)),
                  _wspec(wu, (D, tn), lambda j, i: (0, j))],
        out_specs=[blk, blk, blk], out_shape=[sh, sh, sh],
        compiler_params=_cparams(("parallel", "parallel")), name=name)(h, wg[0], wu[0])


def ffn_down_bwd(dxo, wd, g, u, name):
    S, D = dxo.shape
    Fd = wd[0].shape[-2]
    tm = _tile(S, ROW_TILE, 8)
    tn = _tile(Fd, COL_TILE)

    def body(dx_ref, wd_ref, g_ref, u_ref, dg_ref, du_ref):
        da = _dot_nt(_bf(dx_ref[...]), wd_ref[...]) * 0.5
        g = g_ref[...].astype(F32)
        u = u_ref[...].astype(F32)
        sg = _sigmoid(g)
        dg_ref[...] = _bf(da * u * (sg * (1.0 + g * (1.0 - sg))))
        du_ref[...] = _bf(da * (g * sg))

    blk = pl.BlockSpec((tm, tn), lambda j, i: (i, j))
    sh = jax.ShapeDtypeStruct((S, Fd), BF16)
    return pl.pallas_call(
        body, grid=(Fd // tn, S // tm),
        in_specs=[pl.BlockSpec((tm, D), lambda j, i: (i, 0)), _wspec(wd, (tn, D), lambda j, i: (j, 0)),
                  blk, blk],
        out_specs=[blk, blk], out_shape=[sh, sh],
        compiler_params=_cparams(("parallel", "parallel")), name=name)(dxo, wd[0], g, u)


_GELU_C = math.sqrt(2.0 / math.pi)


def _gelu(p):
    return 0.5 * p * (1.0 + jnp.tanh(_GELU_C * (p + 0.044715 * (p * p * p))))


def _gelu_grad(p):
    th = jnp.tanh(_GELU_C * (p + 0.044715 * (p * p * p)))
    return 0.5 * (1.0 + th) + 0.5 * p * (1.0 - th * th) * (_GELU_C * (1.0 + 3.0 * 0.044715 * (p * p)))


def _chunk_mask():
    t = lax.broadcasted_iota(jnp.int32, (GM_CHUNK, GM_CHUNK), 0)
    s = lax.broadcasted_iota(jnp.int32, (GM_CHUNK, GM_CHUNK), 1)
    return (s // CHUNK) <= (t // CHUNK)


def gmlp_fwd(proj, v_gain, w_s, b_mat, name):
    S = proj.shape[0]
    tw = _tile(S, ROW_TILE, GM_CHUNK)
    n_win = tw // GM_CHUNK

    def body(p_ref, gain_ref, w_ref, b_ref, y_ref):
        mask = _chunk_mask()
        v = _gelu(p_ref[:, GM_W:])
        r = lax.rsqrt(jnp.mean(v * v, axis=-1, keepdims=True) + EPS)
        vn = _bf(v * r * gain_ref[...])
        for g in range(GM_GROUPS):
            wm = _bf(jnp.where(mask, w_ref[g], 0.0))
            cs = slice(g * GM_CHUNK, (g + 1) * GM_CHUNK)
            for w in range(n_win):
                rs = slice(w * GM_CHUNK, (w + 1) * GM_CHUNK)
                mixed = _dot(wm, vn[rs, cs]) + b_ref[g]
                y_ref[rs, cs] = _bf(_gelu(p_ref[rs, cs]) * mixed)

    return pl.pallas_call(
        body, grid=(S // tw,),
        in_specs=[pl.BlockSpec((tw, 2 * GM_W), lambda i: (i, 0)), pl.BlockSpec((1, GM_W), lambda i: (0, 0)),
                  pl.BlockSpec((GM_GROUPS, GM_CHUNK, GM_CHUNK), lambda i: (0, 0, 0)),
                  pl.BlockSpec((GM_GROUPS, GM_CHUNK, GM_CHUNK), lambda i: (0, 0, 0))],
        out_specs=pl.BlockSpec((tw, GM_W), lambda i: (i, 0)),
        out_shape=jax.ShapeDtypeStruct((S, GM_W), BF16),
        compiler_params=_cparams(("parallel",)), name=name)(proj, v_gain.reshape(1, GM_W), w_s, b_mat)


def gmlp_bwd(proj, dy, v_gain, w_s, b_mat, name):
    S = proj.shape[0]
    tw = _tile(S, ROW_TILE, GM_CHUNK)
    n_win = tw // GM_CHUNK

    def body(p_ref, dy_ref, gain_ref, w_ref, b_ref, dp_ref, dw_ref, db_ref, dgain_ref, dvn_ref):
        step = pl.program_id(0)

        @pl.when(step == 0)
        def _():
            dw_ref[...] = jnp.zeros_like(dw_ref)
            db_ref[...] = jnp.zeros_like(db_ref)
            dgain_ref[...] = jnp.zeros_like(dgain_ref)

        mask = _chunk_mask()
        pv = p_ref[:, GM_W:]
        v = _gelu(pv)
        r = lax.rsqrt(jnp.mean(v * v, axis=-1, keepdims=True) + EPS)
        vhat = v * r
        gain = gain_ref[...]
        vn = _bf(vhat * gain)
        for g in range(GM_GROUPS):
            wm = _bf(jnp.where(mask, w_ref[g], 0.0))
            cs = slice(g * GM_CHUNK, (g + 1) * GM_CHUNK)
            dw_acc = jnp.zeros((GM_CHUNK, GM_CHUNK), F32)
            db_acc = jnp.zeros((GM_CHUNK, GM_CHUNK), F32)
            for w in range(n_win):
                rs = slice(w * GM_CHUNK, (w + 1) * GM_CHUNK)
                pu = p_ref[rs, cs]
                u = _gelu(pu)
                vn_blk = vn[rs, cs]
                mixed = _dot(wm, vn_blk) + b_ref[g]
                dyb = dy_ref[rs, cs]
                dp_ref[rs, cs] = _bf(dyb * mixed * _gelu_grad(pu))
                dmix = dyb * u
                db_acc = db_acc + dmix
                dmix_b = _bf(dmix)
                dw_acc = dw_acc + _dot_nt(dmix_b, vn_blk)
                dvn_ref[rs, cs] = _dot_tn(wm, dmix_b)
            dw_ref[g] += jnp.where(mask, dw_acc, 0.0)
            db_ref[g] += jnp.broadcast_to(jnp.sum(db_acc, axis=-1, keepdims=True), (GM_CHUNK, GM_CHUNK))
        dvn = dvn_ref[...]
        dgain_ref[...] += jnp.sum(dvn * vhat, axis=0, keepdims=True)
        dvhat = dvn * gain
        m = jnp.mean(dvhat * vhat, axis=-1, keepdims=True)
        dv = r * (dvhat - vhat * m)
        dp_ref[:, GM_W:] = _bf(dv * _gelu_grad(pv))

    sq = pl.BlockSpec((GM_GROUPS, GM_CHUNK, GM_CHUNK), lambda i: (0, 0, 0))
    sq_sh = jax.ShapeDtypeStruct((GM_GROUPS, GM_CHUNK, GM_CHUNK), F32)
    return pl.pallas_call(
        body, grid=(S // tw,),
        in_specs=[pl.BlockSpec((tw, 2 * GM_W), lambda i: (i, 0)), pl.BlockSpec((tw, GM_W), lambda i: (i, 0)),
                  pl.BlockSpec((1, GM_W), lambda i: (0, 0)), sq, sq],
        out_specs=[pl.BlockSpec((tw, 2 * GM_W), lambda i: (i, 0)), sq, sq,
                   pl.BlockSpec((1, GM_W), lambda i: (0, 0))],
        out_shape=[jax.ShapeDtypeStruct((S, 2 * GM_W), BF16), sq_sh, sq_sh,
                   jax.ShapeDtypeStruct((1, GM_W), F32)],
        scratch_shapes=[pltpu.VMEM((tw, GM_W), F32)],
        compiler_params=_cparams(("arbitrary",)), name=name)(proj, dy, v_gain.reshape(1, GM_W), w_s, b_mat)


def _head_mask(h, width):
    lane = lax.broadcasted_iota(jnp.int32, (1, width), 1)
    return (lane >= HEAD_DIM * h) & (lane < HEAD_DIM * (h + 1))


def _mem_probs(q, k, h):
    kh = jnp.where(_head_mask(h, MEM_W), k, jnp.zeros_like(k))
    s = _dot_nt(q, kh) * (HEAD_DIM ** -0.5)
    s = s - jnp.max(s, axis=-1, keepdims=True)
    e = jnp.exp(s)
    return e * (1.0 / jnp.sum(e, axis=-1, keepdims=True)), kh


def memattn_fwd(proj, q_cb, mem_kv, name):
    S = proj.shape[0]
    NM = mem_kv.shape[0]
    tm = _tile(S, ROW_TILE, 8)

    def body(q_ref, kv_ref, o_ref):
        q = _bf(q_ref[...])
        k = _bf(kv_ref[:, :MEM_W])
        v = _bf(kv_ref[:, MEM_W:])
        acc = jnp.zeros((tm, MEM_W), F32)
        for h in range(MEM_HEADS):
            p, _ = _mem_probs(q, k, h)
            vh = jnp.where(_head_mask(h, MEM_W), v, jnp.zeros_like(v))
            acc = acc + _dot(_bf(p), vh)
        o_ref[...] = _bf(acc)

    return pl.pallas_call(
        body, grid=(S // tm,),
        in_specs=[pl.BlockSpec((tm, MEM_W), lambda i: (i, q_cb)), pl.BlockSpec((NM, 2 * MEM_W), lambda i: (0, 0))],
        out_specs=pl.BlockSpec((tm, MEM_W), lambda i: (i, 0)),
        out_shape=jax.ShapeDtypeStruct((S, MEM_W), BF16),
        compiler_params=_cparams(("parallel",)), name=name)(proj, mem_kv)


def memattn_bwd(proj, q_cb, mem_kv, dy, dy_cb, name):
    S = proj.shape[0]
    NM = mem_kv.shape[0]
    tm = _tile(S, ROW_TILE, 8)

    def body(q_ref, kv_ref, do_ref, dq_ref, dkv_ref):
        @pl.when(pl.program_id(0) == 0)
        def _():
            dkv_ref[...] = jnp.zeros_like(dkv_ref)

        q = _bf(q_ref[...])
        k = _bf(kv_ref[:, :MEM_W])
        v = _bf(kv_ref[:, MEM_W:])
        do = _bf(do_ref[...])
        dq = jnp.zeros((tm, MEM_W), F32)
        dk = jnp.zeros((NM, MEM_W), F32)
        dv = jnp.zeros((NM, MEM_W), F32)
        for h in range(MEM_HEADS):
            hm = _head_mask(h, MEM_W)
            p, kh = _mem_probs(q, k, h)
            vh = jnp.where(hm, v, jnp.zeros_like(v))
            dp = _dot_nt(do, vh)
            ds = _bf(p * (dp - jnp.sum(p * dp, axis=-1, keepdims=True)) * (HEAD_DIM ** -0.5))
            dq = dq + _dot(ds, kh)
            dk = dk + jnp.where(hm, _dot_tn(ds, q), 0.0)
            dv = dv + jnp.where(hm, _dot_tn(_bf(p), do), 0.0)
        dq_ref[...] = _bf(dq)
        dkv_ref[:, :MEM_W] += dk
        dkv_ref[:, MEM_W:] += dv

    return pl.pallas_call(
        body, grid=(S // tm,),
        in_specs=[pl.BlockSpec((tm, MEM_W), lambda i: (i, q_cb)), pl.BlockSpec((NM, 2 * MEM_W), lambda i: (0, 0)),
                  pl.BlockSpec((tm, MEM_W), lambda i: (i, dy_cb))],
        out_specs=[pl.BlockSpec((tm, MEM_W), lambda i: (i, 0)), pl.BlockSpec((NM, 2 * MEM_W), lambda i: (0, 0))],
        out_shape=[jax.ShapeDtypeStruct((S, MEM_W), BF16), jax.ShapeDtypeStruct((NM, 2 * MEM_W), F32)],
        compiler_params=_cparams(("arbitrary",)), name=name)(proj, mem_kv, dy)


def _split_bf(v):
    hi = _bf(v)
    return hi, _bf(v - hi.astype(F32))


def _sb_scores(qp, k2, dpos, row, col, tri):
    z = _dot_nt(qp, k2) * (HEAD_DIM ** -0.5)
    t = jnp.log(1.0 + jnp.exp(-jnp.abs(z)))
    lb = jnp.minimum(z, 0.0) - t
    lom = jnp.where(_sb_mask(dpos, row, col), -jnp.maximum(z, 0.0) - t, 0.0)
    l_hi, l_lo = _split_bf(lom)
    insuf = _dot(l_hi, tri) + _dot(l_lo, tri)
    return lb + insuf, jnp.sum(lom, axis=1, keepdims=True), lb


def _sb_mask(dpos, row, col):
    return (col - row) < dpos


def _sb_weights(s, run, dpos, row, col):
    return jnp.where(_sb_mask(dpos, row, col), jnp.exp(s + run), 0.0)


def _sb_alive(runs):
    top = runs[0]
    for r in runs[1:]:
        top = jnp.maximum(top, r)
    return jnp.max(top) > SB_DEAD


def _sb_consts():
    row = lax.broadcasted_iota(jnp.int32, (2 * SB_BLOCK, SB_BLOCK), 0) % SB_BLOCK
    col = lax.broadcasted_iota(jnp.int32, (2 * SB_BLOCK, SB_BLOCK), 1)
    r = lax.broadcasted_iota(jnp.int32, (SB_BLOCK, SB_BLOCK), 0)
    c = lax.broadcasted_iota(jnp.int32, (SB_BLOCK, SB_BLOCK), 1)
    return row, col, _bf(jnp.where(r > c, 1.0, 0.0)), _bf(jnp.where(r >= c, 1.0, 0.0))


def _stack_heads(x2, masks):
    return jnp.concatenate([_bf(jnp.where(hm, x2, jnp.zeros_like(x2))) for hm in masks], axis=0)


def _side_by_side(x):
    return jnp.concatenate([x[:SB_BLOCK], x[SB_BLOCK:]], axis=1)


def sb_fwd(proj, kv, name):
    S = proj.shape[0]
    n_g = GM_W // SB_COLS
    tq = _tile(S, SB_Q_CHUNK, SB_BLOCK)
    nqb = tq // SB_BLOCK

    def body(q_ref, k_ref, v_ref, o_ref, s_ref):
        chunk = pl.program_id(1)
        row, col, tri, _ = _sb_consts()
        masks = [_head_mask(h, LANES) for h in range(2)]
        pairs = [slice(p * LANES, (p + 1) * LANES) for p in range(SB_PAIRS)]

        def q_loop(qi, _):
            i = chunk * nqb + qi
            rows = pl.ds(pl.multiple_of(qi * SB_BLOCK, SB_BLOCK), SB_BLOCK)
            qps = [_stack_heads(q_ref[rows, cs], masks) for cs in pairs]

            def look_ahead(t):
                keys = pl.ds(pl.multiple_of(jnp.maximum(i - t, 0) * SB_BLOCK, SB_BLOCK), SB_BLOCK)
                sums = []
                for p, cs in enumerate(pairs):
                    s, lom_sum, _ = _sb_scores(qps[p], k_ref[keys, cs], t * SB_BLOCK, row, col, tri)
                    s_ref[p] = s
                    sums.append(lom_sum)
                return tuple(sums)

            def k_step(carry):
                t, accs, runs, sums, _ = carry
                keys = pl.ds(pl.multiple_of((i - t) * SB_BLOCK, SB_BLOCK), SB_BLOCK)
                new_accs, new_runs = [], []
                for p, cs in enumerate(pairs):
                    a = _sb_weights(s_ref[p], runs[p], t * SB_BLOCK, row, col)
                    a_hi, a_lo = _split_bf(_side_by_side(a))
                    vp = _stack_heads(v_ref[keys, cs], masks)
                    new_accs.append(accs[p] + _dot(a_hi, vp) + _dot(a_lo, vp))
                    new_runs.append(runs[p] + sums[p])
                return t + 1, tuple(new_accs), tuple(new_runs), look_ahead(t + 1), _sb_alive(new_runs)

            zero = jnp.zeros((2 * SB_BLOCK, 1), F32)
            _, accs, _, _, _ = lax.while_loop(
                lambda carry: jnp.logical_and(carry[0] <= i, carry[4]), k_step,
                (jnp.int32(0), (jnp.zeros((SB_BLOCK, LANES), F32),) * SB_PAIRS, (zero,) * SB_PAIRS,
                 look_ahead(jnp.int32(0)), jnp.bool_(True)))
            for p, cs in enumerate(pairs):
                o_ref[rows, cs] = accs[p]
            return 0

        lax.fori_loop(0, nqb, q_loop, 0)

    return pl.pallas_call(
        body, grid=(n_g, S // tq),
        in_specs=[pl.BlockSpec((tq, SB_COLS), lambda g, c: (c, g)),
                  pl.BlockSpec((S, SB_COLS), lambda g, c: (0, g)),
                  pl.BlockSpec((S, SB_COLS), lambda g, c: (0, n_g + g))],
        out_specs=pl.BlockSpec((tq, SB_COLS), lambda g, c: (c, g)),
        out_shape=jax.ShapeDtypeStruct((S, GM_W), F32),
        scratch_shapes=[pltpu.VMEM((SB_PAIRS, 2 * SB_BLOCK, LANES), F32)],
        compiler_params=_cparams(("parallel", "parallel")), name=name)(proj, kv, kv)


def sb_bwd(proj, kv, dy, out, dk_init, dv_init, name):
    S = proj.shape[0]
    n_g = GM_W // SB_COLS
    tq = _tile(S, SB_Q_CHUNK, SB_BLOCK)
    nqb = tq // SB_BLOCK
    n_chunks = S // tq
    has_init = dk_init is not None
    scale = HEAD_DIM ** -0.5

    def body(*refs):
        s_ref, beta_ref, da_ref = refs[-3:]
        if has_init:
            q_ref, k_ref, v_ref, do_ref, out_ref, dki_ref, dvi_ref, dq_ref, dko_ref, dvo_ref, dk_acc, dv_acc = refs[:-3]
        else:
            q_ref, k_ref, v_ref, do_ref, out_ref, dq_ref, dko_ref, dvo_ref, dk_acc, dv_acc = refs[:-3]
        chunk = pl.program_id(1)
        cols = pl.ds(pl.multiple_of(pl.program_id(0) * SB_COLS, SB_COLS), SB_COLS)

        @pl.when(chunk == 0)
        def _():
            if has_init:
                pltpu.sync_copy(dki_ref.at[:, cols], dk_acc)
                pltpu.sync_copy(dvi_ref.at[:, cols], dv_acc)
            else:
                dk_acc[...] = jnp.zeros_like(dk_acc)
                dv_acc[...] = jnp.zeros_like(dv_acc)

        row, col, tri, tri_inc = _sb_consts()
        masks = [_head_mask(h, LANES) for h in range(2)]
        pairs = [slice(p * LANES, (p + 1) * LANES) for p in range(SB_PAIRS)]

        def q_loop(qi, _):
            i = chunk * nqb + qi
            rows = pl.ds(pl.multiple_of(qi * SB_BLOCK, SB_BLOCK), SB_BLOCK)
            qps = [_stack_heads(q_ref[rows, cs], masks) for cs in pairs]
            dops = [_stack_heads(do_ref[rows, cs], masks) for cs in pairs]
            e_tots = [jnp.sum(dop.astype(F32) * jnp.concatenate([out_ref[rows, cs]] * 2, axis=0), axis=1,
                              keepdims=True) for dop, cs in zip(dops, pairs)]

            def look_ahead(t):
                keys = pl.ds(pl.multiple_of(jnp.maximum(i - t, 0) * SB_BLOCK, SB_BLOCK), SB_BLOCK)
                sums = []
                for p, cs in enumerate(pairs):
                    s, lom_sum, lb = _sb_scores(qps[p], k_ref[keys, cs], t * SB_BLOCK, row, col, tri)
                    s_ref[p] = s
                    beta_ref[p] = jnp.exp(lb)
                    da_ref[p] = _dot_nt(dops[p], v_ref[keys, cs])
                    sums.append(lom_sum)
                return tuple(sums)

            def k_step(carry):
                t, dqs, runs, e_runs, sums, _ = carry
                keys = pl.ds(pl.multiple_of((i - t) * SB_BLOCK, SB_BLOCK), SB_BLOCK)
                new_dqs, new_runs, new_e_runs = [], [], []
                for p, cs in enumerate(pairs):
                    a = _sb_weights(s_ref[p], runs[p], t * SB_BLOCK, row, col)
                    e = a * da_ref[p]
                    e_hi, e_lo = _split_bf(e)
                    before = e_tots[p] - e_runs[p] - (_dot(e_hi, tri_inc) + _dot(e_lo, tri_inc))
                    beta = beta_ref[p]
                    dz = jnp.where(_sb_mask(t * SB_BLOCK, row, col), e * (1.0 - beta) - before * beta, 0.0)
                    dz = _bf(dz * scale)
                    new_dqs.append(dqs[p] + _dot(_side_by_side(dz), _stack_heads(k_ref[keys, cs], masks)))
                    dk_acc[keys, cs] += _dot_tn(dz, qps[p])
                    dv_acc[keys, cs] += _dot_tn(_bf(a), dops[p])
                    new_runs.append(runs[p] + sums[p])
                    new_e_runs.append(e_runs[p] + jnp.sum(e, axis=1, keepdims=True))
                return (t + 1, tuple(new_dqs), tuple(new_runs), tuple(new_e_runs), look_ahead(t + 1),
                        _sb_alive(new_runs))

            zero = jnp.zeros((2 * SB_BLOCK, 1), F32)
            zeros = (zero,) * SB_PAIRS
            _, dqs, _, _, _, _ = lax.while_loop(
                lambda carry: jnp.logical_and(carry[0] <= i, carry[5]), k_step,
                (jnp.int32(0), (jnp.zeros((SB_BLOCK, LANES), F32),) * SB_PAIRS, zeros, zeros,
                 look_ahead(jnp.int32(0)), jnp.bool_(True)))
            for p, cs in enumerate(pairs):
                dq_ref[rows, cs] = dqs[p]
            return 0

        lax.fori_loop(0, nqb, q_loop, 0)

        @pl.when(chunk == n_chunks - 1)
        def _():
            pltpu.sync_copy(dk_acc, dko_ref.at[:, cols])
            pltpu.sync_copy(dv_acc, dvo_ref.at[:, cols])

    qspec = pl.BlockSpec((tq, SB_COLS), lambda g, c: (c, g))
    kspec = pl.BlockSpec((S, SB_COLS), lambda g, c: (0, g))
    in_specs = [qspec, kspec, pl.BlockSpec((S, SB_COLS), lambda g, c: (0, n_g + g)), qspec, qspec]
    ins = [proj, kv, kv, dy, out]
    if has_init:
        in_specs += [_ANY, _ANY]
        ins += [dk_init, dv_init]
    sh = jax.ShapeDtypeStruct((S, GM_W), F32)
    return pl.pallas_call(
        body, grid=(n_g, n_chunks), in_specs=in_specs, out_specs=[qspec, _ANY, _ANY],
        out_shape=[sh, sh, sh],
        scratch_shapes=[pltpu.VMEM((S, SB_COLS), F32), pltpu.VMEM((S, SB_COLS), F32)]
        + [pltpu.VMEM((SB_PAIRS, 2 * SB_BLOCK, LANES), F32)] * 3,
        compiler_params=_cparams(("arbitrary", "arbitrary")), name=name)(*ins)


def adamw(w, g, m, v, name):
    shape = w.shape
    C = shape[-1] if w.ndim > 1 else shape[0]
    R = w.size // C
    tr = _tile(R, ROW_TILE, 8)

    def body(w_ref, g_ref, m_ref, v_ref, d_ref, nm_ref, nv_ref):
        gv = g_ref[...]
        m2 = ADAM_B1 * m_ref[...] + (1.0 - ADAM_B1) * gv
        v2 = ADAM_B2 * v_ref[...] + (1.0 - ADAM_B2) * (gv * gv)
        m_hat = m2 / (1.0 - ADAM_B1 ** ADAM_STEP)
        v_hat = v2 / (1.0 - ADAM_B2 ** ADAM_STEP)
        d_ref[...] = -ADAM_LR * (m_hat / (jnp.sqrt(v_hat) + ADAM_EPS) + ADAM_WD * w_ref[...])
        nm_ref[...] = m2
        nv_ref[...] = v2

    blk = pl.BlockSpec((tr, C), lambda i: (i, 0))
    sh = jax.ShapeDtypeStruct((R, C), F32)
    outs = pl.pallas_call(
        body, grid=(R // tr,), in_specs=[blk] * 4, out_specs=[blk] * 3, out_shape=[sh] * 3,
        compiler_params=_cparams(("parallel",)), name=name)(
            w.reshape(R, C), g.reshape(R, C), m.reshape(R, C), v.reshape(R, C))
    return tuple(o.reshape(shape) for o in outs)


def _place():
    return lax.axis_index("x"), lax.axis_index("y"), lax.axis_index("c")


def _other_chips(x, y):
    return [(1 - x, y), (x, 1 - y), (1 - x, 1 - y)]


_ANY = pl.BlockSpec(memory_space=pl.ANY)
LOCAL_CHUNKS = 8
AG_CHUNKS = 4


def _local_copies(src_of, dst_of, n_rows, sems, sem_base):
    rows = n_rows // LOCAL_CHUNKS
    copies = []
    for k in range(LOCAL_CHUNKS):
        r = pl.ds(k * rows, rows)
        cp = pltpu.make_async_copy(src_of(r), dst_of(r), sems.at[sem_base + k])
        cp.start()
        copies.append(cp)
    return copies


def _remote(src, dst, send_sems, recv_sems, k, to):
    return pltpu.make_async_remote_copy(src_ref=src, dst_ref=dst, send_sem=send_sems.at[k], recv_sem=recv_sems.at[k],
                                        device_id=to, device_id_type=MESH)


def all_gather_chips(bufs, name):
    nb = len(bufs)

    def body(*refs):
        in_refs, out_refs = refs[:nb], refs[nb:2 * nb]
        send_sems, recv_sems, local_sems = refs[2 * nb:]
        x, y, c = _place()
        sibling = (x, y, 1 - c)
        chips = _other_chips(x, y)

        n_sem = 2 * 3 * AG_CHUNKS

        def piece(b, px, py, pc, q):
            rows = bufs[b].shape[0] // (2 * AG_CHUNKS)
            return out_refs[b].at[2 * px + py, pl.ds((pc * AG_CHUNKS + q) * rows, rows), :]

        def my_piece(b, q):
            rows = bufs[b].shape[0] // (2 * AG_CHUNKS)
            return in_refs[b].at[pl.ds((c * AG_CHUNKS + q) * rows, rows), :]

        def sem(b, stage, j, q):
            return n_sem * b + (stage * 3 + j) * AG_CHUNKS + q

        mine, first, passed = [], [], []
        for q in range(AG_CHUNKS):
            for b in range(nb):
                for j, chip in enumerate(chips):
                    cp = _remote(my_piece(b, q), piece(b, x, y, c, q), send_sems, recv_sems, sem(b, 0, j, q),
                                 (*chip, c))
                    cp.start()
                    first.append(cp)
        for b in range(nb):
            mine += _local_copies(lambda r, b=b: in_refs[b].at[r, :], lambda r, b=b: out_refs[b].at[2 * x + y, r, :],
                                  bufs[b].shape[0], local_sems, b * LOCAL_CHUNKS)
        for q in range(AG_CHUNKS):
            for j, chip in enumerate(chips):
                for b in range(nb):
                    landed = piece(b, *chip, c, q)
                    _remote(landed, landed, send_sems, recv_sems, sem(b, 0, j, q), (*chip, c)).wait_recv()
                    cp = _remote(landed, landed, send_sems, recv_sems, sem(b, 1, j, q), sibling)
                    cp.start()
                    passed.append(cp)
        for q in range(AG_CHUNKS):
            for j, chip in enumerate(chips):
                for b in range(nb):
                    landed = piece(b, *chip, 1 - c, q)
                    _remote(landed, landed, send_sems, recv_sems, sem(b, 1, j, q), sibling).wait_recv()
        for cp in first + passed:
            cp.wait_send()
        for cp in mine:
            cp.wait()

    n_sems = 2 * 3 * AG_CHUNKS * nb
    return pl.pallas_call(
        body, in_specs=[_ANY] * nb, out_specs=[_ANY] * nb,
        out_shape=[jax.ShapeDtypeStruct((N_CHIPS,) + b.shape, b.dtype) for b in bufs],
        scratch_shapes=[pltpu.SemaphoreType.DMA((n_sems,)), pltpu.SemaphoreType.DMA((n_sems,)),
                        pltpu.SemaphoreType.DMA((LOCAL_CHUNKS * nb,))],
        name=name)(*bufs)


def swap_halves(gs, name):
    nb = len(gs)
    n = N_CHIPS

    def body(*refs):
        g_refs, t_refs = refs[:nb], refs[nb:2 * nb]
        send_sems, recv_sems = refs[2 * nb:]
        x, y, c = _place()
        gives = []
        for b in range(nb):
            Ph = gs[b].shape[1] // 2
            for k in range(n):
                gives.append(_remote(g_refs[b].at[k, pl.ds((1 - c) * Ph, Ph), :], t_refs[b].at[k], send_sems,
                                     recv_sems, n * b + k, (x, y, 1 - c)))
        for cp in gives:
            cp.start()
        for cp in gives:
            cp.wait()

    return pl.pallas_call(
        body, in_specs=[_ANY] * nb, out_specs=[_ANY] * nb,
        out_shape=[jax.ShapeDtypeStruct((n, g.shape[1] // 2, g.shape[2]), g.dtype) for g in gs],
        scratch_shapes=[pltpu.SemaphoreType.DMA((n * nb,)), pltpu.SemaphoreType.DMA((n * nb,))],
        name=name)(*gs)


def _core_index():
    return lax.axis_index("c").astype(jnp.int32).reshape(1)


def add_cores(g, theirs, name):
    n, Ph, Wd = theirs.shape
    tr = _tile(Ph, ROW_TILE, 16)
    steps = Ph // tr

    def body(c_ref, g_ref, t_ref, o_ref):
        o_ref[...] = _bf(g_ref[...] + t_ref[...])

    blk = pl.BlockSpec((None, tr, Wd), lambda k, i, c_ref: (k, i, 0))
    return pl.pallas_call(
        body,
        grid_spec=pltpu.PrefetchScalarGridSpec(
            num_scalar_prefetch=1, grid=(n, steps),
            in_specs=[pl.BlockSpec((None, tr, Wd), lambda k, i, c_ref: (k, c_ref[0] * steps + i, 0)), blk],
            out_specs=blk),
        out_shape=jax.ShapeDtypeStruct((n, Ph, Wd), BF16),
        compiler_params=_cparams(("parallel", "parallel")), name=name)(_core_index(), g, theirs)


def add_chips(slots, name):
    n, Ph, Wd = slots.shape
    tr = _tile(Ph, ROW_TILE, 16)
    steps = Ph // tr

    def body(c_ref, *refs):
        acc = refs[0][...].astype(F32)
        for k in range(1, n):
            acc = acc + refs[k][...].astype(F32)
        refs[n][...] = acc

    in_specs = [pl.BlockSpec((None, tr, Wd), functools.partial(lambda i, c_ref, k: (k, i, 0), k=k)) for k in range(n)]
    return pl.pallas_call(
        body,
        grid_spec=pltpu.PrefetchScalarGridSpec(
            num_scalar_prefetch=1, grid=(steps,), in_specs=in_specs,
            out_specs=pl.BlockSpec((tr, Wd), lambda i, c_ref: (c_ref[0] * steps + i, 0))),
        out_shape=jax.ShapeDtypeStruct((2 * Ph, Wd), F32),
        compiler_params=_cparams(("parallel",)), name=name)(_core_index(), *([slots] * n))


def scatter_to_chips(hs, name):
    nb = len(hs)

    def body(*refs):
        h_refs, out_refs = refs[:nb], refs[nb:2 * nb]
        send_sems, recv_sems, local_sems = refs[2 * nb:]
        x, y, c = _place()
        me = 2 * x + y
        chips = _other_chips(x, y)
        keep, sends = [], []
        for b in range(nb):
            keep += _local_copies(lambda r, b=b: h_refs[b].at[me, r, :], lambda r, b=b: out_refs[b].at[me, r, :],
                                  hs[b].shape[1], local_sems, b * LOCAL_CHUNKS)
            for j, (cx, cy) in enumerate(chips):
                cp = _remote(h_refs[b].at[2 * cx + cy], out_refs[b].at[me], send_sems, recv_sems, 3 * b + j,
                             (cx, cy, c))
                cp.start()
                sends.append(cp)
        for b in range(nb):
            for j, (cx, cy) in enumerate(chips):
                _remote(h_refs[b].at[me], out_refs[b].at[2 * cx + cy], send_sems, recv_sems, 3 * b + j,
                        (cx, cy, c)).wait_recv()
        for cp in sends:
            cp.wait_send()
        for cp in keep:
            cp.wait()

    return pl.pallas_call(
        body, in_specs=[_ANY] * nb, out_specs=[_ANY] * nb,
        out_shape=[jax.ShapeDtypeStruct(h.shape, h.dtype) for h in hs],
        scratch_shapes=[pltpu.SemaphoreType.DMA((3 * nb,)), pltpu.SemaphoreType.DMA((3 * nb,)),
                        pltpu.SemaphoreType.DMA((LOCAL_CHUNKS * nb,))],
        name=name)(*hs)


JOIN_CHUNKS = 4


def join_halves(ts, name):
    nb = len(ts)
    n = JOIN_CHUNKS

    def body(*refs):
        out_refs = refs[nb:2 * nb]
        send_sems, recv_sems = refs[2 * nb:]
        x, y, c = _place()
        gives = []
        for b in range(nb):
            Ph = ts[b].shape[0] // 2
            rows = Ph // n
            for k in range(n):
                part = out_refs[b].at[pl.ds(c * Ph + k * rows, rows), :]
                gives.append(_remote(part, part, send_sems, recv_sems, n * b + k, (x, y, 1 - c)))
        for cp in gives:
            cp.start()
        for b in range(nb):
            Ph = ts[b].shape[0] // 2
            rows = Ph // n
            for k in range(n):
                landed = out_refs[b].at[pl.ds((1 - c) * Ph + k * rows, rows), :]
                _remote(landed, landed, send_sems, recv_sems, n * b + k, (x, y, 1 - c)).wait_recv()
        for cp in gives:
            cp.wait_send()

    return pl.pallas_call(
        body, in_specs=[_ANY] * nb, out_specs=[_ANY] * nb,
        out_shape=[jax.ShapeDtypeStruct(t.shape, t.dtype) for t in ts],
        scratch_shapes=[pltpu.SemaphoreType.DMA((n * nb,)), pltpu.SemaphoreType.DMA((n * nb,))],
        input_output_aliases={b: b for b in range(nb)}, name=name)(*ts)


WIDE = ['ffn1_w_gate', 'ffn1_w_up', 'ffn2_w_gate', 'ffn2_w_up']


def _rows_of(p):
    if p.shape[-1] == PACK_W and (p.size // PACK_W) % PART_ROWS == 0:
        return p.reshape(-1, PACK_W)
    flat = p.reshape(-1)
    rows = -(-flat.shape[0] // (PACK_W * PART_ROWS)) * PART_ROWS
    return jnp.pad(flat, (0, rows * PACK_W - flat.shape[0])).reshape(rows, PACK_W)


def _pack(parts):
    buf = jnp.concatenate([_rows_of(p) for p in parts], axis=0)
    rows = buf.shape[0]
    total = -(-rows // PACK_ROWS) * PACK_ROWS
    return jnp.pad(buf, ((0, total - rows), (0, 0)))


def _unpack(buf, shapes):
    outs, r = [], 0
    lead = buf.shape[:-2]
    for shp in shapes:
        n = math.prod(shp)
        rows = -(-n // (PACK_W * PART_ROWS)) * PART_ROWS
        blk = buf[..., r:r + rows, :]
        if n != rows * PACK_W:
            blk = blk.reshape(lead + (rows * PACK_W,))[..., :n]
        outs.append(blk.reshape(lead + tuple(shp)))
        r += rows
    return outs


def gather_weights(shards):
    narrow = [n for n in W_NAMES if n in SHARD_AXIS and n not in WIDE]
    parts = [lax.bitcast_convert_type(shards[n], BF16) if n == 'a_v_norm' else shards[n].astype(BF16)
             for n in narrow]
    wide = [shards[n].astype(BF16) for n in WIDE]
    cols = wide[0].shape[-1]
    wide_buf = jnp.concatenate([w.reshape(-1, cols) for w in wide], axis=0)
    full_wide, full_narrow = all_gather_chips([wide_buf, _pack(parts)], "all_gather_weights")
    out = {}
    rows = wide[0].size // cols
    for i, n in enumerate(WIDE):
        g = full_wide[:, i * rows:(i + 1) * rows, :].reshape((N_CHIPS,) + wide[i].shape)
        out[n] = jnp.concatenate([g[k] for k in range(N_CHIPS)], axis=SHARD_AXIS[n])
    for n, g in zip(narrow, _unpack(full_narrow, [p.shape for p in parts])):
        if n == 'a_v_norm':
            g = lax.bitcast_convert_type(g, F32)
        out[n] = jnp.concatenate([g[k] for k in range(N_CHIPS)], axis=SHARD_AXIS[n])
    return out


def reduce_grads(grads, shard_shapes):
    narrow = [n for n in W_NAMES if n in SHARD_AXIS and n not in WIDE]
    repl = [n for n in W_NAMES if n not in SHARD_AXIS]
    full = {n: (jnp.stack(g, axis=0) if isinstance(g, list) else g) for n, g in grads.items() if n not in WIDE}
    cols = shard_shapes[WIDE[0]][-1]
    g_wide = jnp.stack([jnp.concatenate([g[:, k * cols:(k + 1) * cols] for n in WIDE for g in grads[n]], axis=0)
                        for k in range(N_CHIPS)], axis=0)
    blocks = []
    for k in range(N_CHIPS):
        parts = [jnp.split(full[n], N_CHIPS, axis=SHARD_AXIS[n])[k] for n in narrow]
        parts += [full[n] for n in repl]
        blocks.append(_pack(parts))
    gs = [g_wide, jnp.stack(blocks, axis=0)]
    theirs = swap_halves(gs, "grads_swap_halves")
    chip_sums = [add_cores(g, t, f"grads_add_cores_{i}") for i, (g, t) in enumerate(zip(gs, theirs))]
    slots = scatter_to_chips(chip_sums, "grads_scatter")
    totals = join_halves([add_chips(s, f"grads_add_chips_{i}") for i, s in enumerate(slots)], "grads_join_halves")
    out = {}
    rows = math.prod(shard_shapes[WIDE[0]][:-1])
    for i, n in enumerate(WIDE):
        out[n] = totals[0][i * rows:(i + 1) * rows, :].reshape(shard_shapes[n])
    shapes = [shard_shapes[n] for n in narrow] + [full[n].shape for n in repl]
    out.update(zip(narrow + repl, _unpack(totals[1], shapes)))
    return out


def _ffn_fwd(x, gain, wg, wu, wd, tag):
    h = rms_fwd(x, gain, tag + "_norm")
    g, u, a = ffn_up(h, wg, wu, tag + "_up")
    xo = mm([(a, 0, wd, a.shape[1], 0)], res=x, scale=0.5, name=tag + "_down")
    return xo, (x, h, g, u, a)


def _ffn_bwd(dxo, saved, gain, wg, wu, wd, tag):
    x, h, g, u, a = saved
    Fd = a.shape[1]
    dxo, dxo_b = dxo
    dgp, du = ffn_down_bwd(dxo_b, wd, g, u, tag + "_down_bwd")
    d_wd = mm_tn(a, dxo_b, scale=0.5, name=tag + "_dwd")
    d_wg = mm_tn(h, dgp, name=tag + "_dwg")
    d_wu = mm_tn(h, du, name=tag + "_dwu")
    dx, d_gain = mm([(dgp, 0, wg, Fd, 0), (du, 0, wu, Fd, 0)], tb=True, res=dxo, norm_bwd=(x, gain),
                    name=tag + "_up_bwd")
    return dx, d_gain, d_wg, d_wu, d_wd


def local_step(x, mem, W, target):
    D = x.shape[1]
    depth = W['ffn1_norm'].shape[0]
    n_a = W['a_w_in'].shape[0]
    G = {}
    mem_h = rms_fwd(mem, W['mem_norm'], "mem_norm")
    b_mats = jnp.broadcast_to(W['a_b_spatial'][..., None], W['a_b_spatial'].shape + (GM_CHUNK,))
    w_kv = (W['w_kv'], None)

    def ffn_w(which, l):
        return [(W[f'{which}_w_{part}'], l) for part in ('gate', 'up', 'down')]

    def mixer_w(l):
        if l < n_a:
            return (W['a_w_in'], l), (W['a_w_out'], l), 2 * GM_W
        return (W['b_w_in'], l - n_a), (W['b_w_out'], l - n_a), GM_W

    saved = []
    kv = kvn = x_kv = None
    for l in range(depth):
        if l == n_a:
            x_kv = x
            kvn = rms_fwd(x, W['kv_norm'], "kv_norm")
            kv = mm([(kvn, 0, w_kv, D, 0)], out_dtype=BF16, name="kv_proj")
        x, s1 = _ffn_fwd(x, W['ffn1_norm'][l], *ffn_w('ffn1', l), f"l{l}_ffn1")
        hm = rms_fwd(x, W['mix_norm'][l], f"l{l}_mix_norm")
        mem_kv = mm([(mem_h, 0, (W['w_mem_kv'], l), D, 0)], name=f"l{l}_mem_kv")
        w_in, w_out, tok_w = mixer_w(l)
        proj = mm([(hm, 0, w_in, D, 0)], name=f"l{l}_mix_in")
        if l < n_a:
            y_tok = gmlp_fwd(proj, W['a_v_norm'][l], W['a_w_spatial'][l], b_mats[l], f"l{l}_gmlp")
        else:
            y_tok = sb_fwd(proj, kv, f"l{l}_sb")
        q_cb = tok_w // MEM_W
        y_mem = memattn_fwd(proj, q_cb, mem_kv, f"l{l}_memattn")
        x_mid = x
        x = mm([(y_tok, 0, w_out, GM_W, 0), (y_mem, 0, w_out, MEM_W, GM_W // MEM_W)], res=x, name=f"l{l}_mix_out")
        sm = (x_mid, hm, mem_kv, proj, y_tok, y_mem, q_cb)
        x, s2 = _ffn_fwd(x, W['ffn2_norm'][l], *ffn_w('ffn2', l), f"l{l}_ffn2")
        saved.append((s1, sm, s2))

    loss, dx, d_final = final_loss(x, W['final_norm'], target, "loss_head")
    G['final_norm'] = d_final.reshape(-1)

    per_layer = {n: [None] * depth for n in ['ffn1_norm', 'ffn1_w_gate', 'ffn1_w_up', 'ffn1_w_down', 'mix_norm',
                                             'ffn2_norm', 'ffn2_w_gate', 'ffn2_w_up', 'ffn2_w_down', 'w_mem_kv']}
    per_a = {n: [None] * n_a for n in ['a_w_in', 'a_v_norm', 'a_w_spatial', 'a_b_spatial', 'a_w_out']}
    per_b = {n: [None] * (depth - n_a) for n in ['b_w_in', 'b_w_out']}
    d_mem_h = None
    dk = dv = None
    for l in reversed(range(depth)):
        s1, sm, s2 = saved[l]
        dx, dg, dwg, dwu, dwd = _ffn_bwd(dx, s2, W['ffn2_norm'][l], *ffn_w('ffn2', l), f"l{l}_ffn2")
        per_layer['ffn2_norm'][l], per_layer['ffn2_w_gate'][l] = dg.reshape(-1), dwg
        per_layer['ffn2_w_up'][l], per_layer['ffn2_w_down'][l] = dwu, dwd

        x_mid, hm, mem_kv, proj, y_tok, y_mem, q_cb = sm
        is_a = l < n_a
        w_in, w_out, tok_w = mixer_w(l)
        dy = mm([(dx[1], 0, w_out, D, 0)], tb=True, name=f"l{l}_mix_out_bwd")
        d_w_out = jnp.concatenate([mm_tn(y_tok, dx[1], name=f"l{l}_dwout_tok"),
                                   mm_tn(y_mem, dx[1], name=f"l{l}_dwout_mem")], axis=0)
        dq_mem, d_mem_kv = memattn_bwd(proj, q_cb, mem_kv, dy, GM_W // MEM_W, f"l{l}_memattn_bwd")
        if is_a:
            d_tok, d_ws, d_bs, d_vg = gmlp_bwd(proj, dy, W['a_v_norm'][l], W['a_w_spatial'][l], b_mats[l],
                                               f"l{l}_gmlp_bwd")
            per_a['a_w_spatial'][l], per_a['a_b_spatial'][l] = d_ws, d_bs[:, :, 0]
            per_a['a_v_norm'][l], per_a['a_w_out'][l] = d_vg.reshape(-1), d_w_out
        else:
            d_tok, dk, dv = sb_bwd(proj, kv, dy, y_tok, dk, dv, f"l{l}_sb_bwd")
            per_b['b_w_out'][l - n_a] = d_w_out
        d_w_in = jnp.concatenate([mm_tn(hm, d_tok, name=f"l{l}_dwin_tok"),
                                  mm_tn(hm, dq_mem, name=f"l{l}_dwin_mem")], axis=1)
        (per_a['a_w_in'] if is_a else per_b['b_w_in'])[l if is_a else l - n_a] = d_w_in
        dx, dg = mm([(d_tok, 0, w_in, tok_w, 0), (dq_mem, 0, w_in, MEM_W, tok_w // MEM_W)], tb=True, res=dx[0],
                    norm_bwd=(x_mid, W['mix_norm'][l]), name=f"l{l}_mix_in_bwd")
        per_layer['mix_norm'][l] = dg.reshape(-1)
        per_layer['w_mem_kv'][l] = mm_tn(mem_h, d_mem_kv, name=f"l{l}_dw_mem_kv")
        d_mem_h = mm([(d_mem_kv, 0, (W['w_mem_kv'], l), 2 * MEM_W, 0)], tb=True, res=d_mem_h,
                     name=f"l{l}_mem_kv_bwd")

        dx, dg, dwg, dwu, dwd = _ffn_bwd(dx, s1, W['ffn1_norm'][l], *ffn_w('ffn1', l), f"l{l}_ffn1")
        per_layer['ffn1_norm'][l], per_layer['ffn1_w_gate'][l] = dg.reshape(-1), dwg
        per_layer['ffn1_w_up'][l], per_layer['ffn1_w_down'][l] = dwu, dwd
        if l == n_a:
            G['w_kv'] = jnp.concatenate([mm_tn(kvn, dk, name="dw_k"), mm_tn(kvn, dv, name="dw_v")], axis=1)
            dx, dg = mm([(dk, 0, w_kv, GM_W, 0), (dv, 0, w_kv, GM_W, 1)], tb=True, res=dx[0],
                        norm_bwd=(x_kv, W['kv_norm']), name="kv_proj_bwd")
            G['kv_norm'] = dg.reshape(-1)

    G['mem_norm'] = rms_gain_grad(mem, d_mem_h, "mem_norm_bwd").reshape(-1)
    for d in (per_layer, per_a, per_b):
        G.update(d)
    return loss, dx[0], G


def kernel(x, mem, ffn1_norm, ffn1_w_gate, ffn1_w_up, ffn1_w_down, mix_norm, ffn2_norm, ffn2_w_gate, ffn2_w_up, ffn2_w_down, mem_norm, w_mem_kv, a_w_in, a_v_norm, a_w_spatial, a_b_spatial, a_w_out, kv_norm, w_kv, b_w_in, b_w_out, final_norm, loss_target, m_ffn1_norm, m_ffn1_w_gate, m_ffn1_w_up, m_ffn1_w_down, m_mix_norm, m_ffn2_norm, m_ffn2_w_gate, m_ffn2_w_up, m_ffn2_w_down, m_mem_norm, m_w_mem_kv, m_a_w_in, m_a_v_norm, m_a_w_spatial, m_a_b_spatial, m_a_w_out, m_kv_norm, m_w_kv, m_b_w_in, m_b_w_out, m_final_norm, v_ffn1_norm, v_ffn1_w_gate, v_ffn1_w_up, v_ffn1_w_down, v_mix_norm, v_ffn2_norm, v_ffn2_w_gate, v_ffn2_w_up, v_ffn2_w_down, v_mem_norm, v_w_mem_kv, v_a_w_in, v_a_v_norm, v_a_w_spatial, v_a_b_spatial, v_a_w_out, v_kv_norm, v_w_kv, v_b_w_in, v_b_w_out, v_final_norm):
    weights = dict(zip(W_NAMES, [ffn1_norm, ffn1_w_gate, ffn1_w_up, ffn1_w_down, mix_norm, ffn2_norm, ffn2_w_gate,
                                 ffn2_w_up, ffn2_w_down, mem_norm, w_mem_kv, a_w_in, a_v_norm, a_w_spatial,
                                 a_b_spatial, a_w_out, kv_norm, w_kv, b_w_in, b_w_out, final_norm]))
    m_in = dict(zip(W_NAMES, [m_ffn1_norm, m_ffn1_w_gate, m_ffn1_w_up, m_ffn1_w_down, m_mix_norm, m_ffn2_norm,
                              m_ffn2_w_gate, m_ffn2_w_up, m_ffn2_w_down, m_mem_norm, m_w_mem_kv, m_a_w_in,
                              m_a_v_norm, m_a_w_spatial, m_a_b_spatial, m_a_w_out, m_kv_norm, m_w_kv, m_b_w_in,
                              m_b_w_out, m_final_norm]))
    v_in = dict(zip(W_NAMES, [v_ffn1_norm, v_ffn1_w_gate, v_ffn1_w_up, v_ffn1_w_down, v_mix_norm, v_ffn2_norm,
                              v_ffn2_w_gate, v_ffn2_w_up, v_ffn2_w_down, v_mem_norm, v_w_mem_kv, v_a_w_in,
                              v_a_v_norm, v_a_w_spatial, v_a_b_spatial, v_a_w_out, v_kv_norm, v_w_kv, v_b_w_in,
                              v_b_w_out, v_final_norm]))

    full = gather_weights({n: weights[n] for n in SHARD_AXIS})
    W = {n: (full[n] if n in SHARD_AXIS else weights[n]) for n in W_NAMES}
    loss, dx, grads = local_step(x[0], mem[0], W, loss_target[0])
    total = reduce_grads(grads, {n: weights[n].shape for n in SHARD_AXIS})
    loss = lax.psum(loss[0, 0], ("x", "y", "c"))

    deltas, new_m, new_v = {}, {}, {}
    for n in W_NAMES:
        deltas[n], new_m[n], new_v[n] = adamw(weights[n], total[n], m_in[n], v_in[n], "adamw_" + n)
    return (loss, dx[None], *[total[n] for n in W_NAMES], *[deltas[n] for n in W_NAMES],
            *[new_m[n] for n in W_NAMES], *[new_v[n] for n in W_NAMES])
```

```python
import functools
import math

import jax
import jax.numpy as jnp
from jax import lax
from jax.experimental import pallas as pl
from jax.experimental.pallas import tpu as pltpu

F32 = jnp.float32
BF16 = jnp.bfloat16
EPS = 1e-6
LANES = 128
ROW_TILE = 512
COL_TILE = 1408
PACK_ROWS = 256
PART_ROWS = 16
VMEM_LIMIT = 56 * 1024 * 1024

W_NAMES = ['ffn1_norm', 'ffn1_w_gate', 'ffn1_w_up', 'ffn1_w_down', 'mix_norm', 'ffn2_norm', 'ffn2_w_gate',
           'ffn2_w_up', 'ffn2_w_down', 'mem_norm', 'w_mem_kv', 'a_w_in', 'a_v_norm', 'a_w_spatial',
           'a_b_spatial', 'a_w_out', 'kv_norm', 'w_kv', 'b_w_in', 'b_w_out', 'final_norm']
SHARD_AXIS = {'ffn1_w_gate': 2, 'ffn1_w_up': 2, 'ffn1_w_down': 1, 'ffn2_w_gate': 2, 'ffn2_w_up': 2,
              'ffn2_w_down': 1, 'w_mem_kv': 1, 'a_w_in': 2, 'a_v_norm': 1, 'a_w_out': 1, 'w_kv': 1,
              'b_w_in': 1, 'b_w_out': 1}
N_CHIPS = 4
PACK_W = 1024

MEM_HEADS = 4
MEM_W = 256
HEAD_DIM = 64
GM_W = 768
GM_GROUPS = 6
GM_CHUNK = 128
CHUNK = 64
SB_BLOCK = 128
SB_Q_CHUNK = 1024
SB_DEAD = -110.0
SB_PAIRS = 2
SB_COLS = SB_PAIRS * LANES

ADAM_LR, ADAM_B1, ADAM_B2, ADAM_EPS, ADAM_WD, ADAM_STEP = 0.001, 0.9, 0.999, 1e-08, 0.01, 10

MESH = pl.DeviceIdType.MESH


def _cparams(sem=None):
    return pltpu.CompilerParams(dimension_semantics=sem, vmem_limit_bytes=VMEM_LIMIT)


class Hosted:
    def __init__(self, arrays, n_out, sems, make):
        self.arrays, self.n_out, self.sems, self.make = arrays, n_out, sems, make


def _call(body, *, grid, in_specs, out_specs, out_shape, ins, semantics, name, hosted=None):
    if hosted is None:
        return pl.pallas_call(body, grid=grid, in_specs=in_specs, out_specs=out_specs, out_shape=out_shape,
                              compiler_params=_cparams(semantics), name=name)(*ins)
    n_in, n_out = len(ins), len(out_shape)
    n_h, n_ho = len(hosted.arrays), hosted.n_out

    def hosting_body(*refs):
        base_in, h_in = refs[:n_in], refs[n_in:n_in + n_h]
        base_out = refs[n_in + n_h:n_in + n_h + n_out]
        h_out = refs[n_in + n_h + n_out:n_in + n_h + n_out + n_ho]
        sems = refs[n_in + n_h + n_out + n_ho:]
        copies, waits = hosted.make(h_in, h_out, *sems)
        first = last = None
        for axis, extent in enumerate(grid):
            at_start, at_end = pl.program_id(axis) == 0, pl.program_id(axis) == extent - 1
            first = at_start if first is None else jnp.logical_and(first, at_start)
            last = at_end if last is None else jnp.logical_and(last, at_end)

        @pl.when(first)
        def _():
            for cp in copies:
                cp.start()

        body(*base_in, *base_out)

        @pl.when(last)
        def _():
            for wait in waits:
                wait()

    any_spec = pl.BlockSpec(memory_space=pl.ANY)
    outs = pl.pallas_call(
        hosting_body, grid=grid, in_specs=list(in_specs) + [any_spec] * n_h,
        out_specs=list(out_specs) + [any_spec] * n_ho,
        out_shape=list(out_shape) + [jax.ShapeDtypeStruct(a.shape, a.dtype) for a in hosted.arrays[:n_ho]],
        scratch_shapes=[pltpu.SemaphoreType.DMA((n,)) for n in hosted.sems],
        input_output_aliases={n_in + k: n_out + k for k in range(n_ho)},
        compiler_params=_cparams(("arbitrary",) * len(grid)), name=name)(*ins, *hosted.arrays)
    return outs[:n_out], outs[n_out:]


def _tile(n, target, mult=LANES):
    best = None
    for t in range(mult, min(n, target) + 1, mult):
        if n % t == 0:
            best = t
    return best if best is not None else n


def _dot(a, b):
    return jnp.dot(a, b, preferred_element_type=F32)


def _dot_nt(a, b):
    return lax.dot_general(a, b, (((1,), (1,)), ((), ())), preferred_element_type=F32)


def _dot_tn(a, b):
    return lax.dot_general(a, b, (((0,), (0,)), ((), ())), preferred_element_type=F32)


def _bf(v):
    return v.astype(BF16)


def rms_fwd(x, gain, name):
    S, D = x.shape
    tm = _tile(S, ROW_TILE, 8)

    def body(x_ref, g_ref, o_ref):
        xf = x_ref[...]
        r = lax.rsqrt(jnp.mean(xf * xf, axis=-1, keepdims=True) + EPS)
        o_ref[...] = ((xf * r) * g_ref[...]).astype(o_ref.dtype)

    return pl.pallas_call(
        body, grid=(S // tm,),
        in_specs=[pl.BlockSpec((tm, D), lambda i: (i, 0)), pl.BlockSpec((1, D), lambda i: (0, 0))],
        out_specs=pl.BlockSpec((tm, D), lambda i: (i, 0)),
        out_shape=jax.ShapeDtypeStruct((S, D), BF16),
        compiler_params=_cparams(("parallel",)), name=name)(x, gain.reshape(1, D))


def rms_gain_grad(x, dh, name):
    S, D = x.shape
    tm = _tile(S, ROW_TILE, 8)

    def body(x_ref, dh_ref, dg_ref):
        xf = x_ref[...]
        r = lax.rsqrt(jnp.mean(xf * xf, axis=-1, keepdims=True) + EPS)

        @pl.when(pl.program_id(0) == 0)
        def _():
            dg_ref[...] = jnp.zeros_like(dg_ref)

        dg_ref[...] += jnp.sum(dh_ref[...] * (xf * r), axis=0, keepdims=True)

    row = pl.BlockSpec((tm, D), lambda i: (i, 0))
    return pl.pallas_call(
        body, grid=(S // tm,), in_specs=[row, row], out_specs=pl.BlockSpec((1, D), lambda i: (0, 0)),
        out_shape=jax.ShapeDtypeStruct((1, D), F32),
        compiler_params=_cparams(("arbitrary",)), name=name)(x, dh)


def final_loss(x, gain, target, name):
    S, D = x.shape
    tm = _tile(S, ROW_TILE, 16)

    def body(x_ref, g_ref, t_ref, loss_ref, dx_ref, dxb_ref, dg_ref):
        xf = x_ref[...]
        r = lax.rsqrt(jnp.mean(xf * xf, axis=-1, keepdims=True) + EPS)
        xhat = xf * r
        g = g_ref[...]
        diff = xhat * g - t_ref[...]
        dy = diff * (1.0 / D)
        dxhat = dy * g
        m = jnp.mean(dxhat * xhat, axis=-1, keepdims=True)
        dx = r * (dxhat - xhat * m)
        dx_ref[...] = dx
        dxb_ref[...] = _bf(dx)

        @pl.when(pl.program_id(0) == 0)
        def _():
            dg_ref[...] = jnp.zeros_like(dg_ref)
            loss_ref[...] = jnp.zeros_like(loss_ref)

        dg_ref[...] += jnp.sum(dy * xhat, axis=0, keepdims=True)
        per_tok = jnp.sum(diff * diff, axis=-1, keepdims=True) * (0.5 / D)
        loss_ref[...] += jnp.sum(per_tok, axis=0, keepdims=True)

    row = pl.BlockSpec((tm, D), lambda i: (i, 0))
    vec = pl.BlockSpec((1, D), lambda i: (0, 0))
    one = pl.BlockSpec((1, 1), lambda i: (0, 0))
    loss, dx, dxb, dg = pl.pallas_call(
        body, grid=(S // tm,), in_specs=[row, vec, row], out_specs=[one, row, row, vec],
        out_shape=[jax.ShapeDtypeStruct((1, 1), F32), jax.ShapeDtypeStruct((S, D), F32),
                   jax.ShapeDtypeStruct((S, D), BF16), jax.ShapeDtypeStruct((1, D), F32)],
        compiler_params=_cparams(("arbitrary",)), name=name)(x, gain.reshape(1, D), target)
    return loss, (dx, dxb), dg


def _wspec(w, blk, idx):
    _, l = w
    if l is None:
        return pl.BlockSpec(blk, idx)
    return pl.BlockSpec((None,) + blk, lambda *g: (l,) + idx(*g))


def mm(pairs, *, tb=False, res=None, scale=1.0, out_dtype=F32, norm_bwd=None, hosted=None, name):
    M = pairs[0][0].shape[0]
    N = pairs[0][2][0].shape[-2 if tb else -1]
    tm = _tile(M, ROW_TILE, 16)
    tn = N if norm_bwd is not None else _tile(N, COL_TILE)
    n_p = len(pairs)
    has_res = res is not None
    n_in = 2 * n_p + has_res + (2 if norm_bwd is not None else 0)

    def body(*refs):
        acc = None
        for p in range(n_p):
            a = _bf(refs[2 * p][...])
            b = _bf(refs[2 * p + 1][...])
            d = _dot_nt(a, b) if tb else _dot(a, b)
            acc = d if acc is None else acc + d
        if scale != 1.0:
            acc = acc * scale
        if norm_bwd is None:
            if has_res:
                acc = acc + refs[2 * n_p][...]
            refs[n_in][...] = acc.astype(refs[n_in].dtype)
            return
        x_ref, g_ref = refs[n_in - 2], refs[n_in - 1]
        dx_ref, dxb_ref, dg_ref = refs[n_in:]
        xf = x_ref[...]
        r = lax.rsqrt(jnp.mean(xf * xf, axis=-1, keepdims=True) + EPS)
        xhat = xf * r
        dxhat = acc * g_ref[...]
        m = jnp.mean(dxhat * xhat, axis=-1, keepdims=True)
        dx = r * (dxhat - xhat * m)
        if has_res:
            dx = dx + refs[2 * n_p][...]
        dx_ref[...] = dx
        dxb_ref[...] = _bf(dx)

        @pl.when(pl.program_id(1) == 0)
        def _():
            dg_ref[...] = jnp.zeros_like(dg_ref)

        dg_ref[...] += jnp.sum(acc * xhat, axis=0, keepdims=True)

    ins, in_specs = [], []
    for a, cb, w, K, kb in pairs:
        ins += [a, w[0]]
        in_specs.append(pl.BlockSpec((tm, K), functools.partial(lambda j, i, cb: (i, cb), cb=cb)))
        if tb:
            in_specs.append(_wspec(w, (tn, K), functools.partial(lambda j, i, kb: (j, kb), kb=kb)))
        else:
            in_specs.append(_wspec(w, (K, tn), functools.partial(lambda j, i, kb: (kb, j), kb=kb)))
    tile = pl.BlockSpec((tm, tn), lambda j, i: (i, j))
    if has_res:
        ins.append(res)
        in_specs.append(tile)
    if norm_bwd is None:
        out = _call(body, grid=(N // tn, M // tm), in_specs=in_specs, out_specs=[tile],
                    out_shape=[jax.ShapeDtypeStruct((M, N), out_dtype)], ins=ins,
                    semantics=("parallel", "parallel"), name=name, hosted=hosted)
        return out[0] if hosted is None else (out[0][0], out[1])
    x, gain = norm_bwd
    vec = pl.BlockSpec((1, N), lambda j, i: (0, 0))
    dx, dxb, dg = pl.pallas_call(
        body, grid=(1, M // tm), in_specs=in_specs + [tile, vec], out_specs=[tile, tile, vec],
        out_shape=[jax.ShapeDtypeStruct((M, N), F32), jax.ShapeDtypeStruct((M, N), BF16),
                   jax.ShapeDtypeStruct((1, N), F32)],
        compiler_params=_cparams(("arbitrary", "arbitrary")), name=name)(*ins, x, gain.reshape(1, N))
    return (dx, dxb), dg


def mm_tn(a, b, *, a_cb=0, a_w=None, b_cb=0, b_w=None, scale=1.0, name):
    S = a.shape[0]
    Ka = a.shape[1] if a_w is None else a_w
    Nb = b.shape[1] if b_w is None else b_w
    ts = _tile(S, 2 * ROW_TILE, 8)
    tk = _tile(Ka, COL_TILE)
    tn = _tile(Nb, COL_TILE)
    a_off, b_off = a_cb * (Ka // tk), b_cb * (Nb // tn)

    def body(a_ref, b_ref, o_ref):
        s = pl.program_id(2)

        @pl.when(s == 0)
        def _():
            o_ref[...] = jnp.zeros_like(o_ref)

        o_ref[...] += _dot_tn(_bf(a_ref[...]), _bf(b_ref[...]))
        if scale != 1.0:
            @pl.when(s == pl.num_programs(2) - 1)
            def _():
                o_ref[...] = o_ref[...] * scale

    return pl.pallas_call(
        body, grid=(Ka // tk, Nb // tn, S // ts),
        in_specs=[pl.BlockSpec((ts, tk), lambda k, n, s: (s, a_off + k)),
                  pl.BlockSpec((ts, tn), lambda k, n, s: (s, b_off + n))],
        out_specs=pl.BlockSpec((tk, tn), lambda k, n, s: (k, n)),
        out_shape=jax.ShapeDtypeStruct((Ka, Nb), F32),
        compiler_params=_cparams(("parallel", "parallel", "arbitrary")), name=name)(a, b)


def _sigmoid(x):
    return 0.5 * jnp.tanh(0.5 * x) + 0.5


def ffn_up(h, wg, wu, name, hosted=None):
    S, D = h.shape
    Fd = wg[0].shape[-1]
    tm = _tile(S, ROW_TILE, 8)
    tn = _tile(Fd, COL_TILE)

    def body(h_ref, wg_ref, wu_ref, g_ref, u_ref, a_ref):
        hv = h_ref[...]
        g = _dot(hv, wg_ref[...])
        u = _dot(hv, wu_ref[...])
        g_ref[...] = _bf(g)
        u_ref[...] = _bf(u)
        a_ref[...] = _bf(g * _sigmoid(g) * u)

    blk = pl.BlockSpec((tm, tn), lambda j, i: (i, j))
    sh = jax.ShapeDtypeStruct((S, Fd), BF16)
    return _call(
        body, grid=(Fd // tn, S // tm),
        in_specs=[pl.BlockSpec((tm, D), lambda j, i: (i, 0)), _wspec(wg, (D, tn), lambda j, i: (0, j)),
                  _wspec(wu, (D, tn), lambda j, i: (0, j))],
        out_specs=[blk, blk, blk], out_shape=[sh, sh, sh], ins=[h, wg[0], wu[0]],
        semantics=("parallel", "parallel"), name=name, hosted=hosted)


def ffn_down_bwd(dxo, wd, g, u, name):
    S, D = dxo.shape
    Fd = wd[0].shape[-2]
    tm = _tile(S, ROW_TILE, 8)
    tn = _tile(Fd, COL_TILE)

    def body(dx_ref, wd_ref, g_ref, u_ref, dg_ref, du_ref):
        da = _dot_nt(_bf(dx_ref[...]), wd_ref[...]) * 0.5
        g = g_ref[...].astype(F32)
        u = u_ref[...].astype(F32)
        sg = _sigmoid(g)
        dg_ref[...] = _bf(da * u * (sg * (1.0 + g * (1.0 - sg))))
        du_ref[...] = _bf(da * (g * sg))

    blk = pl.BlockSpec((tm, tn), lambda j, i: (i, j))
    sh = jax.ShapeDtypeStruct((S, Fd), BF16)
    return pl.pallas_call(
        body, grid=(Fd // tn, S // tm),
        in_specs=[pl.BlockSpec((tm, D), lambda j, i: (i, 0)), _wspec(wd, (tn, D), lambda j, i: (j, 0)),
                  blk, blk],
        out_specs=[blk, blk], out_shape=[sh, sh],
        compiler_params=_cparams(("parallel", "parallel")), name=name)(dxo, wd[0], g, u)


_GELU_C = math.sqrt(2.0 / math.pi)


def _gelu(p):
    return 0.5 * p * (1.0 + jnp.tanh(_GELU_C * (p + 0.044715 * (p * p * p))))


def _gelu_grad(p):
    th = jnp.tanh(_GELU_C * (p + 0.044715 * (p * p * p)))
    return 0.5 * (1.0 + th) + 0.5 * p * (1.0 - th * th) * (_GELU_C * (1.0 + 3.0 * 0.044715 * (p * p)))


def _chunk_mask():
    t = lax.broadcasted_iota(jnp.int32, (GM_CHUNK, GM_CHUNK), 0)
    s = lax.broadcasted_iota(jnp.int32, (GM_CHUNK, GM_CHUNK), 1)
    return (s // CHUNK) <= (t // CHUNK)


def gmlp_fwd(proj, v_gain, w_s, b_mat, name):
    S = proj.shape[0]
    tw = _tile(S, ROW_TILE, GM_CHUNK)
    n_win = tw // GM_CHUNK

    def body(p_ref, gain_ref, w_ref, b_ref, y_ref):
        mask = _chunk_mask()
        v = _gelu(p_ref[:, GM_W:])
        r = lax.rsqrt(jnp.mean(v * v, axis=-1, keepdims=True) + EPS)
        vn = _bf(v * r * gain_ref[...])
        for g in range(GM_GROUPS):
            wm = _bf(jnp.where(mask, w_ref[g], 0.0))
            cs = slice(g * GM_CHUNK, (g + 1) * GM_CHUNK)
            for w in range(n_win):
                rs = slice(w * GM_CHUNK, (w + 1) * GM_CHUNK)
                mixed = _dot(wm, vn[rs, cs]) + b_ref[g]
                y_ref[rs, cs] = _bf(_gelu(p_ref[rs, cs]) * mixed)

    return pl.pallas_call(
        body, grid=(S // tw,),
        in_specs=[pl.BlockSpec((tw, 2 * GM_W), lambda i: (i, 0)), pl.BlockSpec((1, GM_W), lambda i: (0, 0)),
                  pl.BlockSpec((GM_GROUPS, GM_CHUNK, GM_CHUNK), lambda i: (0, 0, 0)),
                  pl.BlockSpec((GM_GROUPS, GM_CHUNK, GM_CHUNK), lambda i: (0, 0, 0))],
        out_specs=pl.BlockSpec((tw, GM_W), lambda i: (i, 0)),
        out_shape=jax.ShapeDtypeStruct((S, GM_W), BF16),
        compiler_params=_cparams(("parallel",)), name=name)(proj, v_gain.reshape(1, GM_W), w_s, b_mat)


def gmlp_bwd(proj, dy, v_gain, w_s, b_mat, name):
    S = proj.shape[0]
    tw = _tile(S, ROW_TILE, GM_CHUNK)
    n_win = tw // GM_CHUNK

    def body(p_ref, dy_ref, gain_ref, w_ref, b_ref, dp_ref, dw_ref, db_ref, dgain_ref, dvn_ref):
        step = pl.program_id(0)

        @pl.when(step == 0)
        def _():
            dw_ref[...] = jnp.zeros_like(dw_ref)
            db_ref[...] = jnp.zeros_like(db_ref)
            dgain_ref[...] = jnp.zeros_like(dgain_ref)

        mask = _chunk_mask()
        pv = p_ref[:, GM_W:]
        v = _gelu(pv)
        r = lax.rsqrt(jnp.mean(v * v, axis=-1, keepdims=True) + EPS)
        vhat = v * r
        gain = gain_ref[...]
        vn = _bf(vhat * gain)
        for g in range(GM_GROUPS):
            wm = _bf(jnp.where(mask, w_ref[g], 0.0))
            cs = slice(g * GM_CHUNK, (g + 1) * GM_CHUNK)
            dw_acc = jnp.zeros((GM_CHUNK, GM_CHUNK), F32)
            db_acc = jnp.zeros((GM_CHUNK, GM_CHUNK), F32)
            for w in range(n_win):
                rs = slice(w * GM_CHUNK, (w + 1) * GM_CHUNK)
                pu = p_ref[rs, cs]
                u = _gelu(pu)
                vn_blk = vn[rs, cs]
                mixed = _dot(wm, vn_blk) + b_ref[g]
                dyb = dy_ref[rs, cs]
                dp_ref[rs, cs] = _bf(dyb * mixed * _gelu_grad(pu))
                dmix = dyb * u
                db_acc = db_acc + dmix
                dmix_b = _bf(dmix)
                dw_acc = dw_acc + _dot_nt(dmix_b, vn_blk)
                dvn_ref[rs, cs] = _dot_tn(wm, dmix_b)
            dw_ref[g] += jnp.where(mask, dw_acc, 0.0)
            db_ref[g] += jnp.broadcast_to(jnp.sum(db_acc, axis=-1, keepdims=True), (GM_CHUNK, GM_CHUNK))
        dvn = dvn_ref[...]
        dgain_ref[...] += jnp.sum(dvn * vhat, axis=0, keepdims=True)
        dvhat = dvn * gain
        m = jnp.mean(dvhat * vhat, axis=-1, keepdims=True)
        dv = r * (dvhat - vhat * m)
        dp_ref[:, GM_W:] = _bf(dv * _gelu_grad(pv))

    sq = pl.BlockSpec((GM_GROUPS, GM_CHUNK, GM_CHUNK), lambda i: (0, 0, 0))
    sq_sh = jax.ShapeDtypeStruct((GM_GROUPS, GM_CHUNK, GM_CHUNK), F32)
    return pl.pallas_call(
        body, grid=(S // tw,),
        in_specs=[pl.BlockSpec((tw, 2 * GM_W), lambda i: (i, 0)), pl.BlockSpec((tw, GM_W), lambda i: (i, 0)),
                  pl.BlockSpec((1, GM_W), lambda i: (0, 0)), sq, sq],
        out_specs=[pl.BlockSpec((tw, 2 * GM_W), lambda i: (i, 0)), sq, sq,
                   pl.BlockSpec((1, GM_W), lambda i: (0, 0))],
        out_shape=[jax.ShapeDtypeStruct((S, 2 * GM_W), BF16), sq_sh, sq_sh,
                   jax.ShapeDtypeStruct((1, GM_W), F32)],
        scratch_shapes=[pltpu.VMEM((tw, GM_W), F32)],
        compiler_params=_cparams(("arbitrary",)), name=name)(proj, dy, v_gain.reshape(1, GM_W), w_s, b_mat)


def _head_mask(h, width):
    lane = lax.broadcasted_iota(jnp.int32, (1, width), 1)
    return (lane >= HEAD_DIM * h) & (lane < HEAD_DIM * (h + 1))


def _mem_probs(q, k, h):
    kh = jnp.where(_head_mask(h, MEM_W), k, jnp.zeros_like(k))
    s = _dot_nt(q, kh) * (HEAD_DIM ** -0.5)
    s = s - jnp.max(s, axis=-1, keepdims=True)
    e = jnp.exp(s)
    return e * (1.0 / jnp.sum(e, axis=-1, keepdims=True)), kh


def memattn_fwd(proj, q_cb, mem_kv, name):
    S = proj.shape[0]
    NM = mem_kv.shape[0]
    tm = _tile(S, ROW_TILE, 8)

    def body(q_ref, kv_ref, o_ref):
        q = _bf(q_ref[...])
        k = _bf(kv_ref[:, :MEM_W])
        v = _bf(kv_ref[:, MEM_W:])
        acc = jnp.zeros((tm, MEM_W), F32)
        for h in range(MEM_HEADS):
            p, _ = _mem_probs(q, k, h)
            vh = jnp.where(_head_mask(h, MEM_W), v, jnp.zeros_like(v))
            acc = acc + _dot(_bf(p), vh)
        o_ref[...] = _bf(acc)

    return pl.pallas_call(
        body, grid=(S // tm,),
        in_specs=[pl.BlockSpec((tm, MEM_W), lambda i: (i, q_cb)), pl.BlockSpec((NM, 2 * MEM_W), lambda i: (0, 0))],
        out_specs=pl.BlockSpec((tm, MEM_W), lambda i: (i, 0)),
        out_shape=jax.ShapeDtypeStruct((S, MEM_W), BF16),
        compiler_params=_cparams(("parallel",)), name=name)(proj, mem_kv)


def memattn_bwd(proj, q_cb, mem_kv, dy, dy_cb, name):
    S = proj.shape[0]
    NM = mem_kv.shape[0]
    tm = _tile(S, ROW_TILE, 8)

    def body(q_ref, kv_ref, do_ref, dq_ref, dkv_ref):
        @pl.when(pl.program_id(0) == 0)
        def _():
            dkv_ref[...] = jnp.zeros_like(dkv_ref)

        q = _bf(q_ref[...])
        k = _bf(kv_ref[:, :MEM_W])
        v = _bf(kv_ref[:, MEM_W:])
        do = _bf(do_ref[...])
        dq = jnp.zeros((tm, MEM_W), F32)
        dk = jnp.zeros((NM, MEM_W), F32)
        dv = jnp.zeros((NM, MEM_W), F32)
        for h in range(MEM_HEADS):
            hm = _head_mask(h, MEM_W)
            p, kh = _mem_probs(q, k, h)
            vh = jnp.where(hm, v, jnp.zeros_like(v))
            dp = _dot_nt(do, vh)
            ds = _bf(p * (dp - jnp.sum(p * dp, axis=-1, keepdims=True)) * (HEAD_DIM ** -0.5))
            dq = dq + _dot(ds, kh)
            dk = dk + jnp.where(hm, _dot_tn(ds, q), 0.0)
            dv = dv + jnp.where(hm, _dot_tn(_bf(p), do), 0.0)
        dq_ref[...] = _bf(dq)
        dkv_ref[:, :MEM_W] += dk
        dkv_ref[:, MEM_W:] += dv

    return pl.pallas_call(
        body, grid=(S // tm,),
        in_specs=[pl.BlockSpec((tm, MEM_W), lambda i: (i, q_cb)), pl.BlockSpec((NM, 2 * MEM_W), lambda i: (0, 0)),
                  pl.BlockSpec((tm, MEM_W), lambda i: (i, dy_cb))],
        out_specs=[pl.BlockSpec((tm, MEM_W), lambda i: (i, 0)), pl.BlockSpec((NM, 2 * MEM_W), lambda i: (0, 0))],
        out_shape=[jax.ShapeDtypeStruct((S, MEM_W), BF16), jax.ShapeDtypeStruct((NM, 2 * MEM_W), F32)],
        compiler_params=_cparams(("arbitrary",)), name=name)(proj, mem_kv, dy)


def _split_bf(v):
    hi = _bf(v)
    return hi, _bf(v - hi.astype(F32))


def _sb_scores(qp, k2, dpos, row, col, tri):
    z = _dot_nt(qp, k2) * (HEAD_DIM ** -0.5)
    t = jnp.log(1.0 + jnp.exp(-jnp.abs(z)))
    lb = jnp.minimum(z, 0.0) - t
    lom = jnp.where(_sb_mask(dpos, row, col), -jnp.maximum(z, 0.0) - t, 0.0)
    l_hi, l_lo = _split_bf(lom)
    insuf = _dot(l_hi, tri) + _dot(l_lo, tri)
    return lb + insuf, jnp.sum(lom, axis=1, keepdims=True), lb


def _sb_mask(dpos, row, col):
    return (col - row) < dpos


def _sb_weights(s, run, dpos, row, col):
    return jnp.where(_sb_mask(dpos, row, col), jnp.exp(s + run), 0.0)


def _sb_alive(runs):
    top = runs[0]
    for r in runs[1:]:
        top = jnp.maximum(top, r)
    return jnp.max(top) > SB_DEAD


def _sb_consts():
    row = lax.broadcasted_iota(jnp.int32, (2 * SB_BLOCK, SB_BLOCK), 0) % SB_BLOCK
    col = lax.broadcasted_iota(jnp.int32, (2 * SB_BLOCK, SB_BLOCK), 1)
    r = lax.broadcasted_iota(jnp.int32, (SB_BLOCK, SB_BLOCK), 0)
    c = lax.broadcasted_iota(jnp.int32, (SB_BLOCK, SB_BLOCK), 1)
    return row, col, _bf(jnp.where(r > c, 1.0, 0.0)), _bf(jnp.where(r >= c, 1.0, 0.0))


def _stack_heads(x2, masks):
    return jnp.concatenate([_bf(jnp.where(hm, x2, jnp.zeros_like(x2))) for hm in masks], axis=0)


def _side_by_side(x):
    return jnp.concatenate([x[:SB_BLOCK], x[SB_BLOCK:]], axis=1)


def sb_fwd(proj, kv, name):
    S = proj.shape[0]
    n_g = GM_W // SB_COLS
    tq = _tile(S, SB_Q_CHUNK, SB_BLOCK)
    nqb = tq // SB_BLOCK

    def body(q_ref, k_ref, v_ref, o_ref, s_ref):
        chunk = pl.program_id(1)
        row, col, tri, _ = _sb_consts()
        masks = [_head_mask(h, LANES) for h in range(2)]
        pairs = [slice(p * LANES, (p + 1) * LANES) for p in range(SB_PAIRS)]

        def q_loop(qi, _):
            i = chunk * nqb + qi
            rows = pl.ds(pl.multiple_of(qi * SB_BLOCK, SB_BLOCK), SB_BLOCK)
            qps = [_stack_heads(q_ref[rows, cs], masks) for cs in pairs]

            def look_ahead(t):
                keys = pl.ds(pl.multiple_of(jnp.maximum(i - t, 0) * SB_BLOCK, SB_BLOCK), SB_BLOCK)
                sums = []
                for p, cs in enumerate(pairs):
                    s, lom_sum, _ = _sb_scores(qps[p], k_ref[keys, cs], t * SB_BLOCK, row, col, tri)
                    s_ref[p] = s
                    sums.append(lom_sum)
                return tuple(sums)

            def k_step(carry):
                t, accs, runs, sums, _ = carry
                keys = pl.ds(pl.multiple_of((i - t) * SB_BLOCK, SB_BLOCK), SB_BLOCK)
                new_accs, new_runs = [], []
                for p, cs in enumerate(pairs):
                    a = _sb_weights(s_ref[p], runs[p], t * SB_BLOCK, row, col)
                    a_hi, a_lo = _split_bf(_side_by_side(a))
                    vp = _stack_heads(v_ref[keys, cs], masks)
                    new_accs.append(accs[p] + _dot(a_hi, vp) + _dot(a_lo, vp))
                    new_runs.append(runs[p] + sums[p])
                return t + 1, tuple(new_accs), tuple(new_runs), look_ahead(t + 1), _sb_alive(new_runs)

            zero = jnp.zeros((2 * SB_BLOCK, 1), F32)
            _, accs, _, _, _ = lax.while_loop(
                lambda carry: jnp.logical_and(carry[0] <= i, carry[4]), k_step,
                (jnp.int32(0), (jnp.zeros((SB_BLOCK, LANES), F32),) * SB_PAIRS, (zero,) * SB_PAIRS,
                 look_ahead(jnp.int32(0)), jnp.bool_(True)))
            for p, cs in enumerate(pairs):
                o_ref[rows, cs] = accs[p]
            return 0

        lax.fori_loop(0, nqb, q_loop, 0)

    return pl.pallas_call(
        body, grid=(n_g, S // tq),
        in_specs=[pl.BlockSpec((tq, SB_COLS), lambda g, c: (c, g)),
                  pl.BlockSpec((S, SB_COLS), lambda g, c: (0, g)),
                  pl.BlockSpec((S, SB_COLS), lambda g, c: (0, n_g + g))],
        out_specs=pl.BlockSpec((tq, SB_COLS), lambda g, c: (c, g)),
        out_shape=jax.ShapeDtypeStruct((S, GM_W), F32),
        scratch_shapes=[pltpu.VMEM((SB_PAIRS, 2 * SB_BLOCK, LANES), F32)],
        compiler_params=_cparams(("parallel", "parallel")), name=name)(proj, kv, kv)


def sb_bwd(proj, kv, dy, out, dk_init, dv_init, name):
    S = proj.shape[0]
    n_g = GM_W // SB_COLS
    tq = _tile(S, SB_Q_CHUNK, SB_BLOCK)
    nqb = tq // SB_BLOCK
    n_chunks = S // tq
    has_init = dk_init is not None
    scale = HEAD_DIM ** -0.5

    def body(*refs):
        s_ref, beta_ref, da_ref = refs[-3:]
        if has_init:
            q_ref, k_ref, v_ref, do_ref, out_ref, dki_ref, dvi_ref, dq_ref, dko_ref, dvo_ref, dk_acc, dv_acc = refs[:-3]
        else:
            q_ref, k_ref, v_ref, do_ref, out_ref, dq_ref, dko_ref, dvo_ref, dk_acc, dv_acc = refs[:-3]
        chunk = pl.program_id(1)
        cols = pl.ds(pl.multiple_of(pl.program_id(0) * SB_COLS, SB_COLS), SB_COLS)

        @pl.when(chunk == 0)
        def _():
            if has_init:
                pltpu.sync_copy(dki_ref.at[:, cols], dk_acc)
                pltpu.sync_copy(dvi_ref.at[:, cols], dv_acc)
            else:
                dk_acc[...] = jnp.zeros_like(dk_acc)
                dv_acc[...] = jnp.zeros_like(dv_acc)

        row, col, tri, tri_inc = _sb_consts()
        masks = [_head_mask(h, LANES) for h in range(2)]
        pairs = [slice(p * LANES, (p + 1) * LANES) for p in range(SB_PAIRS)]

        def q_loop(qi, _):
            i = chunk * nqb + qi
            rows = pl.ds(pl.multiple_of(qi * SB_BLOCK, SB_BLOCK), SB_BLOCK)
            qps = [_stack_heads(q_ref[rows, cs], masks) for cs in pairs]
            dops = [_stack_heads(do_ref[rows, cs], masks) for cs in pairs]
            e_tots = [jnp.sum(dop.astype(F32) * jnp.concatenate([out_ref[rows, cs]] * 2, axis=0), axis=1,
                              keepdims=True) for dop, cs in zip(dops, pairs)]

            def look_ahead(t):
                keys = pl.ds(pl.multiple_of(jnp.maximum(i - t, 0) * SB_BLOCK, SB_BLOCK), SB_BLOCK)
                sums = []
                for p, cs in enumerate(pairs):
                    s, lom_sum, lb = _sb_scores(qps[p], k_ref[keys, cs], t * SB_BLOCK, row, col, tri)
                    s_ref[p] = s
                    beta_ref[p] = jnp.exp(lb)
                    da_ref[p] = _dot_nt(dops[p], v_ref[keys, cs])
                    sums.append(lom_sum)
                return tuple(sums)

            def k_step(carry):
                t, dqs, runs, e_runs, sums, _ = carry
                keys = pl.ds(pl.multiple_of((i - t) * SB_BLOCK, SB_BLOCK), SB_BLOCK)
                new_dqs, new_runs, new_e_runs = [], [], []
                for p, cs in enumerate(pairs):
                    a = _sb_weights(s_ref[p], runs[p], t * SB_BLOCK, row, col)
                    e = a * da_ref[p]
                    e_hi, e_lo = _split_bf(e)
                    before = e_tots[p] - e_runs[p] - (_dot(e_hi, tri_inc) + _dot(e_lo, tri_inc))
                    beta = beta_ref[p]
                    dz = jnp.where(_sb_mask(t * SB_BLOCK, row, col), e * (1.0 - beta) - before * beta, 0.0)
                    dz = _bf(dz * scale)
                    new_dqs.append(dqs[p] + _dot(_side_by_side(dz), _stack_heads(k_ref[keys, cs], masks)))
                    dk_acc[keys, cs] += _dot_tn(dz, qps[p])
                    dv_acc[keys, cs] += _dot_tn(_bf(a), dops[p])
                    new_runs.append(runs[p] + sums[p])
                    new_e_runs.append(e_runs[p] + jnp.sum(e, axis=1, keepdims=True))
                return (t + 1, tuple(new_dqs), tuple(new_runs), tuple(new_e_runs), look_ahead(t + 1),
                        _sb_alive(new_runs))

            zero = jnp.zeros((2 * SB_BLOCK, 1), F32)
            zeros = (zero,) * SB_PAIRS
            _, dqs, _, _, _, _ = lax.while_loop(
                lambda carry: jnp.logical_and(carry[0] <= i, carry[5]), k_step,
                (jnp.int32(0), (jnp.zeros((SB_BLOCK, LANES), F32),) * SB_PAIRS, zeros, zeros,
                 look_ahead(jnp.int32(0)), jnp.bool_(True)))
            for p, cs in enumerate(pairs):
                dq_ref[rows, cs] = dqs[p]
            return 0

        lax.fori_loop(0, nqb, q_loop, 0)

        @pl.when(chunk == n_chunks - 1)
        def _():
            pltpu.sync_copy(dk_acc, dko_ref.at[:, cols])
            pltpu.sync_copy(dv_acc, dvo_ref.at[:, cols])

    qspec = pl.BlockSpec((tq, SB_COLS), lambda g, c: (c, g))
    kspec = pl.BlockSpec((S, SB_COLS), lambda g, c: (0, g))
    in_specs = [qspec, kspec, pl.BlockSpec((S, SB_COLS), lambda g, c: (0, n_g + g)), qspec, qspec]
    ins = [proj, kv, kv, dy, out]
    if has_init:
        in_specs += [_ANY, _ANY]
        ins += [dk_init, dv_init]
    sh = jax.ShapeDtypeStruct((S, GM_W), F32)
    return pl.pallas_call(
        body, grid=(n_g, n_chunks), in_specs=in_specs, out_specs=[qspec, _ANY, _ANY],
        out_shape=[sh, sh, sh],
        scratch_shapes=[pltpu.VMEM((S, SB_COLS), F32), pltpu.VMEM((S, SB_COLS), F32)]
        + [pltpu.VMEM((SB_PAIRS, 2 * SB_BLOCK, LANES), F32)] * 3,
        compiler_params=_cparams(("arbitrary", "arbitrary")), name=name)(*ins)


def adamw(w, g, m, v, name):
    shape = w.shape
    C = shape[-1] if w.ndim > 1 else shape[0]
    R = w.size // C
    tr = _tile(R, ROW_TILE, 8)

    def body(w_ref, g_ref, m_ref, v_ref, d_ref, nm_ref, nv_ref):
        gv = g_ref[...]
        m2 = ADAM_B1 * m_ref[...] + (1.0 - ADAM_B1) * gv
        v2 = ADAM_B2 * v_ref[...] + (1.0 - ADAM_B2) * (gv * gv)
        m_hat = m2 / (1.0 - ADAM_B1 ** ADAM_STEP)
        v_hat = v2 / (1.0 - ADAM_B2 ** ADAM_STEP)
        d_ref[...] = -ADAM_LR * (m_hat / (jnp.sqrt(v_hat) + ADAM_EPS) + ADAM_WD * w_ref[...])
        nm_ref[...] = m2
        nv_ref[...] = v2

    blk = pl.BlockSpec((tr, C), lambda i: (i, 0))
    sh = jax.ShapeDtypeStruct((R, C), F32)
    outs = pl.pallas_call(
        body, grid=(R // tr,), in_specs=[blk] * 4, out_specs=[blk] * 3, out_shape=[sh] * 3,
        compiler_params=_cparams(("parallel",)), name=name)(
            w.reshape(R, C), g.reshape(R, C), m.reshape(R, C), v.reshape(R, C))
    return tuple(o.reshape(shape) for o in outs)


def _place():
    return lax.axis_index("x"), lax.axis_index("y"), lax.axis_index("c")


def _other_chips(x, y):
    return [(1 - x, y), (x, 1 - y), (1 - x, 1 - y)]


_ANY = pl.BlockSpec(memory_space=pl.ANY)
LOCAL_CHUNKS = 8
SEG_LOCAL_CHUNKS = 4


def _local_copies(src_of, dst_of, n_rows, sems, sem_base):
    rows = n_rows // LOCAL_CHUNKS
    copies = []
    for k in range(LOCAL_CHUNKS):
        r = pl.ds(k * rows, rows)
        cp = pltpu.make_async_copy(src_of(r), dst_of(r), sems.at[sem_base + k])
        cp.start()
        copies.append(cp)
    return copies


def _remote(src, dst, send_sems, recv_sems, k, to):
    return pltpu.make_async_remote_copy(src_ref=src, dst_ref=dst, send_sem=send_sems.at[k], recv_sem=recv_sems.at[k],
                                        device_id=to, device_id_type=MESH)


def _gather_stage(segs, n_gath, stage):
    def make(ins, outs, send_sems, recv_sems, *local_sems):
        x, y, c = _place()
        sibling = (x, y, 1 - c)
        chips = _other_chips(x, y)
        copies, waits = [], []
        for s, (b, r0, n) in enumerate(segs):
            half = n // 2

            def piece(px, py, pc, b=b, r0=r0, half=half):
                return outs[b].at[2 * px + py, pl.ds(r0 + pc * half, half), :]

            for j, chip in enumerate(chips):
                if stage == 0:
                    mine = ins[n_gath + b].at[pl.ds(r0 + c * half, half), :]
                    cp = _remote(mine, piece(x, y, c), send_sems, recv_sems, 3 * s + j, (*chip, c))
                    landed, sender = piece(*chip, c), (*chip, c)
                else:
                    cp = _remote(piece(*chip, c), piece(*chip, c), send_sems, recv_sems, 3 * s + j, sibling)
                    landed, sender = piece(*chip, 1 - c), sibling
                copies.append(cp)
                waits.append(cp.wait_send)
                waits.append(_remote(landed, landed, send_sems, recv_sems, 3 * s + j, sender).wait_recv)
            if stage == 0:
                rows = n // SEG_LOCAL_CHUNKS
                for q in range(SEG_LOCAL_CHUNKS):
                    r = pl.ds(r0 + q * rows, rows)
                    cp = pltpu.make_async_copy(ins[n_gath + b].at[r, :], outs[b].at[2 * x + y, r, :],
                                               local_sems[0].at[SEG_LOCAL_CHUNKS * s + q])
                    copies.append(cp)
                    waits.append(cp.wait)
        return copies, waits

    return make


def _gather_sems(segs, stage):
    return [3 * len(segs), 3 * len(segs)] + ([SEG_LOCAL_CHUNKS * len(segs)] if stage == 0 else [])


def all_gather_segments(bufs, segs, name):
    nb = len(bufs)
    n0, n1 = len(_gather_sems(segs, 0)), len(_gather_sems(segs, 1))

    def body(*refs):
        in_refs, out_refs = refs[:nb], refs[nb:2 * nb]
        sems = refs[2 * nb:]
        for stage, stage_sems in ((0, sems[:n0]), (1, sems[n0:n0 + n1])):
            copies, waits = _gather_stage(segs, nb, stage)(list(out_refs) + list(in_refs), out_refs, *stage_sems)
            for cp in copies:
                cp.start()
            for wait in waits:
                wait()

    return pl.pallas_call(
        body, in_specs=[_ANY] * nb, out_specs=[_ANY] * nb,
        out_shape=[jax.ShapeDtypeStruct((N_CHIPS,) + b.shape, b.dtype) for b in bufs],
        scratch_shapes=[pltpu.SemaphoreType.DMA((n,)) for n in _gather_sems(segs, 0) + _gather_sems(segs, 1)],
        name=name)(*bufs)


def hosted_gather(gathered, bufs, segs, stage):
    arrays = list(gathered) + (list(bufs) if stage == 0 else [])
    return Hosted(arrays, len(gathered), _gather_sems(segs, stage), _gather_stage(segs, len(gathered), stage))


def swap_halves(gs, name):
    nb = len(gs)
    n = N_CHIPS

    def body(*refs):
        g_refs, t_refs = refs[:nb], refs[nb:2 * nb]
        send_sems, recv_sems = refs[2 * nb:]
        x, y, c = _place()
        gives = []
        for b in range(nb):
            Ph = gs[b].shape[1] // 2
            for k in range(n):
                gives.append(_remote(g_refs[b].at[k, pl.ds((1 - c) * Ph, Ph), :], t_refs[b].at[k], send_sems,
                                     recv_sems, n * b + k, (x, y, 1 - c)))
        for cp in gives:
            cp.start()
        for cp in gives:
            cp.wait()

    return pl.pallas_call(
        body, in_specs=[_ANY] * nb, out_specs=[_ANY] * nb,
        out_shape=[jax.ShapeDtypeStruct((n, g.shape[1] // 2, g.shape[2]), g.dtype) for g in gs],
        scratch_shapes=[pltpu.SemaphoreType.DMA((n * nb,)), pltpu.SemaphoreType.DMA((n * nb,))],
        name=name)(*gs)


def _core_index():
    return lax.axis_index("c").astype(jnp.int32).reshape(1)


def add_cores(g, theirs, name):
    n, Ph, Wd = theirs.shape
    tr = _tile(Ph, ROW_TILE, 16)
    steps = Ph // tr

    def body(c_ref, g_ref, t_ref, o_ref):
        o_ref[...] = _bf(g_ref[...] + t_ref[...])

    blk = pl.BlockSpec((None, tr, Wd), lambda k, i, c_ref: (k, i, 0))
    return pl.pallas_call(
        body,
        grid_spec=pltpu.PrefetchScalarGridSpec(
            num_scalar_prefetch=1, grid=(n, steps),
            in_specs=[pl.BlockSpec((None, tr, Wd), lambda k, i, c_ref: (k, c_ref[0] * steps + i, 0)), blk],
            out_specs=blk),
        out_shape=jax.ShapeDtypeStruct((n, Ph, Wd), BF16),
        compiler_params=_cparams(("parallel", "parallel")), name=name)(_core_index(), g, theirs)


def add_chips(slots, name):
    n, Ph, Wd = slots.shape
    tr = _tile(Ph, ROW_TILE, 16)
    steps = Ph // tr

    def body(c_ref, *refs):
        acc = refs[0][...].astype(F32)
        for k in range(1, n):
            acc = acc + refs[k][...].astype(F32)
        refs[n][...] = acc

    in_specs = [pl.BlockSpec((None, tr, Wd), functools.partial(lambda i, c_ref, k: (k, i, 0), k=k)) for k in range(n)]
    return pl.pallas_call(
        body,
        grid_spec=pltpu.PrefetchScalarGridSpec(
            num_scalar_prefetch=1, grid=(steps,), in_specs=in_specs,
            out_specs=pl.BlockSpec((tr, Wd), lambda i, c_ref: (c_ref[0] * steps + i, 0))),
        out_shape=jax.ShapeDtypeStruct((2 * Ph, Wd), F32),
        compiler_params=_cparams(("parallel",)), name=name)(_core_index(), *([slots] * n))


def scatter_to_chips(hs, name):
    nb = len(hs)

    def body(*refs):
        h_refs, out_refs = refs[:nb], refs[nb:2 * nb]
        send_sems, recv_sems, local_sems = refs[2 * nb:]
        x, y, c = _place()
        me = 2 * x + y
        chips = _other_chips(x, y)
        keep, sends = [], []
        for b in range(nb):
            keep += _local_copies(lambda r, b=b: h_refs[b].at[me, r, :], lambda r, b=b: out_refs[b].at[me, r, :],
                                  hs[b].shape[1], local_sems, b * LOCAL_CHUNKS)
            for j, (cx, cy) in enumerate(chips):
                cp = _remote(h_refs[b].at[2 * cx + cy], out_refs[b].at[me], send_sems, recv_sems, 3 * b + j,
                             (cx, cy, c))
                cp.start()
                sends.append(cp)
        for b in range(nb):
            for j, (cx, cy) in enumerate(chips):
                _remote(h_refs[b].at[me], out_refs[b].at[2 * cx + cy], send_sems, recv_sems, 3 * b + j,
                        (cx, cy, c)).wait_recv()
        for cp in sends:
            cp.wait_send()
        for cp in keep:
            cp.wait()

    return pl.pallas_call(
        body, in_specs=[_ANY] * nb, out_specs=[_ANY] * nb,
        out_shape=[jax.ShapeDtypeStruct(h.shape, h.dtype) for h in hs],
        scratch_shapes=[pltpu.SemaphoreType.DMA((3 * nb,)), pltpu.SemaphoreType.DMA((3 * nb,)),
                        pltpu.SemaphoreType.DMA((LOCAL_CHUNKS * nb,))],
        name=name)(*hs)


JOIN_CHUNKS = 4


def join_halves(ts, name):
    nb = len(ts)
    n = JOIN_CHUNKS

    def body(*refs):
        out_refs = refs[nb:2 * nb]
        send_sems, recv_sems = refs[2 * nb:]
        x, y, c = _place()
        gives = []
        for b in range(nb):
            Ph = ts[b].shape[0] // 2
            rows = Ph // n
            for k in range(n):
                part = out_refs[b].at[pl.ds(c * Ph + k * rows, rows), :]
                gives.append(_remote(part, part, send_sems, recv_sems, n * b + k, (x, y, 1 - c)))
        for cp in gives:
            cp.start()
        for b in range(nb):
            Ph = ts[b].shape[0] // 2
            rows = Ph // n
            for k in range(n):
                landed = out_refs[b].at[pl.ds((1 - c) * Ph + k * rows, rows), :]
                _remote(landed, landed, send_sems, recv_sems, n * b + k, (x, y, 1 - c)).wait_recv()
        for cp in gives:
            cp.wait_send()

    return pl.pallas_call(
        body, in_specs=[_ANY] * nb, out_specs=[_ANY] * nb,
        out_shape=[jax.ShapeDtypeStruct(t.shape, t.dtype) for t in ts],
        scratch_shapes=[pltpu.SemaphoreType.DMA((n * nb,)), pltpu.SemaphoreType.DMA((n * nb,))],
        input_output_aliases={b: b for b in range(nb)}, name=name)(*ts)


WIDE = ['ffn1_w_gate', 'ffn1_w_up', 'ffn2_w_gate', 'ffn2_w_up']


def _rows_of(p):
    if p.shape[-1] == PACK_W and (p.size // PACK_W) % PART_ROWS == 0:
        return p.reshape(-1, PACK_W)
    flat = p.reshape(-1)
    rows = -(-flat.shape[0] // (PACK_W * PART_ROWS)) * PART_ROWS
    return jnp.pad(flat, (0, rows * PACK_W - flat.shape[0])).reshape(rows, PACK_W)


def _pack(parts):
    buf = jnp.concatenate([_rows_of(p) for p in parts], axis=0)
    rows = buf.shape[0]
    total = -(-rows // PACK_ROWS) * PACK_ROWS
    return jnp.pad(buf, ((0, total - rows), (0, 0)))


def _unpack(buf, shapes):
    outs, r = [], 0
    lead = buf.shape[:-2]
    for shp in shapes:
        n = math.prod(shp)
        rows = -(-n // (PACK_W * PART_ROWS)) * PART_ROWS
        blk = buf[..., r:r + rows, :]
        if n != rows * PACK_W:
            blk = blk.reshape(lead + (rows * PACK_W,))[..., :n]
        outs.append(blk.reshape(lead + tuple(shp)))
        r += rows
    return outs


class WeightStream:
    def __init__(self, shards):
        depth = shards['ffn1_w_gate'].shape[0]
        order = [(which, l) for l in range(depth) for which in ('ffn1', 'ffn2')]
        self.count = len(order)
        gate_up = [shards[f'{which}_w_{part}'][l].astype(BF16) for which, l in order for part in ('gate', 'up')]
        downs = [shards[f'{which}_w_down'][l].astype(BF16) for which, l in order]
        self.d, self.cols = gate_up[0].shape
        assert downs[0].shape == (self.cols, PACK_W)
        self.rest_names = [n for n in W_NAMES if n in SHARD_AXIS and n not in WIDE and not n.endswith('_w_down')]
        rest = [lax.bitcast_convert_type(shards[n], BF16) if n == 'a_v_norm' else shards[n].astype(BF16)
                for n in self.rest_names]
        self.rest_shapes = [p.shape for p in rest]
        self.bufs = [jnp.concatenate(gate_up, axis=0), _pack(downs + rest)]
        self.rest_row0 = self.count * self.cols
        first = self.segments(0) + [(1, self.rest_row0, self.bufs[1].shape[0] - self.rest_row0)]
        self.gathered = all_gather_segments(self.bufs, first, "all_gather_first")

    def segments(self, i):
        return [(0, 2 * self.d * i, 2 * self.d), (1, self.cols * i, self.cols)]

    def rest(self):
        out = {}
        for n, g in zip(self.rest_names, _unpack(self.gathered[1][:, self.rest_row0:], self.rest_shapes)):
            if n == 'a_v_norm':
                g = lax.bitcast_convert_type(g, F32)
            out[n] = jnp.concatenate([g[k] for k in range(N_CHIPS)], axis=SHARD_AXIS[n])
        return out

    def ffn(self, i):
        wide, narrow = self.gathered
        r = 2 * self.d * i
        gate = jnp.concatenate([wide[k, r:r + self.d] for k in range(N_CHIPS)], axis=1)
        up = jnp.concatenate([wide[k, r + self.d:r + 2 * self.d] for k in range(N_CHIPS)], axis=1)
        down = narrow[:, self.cols * i:self.cols * (i + 1)].reshape(N_CHIPS * self.cols, PACK_W)
        return (gate, None), (up, None), (down, None)

    def hosted(self, i, stage):
        if i >= self.count:
            return None
        return hosted_gather(self.gathered, self.bufs, self.segments(i), stage)

    def absorb(self, gathered):
        self.gathered = list(gathered)


def reduce_grads(grads, shard_shapes):
    narrow = [n for n in W_NAMES if n in SHARD_AXIS and n not in WIDE]
    repl = [n for n in W_NAMES if n not in SHARD_AXIS]
    full = {n: (jnp.stack(g, axis=0) if isinstance(g, list) else g) for n, g in grads.items() if n not in WIDE}
    cols = shard_shapes[WIDE[0]][-1]
    g_wide = jnp.stack([jnp.concatenate([g[:, k * cols:(k + 1) * cols] for n in WIDE for g in grads[n]], axis=0)
                        for k in range(N_CHIPS)], axis=0)
    blocks = []
    for k in range(N_CHIPS):
        parts = [jnp.split(full[n], N_CHIPS, axis=SHARD_AXIS[n])[k] for n in narrow]
        parts += [full[n] for n in repl]
        blocks.append(_pack(parts))
    gs = [g_wide, jnp.stack(blocks, axis=0)]
    theirs = swap_halves(gs, "grads_swap_halves")
    chip_sums = [add_cores(g, t, f"grads_add_cores_{i}") for i, (g, t) in enumerate(zip(gs, theirs))]
    slots = scatter_to_chips(chip_sums, "grads_scatter")
    totals = join_halves([add_chips(s, f"grads_add_chips_{i}") for i, s in enumerate(slots)], "grads_join_halves")
    out = {}
    rows = math.prod(shard_shapes[WIDE[0]][:-1])
    for i, n in enumerate(WIDE):
        out[n] = totals[0][i * rows:(i + 1) * rows, :].reshape(shard_shapes[n])
    shapes = [shard_shapes[n] for n in narrow] + [full[n].shape for n in repl]
    out.update(zip(narrow + repl, _unpack(totals[1], shapes)))
    return out


def _ffn_fwd(x, gain, stream, i, tag):
    wg, wu, wd = stream.ffn(i)
    h = rms_fwd(x, gain, tag + "_norm")
    ici = stream.hosted(i + 1, 0)
    if ici is None:
        g, u, a = ffn_up(h, wg, wu, tag + "_up")
        xo = mm([(a, 0, wd, a.shape[1], 0)], res=x, scale=0.5, name=tag + "_down")
    else:
        (g, u, a), gathered = ffn_up(h, wg, wu, tag + "_up", hosted=ici)
        stream.absorb(gathered)
        xo, gathered = mm([(a, 0, wd, a.shape[1], 0)], res=x, scale=0.5, name=tag + "_down",
                          hosted=stream.hosted(i + 1, 1))
        stream.absorb(gathered)
    return xo, (x, h, g, u, a, wg, wu, wd)


def _ffn_bwd(dxo, saved, gain, tag):
    x, h, g, u, a, wg, wu, wd = saved
    Fd = a.shape[1]
    dxo, dxo_b = dxo
    dgp, du = ffn_down_bwd(dxo_b, wd, g, u, tag + "_down_bwd")
    d_wd = mm_tn(a, dxo_b, scale=0.5, name=tag + "_dwd")
    d_wg = mm_tn(h, dgp, name=tag + "_dwg")
    d_wu = mm_tn(h, du, name=tag + "_dwu")
    dx, d_gain = mm([(dgp, 0, wg, Fd, 0), (du, 0, wu, Fd, 0)], tb=True, res=dxo, norm_bwd=(x, gain),
                    name=tag + "_up_bwd")
    return dx, d_gain, d_wg, d_wu, d_wd


def local_step(x, mem, W, stream, target):
    D = x.shape[1]
    depth = W['ffn1_norm'].shape[0]
    n_a = W['a_w_in'].shape[0]
    G = {}
    mem_h = rms_fwd(mem, W['mem_norm'], "mem_norm")
    b_mats = jnp.broadcast_to(W['a_b_spatial'][..., None], W['a_b_spatial'].shape + (GM_CHUNK,))
    w_kv = (W['w_kv'], None)

    def mixer_w(l):
        if l < n_a:
            return (W['a_w_in'], l), (W['a_w_out'], l), 2 * GM_W
        return (W['b_w_in'], l - n_a), (W['b_w_out'], l - n_a), GM_W

    saved = []
    kv = kvn = x_kv = None
    for l in range(depth):
        if l == n_a:
            x_kv = x
            kvn = rms_fwd(x, W['kv_norm'], "kv_norm")
            kv = mm([(kvn, 0, w_kv, D, 0)], out_dtype=BF16, name="kv_proj")
        x, s1 = _ffn_fwd(x, W['ffn1_norm'][l], stream, 2 * l, f"l{l}_ffn1")
        hm = rms_fwd(x, W['mix_norm'][l], f"l{l}_mix_norm")
        mem_kv = mm([(mem_h, 0, (W['w_mem_kv'], l), D, 0)], name=f"l{l}_mem_kv")
        w_in, w_out, tok_w = mixer_w(l)
        proj = mm([(hm, 0, w_in, D, 0)], name=f"l{l}_mix_in")
        if l < n_a:
            y_tok = gmlp_fwd(proj, W['a_v_norm'][l], W['a_w_spatial'][l], b_mats[l], f"l{l}_gmlp")
        else:
            y_tok = sb_fwd(proj, kv, f"l{l}_sb")
        q_cb = tok_w // MEM_W
        y_mem = memattn_fwd(proj, q_cb, mem_kv, f"l{l}_memattn")
        x_mid = x
        x = mm([(y_tok, 0, w_out, GM_W, 0), (y_mem, 0, w_out, MEM_W, GM_W // MEM_W)], res=x, name=f"l{l}_mix_out")
        sm = (x_mid, hm, mem_kv, proj, y_tok, y_mem, q_cb)
        x, s2 = _ffn_fwd(x, W['ffn2_norm'][l], stream, 2 * l + 1, f"l{l}_ffn2")
        saved.append((s1, sm, s2))

    loss, dx, d_final = final_loss(x, W['final_norm'], target, "loss_head")
    G['final_norm'] = d_final.reshape(-1)

    per_layer = {n: [None] * depth for n in ['ffn1_norm', 'ffn1_w_gate', 'ffn1_w_up', 'ffn1_w_down', 'mix_norm',
                                             'ffn2_norm', 'ffn2_w_gate', 'ffn2_w_up', 'ffn2_w_down', 'w_mem_kv']}
    per_a = {n: [None] * n_a for n in ['a_w_in', 'a_v_norm', 'a_w_spatial', 'a_b_spatial', 'a_w_out']}
    per_b = {n: [None] * (depth - n_a) for n in ['b_w_in', 'b_w_out']}
    d_mem_h = None
    dk = dv = None
    for l in reversed(range(depth)):
        s1, sm, s2 = saved[l]
        dx, dg, dwg, dwu, dwd = _ffn_bwd(dx, s2, W['ffn2_norm'][l], f"l{l}_ffn2")
        per_layer['ffn2_norm'][l], per_layer['ffn2_w_gate'][l] = dg.reshape(-1), dwg
        per_layer['ffn2_w_up'][l], per_layer['ffn2_w_down'][l] = dwu, dwd

        x_mid, hm, mem_kv, proj, y_tok, y_mem, q_cb = sm
        is_a = l < n_a
        w_in, w_out, tok_w = mixer_w(l)
        dy = mm([(dx[1], 0, w_out, D, 0)], tb=True, name=f"l{l}_mix_out_bwd")
        d_w_out = jnp.concatenate([mm_tn(y_tok, dx[1], name=f"l{l}_dwout_tok"),
                                   mm_tn(y_mem, dx[1], name=f"l{l}_dwout_mem")], axis=0)
        dq_mem, d_mem_kv = memattn_bwd(proj, q_cb, mem_kv, dy, GM_W // MEM_W, f"l{l}_memattn_bwd")
        if is_a:
            d_tok, d_ws, d_bs, d_vg = gmlp_bwd(proj, dy, W['a_v_norm'][l], W['a_w_spatial'][l], b_mats[l],
                                               f"l{l}_gmlp_bwd")
            per_a['a_w_spatial'][l], per_a['a_b_spatial'][l] = d_ws, d_bs[:, :, 0]
            per_a['a_v_norm'][l], per_a['a_w_out'][l] = d_vg.reshape(-1), d_w_out
        else:
            d_tok, dk, dv = sb_bwd(proj, kv, dy, y_tok, dk, dv, f"l{l}_sb_bwd")
            per_b['b_w_out'][l - n_a] = d_w_out
        d_w_in = jnp.concatenate([mm_tn(hm, d_tok, name=f"l{l}_dwin_tok"),
                                  mm_tn(hm, dq_mem, name=f"l{l}_dwin_mem")], axis=1)
        (per_a['a_w_in'] if is_a else per_b['b_w_in'])[l if is_a else l - n_a] = d_w_in
        dx, dg = mm([(d_tok, 0, w_in, tok_w, 0), (dq_mem, 0, w_in, MEM_W, tok_w // MEM_W)], tb=True, res=dx[0],
                    norm_bwd=(x_mid, W['mix_norm'][l]), name=f"l{l}_mix_in_bwd")
        per_layer['mix_norm'][l] = dg.reshape(-1)
        per_layer['w_mem_kv'][l] = mm_tn(mem_h, d_mem_kv, name=f"l{l}_dw_mem_kv")
        d_mem_h = mm([(d_mem_kv, 0, (W['w_mem_kv'], l), 2 * MEM_W, 0)], tb=True, res=d_mem_h,
                     name=f"l{l}_mem_kv_bwd")

        dx, dg, dwg, dwu, dwd = _ffn_bwd(dx, s1, W['ffn1_norm'][l], f"l{l}_ffn1")
        per_layer['ffn1_norm'][l], per_layer['ffn1_w_gate'][l] = dg.reshape(-1), dwg
        per_layer['ffn1_w_up'][l], per_layer['ffn1_w_down'][l] = dwu, dwd
        if l == n_a:
            G['w_kv'] = jnp.concatenate([mm_tn(kvn, dk, name="dw_k"), mm_tn(kvn, dv, name="dw_v")], axis=1)
            dx, dg = mm([(dk, 0, w_kv, GM_W, 0), (dv, 0, w_kv, GM_W, 1)], tb=True, res=dx[0],
                        norm_bwd=(x_kv, W['kv_norm']), name="kv_proj_bwd")
            G['kv_norm'] = dg.reshape(-1)

    G['mem_norm'] = rms_gain_grad(mem, d_mem_h, "mem_norm_bwd").reshape(-1)
    for d in (per_layer, per_a, per_b):
        G.update(d)
    return loss, dx[0], G


def kernel(x, mem, ffn1_norm, ffn1_w_gate, ffn1_w_up, ffn1_w_down, mix_norm, ffn2_norm, ffn2_w_gate, ffn2_w_up, ffn2_w_down, mem_norm, w_mem_kv, a_w_in, a_v_norm, a_w_spatial, a_b_spatial, a_w_out, kv_norm, w_kv, b_w_in, b_w_out, final_norm, loss_target, m_ffn1_norm, m_ffn1_w_gate, m_ffn1_w_up, m_ffn1_w_down, m_mix_norm, m_ffn2_norm, m_ffn2_w_gate, m_ffn2_w_up, m_ffn2_w_down, m_mem_norm, m_w_mem_kv, m_a_w_in, m_a_v_norm, m_a_w_spatial, m_a_b_spatial, m_a_w_out, m_kv_norm, m_w_kv, m_b_w_in, m_b_w_out, m_final_norm, v_ffn1_norm, v_ffn1_w_gate, v_ffn1_w_up, v_ffn1_w_down, v_mix_norm, v_ffn2_norm, v_ffn2_w_gate, v_ffn2_w_up, v_ffn2_w_down, v_mem_norm, v_w_mem_kv, v_a_w_in, v_a_v_norm, v_a_w_spatial, v_a_b_spatial, v_a_w_out, v_kv_norm, v_w_kv, v_b_w_in, v_b_w_out, v_final_norm):
    weights = dict(zip(W_NAMES, [ffn1_norm, ffn1_w_gate, ffn1_w_up, ffn1_w_down, mix_norm, ffn2_norm, ffn2_w_gate,
                                 ffn2_w_up, ffn2_w_down, mem_norm, w_mem_kv, a_w_in, a_v_norm, a_w_spatial,
                                 a_b_spatial, a_w_out, kv_norm, w_kv, b_w_in, b_w_out, final_norm]))
    m_in = dict(zip(W_NAMES, [m_ffn1_norm, m_ffn1_w_gate, m_ffn1_w_up, m_ffn1_w_down, m_mix_norm, m_ffn2_norm,
                              m_ffn2_w_gate, m_ffn2_w_up, m_ffn2_w_down, m_mem_norm, m_w_mem_kv, m_a_w_in,
                              m_a_v_norm, m_a_w_spatial, m_a_b_spatial, m_a_w_out, m_kv_norm, m_w_kv, m_b_w_in,
                              m_b_w_out, m_final_norm]))
    v_in = dict(zip(W_NAMES, [v_ffn1_norm, v_ffn1_w_gate, v_ffn1_w_up, v_ffn1_w_down, v_mix_norm, v_ffn2_norm,
                              v_ffn2_w_gate, v_ffn2_w_up, v_ffn2_w_down, v_mem_norm, v_w_mem_kv, v_a_w_in,
                              v_a_v_norm, v_a_w_spatial, v_a_b_spatial, v_a_w_out, v_kv_norm, v_w_kv, v_b_w_in,
                              v_b_w_out, v_final_norm]))

    stream = WeightStream({n: weights[n] for n in SHARD_AXIS})
    W = {n: weights[n] for n in W_NAMES if n not in SHARD_AXIS}
    W.update(stream.rest())
    loss, dx, grads = local_step(x[0], mem[0], W, stream, loss_target[0])
    total = reduce_grads(grads, {n: weights[n].shape for n in SHARD_AXIS})
    loss = lax.psum(loss[0, 0], ("x", "y", "c"))

    deltas, new_m, new_v = {}, {}, {}
    for n in W_NAMES:
        deltas[n], new_m[n], new_v[n] = adamw(weights[n], total[n], m_in[n], v_in[n], "adamw_" + n)
    return (loss, dx[None], *[total[n] for n in W_NAMES], *[deltas[n] for n in W_NAMES],
            *[new_m[n] for n in W_NAMES], *[new_v[n] for n in W_NAMES])
```

```python
import functools
import math

import jax
import jax.numpy as jnp
from jax import lax
from jax.experimental import pallas as pl
from jax.experimental.pallas import tpu as pltpu

F32 = jnp.float32
BF16 = jnp.bfloat16
EPS = 1e-6
LANES = 128
ROW_TILE = 512
COL_TILE = 1408
PACK_ROWS = 256
PART_ROWS = 16
VMEM_LIMIT = 56 * 1024 * 1024

W_NAMES = ['ffn1_norm', 'ffn1_w_gate', 'ffn1_w_up', 'ffn1_w_down', 'mix_norm', 'ffn2_norm', 'ffn2_w_gate',
           'ffn2_w_up', 'ffn2_w_down', 'mem_norm', 'w_mem_kv', 'a_w_in', 'a_v_norm', 'a_w_spatial',
           'a_b_spatial', 'a_w_out', 'kv_norm', 'w_kv', 'b_w_in', 'b_w_out', 'final_norm']
SHARD_AXIS = {'ffn1_w_gate': 2, 'ffn1_w_up': 2, 'ffn1_w_down': 1, 'ffn2_w_gate': 2, 'ffn2_w_up': 2,
              'ffn2_w_down': 1, 'w_mem_kv': 1, 'a_w_in': 2, 'a_v_norm': 1, 'a_w_out': 1, 'w_kv': 1,
              'b_w_in': 1, 'b_w_out': 1}
N_CHIPS = 4
PACK_W = 1024

MEM_HEADS = 4
MEM_W = 256
HEAD_DIM = 64
GM_W = 768
GM_GROUPS = 6
GM_CHUNK = 128
CHUNK = 64
SB_BLOCK = 128
SB_Q_CHUNK = 1024
SB_DEAD = -110.0
SB_PAIRS = 2
SB_COLS = SB_PAIRS * LANES

ADAM_LR, ADAM_B1, ADAM_B2, ADAM_EPS, ADAM_WD, ADAM_STEP = 0.001, 0.9, 0.999, 1e-08, 0.01, 10

MESH = pl.DeviceIdType.MESH


def _cparams(sem=None):
    return pltpu.CompilerParams(dimension_semantics=sem, vmem_limit_bytes=VMEM_LIMIT)


class Hosted:
    def __init__(self, arrays, n_out, sems, make, fresh=()):
        self.arrays, self.n_out, self.sems, self.make, self.fresh = arrays, n_out, sems, make, list(fresh)


def _call(body, *, grid, in_specs, out_specs, out_shape, ins, semantics, name, hosted=None):
    if hosted is None:
        return pl.pallas_call(body, grid=grid, in_specs=in_specs, out_specs=out_specs, out_shape=out_shape,
                              compiler_params=_cparams(semantics), name=name)(*ins)
    n_in, n_out = len(ins), len(out_shape)
    n_h, n_alias = len(hosted.arrays), hosted.n_out
    n_ho = n_alias + len(hosted.fresh)

    def hosting_body(*refs):
        base_in, h_in = refs[:n_in], refs[n_in:n_in + n_h]
        base_out = refs[n_in + n_h:n_in + n_h + n_out]
        h_out = refs[n_in + n_h + n_out:n_in + n_h + n_out + n_ho]
        sems = refs[n_in + n_h + n_out + n_ho:]
        copies, waits = hosted.make(h_in, h_out, *sems)
        first = last = None
        for axis, extent in enumerate(grid):
            at_start, at_end = pl.program_id(axis) == 0, pl.program_id(axis) == extent - 1
            first = at_start if first is None else jnp.logical_and(first, at_start)
            last = at_end if last is None else jnp.logical_and(last, at_end)

        @pl.when(first)
        def _():
            for cp in copies:
                cp.start()

        body(*base_in, *base_out)

        @pl.when(last)
        def _():
            for wait in waits:
                wait()

    any_spec = pl.BlockSpec(memory_space=pl.ANY)
    outs = pl.pallas_call(
        hosting_body, grid=grid, in_specs=list(in_specs) + [any_spec] * n_h,
        out_specs=list(out_specs) + [any_spec] * n_ho,
        out_shape=(list(out_shape) + [jax.ShapeDtypeStruct(a.shape, a.dtype) for a in hosted.arrays[:n_alias]]
                   + hosted.fresh),
        scratch_shapes=[pltpu.SemaphoreType.DMA((n,)) for n in hosted.sems],
        input_output_aliases={n_in + k: n_out + k for k in range(n_alias)},
        compiler_params=_cparams(("arbitrary",) * len(grid)), name=name)(*ins, *hosted.arrays)
    return outs[:n_out], outs[n_out:]


def _tile(n, target, mult=LANES):
    best = None
    for t in range(mult, min(n, target) + 1, mult):
        if n % t == 0:
            best = t
    return best if best is not None else n


def _dot(a, b):
    return jnp.dot(a, b, preferred_element_type=F32)


def _dot_nt(a, b):
    return lax.dot_general(a, b, (((1,), (1,)), ((), ())), preferred_element_type=F32)


def _dot_tn(a, b):
    return lax.dot_general(a, b, (((0,), (0,)), ((), ())), preferred_element_type=F32)


def _bf(v):
    return v.astype(BF16)


def rms_fwd(x, gain, name):
    S, D = x.shape
    tm = _tile(S, ROW_TILE, 8)

    def body(x_ref, g_ref, o_ref):
        xf = x_ref[...]
        r = lax.rsqrt(jnp.mean(xf * xf, axis=-1, keepdims=True) + EPS)
        o_ref[...] = ((xf * r) * g_ref[...]).astype(o_ref.dtype)

    return pl.pallas_call(
        body, grid=(S // tm,),
        in_specs=[pl.BlockSpec((tm, D), lambda i: (i, 0)), pl.BlockSpec((1, D), lambda i: (0, 0))],
        out_specs=pl.BlockSpec((tm, D), lambda i: (i, 0)),
        out_shape=jax.ShapeDtypeStruct((S, D), BF16),
        compiler_params=_cparams(("parallel",)), name=name)(x, gain.reshape(1, D))


def rms_gain_grad(x, dh, name):
    S, D = x.shape
    tm = _tile(S, ROW_TILE, 8)

    def body(x_ref, dh_ref, dg_ref):
        xf = x_ref[...]
        r = lax.rsqrt(jnp.mean(xf * xf, axis=-1, keepdims=True) + EPS)

        @pl.when(pl.program_id(0) == 0)
        def _():
            dg_ref[...] = jnp.zeros_like(dg_ref)

        dg_ref[...] += jnp.sum(dh_ref[...] * (xf * r), axis=0, keepdims=True)

    row = pl.BlockSpec((tm, D), lambda i: (i, 0))
    return pl.pallas_call(
        body, grid=(S // tm,), in_specs=[row, row], out_specs=pl.BlockSpec((1, D), lambda i: (0, 0)),
        out_shape=jax.ShapeDtypeStruct((1, D), F32),
        compiler_params=_cparams(("arbitrary",)), name=name)(x, dh)


def final_loss(x, gain, target, name):
    S, D = x.shape
    tm = _tile(S, ROW_TILE, 16)

    def body(x_ref, g_ref, t_ref, loss_ref, dx_ref, dxb_ref, dg_ref):
        xf = x_ref[...]
        r = lax.rsqrt(jnp.mean(xf * xf, axis=-1, keepdims=True) + EPS)
        xhat = xf * r
        g = g_ref[...]
        diff = xhat * g - t_ref[...]
        dy = diff * (1.0 / D)
        dxhat = dy * g
        m = jnp.mean(dxhat * xhat, axis=-1, keepdims=True)
        dx = r * (dxhat - xhat * m)
        dx_ref[...] = dx
        dxb_ref[...] = _bf(dx)

        @pl.when(pl.program_id(0) == 0)
        def _():
            dg_ref[...] = jnp.zeros_like(dg_ref)
            loss_ref[...] = jnp.zeros_like(loss_ref)

        dg_ref[...] += jnp.sum(dy * xhat, axis=0, keepdims=True)
        per_tok = jnp.sum(diff * diff, axis=-1, keepdims=True) * (0.5 / D)
        loss_ref[...] += jnp.sum(per_tok, axis=0, keepdims=True)

    row = pl.BlockSpec((tm, D), lambda i: (i, 0))
    vec = pl.BlockSpec((1, D), lambda i: (0, 0))
    one = pl.BlockSpec((1, 1), lambda i: (0, 0))
    loss, dx, dxb, dg = pl.pallas_call(
        body, grid=(S // tm,), in_specs=[row, vec, row], out_specs=[one, row, row, vec],
        out_shape=[jax.ShapeDtypeStruct((1, 1), F32), jax.ShapeDtypeStruct((S, D), F32),
                   jax.ShapeDtypeStruct((S, D), BF16), jax.ShapeDtypeStruct((1, D), F32)],
        compiler_params=_cparams(("arbitrary",)), name=name)(x, gain.reshape(1, D), target)
    return loss, (dx, dxb), dg


def _wspec(w, blk, idx):
    _, l = w
    if l is None:
        return pl.BlockSpec(blk, idx)
    return pl.BlockSpec((None,) + blk, lambda *g: (l,) + idx(*g))


def mm(pairs, *, tb=False, res=None, scale=1.0, out_dtype=F32, norm_bwd=None, hosted=None, name):
    M = pairs[0][0].shape[0]
    N = pairs[0][2][0].shape[-2 if tb else -1]
    tm = _tile(M, ROW_TILE, 16)
    tn = N if norm_bwd is not None else _tile(N, COL_TILE)
    n_p = len(pairs)
    has_res = res is not None
    n_in = 2 * n_p + has_res + (2 if norm_bwd is not None else 0)

    def body(*refs):
        acc = None
        for p in range(n_p):
            a = _bf(refs[2 * p][...])
            b = _bf(refs[2 * p + 1][...])
            d = _dot_nt(a, b) if tb else _dot(a, b)
            acc = d if acc is None else acc + d
        if scale != 1.0:
            acc = acc * scale
        if norm_bwd is None:
            if has_res:
                acc = acc + refs[2 * n_p][...]
            refs[n_in][...] = acc.astype(refs[n_in].dtype)
            return
        x_ref, g_ref = refs[n_in - 2], refs[n_in - 1]
        dx_ref, dxb_ref, dg_ref = refs[n_in:]
        xf = x_ref[...]
        r = lax.rsqrt(jnp.mean(xf * xf, axis=-1, keepdims=True) + EPS)
        xhat = xf * r
        dxhat = acc * g_ref[...]
        m = jnp.mean(dxhat * xhat, axis=-1, keepdims=True)
        dx = r * (dxhat - xhat * m)
        if has_res:
            dx = dx + refs[2 * n_p][...]
        dx_ref[...] = dx
        dxb_ref[...] = _bf(dx)

        @pl.when(pl.program_id(1) == 0)
        def _():
            dg_ref[...] = jnp.zeros_like(dg_ref)

        dg_ref[...] += jnp.sum(acc * xhat, axis=0, keepdims=True)

    ins, in_specs = [], []
    for a, cb, w, K, kb in pairs:
        ins += [a, w[0]]
        in_specs.append(pl.BlockSpec((tm, K), functools.partial(lambda j, i, cb: (i, cb), cb=cb)))
        if tb:
            in_specs.append(_wspec(w, (tn, K), functools.partial(lambda j, i, kb: (j, kb), kb=kb)))
        else:
            in_specs.append(_wspec(w, (K, tn), functools.partial(lambda j, i, kb: (kb, j), kb=kb)))
    tile = pl.BlockSpec((tm, tn), lambda j, i: (i, j))
    if has_res:
        ins.append(res)
        in_specs.append(tile)
    if norm_bwd is None:
        out = _call(body, grid=(N // tn, M // tm), in_specs=in_specs, out_specs=[tile],
                    out_shape=[jax.ShapeDtypeStruct((M, N), out_dtype)], ins=ins,
                    semantics=("parallel", "parallel"), name=name, hosted=hosted)
        return out[0] if hosted is None else (out[0][0], out[1])
    x, gain = norm_bwd
    vec = pl.BlockSpec((1, N), lambda j, i: (0, 0))
    out = _call(
        body, grid=(1, M // tm), in_specs=in_specs + [tile, vec], out_specs=[tile, tile, vec],
        out_shape=[jax.ShapeDtypeStruct((M, N), F32), jax.ShapeDtypeStruct((M, N), BF16),
                   jax.ShapeDtypeStruct((1, N), F32)], ins=ins + [x, gain.reshape(1, N)],
        semantics=("arbitrary", "arbitrary"), name=name, hosted=hosted)
    dx, dxb, dg = out if hosted is None else out[0]
    return ((dx, dxb), dg) if hosted is None else (((dx, dxb), dg), out[1])


def mm_tn(a, b, *, a_cb=0, a_w=None, b_cb=0, b_w=None, scale=1.0, hosted=None, name):
    S = a.shape[0]
    Ka = a.shape[1] if a_w is None else a_w
    Nb = b.shape[1] if b_w is None else b_w
    ts = _tile(S, 2 * ROW_TILE, 8)
    tk = _tile(Ka, COL_TILE)
    tn = _tile(Nb, COL_TILE)
    a_off, b_off = a_cb * (Ka // tk), b_cb * (Nb // tn)

    def body(a_ref, b_ref, o_ref):
        s = pl.program_id(2)

        @pl.when(s == 0)
        def _():
            o_ref[...] = jnp.zeros_like(o_ref)

        o_ref[...] += _dot_tn(_bf(a_ref[...]), _bf(b_ref[...]))
        if scale != 1.0:
            @pl.when(s == pl.num_programs(2) - 1)
            def _():
                o_ref[...] = o_ref[...] * scale

    out = _call(
        body, grid=(Ka // tk, Nb // tn, S // ts),
        in_specs=[pl.BlockSpec((ts, tk), lambda k, n, s: (s, a_off + k)),
                  pl.BlockSpec((ts, tn), lambda k, n, s: (s, b_off + n))],
        out_specs=[pl.BlockSpec((tk, tn), lambda k, n, s: (k, n))],
        out_shape=[jax.ShapeDtypeStruct((Ka, Nb), F32)], ins=[a, b],
        semantics=("parallel", "parallel", "arbitrary"), name=name, hosted=hosted)
    return out[0] if hosted is None else (out[0][0], out[1])


def _sigmoid(x):
    return 0.5 * jnp.tanh(0.5 * x) + 0.5


def ffn_up(h, wg, wu, name, hosted=None):
    S, D = h.shape
    Fd = wg[0].shape[-1]
    tm = _tile(S, ROW_TILE, 8)
    tn = _tile(Fd, COL_TILE)

    def body(h_ref, wg_ref, wu_ref, g_ref, u_ref, a_ref):
        hv = h_ref[...]
        g = _dot(hv, wg_ref[...])
        u = _dot(hv, wu_ref[...])
        g_ref[...] = _bf(g)
        u_ref[...] = _bf(u)
        a_ref[...] = _bf(g * _sigmoid(g) * u)

    blk = pl.BlockSpec((tm, tn), lambda j, i: (i, j))
    sh = jax.ShapeDtypeStruct((S, Fd), BF16)
    return _call(
        body, grid=(Fd // tn, S // tm),
        in_specs=[pl.BlockSpec((tm, D), lambda j, i: (i, 0)), _wspec(wg, (D, tn), lambda j, i: (0, j)),
                  _wspec(wu, (D, tn), lambda j, i: (0, j))],
        out_specs=[blk, blk, blk], out_shape=[sh, sh, sh], ins=[h, wg[0], wu[0]],
        semantics=("parallel", "parallel"), name=name, hosted=hosted)


def ffn_down_bwd(dxo, wd, g, u, name, hosted=None):
    S, D = dxo.shape
    Fd = wd[0].shape[-2]
    tm = _tile(S, ROW_TILE, 8)
    tn = _tile(Fd, COL_TILE)

    def body(dx_ref, wd_ref, g_ref, u_ref, dg_ref, du_ref):
        da = _dot_nt(_bf(dx_ref[...]), wd_ref[...]) * 0.5
        g = g_ref[...].astype(F32)
        u = u_ref[...].astype(F32)
        sg = _sigmoid(g)
        dg_ref[...] = _bf(da * u * (sg * (1.0 + g * (1.0 - sg))))
        du_ref[...] = _bf(da * (g * sg))

    blk = pl.BlockSpec((tm, tn), lambda j, i: (i, j))
    sh = jax.ShapeDtypeStruct((S, Fd), BF16)
    return _call(
        body, grid=(Fd // tn, S // tm),
        in_specs=[pl.BlockSpec((tm, D), lambda j, i: (i, 0)), _wspec(wd, (tn, D), lambda j, i: (j, 0)),
                  blk, blk],
        out_specs=[blk, blk], out_shape=[sh, sh], ins=[dxo, wd[0], g, u],
        semantics=("parallel", "parallel"), name=name, hosted=hosted)


_GELU_C = math.sqrt(2.0 / math.pi)


def _gelu(p):
    return 0.5 * p * (1.0 + jnp.tanh(_GELU_C * (p + 0.044715 * (p * p * p))))


def _gelu_grad(p):
    th = jnp.tanh(_GELU_C * (p + 0.044715 * (p * p * p)))
    return 0.5 * (1.0 + th) + 0.5 * p * (1.0 - th * th) * (_GELU_C * (1.0 + 3.0 * 0.044715 * (p * p)))


def _chunk_mask():
    t = lax.broadcasted_iota(jnp.int32, (GM_CHUNK, GM_CHUNK), 0)
    s = lax.broadcasted_iota(jnp.int32, (GM_CHUNK, GM_CHUNK), 1)
    return (s // CHUNK) <= (t // CHUNK)


def gmlp_fwd(proj, v_gain, w_s, b_mat, name):
    S = proj.shape[0]
    tw = _tile(S, ROW_TILE, GM_CHUNK)
    n_win = tw // GM_CHUNK

    def body(p_ref, gain_ref, w_ref, b_ref, y_ref):
        mask = _chunk_mask()
        v = _gelu(p_ref[:, GM_W:])
        r = lax.rsqrt(jnp.mean(v * v, axis=-1, keepdims=True) + EPS)
        vn = _bf(v * r * gain_ref[...])
        for g in range(GM_GROUPS):
            wm = _bf(jnp.where(mask, w_ref[g], 0.0))
            cs = slice(g * GM_CHUNK, (g + 1) * GM_CHUNK)
            for w in range(n_win):
                rs = slice(w * GM_CHUNK, (w + 1) * GM_CHUNK)
                mixed = _dot(wm, vn[rs, cs]) + b_ref[g]
                y_ref[rs, cs] = _bf(_gelu(p_ref[rs, cs]) * mixed)

    return pl.pallas_call(
        body, grid=(S // tw,),
        in_specs=[pl.BlockSpec((tw, 2 * GM_W), lambda i: (i, 0)), pl.BlockSpec((1, GM_W), lambda i: (0, 0)),
                  pl.BlockSpec((GM_GROUPS, GM_CHUNK, GM_CHUNK), lambda i: (0, 0, 0)),
                  pl.BlockSpec((GM_GROUPS, GM_CHUNK, GM_CHUNK), lambda i: (0, 0, 0))],
        out_specs=pl.BlockSpec((tw, GM_W), lambda i: (i, 0)),
        out_shape=jax.ShapeDtypeStruct((S, GM_W), BF16),
        compiler_params=_cparams(("parallel",)), name=name)(proj, v_gain.reshape(1, GM_W), w_s, b_mat)


def gmlp_bwd(proj, dy, v_gain, w_s, b_mat, name):
    S = proj.shape[0]
    tw = _tile(S, ROW_TILE, GM_CHUNK)
    n_win = tw // GM_CHUNK

    def body(p_ref, dy_ref, gain_ref, w_ref, b_ref, dp_ref, dw_ref, db_ref, dgain_ref, dvn_ref):
        step = pl.program_id(0)

        @pl.when(step == 0)
        def _():
            dw_ref[...] = jnp.zeros_like(dw_ref)
            db_ref[...] = jnp.zeros_like(db_ref)
            dgain_ref[...] = jnp.zeros_like(dgain_ref)

        mask = _chunk_mask()
        pv = p_ref[:, GM_W:]
        v = _gelu(pv)
        r = lax.rsqrt(jnp.mean(v * v, axis=-1, keepdims=True) + EPS)
        vhat = v * r
        gain = gain_ref[...]
        vn = _bf(vhat * gain)
        for g in range(GM_GROUPS):
            wm = _bf(jnp.where(mask, w_ref[g], 0.0))
            cs = slice(g * GM_CHUNK, (g + 1) * GM_CHUNK)
            dw_acc = jnp.zeros((GM_CHUNK, GM_CHUNK), F32)
            db_acc = jnp.zeros((GM_CHUNK, GM_CHUNK), F32)
            for w in range(n_win):
                rs = slice(w * GM_CHUNK, (w + 1) * GM_CHUNK)
                pu = p_ref[rs, cs]
                u = _gelu(pu)
                vn_blk = vn[rs, cs]
                mixed = _dot(wm, vn_blk) + b_ref[g]
                dyb = dy_ref[rs, cs]
                dp_ref[rs, cs] = _bf(dyb * mixed * _gelu_grad(pu))
                dmix = dyb * u
                db_acc = db_acc + dmix
                dmix_b = _bf(dmix)
                dw_acc = dw_acc + _dot_nt(dmix_b, vn_blk)
                dvn_ref[rs, cs] = _dot_tn(wm, dmix_b)
            dw_ref[g] += jnp.where(mask, dw_acc, 0.0)
            db_ref[g] += jnp.broadcast_to(jnp.sum(db_acc, axis=-1, keepdims=True), (GM_CHUNK, GM_CHUNK))
        dvn = dvn_ref[...]
        dgain_ref[...] += jnp.sum(dvn * vhat, axis=0, keepdims=True)
        dvhat = dvn * gain
        m = jnp.mean(dvhat * vhat, axis=-1, keepdims=True)
        dv = r * (dvhat - vhat * m)
        dp_ref[:, GM_W:] = _bf(dv * _gelu_grad(pv))

    sq = pl.BlockSpec((GM_GROUPS, GM_CHUNK, GM_CHUNK), lambda i: (0, 0, 0))
    sq_sh = jax.ShapeDtypeStruct((GM_GROUPS, GM_CHUNK, GM_CHUNK), F32)
    return pl.pallas_call(
        body, grid=(S // tw,),
        in_specs=[pl.BlockSpec((tw, 2 * GM_W), lambda i: (i, 0)), pl.BlockSpec((tw, GM_W), lambda i: (i, 0)),
                  pl.BlockSpec((1, GM_W), lambda i: (0, 0)), sq, sq],
        out_specs=[pl.BlockSpec((tw, 2 * GM_W), lambda i: (i, 0)), sq, sq,
                   pl.BlockSpec((1, GM_W), lambda i: (0, 0))],
        out_shape=[jax.ShapeDtypeStruct((S, 2 * GM_W), BF16), sq_sh, sq_sh,
                   jax.ShapeDtypeStruct((1, GM_W), F32)],
        scratch_shapes=[pltpu.VMEM((tw, GM_W), F32)],
        compiler_params=_cparams(("arbitrary",)), name=name)(proj, dy, v_gain.reshape(1, GM_W), w_s, b_mat)


def _head_mask(h, width):
    lane = lax.broadcasted_iota(jnp.int32, (1, width), 1)
    return (lane >= HEAD_DIM * h) & (lane < HEAD_DIM * (h + 1))


def _mem_probs(q, k, h):
    kh = jnp.where(_head_mask(h, MEM_W), k, jnp.zeros_like(k))
    s = _dot_nt(q, kh) * (HEAD_DIM ** -0.5)
    s = s - jnp.max(s, axis=-1, keepdims=True)
    e = jnp.exp(s)
    return e * (1.0 / jnp.sum(e, axis=-1, keepdims=True)), kh


def memattn_fwd(proj, q_cb, mem_kv, name):
    S = proj.shape[0]
    NM = mem_kv.shape[0]
    tm = _tile(S, ROW_TILE, 8)

    def body(q_ref, kv_ref, o_ref):
        q = _bf(q_ref[...])
        k = _bf(kv_ref[:, :MEM_W])
        v = _bf(kv_ref[:, MEM_W:])
        acc = jnp.zeros((tm, MEM_W), F32)
        for h in range(MEM_HEADS):
            p, _ = _mem_probs(q, k, h)
            vh = jnp.where(_head_mask(h, MEM_W), v, jnp.zeros_like(v))
            acc = acc + _dot(_bf(p), vh)
        o_ref[...] = _bf(acc)

    return pl.pallas_call(
        body, grid=(S // tm,),
        in_specs=[pl.BlockSpec((tm, MEM_W), lambda i: (i, q_cb)), pl.BlockSpec((NM, 2 * MEM_W), lambda i: (0, 0))],
        out_specs=pl.BlockSpec((tm, MEM_W), lambda i: (i, 0)),
        out_shape=jax.ShapeDtypeStruct((S, MEM_W), BF16),
        compiler_params=_cparams(("parallel",)), name=name)(proj, mem_kv)


def memattn_bwd(proj, q_cb, mem_kv, dy, dy_cb, name):
    S = proj.shape[0]
    NM = mem_kv.shape[0]
    tm = _tile(S, ROW_TILE, 8)

    def body(q_ref, kv_ref, do_ref, dq_ref, dkv_ref):
        @pl.when(pl.program_id(0) == 0)
        def _():
            dkv_ref[...] = jnp.zeros_like(dkv_ref)

        q = _bf(q_ref[...])
        k = _bf(kv_ref[:, :MEM_W])
        v = _bf(kv_ref[:, MEM_W:])
        do = _bf(do_ref[...])
        dq = jnp.zeros((tm, MEM_W), F32)
        dk = jnp.zeros((NM, MEM_W), F32)
        dv = jnp.zeros((NM, MEM_W), F32)
        for h in range(MEM_HEADS):
            hm = _head_mask(h, MEM_W)
            p, kh = _mem_probs(q, k, h)
            vh = jnp.where(hm, v, jnp.zeros_like(v))
            dp = _dot_nt(do, vh)
            ds = _bf(p * (dp - jnp.sum(p * dp, axis=-1, keepdims=True)) * (HEAD_DIM ** -0.5))
            dq = dq + _dot(ds, kh)
            dk = dk + jnp.where(hm, _dot_tn(ds, q), 0.0)
            dv = dv + jnp.where(hm, _dot_tn(_bf(p), do), 0.0)
        dq_ref[...] = _bf(dq)
        dkv_ref[:, :MEM_W] += dk
        dkv_ref[:, MEM_W:] += dv

    return pl.pallas_call(
        body, grid=(S // tm,),
        in_specs=[pl.BlockSpec((tm, MEM_W), lambda i: (i, q_cb)), pl.BlockSpec((NM, 2 * MEM_W), lambda i: (0, 0)),
                  pl.BlockSpec((tm, MEM_W), lambda i: (i, dy_cb))],
        out_specs=[pl.BlockSpec((tm, MEM_W), lambda i: (i, 0)), pl.BlockSpec((NM, 2 * MEM_W), lambda i: (0, 0))],
        out_shape=[jax.ShapeDtypeStruct((S, MEM_W), BF16), jax.ShapeDtypeStruct((NM, 2 * MEM_W), F32)],
        compiler_params=_cparams(("arbitrary",)), name=name)(proj, mem_kv, dy)


def _split_bf(v):
    hi = _bf(v)
    return hi, _bf(v - hi.astype(F32))


def _sb_scores(qp, k2, dpos, row, col, tri):
    z = _dot_nt(qp, k2) * (HEAD_DIM ** -0.5)
    t = jnp.log(1.0 + jnp.exp(-jnp.abs(z)))
    lb = jnp.minimum(z, 0.0) - t
    lom = jnp.where(_sb_mask(dpos, row, col), -jnp.maximum(z, 0.0) - t, 0.0)
    l_hi, l_lo = _split_bf(lom)
    insuf = _dot(l_hi, tri) + _dot(l_lo, tri)
    return lb + insuf, jnp.sum(lom, axis=1, keepdims=True), lb


def _sb_mask(dpos, row, col):
    return (col - row) < dpos


def _sb_weights(s, run, dpos, row, col):
    return jnp.where(_sb_mask(dpos, row, col), jnp.exp(s + run), 0.0)


def _sb_alive(runs):
    top = runs[0]
    for r in runs[1:]:
        top = jnp.maximum(top, r)
    return jnp.max(top) > SB_DEAD


def _sb_consts():
    row = lax.broadcasted_iota(jnp.int32, (2 * SB_BLOCK, SB_BLOCK), 0) % SB_BLOCK
    col = lax.broadcasted_iota(jnp.int32, (2 * SB_BLOCK, SB_BLOCK), 1)
    r = lax.broadcasted_iota(jnp.int32, (SB_BLOCK, SB_BLOCK), 0)
    c = lax.broadcasted_iota(jnp.int32, (SB_BLOCK, SB_BLOCK), 1)
    return row, col, _bf(jnp.where(r > c, 1.0, 0.0)), _bf(jnp.where(r >= c, 1.0, 0.0))


def _stack_heads(x2, masks):
    return jnp.concatenate([_bf(jnp.where(hm, x2, jnp.zeros_like(x2))) for hm in masks], axis=0)


def _side_by_side(x):
    return jnp.concatenate([x[:SB_BLOCK], x[SB_BLOCK:]], axis=1)


def sb_fwd(proj, kv, name):
    S = proj.shape[0]
    n_g = GM_W // SB_COLS
    tq = _tile(S, SB_Q_CHUNK, SB_BLOCK)
    nqb = tq // SB_BLOCK

    def body(q_ref, k_ref, v_ref, o_ref, s_ref):
        chunk = pl.program_id(1)
        row, col, tri, _ = _sb_consts()
        masks = [_head_mask(h, LANES) for h in range(2)]
        pairs = [slice(p * LANES, (p + 1) * LANES) for p in range(SB_PAIRS)]

        def q_loop(qi, _):
            i = chunk * nqb + qi
            rows = pl.ds(pl.multiple_of(qi * SB_BLOCK, SB_BLOCK), SB_BLOCK)
            qps = [_stack_heads(q_ref[rows, cs], masks) for cs in pairs]

            def look_ahead(t):
                keys = pl.ds(pl.multiple_of(jnp.maximum(i - t, 0) * SB_BLOCK, SB_BLOCK), SB_BLOCK)
                sums = []
                for p, cs in enumerate(pairs):
                    s, lom_sum, _ = _sb_scores(qps[p], k_ref[keys, cs], t * SB_BLOCK, row, col, tri)
                    s_ref[p] = s
                    sums.append(lom_sum)
                return tuple(sums)

            def k_step(carry):
                t, accs, runs, sums, _ = carry
                keys = pl.ds(pl.multiple_of((i - t) * SB_BLOCK, SB_BLOCK), SB_BLOCK)
                new_accs, new_runs = [], []
                for p, cs in enumerate(pairs):
                    a = _sb_weights(s_ref[p], runs[p], t * SB_BLOCK, row, col)
                    a_hi, a_lo = _split_bf(_side_by_side(a))
                    vp = _stack_heads(v_ref[keys, cs], masks)
                    new_accs.append(accs[p] + _dot(a_hi, vp) + _dot(a_lo, vp))
                    new_runs.append(runs[p] + sums[p])
                return t + 1, tuple(new_accs), tuple(new_runs), look_ahead(t + 1), _sb_alive(new_runs)

            zero = jnp.zeros((2 * SB_BLOCK, 1), F32)
            _, accs, _, _, _ = lax.while_loop(
                lambda carry: jnp.logical_and(carry[0] <= i, carry[4]), k_step,
                (jnp.int32(0), (jnp.zeros((SB_BLOCK, LANES), F32),) * SB_PAIRS, (zero,) * SB_PAIRS,
                 look_ahead(jnp.int32(0)), jnp.bool_(True)))
            for p, cs in enumerate(pairs):
                o_ref[rows, cs] = accs[p]
            return 0

        lax.fori_loop(0, nqb, q_loop, 0)

    return pl.pallas_call(
        body, grid=(n_g, S // tq),
        in_specs=[pl.BlockSpec((tq, SB_COLS), lambda g, c: (c, g)),
                  pl.BlockSpec((S, SB_COLS), lambda g, c: (0, g)),
                  pl.BlockSpec((S, SB_COLS), lambda g, c: (0, n_g + g))],
        out_specs=pl.BlockSpec((tq, SB_COLS), lambda g, c: (c, g)),
        out_shape=jax.ShapeDtypeStruct((S, GM_W), F32),
        scratch_shapes=[pltpu.VMEM((SB_PAIRS, 2 * SB_BLOCK, LANES), F32)],
        compiler_params=_cparams(("parallel", "parallel")), name=name)(proj, kv, kv)


def sb_bwd(proj, kv, dy, out, dk_init, dv_init, name):
    S = proj.shape[0]
    n_g = GM_W // SB_COLS
    tq = _tile(S, SB_Q_CHUNK, SB_BLOCK)
    nqb = tq // SB_BLOCK
    n_chunks = S // tq
    has_init = dk_init is not None
    scale = HEAD_DIM ** -0.5

    def body(*refs):
        s_ref, beta_ref, da_ref = refs[-3:]
        if has_init:
            q_ref, k_ref, v_ref, do_ref, out_ref, dki_ref, dvi_ref, dq_ref, dko_ref, dvo_ref, dk_acc, dv_acc = refs[:-3]
        else:
            q_ref, k_ref, v_ref, do_ref, out_ref, dq_ref, dko_ref, dvo_ref, dk_acc, dv_acc = refs[:-3]
        chunk = pl.program_id(1)
        cols = pl.ds(pl.multiple_of(pl.program_id(0) * SB_COLS, SB_COLS), SB_COLS)

        @pl.when(chunk == 0)
        def _():
            if has_init:
                pltpu.sync_copy(dki_ref.at[:, cols], dk_acc)
                pltpu.sync_copy(dvi_ref.at[:, cols], dv_acc)
            else:
                dk_acc[...] = jnp.zeros_like(dk_acc)
                dv_acc[...] = jnp.zeros_like(dv_acc)

        row, col, tri, tri_inc = _sb_consts()
        masks = [_head_mask(h, LANES) for h in range(2)]
        pairs = [slice(p * LANES, (p + 1) * LANES) for p in range(SB_PAIRS)]

        def q_loop(qi, _):
            i = chunk * nqb + qi
            rows = pl.ds(pl.multiple_of(qi * SB_BLOCK, SB_BLOCK), SB_BLOCK)
            qps = [_stack_heads(q_ref[rows, cs], masks) for cs in pairs]
            dops = [_stack_heads(do_ref[rows, cs], masks) for cs in pairs]
            e_tots = [jnp.sum(dop.astype(F32) * jnp.concatenate([out_ref[rows, cs]] * 2, axis=0), axis=1,
                              keepdims=True) for dop, cs in zip(dops, pairs)]

            def look_ahead(t):
                keys = pl.ds(pl.multiple_of(jnp.maximum(i - t, 0) * SB_BLOCK, SB_BLOCK), SB_BLOCK)
                sums = []
                for p, cs in enumerate(pairs):
                    s, lom_sum, lb = _sb_scores(qps[p], k_ref[keys, cs], t * SB_BLOCK, row, col, tri)
                    s_ref[p] = s
                    beta_ref[p] = jnp.exp(lb)
                    da_ref[p] = _dot_nt(dops[p], v_ref[keys, cs])
                    sums.append(lom_sum)
                return tuple(sums)

            def k_step(carry):
                t, dqs, runs, e_runs, sums, _ = carry
                keys = pl.ds(pl.multiple_of((i - t) * SB_BLOCK, SB_BLOCK), SB_BLOCK)
                new_dqs, new_runs, new_e_runs = [], [], []
                for p, cs in enumerate(pairs):
                    a = _sb_weights(s_ref[p], runs[p], t * SB_BLOCK, row, col)
                    e = a * da_ref[p]
                    e_hi, e_lo = _split_bf(e)
                    before = e_tots[p] - e_runs[p] - (_dot(e_hi, tri_inc) + _dot(e_lo, tri_inc))
                    beta = beta_ref[p]
                    dz = jnp.where(_sb_mask(t * SB_BLOCK, row, col), e * (1.0 - beta) - before * beta, 0.0)
                    dz = _bf(dz * scale)
                    new_dqs.append(dqs[p] + _dot(_side_by_side(dz), _stack_heads(k_ref[keys, cs], masks)))
                    dk_acc[keys, cs] += _dot_tn(dz, qps[p])
                    dv_acc[keys, cs] += _dot_tn(_bf(a), dops[p])
                    new_runs.append(runs[p] + sums[p])
                    new_e_runs.append(e_runs[p] + jnp.sum(e, axis=1, keepdims=True))
                return (t + 1, tuple(new_dqs), tuple(new_runs), tuple(new_e_runs), look_ahead(t + 1),
                        _sb_alive(new_runs))

            zero = jnp.zeros((2 * SB_BLOCK, 1), F32)
            zeros = (zero,) * SB_PAIRS
            _, dqs, _, _, _, _ = lax.while_loop(
                lambda carry: jnp.logical_and(carry[0] <= i, carry[5]), k_step,
                (jnp.int32(0), (jnp.zeros((SB_BLOCK, LANES), F32),) * SB_PAIRS, zeros, zeros,
                 look_ahead(jnp.int32(0)), jnp.bool_(True)))
            for p, cs in enumerate(pairs):
                dq_ref[rows, cs] = dqs[p]
            return 0

        lax.fori_loop(0, nqb, q_loop, 0)

        @pl.when(chunk == n_chunks - 1)
        def _():
            pltpu.sync_copy(dk_acc, dko_ref.at[:, cols])
            pltpu.sync_copy(dv_acc, dvo_ref.at[:, cols])

    qspec = pl.BlockSpec((tq, SB_COLS), lambda g, c: (c, g))
    kspec = pl.BlockSpec((S, SB_COLS), lambda g, c: (0, g))
    in_specs = [qspec, kspec, pl.BlockSpec((S, SB_COLS), lambda g, c: (0, n_g + g)), qspec, qspec]
    ins = [proj, kv, kv, dy, out]
    if has_init:
        in_specs += [_ANY, _ANY]
        ins += [dk_init, dv_init]
    sh = jax.ShapeDtypeStruct((S, GM_W), F32)
    return pl.pallas_call(
        body, grid=(n_g, n_chunks), in_specs=in_specs, out_specs=[qspec, _ANY, _ANY],
        out_shape=[sh, sh, sh],
        scratch_shapes=[pltpu.VMEM((S, SB_COLS), F32), pltpu.VMEM((S, SB_COLS), F32)]
        + [pltpu.VMEM((SB_PAIRS, 2 * SB_BLOCK, LANES), F32)] * 3,
        compiler_params=_cparams(("arbitrary", "arbitrary")), name=name)(*ins)


def adamw(w, g, m, v, name):
    shape = w.shape
    C = shape[-1] if w.ndim > 1 else shape[0]
    R = w.size // C
    tr = _tile(R, ROW_TILE, 8)

    def body(w_ref, g_ref, m_ref, v_ref, d_ref, nm_ref, nv_ref):
        gv = g_ref[...]
        m2 = ADAM_B1 * m_ref[...] + (1.0 - ADAM_B1) * gv
        v2 = ADAM_B2 * v_ref[...] + (1.0 - ADAM_B2) * (gv * gv)
        m_hat = m2 / (1.0 - ADAM_B1 ** ADAM_STEP)
        v_hat = v2 / (1.0 - ADAM_B2 ** ADAM_STEP)
        d_ref[...] = -ADAM_LR * (m_hat / (jnp.sqrt(v_hat) + ADAM_EPS) + ADAM_WD * w_ref[...])
        nm_ref[...] = m2
        nv_ref[...] = v2

    blk = pl.BlockSpec((tr, C), lambda i: (i, 0))
    sh = jax.ShapeDtypeStruct((R, C), F32)
    outs = pl.pallas_call(
        body, grid=(R // tr,), in_specs=[blk] * 4, out_specs=[blk] * 3, out_shape=[sh] * 3,
        compiler_params=_cparams(("parallel",)), name=name)(
            w.reshape(R, C), g.reshape(R, C), m.reshape(R, C), v.reshape(R, C))
    return tuple(o.reshape(shape) for o in outs)


def _place():
    return lax.axis_index("x"), lax.axis_index("y"), lax.axis_index("c")


def _other_chips(x, y):
    return [(1 - x, y), (x, 1 - y), (1 - x, 1 - y)]


_ANY = pl.BlockSpec(memory_space=pl.ANY)
LOCAL_CHUNKS = 8
SEG_LOCAL_CHUNKS = 4


def _remote(src, dst, send_sems, recv_sems, k, to):
    return pltpu.make_async_remote_copy(src_ref=src, dst_ref=dst, send_sem=send_sems.at[k], recv_sem=recv_sems.at[k],
                                        device_id=to, device_id_type=MESH)


def _gather_stage(segs, n_gath, stage):
    def make(ins, outs, send_sems, recv_sems, *local_sems):
        x, y, c = _place()
        sibling = (x, y, 1 - c)
        chips = _other_chips(x, y)
        copies, waits = [], []
        for s, (b, r0, n) in enumerate(segs):
            half = n // 2

            def piece(px, py, pc, b=b, r0=r0, half=half):
                return outs[b].at[2 * px + py, pl.ds(r0 + pc * half, half), :]

            for j, chip in enumerate(chips):
                if stage == 0:
                    mine = ins[n_gath + b].at[pl.ds(r0 + c * half, half), :]
                    cp = _remote(mine, piece(x, y, c), send_sems, recv_sems, 3 * s + j, (*chip, c))
                    landed, sender = piece(*chip, c), (*chip, c)
                else:
                    cp = _remote(piece(*chip, c), piece(*chip, c), send_sems, recv_sems, 3 * s + j, sibling)
                    landed, sender = piece(*chip, 1 - c), sibling
                copies.append(cp)
                waits.append(cp.wait_send)
                waits.append(_remote(landed, landed, send_sems, recv_sems, 3 * s + j, sender).wait_recv)
            if stage == 0:
                rows = n // SEG_LOCAL_CHUNKS
                for q in range(SEG_LOCAL_CHUNKS):
                    r = pl.ds(r0 + q * rows, rows)
                    cp = pltpu.make_async_copy(ins[n_gath + b].at[r, :], outs[b].at[2 * x + y, r, :],
                                               local_sems[0].at[SEG_LOCAL_CHUNKS * s + q])
                    copies.append(cp)
                    waits.append(cp.wait)
        return copies, waits

    return make


def _gather_sems(segs, stage):
    return [3 * len(segs), 3 * len(segs)] + ([SEG_LOCAL_CHUNKS * len(segs)] if stage == 0 else [])


def all_gather_segments(bufs, segs, name):
    nb = len(bufs)
    n0, n1 = len(_gather_sems(segs, 0)), len(_gather_sems(segs, 1))

    def body(*refs):
        in_refs, out_refs = refs[:nb], refs[nb:2 * nb]
        sems = refs[2 * nb:]
        for stage, stage_sems in ((0, sems[:n0]), (1, sems[n0:n0 + n1])):
            copies, waits = _gather_stage(segs, nb, stage)(list(out_refs) + list(in_refs), out_refs, *stage_sems)
            for cp in copies:
                cp.start()
            for wait in waits:
                wait()

    return pl.pallas_call(
        body, in_specs=[_ANY] * nb, out_specs=[_ANY] * nb,
        out_shape=[jax.ShapeDtypeStruct((N_CHIPS,) + b.shape, b.dtype) for b in bufs],
        scratch_shapes=[pltpu.SemaphoreType.DMA((n,)) for n in _gather_sems(segs, 0) + _gather_sems(segs, 1)],
        name=name)(*bufs)


def hosted_gather(gathered, bufs, segs, stage):
    arrays = list(gathered) + (list(bufs) if stage == 0 else [])
    return Hosted(arrays, len(gathered), _gather_sems(segs, stage), _gather_stage(segs, len(gathered), stage))


def _exchange(step, arrays, name):
    n_h, n_alias, n_fresh = len(step.arrays), step.n_out, len(step.fresh)

    def body(*refs):
        h_in = refs[:n_h]
        h_out = refs[n_h:n_h + n_alias + n_fresh]
        copies, waits = step.make(h_in, h_out, *refs[n_h + n_alias + n_fresh:])
        for cp in copies:
            cp.start()
        for wait in waits:
            wait()

    return pl.pallas_call(
        body, in_specs=[_ANY] * n_h, out_specs=[_ANY] * (n_alias + n_fresh),
        out_shape=[jax.ShapeDtypeStruct(a.shape, a.dtype) for a in step.arrays[:n_alias]] + step.fresh,
        scratch_shapes=[pltpu.SemaphoreType.DMA((n,)) for n in step.sems],
        input_output_aliases={k: k for k in range(n_alias)}, name=name)(*arrays)


def swap_step(gs):
    nb = len(gs)

    def make(ins, outs, send_sems, recv_sems):
        x, y, c = _place()
        copies = []
        for b in range(nb):
            half = gs[b].shape[1] // 2
            for k in range(N_CHIPS):
                copies.append(_remote(ins[b].at[k, pl.ds((1 - c) * half, half), :], outs[b].at[k], send_sems,
                                      recv_sems, N_CHIPS * b + k, (x, y, 1 - c)))
        return copies, [cp.wait for cp in copies]

    fresh = [jax.ShapeDtypeStruct((N_CHIPS, g.shape[1] // 2, g.shape[2]), g.dtype) for g in gs]
    return Hosted(list(gs), 0, [N_CHIPS * nb, N_CHIPS * nb], make, fresh)


def _core_index():
    return lax.axis_index("c").astype(jnp.int32).reshape(1)


def add_cores(g, theirs, name):
    n, Ph, Wd = theirs.shape
    tr = _tile(Ph, ROW_TILE, 16)
    steps = Ph // tr

    def body(c_ref, g_ref, t_ref, o_ref):
        o_ref[...] = _bf(g_ref[...] + t_ref[...])

    blk = pl.BlockSpec((None, tr, Wd), lambda k, i, c_ref: (k, i, 0))
    return pl.pallas_call(
        body,
        grid_spec=pltpu.PrefetchScalarGridSpec(
            num_scalar_prefetch=1, grid=(n, steps),
            in_specs=[pl.BlockSpec((None, tr, Wd), lambda k, i, c_ref: (k, c_ref[0] * steps + i, 0)), blk],
            out_specs=blk),
        out_shape=jax.ShapeDtypeStruct((n, Ph, Wd), BF16),
        compiler_params=_cparams(("parallel", "parallel")), name=name)(_core_index(), g, theirs)


def add_chips(slots, name):
    n, Ph, Wd = slots.shape
    tr = _tile(Ph, ROW_TILE, 16)
    steps = Ph // tr

    def body(c_ref, *refs):
        acc = refs[0][...].astype(F32)
        for k in range(1, n):
            acc = acc + refs[k][...].astype(F32)
        refs[n][...] = acc

    in_specs = [pl.BlockSpec((None, tr, Wd), functools.partial(lambda i, c_ref, k: (k, i, 0), k=k)) for k in range(n)]
    return pl.pallas_call(
        body,
        grid_spec=pltpu.PrefetchScalarGridSpec(
            num_scalar_prefetch=1, grid=(steps,), in_specs=in_specs,
            out_specs=pl.BlockSpec((tr, Wd), lambda i, c_ref: (c_ref[0] * steps + i, 0))),
        out_shape=jax.ShapeDtypeStruct((2 * Ph, Wd), F32),
        compiler_params=_cparams(("parallel",)), name=name)(_core_index(), *([slots] * n))


def _whole_tile_chunks(rows, most, tile_rows):
    return max(n for n in range(1, most + 1) if rows % (n * tile_rows) == 0)


def scatter_step(hs):
    nb = len(hs)
    chunks = [_whole_tile_chunks(h.shape[1], LOCAL_CHUNKS, 16) for h in hs]

    def make(ins, outs, send_sems, recv_sems, local_sems):
        x, y, c = _place()
        me = 2 * x + y
        copies, waits = [], []
        for b in range(nb):
            for j, (cx, cy) in enumerate(_other_chips(x, y)):
                cp = _remote(ins[b].at[2 * cx + cy], outs[b].at[me], send_sems, recv_sems, 3 * b + j, (cx, cy, c))
                copies.append(cp)
                waits.append(cp.wait_send)
                landed = outs[b].at[2 * cx + cy]
                waits.append(_remote(landed, landed, send_sems, recv_sems, 3 * b + j, (cx, cy, c)).wait_recv)
            rows = hs[b].shape[1] // chunks[b]
            for q in range(chunks[b]):
                r = pl.ds(q * rows, rows)
                cp = pltpu.make_async_copy(ins[b].at[me, r, :], outs[b].at[me, r, :],
                                           local_sems.at[sum(chunks[:b]) + q])
                copies.append(cp)
                waits.append(cp.wait)
        return copies, waits

    fresh = [jax.ShapeDtypeStruct(h.shape, h.dtype) for h in hs]
    return Hosted(list(hs), 0, [3 * nb, 3 * nb, sum(chunks)], make, fresh)


JOIN_CHUNKS = 4


def join_step(ts):
    nb = len(ts)
    n = JOIN_CHUNKS

    def make(ins, outs, send_sems, recv_sems):
        x, y, c = _place()
        copies, waits = [], []
        for b in range(nb):
            half = ts[b].shape[0] // 2
            rows = half // n
            for k in range(n):
                part = outs[b].at[pl.ds(c * half + k * rows, rows), :]
                cp = _remote(part, part, send_sems, recv_sems, n * b + k, (x, y, 1 - c))
                copies.append(cp)
                waits.append(cp.wait_send)
                landed = outs[b].at[pl.ds((1 - c) * half + k * rows, rows), :]
                waits.append(_remote(landed, landed, send_sems, recv_sems, n * b + k, (x, y, 1 - c)).wait_recv)
        return copies, waits

    return Hosted(list(ts), nb, [n * nb, n * nb], make)


WIDE = ['ffn1_w_gate', 'ffn1_w_up', 'ffn2_w_gate', 'ffn2_w_up']


def _rows_of(p):
    if p.shape[-1] == PACK_W and (p.size // PACK_W) % PART_ROWS == 0:
        return p.reshape(-1, PACK_W)
    flat = p.reshape(-1)
    rows = -(-flat.shape[0] // (PACK_W * PART_ROWS)) * PART_ROWS
    return jnp.pad(flat, (0, rows * PACK_W - flat.shape[0])).reshape(rows, PACK_W)


def _pack(parts):
    buf = jnp.concatenate([_rows_of(p) for p in parts], axis=0)
    rows = buf.shape[0]
    total = -(-rows // PACK_ROWS) * PACK_ROWS
    return jnp.pad(buf, ((0, total - rows), (0, 0)))


def _unpack(buf, shapes):
    outs, r = [], 0
    lead = buf.shape[:-2]
    for shp in shapes:
        n = math.prod(shp)
        rows = -(-n // (PACK_W * PART_ROWS)) * PART_ROWS
        blk = buf[..., r:r + rows, :]
        if n != rows * PACK_W:
            blk = blk.reshape(lead + (rows * PACK_W,))[..., :n]
        outs.append(blk.reshape(lead + tuple(shp)))
        r += rows
    return outs


class WeightStream:
    def __init__(self, shards):
        depth = shards['ffn1_w_gate'].shape[0]
        order = [(which, l) for l in range(depth) for which in ('ffn1', 'ffn2')]
        self.count = len(order)
        gate_up = [shards[f'{which}_w_{part}'][l].astype(BF16) for which, l in order for part in ('gate', 'up')]
        downs = [shards[f'{which}_w_down'][l].astype(BF16) for which, l in order]
        self.d, self.cols = gate_up[0].shape
        assert downs[0].shape == (self.cols, PACK_W)
        self.rest_names = [n for n in W_NAMES if n in SHARD_AXIS and n not in WIDE and not n.endswith('_w_down')]
        rest = [lax.bitcast_convert_type(shards[n], BF16) if n == 'a_v_norm' else shards[n].astype(BF16)
                for n in self.rest_names]
        self.rest_shapes = [p.shape for p in rest]
        self.bufs = [jnp.concatenate(gate_up, axis=0), _pack(downs + rest)]
        self.rest_row0 = self.count * self.cols
        first = self.segments(0) + [(1, self.rest_row0, self.bufs[1].shape[0] - self.rest_row0)]
        self.gathered = all_gather_segments(self.bufs, first, "all_gather_first")

    def segments(self, i):
        return [(0, 2 * self.d * i, 2 * self.d), (1, self.cols * i, self.cols)]

    def rest(self):
        out = {}
        for n, g in zip(self.rest_names, _unpack(self.gathered[1][:, self.rest_row0:], self.rest_shapes)):
            if n == 'a_v_norm':
                g = lax.bitcast_convert_type(g, F32)
            out[n] = jnp.concatenate([g[k] for k in range(N_CHIPS)], axis=SHARD_AXIS[n])
        return out

    def ffn(self, i):
        wide, narrow = self.gathered
        r = 2 * self.d * i
        gate = jnp.concatenate([wide[k, r:r + self.d] for k in range(N_CHIPS)], axis=1)
        up = jnp.concatenate([wide[k, r + self.d:r + 2 * self.d] for k in range(N_CHIPS)], axis=1)
        down = narrow[:, self.cols * i:self.cols * (i + 1)].reshape(N_CHIPS * self.cols, PACK_W)
        return (gate, None), (up, None), (down, None)

    def hosted(self, i, stage):
        if i >= self.count:
            return None
        return hosted_gather(self.gathered, self.bufs, self.segments(i), stage)

    def absorb(self, gathered):
        self.gathered = list(gathered)


class GradSink:
    def __init__(self):
        self.queue = []
        self.totals = {}
        self.held = {}

    def add(self, i, d_wg, d_wu, d_wd, ride):
        cols = d_wg.shape[1] // N_CHIPS
        wide = jnp.stack([jnp.concatenate([d_wg[:, k * cols:(k + 1) * cols], d_wu[:, k * cols:(k + 1) * cols]],
                                          axis=0) for k in range(N_CHIPS)], axis=0)
        bufs = [wide, d_wd.reshape(N_CHIPS, cols, d_wd.shape[1])]
        if ride:
            self.queue.append([0, bufs, i])
        else:
            self.held[i] = bufs

    def step(self):
        if not self.queue:
            return None
        stage, bufs, _ = self.queue[0]
        return (swap_step, scatter_step, join_step)[stage](bufs)

    def absorb(self, results):
        entry = self.queue[0]
        stage, bufs, i = entry
        if stage == 0:
            entry[1] = [add_cores(g, t, f"ffn{i}_add_cores_{b}") for b, (g, t) in enumerate(zip(bufs, results))]
        elif stage == 1:
            entry[1] = [add_chips(s, f"ffn{i}_add_chips_{b}") for b, s in enumerate(results)]
        else:
            self.totals[i] = list(results)
            self.queue.pop(0)
        entry[0] = stage + 1

    def reduce(self, grads, shard_shapes):
        assert not self.queue
        narrow = [n for n in W_NAMES if n in SHARD_AXIS and n in grads]
        repl = [n for n in W_NAMES if n not in SHARD_AXIS]
        full = {n: (jnp.stack(g, axis=0) if isinstance(g, list) else g) for n, g in grads.items()}
        blocks = []
        for k in range(N_CHIPS):
            parts = [jnp.split(full[n], N_CHIPS, axis=SHARD_AXIS[n])[k] for n in narrow]
            parts += [full[n] for n in repl]
            blocks.append(_pack(parts))
        held = sorted(self.held)
        gs = [b for i in held for b in self.held[i]] + [jnp.stack(blocks, axis=0)]
        theirs = _exchange(swap_step(gs), gs, "grads_swap_halves")
        sums = [add_cores(g, t, f"grads_add_cores_{b}") for b, (g, t) in enumerate(zip(gs, theirs))]
        slots = _exchange(scatter_step(sums), sums, "grads_scatter")
        halves = [add_chips(s, f"grads_add_chips_{b}") for b, s in enumerate(slots)]
        totals = _exchange(join_step(halves), halves, "grads_join_halves")
        for n, i in enumerate(held):
            self.totals[i] = totals[2 * n:2 * n + 2]
        shapes = [shard_shapes[n] for n in narrow] + [full[n].shape for n in repl]
        out = dict(zip(narrow + repl, _unpack(totals[-1], shapes)))
        d = self.totals[0][0].shape[0] // 2
        for w, which in enumerate(('ffn1', 'ffn2')):
            mine = [self.totals[i] for i in sorted(self.totals) if i % 2 == w]
            out[f'{which}_w_gate'] = jnp.stack([t[0][:d] for t in mine], axis=0)
            out[f'{which}_w_up'] = jnp.stack([t[0][d:] for t in mine], axis=0)
            out[f'{which}_w_down'] = jnp.stack([t[1] for t in mine], axis=0)
        return out


def _ffn_fwd(x, gain, stream, i, tag):
    wg, wu, wd = stream.ffn(i)
    h = rms_fwd(x, gain, tag + "_norm")
    ici = stream.hosted(i + 1, 0)
    if ici is None:
        g, u, a = ffn_up(h, wg, wu, tag + "_up")
        xo = mm([(a, 0, wd, a.shape[1], 0)], res=x, scale=0.5, name=tag + "_down")
    else:
        (g, u, a), gathered = ffn_up(h, wg, wu, tag + "_up", hosted=ici)
        stream.absorb(gathered)
        xo, gathered = mm([(a, 0, wd, a.shape[1], 0)], res=x, scale=0.5, name=tag + "_down",
                          hosted=stream.hosted(i + 1, 1))
        stream.absorb(gathered)
    return xo, (x, h, g, u, a, wg, wu, wd)


def _carrying(sink, kernel, *args, **kwargs):
    step = sink.step()
    if step is None:
        return kernel(*args, **kwargs)
    out, results = kernel(*args, hosted=step, **kwargs)
    sink.absorb(results)
    return out


def _ffn_bwd(dxo, saved, gain, sink, i, tag):
    x, h, g, u, a, wg, wu, wd = saved
    Fd = a.shape[1]
    dxo, dxo_b = dxo
    dgp, du = _carrying(sink, ffn_down_bwd, dxo_b, wd, g, u, tag + "_down_bwd")
    d_wd = _carrying(sink, mm_tn, a, dxo_b, scale=0.5, name=tag + "_dwd")
    d_wg = _carrying(sink, mm_tn, h, dgp, name=tag + "_dwg")
    d_wu = _carrying(sink, mm_tn, h, du, name=tag + "_dwu")
    sink.add(i, d_wg, d_wu, d_wd, ride=i > 0)
    return _carrying(sink, mm, [(dgp, 0, wg, Fd, 0), (du, 0, wu, Fd, 0)], tb=True, res=dxo, norm_bwd=(x, gain),
                     name=tag + "_up_bwd")


def local_step(x, mem, W, stream, sink, target):
    D = x.shape[1]
    depth = W['ffn1_norm'].shape[0]
    n_a = W['a_w_in'].shape[0]
    G = {}
    mem_h = rms_fwd(mem, W['mem_norm'], "mem_norm")
    b_mats = jnp.broadcast_to(W['a_b_spatial'][..., None], W['a_b_spatial'].shape + (GM_CHUNK,))
    w_kv = (W['w_kv'], None)

    def mixer_w(l):
        if l < n_a:
            return (W['a_w_in'], l), (W['a_w_out'], l), 2 * GM_W
        return (W['b_w_in'], l - n_a), (W['b_w_out'], l - n_a), GM_W

    saved = []
    kv = kvn = x_kv = None
    for l in range(depth):
        if l == n_a:
            x_kv = x
            kvn = rms_fwd(x, W['kv_norm'], "kv_norm")
            kv = mm([(kvn, 0, w_kv, D, 0)], out_dtype=BF16, name="kv_proj")
        x, s1 = _ffn_fwd(x, W['ffn1_norm'][l], stream, 2 * l, f"l{l}_ffn1")
        hm = rms_fwd(x, W['mix_norm'][l], f"l{l}_mix_norm")
        mem_kv = mm([(mem_h, 0, (W['w_mem_kv'], l), D, 0)], name=f"l{l}_mem_kv")
        w_in, w_out, tok_w = mixer_w(l)
        proj = mm([(hm, 0, w_in, D, 0)], name=f"l{l}_mix_in")
        if l < n_a:
            y_tok = gmlp_fwd(proj, W['a_v_norm'][l], W['a_w_spatial'][l], b_mats[l], f"l{l}_gmlp")
        else:
            y_tok = sb_fwd(proj, kv, f"l{l}_sb")
        q_cb = tok_w // MEM_W
        y_mem = memattn_fwd(proj, q_cb, mem_kv, f"l{l}_memattn")
        x_mid = x
        x = mm([(y_tok, 0, w_out, GM_W, 0), (y_mem, 0, w_out, MEM_W, GM_W // MEM_W)], res=x, name=f"l{l}_mix_out")
        sm = (x_mid, hm, mem_kv, proj, y_tok, y_mem, q_cb)
        x, s2 = _ffn_fwd(x, W['ffn2_norm'][l], stream, 2 * l + 1, f"l{l}_ffn2")
        saved.append((s1, sm, s2))

    loss, dx, d_final = final_loss(x, W['final_norm'], target, "loss_head")
    G['final_norm'] = d_final.reshape(-1)

    per_layer = {n: [None] * depth for n in ['ffn1_norm', 'mix_norm', 'ffn2_norm', 'w_mem_kv']}
    per_a = {n: [None] * n_a for n in ['a_w_in', 'a_v_norm', 'a_w_spatial', 'a_b_spatial', 'a_w_out']}
    per_b = {n: [None] * (depth - n_a) for n in ['b_w_in', 'b_w_out']}
    d_mem_h = None
    dk = dv = None
    for l in reversed(range(depth)):
        s1, sm, s2 = saved[l]
        dx, dg = _ffn_bwd(dx, s2, W['ffn2_norm'][l], sink, 2 * l + 1, f"l{l}_ffn2")
        per_layer['ffn2_norm'][l] = dg.reshape(-1)

        x_mid, hm, mem_kv, proj, y_tok, y_mem, q_cb = sm
        is_a = l < n_a
        w_in, w_out, tok_w = mixer_w(l)
        dy = mm([(dx[1], 0, w_out, D, 0)], tb=True, name=f"l{l}_mix_out_bwd")
        d_w_out = jnp.concatenate([mm_tn(y_tok, dx[1], name=f"l{l}_dwout_tok"),
                                   mm_tn(y_mem, dx[1], name=f"l{l}_dwout_mem")], axis=0)
        dq_mem, d_mem_kv = memattn_bwd(proj, q_cb, mem_kv, dy, GM_W // MEM_W, f"l{l}_memattn_bwd")
        if is_a:
            d_tok, d_ws, d_bs, d_vg = gmlp_bwd(proj, dy, W['a_v_norm'][l], W['a_w_spatial'][l], b_mats[l],
                                               f"l{l}_gmlp_bwd")
            per_a['a_w_spatial'][l], per_a['a_b_spatial'][l] = d_ws, d_bs[:, :, 0]
            per_a['a_v_norm'][l], per_a['a_w_out'][l] = d_vg.reshape(-1), d_w_out
        else:
            d_tok, dk, dv = sb_bwd(proj, kv, dy, y_tok, dk, dv, f"l{l}_sb_bwd")
            per_b['b_w_out'][l - n_a] = d_w_out
        d_w_in = jnp.concatenate([mm_tn(hm, d_tok, name=f"l{l}_dwin_tok"),
                                  mm_tn(hm, dq_mem, name=f"l{l}_dwin_mem")], axis=1)
        (per_a['a_w_in'] if is_a else per_b['b_w_in'])[l if is_a else l - n_a] = d_w_in
        dx, dg = mm([(d_tok, 0, w_in, tok_w, 0), (dq_mem, 0, w_in, MEM_W, tok_w // MEM_W)], tb=True, res=dx[0],
                    norm_bwd=(x_mid, W['mix_norm'][l]), name=f"l{l}_mix_in_bwd")
        per_layer['mix_norm'][l] = dg.reshape(-1)
        per_layer['w_mem_kv'][l] = mm_tn(mem_h, d_mem_kv, name=f"l{l}_dw_mem_kv")
        d_mem_h = mm([(d_mem_kv, 0, (W['w_mem_kv'], l), 2 * MEM_W, 0)], tb=True, res=d_mem_h,
                     name=f"l{l}_mem_kv_bwd")

        dx, dg = _ffn_bwd(dx, s1, W['ffn1_norm'][l], sink, 2 * l, f"l{l}_ffn1")
        per_layer['ffn1_norm'][l] = dg.reshape(-1)
        if l == n_a:
            G['w_kv'] = jnp.concatenate([mm_tn(kvn, dk, name="dw_k"), mm_tn(kvn, dv, name="dw_v")], axis=1)
            dx, dg = mm([(dk, 0, w_kv, GM_W, 0), (dv, 0, w_kv, GM_W, 1)], tb=True, res=dx[0],
                        norm_bwd=(x_kv, W['kv_norm']), name="kv_proj_bwd")
            G['kv_norm'] = dg.reshape(-1)

    G['mem_norm'] = rms_gain_grad(mem, d_mem_h, "mem_norm_bwd").reshape(-1)
    for d in (per_layer, per_a, per_b):
        G.update(d)
    return loss, dx[0], G


def kernel(x, mem, ffn1_norm, ffn1_w_gate, ffn1_w_up, ffn1_w_down, mix_norm, ffn2_norm, ffn2_w_gate, ffn2_w_up, ffn2_w_down, mem_norm, w_mem_kv, a_w_in, a_v_norm, a_w_spatial, a_b_spatial, a_w_out, kv_norm, w_kv, b_w_in, b_w_out, final_norm, loss_target, m_ffn1_norm, m_ffn1_w_gate, m_ffn1_w_up, m_ffn1_w_down, m_mix_norm, m_ffn2_norm, m_ffn2_w_gate, m_ffn2_w_up, m_ffn2_w_down, m_mem_norm, m_w_mem_kv, m_a_w_in, m_a_v_norm, m_a_w_spatial, m_a_b_spatial, m_a_w_out, m_kv_norm, m_w_kv, m_b_w_in, m_b_w_out, m_final_norm, v_ffn1_norm, v_ffn1_w_gate, v_ffn1_w_up, v_ffn1_w_down, v_mix_norm, v_ffn2_norm, v_ffn2_w_gate, v_ffn2_w_up, v_ffn2_w_down, v_mem_norm, v_w_mem_kv, v_a_w_in, v_a_v_norm, v_a_w_spatial, v_a_b_spatial, v_a_w_out, v_kv_norm, v_w_kv, v_b_w_in, v_b_w_out, v_final_norm):
    weights = dict(zip(W_NAMES, [ffn1_norm, ffn1_w_gate, ffn1_w_up, ffn1_w_down, mix_norm, ffn2_norm, ffn2_w_gate,
                                 ffn2_w_up, ffn2_w_down, mem_norm, w_mem_kv, a_w_in, a_v_norm, a_w_spatial,
                                 a_b_spatial, a_w_out, kv_norm, w_kv, b_w_in, b_w_out, final_norm]))
    m_in = dict(zip(W_NAMES, [m_ffn1_norm, m_ffn1_w_gate, m_ffn1_w_up, m_ffn1_w_down, m_mix_norm, m_ffn2_norm,
                              m_ffn2_w_gate, m_ffn2_w_up, m_ffn2_w_down, m_mem_norm, m_w_mem_kv, m_a_w_in,
                              m_a_v_norm, m_a_w_spatial, m_a_b_spatial, m_a_w_out, m_kv_norm, m_w_kv, m_b_w_in,
                              m_b_w_out, m_final_norm]))
    v_in = dict(zip(W_NAMES, [v_ffn1_norm, v_ffn1_w_gate, v_ffn1_w_up, v_ffn1_w_down, v_mix_norm, v_ffn2_norm,
                              v_ffn2_w_gate, v_ffn2_w_up, v_ffn2_w_down, v_mem_norm, v_w_mem_kv, v_a_w_in,
                              v_a_v_norm, v_a_w_spatial, v_a_b_spatial, v_a_w_out, v_kv_norm, v_w_kv, v_b_w_in,
                              v_b_w_out, v_final_norm]))

    stream = WeightStream({n: weights[n] for n in SHARD_AXIS})
    W = {n: weights[n] for n in W_NAMES if n not in SHARD_AXIS}
    W.update(stream.rest())
    sink = GradSink()
    loss, dx, grads = local_step(x[0], mem[0], W, stream, sink, loss_target[0])
    total = sink.reduce(grads, {n: weights[n].shape for n in SHARD_AXIS})
    loss = lax.psum(loss[0, 0], ("x", "y", "c"))

    deltas, new_m, new_v = {}, {}, {}
    for n in W_NAMES:
        deltas[n], new_m[n], new_v[n] = adamw(weights[n], total[n], m_in[n], v_in[n], "adamw_" + n)
    return (loss, dx[None], *[total[n] for n in W_NAMES], *[deltas[n] for n in W_NAMES],
            *[new_m[n] for n in W_NAMES], *[new_v[n] for n in W_NAMES])
```

```python
import functools
import math

import jax
import jax.numpy as jnp
from jax import lax
from jax.experimental import pallas as pl
from jax.experimental.pallas import tpu as pltpu

F32 = jnp.float32
BF16 = jnp.bfloat16
EPS = 1e-6
LANES = 128
ROW_TILE = 512
COL_TILE = 1408
PACK_ROWS = 256
PART_ROWS = 16
VMEM_LIMIT = 56 * 1024 * 1024

W_NAMES = ['ffn1_norm', 'ffn1_w_gate', 'ffn1_w_up', 'ffn1_w_down', 'mix_norm', 'ffn2_norm', 'ffn2_w_gate',
           'ffn2_w_up', 'ffn2_w_down', 'mem_norm', 'w_mem_kv', 'a_w_in', 'a_v_norm', 'a_w_spatial',
           'a_b_spatial', 'a_w_out', 'kv_norm', 'w_kv', 'b_w_in', 'b_w_out', 'final_norm']
SHARD_AXIS = {'ffn1_w_gate': 2, 'ffn1_w_up': 2, 'ffn1_w_down': 1, 'ffn2_w_gate': 2, 'ffn2_w_up': 2,
              'ffn2_w_down': 1, 'w_mem_kv': 1, 'a_w_in': 2, 'a_v_norm': 1, 'a_w_out': 1, 'w_kv': 1,
              'b_w_in': 1, 'b_w_out': 1}
N_CHIPS = 4
PACK_W = 1024

MEM_HEADS = 4
MEM_W = 256
HEAD_DIM = 64
GM_W = 768
GM_GROUPS = 6
GM_CHUNK = 128
CHUNK = 64
SB_BLOCK = 128
SB_Q_CHUNK = 1024
SB_DEAD = -110.0
SB_PAIRS = 2
SB_COLS = SB_PAIRS * LANES

ADAM_LR, ADAM_B1, ADAM_B2, ADAM_EPS, ADAM_WD, ADAM_STEP = 0.001, 0.9, 0.999, 1e-08, 0.01, 10

MESH = pl.DeviceIdType.MESH


def _cparams(sem=None):
    return pltpu.CompilerParams(dimension_semantics=sem, vmem_limit_bytes=VMEM_LIMIT)


class Hosted:
    def __init__(self, arrays, n_out, sems, make, fresh=()):
        self.arrays, self.n_out, self.sems, self.make, self.fresh = arrays, n_out, sems, make, list(fresh)


def _call(body, *, grid, in_specs, out_specs, out_shape, ins, semantics, name, hosted=None):
    if hosted is None:
        return pl.pallas_call(body, grid=grid, in_specs=in_specs, out_specs=out_specs, out_shape=out_shape,
                              compiler_params=_cparams(semantics), name=name)(*ins)
    n_in, n_out = len(ins), len(out_shape)
    n_h, n_alias = len(hosted.arrays), hosted.n_out
    n_ho = n_alias + len(hosted.fresh)

    def hosting_body(*refs):
        base_in, h_in = refs[:n_in], refs[n_in:n_in + n_h]
        base_out = refs[n_in + n_h:n_in + n_h + n_out]
        h_out = refs[n_in + n_h + n_out:n_in + n_h + n_out + n_ho]
        sems = refs[n_in + n_h + n_out + n_ho:]
        copies, waits = hosted.make(h_in, h_out, *sems)
        first = last = None
        for axis, extent in enumerate(grid):
            at_start, at_end = pl.program_id(axis) == 0, pl.program_id(axis) == extent - 1
            first = at_start if first is None else jnp.logical_and(first, at_start)
            last = at_end if last is None else jnp.logical_and(last, at_end)

        @pl.when(first)
        def _():
            for cp in copies:
                cp.start()

        body(*base_in, *base_out)

        @pl.when(last)
        def _():
            for wait in waits:
                wait()

    any_spec = pl.BlockSpec(memory_space=pl.ANY)
    outs = pl.pallas_call(
        hosting_body, grid=grid, in_specs=list(in_specs) + [any_spec] * n_h,
        out_specs=list(out_specs) + [any_spec] * n_ho,
        out_shape=(list(out_shape) + [jax.ShapeDtypeStruct(a.shape, a.dtype) for a in hosted.arrays[:n_alias]]
                   + hosted.fresh),
        scratch_shapes=[pltpu.SemaphoreType.DMA((n,)) for n in hosted.sems],
        input_output_aliases={n_in + k: n_out + k for k in range(n_alias)},
        compiler_params=_cparams(("arbitrary",) * len(grid)), name=name)(*ins, *hosted.arrays)
    return outs[:n_out], outs[n_out:]


def _tile(n, target, mult=LANES):
    best = None
    for t in range(mult, min(n, target) + 1, mult):
        if n % t == 0:
            best = t
    return best if best is not None else n


def _dot(a, b):
    return jnp.dot(a, b, preferred_element_type=F32)


def _dot_nt(a, b):
    return lax.dot_general(a, b, (((1,), (1,)), ((), ())), preferred_element_type=F32)


def _dot_tn(a, b):
    return lax.dot_general(a, b, (((0,), (0,)), ((), ())), preferred_element_type=F32)


def _bf(v):
    return v.astype(BF16)


def rms_fwd(x, gain, name):
    S, D = x.shape
    tm = _tile(S, ROW_TILE, 8)

    def body(x_ref, g_ref, o_ref):
        xf = x_ref[...]
        r = lax.rsqrt(jnp.mean(xf * xf, axis=-1, keepdims=True) + EPS)
        o_ref[...] = ((xf * r) * g_ref[...]).astype(o_ref.dtype)

    return pl.pallas_call(
        body, grid=(S // tm,),
        in_specs=[pl.BlockSpec((tm, D), lambda i: (i, 0)), pl.BlockSpec((1, D), lambda i: (0, 0))],
        out_specs=pl.BlockSpec((tm, D), lambda i: (i, 0)),
        out_shape=jax.ShapeDtypeStruct((S, D), BF16),
        compiler_params=_cparams(("parallel",)), name=name)(x, gain.reshape(1, D))


def rms_gain_grad(x, dh, name):
    S, D = x.shape
    tm = _tile(S, ROW_TILE, 8)

    def body(x_ref, dh_ref, dg_ref):
        xf = x_ref[...]
        r = lax.rsqrt(jnp.mean(xf * xf, axis=-1, keepdims=True) + EPS)

        @pl.when(pl.program_id(0) == 0)
        def _():
            dg_ref[...] = jnp.zeros_like(dg_ref)

        dg_ref[...] += jnp.sum(dh_ref[...] * (xf * r), axis=0, keepdims=True)

    row = pl.BlockSpec((tm, D), lambda i: (i, 0))
    return pl.pallas_call(
        body, grid=(S // tm,), in_specs=[row, row], out_specs=pl.BlockSpec((1, D), lambda i: (0, 0)),
        out_shape=jax.ShapeDtypeStruct((1, D), F32),
        compiler_params=_cparams(("arbitrary",)), name=name)(x, dh)


def final_loss(x, gain, target, name):
    S, D = x.shape
    tm = _tile(S, ROW_TILE, 16)

    def body(x_ref, g_ref, t_ref, loss_ref, dx_ref, dxb_ref, dg_ref):
        xf = x_ref[...]
        r = lax.rsqrt(jnp.mean(xf * xf, axis=-1, keepdims=True) + EPS)
        xhat = xf * r
        g = g_ref[...]
        diff = xhat * g - t_ref[...]
        dy = diff * (1.0 / D)
        dxhat = dy * g
        m = jnp.mean(dxhat * xhat, axis=-1, keepdims=True)
        dx = r * (dxhat - xhat * m)
        dx_ref[...] = dx
        dxb_ref[...] = _bf(dx)

        @pl.when(pl.program_id(0) == 0)
        def _():
            dg_ref[...] = jnp.zeros_like(dg_ref)
            loss_ref[...] = jnp.zeros_like(loss_ref)

        dg_ref[...] += jnp.sum(dy * xhat, axis=0, keepdims=True)
        per_tok = jnp.sum(diff * diff, axis=-1, keepdims=True) * (0.5 / D)
        loss_ref[...] += jnp.sum(per_tok, axis=0, keepdims=True)

    row = pl.BlockSpec((tm, D), lambda i: (i, 0))
    vec = pl.BlockSpec((1, D), lambda i: (0, 0))
    one = pl.BlockSpec((1, 1), lambda i: (0, 0))
    loss, dx, dxb, dg = pl.pallas_call(
        body, grid=(S // tm,), in_specs=[row, vec, row], out_specs=[one, row, row, vec],
        out_shape=[jax.ShapeDtypeStruct((1, 1), F32), jax.ShapeDtypeStruct((S, D), F32),
                   jax.ShapeDtypeStruct((S, D), BF16), jax.ShapeDtypeStruct((1, D), F32)],
        compiler_params=_cparams(("arbitrary",)), name=name)(x, gain.reshape(1, D), target)
    return loss, (dx, dxb), dg


def _wspec(w, blk, idx):
    _, l = w
    if l is None:
        return pl.BlockSpec(blk, idx)
    return pl.BlockSpec((None,) + blk, lambda *g: (l,) + idx(*g))


def mm(pairs, *, tb=False, res=None, scale=1.0, out_dtype=F32, norm_bwd=None, norm_next=None, hosted=None, name):
    M = pairs[0][0].shape[0]
    N = pairs[0][2][0].shape[-2 if tb else -1]
    tm = _tile(M, ROW_TILE, 16)
    tn = N if (norm_bwd is not None or norm_next is not None) else _tile(N, COL_TILE)
    n_p = len(pairs)
    has_res = res is not None
    n_in = 2 * n_p + has_res + (2 if norm_bwd is not None else 0) + (norm_next is not None)

    def body(*refs):
        acc = None
        for p in range(n_p):
            a = _bf(refs[2 * p][...])
            b = _bf(refs[2 * p + 1][...])
            d = _dot_nt(a, b) if tb else _dot(a, b)
            acc = d if acc is None else acc + d
        if scale != 1.0:
            acc = acc * scale
        if norm_bwd is None:
            if has_res:
                acc = acc + refs[2 * n_p][...]
            refs[n_in][...] = acc.astype(refs[n_in].dtype)
            if norm_next is not None:
                r = lax.rsqrt(jnp.mean(acc * acc, axis=-1, keepdims=True) + EPS)
                refs[n_in + 1][...] = _bf((acc * r) * refs[n_in - 1][...])
            return
        x_ref, g_ref = refs[n_in - 2], refs[n_in - 1]
        dx_ref, dxb_ref, dg_ref = refs[n_in:]
        xf = x_ref[...]
        r = lax.rsqrt(jnp.mean(xf * xf, axis=-1, keepdims=True) + EPS)
        xhat = xf * r
        dxhat = acc * g_ref[...]
        m = jnp.mean(dxhat * xhat, axis=-1, keepdims=True)
        dx = r * (dxhat - xhat * m)
        if has_res:
            dx = dx + refs[2 * n_p][...]
        dx_ref[...] = dx
        dxb_ref[...] = _bf(dx)

        @pl.when(pl.program_id(1) == 0)
        def _():
            dg_ref[...] = jnp.zeros_like(dg_ref)

        dg_ref[...] += jnp.sum(acc * xhat, axis=0, keepdims=True)

    ins, in_specs = [], []
    for a, cb, w, K, kb in pairs:
        ins += [a, w[0]]
        in_specs.append(pl.BlockSpec((tm, K), functools.partial(lambda j, i, cb: (i, cb), cb=cb)))
        if tb:
            in_specs.append(_wspec(w, (tn, K), functools.partial(lambda j, i, kb: (j, kb), kb=kb)))
        else:
            in_specs.append(_wspec(w, (K, tn), functools.partial(lambda j, i, kb: (kb, j), kb=kb)))
    tile = pl.BlockSpec((tm, tn), lambda j, i: (i, j))
    if has_res:
        ins.append(res)
        in_specs.append(tile)
    if norm_bwd is None:
        out_shape = [jax.ShapeDtypeStruct((M, N), out_dtype)]
        if norm_next is not None:
            ins.append(norm_next.reshape(1, N))
            in_specs.append(pl.BlockSpec((1, N), lambda j, i: (0, 0)))
            out_shape.append(jax.ShapeDtypeStruct((M, N), BF16))
        out = _call(body, grid=(N // tn, M // tm), in_specs=in_specs, out_specs=[tile] * len(out_shape),
                    out_shape=out_shape, ins=ins, semantics=("parallel", "parallel"), name=name, hosted=hosted)
        base = out if hosted is None else out[0]
        value = base[0] if norm_next is None else (base[0], base[1])
        return value if hosted is None else (value, out[1])
    x, gain = norm_bwd
    vec = pl.BlockSpec((1, N), lambda j, i: (0, 0))
    out = _call(
        body, grid=(1, M // tm), in_specs=in_specs + [tile, vec], out_specs=[tile, tile, vec],
        out_shape=[jax.ShapeDtypeStruct((M, N), F32), jax.ShapeDtypeStruct((M, N), BF16),
                   jax.ShapeDtypeStruct((1, N), F32)], ins=ins + [x, gain.reshape(1, N)],
        semantics=("arbitrary", "arbitrary"), name=name, hosted=hosted)
    dx, dxb, dg = out if hosted is None else out[0]
    return ((dx, dxb), dg) if hosted is None else (((dx, dxb), dg), out[1])


def mm_tn(a, b, *, a_cb=0, a_w=None, b_cb=0, b_w=None, scale=1.0, hosted=None, name):
    S = a.shape[0]
    Ka = a.shape[1] if a_w is None else a_w
    Nb = b.shape[1] if b_w is None else b_w
    ts = _tile(S, 2 * ROW_TILE, 8)
    tk = _tile(Ka, COL_TILE)
    tn = _tile(Nb, COL_TILE)
    a_off, b_off = a_cb * (Ka // tk), b_cb * (Nb // tn)

    def body(a_ref, b_ref, o_ref):
        s = pl.program_id(2)

        @pl.when(s == 0)
        def _():
            o_ref[...] = jnp.zeros_like(o_ref)

        o_ref[...] += _dot_tn(_bf(a_ref[...]), _bf(b_ref[...]))
        if scale != 1.0:
            @pl.when(s == pl.num_programs(2) - 1)
            def _():
                o_ref[...] = o_ref[...] * scale

    out = _call(
        body, grid=(Ka // tk, Nb // tn, S // ts),
        in_specs=[pl.BlockSpec((ts, tk), lambda k, n, s: (s, a_off + k)),
                  pl.BlockSpec((ts, tn), lambda k, n, s: (s, b_off + n))],
        out_specs=[pl.BlockSpec((tk, tn), lambda k, n, s: (k, n))],
        out_shape=[jax.ShapeDtypeStruct((Ka, Nb), F32)], ins=[a, b],
        semantics=("parallel", "parallel", "arbitrary"), name=name, hosted=hosted)
    return out[0] if hosted is None else (out[0][0], out[1])


def _sigmoid(x):
    return 0.5 * jnp.tanh(0.5 * x) + 0.5


def ffn_up(h, wg, wu, name, hosted=None):
    S, D = h.shape
    Fd = wg[0].shape[-1]
    tm = _tile(S, ROW_TILE, 8)
    tn = _tile(Fd, COL_TILE)

    def body(h_ref, wg_ref, wu_ref, g_ref, u_ref, a_ref):
        hv = h_ref[...]
        g = _dot(hv, wg_ref[...])
        u = _dot(hv, wu_ref[...])
        g_ref[...] = _bf(g)
        u_ref[...] = _bf(u)
        a_ref[...] = _bf(g * _sigmoid(g) * u)

    blk = pl.BlockSpec((tm, tn), lambda j, i: (i, j))
    sh = jax.ShapeDtypeStruct((S, Fd), BF16)
    return _call(
        body, grid=(Fd // tn, S // tm),
        in_specs=[pl.BlockSpec((tm, D), lambda j, i: (i, 0)), _wspec(wg, (D, tn), lambda j, i: (0, j)),
                  _wspec(wu, (D, tn), lambda j, i: (0, j))],
        out_specs=[blk, blk, blk], out_shape=[sh, sh, sh], ins=[h, wg[0], wu[0]],
        semantics=("parallel", "parallel"), name=name, hosted=hosted)


def ffn_down_bwd(dxo, wd, g, u, name, hosted=None):
    S, D = dxo.shape
    Fd = wd[0].shape[-2]
    tm = _tile(S, ROW_TILE, 8)
    tn = _tile(Fd, COL_TILE)

    def body(dx_ref, wd_ref, g_ref, u_ref, dg_ref, du_ref):
        da = _dot_nt(_bf(dx_ref[...]), wd_ref[...]) * 0.5
        g = g_ref[...].astype(F32)
        u = u_ref[...].astype(F32)
        sg = _sigmoid(g)
        dg_ref[...] = _bf(da * u * (sg * (1.0 + g * (1.0 - sg))))
        du_ref[...] = _bf(da * (g * sg))

    blk = pl.BlockSpec((tm, tn), lambda j, i: (i, j))
    sh = jax.ShapeDtypeStruct((S, Fd), BF16)
    return _call(
        body, grid=(Fd // tn, S // tm),
        in_specs=[pl.BlockSpec((tm, D), lambda j, i: (i, 0)), _wspec(wd, (tn, D), lambda j, i: (j, 0)),
                  blk, blk],
        out_specs=[blk, blk], out_shape=[sh, sh], ins=[dxo, wd[0], g, u],
        semantics=("parallel", "parallel"), name=name, hosted=hosted)


_GELU_C = math.sqrt(2.0 / math.pi)


def _gelu(p):
    return 0.5 * p * (1.0 + jnp.tanh(_GELU_C * (p + 0.044715 * (p * p * p))))


def _gelu_grad(p):
    th = jnp.tanh(_GELU_C * (p + 0.044715 * (p * p * p)))
    return 0.5 * (1.0 + th) + 0.5 * p * (1.0 - th * th) * (_GELU_C * (1.0 + 3.0 * 0.044715 * (p * p)))


def _chunk_mask():
    t = lax.broadcasted_iota(jnp.int32, (GM_CHUNK, GM_CHUNK), 0)
    s = lax.broadcasted_iota(jnp.int32, (GM_CHUNK, GM_CHUNK), 1)
    return (s // CHUNK) <= (t // CHUNK)


def gmlp_fwd(proj, v_gain, w_s, b_mat, name):
    S = proj.shape[0]
    tw = _tile(S, ROW_TILE, GM_CHUNK)
    n_win = tw // GM_CHUNK

    def body(p_ref, gain_ref, w_ref, b_ref, y_ref):
        mask = _chunk_mask()
        v = _gelu(p_ref[:, GM_W:])
        r = lax.rsqrt(jnp.mean(v * v, axis=-1, keepdims=True) + EPS)
        vn = _bf(v * r * gain_ref[...])
        for g in range(GM_GROUPS):
            wm = _bf(jnp.where(mask, w_ref[g], 0.0))
            cs = slice(g * GM_CHUNK, (g + 1) * GM_CHUNK)
            for w in range(n_win):
                rs = slice(w * GM_CHUNK, (w + 1) * GM_CHUNK)
                mixed = _dot(wm, vn[rs, cs]) + b_ref[g]
                y_ref[rs, cs] = _bf(_gelu(p_ref[rs, cs]) * mixed)

    return pl.pallas_call(
        body, grid=(S // tw,),
        in_specs=[pl.BlockSpec((tw, 2 * GM_W), lambda i: (i, 0)), pl.BlockSpec((1, GM_W), lambda i: (0, 0)),
                  pl.BlockSpec((GM_GROUPS, GM_CHUNK, GM_CHUNK), lambda i: (0, 0, 0)),
                  pl.BlockSpec((GM_GROUPS, GM_CHUNK, GM_CHUNK), lambda i: (0, 0, 0))],
        out_specs=pl.BlockSpec((tw, GM_W), lambda i: (i, 0)),
        out_shape=jax.ShapeDtypeStruct((S, GM_W), BF16),
        compiler_params=_cparams(("parallel",)), name=name)(proj, v_gain.reshape(1, GM_W), w_s, b_mat)


def gmlp_bwd(proj, dy, v_gain, w_s, b_mat, name):
    S = proj.shape[0]
    tw = _tile(S, ROW_TILE, GM_CHUNK)
    n_win = tw // GM_CHUNK

    def body(p_ref, dy_ref, gain_ref, w_ref, b_ref, dp_ref, dw_ref, db_ref, dgain_ref, dvn_ref):
        step = pl.program_id(0)

        @pl.when(step == 0)
        def _():
            dw_ref[...] = jnp.zeros_like(dw_ref)
            db_ref[...] = jnp.zeros_like(db_ref)
            dgain_ref[...] = jnp.zeros_like(dgain_ref)

        mask = _chunk_mask()
        pv = p_ref[:, GM_W:]
        v = _gelu(pv)
        r = lax.rsqrt(jnp.mean(v * v, axis=-1, keepdims=True) + EPS)
        vhat = v * r
        gain = gain_ref[...]
        vn = _bf(vhat * gain)
        for g in range(GM_GROUPS):
            wm = _bf(jnp.where(mask, w_ref[g], 0.0))
            cs = slice(g * GM_CHUNK, (g + 1) * GM_CHUNK)
            dw_acc = jnp.zeros((GM_CHUNK, GM_CHUNK), F32)
            db_acc = jnp.zeros((GM_CHUNK, GM_CHUNK), F32)
            for w in range(n_win):
                rs = slice(w * GM_CHUNK, (w + 1) * GM_CHUNK)
                pu = p_ref[rs, cs]
                u = _gelu(pu)
                vn_blk = vn[rs, cs]
                mixed = _dot(wm, vn_blk) + b_ref[g]
                dyb = dy_ref[rs, cs]
                dp_ref[rs, cs] = _bf(dyb * mixed * _gelu_grad(pu))
                dmix = dyb * u
                db_acc = db_acc + dmix
                dmix_b = _bf(dmix)
                dw_acc = dw_acc + _dot_nt(dmix_b, vn_blk)
                dvn_ref[rs, cs] = _dot_tn(wm, dmix_b)
            dw_ref[g] += jnp.where(mask, dw_acc, 0.0)
            db_ref[g] += jnp.broadcast_to(jnp.sum(db_acc, axis=-1, keepdims=True), (GM_CHUNK, GM_CHUNK))
        dvn = dvn_ref[...]
        dgain_ref[...] += jnp.sum(dvn * vhat, axis=0, keepdims=True)
        dvhat = dvn * gain
        m = jnp.mean(dvhat * vhat, axis=-1, keepdims=True)
        dv = r * (dvhat - vhat * m)
        dp_ref[:, GM_W:] = _bf(dv * _gelu_grad(pv))

    sq = pl.BlockSpec((GM_GROUPS, GM_CHUNK, GM_CHUNK), lambda i: (0, 0, 0))
    sq_sh = jax.ShapeDtypeStruct((GM_GROUPS, GM_CHUNK, GM_CHUNK), F32)
    return pl.pallas_call(
        body, grid=(S // tw,),
        in_specs=[pl.BlockSpec((tw, 2 * GM_W), lambda i: (i, 0)), pl.BlockSpec((tw, GM_W), lambda i: (i, 0)),
                  pl.BlockSpec((1, GM_W), lambda i: (0, 0)), sq, sq],
        out_specs=[pl.BlockSpec((tw, 2 * GM_W), lambda i: (i, 0)), sq, sq,
                   pl.BlockSpec((1, GM_W), lambda i: (0, 0))],
        out_shape=[jax.ShapeDtypeStruct((S, 2 * GM_W), BF16), sq_sh, sq_sh,
                   jax.ShapeDtypeStruct((1, GM_W), F32)],
        scratch_shapes=[pltpu.VMEM((tw, GM_W), F32)],
        compiler_params=_cparams(("arbitrary",)), name=name)(proj, dy, v_gain.reshape(1, GM_W), w_s, b_mat)


def _head_mask(h, width):
    lane = lax.broadcasted_iota(jnp.int32, (1, width), 1)
    return (lane >= HEAD_DIM * h) & (lane < HEAD_DIM * (h + 1))


def _mem_probs(q, k, h):
    kh = jnp.where(_head_mask(h, MEM_W), k, jnp.zeros_like(k))
    s = _dot_nt(q, kh) * (HEAD_DIM ** -0.5)
    s = s - jnp.max(s, axis=-1, keepdims=True)
    e = jnp.exp(s)
    return e * (1.0 / jnp.sum(e, axis=-1, keepdims=True)), kh


def memattn_fwd(proj, q_cb, mem_kv, name):
    S = proj.shape[0]
    NM = mem_kv.shape[0]
    tm = _tile(S, ROW_TILE, 8)

    def body(q_ref, kv_ref, o_ref):
        q = _bf(q_ref[...])
        k = _bf(kv_ref[:, :MEM_W])
        v = _bf(kv_ref[:, MEM_W:])
        acc = jnp.zeros((tm, MEM_W), F32)
        for h in range(MEM_HEADS):
            p, _ = _mem_probs(q, k, h)
            vh = jnp.where(_head_mask(h, MEM_W), v, jnp.zeros_like(v))
            acc = acc + _dot(_bf(p), vh)
        o_ref[...] = _bf(acc)

    return pl.pallas_call(
        body, grid=(S // tm,),
        in_specs=[pl.BlockSpec((tm, MEM_W), lambda i: (i, q_cb)), pl.BlockSpec((NM, 2 * MEM_W), lambda i: (0, 0))],
        out_specs=pl.BlockSpec((tm, MEM_W), lambda i: (i, 0)),
        out_shape=jax.ShapeDtypeStruct((S, MEM_W), BF16),
        compiler_params=_cparams(("parallel",)), name=name)(proj, mem_kv)


def memattn_bwd(proj, q_cb, mem_kv, dy, dy_cb, name):
    S = proj.shape[0]
    NM = mem_kv.shape[0]
    tm = _tile(S, ROW_TILE, 8)

    def body(q_ref, kv_ref, do_ref, dq_ref, dkv_ref):
        @pl.when(pl.program_id(0) == 0)
        def _():
            dkv_ref[...] = jnp.zeros_like(dkv_ref)

        q = _bf(q_ref[...])
        k = _bf(kv_ref[:, :MEM_W])
        v = _bf(kv_ref[:, MEM_W:])
        do = _bf(do_ref[...])
        dq = jnp.zeros((tm, MEM_W), F32)
        dk = jnp.zeros((NM, MEM_W), F32)
        dv = jnp.zeros((NM, MEM_W), F32)
        for h in range(MEM_HEADS):
            hm = _head_mask(h, MEM_W)
            p, kh = _mem_probs(q, k, h)
            vh = jnp.where(hm, v, jnp.zeros_like(v))
            dp = _dot_nt(do, vh)
            ds = _bf(p * (dp - jnp.sum(p * dp, axis=-1, keepdims=True)) * (HEAD_DIM ** -0.5))
            dq = dq + _dot(ds, kh)
            dk = dk + jnp.where(hm, _dot_tn(ds, q), 0.0)
            dv = dv + jnp.where(hm, _dot_tn(_bf(p), do), 0.0)
        dq_ref[...] = _bf(dq)
        dkv_ref[:, :MEM_W] += dk
        dkv_ref[:, MEM_W:] += dv

    return pl.pallas_call(
        body, grid=(S // tm,),
        in_specs=[pl.BlockSpec((tm, MEM_W), lambda i: (i, q_cb)), pl.BlockSpec((NM, 2 * MEM_W), lambda i: (0, 0)),
                  pl.BlockSpec((tm, MEM_W), lambda i: (i, dy_cb))],
        out_specs=[pl.BlockSpec((tm, MEM_W), lambda i: (i, 0)), pl.BlockSpec((NM, 2 * MEM_W), lambda i: (0, 0))],
        out_shape=[jax.ShapeDtypeStruct((S, MEM_W), BF16), jax.ShapeDtypeStruct((NM, 2 * MEM_W), F32)],
        compiler_params=_cparams(("arbitrary",)), name=name)(proj, mem_kv, dy)


def _split_bf(v):
    hi = _bf(v)
    return hi, _bf(v - hi.astype(F32))


def _sb_scores(qp, k2, dpos, row, col, tri):
    z = _dot_nt(qp, k2) * (HEAD_DIM ** -0.5)
    t = jnp.log(1.0 + jnp.exp(-jnp.abs(z)))
    lb = jnp.minimum(z, 0.0) - t
    lom = jnp.where(_sb_mask(dpos, row, col), -jnp.maximum(z, 0.0) - t, 0.0)
    l_hi, l_lo = _split_bf(lom)
    insuf = _dot(l_hi, tri) + _dot(l_lo, tri)
    return lb + insuf, jnp.sum(lom, axis=1, keepdims=True), lb


def _sb_mask(dpos, row, col):
    return (col - row) < dpos


def _sb_weights(s, run, dpos, row, col):
    return jnp.where(_sb_mask(dpos, row, col), jnp.exp(s + run), 0.0)


def _sb_alive(runs):
    top = runs[0]
    for r in runs[1:]:
        top = jnp.maximum(top, r)
    return jnp.max(top) > SB_DEAD


def _sb_consts():
    row = lax.broadcasted_iota(jnp.int32, (2 * SB_BLOCK, SB_BLOCK), 0) % SB_BLOCK
    col = lax.broadcasted_iota(jnp.int32, (2 * SB_BLOCK, SB_BLOCK), 1)
    r = lax.broadcasted_iota(jnp.int32, (SB_BLOCK, SB_BLOCK), 0)
    c = lax.broadcasted_iota(jnp.int32, (SB_BLOCK, SB_BLOCK), 1)
    return row, col, _bf(jnp.where(r > c, 1.0, 0.0)), _bf(jnp.where(r >= c, 1.0, 0.0))


def _stack_heads(x2, masks):
    return jnp.concatenate([_bf(jnp.where(hm, x2, jnp.zeros_like(x2))) for hm in masks], axis=0)


def _side_by_side(x):
    return jnp.concatenate([x[:SB_BLOCK], x[SB_BLOCK:]], axis=1)


def sb_fwd(proj, kv, name):
    S = proj.shape[0]
    n_g = GM_W // SB_COLS
    tq = _tile(S, SB_Q_CHUNK, SB_BLOCK)
    nqb = tq // SB_BLOCK

    def body(q_ref, k_ref, v_ref, o_ref, s_ref):
        chunk = pl.program_id(1)
        row, col, tri, _ = _sb_consts()
        masks = [_head_mask(h, LANES) for h in range(2)]
        pairs = [slice(p * LANES, (p + 1) * LANES) for p in range(SB_PAIRS)]

        def q_loop(qi, _):
            i = chunk * nqb + qi
            rows = pl.ds(pl.multiple_of(qi * SB_BLOCK, SB_BLOCK), SB_BLOCK)
            qps = [_stack_heads(q_ref[rows, cs], masks) for cs in pairs]

            def look_ahead(t):
                keys = pl.ds(pl.multiple_of(jnp.maximum(i - t, 0) * SB_BLOCK, SB_BLOCK), SB_BLOCK)
                sums = []
                for p, cs in enumerate(pairs):
                    s, lom_sum, _ = _sb_scores(qps[p], k_ref[keys, cs], t * SB_BLOCK, row, col, tri)
                    s_ref[p] = s
                    sums.append(lom_sum)
                return tuple(sums)

            def k_step(carry):
                t, accs, runs, sums, _ = carry
                keys = pl.ds(pl.multiple_of((i - t) * SB_BLOCK, SB_BLOCK), SB_BLOCK)
                new_accs, new_runs = [], []
                for p, cs in enumerate(pairs):
                    a = _sb_weights(s_ref[p], runs[p], t * SB_BLOCK, row, col)
                    a_hi, a_lo = _split_bf(_side_by_side(a))
                    vp = _stack_heads(v_ref[keys, cs], masks)
                    new_accs.append(accs[p] + _dot(a_hi, vp) + _dot(a_lo, vp))
                    new_runs.append(runs[p] + sums[p])
                return t + 1, tuple(new_accs), tuple(new_runs), look_ahead(t + 1), _sb_alive(new_runs)

            zero = jnp.zeros((2 * SB_BLOCK, 1), F32)
            _, accs, _, _, _ = lax.while_loop(
                lambda carry: jnp.logical_and(carry[0] <= i, carry[4]), k_step,
                (jnp.int32(0), (jnp.zeros((SB_BLOCK, LANES), F32),) * SB_PAIRS, (zero,) * SB_PAIRS,
                 look_ahead(jnp.int32(0)), jnp.bool_(True)))
            for p, cs in enumerate(pairs):
                o_ref[rows, cs] = accs[p]
            return 0

        lax.fori_loop(0, nqb, q_loop, 0)

    return pl.pallas_call(
        body, grid=(n_g, S // tq),
        in_specs=[pl.BlockSpec((tq, SB_COLS), lambda g, c: (c, g)),
                  pl.BlockSpec((S, SB_COLS), lambda g, c: (0, g)),
                  pl.BlockSpec((S, SB_COLS), lambda g, c: (0, n_g + g))],
        out_specs=pl.BlockSpec((tq, SB_COLS), lambda g, c: (c, g)),
        out_shape=jax.ShapeDtypeStruct((S, GM_W), F32),
        scratch_shapes=[pltpu.VMEM((SB_PAIRS, 2 * SB_BLOCK, LANES), F32)],
        compiler_params=_cparams(("parallel", "parallel")), name=name)(proj, kv, kv)


def sb_bwd(proj, kv, dy, out, dk_init, dv_init, name):
    S = proj.shape[0]
    n_g = GM_W // SB_COLS
    tq = _tile(S, SB_Q_CHUNK, SB_BLOCK)
    nqb = tq // SB_BLOCK
    n_chunks = S // tq
    has_init = dk_init is not None
    scale = HEAD_DIM ** -0.5

    def body(*refs):
        s_ref, beta_ref, da_ref = refs[-3:]
        if has_init:
            q_ref, k_ref, v_ref, do_ref, out_ref, dki_ref, dvi_ref, dq_ref, dko_ref, dvo_ref, dk_acc, dv_acc = refs[:-3]
        else:
            q_ref, k_ref, v_ref, do_ref, out_ref, dq_ref, dko_ref, dvo_ref, dk_acc, dv_acc = refs[:-3]
        chunk = pl.program_id(1)
        cols = pl.ds(pl.multiple_of(pl.program_id(0) * SB_COLS, SB_COLS), SB_COLS)

        @pl.when(chunk == 0)
        def _():
            if has_init:
                pltpu.sync_copy(dki_ref.at[:, cols], dk_acc)
                pltpu.sync_copy(dvi_ref.at[:, cols], dv_acc)
            else:
                dk_acc[...] = jnp.zeros_like(dk_acc)
                dv_acc[...] = jnp.zeros_like(dv_acc)

        row, col, tri, tri_inc = _sb_consts()
        masks = [_head_mask(h, LANES) for h in range(2)]
        pairs = [slice(p * LANES, (p + 1) * LANES) for p in range(SB_PAIRS)]

        def q_loop(qi, _):
            i = chunk * nqb + qi
            rows = pl.ds(pl.multiple_of(qi * SB_BLOCK, SB_BLOCK), SB_BLOCK)
            qps = [_stack_heads(q_ref[rows, cs], masks) for cs in pairs]
            dops = [_stack_heads(do_ref[rows, cs], masks) for cs in pairs]
            e_tots = [jnp.sum(dop.astype(F32) * jnp.concatenate([out_ref[rows, cs]] * 2, axis=0), axis=1,
                              keepdims=True) for dop, cs in zip(dops, pairs)]

            def look_ahead(t):
                keys = pl.ds(pl.multiple_of(jnp.maximum(i - t, 0) * SB_BLOCK, SB_BLOCK), SB_BLOCK)
                sums = []
                for p, cs in enumerate(pairs):
                    s, lom_sum, lb = _sb_scores(qps[p], k_ref[keys, cs], t * SB_BLOCK, row, col, tri)
                    s_ref[p] = s
                    beta_ref[p] = jnp.exp(lb)
                    da_ref[p] = _dot_nt(dops[p], v_ref[keys, cs])
                    sums.append(lom_sum)
                return tuple(sums)

            def k_step(carry):
                t, dqs, runs, e_runs, sums, _ = carry
                keys = pl.ds(pl.multiple_of((i - t) * SB_BLOCK, SB_BLOCK), SB_BLOCK)
                new_dqs, new_runs, new_e_runs = [], [], []
                for p, cs in enumerate(pairs):
                    a = _sb_weights(s_ref[p], runs[p], t * SB_BLOCK, row, col)
                    e = a * da_ref[p]
                    e_hi, e_lo = _split_bf(e)
                    before = e_tots[p] - e_runs[p] - (_dot(e_hi, tri_inc) + _dot(e_lo, tri_inc))
                    beta = beta_ref[p]
                    dz = jnp.where(_sb_mask(t * SB_BLOCK, row, col), e * (1.0 - beta) - before * beta, 0.0)
                    dz = _bf(dz * scale)
                    new_dqs.append(dqs[p] + _dot(_side_by_side(dz), _stack_heads(k_ref[keys, cs], masks)))
                    dk_acc[keys, cs] += _dot_tn(dz, qps[p])
                    dv_acc[keys, cs] += _dot_tn(_bf(a), dops[p])
                    new_runs.append(runs[p] + sums[p])
                    new_e_runs.append(e_runs[p] + jnp.sum(e, axis=1, keepdims=True))
                return (t + 1, tuple(new_dqs), tuple(new_runs), tuple(new_e_runs), look_ahead(t + 1),
                        _sb_alive(new_runs))

            zero = jnp.zeros((2 * SB_BLOCK, 1), F32)
            zeros = (zero,) * SB_PAIRS
            _, dqs, _, _, _, _ = lax.while_loop(
                lambda carry: jnp.logical_and(carry[0] <= i, carry[5]), k_step,
                (jnp.int32(0), (jnp.zeros((SB_BLOCK, LANES), F32),) * SB_PAIRS, zeros, zeros,
                 look_ahead(jnp.int32(0)), jnp.bool_(True)))
            for p, cs in enumerate(pairs):
                dq_ref[rows, cs] = dqs[p]
            return 0

        lax.fori_loop(0, nqb, q_loop, 0)

        @pl.when(chunk == n_chunks - 1)
        def _():
            pltpu.sync_copy(dk_acc, dko_ref.at[:, cols])
            pltpu.sync_copy(dv_acc, dvo_ref.at[:, cols])

    qspec = pl.BlockSpec((tq, SB_COLS), lambda g, c: (c, g))
    kspec = pl.BlockSpec((S, SB_COLS), lambda g, c: (0, g))
    in_specs = [qspec, kspec, pl.BlockSpec((S, SB_COLS), lambda g, c: (0, n_g + g)), qspec, qspec]
    ins = [proj, kv, kv, dy, out]
    if has_init:
        in_specs += [_ANY, _ANY]
        ins += [dk_init, dv_init]
    sh = jax.ShapeDtypeStruct((S, GM_W), F32)
    return pl.pallas_call(
        body, grid=(n_g, n_chunks), in_specs=in_specs, out_specs=[qspec, _ANY, _ANY],
        out_shape=[sh, sh, sh],
        scratch_shapes=[pltpu.VMEM((S, SB_COLS), F32), pltpu.VMEM((S, SB_COLS), F32)]
        + [pltpu.VMEM((SB_PAIRS, 2 * SB_BLOCK, LANES), F32)] * 3,
        compiler_params=_cparams(("arbitrary", "arbitrary")), name=name)(*ins)


def adamw(w, g, m, v, name):
    shape = w.shape
    C = shape[-1] if w.ndim > 1 else shape[0]
    R = w.size // C
    tr = _tile(R, ROW_TILE, 8)

    def body(w_ref, g_ref, m_ref, v_ref, d_ref, nm_ref, nv_ref):
        gv = g_ref[...]
        m2 = ADAM_B1 * m_ref[...] + (1.0 - ADAM_B1) * gv
        v2 = ADAM_B2 * v_ref[...] + (1.0 - ADAM_B2) * (gv * gv)
        m_hat = m2 / (1.0 - ADAM_B1 ** ADAM_STEP)
        v_hat = v2 / (1.0 - ADAM_B2 ** ADAM_STEP)
        d_ref[...] = -ADAM_LR * (m_hat / (jnp.sqrt(v_hat) + ADAM_EPS) + ADAM_WD * w_ref[...])
        nm_ref[...] = m2
        nv_ref[...] = v2

    blk = pl.BlockSpec((tr, C), lambda i: (i, 0))
    sh = jax.ShapeDtypeStruct((R, C), F32)
    outs = pl.pallas_call(
        body, grid=(R // tr,), in_specs=[blk] * 4, out_specs=[blk] * 3, out_shape=[sh] * 3,
        compiler_params=_cparams(("parallel",)), name=name)(
            w.reshape(R, C), g.reshape(R, C), m.reshape(R, C), v.reshape(R, C))
    return tuple(o.reshape(shape) for o in outs)


def _place():
    return lax.axis_index("x"), lax.axis_index("y"), lax.axis_index("c")


def _other_chips(x, y):
    return [(1 - x, y), (x, 1 - y), (1 - x, 1 - y)]


_ANY = pl.BlockSpec(memory_space=pl.ANY)
LOCAL_CHUNKS = 8
SEG_LOCAL_CHUNKS = 4


def _remote(src, dst, send_sems, recv_sems, k, to):
    return pltpu.make_async_remote_copy(src_ref=src, dst_ref=dst, send_sem=send_sems.at[k], recv_sem=recv_sems.at[k],
                                        device_id=to, device_id_type=MESH)


def _gather_stage(segs, n_gath, stage):
    def make(ins, outs, send_sems, recv_sems, *local_sems):
        x, y, c = _place()
        sibling = (x, y, 1 - c)
        chips = _other_chips(x, y)
        copies, waits = [], []
        for s, (b, r0, n) in enumerate(segs):
            half = n // 2

            def piece(px, py, pc, b=b, r0=r0, half=half):
                return outs[b].at[2 * px + py, pl.ds(r0 + pc * half, half), :]

            for j, chip in enumerate(chips):
                if stage == 0:
                    mine = ins[n_gath + b].at[pl.ds(r0 + c * half, half), :]
                    cp = _remote(mine, piece(x, y, c), send_sems, recv_sems, 3 * s + j, (*chip, c))
                    landed, sender = piece(*chip, c), (*chip, c)
                else:
                    cp = _remote(piece(*chip, c), piece(*chip, c), send_sems, recv_sems, 3 * s + j, sibling)
                    landed, sender = piece(*chip, 1 - c), sibling
                copies.append(cp)
                waits.append(cp.wait_send)
                waits.append(_remote(landed, landed, send_sems, recv_sems, 3 * s + j, sender).wait_recv)
            if stage == 0:
                rows = n // SEG_LOCAL_CHUNKS
                for q in range(SEG_LOCAL_CHUNKS):
                    r = pl.ds(r0 + q * rows, rows)
                    cp = pltpu.make_async_copy(ins[n_gath + b].at[r, :], outs[b].at[2 * x + y, r, :],
                                               local_sems[0].at[SEG_LOCAL_CHUNKS * s + q])
                    copies.append(cp)
                    waits.append(cp.wait)
        return copies, waits

    return make


def _gather_sems(segs, stage):
    return [3 * len(segs), 3 * len(segs)] + ([SEG_LOCAL_CHUNKS * len(segs)] if stage == 0 else [])


def all_gather_segments(bufs, segs, name):
    nb = len(bufs)
    n0, n1 = len(_gather_sems(segs, 0)), len(_gather_sems(segs, 1))

    def body(*refs):
        in_refs, out_refs = refs[:nb], refs[nb:2 * nb]
        sems = refs[2 * nb:]
        for stage, stage_sems in ((0, sems[:n0]), (1, sems[n0:n0 + n1])):
            copies, waits = _gather_stage(segs, nb, stage)(list(out_refs) + list(in_refs), out_refs, *stage_sems)
            for cp in copies:
                cp.start()
            for wait in waits:
                wait()

    return pl.pallas_call(
        body, in_specs=[_ANY] * nb, out_specs=[_ANY] * nb,
        out_shape=[jax.ShapeDtypeStruct((N_CHIPS,) + b.shape, b.dtype) for b in bufs],
        scratch_shapes=[pltpu.SemaphoreType.DMA((n,)) for n in _gather_sems(segs, 0) + _gather_sems(segs, 1)],
        name=name)(*bufs)


def hosted_gather(gathered, bufs, segs, stage):
    arrays = list(gathered) + (list(bufs) if stage == 0 else [])
    return Hosted(arrays, len(gathered), _gather_sems(segs, stage), _gather_stage(segs, len(gathered), stage))


def _exchange(step, arrays, name):
    n_h, n_alias, n_fresh = len(step.arrays), step.n_out, len(step.fresh)

    def body(*refs):
        h_in = refs[:n_h]
        h_out = refs[n_h:n_h + n_alias + n_fresh]
        copies, waits = step.make(h_in, h_out, *refs[n_h + n_alias + n_fresh:])
        for cp in copies:
            cp.start()
        for wait in waits:
            wait()

    return pl.pallas_call(
        body, in_specs=[_ANY] * n_h, out_specs=[_ANY] * (n_alias + n_fresh),
        out_shape=[jax.ShapeDtypeStruct(a.shape, a.dtype) for a in step.arrays[:n_alias]] + step.fresh,
        scratch_shapes=[pltpu.SemaphoreType.DMA((n,)) for n in step.sems],
        input_output_aliases={k: k for k in range(n_alias)}, name=name)(*arrays)


def swap_step(gs):
    nb = len(gs)

    def make(ins, outs, send_sems, recv_sems):
        x, y, c = _place()
        copies = []
        for b in range(nb):
            half = gs[b].shape[1] // 2
            for k in range(N_CHIPS):
                copies.append(_remote(ins[b].at[k, pl.ds((1 - c) * half, half), :], outs[b].at[k], send_sems,
                                      recv_sems, N_CHIPS * b + k, (x, y, 1 - c)))
        return copies, [cp.wait for cp in copies]

    fresh = [jax.ShapeDtypeStruct((N_CHIPS, g.shape[1] // 2, g.shape[2]), g.dtype) for g in gs]
    return Hosted(list(gs), 0, [N_CHIPS * nb, N_CHIPS * nb], make, fresh)


def _core_index():
    return lax.axis_index("c").astype(jnp.int32).reshape(1)


def add_cores(g, theirs, name):
    n, Ph, Wd = theirs.shape
    tr = _tile(Ph, ROW_TILE, 16)
    steps = Ph // tr

    def body(c_ref, g_ref, t_ref, o_ref):
        o_ref[...] = _bf(g_ref[...] + t_ref[...])

    blk = pl.BlockSpec((None, tr, Wd), lambda k, i, c_ref: (k, i, 0))
    return pl.pallas_call(
        body,
        grid_spec=pltpu.PrefetchScalarGridSpec(
            num_scalar_prefetch=1, grid=(n, steps),
            in_specs=[pl.BlockSpec((None, tr, Wd), lambda k, i, c_ref: (k, c_ref[0] * steps + i, 0)), blk],
            out_specs=blk),
        out_shape=jax.ShapeDtypeStruct((n, Ph, Wd), BF16),
        compiler_params=_cparams(("parallel", "parallel")), name=name)(_core_index(), g, theirs)


def add_chips(slots, name):
    n, Ph, Wd = slots.shape
    tr = _tile(Ph, ROW_TILE, 16)
    steps = Ph // tr

    def body(c_ref, *refs):
        acc = refs[0][...].astype(F32)
        for k in range(1, n):
            acc = acc + refs[k][...].astype(F32)
        refs[n][...] = acc

    in_specs = [pl.BlockSpec((None, tr, Wd), functools.partial(lambda i, c_ref, k: (k, i, 0), k=k)) for k in range(n)]
    return pl.pallas_call(
        body,
        grid_spec=pltpu.PrefetchScalarGridSpec(
            num_scalar_prefetch=1, grid=(steps,), in_specs=in_specs,
            out_specs=pl.BlockSpec((tr, Wd), lambda i, c_ref: (c_ref[0] * steps + i, 0))),
        out_shape=jax.ShapeDtypeStruct((2 * Ph, Wd), F32),
        compiler_params=_cparams(("parallel",)), name=name)(_core_index(), *([slots] * n))


def _whole_tile_chunks(rows, most, tile_rows):
    return max(n for n in range(1, most + 1) if rows % (n * tile_rows) == 0)


def scatter_step(hs):
    nb = len(hs)
    chunks = [_whole_tile_chunks(h.shape[1], LOCAL_CHUNKS, 16) for h in hs]

    def make(ins, outs, send_sems, recv_sems, local_sems):
        x, y, c = _place()
        me = 2 * x + y
        copies, waits = [], []
        for b in range(nb):
            for j, (cx, cy) in enumerate(_other_chips(x, y)):
                cp = _remote(ins[b].at[2 * cx + cy], outs[b].at[me], send_sems, recv_sems, 3 * b + j, (cx, cy, c))
                copies.append(cp)
                waits.append(cp.wait_send)
                landed = outs[b].at[2 * cx + cy]
                waits.append(_remote(landed, landed, send_sems, recv_sems, 3 * b + j, (cx, cy, c)).wait_recv)
            rows = hs[b].shape[1] // chunks[b]
            for q in range(chunks[b]):
                r = pl.ds(q * rows, rows)
                cp = pltpu.make_async_copy(ins[b].at[me, r, :], outs[b].at[me, r, :],
                                           local_sems.at[sum(chunks[:b]) + q])
                copies.append(cp)
                waits.append(cp.wait)
        return copies, waits

    fresh = [jax.ShapeDtypeStruct(h.shape, h.dtype) for h in hs]
    return Hosted(list(hs), 0, [3 * nb, 3 * nb, sum(chunks)], make, fresh)


JOIN_CHUNKS = 4


def join_step(ts):
    nb = len(ts)
    n = JOIN_CHUNKS

    def make(ins, outs, send_sems, recv_sems):
        x, y, c = _place()
        copies, waits = [], []
        for b in range(nb):
            half = ts[b].shape[0] // 2
            rows = half // n
            for k in range(n):
                part = outs[b].at[pl.ds(c * half + k * rows, rows), :]
                cp = _remote(part, part, send_sems, recv_sems, n * b + k, (x, y, 1 - c))
                copies.append(cp)
                waits.append(cp.wait_send)
                landed = outs[b].at[pl.ds((1 - c) * half + k * rows, rows), :]
                waits.append(_remote(landed, landed, send_sems, recv_sems, n * b + k, (x, y, 1 - c)).wait_recv)
        return copies, waits

    return Hosted(list(ts), nb, [n * nb, n * nb], make)


WIDE = ['ffn1_w_gate', 'ffn1_w_up', 'ffn2_w_gate', 'ffn2_w_up']


def _rows_of(p):
    if p.shape[-1] == PACK_W and (p.size // PACK_W) % PART_ROWS == 0:
        return p.reshape(-1, PACK_W)
    flat = p.reshape(-1)
    rows = -(-flat.shape[0] // (PACK_W * PART_ROWS)) * PART_ROWS
    return jnp.pad(flat, (0, rows * PACK_W - flat.shape[0])).reshape(rows, PACK_W)


def _pack(parts):
    buf = jnp.concatenate([_rows_of(p) for p in parts], axis=0)
    rows = buf.shape[0]
    total = -(-rows // PACK_ROWS) * PACK_ROWS
    return jnp.pad(buf, ((0, total - rows), (0, 0)))


def _unpack(buf, shapes):
    outs, r = [], 0
    lead = buf.shape[:-2]
    for shp in shapes:
        n = math.prod(shp)
        rows = -(-n // (PACK_W * PART_ROWS)) * PART_ROWS
        blk = buf[..., r:r + rows, :]
        if n != rows * PACK_W:
            blk = blk.reshape(lead + (rows * PACK_W,))[..., :n]
        outs.append(blk.reshape(lead + tuple(shp)))
        r += rows
    return outs


class WeightStream:
    def __init__(self, shards):
        depth = shards['ffn1_w_gate'].shape[0]
        order = [(which, l) for l in range(depth) for which in ('ffn1', 'ffn2')]
        self.count = len(order)
        gate_up = [shards[f'{which}_w_{part}'][l].astype(BF16) for which, l in order for part in ('gate', 'up')]
        downs = [shards[f'{which}_w_down'][l].astype(BF16) for which, l in order]
        self.d, self.cols = gate_up[0].shape
        assert downs[0].shape == (self.cols, PACK_W)
        self.rest_names = [n for n in W_NAMES if n in SHARD_AXIS and n not in WIDE and not n.endswith('_w_down')]
        rest = [lax.bitcast_convert_type(shards[n], BF16) if n == 'a_v_norm' else shards[n].astype(BF16)
                for n in self.rest_names]
        self.rest_shapes = [p.shape for p in rest]
        self.bufs = [jnp.concatenate(gate_up, axis=0), _pack(downs + rest)]
        self.rest_row0 = self.count * self.cols
        first = self.segments(0) + [(1, self.rest_row0, self.bufs[1].shape[0] - self.rest_row0)]
        self.gathered = all_gather_segments(self.bufs, first, "all_gather_first")

    def segments(self, i):
        return [(0, 2 * self.d * i, 2 * self.d), (1, self.cols * i, self.cols)]

    def rest(self):
        out = {}
        for n, g in zip(self.rest_names, _unpack(self.gathered[1][:, self.rest_row0:], self.rest_shapes)):
            if n == 'a_v_norm':
                g = lax.bitcast_convert_type(g, F32)
            out[n] = jnp.concatenate([g[k] for k in range(N_CHIPS)], axis=SHARD_AXIS[n])
        return out

    def ffn(self, i):
        wide, narrow = self.gathered
        r = 2 * self.d * i
        gate = jnp.concatenate([wide[k, r:r + self.d] for k in range(N_CHIPS)], axis=1)
        up = jnp.concatenate([wide[k, r + self.d:r + 2 * self.d] for k in range(N_CHIPS)], axis=1)
        down = narrow[:, self.cols * i:self.cols * (i + 1)].reshape(N_CHIPS * self.cols, PACK_W)
        return (gate, None), (up, None), (down, None)

    def hosted(self, i, stage):
        if i >= self.count:
            return None
        return hosted_gather(self.gathered, self.bufs, self.segments(i), stage)

    def absorb(self, gathered):
        self.gathered = list(gathered)


class GradSink:
    def __init__(self):
        self.queue = []
        self.totals = {}
        self.held = {}

    def add(self, i, d_wg, d_wu, d_wd, ride):
        cols = d_wg.shape[1] // N_CHIPS
        wide = jnp.stack([jnp.concatenate([d_wg[:, k * cols:(k + 1) * cols], d_wu[:, k * cols:(k + 1) * cols]],
                                          axis=0) for k in range(N_CHIPS)], axis=0)
        bufs = [wide, d_wd.reshape(N_CHIPS, cols, d_wd.shape[1])]
        if ride:
            self.queue.append([0, bufs, i])
        else:
            self.held[i] = bufs

    def step(self):
        if not self.queue:
            return None
        stage, bufs, _ = self.queue[0]
        return (swap_step, scatter_step, join_step)[stage](bufs)

    def absorb(self, results):
        entry = self.queue[0]
        stage, bufs, i = entry
        if stage == 0:
            entry[1] = [add_cores(g, t, f"ffn{i}_add_cores_{b}") for b, (g, t) in enumerate(zip(bufs, results))]
        elif stage == 1:
            entry[1] = [add_chips(s, f"ffn{i}_add_chips_{b}") for b, s in enumerate(results)]
        else:
            self.totals[i] = list(results)
            self.queue.pop(0)
        entry[0] = stage + 1

    def reduce(self, grads, shard_shapes):
        assert not self.queue
        narrow = [n for n in W_NAMES if n in SHARD_AXIS and n in grads]
        repl = [n for n in W_NAMES if n not in SHARD_AXIS]
        full = {n: (jnp.stack(g, axis=0) if isinstance(g, list) else g) for n, g in grads.items()}
        blocks = []
        for k in range(N_CHIPS):
            parts = [jnp.split(full[n], N_CHIPS, axis=SHARD_AXIS[n])[k] for n in narrow]
            parts += [full[n] for n in repl]
            blocks.append(_pack(parts))
        held = sorted(self.held)
        gs = [b for i in held for b in self.held[i]] + [jnp.stack(blocks, axis=0)]
        theirs = _exchange(swap_step(gs), gs, "grads_swap_halves")
        sums = [add_cores(g, t, f"grads_add_cores_{b}") for b, (g, t) in enumerate(zip(gs, theirs))]
        slots = _exchange(scatter_step(sums), sums, "grads_scatter")
        halves = [add_chips(s, f"grads_add_chips_{b}") for b, s in enumerate(slots)]
        totals = _exchange(join_step(halves), halves, "grads_join_halves")
        for n, i in enumerate(held):
            self.totals[i] = totals[2 * n:2 * n + 2]
        shapes = [shard_shapes[n] for n in narrow] + [full[n].shape for n in repl]
        out = dict(zip(narrow + repl, _unpack(totals[-1], shapes)))
        d = self.totals[0][0].shape[0] // 2
        for w, which in enumerate(('ffn1', 'ffn2')):
            mine = [self.totals[i] for i in sorted(self.totals) if i % 2 == w]
            out[f'{which}_w_gate'] = jnp.stack([t[0][:d] for t in mine], axis=0)
            out[f'{which}_w_up'] = jnp.stack([t[0][d:] for t in mine], axis=0)
            out[f'{which}_w_down'] = jnp.stack([t[1] for t in mine], axis=0)
        return out


def _ffn_fwd(x, h, stream, i, next_gain, tag):
    wg, wu, wd = stream.ffn(i)
    ici = stream.hosted(i + 1, 0)
    if ici is None:
        g, u, a = ffn_up(h, wg, wu, tag + "_up")
        out = mm([(a, 0, wd, a.shape[1], 0)], res=x, scale=0.5, norm_next=next_gain, name=tag + "_down")
    else:
        (g, u, a), gathered = ffn_up(h, wg, wu, tag + "_up", hosted=ici)
        stream.absorb(gathered)
        out, gathered = mm([(a, 0, wd, a.shape[1], 0)], res=x, scale=0.5, norm_next=next_gain,
                           name=tag + "_down", hosted=stream.hosted(i + 1, 1))
        stream.absorb(gathered)
    xo, h_next = (out, None) if next_gain is None else out
    return xo, h_next, (x, h, g, u, a, wg, wu, wd)


def _carrying(sink, kernel, *args, **kwargs):
    step = sink.step()
    if step is None:
        return kernel(*args, **kwargs)
    out, results = kernel(*args, hosted=step, **kwargs)
    sink.absorb(results)
    return out


def _ffn_bwd(dxo, saved, gain, sink, i, tag):
    x, h, g, u, a, wg, wu, wd = saved
    Fd = a.shape[1]
    dxo, dxo_b = dxo
    dgp, du = _carrying(sink, ffn_down_bwd, dxo_b, wd, g, u, tag + "_down_bwd")
    d_wd = _carrying(sink, mm_tn, a, dxo_b, scale=0.5, name=tag + "_dwd")
    d_wg = _carrying(sink, mm_tn, h, dgp, name=tag + "_dwg")
    d_wu = _carrying(sink, mm_tn, h, du, name=tag + "_dwu")
    sink.add(i, d_wg, d_wu, d_wd, ride=i > 0)
    return _carrying(sink, mm, [(dgp, 0, wg, Fd, 0), (du, 0, wu, Fd, 0)], tb=True, res=dxo, norm_bwd=(x, gain),
                     name=tag + "_up_bwd")


def local_step(x, mem, W, stream, sink, target):
    D = x.shape[1]
    depth = W['ffn1_norm'].shape[0]
    n_a = W['a_w_in'].shape[0]
    G = {}
    mem_h = rms_fwd(mem, W['mem_norm'], "mem_norm")
    b_mats = jnp.broadcast_to(W['a_b_spatial'][..., None], W['a_b_spatial'].shape + (GM_CHUNK,))
    w_kv = (W['w_kv'], None)

    def mixer_w(l):
        if l < n_a:
            return (W['a_w_in'], l), (W['a_w_out'], l), 2 * GM_W
        return (W['b_w_in'], l - n_a), (W['b_w_out'], l - n_a), GM_W

    saved = []
    kv = kvn = x_kv = None
    h = rms_fwd(x, W['ffn1_norm'][0], "l0_ffn1_norm")
    for l in range(depth):
        if l == n_a:
            x_kv = x
            kvn = rms_fwd(x, W['kv_norm'], "kv_norm")
            kv = mm([(kvn, 0, w_kv, D, 0)], out_dtype=BF16, name="kv_proj")
        x, hm, s1 = _ffn_fwd(x, h, stream, 2 * l, W['mix_norm'][l], f"l{l}_ffn1")
        mem_kv = mm([(mem_h, 0, (W['w_mem_kv'], l), D, 0)], name=f"l{l}_mem_kv")
        w_in, w_out, tok_w = mixer_w(l)
        proj = mm([(hm, 0, w_in, D, 0)], name=f"l{l}_mix_in")
        if l < n_a:
            y_tok = gmlp_fwd(proj, W['a_v_norm'][l], W['a_w_spatial'][l], b_mats[l], f"l{l}_gmlp")
        else:
            y_tok = sb_fwd(proj, kv, f"l{l}_sb")
        q_cb = tok_w // MEM_W
        y_mem = memattn_fwd(proj, q_cb, mem_kv, f"l{l}_memattn")
        x_mid = x
        x, h2 = mm([(y_tok, 0, w_out, GM_W, 0), (y_mem, 0, w_out, MEM_W, GM_W // MEM_W)], res=x,
                   norm_next=W['ffn2_norm'][l], name=f"l{l}_mix_out")
        sm = (x_mid, hm, mem_kv, proj, y_tok, y_mem, q_cb)
        next_gain = W['ffn1_norm'][l + 1] if l + 1 < depth else None
        x, h, s2 = _ffn_fwd(x, h2, stream, 2 * l + 1, next_gain, f"l{l}_ffn2")
        saved.append((s1, sm, s2))

    loss, dx, d_final = final_loss(x, W['final_norm'], target, "loss_head")
    G['final_norm'] = d_final.reshape(-1)

    per_layer = {n: [None] * depth for n in ['ffn1_norm', 'mix_norm', 'ffn2_norm', 'w_mem_kv']}
    per_a = {n: [None] * n_a for n in ['a_w_in', 'a_v_norm', 'a_w_spatial', 'a_b_spatial', 'a_w_out']}
    per_b = {n: [None] * (depth - n_a) for n in ['b_w_in', 'b_w_out']}
    d_mem_h = None
    dk = dv = None
    for l in reversed(range(depth)):
        s1, sm, s2 = saved[l]
        dx, dg = _ffn_bwd(dx, s2, W['ffn2_norm'][l], sink, 2 * l + 1, f"l{l}_ffn2")
        per_layer['ffn2_norm'][l] = dg.reshape(-1)

        x_mid, hm, mem_kv, proj, y_tok, y_mem, q_cb = sm
        is_a = l < n_a
        w_in, w_out, tok_w = mixer_w(l)
        dy = mm([(dx[1], 0, w_out, D, 0)], tb=True, name=f"l{l}_mix_out_bwd")
        d_w_out = jnp.concatenate([mm_tn(y_tok, dx[1], name=f"l{l}_dwout_tok"),
                                   mm_tn(y_mem, dx[1], name=f"l{l}_dwout_mem")], axis=0)
        dq_mem, d_mem_kv = memattn_bwd(proj, q_cb, mem_kv, dy, GM_W // MEM_W, f"l{l}_memattn_bwd")
        if is_a:
            d_tok, d_ws, d_bs, d_vg = gmlp_bwd(proj, dy, W['a_v_norm'][l], W['a_w_spatial'][l], b_mats[l],
                                               f"l{l}_gmlp_bwd")
            per_a['a_w_spatial'][l], per_a['a_b_spatial'][l] = d_ws, d_bs[:, :, 0]
            per_a['a_v_norm'][l], per_a['a_w_out'][l] = d_vg.reshape(-1), d_w_out
        else:
            d_tok, dk, dv = sb_bwd(proj, kv, dy, y_tok, dk, dv, f"l{l}_sb_bwd")
            per_b['b_w_out'][l - n_a] = d_w_out
        d_w_in = jnp.concatenate([mm_tn(hm, d_tok, name=f"l{l}_dwin_tok"),
                                  mm_tn(hm, dq_mem, name=f"l{l}_dwin_mem")], axis=1)
        (per_a['a_w_in'] if is_a else per_b['b_w_in'])[l if is_a else l - n_a] = d_w_in
        dx, dg = mm([(d_tok, 0, w_in, tok_w, 0), (dq_mem, 0, w_in, MEM_W, tok_w // MEM_W)], tb=True, res=dx[0],
                    norm_bwd=(x_mid, W['mix_norm'][l]), name=f"l{l}_mix_in_bwd")
        per_layer['mix_norm'][l] = dg.reshape(-1)
        per_layer['w_mem_kv'][l] = mm_tn(mem_h, d_mem_kv, name=f"l{l}_dw_mem_kv")
        d_mem_h = mm([(d_mem_kv, 0, (W['w_mem_kv'], l), 2 * MEM_W, 0)], tb=True, res=d_mem_h,
                     name=f"l{l}_mem_kv_bwd")

        dx, dg = _ffn_bwd(dx, s1, W['ffn1_norm'][l], sink, 2 * l, f"l{l}_ffn1")
        per_layer['ffn1_norm'][l] = dg.reshape(-1)
        if l == n_a:
            G['w_kv'] = jnp.concatenate([mm_tn(kvn, dk, name="dw_k"), mm_tn(kvn, dv, name="dw_v")], axis=1)
            dx, dg = mm([(dk, 0, w_kv, GM_W, 0), (dv, 0, w_kv, GM_W, 1)], tb=True, res=dx[0],
                        norm_bwd=(x_kv, W['kv_norm']), name="kv_proj_bwd")
            G['kv_norm'] = dg.reshape(-1)

    G['mem_norm'] = rms_gain_grad(mem, d_mem_h, "mem_norm_bwd").reshape(-1)
    for d in (per_layer, per_a, per_b):
        G.update(d)
    return loss, dx[0], G


def kernel(x, mem, ffn1_norm, ffn1_w_gate, ffn1_w_up, ffn1_w_down, mix_norm, ffn2_norm, ffn2_w_gate, ffn2_w_up, ffn2_w_down, mem_norm, w_mem_kv, a_w_in, a_v_norm, a_w_spatial, a_b_spatial, a_w_out, kv_norm, w_kv, b_w_in, b_w_out, final_norm, loss_target, m_ffn1_norm, m_ffn1_w_gate, m_ffn1_w_up, m_ffn1_w_down, m_mix_norm, m_ffn2_norm, m_ffn2_w_gate, m_ffn2_w_up, m_ffn2_w_down, m_mem_norm, m_w_mem_kv, m_a_w_in, m_a_v_norm, m_a_w_spatial, m_a_b_spatial, m_a_w_out, m_kv_norm, m_w_kv, m_b_w_in, m_b_w_out, m_final_norm, v_ffn1_norm, v_ffn1_w_gate, v_ffn1_w_up, v_ffn1_w_down, v_mix_norm, v_ffn2_norm, v_ffn2_w_gate, v_ffn2_w_up, v_ffn2_w_down, v_mem_norm, v_w_mem_kv, v_a_w_in, v_a_v_norm, v_a_w_spatial, v_a_b_spatial, v_a_w_out, v_kv_norm, v_w_kv, v_b_w_in, v_b_w_out, v_final_norm):
    weights = dict(zip(W_NAMES, [ffn1_norm, ffn1_w_gate, ffn1_w_up, ffn1_w_down, mix_norm, ffn2_norm, ffn2_w_gate,
                                 ffn2_w_up, ffn2_w_down, mem_norm, w_mem_kv, a_w_in, a_v_norm, a_w_spatial,
                                 a_b_spatial, a_w_out, kv_norm, w_kv, b_w_in, b_w_out, final_norm]))
    m_in = dict(zip(W_NAMES, [m_ffn1_norm, m_ffn1_w_gate, m_ffn1_w_up, m_ffn1_w_down, m_mix_norm, m_ffn2_norm,
                              m_ffn2_w_gate, m_ffn2_w_up, m_ffn2_w_down, m_mem_norm, m_w_mem_kv, m_a_w_in,
                              m_a_v_norm, m_a_w_spatial, m_a_b_spatial, m_a_w_out, m_kv_norm, m_w_kv, m_b_w_in,
                              m_b_w_out, m_final_norm]))
    v_in = dict(zip(W_NAMES, [v_ffn1_norm, v_ffn1_w_gate, v_ffn1_w_up, v_ffn1_w_down, v_mix_norm, v_ffn2_norm,
                              v_ffn2_w_gate, v_ffn2_w_up, v_ffn2_w_down, v_mem_norm, v_w_mem_kv, v_a_w_in,
                              v_a_v_norm, v_a_w_spatial, v_a_b_spatial, v_a_w_out, v_kv_norm, v_w_kv, v_b_w_in,
                              v_b_w_out, v_final_norm]))

    stream = WeightStream({n: weights[n] for n in SHARD_AXIS})
    W = {n: weights[n] for n in W_NAMES if n not in SHARD_AXIS}
    W.update(stream.rest())
    sink = GradSink()
    loss, dx, grads = local_step(x[0], mem[0], W, stream, sink, loss_target[0])
    total = sink.reduce(grads, {n: weights[n].shape for n in SHARD_AXIS})
    loss = lax.psum(loss[0, 0], ("x", "y", "c"))

    deltas, new_m, new_v = {}, {}, {}
    for n in W_NAMES:
        deltas[n], new_m[n], new_v[n] = adamw(weights[n], total[n], m_in[n], v_in[n], "adamw_" + n)
    return (loss, dx[None], *[total[n] for n in W_NAMES], *[deltas[n] for n in W_NAMES],
            *[new_m[n] for n in W_NAMES], *[new_v[n] for n in W_NAMES])
```

```python
import functools
import math

import jax
import jax.numpy as jnp
from jax import lax
from jax.experimental import pallas as pl
from jax.experimental.pallas import tpu as pltpu

F32 = jnp.float32
BF16 = jnp.bfloat16
EPS = 1e-6
LANES = 128
ROW_TILE = 512
COL_TILE = 1408
PACK_ROWS = 256
PART_ROWS = 16
VMEM_LIMIT = 56 * 1024 * 1024

W_NAMES = ['ffn1_norm', 'ffn1_w_gate', 'ffn1_w_up', 'ffn1_w_down', 'mix_norm', 'ffn2_norm', 'ffn2_w_gate',
           'ffn2_w_up', 'ffn2_w_down', 'mem_norm', 'w_mem_kv', 'a_w_in', 'a_v_norm', 'a_w_spatial',
           'a_b_spatial', 'a_w_out', 'kv_norm', 'w_kv', 'b_w_in', 'b_w_out', 'final_norm']
SHARD_AXIS = {'ffn1_w_gate': 2, 'ffn1_w_up': 2, 'ffn1_w_down': 1, 'ffn2_w_gate': 2, 'ffn2_w_up': 2,
              'ffn2_w_down': 1, 'w_mem_kv': 1, 'a_w_in': 2, 'a_v_norm': 1, 'a_w_out': 1, 'w_kv': 1,
              'b_w_in': 1, 'b_w_out': 1}
N_CHIPS = 4
PACK_W = 1024

MEM_HEADS = 4
MEM_W = 256
HEAD_DIM = 64
GM_W = 768
GM_GROUPS = 6
GM_CHUNK = 128
CHUNK = 64
SB_BLOCK = 128
SB_Q_CHUNK = 1024
SB_DEAD = -110.0
SB_PAIRS = 2
SB_COLS = SB_PAIRS * LANES

ADAM_LR, ADAM_B1, ADAM_B2, ADAM_EPS, ADAM_WD, ADAM_STEP = 0.001, 0.9, 0.999, 1e-08, 0.01, 10

MESH = pl.DeviceIdType.MESH


def _cparams(sem=None):
    return pltpu.CompilerParams(dimension_semantics=sem, vmem_limit_bytes=VMEM_LIMIT)


class Hosted:
    def __init__(self, arrays, n_out, sems, make, fresh=()):
        self.arrays, self.n_out, self.sems, self.make, self.fresh = arrays, n_out, sems, make, list(fresh)


def _call(body, *, grid, in_specs, out_specs, out_shape, ins, semantics, name, hosted=None):
    if hosted is None:
        return pl.pallas_call(body, grid=grid, in_specs=in_specs, out_specs=out_specs, out_shape=out_shape,
                              compiler_params=_cparams(semantics), name=name)(*ins)
    n_in, n_out = len(ins), len(out_shape)
    n_h, n_alias = len(hosted.arrays), hosted.n_out
    n_ho = n_alias + len(hosted.fresh)

    def hosting_body(*refs):
        base_in, h_in = refs[:n_in], refs[n_in:n_in + n_h]
        base_out = refs[n_in + n_h:n_in + n_h + n_out]
        h_out = refs[n_in + n_h + n_out:n_in + n_h + n_out + n_ho]
        sems = refs[n_in + n_h + n_out + n_ho:]
        copies, waits = hosted.make(h_in, h_out, *sems)
        first = last = None
        for axis, extent in enumerate(grid):
            at_start, at_end = pl.program_id(axis) == 0, pl.program_id(axis) == extent - 1
            first = at_start if first is None else jnp.logical_and(first, at_start)
            last = at_end if last is None else jnp.logical_and(last, at_end)

        @pl.when(first)
        def _():
            for cp in copies:
                cp.start()

        body(*base_in, *base_out)

        @pl.when(last)
        def _():
            for wait in waits:
                wait()

    any_spec = pl.BlockSpec(memory_space=pl.ANY)
    outs = pl.pallas_call(
        hosting_body, grid=grid, in_specs=list(in_specs) + [any_spec] * n_h,
        out_specs=list(out_specs) + [any_spec] * n_ho,
        out_shape=(list(out_shape) + [jax.ShapeDtypeStruct(a.shape, a.dtype) for a in hosted.arrays[:n_alias]]
                   + hosted.fresh),
        scratch_shapes=[pltpu.SemaphoreType.DMA((n,)) for n in hosted.sems],
        input_output_aliases={n_in + k: n_out + k for k in range(n_alias)},
        compiler_params=_cparams(("arbitrary",) * len(grid)), name=name)(*ins, *hosted.arrays)
    return outs[:n_out], outs[n_out:]


def _tile(n, target, mult=LANES):
    best = None
    for t in range(mult, min(n, target) + 1, mult):
        if n % t == 0:
            best = t
    return best if best is not None else n


def _dot(a, b):
    return jnp.dot(a, b, preferred_element_type=F32)


def _dot_nt(a, b):
    return lax.dot_general(a, b, (((1,), (1,)), ((), ())), preferred_element_type=F32)


def _dot_tn(a, b):
    return lax.dot_general(a, b, (((0,), (0,)), ((), ())), preferred_element_type=F32)


def _bf(v):
    return v.astype(BF16)


def rms_fwd(x, gain, name):
    S, D = x.shape
    tm = _tile(S, ROW_TILE, 8)

    def body(x_ref, g_ref, o_ref):
        xf = x_ref[...]
        r = lax.rsqrt(jnp.mean(xf * xf, axis=-1, keepdims=True) + EPS)
        o_ref[...] = ((xf * r) * g_ref[...]).astype(o_ref.dtype)

    return pl.pallas_call(
        body, grid=(S // tm,),
        in_specs=[pl.BlockSpec((tm, D), lambda i: (i, 0)), pl.BlockSpec((1, D), lambda i: (0, 0))],
        out_specs=pl.BlockSpec((tm, D), lambda i: (i, 0)),
        out_shape=jax.ShapeDtypeStruct((S, D), BF16),
        compiler_params=_cparams(("parallel",)), name=name)(x, gain.reshape(1, D))


def rms_gain_grad(x, dh, name):
    S, D = x.shape
    tm = _tile(S, ROW_TILE, 8)

    def body(x_ref, dh_ref, dg_ref):
        xf = x_ref[...]
        r = lax.rsqrt(jnp.mean(xf * xf, axis=-1, keepdims=True) + EPS)

        @pl.when(pl.program_id(0) == 0)
        def _():
            dg_ref[...] = jnp.zeros_like(dg_ref)

        dg_ref[...] += jnp.sum(dh_ref[...] * (xf * r), axis=0, keepdims=True)

    row = pl.BlockSpec((tm, D), lambda i: (i, 0))
    return pl.pallas_call(
        body, grid=(S // tm,), in_specs=[row, row], out_specs=pl.BlockSpec((1, D), lambda i: (0, 0)),
        out_shape=jax.ShapeDtypeStruct((1, D), F32),
        compiler_params=_cparams(("arbitrary",)), name=name)(x, dh)


def final_loss(x, gain, target, name):
    S, D = x.shape
    tm = _tile(S, ROW_TILE, 16)

    def body(x_ref, g_ref, t_ref, loss_ref, dx_ref, dxb_ref, dg_ref):
        xf = x_ref[...]
        r = lax.rsqrt(jnp.mean(xf * xf, axis=-1, keepdims=True) + EPS)
        xhat = xf * r
        g = g_ref[...]
        diff = xhat * g - t_ref[...]
        dy = diff * (1.0 / D)
        dxhat = dy * g
        m = jnp.mean(dxhat * xhat, axis=-1, keepdims=True)
        dx = r * (dxhat - xhat * m)
        dx_ref[...] = dx
        dxb_ref[...] = _bf(dx)

        @pl.when(pl.program_id(0) == 0)
        def _():
            dg_ref[...] = jnp.zeros_like(dg_ref)
            loss_ref[...] = jnp.zeros_like(loss_ref)

        dg_ref[...] += jnp.sum(dy * xhat, axis=0, keepdims=True)
        per_tok = jnp.sum(diff * diff, axis=-1, keepdims=True) * (0.5 / D)
        loss_ref[...] += jnp.sum(per_tok, axis=0, keepdims=True)

    row = pl.BlockSpec((tm, D), lambda i: (i, 0))
    vec = pl.BlockSpec((1, D), lambda i: (0, 0))
    one = pl.BlockSpec((1, 1), lambda i: (0, 0))
    loss, dx, dxb, dg = pl.pallas_call(
        body, grid=(S // tm,), in_specs=[row, vec, row], out_specs=[one, row, row, vec],
        out_shape=[jax.ShapeDtypeStruct((1, 1), F32), jax.ShapeDtypeStruct((S, D), F32),
                   jax.ShapeDtypeStruct((S, D), BF16), jax.ShapeDtypeStruct((1, D), F32)],
        compiler_params=_cparams(("arbitrary",)), name=name)(x, gain.reshape(1, D), target)
    return loss, (dx, dxb), dg


def _wspec(w, blk, idx):
    _, l = w
    if l is None:
        return pl.BlockSpec(blk, idx)
    return pl.BlockSpec((None,) + blk, lambda *g: (l,) + idx(*g))


def mm(pairs, *, tb=False, res=None, scale=1.0, out_dtype=F32, norm_bwd=None, norm_next=None, hosted=None, name):
    M = pairs[0][0].shape[0]
    N = pairs[0][2][0].shape[-2 if tb else -1]
    tm = _tile(M, ROW_TILE, 16)
    tn = N if (norm_bwd is not None or norm_next is not None) else _tile(N, COL_TILE)
    n_p = len(pairs)
    has_res = res is not None
    n_in = 2 * n_p + has_res + (2 if norm_bwd is not None else 0) + (norm_next is not None)

    def body(*refs):
        acc = None
        for p in range(n_p):
            a = _bf(refs[2 * p][...])
            b = _bf(refs[2 * p + 1][...])
            d = _dot_nt(a, b) if tb else _dot(a, b)
            acc = d if acc is None else acc + d
        if scale != 1.0:
            acc = acc * scale
        if norm_bwd is None:
            if has_res:
                acc = acc + refs[2 * n_p][...]
            refs[n_in][...] = acc.astype(refs[n_in].dtype)
            if norm_next is not None:
                r = lax.rsqrt(jnp.mean(acc * acc, axis=-1, keepdims=True) + EPS)
                refs[n_in + 1][...] = _bf((acc * r) * refs[n_in - 1][...])
            return
        x_ref, g_ref = refs[n_in - 2], refs[n_in - 1]
        dx_ref, dxb_ref, dg_ref = refs[n_in:]
        xf = x_ref[...]
        r = lax.rsqrt(jnp.mean(xf * xf, axis=-1, keepdims=True) + EPS)
        xhat = xf * r
        dxhat = acc * g_ref[...]
        m = jnp.mean(dxhat * xhat, axis=-1, keepdims=True)
        dx = r * (dxhat - xhat * m)
        if has_res:
            dx = dx + refs[2 * n_p][...]
        dx_ref[...] = dx
        dxb_ref[...] = _bf(dx)

        @pl.when(pl.program_id(1) == 0)
        def _():
            dg_ref[...] = jnp.zeros_like(dg_ref)

        dg_ref[...] += jnp.sum(acc * xhat, axis=0, keepdims=True)

    ins, in_specs = [], []
    for a, cb, w, K, kb in pairs:
        ins += [a, w[0]]
        in_specs.append(pl.BlockSpec((tm, K), functools.partial(lambda j, i, cb: (i, cb), cb=cb)))
        if tb:
            in_specs.append(_wspec(w, (tn, K), functools.partial(lambda j, i, kb: (j, kb), kb=kb)))
        else:
            in_specs.append(_wspec(w, (K, tn), functools.partial(lambda j, i, kb: (kb, j), kb=kb)))
    tile = pl.BlockSpec((tm, tn), lambda j, i: (i, j))
    if has_res:
        ins.append(res)
        in_specs.append(tile)
    if norm_bwd is None:
        out_shape = [jax.ShapeDtypeStruct((M, N), out_dtype)]
        if norm_next is not None:
            ins.append(norm_next.reshape(1, N))
            in_specs.append(pl.BlockSpec((1, N), lambda j, i: (0, 0)))
            out_shape.append(jax.ShapeDtypeStruct((M, N), BF16))
        out = _call(body, grid=(N // tn, M // tm), in_specs=in_specs, out_specs=[tile] * len(out_shape),
                    out_shape=out_shape, ins=ins, semantics=("parallel", "parallel"), name=name, hosted=hosted)
        base = out if hosted is None else out[0]
        value = base[0] if norm_next is None else (base[0], base[1])
        return value if hosted is None else (value, out[1])
    x, gain = norm_bwd
    vec = pl.BlockSpec((1, N), lambda j, i: (0, 0))
    out = _call(
        body, grid=(1, M // tm), in_specs=in_specs + [tile, vec], out_specs=[tile, tile, vec],
        out_shape=[jax.ShapeDtypeStruct((M, N), F32), jax.ShapeDtypeStruct((M, N), BF16),
                   jax.ShapeDtypeStruct((1, N), F32)], ins=ins + [x, gain.reshape(1, N)],
        semantics=("arbitrary", "arbitrary"), name=name, hosted=hosted)
    dx, dxb, dg = out if hosted is None else out[0]
    return ((dx, dxb), dg) if hosted is None else (((dx, dxb), dg), out[1])


def mm_tn(a, b, *, a_cb=0, a_w=None, b_cb=0, b_w=None, hosted=None, name):
    S = a.shape[0]
    Ka = a.shape[1] if a_w is None else a_w
    Nb = b.shape[1] if b_w is None else b_w
    ts = _tile(S, 2 * ROW_TILE, 8)
    tk = _tile(Ka, COL_TILE)
    tn = _tile(Nb, COL_TILE)
    a_off, b_off = a_cb * (Ka // tk), b_cb * (Nb // tn)

    def body(a_ref, b_ref, o_ref):
        s = pl.program_id(2)

        @pl.when(s == 0)
        def _():
            o_ref[...] = jnp.zeros_like(o_ref)

        o_ref[...] += _dot_tn(_bf(a_ref[...]), _bf(b_ref[...]))

    out = _call(
        body, grid=(Ka // tk, Nb // tn, S // ts),
        in_specs=[pl.BlockSpec((ts, tk), lambda k, n, s: (s, a_off + k)),
                  pl.BlockSpec((ts, tn), lambda k, n, s: (s, b_off + n))],
        out_specs=[pl.BlockSpec((tk, tn), lambda k, n, s: (k, n))],
        out_shape=[jax.ShapeDtypeStruct((Ka, Nb), F32)], ins=[a, b],
        semantics=("parallel", "parallel", "arbitrary"), name=name, hosted=hosted)
    return out[0] if hosted is None else (out[0][0], out[1])


def _sigmoid(x):
    return 0.5 * jnp.tanh(0.5 * x) + 0.5


def ffn_up(h, wg, wu, name, hosted=None):
    S, D = h.shape
    Fd = wg[0].shape[-1]
    tm = _tile(S, ROW_TILE, 8)
    tn = _tile(Fd, COL_TILE)

    def body(h_ref, wg_ref, wu_ref, g_ref, u_ref, a_ref):
        hv = h_ref[...]
        g = _dot(hv, wg_ref[...])
        u = _dot(hv, wu_ref[...])
        g_ref[...] = _bf(g)
        u_ref[...] = _bf(u)
        a_ref[...] = _bf(g * _sigmoid(g) * u)

    blk = pl.BlockSpec((tm, tn), lambda j, i: (i, j))
    sh = jax.ShapeDtypeStruct((S, Fd), BF16)
    return _call(
        body, grid=(Fd // tn, S // tm),
        in_specs=[pl.BlockSpec((tm, D), lambda j, i: (i, 0)), _wspec(wg, (D, tn), lambda j, i: (0, j)),
                  _wspec(wu, (D, tn), lambda j, i: (0, j))],
        out_specs=[blk, blk, blk], out_shape=[sh, sh, sh], ins=[h, wg[0], wu[0]],
        semantics=("parallel", "parallel"), name=name, hosted=hosted)


def ffn_down_bwd(dxo, wd, g, u, a, name, hosted=None):
    S, D = dxo.shape
    Fd = wd[0].shape[-2]
    tm = _tile(S, ROW_TILE, 8)
    tn = _tile(Fd, COL_TILE)

    def body(dx_ref, wd_ref, g_ref, u_ref, a_ref, dg_ref, du_ref, dwd_ref):
        dx = _bf(dx_ref[...])
        da = _dot_nt(dx, wd_ref[...]) * 0.5
        g = g_ref[...].astype(F32)
        u = u_ref[...].astype(F32)
        sg = _sigmoid(g)
        dg_ref[...] = _bf(da * u * (sg * (1.0 + g * (1.0 - sg))))
        du_ref[...] = _bf(da * (g * sg))

        @pl.when(pl.program_id(1) == 0)
        def _():
            dwd_ref[...] = jnp.zeros_like(dwd_ref)

        dwd_ref[...] += _dot_tn(a_ref[...], dx) * 0.5

    blk = pl.BlockSpec((tm, tn), lambda j, i: (i, j))
    sh = jax.ShapeDtypeStruct((S, Fd), BF16)
    return _call(
        body, grid=(Fd // tn, S // tm),
        in_specs=[pl.BlockSpec((tm, D), lambda j, i: (i, 0)), _wspec(wd, (tn, D), lambda j, i: (j, 0)),
                  blk, blk, blk],
        out_specs=[blk, blk, pl.BlockSpec((tn, D), lambda j, i: (j, 0))],
        out_shape=[sh, sh, jax.ShapeDtypeStruct((Fd, D), F32)], ins=[dxo, wd[0], g, u, a],
        semantics=("parallel", "arbitrary"), name=name, hosted=hosted)


_GELU_C = math.sqrt(2.0 / math.pi)


def _gelu(p):
    return 0.5 * p * (1.0 + jnp.tanh(_GELU_C * (p + 0.044715 * (p * p * p))))


def _gelu_grad(p):
    th = jnp.tanh(_GELU_C * (p + 0.044715 * (p * p * p)))
    return 0.5 * (1.0 + th) + 0.5 * p * (1.0 - th * th) * (_GELU_C * (1.0 + 3.0 * 0.044715 * (p * p)))


def _chunk_mask():
    t = lax.broadcasted_iota(jnp.int32, (GM_CHUNK, GM_CHUNK), 0)
    s = lax.broadcasted_iota(jnp.int32, (GM_CHUNK, GM_CHUNK), 1)
    return (s // CHUNK) <= (t // CHUNK)


def gmlp_fwd(proj, v_gain, w_s, b_mat, name):
    S = proj.shape[0]
    tw = _tile(S, ROW_TILE, GM_CHUNK)
    n_win = tw // GM_CHUNK

    def body(p_ref, gain_ref, w_ref, b_ref, y_ref):
        mask = _chunk_mask()
        v = _gelu(p_ref[:, GM_W:])
        r = lax.rsqrt(jnp.mean(v * v, axis=-1, keepdims=True) + EPS)
        vn = _bf(v * r * gain_ref[...])
        for g in range(GM_GROUPS):
            wm = _bf(jnp.where(mask, w_ref[g], 0.0))
            cs = slice(g * GM_CHUNK, (g + 1) * GM_CHUNK)
            for w in range(n_win):
                rs = slice(w * GM_CHUNK, (w + 1) * GM_CHUNK)
                mixed = _dot(wm, vn[rs, cs]) + b_ref[g]
                y_ref[rs, cs] = _bf(_gelu(p_ref[rs, cs]) * mixed)

    return pl.pallas_call(
        body, grid=(S // tw,),
        in_specs=[pl.BlockSpec((tw, 2 * GM_W), lambda i: (i, 0)), pl.BlockSpec((1, GM_W), lambda i: (0, 0)),
                  pl.BlockSpec((GM_GROUPS, GM_CHUNK, GM_CHUNK), lambda i: (0, 0, 0)),
                  pl.BlockSpec((GM_GROUPS, GM_CHUNK, GM_CHUNK), lambda i: (0, 0, 0))],
        out_specs=pl.BlockSpec((tw, GM_W), lambda i: (i, 0)),
        out_shape=jax.ShapeDtypeStruct((S, GM_W), BF16),
        compiler_params=_cparams(("parallel",)), name=name)(proj, v_gain.reshape(1, GM_W), w_s, b_mat)


def gmlp_bwd(proj, dy, v_gain, w_s, b_mat, name):
    S = proj.shape[0]
    tw = _tile(S, ROW_TILE, GM_CHUNK)
    n_win = tw // GM_CHUNK

    def body(p_ref, dy_ref, gain_ref, w_ref, b_ref, dp_ref, dw_ref, db_ref, dgain_ref, dvn_ref):
        step = pl.program_id(0)

        @pl.when(step == 0)
        def _():
            dw_ref[...] = jnp.zeros_like(dw_ref)
            db_ref[...] = jnp.zeros_like(db_ref)
            dgain_ref[...] = jnp.zeros_like(dgain_ref)

        mask = _chunk_mask()
        pv = p_ref[:, GM_W:]
        v = _gelu(pv)
        r = lax.rsqrt(jnp.mean(v * v, axis=-1, keepdims=True) + EPS)
        vhat = v * r
        gain = gain_ref[...]
        vn = _bf(vhat * gain)
        for g in range(GM_GROUPS):
            wm = _bf(jnp.where(mask, w_ref[g], 0.0))
            cs = slice(g * GM_CHUNK, (g + 1) * GM_CHUNK)
            dw_acc = jnp.zeros((GM_CHUNK, GM_CHUNK), F32)
            db_acc = jnp.zeros((GM_CHUNK, GM_CHUNK), F32)
            for w in range(n_win):
                rs = slice(w * GM_CHUNK, (w + 1) * GM_CHUNK)
                pu = p_ref[rs, cs]
                u = _gelu(pu)
                vn_blk = vn[rs, cs]
                mixed = _dot(wm, vn_blk) + b_ref[g]
                dyb = dy_ref[rs, cs]
                dp_ref[rs, cs] = _bf(dyb * mixed * _gelu_grad(pu))
                dmix = dyb * u
                db_acc = db_acc + dmix
                dmix_b = _bf(dmix)
                dw_acc = dw_acc + _dot_nt(dmix_b, vn_blk)
                dvn_ref[rs, cs] = _dot_tn(wm, dmix_b)
            dw_ref[g] += jnp.where(mask, dw_acc, 0.0)
            db_ref[g] += jnp.broadcast_to(jnp.sum(db_acc, axis=-1, keepdims=True), (GM_CHUNK, GM_CHUNK))
        dvn = dvn_ref[...]
        dgain_ref[...] += jnp.sum(dvn * vhat, axis=0, keepdims=True)
        dvhat = dvn * gain
        m = jnp.mean(dvhat * vhat, axis=-1, keepdims=True)
        dv = r * (dvhat - vhat * m)
        dp_ref[:, GM_W:] = _bf(dv * _gelu_grad(pv))

    sq = pl.BlockSpec((GM_GROUPS, GM_CHUNK, GM_CHUNK), lambda i: (0, 0, 0))
    sq_sh = jax.ShapeDtypeStruct((GM_GROUPS, GM_CHUNK, GM_CHUNK), F32)
    return pl.pallas_call(
        body, grid=(S // tw,),
        in_specs=[pl.BlockSpec((tw, 2 * GM_W), lambda i: (i, 0)), pl.BlockSpec((tw, GM_W), lambda i: (i, 0)),
                  pl.BlockSpec((1, GM_W), lambda i: (0, 0)), sq, sq],
        out_specs=[pl.BlockSpec((tw, 2 * GM_W), lambda i: (i, 0)), sq, sq,
                   pl.BlockSpec((1, GM_W), lambda i: (0, 0))],
        out_shape=[jax.ShapeDtypeStruct((S, 2 * GM_W), BF16), sq_sh, sq_sh,
                   jax.ShapeDtypeStruct((1, GM_W), F32)],
        scratch_shapes=[pltpu.VMEM((tw, GM_W), F32)],
        compiler_params=_cparams(("arbitrary",)), name=name)(proj, dy, v_gain.reshape(1, GM_W), w_s, b_mat)


def _head_mask(h, width):
    lane = lax.broadcasted_iota(jnp.int32, (1, width), 1)
    return (lane >= HEAD_DIM * h) & (lane < HEAD_DIM * (h + 1))


def _mem_probs(q, k, h):
    kh = jnp.where(_head_mask(h, MEM_W), k, jnp.zeros_like(k))
    s = _dot_nt(q, kh) * (HEAD_DIM ** -0.5)
    s = s - jnp.max(s, axis=-1, keepdims=True)
    e = jnp.exp(s)
    return e * (1.0 / jnp.sum(e, axis=-1, keepdims=True)), kh


def memattn_fwd(proj, q_cb, mem_kv, name):
    S = proj.shape[0]
    NM = mem_kv.shape[0]
    tm = _tile(S, ROW_TILE, 8)

    def body(q_ref, kv_ref, o_ref):
        q = _bf(q_ref[...])
        k = _bf(kv_ref[:, :MEM_W])
        v = _bf(kv_ref[:, MEM_W:])
        acc = jnp.zeros((tm, MEM_W), F32)
        for h in range(MEM_HEADS):
            p, _ = _mem_probs(q, k, h)
            vh = jnp.where(_head_mask(h, MEM_W), v, jnp.zeros_like(v))
            acc = acc + _dot(_bf(p), vh)
        o_ref[...] = _bf(acc)

    return pl.pallas_call(
        body, grid=(S // tm,),
        in_specs=[pl.BlockSpec((tm, MEM_W), lambda i: (i, q_cb)), pl.BlockSpec((NM, 2 * MEM_W), lambda i: (0, 0))],
        out_specs=pl.BlockSpec((tm, MEM_W), lambda i: (i, 0)),
        out_shape=jax.ShapeDtypeStruct((S, MEM_W), BF16),
        compiler_params=_cparams(("parallel",)), name=name)(proj, mem_kv)


def memattn_bwd(proj, q_cb, mem_kv, dy, dy_cb, name):
    S = proj.shape[0]
    NM = mem_kv.shape[0]
    tm = _tile(S, ROW_TILE, 8)

    def body(q_ref, kv_ref, do_ref, dq_ref, dkv_ref):
        @pl.when(pl.program_id(0) == 0)
        def _():
            dkv_ref[...] = jnp.zeros_like(dkv_ref)

        q = _bf(q_ref[...])
        k = _bf(kv_ref[:, :MEM_W])
        v = _bf(kv_ref[:, MEM_W:])
        do = _bf(do_ref[...])
        dq = jnp.zeros((tm, MEM_W), F32)
        dk = jnp.zeros((NM, MEM_W), F32)
        dv = jnp.zeros((NM, MEM_W), F32)
        for h in range(MEM_HEADS):
            hm = _head_mask(h, MEM_W)
            p, kh = _mem_probs(q, k, h)
            vh = jnp.where(hm, v, jnp.zeros_like(v))
            dp = _dot_nt(do, vh)
            ds = _bf(p * (dp - jnp.sum(p * dp, axis=-1, keepdims=True)) * (HEAD_DIM ** -0.5))
            dq = dq + _dot(ds, kh)
            dk = dk + jnp.where(hm, _dot_tn(ds, q), 0.0)
            dv = dv + jnp.where(hm, _dot_tn(_bf(p), do), 0.0)
        dq_ref[...] = _bf(dq)
        dkv_ref[:, :MEM_W] += dk
        dkv_ref[:, MEM_W:] += dv

    return pl.pallas_call(
        body, grid=(S // tm,),
        in_specs=[pl.BlockSpec((tm, MEM_W), lambda i: (i, q_cb)), pl.BlockSpec((NM, 2 * MEM_W), lambda i: (0, 0)),
                  pl.BlockSpec((tm, MEM_W), lambda i: (i, dy_cb))],
        out_specs=[pl.BlockSpec((tm, MEM_W), lambda i: (i, 0)), pl.BlockSpec((NM, 2 * MEM_W), lambda i: (0, 0))],
        out_shape=[jax.ShapeDtypeStruct((S, MEM_W), BF16), jax.ShapeDtypeStruct((NM, 2 * MEM_W), F32)],
        compiler_params=_cparams(("arbitrary",)), name=name)(proj, mem_kv, dy)


def _split_bf(v):
    hi = _bf(v)
    return hi, _bf(v - hi.astype(F32))


def _sb_scores(qp, k2, dpos, row, col, tri):
    z = _dot_nt(qp, k2) * (HEAD_DIM ** -0.5)
    t = jnp.log(1.0 + jnp.exp(-jnp.abs(z)))
    lb = jnp.minimum(z, 0.0) - t
    lom = jnp.where(_sb_mask(dpos, row, col), -jnp.maximum(z, 0.0) - t, 0.0)
    l_hi, l_lo = _split_bf(lom)
    insuf = _dot(l_hi, tri) + _dot(l_lo, tri)
    return lb + insuf, jnp.sum(lom, axis=1, keepdims=True), lb


def _sb_mask(dpos, row, col):
    return (col - row) < dpos


def _sb_weights(s, run, dpos, row, col):
    return jnp.where(_sb_mask(dpos, row, col), jnp.exp(s + run), 0.0)


def _sb_alive(runs):
    top = runs[0]
    for r in runs[1:]:
        top = jnp.maximum(top, r)
    return jnp.max(top) > SB_DEAD


def _sb_consts():
    row = lax.broadcasted_iota(jnp.int32, (2 * SB_BLOCK, SB_BLOCK), 0) % SB_BLOCK
    col = lax.broadcasted_iota(jnp.int32, (2 * SB_BLOCK, SB_BLOCK), 1)
    r = lax.broadcasted_iota(jnp.int32, (SB_BLOCK, SB_BLOCK), 0)
    c = lax.broadcasted_iota(jnp.int32, (SB_BLOCK, SB_BLOCK), 1)
    return row, col, _bf(jnp.where(r > c, 1.0, 0.0)), _bf(jnp.where(r >= c, 1.0, 0.0))


def _stack_heads(x2, masks):
    return jnp.concatenate([_bf(jnp.where(hm, x2, jnp.zeros_like(x2))) for hm in masks], axis=0)


def _side_by_side(x):
    return jnp.concatenate([x[:SB_BLOCK], x[SB_BLOCK:]], axis=1)


def sb_fwd(proj, kv, name):
    S = proj.shape[0]
    n_g = GM_W // SB_COLS
    tq = _tile(S, SB_Q_CHUNK, SB_BLOCK)
    nqb = tq // SB_BLOCK

    def body(q_ref, k_ref, v_ref, o_ref, s_ref):
        chunk = pl.program_id(1)
        row, col, tri, _ = _sb_consts()
        masks = [_head_mask(h, LANES) for h in range(2)]
        pairs = [slice(p * LANES, (p + 1) * LANES) for p in range(SB_PAIRS)]

        def q_loop(qi, _):
            i = chunk * nqb + qi
            rows = pl.ds(pl.multiple_of(qi * SB_BLOCK, SB_BLOCK), SB_BLOCK)
            qps = [_stack_heads(q_ref[rows, cs], masks) for cs in pairs]

            def look_ahead(t):
                keys = pl.ds(pl.multiple_of(jnp.maximum(i - t, 0) * SB_BLOCK, SB_BLOCK), SB_BLOCK)
                sums = []
                for p, cs in enumerate(pairs):
                    s, lom_sum, _ = _sb_scores(qps[p], k_ref[keys, cs], t * SB_BLOCK, row, col, tri)
                    s_ref[p] = s
                    sums.append(lom_sum)
                return tuple(sums)

            def k_step(carry):
                t, accs, runs, sums, _ = carry
                keys = pl.ds(pl.multiple_of((i - t) * SB_BLOCK, SB_BLOCK), SB_BLOCK)
                new_accs, new_runs = [], []
                for p, cs in enumerate(pairs):
                    a = _sb_weights(s_ref[p], runs[p], t * SB_BLOCK, row, col)
                    a_hi, a_lo = _split_bf(_side_by_side(a))
                    vp = _stack_heads(v_ref[keys, cs], masks)
                    new_accs.append(accs[p] + _dot(a_hi, vp) + _dot(a_lo, vp))
                    new_runs.append(runs[p] + sums[p])
                return t + 1, tuple(new_accs), tuple(new_runs), look_ahead(t + 1), _sb_alive(new_runs)

            zero = jnp.zeros((2 * SB_BLOCK, 1), F32)
            _, accs, _, _, _ = lax.while_loop(
                lambda carry: jnp.logical_and(carry[0] <= i, carry[4]), k_step,
                (jnp.int32(0), (jnp.zeros((SB_BLOCK, LANES), F32),) * SB_PAIRS, (zero,) * SB_PAIRS,
                 look_ahead(jnp.int32(0)), jnp.bool_(True)))
            for p, cs in enumerate(pairs):
                o_ref[rows, cs] = accs[p]
            return 0

        lax.fori_loop(0, nqb, q_loop, 0)

    return pl.pallas_call(
        body, grid=(n_g, S // tq),
        in_specs=[pl.BlockSpec((tq, SB_COLS), lambda g, c: (c, g)),
                  pl.BlockSpec((S, SB_COLS), lambda g, c: (0, g)),
                  pl.BlockSpec((S, SB_COLS), lambda g, c: (0, n_g + g))],
        out_specs=pl.BlockSpec((tq, SB_COLS), lambda g, c: (c, g)),
        out_shape=jax.ShapeDtypeStruct((S, GM_W), F32),
        scratch_shapes=[pltpu.VMEM((SB_PAIRS, 2 * SB_BLOCK, LANES), F32)],
        compiler_params=_cparams(("parallel", "parallel")), name=name)(proj, kv, kv)


def sb_bwd(proj, kv, dy, out, dk_init, dv_init, name):
    S = proj.shape[0]
    n_g = GM_W // SB_COLS
    tq = _tile(S, SB_Q_CHUNK, SB_BLOCK)
    nqb = tq // SB_BLOCK
    n_chunks = S // tq
    has_init = dk_init is not None
    scale = HEAD_DIM ** -0.5

    def body(*refs):
        s_ref, beta_ref, da_ref = refs[-3:]
        if has_init:
            q_ref, k_ref, v_ref, do_ref, out_ref, dki_ref, dvi_ref, dq_ref, dko_ref, dvo_ref, dk_acc, dv_acc = refs[:-3]
        else:
            q_ref, k_ref, v_ref, do_ref, out_ref, dq_ref, dko_ref, dvo_ref, dk_acc, dv_acc = refs[:-3]
        chunk = pl.program_id(1)
        cols = pl.ds(pl.multiple_of(pl.program_id(0) * SB_COLS, SB_COLS), SB_COLS)

        @pl.when(chunk == 0)
        def _():
            if has_init:
                pltpu.sync_copy(dki_ref.at[:, cols], dk_acc)
                pltpu.sync_copy(dvi_ref.at[:, cols], dv_acc)
            else:
                dk_acc[...] = jnp.zeros_like(dk_acc)
                dv_acc[...] = jnp.zeros_like(dv_acc)

        row, col, tri, tri_inc = _sb_consts()
        masks = [_head_mask(h, LANES) for h in range(2)]
        pairs = [slice(p * LANES, (p + 1) * LANES) for p in range(SB_PAIRS)]

        def q_loop(qi, _):
            i = chunk * nqb + qi
            rows = pl.ds(pl.multiple_of(qi * SB_BLOCK, SB_BLOCK), SB_BLOCK)
            qps = [_stack_heads(q_ref[rows, cs], masks) for cs in pairs]
            dops = [_stack_heads(do_ref[rows, cs], masks) for cs in pairs]
            e_tots = [jnp.sum(dop.astype(F32) * jnp.concatenate([out_ref[rows, cs]] * 2, axis=0), axis=1,
                              keepdims=True) for dop, cs in zip(dops, pairs)]

            def look_ahead(t):
                keys = pl.ds(pl.multiple_of(jnp.maximum(i - t, 0) * SB_BLOCK, SB_BLOCK), SB_BLOCK)
                sums = []
                for p, cs in enumerate(pairs):
                    s, lom_sum, lb = _sb_scores(qps[p], k_ref[keys, cs], t * SB_BLOCK, row, col, tri)
                    s_ref[p] = s
                    beta_ref[p] = jnp.exp(lb)
                    da_ref[p] = _dot_nt(dops[p], v_ref[keys, cs])
                    sums.append(lom_sum)
                return tuple(sums)

            def k_step(carry):
                t, dqs, runs, e_runs, sums, _ = carry
                keys = pl.ds(pl.multiple_of((i - t) * SB_BLOCK, SB_BLOCK), SB_BLOCK)
                new_dqs, new_runs, new_e_runs = [], [], []
                for p, cs in enumerate(pairs):
                    a = _sb_weights(s_ref[p], runs[p], t * SB_BLOCK, row, col)
                    e = a * da_ref[p]
                    e_hi, e_lo = _split_bf(e)
                    before = e_tots[p] - e_runs[p] - (_dot(e_hi, tri_inc) + _dot(e_lo, tri_inc))
                    beta = beta_ref[p]
                    dz = jnp.where(_sb_mask(t * SB_BLOCK, row, col), e * (1.0 - beta) - before * beta, 0.0)
                    dz = _bf(dz * scale)
                    new_dqs.append(dqs[p] + _dot(_side_by_side(dz), _stack_heads(k_ref[keys, cs], masks)))
                    dk_acc[keys, cs] += _dot_tn(dz, qps[p])
                    dv_acc[keys, cs] += _dot_tn(_bf(a), dops[p])
                    new_runs.append(runs[p] + sums[p])
                    new_e_runs.append(e_runs[p] + jnp.sum(e, axis=1, keepdims=True))
                return (t + 1, tuple(new_dqs), tuple(new_runs), tuple(new_e_runs), look_ahead(t + 1),
                        _sb_alive(new_runs))

            zero = jnp.zeros((2 * SB_BLOCK, 1), F32)
            zeros = (zero,) * SB_PAIRS
            _, dqs, _, _, _, _ = lax.while_loop(
                lambda carry: jnp.logical_and(carry[0] <= i, carry[5]), k_step,
                (jnp.int32(0), (jnp.zeros((SB_BLOCK, LANES), F32),) * SB_PAIRS, zeros, zeros,
                 look_ahead(jnp.int32(0)), jnp.bool_(True)))
            for p, cs in enumerate(pairs):
                dq_ref[rows, cs] = dqs[p]
            return 0

        lax.fori_loop(0, nqb, q_loop, 0)

        @pl.when(chunk == n_chunks - 1)
        def _():
            pltpu.sync_copy(dk_acc, dko_ref.at[:, cols])
            pltpu.sync_copy(dv_acc, dvo_ref.at[:, cols])

    qspec = pl.BlockSpec((tq, SB_COLS), lambda g, c: (c, g))
    kspec = pl.BlockSpec((S, SB_COLS), lambda g, c: (0, g))
    in_specs = [qspec, kspec, pl.BlockSpec((S, SB_COLS), lambda g, c: (0, n_g + g)), qspec, qspec]
    ins = [proj, kv, kv, dy, out]
    if has_init:
        in_specs += [_ANY, _ANY]
        ins += [dk_init, dv_init]
    sh = jax.ShapeDtypeStruct((S, GM_W), F32)
    return pl.pallas_call(
        body, grid=(n_g, n_chunks), in_specs=in_specs, out_specs=[qspec, _ANY, _ANY],
        out_shape=[sh, sh, sh],
        scratch_shapes=[pltpu.VMEM((S, SB_COLS), F32), pltpu.VMEM((S, SB_COLS), F32)]
        + [pltpu.VMEM((SB_PAIRS, 2 * SB_BLOCK, LANES), F32)] * 3,
        compiler_params=_cparams(("arbitrary", "arbitrary")), name=name)(*ins)


def adamw(w, g, m, v, name):
    shape = w.shape
    C = shape[-1] if w.ndim > 1 else shape[0]
    R = w.size // C
    tr = _tile(R, ROW_TILE, 8)

    def body(w_ref, g_ref, m_ref, v_ref, d_ref, nm_ref, nv_ref):
        gv = g_ref[...]
        m2 = ADAM_B1 * m_ref[...] + (1.0 - ADAM_B1) * gv
        v2 = ADAM_B2 * v_ref[...] + (1.0 - ADAM_B2) * (gv * gv)
        m_hat = m2 / (1.0 - ADAM_B1 ** ADAM_STEP)
        v_hat = v2 / (1.0 - ADAM_B2 ** ADAM_STEP)
        d_ref[...] = -ADAM_LR * (m_hat / (jnp.sqrt(v_hat) + ADAM_EPS) + ADAM_WD * w_ref[...])
        nm_ref[...] = m2
        nv_ref[...] = v2

    blk = pl.BlockSpec((tr, C), lambda i: (i, 0))
    sh = jax.ShapeDtypeStruct((R, C), F32)
    outs = pl.pallas_call(
        body, grid=(R // tr,), in_specs=[blk] * 4, out_specs=[blk] * 3, out_shape=[sh] * 3,
        compiler_params=_cparams(("parallel",)), name=name)(
            w.reshape(R, C), g.reshape(R, C), m.reshape(R, C), v.reshape(R, C))
    return tuple(o.reshape(shape) for o in outs)


def _place():
    return lax.axis_index("x"), lax.axis_index("y"), lax.axis_index("c")


def _other_chips(x, y):
    return [(1 - x, y), (x, 1 - y), (1 - x, 1 - y)]


_ANY = pl.BlockSpec(memory_space=pl.ANY)
LOCAL_CHUNKS = 8
SEG_LOCAL_CHUNKS = 4


def _remote(src, dst, send_sems, recv_sems, k, to):
    return pltpu.make_async_remote_copy(src_ref=src, dst_ref=dst, send_sem=send_sems.at[k], recv_sem=recv_sems.at[k],
                                        device_id=to, device_id_type=MESH)


def _gather_stage(segs, n_gath, stage):
    def make(ins, outs, send_sems, recv_sems, *local_sems):
        x, y, c = _place()
        sibling = (x, y, 1 - c)
        chips = _other_chips(x, y)
        copies, waits = [], []
        for s, (b, r0, n) in enumerate(segs):
            half = n // 2

            def piece(px, py, pc, b=b, r0=r0, half=half):
                return outs[b].at[2 * px + py, pl.ds(r0 + pc * half, half), :]

            for j, chip in enumerate(chips):
                if stage == 0:
                    mine = ins[n_gath + b].at[pl.ds(r0 + c * half, half), :]
                    cp = _remote(mine, piece(x, y, c), send_sems, recv_sems, 3 * s + j, (*chip, c))
                    landed, sender = piece(*chip, c), (*chip, c)
                else:
                    cp = _remote(piece(*chip, c), piece(*chip, c), send_sems, recv_sems, 3 * s + j, sibling)
                    landed, sender = piece(*chip, 1 - c), sibling
                copies.append(cp)
                waits.append(cp.wait_send)
                waits.append(_remote(landed, landed, send_sems, recv_sems, 3 * s + j, sender).wait_recv)
            if stage == 0:
                rows = n // SEG_LOCAL_CHUNKS
                for q in range(SEG_LOCAL_CHUNKS):
                    r = pl.ds(r0 + q * rows, rows)
                    cp = pltpu.make_async_copy(ins[n_gath + b].at[r, :], outs[b].at[2 * x + y, r, :],
                                               local_sems[0].at[SEG_LOCAL_CHUNKS * s + q])
                    copies.append(cp)
                    waits.append(cp.wait)
        return copies, waits

    return make


def _gather_sems(segs, stage):
    return [3 * len(segs), 3 * len(segs)] + ([SEG_LOCAL_CHUNKS * len(segs)] if stage == 0 else [])


def all_gather_segments(bufs, segs, name):
    nb = len(bufs)
    n0, n1 = len(_gather_sems(segs, 0)), len(_gather_sems(segs, 1))

    def body(*refs):
        in_refs, out_refs = refs[:nb], refs[nb:2 * nb]
        sems = refs[2 * nb:]
        for stage, stage_sems in ((0, sems[:n0]), (1, sems[n0:n0 + n1])):
            copies, waits = _gather_stage(segs, nb, stage)(list(out_refs) + list(in_refs), out_refs, *stage_sems)
            for cp in copies:
                cp.start()
            for wait in waits:
                wait()

    return pl.pallas_call(
        body, in_specs=[_ANY] * nb, out_specs=[_ANY] * nb,
        out_shape=[jax.ShapeDtypeStruct((N_CHIPS,) + b.shape, b.dtype) for b in bufs],
        scratch_shapes=[pltpu.SemaphoreType.DMA((n,)) for n in _gather_sems(segs, 0) + _gather_sems(segs, 1)],
        name=name)(*bufs)


def hosted_gather(gathered, bufs, segs, stage):
    arrays = list(gathered) + (list(bufs) if stage == 0 else [])
    return Hosted(arrays, len(gathered), _gather_sems(segs, stage), _gather_stage(segs, len(gathered), stage))


def _exchange(step, arrays, name):
    n_h, n_alias, n_fresh = len(step.arrays), step.n_out, len(step.fresh)

    def body(*refs):
        h_in = refs[:n_h]
        h_out = refs[n_h:n_h + n_alias + n_fresh]
        copies, waits = step.make(h_in, h_out, *refs[n_h + n_alias + n_fresh:])
        for cp in copies:
            cp.start()
        for wait in waits:
            wait()

    return pl.pallas_call(
        body, in_specs=[_ANY] * n_h, out_specs=[_ANY] * (n_alias + n_fresh),
        out_shape=[jax.ShapeDtypeStruct(a.shape, a.dtype) for a in step.arrays[:n_alias]] + step.fresh,
        scratch_shapes=[pltpu.SemaphoreType.DMA((n,)) for n in step.sems],
        input_output_aliases={k: k for k in range(n_alias)}, name=name)(*arrays)


def swap_step(gs):
    nb = len(gs)

    def make(ins, outs, send_sems, recv_sems):
        x, y, c = _place()
        copies = []
        for b in range(nb):
            half = gs[b].shape[1] // 2
            for k in range(N_CHIPS):
                copies.append(_remote(ins[b].at[k, pl.ds((1 - c) * half, half), :], outs[b].at[k], send_sems,
                                      recv_sems, N_CHIPS * b + k, (x, y, 1 - c)))
        return copies, [cp.wait for cp in copies]

    fresh = [jax.ShapeDtypeStruct((N_CHIPS, g.shape[1] // 2, g.shape[2]), g.dtype) for g in gs]
    return Hosted(list(gs), 0, [N_CHIPS * nb, N_CHIPS * nb], make, fresh)


def _core_index():
    return lax.axis_index("c").astype(jnp.int32).reshape(1)


def add_cores(g, theirs, name):
    n, Ph, Wd = theirs.shape
    tr = _tile(Ph, ROW_TILE, 16)
    steps = Ph // tr

    def body(c_ref, g_ref, t_ref, o_ref):
        o_ref[...] = _bf(g_ref[...] + t_ref[...])

    blk = pl.BlockSpec((None, tr, Wd), lambda k, i, c_ref: (k, i, 0))
    return pl.pallas_call(
        body,
        grid_spec=pltpu.PrefetchScalarGridSpec(
            num_scalar_prefetch=1, grid=(n, steps),
            in_specs=[pl.BlockSpec((None, tr, Wd), lambda k, i, c_ref: (k, c_ref[0] * steps + i, 0)), blk],
            out_specs=blk),
        out_shape=jax.ShapeDtypeStruct((n, Ph, Wd), BF16),
        compiler_params=_cparams(("parallel", "parallel")), name=name)(_core_index(), g, theirs)


def add_chips(slots, name):
    n, Ph, Wd = slots.shape
    tr = _tile(Ph, ROW_TILE, 16)
    steps = Ph // tr

    def body(c_ref, *refs):
        acc = refs[0][...].astype(F32)
        for k in range(1, n):
            acc = acc + refs[k][...].astype(F32)
        refs[n][...] = acc

    in_specs = [pl.BlockSpec((None, tr, Wd), functools.partial(lambda i, c_ref, k: (k, i, 0), k=k)) for k in range(n)]
    return pl.pallas_call(
        body,
        grid_spec=pltpu.PrefetchScalarGridSpec(
            num_scalar_prefetch=1, grid=(steps,), in_specs=in_specs,
            out_specs=pl.BlockSpec((tr, Wd), lambda i, c_ref: (c_ref[0] * steps + i, 0))),
        out_shape=jax.ShapeDtypeStruct((2 * Ph, Wd), F32),
        compiler_params=_cparams(("parallel",)), name=name)(_core_index(), *([slots] * n))


def _whole_tile_chunks(rows, most, tile_rows):
    return max(n for n in range(1, most + 1) if rows % (n * tile_rows) == 0)


def scatter_step(hs):
    nb = len(hs)
    chunks = [_whole_tile_chunks(h.shape[1], LOCAL_CHUNKS, 16) for h in hs]

    def make(ins, outs, send_sems, recv_sems, local_sems):
        x, y, c = _place()
        me = 2 * x + y
        copies, waits = [], []
        for b in range(nb):
            for j, (cx, cy) in enumerate(_other_chips(x, y)):
                cp = _remote(ins[b].at[2 * cx + cy], outs[b].at[me], send_sems, recv_sems, 3 * b + j, (cx, cy, c))
                copies.append(cp)
                waits.append(cp.wait_send)
                landed = outs[b].at[2 * cx + cy]
                waits.append(_remote(landed, landed, send_sems, recv_sems, 3 * b + j, (cx, cy, c)).wait_recv)
            rows = hs[b].shape[1] // chunks[b]
            for q in range(chunks[b]):
                r = pl.ds(q * rows, rows)
                cp = pltpu.make_async_copy(ins[b].at[me, r, :], outs[b].at[me, r, :],
                                           local_sems.at[sum(chunks[:b]) + q])
                copies.append(cp)
                waits.append(cp.wait)
        return copies, waits

    fresh = [jax.ShapeDtypeStruct(h.shape, h.dtype) for h in hs]
    return Hosted(list(hs), 0, [3 * nb, 3 * nb, sum(chunks)], make, fresh)


JOIN_CHUNKS = 4


def join_step(ts):
    nb = len(ts)
    n = JOIN_CHUNKS

    def make(ins, outs, send_sems, recv_sems):
        x, y, c = _place()
        copies, waits = [], []
        for b in range(nb):
            half = ts[b].shape[0] // 2
            rows = half // n
            for k in range(n):
                part = outs[b].at[pl.ds(c * half + k * rows, rows), :]
                cp = _remote(part, part, send_sems, recv_sems, n * b + k, (x, y, 1 - c))
                copies.append(cp)
                waits.append(cp.wait_send)
                landed = outs[b].at[pl.ds((1 - c) * half + k * rows, rows), :]
                waits.append(_remote(landed, landed, send_sems, recv_sems, n * b + k, (x, y, 1 - c)).wait_recv)
        return copies, waits

    return Hosted(list(ts), nb, [n * nb, n * nb], make)


WIDE = ['ffn1_w_gate', 'ffn1_w_up', 'ffn2_w_gate', 'ffn2_w_up']


def _rows_of(p):
    if p.shape[-1] == PACK_W and (p.size // PACK_W) % PART_ROWS == 0:
        return p.reshape(-1, PACK_W)
    flat = p.reshape(-1)
    rows = -(-flat.shape[0] // (PACK_W * PART_ROWS)) * PART_ROWS
    return jnp.pad(flat, (0, rows * PACK_W - flat.shape[0])).reshape(rows, PACK_W)


def _pack(parts):
    buf = jnp.concatenate([_rows_of(p) for p in parts], axis=0)
    rows = buf.shape[0]
    total = -(-rows // PACK_ROWS) * PACK_ROWS
    return jnp.pad(buf, ((0, total - rows), (0, 0)))


def _unpack(buf, shapes):
    outs, r = [], 0
    lead = buf.shape[:-2]
    for shp in shapes:
        n = math.prod(shp)
        rows = -(-n // (PACK_W * PART_ROWS)) * PART_ROWS
        blk = buf[..., r:r + rows, :]
        if n != rows * PACK_W:
            blk = blk.reshape(lead + (rows * PACK_W,))[..., :n]
        outs.append(blk.reshape(lead + tuple(shp)))
        r += rows
    return outs


class WeightStream:
    def __init__(self, shards):
        depth = shards['ffn1_w_gate'].shape[0]
        order = [(which, l) for l in range(depth) for which in ('ffn1', 'ffn2')]
        self.count = len(order)
        gate_up = [shards[f'{which}_w_{part}'][l].astype(BF16) for which, l in order for part in ('gate', 'up')]
        downs = [shards[f'{which}_w_down'][l].astype(BF16) for which, l in order]
        self.d, self.cols = gate_up[0].shape
        assert downs[0].shape == (self.cols, PACK_W)
        self.rest_names = [n for n in W_NAMES if n in SHARD_AXIS and n not in WIDE and not n.endswith('_w_down')]
        rest = [lax.bitcast_convert_type(shards[n], BF16) if n == 'a_v_norm' else shards[n].astype(BF16)
                for n in self.rest_names]
        self.rest_shapes = [p.shape for p in rest]
        self.bufs = [jnp.concatenate(gate_up, axis=0), _pack(downs + rest)]
        self.rest_row0 = self.count * self.cols
        first = self.segments(0) + [(1, self.rest_row0, self.bufs[1].shape[0] - self.rest_row0)]
        self.gathered = all_gather_segments(self.bufs, first, "all_gather_first")

    def segments(self, i):
        return [(0, 2 * self.d * i, 2 * self.d), (1, self.cols * i, self.cols)]

    def rest(self):
        out = {}
        for n, g in zip(self.rest_names, _unpack(self.gathered[1][:, self.rest_row0:], self.rest_shapes)):
            if n == 'a_v_norm':
                g = lax.bitcast_convert_type(g, F32)
            out[n] = jnp.concatenate([g[k] for k in range(N_CHIPS)], axis=SHARD_AXIS[n])
        return out

    def ffn(self, i):
        wide, narrow = self.gathered
        r = 2 * self.d * i
        gate = jnp.concatenate([wide[k, r:r + self.d] for k in range(N_CHIPS)], axis=1)
        up = jnp.concatenate([wide[k, r + self.d:r + 2 * self.d] for k in range(N_CHIPS)], axis=1)
        down = narrow[:, self.cols * i:self.cols * (i + 1)].reshape(N_CHIPS * self.cols, PACK_W)
        return (gate, None), (up, None), (down, None)

    def hosted(self, i, stage):
        if i >= self.count:
            return None
        return hosted_gather(self.gathered, self.bufs, self.segments(i), stage)

    def absorb(self, gathered):
        self.gathered = list(gathered)


class GradSink:
    def __init__(self):
        self.queue = []
        self.totals = {}
        self.held = {}

    def add(self, i, d_wg, d_wu, d_wd, ride):
        cols = d_wg.shape[1] // N_CHIPS
        wide = jnp.stack([jnp.concatenate([d_wg[:, k * cols:(k + 1) * cols], d_wu[:, k * cols:(k + 1) * cols]],
                                          axis=0) for k in range(N_CHIPS)], axis=0)
        bufs = [wide, d_wd.reshape(N_CHIPS, cols, d_wd.shape[1])]
        if ride:
            self.queue.append([0, bufs, i])
        else:
            self.held[i] = bufs

    def step(self):
        if not self.queue:
            return None
        stage, bufs, _ = self.queue[0]
        return (swap_step, scatter_step, join_step)[stage](bufs)

    def absorb(self, results):
        entry = self.queue[0]
        stage, bufs, i = entry
        if stage == 0:
            entry[1] = [add_cores(g, t, f"ffn{i}_add_cores_{b}") for b, (g, t) in enumerate(zip(bufs, results))]
        elif stage == 1:
            entry[1] = [add_chips(s, f"ffn{i}_add_chips_{b}") for b, s in enumerate(results)]
        else:
            self.totals[i] = list(results)
            self.queue.pop(0)
        entry[0] = stage + 1

    def reduce(self, grads, shard_shapes):
        assert not self.queue
        narrow = [n for n in W_NAMES if n in SHARD_AXIS and n in grads]
        repl = [n for n in W_NAMES if n not in SHARD_AXIS]
        full = {n: (jnp.stack(g, axis=0) if isinstance(g, list) else g) for n, g in grads.items()}
        blocks = []
        for k in range(N_CHIPS):
            parts = [jnp.split(full[n], N_CHIPS, axis=SHARD_AXIS[n])[k] for n in narrow]
            parts += [full[n] for n in repl]
            blocks.append(_pack(parts))
        held = sorted(self.held)
        gs = [b for i in held for b in self.held[i]] + [jnp.stack(blocks, axis=0)]
        theirs = _exchange(swap_step(gs), gs, "grads_swap_halves")
        sums = [add_cores(g, t, f"grads_add_cores_{b}") for b, (g, t) in enumerate(zip(gs, theirs))]
        slots = _exchange(scatter_step(sums), sums, "grads_scatter")
        halves = [add_chips(s, f"grads_add_chips_{b}") for b, s in enumerate(slots)]
        totals = _exchange(join_step(halves), halves, "grads_join_halves")
        for n, i in enumerate(held):
            self.totals[i] = totals[2 * n:2 * n + 2]
        shapes = [shard_shapes[n] for n in narrow] + [full[n].shape for n in repl]
        out = dict(zip(narrow + repl, _unpack(totals[-1], shapes)))
        d = self.totals[0][0].shape[0] // 2
        for w, which in enumerate(('ffn1', 'ffn2')):
            mine = [self.totals[i] for i in sorted(self.totals) if i % 2 == w]
            out[f'{which}_w_gate'] = jnp.stack([t[0][:d] for t in mine], axis=0)
            out[f'{which}_w_up'] = jnp.stack([t[0][d:] for t in mine], axis=0)
            out[f'{which}_w_down'] = jnp.stack([t[1] for t in mine], axis=0)
        return out


def _ffn_fwd(x, h, stream, i, next_gain, tag):
    wg, wu, wd = stream.ffn(i)
    ici = stream.hosted(i + 1, 0)
    if ici is None:
        g, u, a = ffn_up(h, wg, wu, tag + "_up")
        out = mm([(a, 0, wd, a.shape[1], 0)], res=x, scale=0.5, norm_next=next_gain, name=tag + "_down")
    else:
        (g, u, a), gathered = ffn_up(h, wg, wu, tag + "_up", hosted=ici)
        stream.absorb(gathered)
        out, gathered = mm([(a, 0, wd, a.shape[1], 0)], res=x, scale=0.5, norm_next=next_gain,
                           name=tag + "_down", hosted=stream.hosted(i + 1, 1))
        stream.absorb(gathered)
    xo, h_next = (out, None) if next_gain is None else out
    return xo, h_next, (x, h, g, u, a, wg, wu, wd)


def _carrying(sink, kernel, *args, **kwargs):
    step = sink.step()
    if step is None:
        return kernel(*args, **kwargs)
    out, results = kernel(*args, hosted=step, **kwargs)
    sink.absorb(results)
    return out


def _ffn_bwd(dxo, saved, gain, sink, i, tag):
    x, h, g, u, a, wg, wu, wd = saved
    Fd = a.shape[1]
    dxo, dxo_b = dxo
    dgp, du, d_wd = _carrying(sink, ffn_down_bwd, dxo_b, wd, g, u, a, tag + "_down_bwd")
    d_wg = _carrying(sink, mm_tn, h, dgp, name=tag + "_dwg")
    d_wu = _carrying(sink, mm_tn, h, du, name=tag + "_dwu")
    sink.add(i, d_wg, d_wu, d_wd, ride=i > 0)
    return _carrying(sink, mm, [(dgp, 0, wg, Fd, 0), (du, 0, wu, Fd, 0)], tb=True, res=dxo, norm_bwd=(x, gain),
                     name=tag + "_up_bwd")


def local_step(x, mem, W, stream, sink, target):
    D = x.shape[1]
    depth = W['ffn1_norm'].shape[0]
    n_a = W['a_w_in'].shape[0]
    G = {}
    mem_h = rms_fwd(mem, W['mem_norm'], "mem_norm")
    b_mats = jnp.broadcast_to(W['a_b_spatial'][..., None], W['a_b_spatial'].shape + (GM_CHUNK,))
    w_kv = (W['w_kv'], None)

    def mixer_w(l):
        if l < n_a:
            return (W['a_w_in'], l), (W['a_w_out'], l), 2 * GM_W
        return (W['b_w_in'], l - n_a), (W['b_w_out'], l - n_a), GM_W

    saved = []
    kv = kvn = x_kv = None
    h = rms_fwd(x, W['ffn1_norm'][0], "l0_ffn1_norm")
    for l in range(depth):
        if l == n_a:
            x_kv = x
            kvn = rms_fwd(x, W['kv_norm'], "kv_norm")
            kv = mm([(kvn, 0, w_kv, D, 0)], out_dtype=BF16, name="kv_proj")
        x, hm, s1 = _ffn_fwd(x, h, stream, 2 * l, W['mix_norm'][l], f"l{l}_ffn1")
        mem_kv = mm([(mem_h, 0, (W['w_mem_kv'], l), D, 0)], name=f"l{l}_mem_kv")
        w_in, w_out, tok_w = mixer_w(l)
        proj = mm([(hm, 0, w_in, D, 0)], name=f"l{l}_mix_in")
        if l < n_a:
            y_tok = gmlp_fwd(proj, W['a_v_norm'][l], W['a_w_spatial'][l], b_mats[l], f"l{l}_gmlp")
        else:
            y_tok = sb_fwd(proj, kv, f"l{l}_sb")
        q_cb = tok_w // MEM_W
        y_mem = memattn_fwd(proj, q_cb, mem_kv, f"l{l}_memattn")
        x_mid = x
        x, h2 = mm([(y_tok, 0, w_out, GM_W, 0), (y_mem, 0, w_out, MEM_W, GM_W // MEM_W)], res=x,
                   norm_next=W['ffn2_norm'][l], name=f"l{l}_mix_out")
        sm = (x_mid, hm, mem_kv, proj, y_tok, y_mem, q_cb)
        next_gain = W['ffn1_norm'][l + 1] if l + 1 < depth else None
        x, h, s2 = _ffn_fwd(x, h2, stream, 2 * l + 1, next_gain, f"l{l}_ffn2")
        saved.append((s1, sm, s2))

    loss, dx, d_final = final_loss(x, W['final_norm'], target, "loss_head")
    G['final_norm'] = d_final.reshape(-1)

    per_layer = {n: [None] * depth for n in ['ffn1_norm', 'mix_norm', 'ffn2_norm', 'w_mem_kv']}
    per_a = {n: [None] * n_a for n in ['a_w_in', 'a_v_norm', 'a_w_spatial', 'a_b_spatial', 'a_w_out']}
    per_b = {n: [None] * (depth - n_a) for n in ['b_w_in', 'b_w_out']}
    d_mem_h = None
    dk = dv = None
    for l in reversed(range(depth)):
        s1, sm, s2 = saved[l]
        dx, dg = _ffn_bwd(dx, s2, W['ffn2_norm'][l], sink, 2 * l + 1, f"l{l}_ffn2")
        per_layer['ffn2_norm'][l] = dg.reshape(-1)

        x_mid, hm, mem_kv, proj, y_tok, y_mem, q_cb = sm
        is_a = l < n_a
        w_in, w_out, tok_w = mixer_w(l)
        dy = mm([(dx[1], 0, w_out, D, 0)], tb=True, name=f"l{l}_mix_out_bwd")
        d_w_out = jnp.concatenate([mm_tn(y_tok, dx[1], name=f"l{l}_dwout_tok"),
                                   mm_tn(y_mem, dx[1], name=f"l{l}_dwout_mem")], axis=0)
        dq_mem, d_mem_kv = memattn_bwd(proj, q_cb, mem_kv, dy, GM_W // MEM_W, f"l{l}_memattn_bwd")
        if is_a:
            d_tok, d_ws, d_bs, d_vg = gmlp_bwd(proj, dy, W['a_v_norm'][l], W['a_w_spatial'][l], b_mats[l],
                                               f"l{l}_gmlp_bwd")
            per_a['a_w_spatial'][l], per_a['a_b_spatial'][l] = d_ws, d_bs[:, :, 0]
            per_a['a_v_norm'][l], per_a['a_w_out'][l] = d_vg.reshape(-1), d_w_out
        else:
            d_tok, dk, dv = sb_bwd(proj, kv, dy, y_tok, dk, dv, f"l{l}_sb_bwd")
            per_b['b_w_out'][l - n_a] = d_w_out
        d_w_in = jnp.concatenate([mm_tn(hm, d_tok, name=f"l{l}_dwin_tok"),
                                  mm_tn(hm, dq_mem, name=f"l{l}_dwin_mem")], axis=1)
        (per_a['a_w_in'] if is_a else per_b['b_w_in'])[l if is_a else l - n_a] = d_w_in
        dx, dg = mm([(d_tok, 0, w_in, tok_w, 0), (dq_mem, 0, w_in, MEM_W, tok_w // MEM_W)], tb=True, res=dx[0],
                    norm_bwd=(x_mid, W['mix_norm'][l]), name=f"l{l}_mix_in_bwd")
        per_layer['mix_norm'][l] = dg.reshape(-1)
        per_layer['w_mem_kv'][l] = mm_tn(mem_h, d_mem_kv, name=f"l{l}_dw_mem_kv")
        d_mem_h = mm([(d_mem_kv, 0, (W['w_mem_kv'], l), 2 * MEM_W, 0)], tb=True, res=d_mem_h,
                     name=f"l{l}_mem_kv_bwd")

        dx, dg = _ffn_bwd(dx, s1, W['ffn1_norm'][l], sink, 2 * l, f"l{l}_ffn1")
        per_layer['ffn1_norm'][l] = dg.reshape(-1)
        if l == n_a:
            G['w_kv'] = jnp.concatenate([mm_tn(kvn, dk, name="dw_k"), mm_tn(kvn, dv, name="dw_v")], axis=1)
            dx, dg = mm([(dk, 0, w_kv, GM_W, 0), (dv, 0, w_kv, GM_W, 1)], tb=True, res=dx[0],
                        norm_bwd=(x_kv, W['kv_norm']), name="kv_proj_bwd")
            G['kv_norm'] = dg.reshape(-1)

    G['mem_norm'] = rms_gain_grad(mem, d_mem_h, "mem_norm_bwd").reshape(-1)
    for d in (per_layer, per_a, per_b):
        G.update(d)
    return loss, dx[0], G


def kernel(x, mem, ffn1_norm, ffn1_w_gate, ffn1_w_up, ffn1_w_down, mix_norm, ffn2_norm, ffn2_w_gate, ffn2_w_up, ffn2_w_down, mem_norm, w_mem_kv, a_w_in, a_v_norm, a_w_spatial, a_b_spatial, a_w_out, kv_norm, w_kv, b_w_in, b_w_out, final_norm, loss_target, m_ffn1_norm, m_ffn1_w_gate, m_ffn1_w_up, m_ffn1_w_down, m_mix_norm, m_ffn2_norm, m_ffn2_w_gate, m_ffn2_w_up, m_ffn2_w_down, m_mem_norm, m_w_mem_kv, m_a_w_in, m_a_v_norm, m_a_w_spatial, m_a_b_spatial, m_a_w_out, m_kv_norm, m_w_kv, m_b_w_in, m_b_w_out, m_final_norm, v_ffn1_norm, v_ffn1_w_gate, v_ffn1_w_up, v_ffn1_w_down, v_mix_norm, v_ffn2_norm, v_ffn2_w_gate, v_ffn2_w_up, v_ffn2_w_down, v_mem_norm, v_w_mem_kv, v_a_w_in, v_a_v_norm, v_a_w_spatial, v_a_b_spatial, v_a_w_out, v_kv_norm, v_w_kv, v_b_w_in, v_b_w_out, v_final_norm):
    weights = dict(zip(W_NAMES, [ffn1_norm, ffn1_w_gate, ffn1_w_up, ffn1_w_down, mix_norm, ffn2_norm, ffn2_w_gate,
                                 ffn2_w_up, ffn2_w_down, mem_norm, w_mem_kv, a_w_in, a_v_norm, a_w_spatial,
                                 a_b_spatial, a_w_out, kv_norm, w_kv, b_w_in, b_w_out, final_norm]))
    m_in = dict(zip(W_NAMES, [m_ffn1_norm, m_ffn1_w_gate, m_ffn1_w_up, m_ffn1_w_down, m_mix_norm, m_ffn2_norm,
                              m_ffn2_w_gate, m_ffn2_w_up, m_ffn2_w_down, m_mem_norm, m_w_mem_kv, m_a_w_in,
                              m_a_v_norm, m_a_w_spatial, m_a_b_spatial, m_a_w_out, m_kv_norm, m_w_kv, m_b_w_in,
                              m_b_w_out, m_final_norm]))
    v_in = dict(zip(W_NAMES, [v_ffn1_norm, v_ffn1_w_gate, v_ffn1_w_up, v_ffn1_w_down, v_mix_norm, v_ffn2_norm,
                              v_ffn2_w_gate, v_ffn2_w_up, v_ffn2_w_down, v_mem_norm, v_w_mem_kv, v_a_w_in,
                              v_a_v_norm, v_a_w_spatial, v_a_b_spatial, v_a_w_out, v_kv_norm, v_w_kv, v_b_w_in,
                              v_b_w_out, v_final_norm]))

    stream = WeightStream({n: weights[n] for n in SHARD_AXIS})
    W = {n: weights[n] for n in W_NAMES if n not in SHARD_AXIS}
    W.update(stream.rest())
    sink = GradSink()
    loss, dx, grads = local_step(x[0], mem[0], W, stream, sink, loss_target[0])
    total = sink.reduce(grads, {n: weights[n].shape for n in SHARD_AXIS})
    loss = lax.psum(loss[0, 0], ("x", "y", "c"))

    deltas, new_m, new_v = {}, {}, {}
    for n in W_NAMES:
        deltas[n], new_m[n], new_v[n] = adamw(weights[n], total[n], m_in[n], v_in[n], "adamw_" + n)
    return (loss, dx[None], *[total[n] for n in W_NAMES], *[deltas[n] for n in W_NAMES],
            *[new_m[n] for n in W_NAMES], *[new_v[n] for n in W_NAMES])
```

```python
import functools
import math

import jax
import jax.numpy as jnp
from jax import lax
from jax.experimental import pallas as pl
from jax.experimental.pallas import tpu as pltpu

F32 = jnp.float32
BF16 = jnp.bfloat16
EPS = 1e-6
LANES = 128
ROW_TILE = 512
COL_TILE = 1408
PACK_ROWS = 256
PART_ROWS = 16
VMEM_LIMIT = 56 * 1024 * 1024

W_NAMES = ['ffn1_norm', 'ffn1_w_gate', 'ffn1_w_up', 'ffn1_w_down', 'mix_norm', 'ffn2_norm', 'ffn2_w_gate',
           'ffn2_w_up', 'ffn2_w_down', 'mem_norm', 'w_mem_kv', 'a_w_in', 'a_v_norm', 'a_w_spatial',
           'a_b_spatial', 'a_w_out', 'kv_norm', 'w_kv', 'b_w_in', 'b_w_out', 'final_norm']
SHARD_AXIS = {'ffn1_w_gate': 2, 'ffn1_w_up': 2, 'ffn1_w_down': 1, 'ffn2_w_gate': 2, 'ffn2_w_up': 2,
              'ffn2_w_down': 1, 'w_mem_kv': 1, 'a_w_in': 2, 'a_v_norm': 1, 'a_w_out': 1, 'w_kv': 1,
              'b_w_in': 1, 'b_w_out': 1}
N_CHIPS = 4
PACK_W = 1024

MEM_HEADS = 4
MEM_W = 256
HEAD_DIM = 64
GM_W = 768
GM_GROUPS = 6
GM_CHUNK = 128
CHUNK = 64
SB_BLOCK = 128
SB_Q_CHUNK = 1024
SB_DEAD = -110.0
SB_PAIRS = 2
SB_COLS = SB_PAIRS * LANES

ADAM_LR, ADAM_B1, ADAM_B2, ADAM_EPS, ADAM_WD, ADAM_STEP = 0.001, 0.9, 0.999, 1e-08, 0.01, 10

MESH = pl.DeviceIdType.MESH


def _cparams(sem=None):
    return pltpu.CompilerParams(dimension_semantics=sem, vmem_limit_bytes=VMEM_LIMIT)


class Hosted:
    def __init__(self, arrays, n_out, sems, make, fresh=()):
        self.arrays, self.n_out, self.sems, self.make, self.fresh = arrays, n_out, sems, make, list(fresh)


def _call(body, *, grid, in_specs, out_specs, out_shape, ins, semantics, name, hosted=None):
    if hosted is None:
        return pl.pallas_call(body, grid=grid, in_specs=in_specs, out_specs=out_specs, out_shape=out_shape,
                              compiler_params=_cparams(semantics), name=name)(*ins)
    n_in, n_out = len(ins), len(out_shape)
    n_h, n_alias = len(hosted.arrays), hosted.n_out
    n_ho = n_alias + len(hosted.fresh)

    def hosting_body(*refs):
        base_in, h_in = refs[:n_in], refs[n_in:n_in + n_h]
        base_out = refs[n_in + n_h:n_in + n_h + n_out]
        h_out = refs[n_in + n_h + n_out:n_in + n_h + n_out + n_ho]
        sems = refs[n_in + n_h + n_out + n_ho:]
        copies, waits = hosted.make(h_in, h_out, *sems)
        first = last = None
        for axis, extent in enumerate(grid):
            at_start, at_end = pl.program_id(axis) == 0, pl.program_id(axis) == extent - 1
            first = at_start if first is None else jnp.logical_and(first, at_start)
            last = at_end if last is None else jnp.logical_and(last, at_end)

        @pl.when(first)
        def _():
            for cp in copies:
                cp.start()

        body(*base_in, *base_out)

        @pl.when(last)
        def _():
            for wait in waits:
                wait()

    any_spec = pl.BlockSpec(memory_space=pl.ANY)
    outs = pl.pallas_call(
        hosting_body, grid=grid, in_specs=list(in_specs) + [any_spec] * n_h,
        out_specs=list(out_specs) + [any_spec] * n_ho,
        out_shape=(list(out_shape) + [jax.ShapeDtypeStruct(a.shape, a.dtype) for a in hosted.arrays[:n_alias]]
                   + hosted.fresh),
        scratch_shapes=[pltpu.SemaphoreType.DMA((n,)) for n in hosted.sems],
        input_output_aliases={n_in + k: n_out + k for k in range(n_alias)},
        compiler_params=_cparams(("arbitrary",) * len(grid)), name=name)(*ins, *hosted.arrays)
    return outs[:n_out], outs[n_out:]


def _tile(n, target, mult=LANES):
    best = None
    for t in range(mult, min(n, target) + 1, mult):
        if n % t == 0:
            best = t
    return best if best is not None else n


def _dot(a, b):
    return jnp.dot(a, b, preferred_element_type=F32)


def _dot_nt(a, b):
    return lax.dot_general(a, b, (((1,), (1,)), ((), ())), preferred_element_type=F32)


def _dot_tn(a, b):
    return lax.dot_general(a, b, (((0,), (0,)), ((), ())), preferred_element_type=F32)


def _bf(v):
    return v.astype(BF16)


def rms_fwd(x, gain, name):
    S, D = x.shape
    tm = _tile(S, ROW_TILE, 8)

    def body(x_ref, g_ref, o_ref):
        xf = x_ref[...]
        r = lax.rsqrt(jnp.mean(xf * xf, axis=-1, keepdims=True) + EPS)
        o_ref[...] = ((xf * r) * g_ref[...]).astype(o_ref.dtype)

    return pl.pallas_call(
        body, grid=(S // tm,),
        in_specs=[pl.BlockSpec((tm, D), lambda i: (i, 0)), pl.BlockSpec((1, D), lambda i: (0, 0))],
        out_specs=pl.BlockSpec((tm, D), lambda i: (i, 0)),
        out_shape=jax.ShapeDtypeStruct((S, D), BF16),
        compiler_params=_cparams(("parallel",)), name=name)(x, gain.reshape(1, D))


def rms_gain_grad(x, dh, name):
    S, D = x.shape
    tm = _tile(S, ROW_TILE, 8)

    def body(x_ref, dh_ref, dg_ref):
        xf = x_ref[...]
        r = lax.rsqrt(jnp.mean(xf * xf, axis=-1, keepdims=True) + EPS)

        @pl.when(pl.program_id(0) == 0)
        def _():
            dg_ref[...] = jnp.zeros_like(dg_ref)

        dg_ref[...] += jnp.sum(dh_ref[...] * (xf * r), axis=0, keepdims=True)

    row = pl.BlockSpec((tm, D), lambda i: (i, 0))
    return pl.pallas_call(
        body, grid=(S // tm,), in_specs=[row, row], out_specs=pl.BlockSpec((1, D), lambda i: (0, 0)),
        out_shape=jax.ShapeDtypeStruct((1, D), F32),
        compiler_params=_cparams(("arbitrary",)), name=name)(x, dh)


def final_loss(x, gain, target, name):
    S, D = x.shape
    tm = _tile(S, ROW_TILE, 16)

    def body(x_ref, g_ref, t_ref, loss_ref, dx_ref, dxb_ref, dg_ref):
        xf = x_ref[...]
        r = lax.rsqrt(jnp.mean(xf * xf, axis=-1, keepdims=True) + EPS)
        xhat = xf * r
        g = g_ref[...]
        diff = xhat * g - t_ref[...]
        dy = diff * (1.0 / D)
        dxhat = dy * g
        m = jnp.mean(dxhat * xhat, axis=-1, keepdims=True)
        dx = r * (dxhat - xhat * m)
        dx_ref[...] = dx
        dxb_ref[...] = _bf(dx)

        @pl.when(pl.program_id(0) == 0)
        def _():
            dg_ref[...] = jnp.zeros_like(dg_ref)
            loss_ref[...] = jnp.zeros_like(loss_ref)

        dg_ref[...] += jnp.sum(dy * xhat, axis=0, keepdims=True)
        per_tok = jnp.sum(diff * diff, axis=-1, keepdims=True) * (0.5 / D)
        loss_ref[...] += jnp.sum(per_tok, axis=0, keepdims=True)

    row = pl.BlockSpec((tm, D), lambda i: (i, 0))
    vec = pl.BlockSpec((1, D), lambda i: (0, 0))
    one = pl.BlockSpec((1, 1), lambda i: (0, 0))
    loss, dx, dxb, dg = pl.pallas_call(
        body, grid=(S // tm,), in_specs=[row, vec, row], out_specs=[one, row, row, vec],
        out_shape=[jax.ShapeDtypeStruct((1, 1), F32), jax.ShapeDtypeStruct((S, D), F32),
                   jax.ShapeDtypeStruct((S, D), BF16), jax.ShapeDtypeStruct((1, D), F32)],
        compiler_params=_cparams(("arbitrary",)), name=name)(x, gain.reshape(1, D), target)
    return loss, (dx, dxb), dg


def _wspec(w, blk, idx):
    _, l = w
    if l is None:
        return pl.BlockSpec(blk, idx)
    return pl.BlockSpec((None,) + blk, lambda *g: (l,) + idx(*g))


def mm(pairs, *, tb=False, res=None, scale=1.0, out_dtype=F32, norm_bwd=None, norm_next=None, hosted=None, name):
    M = pairs[0][0].shape[0]
    N = pairs[0][2][0].shape[-2 if tb else -1]
    tm = _tile(M, ROW_TILE, 16)
    tn = N if (norm_bwd is not None or norm_next is not None) else _tile(N, COL_TILE)
    n_p = len(pairs)
    has_res = res is not None
    n_in = 2 * n_p + has_res + (2 if norm_bwd is not None else 0) + (norm_next is not None)

    def body(*refs):
        acc = None
        for p in range(n_p):
            a = _bf(refs[2 * p][...])
            b = _bf(refs[2 * p + 1][...])
            d = _dot_nt(a, b) if tb else _dot(a, b)
            acc = d if acc is None else acc + d
        if scale != 1.0:
            acc = acc * scale
        if norm_bwd is None:
            if has_res:
                acc = acc + refs[2 * n_p][...]
            refs[n_in][...] = acc.astype(refs[n_in].dtype)
            if norm_next is not None:
                r = lax.rsqrt(jnp.mean(acc * acc, axis=-1, keepdims=True) + EPS)
                refs[n_in + 1][...] = _bf((acc * r) * refs[n_in - 1][...])
            return
        x_ref, g_ref = refs[n_in - 2], refs[n_in - 1]
        dx_ref, dxb_ref, dg_ref = refs[n_in:]
        xf = x_ref[...]
        r = lax.rsqrt(jnp.mean(xf * xf, axis=-1, keepdims=True) + EPS)
        xhat = xf * r
        dxhat = acc * g_ref[...]
        m = jnp.mean(dxhat * xhat, axis=-1, keepdims=True)
        dx = r * (dxhat - xhat * m)
        if has_res:
            dx = dx + refs[2 * n_p][...]
        dx_ref[...] = dx
        dxb_ref[...] = _bf(dx)

        @pl.when(pl.program_id(1) == 0)
        def _():
            dg_ref[...] = jnp.zeros_like(dg_ref)

        dg_ref[...] += jnp.sum(acc * xhat, axis=0, keepdims=True)

    ins, in_specs = [], []
    for a, cb, w, K, kb in pairs:
        ins += [a, w[0]]
        in_specs.append(pl.BlockSpec((tm, K), functools.partial(lambda j, i, cb: (i, cb), cb=cb)))
        if tb:
            in_specs.append(_wspec(w, (tn, K), functools.partial(lambda j, i, kb: (j, kb), kb=kb)))
        else:
            in_specs.append(_wspec(w, (K, tn), functools.partial(lambda j, i, kb: (kb, j), kb=kb)))
    tile = pl.BlockSpec((tm, tn), lambda j, i: (i, j))
    if has_res:
        ins.append(res)
        in_specs.append(tile)
    if norm_bwd is None:
        out_shape = [jax.ShapeDtypeStruct((M, N), out_dtype)]
        if norm_next is not None:
            ins.append(norm_next.reshape(1, N))
            in_specs.append(pl.BlockSpec((1, N), lambda j, i: (0, 0)))
            out_shape.append(jax.ShapeDtypeStruct((M, N), BF16))
        out = _call(body, grid=(N // tn, M // tm), in_specs=in_specs, out_specs=[tile] * len(out_shape),
                    out_shape=out_shape, ins=ins, semantics=("parallel", "parallel"), name=name, hosted=hosted)
        base = out if hosted is None else out[0]
        value = base[0] if norm_next is None else (base[0], base[1])
        return value if hosted is None else (value, out[1])
    x, gain = norm_bwd
    vec = pl.BlockSpec((1, N), lambda j, i: (0, 0))
    out = _call(
        body, grid=(1, M // tm), in_specs=in_specs + [tile, vec], out_specs=[tile, tile, vec],
        out_shape=[jax.ShapeDtypeStruct((M, N), F32), jax.ShapeDtypeStruct((M, N), BF16),
                   jax.ShapeDtypeStruct((1, N), F32)], ins=ins + [x, gain.reshape(1, N)],
        semantics=("arbitrary", "arbitrary"), name=name, hosted=hosted)
    dx, dxb, dg = out if hosted is None else out[0]
    return ((dx, dxb), dg) if hosted is None else (((dx, dxb), dg), out[1])


def mm_tn(a, b, *, a_cb=0, a_w=None, b_cb=0, b_w=None, hosted=None, name):
    S = a.shape[0]
    Ka = a.shape[1] if a_w is None else a_w
    Nb = b.shape[1] if b_w is None else b_w
    ts = _tile(S, 2 * ROW_TILE, 8)
    tk = _tile(Ka, COL_TILE)
    tn = _tile(Nb, COL_TILE)
    a_off, b_off = a_cb * (Ka // tk), b_cb * (Nb // tn)

    def body(a_ref, b_ref, o_ref):
        s = pl.program_id(2)

        @pl.when(s == 0)
        def _():
            o_ref[...] = jnp.zeros_like(o_ref)

        o_ref[...] += _dot_tn(_bf(a_ref[...]), _bf(b_ref[...]))

    out = _call(
        body, grid=(Ka // tk, Nb // tn, S // ts),
        in_specs=[pl.BlockSpec((ts, tk), lambda k, n, s: (s, a_off + k)),
                  pl.BlockSpec((ts, tn), lambda k, n, s: (s, b_off + n))],
        out_specs=[pl.BlockSpec((tk, tn), lambda k, n, s: (k, n))],
        out_shape=[jax.ShapeDtypeStruct((Ka, Nb), F32)], ins=[a, b],
        semantics=("parallel", "parallel", "arbitrary"), name=name, hosted=hosted)
    return out[0] if hosted is None else (out[0][0], out[1])


def mm_tn_pair(a, b1, b2, *, hosted=None, name):
    S, Ka = a.shape
    Nb = b1.shape[1]
    ts = _tile(S, 2 * ROW_TILE, 8)
    tk = _tile(Ka, COL_TILE)
    tn = _tile(Nb, COL_TILE)

    def body(a_ref, b1_ref, b2_ref, o1_ref, o2_ref):
        @pl.when(pl.program_id(2) == 0)
        def _():
            o1_ref[...] = jnp.zeros_like(o1_ref)
            o2_ref[...] = jnp.zeros_like(o2_ref)

        av = _bf(a_ref[...])
        o1_ref[...] += _dot_tn(av, _bf(b1_ref[...]))
        o2_ref[...] += _dot_tn(av, _bf(b2_ref[...]))

    b_spec = pl.BlockSpec((ts, tn), lambda k, n, s: (s, n))
    o_spec = pl.BlockSpec((tk, tn), lambda k, n, s: (k, n))
    sh = jax.ShapeDtypeStruct((Ka, Nb), F32)
    out = _call(
        body, grid=(Ka // tk, Nb // tn, S // ts),
        in_specs=[pl.BlockSpec((ts, tk), lambda k, n, s: (s, k)), b_spec, b_spec],
        out_specs=[o_spec, o_spec], out_shape=[sh, sh], ins=[a, b1, b2],
        semantics=("parallel", "parallel", "arbitrary"), name=name, hosted=hosted)
    return tuple(out) if hosted is None else (tuple(out[0]), out[1])


def _sigmoid(x):
    return 0.5 * jnp.tanh(0.5 * x) + 0.5


def ffn_up(h, wg, wu, name, hosted=None):
    S, D = h.shape
    Fd = wg[0].shape[-1]
    tm = _tile(S, ROW_TILE, 8)
    tn = _tile(Fd, COL_TILE)

    def body(h_ref, wg_ref, wu_ref, g_ref, u_ref, a_ref):
        hv = h_ref[...]
        g = _dot(hv, wg_ref[...])
        u = _dot(hv, wu_ref[...])
        g_ref[...] = _bf(g)
        u_ref[...] = _bf(u)
        a_ref[...] = _bf(g * _sigmoid(g) * u)

    blk = pl.BlockSpec((tm, tn), lambda j, i: (i, j))
    sh = jax.ShapeDtypeStruct((S, Fd), BF16)
    return _call(
        body, grid=(Fd // tn, S // tm),
        in_specs=[pl.BlockSpec((tm, D), lambda j, i: (i, 0)), _wspec(wg, (D, tn), lambda j, i: (0, j)),
                  _wspec(wu, (D, tn), lambda j, i: (0, j))],
        out_specs=[blk, blk, blk], out_shape=[sh, sh, sh], ins=[h, wg[0], wu[0]],
        semantics=("parallel", "parallel"), name=name, hosted=hosted)


def ffn_down_bwd(dxo, wd, g, u, a, name, hosted=None):
    S, D = dxo.shape
    Fd = wd[0].shape[-2]
    tm = _tile(S, ROW_TILE, 8)
    tn = _tile(Fd, COL_TILE)

    def body(dx_ref, wd_ref, g_ref, u_ref, a_ref, dg_ref, du_ref, dwd_ref):
        dx = _bf(dx_ref[...])
        da = _dot_nt(dx, wd_ref[...]) * 0.5
        g = g_ref[...].astype(F32)
        u = u_ref[...].astype(F32)
        sg = _sigmoid(g)
        dg_ref[...] = _bf(da * u * (sg * (1.0 + g * (1.0 - sg))))
        du_ref[...] = _bf(da * (g * sg))

        @pl.when(pl.program_id(1) == 0)
        def _():
            dwd_ref[...] = jnp.zeros_like(dwd_ref)

        dwd_ref[...] += _dot_tn(a_ref[...], dx) * 0.5

    blk = pl.BlockSpec((tm, tn), lambda j, i: (i, j))
    sh = jax.ShapeDtypeStruct((S, Fd), BF16)
    return _call(
        body, grid=(Fd // tn, S // tm),
        in_specs=[pl.BlockSpec((tm, D), lambda j, i: (i, 0)), _wspec(wd, (tn, D), lambda j, i: (j, 0)),
                  blk, blk, blk],
        out_specs=[blk, blk, pl.BlockSpec((tn, D), lambda j, i: (j, 0))],
        out_shape=[sh, sh, jax.ShapeDtypeStruct((Fd, D), F32)], ins=[dxo, wd[0], g, u, a],
        semantics=("parallel", "arbitrary"), name=name, hosted=hosted)


_GELU_C = math.sqrt(2.0 / math.pi)


def _gelu(p):
    return 0.5 * p * (1.0 + jnp.tanh(_GELU_C * (p + 0.044715 * (p * p * p))))


def _gelu_grad(p):
    th = jnp.tanh(_GELU_C * (p + 0.044715 * (p * p * p)))
    return 0.5 * (1.0 + th) + 0.5 * p * (1.0 - th * th) * (_GELU_C * (1.0 + 3.0 * 0.044715 * (p * p)))


def _chunk_mask():
    t = lax.broadcasted_iota(jnp.int32, (GM_CHUNK, GM_CHUNK), 0)
    s = lax.broadcasted_iota(jnp.int32, (GM_CHUNK, GM_CHUNK), 1)
    return (s // CHUNK) <= (t // CHUNK)


def gmlp_fwd(proj, v_gain, w_s, b_mat, name):
    S = proj.shape[0]
    tw = _tile(S, ROW_TILE, GM_CHUNK)
    n_win = tw // GM_CHUNK

    def body(p_ref, gain_ref, w_ref, b_ref, y_ref):
        mask = _chunk_mask()
        v = _gelu(p_ref[:, GM_W:])
        r = lax.rsqrt(jnp.mean(v * v, axis=-1, keepdims=True) + EPS)
        vn = _bf(v * r * gain_ref[...])
        for g in range(GM_GROUPS):
            wm = _bf(jnp.where(mask, w_ref[g], 0.0))
            cs = slice(g * GM_CHUNK, (g + 1) * GM_CHUNK)
            for w in range(n_win):
                rs = slice(w * GM_CHUNK, (w + 1) * GM_CHUNK)
                mixed = _dot(wm, vn[rs, cs]) + b_ref[g]
                y_ref[rs, cs] = _bf(_gelu(p_ref[rs, cs]) * mixed)

    return pl.pallas_call(
        body, grid=(S // tw,),
        in_specs=[pl.BlockSpec((tw, 2 * GM_W), lambda i: (i, 0)), pl.BlockSpec((1, GM_W), lambda i: (0, 0)),
                  pl.BlockSpec((GM_GROUPS, GM_CHUNK, GM_CHUNK), lambda i: (0, 0, 0)),
                  pl.BlockSpec((GM_GROUPS, GM_CHUNK, GM_CHUNK), lambda i: (0, 0, 0))],
        out_specs=pl.BlockSpec((tw, GM_W), lambda i: (i, 0)),
        out_shape=jax.ShapeDtypeStruct((S, GM_W), BF16),
        compiler_params=_cparams(("parallel",)), name=name)(proj, v_gain.reshape(1, GM_W), w_s, b_mat)


def gmlp_bwd(proj, dy, v_gain, w_s, b_mat, name):
    S = proj.shape[0]
    tw = _tile(S, ROW_TILE, GM_CHUNK)
    n_win = tw // GM_CHUNK

    def body(p_ref, dy_ref, gain_ref, w_ref, b_ref, dp_ref, dw_ref, db_ref, dgain_ref, dvn_ref):
        step = pl.program_id(0)

        @pl.when(step == 0)
        def _():
            dw_ref[...] = jnp.zeros_like(dw_ref)
            db_ref[...] = jnp.zeros_like(db_ref)
            dgain_ref[...] = jnp.zeros_like(dgain_ref)

        mask = _chunk_mask()
        pv = p_ref[:, GM_W:]
        v = _gelu(pv)
        r = lax.rsqrt(jnp.mean(v * v, axis=-1, keepdims=True) + EPS)
        vhat = v * r
        gain = gain_ref[...]
        vn = _bf(vhat * gain)
        for g in range(GM_GROUPS):
            wm = _bf(jnp.where(mask, w_ref[g], 0.0))
            cs = slice(g * GM_CHUNK, (g + 1) * GM_CHUNK)
            dw_acc = jnp.zeros((GM_CHUNK, GM_CHUNK), F32)
            db_acc = jnp.zeros((GM_CHUNK, GM_CHUNK), F32)
            for w in range(n_win):
                rs = slice(w * GM_CHUNK, (w + 1) * GM_CHUNK)
                pu = p_ref[rs, cs]
                u = _gelu(pu)
                vn_blk = vn[rs, cs]
                mixed = _dot(wm, vn_blk) + b_ref[g]
                dyb = dy_ref[rs, cs]
                dp_ref[rs, cs] = _bf(dyb * mixed * _gelu_grad(pu))
                dmix = dyb * u
                db_acc = db_acc + dmix
                dmix_b = _bf(dmix)
                dw_acc = dw_acc + _dot_nt(dmix_b, vn_blk)
                dvn_ref[rs, cs] = _dot_tn(wm, dmix_b)
            dw_ref[g] += jnp.where(mask, dw_acc, 0.0)
            db_ref[g] += jnp.broadcast_to(jnp.sum(db_acc, axis=-1, keepdims=True), (GM_CHUNK, GM_CHUNK))
        dvn = dvn_ref[...]
        dgain_ref[...] += jnp.sum(dvn * vhat, axis=0, keepdims=True)
        dvhat = dvn * gain
        m = jnp.mean(dvhat * vhat, axis=-1, keepdims=True)
        dv = r * (dvhat - vhat * m)
        dp_ref[:, GM_W:] = _bf(dv * _gelu_grad(pv))

    sq = pl.BlockSpec((GM_GROUPS, GM_CHUNK, GM_CHUNK), lambda i: (0, 0, 0))
    sq_sh = jax.ShapeDtypeStruct((GM_GROUPS, GM_CHUNK, GM_CHUNK), F32)
    return pl.pallas_call(
        body, grid=(S // tw,),
        in_specs=[pl.BlockSpec((tw, 2 * GM_W), lambda i: (i, 0)), pl.BlockSpec((tw, GM_W), lambda i: (i, 0)),
                  pl.BlockSpec((1, GM_W), lambda i: (0, 0)), sq, sq],
        out_specs=[pl.BlockSpec((tw, 2 * GM_W), lambda i: (i, 0)), sq, sq,
                   pl.BlockSpec((1, GM_W), lambda i: (0, 0))],
        out_shape=[jax.ShapeDtypeStruct((S, 2 * GM_W), BF16), sq_sh, sq_sh,
                   jax.ShapeDtypeStruct((1, GM_W), F32)],
        scratch_shapes=[pltpu.VMEM((tw, GM_W), F32)],
        compiler_params=_cparams(("arbitrary",)), name=name)(proj, dy, v_gain.reshape(1, GM_W), w_s, b_mat)


def _head_mask(h, width):
    lane = lax.broadcasted_iota(jnp.int32, (1, width), 1)
    return (lane >= HEAD_DIM * h) & (lane < HEAD_DIM * (h + 1))


def _mem_probs(q, k, h):
    kh = jnp.where(_head_mask(h, MEM_W), k, jnp.zeros_like(k))
    s = _dot_nt(q, kh) * (HEAD_DIM ** -0.5)
    s = s - jnp.max(s, axis=-1, keepdims=True)
    e = jnp.exp(s)
    return e * (1.0 / jnp.sum(e, axis=-1, keepdims=True)), kh


def memattn_fwd(proj, q_cb, mem_kv, name):
    S = proj.shape[0]
    NM = mem_kv.shape[0]
    tm = _tile(S, ROW_TILE, 8)

    def body(q_ref, kv_ref, o_ref):
        q = _bf(q_ref[...])
        k = _bf(kv_ref[:, :MEM_W])
        v = _bf(kv_ref[:, MEM_W:])
        acc = jnp.zeros((tm, MEM_W), F32)
        for h in range(MEM_HEADS):
            p, _ = _mem_probs(q, k, h)
            vh = jnp.where(_head_mask(h, MEM_W), v, jnp.zeros_like(v))
            acc = acc + _dot(_bf(p), vh)
        o_ref[...] = _bf(acc)

    return pl.pallas_call(
        body, grid=(S // tm,),
        in_specs=[pl.BlockSpec((tm, MEM_W), lambda i: (i, q_cb)), pl.BlockSpec((NM, 2 * MEM_W), lambda i: (0, 0))],
        out_specs=pl.BlockSpec((tm, MEM_W), lambda i: (i, 0)),
        out_shape=jax.ShapeDtypeStruct((S, MEM_W), BF16),
        compiler_params=_cparams(("parallel",)), name=name)(proj, mem_kv)


def memattn_bwd(proj, q_cb, mem_kv, dy, dy_cb, name):
    S = proj.shape[0]
    NM = mem_kv.shape[0]
    tm = _tile(S, ROW_TILE, 8)

    def body(q_ref, kv_ref, do_ref, dq_ref, dkv_ref):
        @pl.when(pl.program_id(0) == 0)
        def _():
            dkv_ref[...] = jnp.zeros_like(dkv_ref)

        q = _bf(q_ref[...])
        k = _bf(kv_ref[:, :MEM_W])
        v = _bf(kv_ref[:, MEM_W:])
        do = _bf(do_ref[...])
        dq = jnp.zeros((tm, MEM_W), F32)
        dk = jnp.zeros((NM, MEM_W), F32)
        dv = jnp.zeros((NM, MEM_W), F32)
        for h in range(MEM_HEADS):
            hm = _head_mask(h, MEM_W)
            p, kh = _mem_probs(q, k, h)
            vh = jnp.where(hm, v, jnp.zeros_like(v))
            dp = _dot_nt(do, vh)
            ds = _bf(p * (dp - jnp.sum(p * dp, axis=-1, keepdims=True)) * (HEAD_DIM ** -0.5))
            dq = dq + _dot(ds, kh)
            dk = dk + jnp.where(hm, _dot_tn(ds, q), 0.0)
            dv = dv + jnp.where(hm, _dot_tn(_bf(p), do), 0.0)
        dq_ref[...] = _bf(dq)
        dkv_ref[:, :MEM_W] += dk
        dkv_ref[:, MEM_W:] += dv

    return pl.pallas_call(
        body, grid=(S // tm,),
        in_specs=[pl.BlockSpec((tm, MEM_W), lambda i: (i, q_cb)), pl.BlockSpec((NM, 2 * MEM_W), lambda i: (0, 0)),
                  pl.BlockSpec((tm, MEM_W), lambda i: (i, dy_cb))],
        out_specs=[pl.BlockSpec((tm, MEM_W), lambda i: (i, 0)), pl.BlockSpec((NM, 2 * MEM_W), lambda i: (0, 0))],
        out_shape=[jax.ShapeDtypeStruct((S, MEM_W), BF16), jax.ShapeDtypeStruct((NM, 2 * MEM_W), F32)],
        compiler_params=_cparams(("arbitrary",)), name=name)(proj, mem_kv, dy)


def _split_bf(v):
    hi = _bf(v)
    return hi, _bf(v - hi.astype(F32))


def _sb_scores(qp, k2, dpos, row, col, tri):
    z = _dot_nt(qp, k2) * (HEAD_DIM ** -0.5)
    t = jnp.log(1.0 + jnp.exp(-jnp.abs(z)))
    lb = jnp.minimum(z, 0.0) - t
    lom = jnp.where(_sb_mask(dpos, row, col), -jnp.maximum(z, 0.0) - t, 0.0)
    l_hi, l_lo = _split_bf(lom)
    insuf = _dot(l_hi, tri) + _dot(l_lo, tri)
    return lb + insuf, jnp.sum(lom, axis=1, keepdims=True), lb


def _sb_mask(dpos, row, col):
    return (col - row) < dpos


def _sb_weights(s, run, dpos, row, col):
    return jnp.where(_sb_mask(dpos, row, col), jnp.exp(s + run), 0.0)


def _sb_alive(runs):
    top = runs[0]
    for r in runs[1:]:
        top = jnp.maximum(top, r)
    return jnp.max(top) > SB_DEAD


def _sb_consts():
    row = lax.broadcasted_iota(jnp.int32, (2 * SB_BLOCK, SB_BLOCK), 0) % SB_BLOCK
    col = lax.broadcasted_iota(jnp.int32, (2 * SB_BLOCK, SB_BLOCK), 1)
    r = lax.broadcasted_iota(jnp.int32, (SB_BLOCK, SB_BLOCK), 0)
    c = lax.broadcasted_iota(jnp.int32, (SB_BLOCK, SB_BLOCK), 1)
    return row, col, _bf(jnp.where(r > c, 1.0, 0.0)), _bf(jnp.where(r >= c, 1.0, 0.0))


def _stack_heads(x2, masks):
    return jnp.concatenate([_bf(jnp.where(hm, x2, jnp.zeros_like(x2))) for hm in masks], axis=0)


def _side_by_side(x):
    return jnp.concatenate([x[:SB_BLOCK], x[SB_BLOCK:]], axis=1)


def sb_fwd(proj, kv, name):
    S = proj.shape[0]
    n_g = GM_W // SB_COLS
    tq = _tile(S, SB_Q_CHUNK, SB_BLOCK)
    nqb = tq // SB_BLOCK

    def body(q_ref, k_ref, v_ref, o_ref, s_ref):
        chunk = pl.program_id(1)
        row, col, tri, _ = _sb_consts()
        masks = [_head_mask(h, LANES) for h in range(2)]
        pairs = [slice(p * LANES, (p + 1) * LANES) for p in range(SB_PAIRS)]

        def q_loop(qi, _):
            i = chunk * nqb + qi
            rows = pl.ds(pl.multiple_of(qi * SB_BLOCK, SB_BLOCK), SB_BLOCK)
            qps = [_stack_heads(q_ref[rows, cs], masks) for cs in pairs]

            def look_ahead(t):
                keys = pl.ds(pl.multiple_of(jnp.maximum(i - t, 0) * SB_BLOCK, SB_BLOCK), SB_BLOCK)
                sums = []
                for p, cs in enumerate(pairs):
                    s, lom_sum, _ = _sb_scores(qps[p], k_ref[keys, cs], t * SB_BLOCK, row, col, tri)
                    s_ref[p] = s
                    sums.append(lom_sum)
                return tuple(sums)

            def k_step(carry):
                t, accs, runs, sums, _ = carry
                keys = pl.ds(pl.multiple_of((i - t) * SB_BLOCK, SB_BLOCK), SB_BLOCK)
                new_accs, new_runs = [], []
                for p, cs in enumerate(pairs):
                    a = _sb_weights(s_ref[p], runs[p], t * SB_BLOCK, row, col)
                    a_hi, a_lo = _split_bf(_side_by_side(a))
                    vp = _stack_heads(v_ref[keys, cs], masks)
                    new_accs.append(accs[p] + _dot(a_hi, vp) + _dot(a_lo, vp))
                    new_runs.append(runs[p] + sums[p])
                return t + 1, tuple(new_accs), tuple(new_runs), look_ahead(t + 1), _sb_alive(new_runs)

            zero = jnp.zeros((2 * SB_BLOCK, 1), F32)
            _, accs, _, _, _ = lax.while_loop(
                lambda carry: jnp.logical_and(carry[0] <= i, carry[4]), k_step,
                (jnp.int32(0), (jnp.zeros((SB_BLOCK, LANES), F32),) * SB_PAIRS, (zero,) * SB_PAIRS,
                 look_ahead(jnp.int32(0)), jnp.bool_(True)))
            for p, cs in enumerate(pairs):
                o_ref[rows, cs] = accs[p]
            return 0

        lax.fori_loop(0, nqb, q_loop, 0)

    return pl.pallas_call(
        body, grid=(n_g, S // tq),
        in_specs=[pl.BlockSpec((tq, SB_COLS), lambda g, c: (c, g)),
                  pl.BlockSpec((S, SB_COLS), lambda g, c: (0, g)),
                  pl.BlockSpec((S, SB_COLS), lambda g, c: (0, n_g + g))],
        out_specs=pl.BlockSpec((tq, SB_COLS), lambda g, c: (c, g)),
        out_shape=jax.ShapeDtypeStruct((S, GM_W), F32),
        scratch_shapes=[pltpu.VMEM((SB_PAIRS, 2 * SB_BLOCK, LANES), F32)],
        compiler_params=_cparams(("parallel", "parallel")), name=name)(proj, kv, kv)


def sb_bwd(proj, kv, dy, out, dk_init, dv_init, name):
    S = proj.shape[0]
    n_g = GM_W // SB_COLS
    tq = _tile(S, SB_Q_CHUNK, SB_BLOCK)
    nqb = tq // SB_BLOCK
    n_chunks = S // tq
    has_init = dk_init is not None
    scale = HEAD_DIM ** -0.5

    def body(*refs):
        s_ref, beta_ref, da_ref = refs[-3:]
        if has_init:
            q_ref, k_ref, v_ref, do_ref, out_ref, dki_ref, dvi_ref, dq_ref, dko_ref, dvo_ref, dk_acc, dv_acc = refs[:-3]
        else:
            q_ref, k_ref, v_ref, do_ref, out_ref, dq_ref, dko_ref, dvo_ref, dk_acc, dv_acc = refs[:-3]
        chunk = pl.program_id(1)
        cols = pl.ds(pl.multiple_of(pl.program_id(0) * SB_COLS, SB_COLS), SB_COLS)

        @pl.when(chunk == 0)
        def _():
            if has_init:
                pltpu.sync_copy(dki_ref.at[:, cols], dk_acc)
                pltpu.sync_copy(dvi_ref.at[:, cols], dv_acc)
            else:
                dk_acc[...] = jnp.zeros_like(dk_acc)
                dv_acc[...] = jnp.zeros_like(dv_acc)

        row, col, tri, tri_inc = _sb_consts()
        masks = [_head_mask(h, LANES) for h in range(2)]
        pairs = [slice(p * LANES, (p + 1) * LANES) for p in range(SB_PAIRS)]

        def q_loop(qi, _):
            i = chunk * nqb + qi
            rows = pl.ds(pl.multiple_of(qi * SB_BLOCK, SB_BLOCK), SB_BLOCK)
            qps = [_stack_heads(q_ref[rows, cs], masks) for cs in pairs]
            dops = [_stack_heads(do_ref[rows, cs], masks) for cs in pairs]
            e_tots = [jnp.sum(dop.astype(F32) * jnp.concatenate([out_ref[rows, cs]] * 2, axis=0), axis=1,
                              keepdims=True) for dop, cs in zip(dops, pairs)]

            def look_ahead(t):
                keys = pl.ds(pl.multiple_of(jnp.maximum(i - t, 0) * SB_BLOCK, SB_BLOCK), SB_BLOCK)
                sums = []
                for p, cs in enumerate(pairs):
                    s, lom_sum, lb = _sb_scores(qps[p], k_ref[keys, cs], t * SB_BLOCK, row, col, tri)
                    s_ref[p] = s
                    beta_ref[p] = jnp.exp(lb)
                    da_ref[p] = _dot_nt(dops[p], v_ref[keys, cs])
                    sums.append(lom_sum)
                return tuple(sums)

            def k_step(carry):
                t, dqs, runs, e_runs, sums, _ = carry
                keys = pl.ds(pl.multiple_of((i - t) * SB_BLOCK, SB_BLOCK), SB_BLOCK)
                new_dqs, new_runs, new_e_runs = [], [], []
                for p, cs in enumerate(pairs):
                    a = _sb_weights(s_ref[p], runs[p], t * SB_BLOCK, row, col)
                    e = a * da_ref[p]
                    e_hi, e_lo = _split_bf(e)
                    before = e_tots[p] - e_runs[p] - (_dot(e_hi, tri_inc) + _dot(e_lo, tri_inc))
                    beta = beta_ref[p]
                    dz = jnp.where(_sb_mask(t * SB_BLOCK, row, col), e * (1.0 - beta) - before * beta, 0.0)
                    dz = _bf(dz * scale)
                    new_dqs.append(dqs[p] + _dot(_side_by_side(dz), _stack_heads(k_ref[keys, cs], masks)))
                    dk_acc[keys, cs] += _dot_tn(dz, qps[p])
                    dv_acc[keys, cs] += _dot_tn(_bf(a), dops[p])
                    new_runs.append(runs[p] + sums[p])
                    new_e_runs.append(e_runs[p] + jnp.sum(e, axis=1, keepdims=True))
                return (t + 1, tuple(new_dqs), tuple(new_runs), tuple(new_e_runs), look_ahead(t + 1),
                        _sb_alive(new_runs))

            zero = jnp.zeros((2 * SB_BLOCK, 1), F32)
            zeros = (zero,) * SB_PAIRS
            _, dqs, _, _, _, _ = lax.while_loop(
                lambda carry: jnp.logical_and(carry[0] <= i, carry[5]), k_step,
                (jnp.int32(0), (jnp.zeros((SB_BLOCK, LANES), F32),) * SB_PAIRS, zeros, zeros,
                 look_ahead(jnp.int32(0)), jnp.bool_(True)))
            for p, cs in enumerate(pairs):
                dq_ref[rows, cs] = dqs[p]
            return 0

        lax.fori_loop(0, nqb, q_loop, 0)

        @pl.when(chunk == n_chunks - 1)
        def _():
            pltpu.sync_copy(dk_acc, dko_ref.at[:, cols])
            pltpu.sync_copy(dv_acc, dvo_ref.at[:, cols])

    qspec = pl.BlockSpec((tq, SB_COLS), lambda g, c: (c, g))
    kspec = pl.BlockSpec((S, SB_COLS), lambda g, c: (0, g))
    in_specs = [qspec, kspec, pl.BlockSpec((S, SB_COLS), lambda g, c: (0, n_g + g)), qspec, qspec]
    ins = [proj, kv, kv, dy, out]
    if has_init:
        in_specs += [_ANY, _ANY]
        ins += [dk_init, dv_init]
    sh = jax.ShapeDtypeStruct((S, GM_W), F32)
    return pl.pallas_call(
        body, grid=(n_g, n_chunks), in_specs=in_specs, out_specs=[qspec, _ANY, _ANY],
        out_shape=[sh, sh, sh],
        scratch_shapes=[pltpu.VMEM((S, SB_COLS), F32), pltpu.VMEM((S, SB_COLS), F32)]
        + [pltpu.VMEM((SB_PAIRS, 2 * SB_BLOCK, LANES), F32)] * 3,
        compiler_params=_cparams(("arbitrary", "arbitrary")), name=name)(*ins)


def adamw(w, g, m, v, name):
    shape = w.shape
    C = shape[-1] if w.ndim > 1 else shape[0]
    R = w.size // C
    tr = _tile(R, ROW_TILE, 8)

    def body(w_ref, g_ref, m_ref, v_ref, d_ref, nm_ref, nv_ref):
        gv = g_ref[...]
        m2 = ADAM_B1 * m_ref[...] + (1.0 - ADAM_B1) * gv
        v2 = ADAM_B2 * v_ref[...] + (1.0 - ADAM_B2) * (gv * gv)
        m_hat = m2 / (1.0 - ADAM_B1 ** ADAM_STEP)
        v_hat = v2 / (1.0 - ADAM_B2 ** ADAM_STEP)
        d_ref[...] = -ADAM_LR * (m_hat / (jnp.sqrt(v_hat) + ADAM_EPS) + ADAM_WD * w_ref[...])
        nm_ref[...] = m2
        nv_ref[...] = v2

    blk = pl.BlockSpec((tr, C), lambda i: (i, 0))
    sh = jax.ShapeDtypeStruct((R, C), F32)
    outs = pl.pallas_call(
        body, grid=(R // tr,), in_specs=[blk] * 4, out_specs=[blk] * 3, out_shape=[sh] * 3,
        compiler_params=_cparams(("parallel",)), name=name)(
            w.reshape(R, C), g.reshape(R, C), m.reshape(R, C), v.reshape(R, C))
    return tuple(o.reshape(shape) for o in outs)


def _place():
    return lax.axis_index("x"), lax.axis_index("y"), lax.axis_index("c")


def _other_chips(x, y):
    return [(1 - x, y), (x, 1 - y), (1 - x, 1 - y)]


_ANY = pl.BlockSpec(memory_space=pl.ANY)
LOCAL_CHUNKS = 8
SEG_LOCAL_CHUNKS = 4


def _remote(src, dst, send_sems, recv_sems, k, to):
    return pltpu.make_async_remote_copy(src_ref=src, dst_ref=dst, send_sem=send_sems.at[k], recv_sem=recv_sems.at[k],
                                        device_id=to, device_id_type=MESH)


def _gather_stage(segs, n_gath, stage):
    def make(ins, outs, send_sems, recv_sems, *local_sems):
        x, y, c = _place()
        sibling = (x, y, 1 - c)
        chips = _other_chips(x, y)
        copies, waits = [], []
        for s, (b, r0, n) in enumerate(segs):
            half = n // 2

            def piece(px, py, pc, b=b, r0=r0, half=half):
                return outs[b].at[2 * px + py, pl.ds(r0 + pc * half, half), :]

            for j, chip in enumerate(chips):
                if stage == 0:
                    mine = ins[n_gath + b].at[pl.ds(r0 + c * half, half), :]
                    cp = _remote(mine, piece(x, y, c), send_sems, recv_sems, 3 * s + j, (*chip, c))
                    landed, sender = piece(*chip, c), (*chip, c)
                else:
                    cp = _remote(piece(*chip, c), piece(*chip, c), send_sems, recv_sems, 3 * s + j, sibling)
                    landed, sender = piece(*chip, 1 - c), sibling
                copies.append(cp)
                waits.append(cp.wait_send)
                waits.append(_remote(landed, landed, send_sems, recv_sems, 3 * s + j, sender).wait_recv)
            if stage == 0:
                rows = n // SEG_LOCAL_CHUNKS
                for q in range(SEG_LOCAL_CHUNKS):
                    r = pl.ds(r0 + q * rows, rows)
                    cp = pltpu.make_async_copy(ins[n_gath + b].at[r, :], outs[b].at[2 * x + y, r, :],
                                               local_sems[0].at[SEG_LOCAL_CHUNKS * s + q])
                    copies.append(cp)
                    waits.append(cp.wait)
        return copies, waits

    return make


def _gather_sems(segs, stage):
    return [3 * len(segs), 3 * len(segs)] + ([SEG_LOCAL_CHUNKS * len(segs)] if stage == 0 else [])


def all_gather_segments(bufs, segs, name):
    nb = len(bufs)
    n0, n1 = len(_gather_sems(segs, 0)), len(_gather_sems(segs, 1))

    def body(*refs):
        in_refs, out_refs = refs[:nb], refs[nb:2 * nb]
        sems = refs[2 * nb:]
        for stage, stage_sems in ((0, sems[:n0]), (1, sems[n0:n0 + n1])):
            copies, waits = _gather_stage(segs, nb, stage)(list(out_refs) + list(in_refs), out_refs, *stage_sems)
            for cp in copies:
                cp.start()
            for wait in waits:
                wait()

    return pl.pallas_call(
        body, in_specs=[_ANY] * nb, out_specs=[_ANY] * nb,
        out_shape=[jax.ShapeDtypeStruct((N_CHIPS,) + b.shape, b.dtype) for b in bufs],
        scratch_shapes=[pltpu.SemaphoreType.DMA((n,)) for n in _gather_sems(segs, 0) + _gather_sems(segs, 1)],
        name=name)(*bufs)


def hosted_gather(gathered, bufs, segs, stage):
    arrays = list(gathered) + (list(bufs) if stage == 0 else [])
    return Hosted(arrays, len(gathered), _gather_sems(segs, stage), _gather_stage(segs, len(gathered), stage))


def _exchange(step, arrays, name):
    n_h, n_alias, n_fresh = len(step.arrays), step.n_out, len(step.fresh)

    def body(*refs):
        h_in = refs[:n_h]
        h_out = refs[n_h:n_h + n_alias + n_fresh]
        copies, waits = step.make(h_in, h_out, *refs[n_h + n_alias + n_fresh:])
        for cp in copies:
            cp.start()
        for wait in waits:
            wait()

    return pl.pallas_call(
        body, in_specs=[_ANY] * n_h, out_specs=[_ANY] * (n_alias + n_fresh),
        out_shape=[jax.ShapeDtypeStruct(a.shape, a.dtype) for a in step.arrays[:n_alias]] + step.fresh,
        scratch_shapes=[pltpu.SemaphoreType.DMA((n,)) for n in step.sems],
        input_output_aliases={k: k for k in range(n_alias)}, name=name)(*arrays)


def swap_step(gs):
    nb = len(gs)

    def make(ins, outs, send_sems, recv_sems):
        x, y, c = _place()
        copies = []
        for b in range(nb):
            half = gs[b].shape[1] // 2
            for k in range(N_CHIPS):
                copies.append(_remote(ins[b].at[k, pl.ds((1 - c) * half, half), :], outs[b].at[k], send_sems,
                                      recv_sems, N_CHIPS * b + k, (x, y, 1 - c)))
        return copies, [cp.wait for cp in copies]

    fresh = [jax.ShapeDtypeStruct((N_CHIPS, g.shape[1] // 2, g.shape[2]), g.dtype) for g in gs]
    return Hosted(list(gs), 0, [N_CHIPS * nb, N_CHIPS * nb], make, fresh)


def _core_index():
    return lax.axis_index("c").astype(jnp.int32).reshape(1)


def add_cores(g, theirs, name):
    n, Ph, Wd = theirs.shape
    tr = _tile(Ph, ROW_TILE, 16)
    steps = Ph // tr

    def body(c_ref, g_ref, t_ref, o_ref):
        o_ref[...] = _bf(g_ref[...] + t_ref[...])

    blk = pl.BlockSpec((None, tr, Wd), lambda k, i, c_ref: (k, i, 0))
    return pl.pallas_call(
        body,
        grid_spec=pltpu.PrefetchScalarGridSpec(
            num_scalar_prefetch=1, grid=(n, steps),
            in_specs=[pl.BlockSpec((None, tr, Wd), lambda k, i, c_ref: (k, c_ref[0] * steps + i, 0)), blk],
            out_specs=blk),
        out_shape=jax.ShapeDtypeStruct((n, Ph, Wd), BF16),
        compiler_params=_cparams(("parallel", "parallel")), name=name)(_core_index(), g, theirs)


def add_chips(slots, name):
    n, Ph, Wd = slots.shape
    tr = _tile(Ph, ROW_TILE, 16)
    steps = Ph // tr

    def body(c_ref, *refs):
        acc = refs[0][...].astype(F32)
        for k in range(1, n):
            acc = acc + refs[k][...].astype(F32)
        refs[n][...] = acc

    in_specs = [pl.BlockSpec((None, tr, Wd), functools.partial(lambda i, c_ref, k: (k, i, 0), k=k)) for k in range(n)]
    return pl.pallas_call(
        body,
        grid_spec=pltpu.PrefetchScalarGridSpec(
            num_scalar_prefetch=1, grid=(steps,), in_specs=in_specs,
            out_specs=pl.BlockSpec((tr, Wd), lambda i, c_ref: (c_ref[0] * steps + i, 0))),
        out_shape=jax.ShapeDtypeStruct((2 * Ph, Wd), F32),
        compiler_params=_cparams(("parallel",)), name=name)(_core_index(), *([slots] * n))


def _whole_tile_chunks(rows, most, tile_rows):
    return max(n for n in range(1, most + 1) if rows % (n * tile_rows) == 0)


def scatter_step(hs):
    nb = len(hs)
    chunks = [_whole_tile_chunks(h.shape[1], LOCAL_CHUNKS, 16) for h in hs]

    def make(ins, outs, send_sems, recv_sems, local_sems):
        x, y, c = _place()
        me = 2 * x + y
        copies, waits = [], []
        for b in range(nb):
            for j, (cx, cy) in enumerate(_other_chips(x, y)):
                cp = _remote(ins[b].at[2 * cx + cy], outs[b].at[me], send_sems, recv_sems, 3 * b + j, (cx, cy, c))
                copies.append(cp)
                waits.append(cp.wait_send)
                landed = outs[b].at[2 * cx + cy]
                waits.append(_remote(landed, landed, send_sems, recv_sems, 3 * b + j, (cx, cy, c)).wait_recv)
            rows = hs[b].shape[1] // chunks[b]
            for q in range(chunks[b]):
                r = pl.ds(q * rows, rows)
                cp = pltpu.make_async_copy(ins[b].at[me, r, :], outs[b].at[me, r, :],
                                           local_sems.at[sum(chunks[:b]) + q])
                copies.append(cp)
                waits.append(cp.wait)
        return copies, waits

    fresh = [jax.ShapeDtypeStruct(h.shape, h.dtype) for h in hs]
    return Hosted(list(hs), 0, [3 * nb, 3 * nb, sum(chunks)], make, fresh)


JOIN_CHUNKS = 4


def join_step(ts):
    nb = len(ts)
    n = JOIN_CHUNKS

    def make(ins, outs, send_sems, recv_sems):
        x, y, c = _place()
        copies, waits = [], []
        for b in range(nb):
            half = ts[b].shape[0] // 2
            rows = half // n
            for k in range(n):
                part = outs[b].at[pl.ds(c * half + k * rows, rows), :]
                cp = _remote(part, part, send_sems, recv_sems, n * b + k, (x, y, 1 - c))
                copies.append(cp)
                waits.append(cp.wait_send)
                landed = outs[b].at[pl.ds((1 - c) * half + k * rows, rows), :]
                waits.append(_remote(landed, landed, send_sems, recv_sems, n * b + k, (x, y, 1 - c)).wait_recv)
        return copies, waits

    return Hosted(list(ts), nb, [n * nb, n * nb], make)


WIDE = ['ffn1_w_gate', 'ffn1_w_up', 'ffn2_w_gate', 'ffn2_w_up']


def _rows_of(p):
    if p.shape[-1] == PACK_W and (p.size // PACK_W) % PART_ROWS == 0:
        return p.reshape(-1, PACK_W)
    flat = p.reshape(-1)
    rows = -(-flat.shape[0] // (PACK_W * PART_ROWS)) * PART_ROWS
    return jnp.pad(flat, (0, rows * PACK_W - flat.shape[0])).reshape(rows, PACK_W)


def _pack(parts):
    buf = jnp.concatenate([_rows_of(p) for p in parts], axis=0)
    rows = buf.shape[0]
    total = -(-rows // PACK_ROWS) * PACK_ROWS
    return jnp.pad(buf, ((0, total - rows), (0, 0)))


def _unpack(buf, shapes):
    outs, r = [], 0
    lead = buf.shape[:-2]
    for shp in shapes:
        n = math.prod(shp)
        rows = -(-n // (PACK_W * PART_ROWS)) * PART_ROWS
        blk = buf[..., r:r + rows, :]
        if n != rows * PACK_W:
            blk = blk.reshape(lead + (rows * PACK_W,))[..., :n]
        outs.append(blk.reshape(lead + tuple(shp)))
        r += rows
    return outs


class WeightStream:
    def __init__(self, shards):
        depth = shards['ffn1_w_gate'].shape[0]
        order = [(which, l) for l in range(depth) for which in ('ffn1', 'ffn2')]
        self.count = len(order)
        gate_up = [shards[f'{which}_w_{part}'][l].astype(BF16) for which, l in order for part in ('gate', 'up')]
        downs = [shards[f'{which}_w_down'][l].astype(BF16) for which, l in order]
        self.d, self.cols = gate_up[0].shape
        assert downs[0].shape == (self.cols, PACK_W)
        self.rest_names = [n for n in W_NAMES if n in SHARD_AXIS and n not in WIDE and not n.endswith('_w_down')]
        rest = [lax.bitcast_convert_type(shards[n], BF16) if n == 'a_v_norm' else shards[n].astype(BF16)
                for n in self.rest_names]
        self.rest_shapes = [p.shape for p in rest]
        self.bufs = [jnp.concatenate(gate_up, axis=0), _pack(downs + rest)]
        self.rest_row0 = self.count * self.cols
        first = self.segments(0) + [(1, self.rest_row0, self.bufs[1].shape[0] - self.rest_row0)]
        self.gathered = all_gather_segments(self.bufs, first, "all_gather_first")

    def segments(self, i):
        return [(0, 2 * self.d * i, 2 * self.d), (1, self.cols * i, self.cols)]

    def rest(self):
        out = {}
        for n, g in zip(self.rest_names, _unpack(self.gathered[1][:, self.rest_row0:], self.rest_shapes)):
            if n == 'a_v_norm':
                g = lax.bitcast_convert_type(g, F32)
            out[n] = jnp.concatenate([g[k] for k in range(N_CHIPS)], axis=SHARD_AXIS[n])
        return out

    def ffn(self, i):
        wide, narrow = self.gathered
        r = 2 * self.d * i
        gate = jnp.concatenate([wide[k, r:r + self.d] for k in range(N_CHIPS)], axis=1)
        up = jnp.concatenate([wide[k, r + self.d:r + 2 * self.d] for k in range(N_CHIPS)], axis=1)
        down = narrow[:, self.cols * i:self.cols * (i + 1)].reshape(N_CHIPS * self.cols, PACK_W)
        return (gate, None), (up, None), (down, None)

    def hosted(self, i, stage):
        if i >= self.count:
            return None
        return hosted_gather(self.gathered, self.bufs, self.segments(i), stage)

    def absorb(self, gathered):
        self.gathered = list(gathered)


class GradSink:
    def __init__(self):
        self.queue = []
        self.totals = {}
        self.held = {}

    def add(self, i, d_wg, d_wu, d_wd, ride):
        cols = d_wg.shape[1] // N_CHIPS
        wide = jnp.stack([jnp.concatenate([d_wg[:, k * cols:(k + 1) * cols], d_wu[:, k * cols:(k + 1) * cols]],
                                          axis=0) for k in range(N_CHIPS)], axis=0)
        bufs = [wide, d_wd.reshape(N_CHIPS, cols, d_wd.shape[1])]
        if ride:
            self.queue.append([0, bufs, i])
        else:
            self.held[i] = bufs

    def step(self):
        if not self.queue:
            return None
        stage, bufs, _ = self.queue[0]
        return (swap_step, scatter_step, join_step)[stage](bufs)

    def absorb(self, results):
        entry = self.queue[0]
        stage, bufs, i = entry
        if stage == 0:
            entry[1] = [add_cores(g, t, f"ffn{i}_add_cores_{b}") for b, (g, t) in enumerate(zip(bufs, results))]
        elif stage == 1:
            entry[1] = [add_chips(s, f"ffn{i}_add_chips_{b}") for b, s in enumerate(results)]
        else:
            self.totals[i] = list(results)
            self.queue.pop(0)
        entry[0] = stage + 1

    def reduce(self, grads, shard_shapes):
        assert not self.queue
        narrow = [n for n in W_NAMES if n in SHARD_AXIS and n in grads]
        repl = [n for n in W_NAMES if n not in SHARD_AXIS]
        full = {n: (jnp.stack(g, axis=0) if isinstance(g, list) else g) for n, g in grads.items()}
        blocks = []
        for k in range(N_CHIPS):
            parts = [jnp.split(full[n], N_CHIPS, axis=SHARD_AXIS[n])[k] for n in narrow]
            parts += [full[n] for n in repl]
            blocks.append(_pack(parts))
        held = sorted(self.held)
        gs = [b for i in held for b in self.held[i]] + [jnp.stack(blocks, axis=0)]
        theirs = _exchange(swap_step(gs), gs, "grads_swap_halves")
        sums = [add_cores(g, t, f"grads_add_cores_{b}") for b, (g, t) in enumerate(zip(gs, theirs))]
        slots = _exchange(scatter_step(sums), sums, "grads_scatter")
        halves = [add_chips(s, f"grads_add_chips_{b}") for b, s in enumerate(slots)]
        totals = _exchange(join_step(halves), halves, "grads_join_halves")
        for n, i in enumerate(held):
            self.totals[i] = totals[2 * n:2 * n + 2]
        shapes = [shard_shapes[n] for n in narrow] + [full[n].shape for n in repl]
        out = dict(zip(narrow + repl, _unpack(totals[-1], shapes)))
        d = self.totals[0][0].shape[0] // 2
        for w, which in enumerate(('ffn1', 'ffn2')):
            mine = [self.totals[i] for i in sorted(self.totals) if i % 2 == w]
            out[f'{which}_w_gate'] = jnp.stack([t[0][:d] for t in mine], axis=0)
            out[f'{which}_w_up'] = jnp.stack([t[0][d:] for t in mine], axis=0)
            out[f'{which}_w_down'] = jnp.stack([t[1] for t in mine], axis=0)
        return out


def _ffn_fwd(x, h, stream, i, next_gain, tag):
    wg, wu, wd = stream.ffn(i)
    ici = stream.hosted(i + 1, 0)
    if ici is None:
        g, u, a = ffn_up(h, wg, wu, tag + "_up")
        out = mm([(a, 0, wd, a.shape[1], 0)], res=x, scale=0.5, norm_next=next_gain, name=tag + "_down")
    else:
        (g, u, a), gathered = ffn_up(h, wg, wu, tag + "_up", hosted=ici)
        stream.absorb(gathered)
        out, gathered = mm([(a, 0, wd, a.shape[1], 0)], res=x, scale=0.5, norm_next=next_gain,
                           name=tag + "_down", hosted=stream.hosted(i + 1, 1))
        stream.absorb(gathered)
    xo, h_next = (out, None) if next_gain is None else out
    return xo, h_next, (x, h, g, u, a, wg, wu, wd)


def _carrying(sink, kernel, *args, **kwargs):
    step = sink.step()
    if step is None:
        return kernel(*args, **kwargs)
    out, results = kernel(*args, hosted=step, **kwargs)
    sink.absorb(results)
    return out


def _ffn_bwd(dxo, saved, gain, sink, i, tag):
    x, h, g, u, a, wg, wu, wd = saved
    Fd = a.shape[1]
    dxo, dxo_b = dxo
    dgp, du, d_wd = _carrying(sink, ffn_down_bwd, dxo_b, wd, g, u, a, tag + "_down_bwd")
    d_wg, d_wu = _carrying(sink, mm_tn_pair, h, dgp, du, name=tag + "_dwgu")
    sink.add(i, d_wg, d_wu, d_wd, ride=i > 0)
    return _carrying(sink, mm, [(dgp, 0, wg, Fd, 0), (du, 0, wu, Fd, 0)], tb=True, res=dxo, norm_bwd=(x, gain),
                     name=tag + "_up_bwd")


def local_step(x, mem, W, stream, sink, target):
    D = x.shape[1]
    depth = W['ffn1_norm'].shape[0]
    n_a = W['a_w_in'].shape[0]
    G = {}
    mem_h = rms_fwd(mem, W['mem_norm'], "mem_norm")
    b_mats = jnp.broadcast_to(W['a_b_spatial'][..., None], W['a_b_spatial'].shape + (GM_CHUNK,))
    w_kv = (W['w_kv'], None)

    def mixer_w(l):
        if l < n_a:
            return (W['a_w_in'], l), (W['a_w_out'], l), 2 * GM_W
        return (W['b_w_in'], l - n_a), (W['b_w_out'], l - n_a), GM_W

    saved = []
    kv = kvn = x_kv = None
    h = rms_fwd(x, W['ffn1_norm'][0], "l0_ffn1_norm")
    for l in range(depth):
        if l == n_a:
            x_kv = x
            kvn = rms_fwd(x, W['kv_norm'], "kv_norm")
            kv = mm([(kvn, 0, w_kv, D, 0)], out_dtype=BF16, name="kv_proj")
        x, hm, s1 = _ffn_fwd(x, h, stream, 2 * l, W['mix_norm'][l], f"l{l}_ffn1")
        mem_kv = mm([(mem_h, 0, (W['w_mem_kv'], l), D, 0)], name=f"l{l}_mem_kv")
        w_in, w_out, tok_w = mixer_w(l)
        proj = mm([(hm, 0, w_in, D, 0)], name=f"l{l}_mix_in")
        if l < n_a:
            y_tok = gmlp_fwd(proj, W['a_v_norm'][l], W['a_w_spatial'][l], b_mats[l], f"l{l}_gmlp")
        else:
            y_tok = sb_fwd(proj, kv, f"l{l}_sb")
        q_cb = tok_w // MEM_W
        y_mem = memattn_fwd(proj, q_cb, mem_kv, f"l{l}_memattn")
        x_mid = x
        x, h2 = mm([(y_tok, 0, w_out, GM_W, 0), (y_mem, 0, w_out, MEM_W, GM_W // MEM_W)], res=x,
                   norm_next=W['ffn2_norm'][l], name=f"l{l}_mix_out")
        sm = (x_mid, hm, mem_kv, proj, y_tok, y_mem, q_cb)
        next_gain = W['ffn1_norm'][l + 1] if l + 1 < depth else None
        x, h, s2 = _ffn_fwd(x, h2, stream, 2 * l + 1, next_gain, f"l{l}_ffn2")
        saved.append((s1, sm, s2))

    loss, dx, d_final = final_loss(x, W['final_norm'], target, "loss_head")
    G['final_norm'] = d_final.reshape(-1)

    per_layer = {n: [None] * depth for n in ['ffn1_norm', 'mix_norm', 'ffn2_norm', 'w_mem_kv']}
    per_a = {n: [None] * n_a for n in ['a_w_in', 'a_v_norm', 'a_w_spatial', 'a_b_spatial', 'a_w_out']}
    per_b = {n: [None] * (depth - n_a) for n in ['b_w_in', 'b_w_out']}
    d_mem_h = None
    dk = dv = None
    for l in reversed(range(depth)):
        s1, sm, s2 = saved[l]
        dx, dg = _ffn_bwd(dx, s2, W['ffn2_norm'][l], sink, 2 * l + 1, f"l{l}_ffn2")
        per_layer['ffn2_norm'][l] = dg.reshape(-1)

        x_mid, hm, mem_kv, proj, y_tok, y_mem, q_cb = sm
        is_a = l < n_a
        w_in, w_out, tok_w = mixer_w(l)
        dy = mm([(dx[1], 0, w_out, D, 0)], tb=True, name=f"l{l}_mix_out_bwd")
        d_w_out = jnp.concatenate([mm_tn(y_tok, dx[1], name=f"l{l}_dwout_tok"),
                                   mm_tn(y_mem, dx[1], name=f"l{l}_dwout_mem")], axis=0)
        dq_mem, d_mem_kv = memattn_bwd(proj, q_cb, mem_kv, dy, GM_W // MEM_W, f"l{l}_memattn_bwd")
        if is_a:
            d_tok, d_ws, d_bs, d_vg = gmlp_bwd(proj, dy, W['a_v_norm'][l], W['a_w_spatial'][l], b_mats[l],
                                               f"l{l}_gmlp_bwd")
            per_a['a_w_spatial'][l], per_a['a_b_spatial'][l] = d_ws, d_bs[:, :, 0]
            per_a['a_v_norm'][l], per_a['a_w_out'][l] = d_vg.reshape(-1), d_w_out
        else:
            d_tok, dk, dv = sb_bwd(proj, kv, dy, y_tok, dk, dv, f"l{l}_sb_bwd")
            per_b['b_w_out'][l - n_a] = d_w_out
        d_w_in = jnp.concatenate([mm_tn(hm, d_tok, name=f"l{l}_dwin_tok"),
                                  mm_tn(hm, dq_mem, name=f"l{l}_dwin_mem")], axis=1)
        (per_a['a_w_in'] if is_a else per_b['b_w_in'])[l if is_a else l - n_a] = d_w_in
        dx, dg = mm([(d_tok, 0, w_in, tok_w, 0), (dq_mem, 0, w_in, MEM_W, tok_w // MEM_W)], tb=True, res=dx[0],
                    norm_bwd=(x_mid, W['mix_norm'][l]), name=f"l{l}_mix_in_bwd")
        per_layer['mix_norm'][l] = dg.reshape(-1)
        per_layer['w_mem_kv'][l] = mm_tn(mem_h, d_mem_kv, name=f"l{l}_dw_mem_kv")
        d_mem_h = mm([(d_mem_kv, 0, (W['w_mem_kv'], l), 2 * MEM_W, 0)], tb=True, res=d_mem_h,
                     name=f"l{l}_mem_kv_bwd")

        dx, dg = _ffn_bwd(dx, s1, W['ffn1_norm'][l], sink, 2 * l, f"l{l}_ffn1")
        per_layer['ffn1_norm'][l] = dg.reshape(-1)
        if l == n_a:
            G['w_kv'] = jnp.concatenate([mm_tn(kvn, dk, name="dw_k"), mm_tn(kvn, dv, name="dw_v")], axis=1)
            dx, dg = mm([(dk, 0, w_kv, GM_W, 0), (dv, 0, w_kv, GM_W, 1)], tb=True, res=dx[0],
                        norm_bwd=(x_kv, W['kv_norm']), name="kv_proj_bwd")
            G['kv_norm'] = dg.reshape(-1)

    G['mem_norm'] = rms_gain_grad(mem, d_mem_h, "mem_norm_bwd").reshape(-1)
    for d in (per_layer, per_a, per_b):
        G.update(d)
    return loss, dx[0], G


def kernel(x, mem, ffn1_norm, ffn1_w_gate, ffn1_w_up, ffn1_w_down, mix_norm, ffn2_norm, ffn2_w_gate, ffn2_w_up, ffn2_w_down, mem_norm, w_mem_kv, a_w_in, a_v_norm, a_w_spatial, a_b_spatial, a_w_out, kv_norm, w_kv, b_w_in, b_w_out, final_norm, loss_target, m_ffn1_norm, m_ffn1_w_gate, m_ffn1_w_up, m_ffn1_w_down, m_mix_norm, m_ffn2_norm, m_ffn2_w_gate, m_ffn2_w_up, m_ffn2_w_down, m_mem_norm, m_w_mem_kv, m_a_w_in, m_a_v_norm, m_a_w_spatial, m_a_b_spatial, m_a_w_out, m_kv_norm, m_w_kv, m_b_w_in, m_b_w_out, m_final_norm, v_ffn1_norm, v_ffn1_w_gate, v_ffn1_w_up, v_ffn1_w_down, v_mix_norm, v_ffn2_norm, v_ffn2_w_gate, v_ffn2_w_up, v_ffn2_w_down, v_mem_norm, v_w_mem_kv, v_a_w_in, v_a_v_norm, v_a_w_spatial, v_a_b_spatial, v_a_w_out, v_kv_norm, v_w_kv, v_b_w_in, v_b_w_out, v_final_norm):
    weights = dict(zip(W_NAMES, [ffn1_norm, ffn1_w_gate, ffn1_w_up, ffn1_w_down, mix_norm, ffn2_norm, ffn2_w_gate,
                                 ffn2_w_up, ffn2_w_down, mem_norm, w_mem_kv, a_w_in, a_v_norm, a_w_spatial,
                                 a_b_spatial, a_w_out, kv_norm, w_kv, b_w_in, b_w_out, final_norm]))
    m_in = dict(zip(W_NAMES, [m_ffn1_norm, m_ffn1_w_gate, m_ffn1_w_up, m_ffn1_w_down, m_mix_norm, m_ffn2_norm,
                              m_ffn2_w_gate, m_ffn2_w_up, m_ffn2_w_down, m_mem_norm, m_w_mem_kv, m_a_w_in,
                              m_a_v_norm, m_a_w_spatial, m_a_b_spatial, m_a_w_out, m_kv_norm, m_w_kv, m_b_w_in,
                              m_b_w_out, m_final_norm]))
    v_in = dict(zip(W_NAMES, [v_ffn1_norm, v_ffn1_w_gate, v_ffn1_w_up, v_ffn1_w_down, v_mix_norm, v_ffn2_norm,
                              v_ffn2_w_gate, v_ffn2_w_up, v_ffn2_w_down, v_mem_norm, v_w_mem_kv, v_a_w_in,
                              v_a_v_norm, v_a_w_spatial, v_a_b_spatial, v_a_w_out, v_kv_norm, v_w_kv, v_b_w_in,
                              v_b_w_out, v_final_norm]))

    stream = WeightStream({n: weights[n] for n in SHARD_AXIS})
    W = {n: weights[n] for n in W_NAMES if n not in SHARD_AXIS}
    W.update(stream.rest())
    sink = GradSink()
    loss, dx, grads = local_step(x[0], mem[0], W, stream, sink, loss_target[0])
    total = sink.reduce(grads, {n: weights[n].shape for n in SHARD_AXIS})
    loss = lax.psum(loss[0, 0], ("x", "y", "c"))

    deltas, new_m, new_v = {}, {}, {}
    for n in W_NAMES:
        deltas[n], new_m[n], new_v[n] = adamw(weights[n], total[n], m_in[n], v_in[n], "adamw_" + n)
    return (loss, dx[None], *[total[n] for n in W_NAMES], *[deltas[n] for n in W_NAMES],
            *[new_m[n] for n in W_NAMES], *[new_v[n] for n in W_NAMES])
```

```python
import functools
import math

import jax
import jax.numpy as jnp
from jax import lax
from jax.experimental import pallas as pl
from jax.experimental.pallas import tpu as pltpu

F32 = jnp.float32
BF16 = jnp.bfloat16
EPS = 1e-6
LANES = 128
ROW_TILE = 512
COL_TILE = 1408
PACK_ROWS = 256
PART_ROWS = 16
VMEM_LIMIT = 56 * 1024 * 1024

W_NAMES = ['ffn1_norm', 'ffn1_w_gate', 'ffn1_w_up', 'ffn1_w_down', 'mix_norm', 'ffn2_norm', 'ffn2_w_gate',
           'ffn2_w_up', 'ffn2_w_down', 'mem_norm', 'w_mem_kv', 'a_w_in', 'a_v_norm', 'a_w_spatial',
           'a_b_spatial', 'a_w_out', 'kv_norm', 'w_kv', 'b_w_in', 'b_w_out', 'final_norm']
SHARD_AXIS = {'ffn1_w_gate': 2, 'ffn1_w_up': 2, 'ffn1_w_down': 1, 'ffn2_w_gate': 2, 'ffn2_w_up': 2,
              'ffn2_w_down': 1, 'w_mem_kv': 1, 'a_w_in': 2, 'a_v_norm': 1, 'a_w_out': 1, 'w_kv': 1,
              'b_w_in': 1, 'b_w_out': 1}
N_CHIPS = 4
PACK_W = 1024

MEM_HEADS = 4
MEM_W = 256
HEAD_DIM = 64
GM_W = 768
GM_GROUPS = 6
GM_CHUNK = 128
CHUNK = 64
SB_BLOCK = 128
SB_Q_CHUNK = 1024
SB_DEAD = -104.0
SB_PAIRS = 2
SB_COLS = SB_PAIRS * LANES

ADAM_LR, ADAM_B1, ADAM_B2, ADAM_EPS, ADAM_WD, ADAM_STEP = 0.001, 0.9, 0.999, 1e-08, 0.01, 10

MESH = pl.DeviceIdType.MESH


def _cparams(sem=None):
    return pltpu.CompilerParams(dimension_semantics=sem, vmem_limit_bytes=VMEM_LIMIT)


class Hosted:
    def __init__(self, arrays, n_out, sems, make, fresh=()):
        self.arrays, self.n_out, self.sems, self.make, self.fresh = arrays, n_out, sems, make, list(fresh)


def _call(body, *, grid, in_specs, out_specs, out_shape, ins, semantics, name, hosted=None):
    if hosted is None:
        return pl.pallas_call(body, grid=grid, in_specs=in_specs, out_specs=out_specs, out_shape=out_shape,
                              compiler_params=_cparams(semantics), name=name)(*ins)
    n_in, n_out = len(ins), len(out_shape)
    n_h, n_alias = len(hosted.arrays), hosted.n_out
    n_ho = n_alias + len(hosted.fresh)

    def hosting_body(*refs):
        base_in, h_in = refs[:n_in], refs[n_in:n_in + n_h]
        base_out = refs[n_in + n_h:n_in + n_h + n_out]
        h_out = refs[n_in + n_h + n_out:n_in + n_h + n_out + n_ho]
        sems = refs[n_in + n_h + n_out + n_ho:]
        copies, waits = hosted.make(h_in, h_out, *sems)
        first = last = None
        for axis, extent in enumerate(grid):
            at_start, at_end = pl.program_id(axis) == 0, pl.program_id(axis) == extent - 1
            first = at_start if first is None else jnp.logical_and(first, at_start)
            last = at_end if last is None else jnp.logical_and(last, at_end)

        @pl.when(first)
        def _():
            for cp in copies:
                cp.start()

        body(*base_in, *base_out)

        @pl.when(last)
        def _():
            for wait in waits:
                wait()

    any_spec = pl.BlockSpec(memory_space=pl.ANY)
    outs = pl.pallas_call(
        hosting_body, grid=grid, in_specs=list(in_specs) + [any_spec] * n_h,
        out_specs=list(out_specs) + [any_spec] * n_ho,
        out_shape=(list(out_shape) + [jax.ShapeDtypeStruct(a.shape, a.dtype) for a in hosted.arrays[:n_alias]]
                   + hosted.fresh),
        scratch_shapes=[pltpu.SemaphoreType.DMA((n,)) for n in hosted.sems],
        input_output_aliases={n_in + k: n_out + k for k in range(n_alias)},
        compiler_params=_cparams(("arbitrary",) * len(grid)), name=name)(*ins, *hosted.arrays)
    return outs[:n_out], outs[n_out:]


def _tile(n, target, mult=LANES):
    best = None
    for t in range(mult, min(n, target) + 1, mult):
        if n % t == 0:
            best = t
    return best if best is not None else n


def _dot(a, b):
    return jnp.dot(a, b, preferred_element_type=F32)


def _dot_nt(a, b):
    return lax.dot_general(a, b, (((1,), (1,)), ((), ())), preferred_element_type=F32)


def _dot_tn(a, b):
    return lax.dot_general(a, b, (((0,), (0,)), ((), ())), preferred_element_type=F32)


def _bf(v):
    return v.astype(BF16)


def rms_fwd(x, gain, name):
    S, D = x.shape
    tm = _tile(S, ROW_TILE, 8)

    def body(x_ref, g_ref, o_ref):
        xf = x_ref[...]
        r = lax.rsqrt(jnp.mean(xf * xf, axis=-1, keepdims=True) + EPS)
        o_ref[...] = ((xf * r) * g_ref[...]).astype(o_ref.dtype)

    return pl.pallas_call(
        body, grid=(S // tm,),
        in_specs=[pl.BlockSpec((tm, D), lambda i: (i, 0)), pl.BlockSpec((1, D), lambda i: (0, 0))],
        out_specs=pl.BlockSpec((tm, D), lambda i: (i, 0)),
        out_shape=jax.ShapeDtypeStruct((S, D), BF16),
        compiler_params=_cparams(("parallel",)), name=name)(x, gain.reshape(1, D))


def rms_gain_grad(x, dh, name):
    S, D = x.shape
    tm = _tile(S, ROW_TILE, 8)

    def body(x_ref, dh_ref, dg_ref):
        xf = x_ref[...]
        r = lax.rsqrt(jnp.mean(xf * xf, axis=-1, keepdims=True) + EPS)

        @pl.when(pl.program_id(0) == 0)
        def _():
            dg_ref[...] = jnp.zeros_like(dg_ref)

        dg_ref[...] += jnp.sum(dh_ref[...] * (xf * r), axis=0, keepdims=True)

    row = pl.BlockSpec((tm, D), lambda i: (i, 0))
    return pl.pallas_call(
        body, grid=(S // tm,), in_specs=[row, row], out_specs=pl.BlockSpec((1, D), lambda i: (0, 0)),
        out_shape=jax.ShapeDtypeStruct((1, D), F32),
        compiler_params=_cparams(("arbitrary",)), name=name)(x, dh)


def final_loss(x, gain, target, name):
    S, D = x.shape
    tm = _tile(S, ROW_TILE, 16)

    def body(x_ref, g_ref, t_ref, loss_ref, dx_ref, dxb_ref, dg_ref):
        xf = x_ref[...]
        r = lax.rsqrt(jnp.mean(xf * xf, axis=-1, keepdims=True) + EPS)
        xhat = xf * r
        g = g_ref[...]
        diff = xhat * g - t_ref[...]
        dy = diff * (1.0 / D)
        dxhat = dy * g
        m = jnp.mean(dxhat * xhat, axis=-1, keepdims=True)
        dx = r * (dxhat - xhat * m)
        dx_ref[...] = dx
        dxb_ref[...] = _bf(dx)

        @pl.when(pl.program_id(0) == 0)
        def _():
            dg_ref[...] = jnp.zeros_like(dg_ref)
            loss_ref[...] = jnp.zeros_like(loss_ref)

        dg_ref[...] += jnp.sum(dy * xhat, axis=0, keepdims=True)
        per_tok = jnp.sum(diff * diff, axis=-1, keepdims=True) * (0.5 / D)
        loss_ref[...] += jnp.sum(per_tok, axis=0, keepdims=True)

    row = pl.BlockSpec((tm, D), lambda i: (i, 0))
    vec = pl.BlockSpec((1, D), lambda i: (0, 0))
    one = pl.BlockSpec((1, 1), lambda i: (0, 0))
    loss, dx, dxb, dg = pl.pallas_call(
        body, grid=(S // tm,), in_specs=[row, vec, row], out_specs=[one, row, row, vec],
        out_shape=[jax.ShapeDtypeStruct((1, 1), F32), jax.ShapeDtypeStruct((S, D), F32),
                   jax.ShapeDtypeStruct((S, D), BF16), jax.ShapeDtypeStruct((1, D), F32)],
        compiler_params=_cparams(("arbitrary",)), name=name)(x, gain.reshape(1, D), target)
    return loss, (dx, dxb), dg


def _wspec(w, blk, idx):
    _, l = w
    if l is None:
        return pl.BlockSpec(blk, idx)
    return pl.BlockSpec((None,) + blk, lambda *g: (l,) + idx(*g))


def mm(pairs, *, tb=False, res=None, scale=1.0, out_dtype=F32, norm_bwd=None, norm_next=None, hosted=None, name):
    M = pairs[0][0].shape[0]
    N = pairs[0][2][0].shape[-2 if tb else -1]
    tm = _tile(M, ROW_TILE, 16)
    tn = N if (norm_bwd is not None or norm_next is not None) else _tile(N, COL_TILE)
    n_p = len(pairs)
    has_res = res is not None
    n_in = 2 * n_p + has_res + (2 if norm_bwd is not None else 0) + (norm_next is not None)

    def body(*refs):
        acc = None
        for p in range(n_p):
            a = _bf(refs[2 * p][...])
            b = _bf(refs[2 * p + 1][...])
            d = _dot_nt(a, b) if tb else _dot(a, b)
            acc = d if acc is None else acc + d
        if scale != 1.0:
            acc = acc * scale
        if norm_bwd is None:
            if has_res:
                acc = acc + refs[2 * n_p][...]
            refs[n_in][...] = acc.astype(refs[n_in].dtype)
            if norm_next is not None:
                r = lax.rsqrt(jnp.mean(acc * acc, axis=-1, keepdims=True) + EPS)
                refs[n_in + 1][...] = _bf((acc * r) * refs[n_in - 1][...])
            return
        x_ref, g_ref = refs[n_in - 2], refs[n_in - 1]
        dx_ref, dxb_ref, dg_ref = refs[n_in:]
        xf = x_ref[...]
        r = lax.rsqrt(jnp.mean(xf * xf, axis=-1, keepdims=True) + EPS)
        xhat = xf * r
        dxhat = acc * g_ref[...]
        m = jnp.mean(dxhat * xhat, axis=-1, keepdims=True)
        dx = r * (dxhat - xhat * m)
        if has_res:
            dx = dx + refs[2 * n_p][...]
        dx_ref[...] = dx
        dxb_ref[...] = _bf(dx)

        @pl.when(pl.program_id(1) == 0)
        def _():
            dg_ref[...] = jnp.zeros_like(dg_ref)

        dg_ref[...] += jnp.sum(acc * xhat, axis=0, keepdims=True)

    ins, in_specs = [], []
    for a, cb, w, K, kb in pairs:
        ins += [a, w[0]]
        in_specs.append(pl.BlockSpec((tm, K), functools.partial(lambda j, i, cb: (i, cb), cb=cb)))
        if tb:
            in_specs.append(_wspec(w, (tn, K), functools.partial(lambda j, i, kb: (j, kb), kb=kb)))
        else:
            in_specs.append(_wspec(w, (K, tn), functools.partial(lambda j, i, kb: (kb, j), kb=kb)))
    tile = pl.BlockSpec((tm, tn), lambda j, i: (i, j))
    if has_res:
        ins.append(res)
        in_specs.append(tile)
    if norm_bwd is None:
        out_shape = [jax.ShapeDtypeStruct((M, N), out_dtype)]
        if norm_next is not None:
            ins.append(norm_next.reshape(1, N))
            in_specs.append(pl.BlockSpec((1, N), lambda j, i: (0, 0)))
            out_shape.append(jax.ShapeDtypeStruct((M, N), BF16))
        out = _call(body, grid=(N // tn, M // tm), in_specs=in_specs, out_specs=[tile] * len(out_shape),
                    out_shape=out_shape, ins=ins, semantics=("parallel", "parallel"), name=name, hosted=hosted)
        base = out if hosted is None else out[0]
        value = base[0] if norm_next is None else (base[0], base[1])
        return value if hosted is None else (value, out[1])
    x, gain = norm_bwd
    vec = pl.BlockSpec((1, N), lambda j, i: (0, 0))
    out = _call(
        body, grid=(1, M // tm), in_specs=in_specs + [tile, vec], out_specs=[tile, tile, vec],
        out_shape=[jax.ShapeDtypeStruct((M, N), F32), jax.ShapeDtypeStruct((M, N), BF16),
                   jax.ShapeDtypeStruct((1, N), F32)], ins=ins + [x, gain.reshape(1, N)],
        semantics=("arbitrary", "arbitrary"), name=name, hosted=hosted)
    dx, dxb, dg = out if hosted is None else out[0]
    return ((dx, dxb), dg) if hosted is None else (((dx, dxb), dg), out[1])


def mm_tn(a, b, *, hosted=None, name):
    S, Ka = a.shape
    Nb = b.shape[1]
    ts = _tile(S, 2 * ROW_TILE, 8)
    tk = _tile(Ka, COL_TILE)
    tn = _tile(Nb, COL_TILE)

    def body(a_ref, b_ref, o_ref):
        s = pl.program_id(2)

        @pl.when(s == 0)
        def _():
            o_ref[...] = jnp.zeros_like(o_ref)

        o_ref[...] += _dot_tn(_bf(a_ref[...]), _bf(b_ref[...]))

    out = _call(
        body, grid=(Ka // tk, Nb // tn, S // ts),
        in_specs=[pl.BlockSpec((ts, tk), lambda k, n, s: (s, k)),
                  pl.BlockSpec((ts, tn), lambda k, n, s: (s, n))],
        out_specs=[pl.BlockSpec((tk, tn), lambda k, n, s: (k, n))],
        out_shape=[jax.ShapeDtypeStruct((Ka, Nb), F32)], ins=[a, b],
        semantics=("parallel", "parallel", "arbitrary"), name=name, hosted=hosted)
    return out[0] if hosted is None else (out[0][0], out[1])


def mm_tn_pair(a, b1, b2, *, hosted=None, name):
    S, Ka = a.shape
    Nb = b1.shape[1]
    ts = _tile(S, 2 * ROW_TILE, 8)
    tk = _tile(Ka, COL_TILE)
    tn = _tile(Nb, COL_TILE)

    def body(a_ref, b1_ref, b2_ref, o1_ref, o2_ref):
        @pl.when(pl.program_id(2) == 0)
        def _():
            o1_ref[...] = jnp.zeros_like(o1_ref)
            o2_ref[...] = jnp.zeros_like(o2_ref)

        av = _bf(a_ref[...])
        o1_ref[...] += _dot_tn(av, _bf(b1_ref[...]))
        o2_ref[...] += _dot_tn(av, _bf(b2_ref[...]))

    b_spec = pl.BlockSpec((ts, tn), lambda k, n, s: (s, n))
    o_spec = pl.BlockSpec((tk, tn), lambda k, n, s: (k, n))
    sh = jax.ShapeDtypeStruct((Ka, Nb), F32)
    out = _call(
        body, grid=(Ka // tk, Nb // tn, S // ts),
        in_specs=[pl.BlockSpec((ts, tk), lambda k, n, s: (s, k)), b_spec, b_spec],
        out_specs=[o_spec, o_spec], out_shape=[sh, sh], ins=[a, b1, b2],
        semantics=("parallel", "parallel", "arbitrary"), name=name, hosted=hosted)
    return tuple(out) if hosted is None else (tuple(out[0]), out[1])


def _sigmoid(x):
    return 0.5 * jnp.tanh(0.5 * x) + 0.5


def ffn_up(h, wg, wu, name, hosted=None):
    S, D = h.shape
    Fd = wg[0].shape[-1]
    tm = _tile(S, ROW_TILE, 8)
    tn = _tile(Fd, COL_TILE)

    def body(h_ref, wg_ref, wu_ref, g_ref, u_ref, a_ref):
        hv = h_ref[...]
        g = _dot(hv, wg_ref[...])
        u = _dot(hv, wu_ref[...])
        g_ref[...] = _bf(g)
        u_ref[...] = _bf(u)
        a_ref[...] = _bf(g * _sigmoid(g) * u)

    blk = pl.BlockSpec((tm, tn), lambda j, i: (i, j))
    sh = jax.ShapeDtypeStruct((S, Fd), BF16)
    return _call(
        body, grid=(Fd // tn, S // tm),
        in_specs=[pl.BlockSpec((tm, D), lambda j, i: (i, 0)), _wspec(wg, (D, tn), lambda j, i: (0, j)),
                  _wspec(wu, (D, tn), lambda j, i: (0, j))],
        out_specs=[blk, blk, blk], out_shape=[sh, sh, sh], ins=[h, wg[0], wu[0]],
        semantics=("parallel", "parallel"), name=name, hosted=hosted)


def ffn_down_bwd(dxo, wd, g, u, a, name, hosted=None):
    S, D = dxo.shape
    Fd = wd[0].shape[-2]
    tm = _tile(S, ROW_TILE, 8)
    tn = _tile(Fd, COL_TILE)

    def body(dx_ref, wd_ref, g_ref, u_ref, a_ref, dg_ref, du_ref, dwd_ref):
        dx = _bf(dx_ref[...])
        da = _dot_nt(dx, wd_ref[...]) * 0.5
        g = g_ref[...].astype(F32)
        u = u_ref[...].astype(F32)
        sg = _sigmoid(g)
        dg_ref[...] = _bf(da * u * (sg * (1.0 + g * (1.0 - sg))))
        du_ref[...] = _bf(da * (g * sg))

        @pl.when(pl.program_id(1) == 0)
        def _():
            dwd_ref[...] = jnp.zeros_like(dwd_ref)

        dwd_ref[...] += _dot_tn(a_ref[...], dx) * 0.5

    blk = pl.BlockSpec((tm, tn), lambda j, i: (i, j))
    sh = jax.ShapeDtypeStruct((S, Fd), BF16)
    return _call(
        body, grid=(Fd // tn, S // tm),
        in_specs=[pl.BlockSpec((tm, D), lambda j, i: (i, 0)), _wspec(wd, (tn, D), lambda j, i: (j, 0)),
                  blk, blk, blk],
        out_specs=[blk, blk, pl.BlockSpec((tn, D), lambda j, i: (j, 0))],
        out_shape=[sh, sh, jax.ShapeDtypeStruct((Fd, D), F32)], ins=[dxo, wd[0], g, u, a],
        semantics=("parallel", "arbitrary"), name=name, hosted=hosted)


_GELU_C = math.sqrt(2.0 / math.pi)


def _gelu(p):
    return 0.5 * p * (1.0 + jnp.tanh(_GELU_C * (p + 0.044715 * (p * p * p))))


def _gelu_grad(p):
    th = jnp.tanh(_GELU_C * (p + 0.044715 * (p * p * p)))
    return 0.5 * (1.0 + th) + 0.5 * p * (1.0 - th * th) * (_GELU_C * (1.0 + 3.0 * 0.044715 * (p * p)))


def _chunk_mask():
    t = lax.broadcasted_iota(jnp.int32, (GM_CHUNK, GM_CHUNK), 0)
    s = lax.broadcasted_iota(jnp.int32, (GM_CHUNK, GM_CHUNK), 1)
    return (s // CHUNK) <= (t // CHUNK)


def gmlp_fwd(proj, v_gain, w_s, b_mat, name):
    S = proj.shape[0]
    tw = _tile(S, ROW_TILE, GM_CHUNK)
    n_win = tw // GM_CHUNK

    def body(p_ref, gain_ref, w_ref, b_ref, y_ref):
        mask = _chunk_mask()
        v = _gelu(p_ref[:, GM_W:])
        r = lax.rsqrt(jnp.mean(v * v, axis=-1, keepdims=True) + EPS)
        vn = _bf(v * r * gain_ref[...])
        for g in range(GM_GROUPS):
            wm = _bf(jnp.where(mask, w_ref[g], 0.0))
            cs = slice(g * GM_CHUNK, (g + 1) * GM_CHUNK)
            for w in range(n_win):
                rs = slice(w * GM_CHUNK, (w + 1) * GM_CHUNK)
                mixed = _dot(wm, vn[rs, cs]) + b_ref[g]
                y_ref[rs, cs] = _bf(_gelu(p_ref[rs, cs]) * mixed)

    return pl.pallas_call(
        body, grid=(S // tw,),
        in_specs=[pl.BlockSpec((tw, 2 * GM_W), lambda i: (i, 0)), pl.BlockSpec((1, GM_W), lambda i: (0, 0)),
                  pl.BlockSpec((GM_GROUPS, GM_CHUNK, GM_CHUNK), lambda i: (0, 0, 0)),
                  pl.BlockSpec((GM_GROUPS, GM_CHUNK, GM_CHUNK), lambda i: (0, 0, 0))],
        out_specs=pl.BlockSpec((tw, GM_W), lambda i: (i, 0)),
        out_shape=jax.ShapeDtypeStruct((S, GM_W), BF16),
        compiler_params=_cparams(("parallel",)), name=name)(proj, v_gain.reshape(1, GM_W), w_s, b_mat)


def gmlp_bwd(proj, dy, v_gain, w_s, b_mat, name):
    S = proj.shape[0]
    tw = _tile(S, ROW_TILE, GM_CHUNK)
    n_win = tw // GM_CHUNK

    def body(p_ref, dy_ref, gain_ref, w_ref, b_ref, dp_ref, dw_ref, db_ref, dgain_ref, dvn_ref):
        step = pl.program_id(0)

        @pl.when(step == 0)
        def _():
            dw_ref[...] = jnp.zeros_like(dw_ref)
            db_ref[...] = jnp.zeros_like(db_ref)
            dgain_ref[...] = jnp.zeros_like(dgain_ref)

        mask = _chunk_mask()
        pv = p_ref[:, GM_W:]
        v = _gelu(pv)
        r = lax.rsqrt(jnp.mean(v * v, axis=-1, keepdims=True) + EPS)
        vhat = v * r
        gain = gain_ref[...]
        vn = _bf(vhat * gain)
        for g in range(GM_GROUPS):
            wm = _bf(jnp.where(mask, w_ref[g], 0.0))
            cs = slice(g * GM_CHUNK, (g + 1) * GM_CHUNK)
            dw_acc = jnp.zeros((GM_CHUNK, GM_CHUNK), F32)
            db_acc = jnp.zeros((GM_CHUNK, GM_CHUNK), F32)
            for w in range(n_win):
                rs = slice(w * GM_CHUNK, (w + 1) * GM_CHUNK)
                pu = p_ref[rs, cs]
                u = _gelu(pu)
                vn_blk = vn[rs, cs]
                mixed = _dot(wm, vn_blk) + b_ref[g]
                dyb = dy_ref[rs, cs]
                dp_ref[rs, cs] = _bf(dyb * mixed * _gelu_grad(pu))
                dmix = dyb * u
                db_acc = db_acc + dmix
                dmix_b = _bf(dmix)
                dw_acc = dw_acc + _dot_nt(dmix_b, vn_blk)
                dvn_ref[rs, cs] = _dot_tn(wm, dmix_b)
            dw_ref[g] += jnp.where(mask, dw_acc, 0.0)
            db_ref[g] += jnp.broadcast_to(jnp.sum(db_acc, axis=-1, keepdims=True), (GM_CHUNK, GM_CHUNK))
        dvn = dvn_ref[...]
        dgain_ref[...] += jnp.sum(dvn * vhat, axis=0, keepdims=True)
        dvhat = dvn * gain
        m = jnp.mean(dvhat * vhat, axis=-1, keepdims=True)
        dv = r * (dvhat - vhat * m)
        dp_ref[:, GM_W:] = _bf(dv * _gelu_grad(pv))

    sq = pl.BlockSpec((GM_GROUPS, GM_CHUNK, GM_CHUNK), lambda i: (0, 0, 0))
    sq_sh = jax.ShapeDtypeStruct((GM_GROUPS, GM_CHUNK, GM_CHUNK), F32)
    return pl.pallas_call(
        body, grid=(S // tw,),
        in_specs=[pl.BlockSpec((tw, 2 * GM_W), lambda i: (i, 0)), pl.BlockSpec((tw, GM_W), lambda i: (i, 0)),
                  pl.BlockSpec((1, GM_W), lambda i: (0, 0)), sq, sq],
        out_specs=[pl.BlockSpec((tw, 2 * GM_W), lambda i: (i, 0)), sq, sq,
                   pl.BlockSpec((1, GM_W), lambda i: (0, 0))],
        out_shape=[jax.ShapeDtypeStruct((S, 2 * GM_W), BF16), sq_sh, sq_sh,
                   jax.ShapeDtypeStruct((1, GM_W), F32)],
        scratch_shapes=[pltpu.VMEM((tw, GM_W), F32)],
        compiler_params=_cparams(("arbitrary",)), name=name)(proj, dy, v_gain.reshape(1, GM_W), w_s, b_mat)


def _head_mask(h, width):
    lane = lax.broadcasted_iota(jnp.int32, (1, width), 1)
    return (lane >= HEAD_DIM * h) & (lane < HEAD_DIM * (h + 1))


def _mem_probs(q, k, h):
    kh = jnp.where(_head_mask(h, MEM_W), k, jnp.zeros_like(k))
    s = _dot_nt(q, kh) * (HEAD_DIM ** -0.5)
    s = s - jnp.max(s, axis=-1, keepdims=True)
    e = jnp.exp(s)
    return e * (1.0 / jnp.sum(e, axis=-1, keepdims=True)), kh


def memattn_fwd(proj, q_cb, mem_kv, name):
    S = proj.shape[0]
    NM = mem_kv.shape[0]
    tm = _tile(S, ROW_TILE, 8)

    def body(q_ref, kv_ref, o_ref):
        q = _bf(q_ref[...])
        k = _bf(kv_ref[:, :MEM_W])
        v = _bf(kv_ref[:, MEM_W:])
        acc = jnp.zeros((tm, MEM_W), F32)
        for h in range(MEM_HEADS):
            p, _ = _mem_probs(q, k, h)
            vh = jnp.where(_head_mask(h, MEM_W), v, jnp.zeros_like(v))
            acc = acc + _dot(_bf(p), vh)
        o_ref[...] = _bf(acc)

    return pl.pallas_call(
        body, grid=(S // tm,),
        in_specs=[pl.BlockSpec((tm, MEM_W), lambda i: (i, q_cb)), pl.BlockSpec((NM, 2 * MEM_W), lambda i: (0, 0))],
        out_specs=pl.BlockSpec((tm, MEM_W), lambda i: (i, 0)),
        out_shape=jax.ShapeDtypeStruct((S, MEM_W), BF16),
        compiler_params=_cparams(("parallel",)), name=name)(proj, mem_kv)


def memattn_bwd(proj, q_cb, mem_kv, dy, dy_cb, name):
    S = proj.shape[0]
    NM = mem_kv.shape[0]
    tm = _tile(S, ROW_TILE, 8)

    def body(q_ref, kv_ref, do_ref, dq_ref, dkv_ref):
        @pl.when(pl.program_id(0) == 0)
        def _():
            dkv_ref[...] = jnp.zeros_like(dkv_ref)

        q = _bf(q_ref[...])
        k = _bf(kv_ref[:, :MEM_W])
        v = _bf(kv_ref[:, MEM_W:])
        do = _bf(do_ref[...])
        dq = jnp.zeros((tm, MEM_W), F32)
        dk = jnp.zeros((NM, MEM_W), F32)
        dv = jnp.zeros((NM, MEM_W), F32)
        for h in range(MEM_HEADS):
            hm = _head_mask(h, MEM_W)
            p, kh = _mem_probs(q, k, h)
            vh = jnp.where(hm, v, jnp.zeros_like(v))
            dp = _dot_nt(do, vh)
            ds = _bf(p * (dp - jnp.sum(p * dp, axis=-1, keepdims=True)) * (HEAD_DIM ** -0.5))
            dq = dq + _dot(ds, kh)
            dk = dk + jnp.where(hm, _dot_tn(ds, q), 0.0)
            dv = dv + jnp.where(hm, _dot_tn(_bf(p), do), 0.0)
        dq_ref[...] = _bf(dq)
        dkv_ref[:, :MEM_W] += dk
        dkv_ref[:, MEM_W:] += dv

    return pl.pallas_call(
        body, grid=(S // tm,),
        in_specs=[pl.BlockSpec((tm, MEM_W), lambda i: (i, q_cb)), pl.BlockSpec((NM, 2 * MEM_W), lambda i: (0, 0)),
                  pl.BlockSpec((tm, MEM_W), lambda i: (i, dy_cb))],
        out_specs=[pl.BlockSpec((tm, MEM_W), lambda i: (i, 0)), pl.BlockSpec((NM, 2 * MEM_W), lambda i: (0, 0))],
        out_shape=[jax.ShapeDtypeStruct((S, MEM_W), BF16), jax.ShapeDtypeStruct((NM, 2 * MEM_W), F32)],
        compiler_params=_cparams(("arbitrary",)), name=name)(proj, mem_kv, dy)


def _split_bf(v):
    hi = _bf(v)
    return hi, _bf(v - hi.astype(F32))


def _sb_scores(qp, k2, dpos, row, col, tri):
    z = _dot_nt(qp, k2) * (HEAD_DIM ** -0.5)
    t = jnp.log(1.0 + jnp.exp(-jnp.abs(z)))
    lb = jnp.minimum(z, 0.0) - t
    lom = jnp.where(_sb_mask(dpos, row, col), -jnp.maximum(z, 0.0) - t, 0.0)
    l_hi, l_lo = _split_bf(lom)
    insuf = _dot(l_hi, tri) + _dot(l_lo, tri)
    return lb + insuf, jnp.sum(lom, axis=1, keepdims=True), lb


def _sb_mask(dpos, row, col):
    return (col - row) < dpos


def _sb_weights(s, run, dpos, row, col):
    return jnp.where(_sb_mask(dpos, row, col), jnp.exp(s + run), 0.0)


def _sb_alive(runs):
    top = runs[0]
    for r in runs[1:]:
        top = jnp.maximum(top, r)
    return jnp.max(top) > SB_DEAD


def _sb_consts():
    row = lax.broadcasted_iota(jnp.int32, (2 * SB_BLOCK, SB_BLOCK), 0) % SB_BLOCK
    col = lax.broadcasted_iota(jnp.int32, (2 * SB_BLOCK, SB_BLOCK), 1)
    r = lax.broadcasted_iota(jnp.int32, (SB_BLOCK, SB_BLOCK), 0)
    c = lax.broadcasted_iota(jnp.int32, (SB_BLOCK, SB_BLOCK), 1)
    return row, col, _bf(jnp.where(r > c, 1.0, 0.0)), _bf(jnp.where(r >= c, 1.0, 0.0))


def _stack_heads(x2, masks):
    return jnp.concatenate([_bf(jnp.where(hm, x2, jnp.zeros_like(x2))) for hm in masks], axis=0)


def _side_by_side(x):
    return jnp.concatenate([x[:SB_BLOCK], x[SB_BLOCK:]], axis=1)


def sb_fwd(proj, kv, name):
    S = proj.shape[0]
    n_g = GM_W // SB_COLS
    tq = _tile(S, SB_Q_CHUNK, SB_BLOCK)
    nqb = tq // SB_BLOCK

    def body(q_ref, k_ref, v_ref, o_ref, s_ref):
        chunk = pl.program_id(1)
        row, col, tri, _ = _sb_consts()
        masks = [_head_mask(h, LANES) for h in range(2)]
        pairs = [slice(p * LANES, (p + 1) * LANES) for p in range(SB_PAIRS)]

        def q_loop(qi, _):
            i = chunk * nqb + qi
            rows = pl.ds(pl.multiple_of(qi * SB_BLOCK, SB_BLOCK), SB_BLOCK)
            qps = [_stack_heads(q_ref[rows, cs], masks) for cs in pairs]

            def look_ahead(t):
                keys = pl.ds(pl.multiple_of(jnp.maximum(i - t, 0) * SB_BLOCK, SB_BLOCK), SB_BLOCK)
                sums = []
                for p, cs in enumerate(pairs):
                    s, lom_sum, _ = _sb_scores(qps[p], k_ref[keys, cs], t * SB_BLOCK, row, col, tri)
                    s_ref[p] = s
                    sums.append(lom_sum)
                return tuple(sums)

            def k_step(carry):
                t, accs, runs, sums, _ = carry
                keys = pl.ds(pl.multiple_of((i - t) * SB_BLOCK, SB_BLOCK), SB_BLOCK)
                new_accs, new_runs = [], []
                for p, cs in enumerate(pairs):
                    a = _sb_weights(s_ref[p], runs[p], t * SB_BLOCK, row, col)
                    a_hi, a_lo = _split_bf(_side_by_side(a))
                    vp = _stack_heads(v_ref[keys, cs], masks)
                    new_accs.append(accs[p] + _dot(a_hi, vp) + _dot(a_lo, vp))
                    new_runs.append(runs[p] + sums[p])
                return t + 1, tuple(new_accs), tuple(new_runs), look_ahead(t + 1), _sb_alive(new_runs)

            zero = jnp.zeros((2 * SB_BLOCK, 1), F32)
            _, accs, _, _, _ = lax.while_loop(
                lambda carry: jnp.logical_and(carry[0] <= i, carry[4]), k_step,
                (jnp.int32(0), (jnp.zeros((SB_BLOCK, LANES), F32),) * SB_PAIRS, (zero,) * SB_PAIRS,
                 look_ahead(jnp.int32(0)), jnp.bool_(True)))
            for p, cs in enumerate(pairs):
                o_ref[rows, cs] = accs[p]
            return 0

        lax.fori_loop(0, nqb, q_loop, 0)

    return pl.pallas_call(
        body, grid=(n_g, S // tq),
        in_specs=[pl.BlockSpec((tq, SB_COLS), lambda g, c: (c, g)),
                  pl.BlockSpec((S, SB_COLS), lambda g, c: (0, g)),
                  pl.BlockSpec((S, SB_COLS), lambda g, c: (0, n_g + g))],
        out_specs=pl.BlockSpec((tq, SB_COLS), lambda g, c: (c, g)),
        out_shape=jax.ShapeDtypeStruct((S, GM_W), F32),
        scratch_shapes=[pltpu.VMEM((SB_PAIRS, 2 * SB_BLOCK, LANES), F32)],
        compiler_params=_cparams(("parallel", "parallel")), name=name)(proj, kv, kv)


def sb_bwd(proj, kv, dy, out, dk_init, dv_init, name):
    S = proj.shape[0]
    n_g = GM_W // SB_COLS
    tq = _tile(S, SB_Q_CHUNK, SB_BLOCK)
    nqb = tq // SB_BLOCK
    n_chunks = S // tq
    has_init = dk_init is not None
    scale = HEAD_DIM ** -0.5

    def body(*refs):
        s_ref, beta_ref, da_ref = refs[-3:]
        if has_init:
            q_ref, k_ref, v_ref, do_ref, out_ref, dki_ref, dvi_ref, dq_ref, dko_ref, dvo_ref, dk_acc, dv_acc = refs[:-3]
        else:
            q_ref, k_ref, v_ref, do_ref, out_ref, dq_ref, dko_ref, dvo_ref, dk_acc, dv_acc = refs[:-3]
        chunk = pl.program_id(1)
        cols = pl.ds(pl.multiple_of(pl.program_id(0) * SB_COLS, SB_COLS), SB_COLS)

        @pl.when(chunk == 0)
        def _():
            if has_init:
                pltpu.sync_copy(dki_ref.at[:, cols], dk_acc)
                pltpu.sync_copy(dvi_ref.at[:, cols], dv_acc)
            else:
                dk_acc[...] = jnp.zeros_like(dk_acc)
                dv_acc[...] = jnp.zeros_like(dv_acc)

        row, col, tri, tri_inc = _sb_consts()
        masks = [_head_mask(h, LANES) for h in range(2)]
        pairs = [slice(p * LANES, (p + 1) * LANES) for p in range(SB_PAIRS)]

        def q_loop(qi, _):
            i = chunk * nqb + qi
            rows = pl.ds(pl.multiple_of(qi * SB_BLOCK, SB_BLOCK), SB_BLOCK)
            qps = [_stack_heads(q_ref[rows, cs], masks) for cs in pairs]
            dops = [_stack_heads(do_ref[rows, cs], masks) for cs in pairs]
            e_tots = [jnp.sum(dop.astype(F32) * jnp.concatenate([out_ref[rows, cs]] * 2, axis=0), axis=1,
                              keepdims=True) for dop, cs in zip(dops, pairs)]

            def look_ahead(t):
                keys = pl.ds(pl.multiple_of(jnp.maximum(i - t, 0) * SB_BLOCK, SB_BLOCK), SB_BLOCK)
                sums = []
                for p, cs in enumerate(pairs):
                    s, lom_sum, lb = _sb_scores(qps[p], k_ref[keys, cs], t * SB_BLOCK, row, col, tri)
                    s_ref[p] = s
                    beta_ref[p] = jnp.exp(lb)
                    da_ref[p] = _dot_nt(dops[p], v_ref[keys, cs])
                    sums.append(lom_sum)
                return tuple(sums)

            def k_step(carry):
                t, dqs, runs, e_runs, sums, _ = carry
                keys = pl.ds(pl.multiple_of((i - t) * SB_BLOCK, SB_BLOCK), SB_BLOCK)
                new_dqs, new_runs, new_e_runs = [], [], []
                for p, cs in enumerate(pairs):
                    a = _sb_weights(s_ref[p], runs[p], t * SB_BLOCK, row, col)
                    e = a * da_ref[p]
                    e_hi, e_lo = _split_bf(e)
                    before = e_tots[p] - e_runs[p] - (_dot(e_hi, tri_inc) + _dot(e_lo, tri_inc))
                    beta = beta_ref[p]
                    dz = jnp.where(_sb_mask(t * SB_BLOCK, row, col), e * (1.0 - beta) - before * beta, 0.0)
                    dz = _bf(dz * scale)
                    new_dqs.append(dqs[p] + _dot(_side_by_side(dz), _stack_heads(k_ref[keys, cs], masks)))
                    dk_acc[keys, cs] += _dot_tn(dz, qps[p])
                    dv_acc[keys, cs] += _dot_tn(_bf(a), dops[p])
                    new_runs.append(runs[p] + sums[p])
                    new_e_runs.append(e_runs[p] + jnp.sum(e, axis=1, keepdims=True))
                return (t + 1, tuple(new_dqs), tuple(new_runs), tuple(new_e_runs), look_ahead(t + 1),
                        _sb_alive(new_runs))

            zero = jnp.zeros((2 * SB_BLOCK, 1), F32)
            zeros = (zero,) * SB_PAIRS
            _, dqs, _, _, _, _ = lax.while_loop(
                lambda carry: jnp.logical_and(carry[0] <= i, carry[5]), k_step,
                (jnp.int32(0), (jnp.zeros((SB_BLOCK, LANES), F32),) * SB_PAIRS, zeros, zeros,
                 look_ahead(jnp.int32(0)), jnp.bool_(True)))
            for p, cs in enumerate(pairs):
                dq_ref[rows, cs] = dqs[p]
            return 0

        lax.fori_loop(0, nqb, q_loop, 0)

        @pl.when(chunk == n_chunks - 1)
        def _():
            pltpu.sync_copy(dk_acc, dko_ref.at[:, cols])
            pltpu.sync_copy(dv_acc, dvo_ref.at[:, cols])

    qspec = pl.BlockSpec((tq, SB_COLS), lambda g, c: (c, g))
    kspec = pl.BlockSpec((S, SB_COLS), lambda g, c: (0, g))
    in_specs = [qspec, kspec, pl.BlockSpec((S, SB_COLS), lambda g, c: (0, n_g + g)), qspec, qspec]
    ins = [proj, kv, kv, dy, out]
    if has_init:
        in_specs += [_ANY, _ANY]
        ins += [dk_init, dv_init]
    sh = jax.ShapeDtypeStruct((S, GM_W), F32)
    return pl.pallas_call(
        body, grid=(n_g, n_chunks), in_specs=in_specs, out_specs=[qspec, _ANY, _ANY],
        out_shape=[sh, sh, sh],
        scratch_shapes=[pltpu.VMEM((S, SB_COLS), F32), pltpu.VMEM((S, SB_COLS), F32)]
        + [pltpu.VMEM((SB_PAIRS, 2 * SB_BLOCK, LANES), F32)] * 3,
        compiler_params=_cparams(("arbitrary", "arbitrary")), name=name)(*ins)


def adamw(w, g, m, v, name):
    shape = w.shape
    C = shape[-1] if w.ndim > 1 else shape[0]
    R = w.size // C
    tr = _tile(R, ROW_TILE, 8)

    def body(w_ref, g_ref, m_ref, v_ref, d_ref, nm_ref, nv_ref):
        gv = g_ref[...]
        m2 = ADAM_B1 * m_ref[...] + (1.0 - ADAM_B1) * gv
        v2 = ADAM_B2 * v_ref[...] + (1.0 - ADAM_B2) * (gv * gv)
        m_hat = m2 / (1.0 - ADAM_B1 ** ADAM_STEP)
        v_hat = v2 / (1.0 - ADAM_B2 ** ADAM_STEP)
        d_ref[...] = -ADAM_LR * (m_hat / (jnp.sqrt(v_hat) + ADAM_EPS) + ADAM_WD * w_ref[...])
        nm_ref[...] = m2
        nv_ref[...] = v2

    blk = pl.BlockSpec((tr, C), lambda i: (i, 0))
    sh = jax.ShapeDtypeStruct((R, C), F32)
    outs = pl.pallas_call(
        body, grid=(R // tr,), in_specs=[blk] * 4, out_specs=[blk] * 3, out_shape=[sh] * 3,
        compiler_params=_cparams(("parallel",)), name=name)(
            w.reshape(R, C), g.reshape(R, C), m.reshape(R, C), v.reshape(R, C))
    return tuple(o.reshape(shape) for o in outs)


def _place():
    return lax.axis_index("x"), lax.axis_index("y"), lax.axis_index("c")


def _other_chips(x, y):
    return [(1 - x, y), (x, 1 - y), (1 - x, 1 - y)]


_ANY = pl.BlockSpec(memory_space=pl.ANY)
LOCAL_CHUNKS = 8
SEG_LOCAL_CHUNKS = 4


def _remote(src, dst, send_sems, recv_sems, k, to):
    return pltpu.make_async_remote_copy(src_ref=src, dst_ref=dst, send_sem=send_sems.at[k], recv_sem=recv_sems.at[k],
                                        device_id=to, device_id_type=MESH)


def _gather_stage(segs, n_gath, stage):
    def make(ins, outs, send_sems, recv_sems, *local_sems):
        x, y, c = _place()
        sibling = (x, y, 1 - c)
        chips = _other_chips(x, y)
        copies, waits = [], []
        for s, (b, r0, n) in enumerate(segs):
            half = n // 2

            def piece(px, py, pc, b=b, r0=r0, half=half):
                return outs[b].at[2 * px + py, pl.ds(r0 + pc * half, half), :]

            for j, chip in enumerate(chips):
                if stage == 0:
                    mine = ins[n_gath + b].at[pl.ds(r0 + c * half, half), :]
                    cp = _remote(mine, piece(x, y, c), send_sems, recv_sems, 3 * s + j, (*chip, c))
                    landed, sender = piece(*chip, c), (*chip, c)
                else:
                    cp = _remote(piece(*chip, c), piece(*chip, c), send_sems, recv_sems, 3 * s + j, sibling)
                    landed, sender = piece(*chip, 1 - c), sibling
                copies.append(cp)
                waits.append(cp.wait_send)
                waits.append(_remote(landed, landed, send_sems, recv_sems, 3 * s + j, sender).wait_recv)
            if stage == 0:
                rows = n // SEG_LOCAL_CHUNKS
                for q in range(SEG_LOCAL_CHUNKS):
                    r = pl.ds(r0 + q * rows, rows)
                    cp = pltpu.make_async_copy(ins[n_gath + b].at[r, :], outs[b].at[2 * x + y, r, :],
                                               local_sems[0].at[SEG_LOCAL_CHUNKS * s + q])
                    copies.append(cp)
                    waits.append(cp.wait)
        return copies, waits

    return make


def _gather_sems(segs, stage):
    return [3 * len(segs), 3 * len(segs)] + ([SEG_LOCAL_CHUNKS * len(segs)] if stage == 0 else [])


def all_gather_segments(bufs, segs, name):
    nb = len(bufs)
    n0, n1 = len(_gather_sems(segs, 0)), len(_gather_sems(segs, 1))

    def body(*refs):
        in_refs, out_refs = refs[:nb], refs[nb:2 * nb]
        sems = refs[2 * nb:]
        for stage, stage_sems in ((0, sems[:n0]), (1, sems[n0:n0 + n1])):
            copies, waits = _gather_stage(segs, nb, stage)(list(out_refs) + list(in_refs), out_refs, *stage_sems)
            for cp in copies:
                cp.start()
            for wait in waits:
                wait()

    return pl.pallas_call(
        body, in_specs=[_ANY] * nb, out_specs=[_ANY] * nb,
        out_shape=[jax.ShapeDtypeStruct((N_CHIPS,) + b.shape, b.dtype) for b in bufs],
        scratch_shapes=[pltpu.SemaphoreType.DMA((n,)) for n in _gather_sems(segs, 0) + _gather_sems(segs, 1)],
        name=name)(*bufs)


def hosted_gather(gathered, bufs, segs, stage):
    arrays = list(gathered) + (list(bufs) if stage == 0 else [])
    return Hosted(arrays, len(gathered), _gather_sems(segs, stage), _gather_stage(segs, len(gathered), stage))


def _exchange(step, arrays, name):
    n_h, n_alias, n_fresh = len(step.arrays), step.n_out, len(step.fresh)

    def body(*refs):
        h_in = refs[:n_h]
        h_out = refs[n_h:n_h + n_alias + n_fresh]
        copies, waits = step.make(h_in, h_out, *refs[n_h + n_alias + n_fresh:])
        for cp in copies:
            cp.start()
        for wait in waits:
            wait()

    return pl.pallas_call(
        body, in_specs=[_ANY] * n_h, out_specs=[_ANY] * (n_alias + n_fresh),
        out_shape=[jax.ShapeDtypeStruct(a.shape, a.dtype) for a in step.arrays[:n_alias]] + step.fresh,
        scratch_shapes=[pltpu.SemaphoreType.DMA((n,)) for n in step.sems],
        input_output_aliases={k: k for k in range(n_alias)}, name=name)(*arrays)


def swap_step(gs):
    nb = len(gs)

    def make(ins, outs, send_sems, recv_sems):
        x, y, c = _place()
        copies = []
        for b in range(nb):
            half = gs[b].shape[1] // 2
            for k in range(N_CHIPS):
                copies.append(_remote(ins[b].at[k, pl.ds((1 - c) * half, half), :], outs[b].at[k], send_sems,
                                      recv_sems, N_CHIPS * b + k, (x, y, 1 - c)))
        return copies, [cp.wait for cp in copies]

    fresh = [jax.ShapeDtypeStruct((N_CHIPS, g.shape[1] // 2, g.shape[2]), g.dtype) for g in gs]
    return Hosted(list(gs), 0, [N_CHIPS * nb, N_CHIPS * nb], make, fresh)


def _core_index():
    return lax.axis_index("c").astype(jnp.int32).reshape(1)


def add_cores(g, theirs, name):
    n, Ph, Wd = theirs.shape
    tr = _tile(Ph, ROW_TILE, 16)
    steps = Ph // tr

    def body(c_ref, g_ref, t_ref, o_ref):
        o_ref[...] = _bf(g_ref[...] + t_ref[...])

    blk = pl.BlockSpec((None, tr, Wd), lambda k, i, c_ref: (k, i, 0))
    return pl.pallas_call(
        body,
        grid_spec=pltpu.PrefetchScalarGridSpec(
            num_scalar_prefetch=1, grid=(n, steps),
            in_specs=[pl.BlockSpec((None, tr, Wd), lambda k, i, c_ref: (k, c_ref[0] * steps + i, 0)), blk],
            out_specs=blk),
        out_shape=jax.ShapeDtypeStruct((n, Ph, Wd), BF16),
        compiler_params=_cparams(("parallel", "parallel")), name=name)(_core_index(), g, theirs)


def add_chips(slots, name):
    n, Ph, Wd = slots.shape
    tr = _tile(Ph, ROW_TILE, 16)
    steps = Ph // tr

    def body(c_ref, *refs):
        acc = refs[0][...].astype(F32)
        for k in range(1, n):
            acc = acc + refs[k][...].astype(F32)
        refs[n][...] = acc

    in_specs = [pl.BlockSpec((None, tr, Wd), functools.partial(lambda i, c_ref, k: (k, i, 0), k=k)) for k in range(n)]
    return pl.pallas_call(
        body,
        grid_spec=pltpu.PrefetchScalarGridSpec(
            num_scalar_prefetch=1, grid=(steps,), in_specs=in_specs,
            out_specs=pl.BlockSpec((tr, Wd), lambda i, c_ref: (c_ref[0] * steps + i, 0))),
        out_shape=jax.ShapeDtypeStruct((2 * Ph, Wd), F32),
        compiler_params=_cparams(("parallel",)), name=name)(_core_index(), *([slots] * n))


def _whole_tile_chunks(rows, most, tile_rows):
    return max(n for n in range(1, most + 1) if rows % (n * tile_rows) == 0)


def scatter_step(hs):
    nb = len(hs)
    chunks = [_whole_tile_chunks(h.shape[1], LOCAL_CHUNKS, 16) for h in hs]

    def make(ins, outs, send_sems, recv_sems, local_sems):
        x, y, c = _place()
        me = 2 * x + y
        copies, waits = [], []
        for b in range(nb):
            for j, (cx, cy) in enumerate(_other_chips(x, y)):
                cp = _remote(ins[b].at[2 * cx + cy], outs[b].at[me], send_sems, recv_sems, 3 * b + j, (cx, cy, c))
                copies.append(cp)
                waits.append(cp.wait_send)
                landed = outs[b].at[2 * cx + cy]
                waits.append(_remote(landed, landed, send_sems, recv_sems, 3 * b + j, (cx, cy, c)).wait_recv)
            rows = hs[b].shape[1] // chunks[b]
            for q in range(chunks[b]):
                r = pl.ds(q * rows, rows)
                cp = pltpu.make_async_copy(ins[b].at[me, r, :], outs[b].at[me, r, :],
                                           local_sems.at[sum(chunks[:b]) + q])
                copies.append(cp)
                waits.append(cp.wait)
        return copies, waits

    fresh = [jax.ShapeDtypeStruct(h.shape, h.dtype) for h in hs]
    return Hosted(list(hs), 0, [3 * nb, 3 * nb, sum(chunks)], make, fresh)


JOIN_CHUNKS = 4


def join_step(ts):
    nb = len(ts)
    n = JOIN_CHUNKS

    def make(ins, outs, send_sems, recv_sems):
        x, y, c = _place()
        copies, waits = [], []
        for b in range(nb):
            half = ts[b].shape[0] // 2
            rows = half // n
            for k in range(n):
                part = outs[b].at[pl.ds(c * half + k * rows, rows), :]
                cp = _remote(part, part, send_sems, recv_sems, n * b + k, (x, y, 1 - c))
                copies.append(cp)
                waits.append(cp.wait_send)
                landed = outs[b].at[pl.ds((1 - c) * half + k * rows, rows), :]
                waits.append(_remote(landed, landed, send_sems, recv_sems, n * b + k, (x, y, 1 - c)).wait_recv)
        return copies, waits

    return Hosted(list(ts), nb, [n * nb, n * nb], make)


WIDE = ['ffn1_w_gate', 'ffn1_w_up', 'ffn2_w_gate', 'ffn2_w_up']


def _rows_of(p):
    if p.shape[-1] == PACK_W and (p.size // PACK_W) % PART_ROWS == 0:
        return p.reshape(-1, PACK_W)
    flat = p.reshape(-1)
    rows = -(-flat.shape[0] // (PACK_W * PART_ROWS)) * PART_ROWS
    return jnp.pad(flat, (0, rows * PACK_W - flat.shape[0])).reshape(rows, PACK_W)


def _pack(parts):
    buf = jnp.concatenate([_rows_of(p) for p in parts], axis=0)
    rows = buf.shape[0]
    total = -(-rows // PACK_ROWS) * PACK_ROWS
    return jnp.pad(buf, ((0, total - rows), (0, 0)))


def _unpack(buf, shapes):
    outs, r = [], 0
    lead = buf.shape[:-2]
    for shp in shapes:
        n = math.prod(shp)
        rows = -(-n // (PACK_W * PART_ROWS)) * PART_ROWS
        blk = buf[..., r:r + rows, :]
        if n != rows * PACK_W:
            blk = blk.reshape(lead + (rows * PACK_W,))[..., :n]
        outs.append(blk.reshape(lead + tuple(shp)))
        r += rows
    return outs


class WeightStream:
    def __init__(self, shards):
        depth = shards['ffn1_w_gate'].shape[0]
        order = [(which, l) for l in range(depth) for which in ('ffn1', 'ffn2')]
        self.count = len(order)
        gate_up = [shards[f'{which}_w_{part}'][l].astype(BF16) for which, l in order for part in ('gate', 'up')]
        downs = [shards[f'{which}_w_down'][l].astype(BF16) for which, l in order]
        self.d, self.cols = gate_up[0].shape
        assert downs[0].shape == (self.cols, PACK_W)
        self.rest_names = [n for n in W_NAMES if n in SHARD_AXIS and n not in WIDE and not n.endswith('_w_down')]
        rest = [lax.bitcast_convert_type(shards[n], BF16) if n == 'a_v_norm' else shards[n].astype(BF16)
                for n in self.rest_names]
        self.rest_shapes = [p.shape for p in rest]
        self.bufs = [jnp.concatenate(gate_up, axis=0), _pack(downs + rest)]
        self.rest_row0 = self.count * self.cols
        first = self.segments(0) + [(1, self.rest_row0, self.bufs[1].shape[0] - self.rest_row0)]
        self.gathered = all_gather_segments(self.bufs, first, "all_gather_first")

    def segments(self, i):
        return [(0, 2 * self.d * i, 2 * self.d), (1, self.cols * i, self.cols)]

    def rest(self):
        out = {}
        for n, g in zip(self.rest_names, _unpack(self.gathered[1][:, self.rest_row0:], self.rest_shapes)):
            if n == 'a_v_norm':
                g = lax.bitcast_convert_type(g, F32)
            out[n] = jnp.concatenate([g[k] for k in range(N_CHIPS)], axis=SHARD_AXIS[n])
        return out

    def ffn(self, i):
        wide, narrow = self.gathered
        r = 2 * self.d * i
        gate = jnp.concatenate([wide[k, r:r + self.d] for k in range(N_CHIPS)], axis=1)
        up = jnp.concatenate([wide[k, r + self.d:r + 2 * self.d] for k in range(N_CHIPS)], axis=1)
        down = narrow[:, self.cols * i:self.cols * (i + 1)].reshape(N_CHIPS * self.cols, PACK_W)
        return (gate, None), (up, None), (down, None)

    def hosted(self, i, stage):
        if i >= self.count:
            return None
        return hosted_gather(self.gathered, self.bufs, self.segments(i), stage)

    def absorb(self, gathered):
        self.gathered = list(gathered)


class GradSink:
    def __init__(self):
        self.queue = []
        self.totals = {}
        self.held = {}

    def add(self, i, d_wg, d_wu, d_wd, ride):
        cols = d_wg.shape[1] // N_CHIPS
        wide = jnp.stack([jnp.concatenate([d_wg[:, k * cols:(k + 1) * cols], d_wu[:, k * cols:(k + 1) * cols]],
                                          axis=0) for k in range(N_CHIPS)], axis=0)
        bufs = [wide, d_wd.reshape(N_CHIPS, cols, d_wd.shape[1])]
        if ride:
            self.queue.append([0, bufs, i])
        else:
            self.held[i] = bufs

    def step(self):
        if not self.queue:
            return None
        stage, bufs, _ = self.queue[0]
        return (swap_step, scatter_step, join_step)[stage](bufs)

    def absorb(self, results):
        entry = self.queue[0]
        stage, bufs, i = entry
        if stage == 0:
            entry[1] = [add_cores(g, t, f"ffn{i}_add_cores_{b}") for b, (g, t) in enumerate(zip(bufs, results))]
        elif stage == 1:
            entry[1] = [add_chips(s, f"ffn{i}_add_chips_{b}") for b, s in enumerate(results)]
        else:
            self.totals[i] = list(results)
            self.queue.pop(0)
        entry[0] = stage + 1

    def reduce(self, grads, shard_shapes):
        assert not self.queue
        narrow = [n for n in W_NAMES if n in SHARD_AXIS and n in grads]
        repl = [n for n in W_NAMES if n not in SHARD_AXIS]
        full = {n: (jnp.stack(g, axis=0) if isinstance(g, list) else g) for n, g in grads.items()}
        blocks = []
        for k in range(N_CHIPS):
            parts = [jnp.split(full[n], N_CHIPS, axis=SHARD_AXIS[n])[k] for n in narrow]
            parts += [full[n] for n in repl]
            blocks.append(_pack(parts))
        held = sorted(self.held)
        gs = [b for i in held for b in self.held[i]] + [jnp.stack(blocks, axis=0)]
        theirs = _exchange(swap_step(gs), gs, "grads_swap_halves")
        sums = [add_cores(g, t, f"grads_add_cores_{b}") for b, (g, t) in enumerate(zip(gs, theirs))]
        slots = _exchange(scatter_step(sums), sums, "grads_scatter")
        halves = [add_chips(s, f"grads_add_chips_{b}") for b, s in enumerate(slots)]
        totals = _exchange(join_step(halves), halves, "grads_join_halves")
        for n, i in enumerate(held):
            self.totals[i] = totals[2 * n:2 * n + 2]
        shapes = [shard_shapes[n] for n in narrow] + [full[n].shape for n in repl]
        out = dict(zip(narrow + repl, _unpack(totals[-1], shapes)))
        d = self.totals[0][0].shape[0] // 2
        for w, which in enumerate(('ffn1', 'ffn2')):
            mine = [self.totals[i] for i in sorted(self.totals) if i % 2 == w]
            out[f'{which}_w_gate'] = jnp.stack([t[0][:d] for t in mine], axis=0)
            out[f'{which}_w_up'] = jnp.stack([t[0][d:] for t in mine], axis=0)
            out[f'{which}_w_down'] = jnp.stack([t[1] for t in mine], axis=0)
        return out


def _ffn_fwd(x, h, stream, i, next_gain, tag):
    wg, wu, wd = stream.ffn(i)
    ici = stream.hosted(i + 1, 0)
    if ici is None:
        g, u, a = ffn_up(h, wg, wu, tag + "_up")
        out = mm([(a, 0, wd, a.shape[1], 0)], res=x, scale=0.5, norm_next=next_gain, name=tag + "_down")
    else:
        (g, u, a), gathered = ffn_up(h, wg, wu, tag + "_up", hosted=ici)
        stream.absorb(gathered)
        out, gathered = mm([(a, 0, wd, a.shape[1], 0)], res=x, scale=0.5, norm_next=next_gain,
                           name=tag + "_down", hosted=stream.hosted(i + 1, 1))
        stream.absorb(gathered)
    xo, h_next = (out, None) if next_gain is None else out
    return xo, h_next, (x, h, g, u, a, wg, wu, wd)


def _carrying(sink, kernel, *args, **kwargs):
    step = sink.step()
    if step is None:
        return kernel(*args, **kwargs)
    out, results = kernel(*args, hosted=step, **kwargs)
    sink.absorb(results)
    return out


def _ffn_bwd(dxo, saved, gain, sink, i, tag):
    x, h, g, u, a, wg, wu, wd = saved
    Fd = a.shape[1]
    dxo, dxo_b = dxo
    dgp, du, d_wd = _carrying(sink, ffn_down_bwd, dxo_b, wd, g, u, a, tag + "_down_bwd")
    d_wg, d_wu = _carrying(sink, mm_tn_pair, h, dgp, du, name=tag + "_dwgu")
    sink.add(i, d_wg, d_wu, d_wd, ride=i > 0)
    return _carrying(sink, mm, [(dgp, 0, wg, Fd, 0), (du, 0, wu, Fd, 0)], tb=True, res=dxo, norm_bwd=(x, gain),
                     name=tag + "_up_bwd")


def local_step(x, mem, W, stream, sink, target):
    D = x.shape[1]
    depth = W['ffn1_norm'].shape[0]
    n_a = W['a_w_in'].shape[0]
    G = {}
    mem_h = rms_fwd(mem, W['mem_norm'], "mem_norm")
    b_mats = jnp.broadcast_to(W['a_b_spatial'][..., None], W['a_b_spatial'].shape + (GM_CHUNK,))
    w_kv = (W['w_kv'], None)

    def mixer_w(l):
        if l < n_a:
            return (W['a_w_in'], l), (W['a_w_out'], l), 2 * GM_W
        return (W['b_w_in'], l - n_a), (W['b_w_out'], l - n_a), GM_W

    saved = []
    kv = kvn = x_kv = None
    h = rms_fwd(x, W['ffn1_norm'][0], "l0_ffn1_norm")
    for l in range(depth):
        if l == n_a:
            x_kv = x
            kvn = rms_fwd(x, W['kv_norm'], "kv_norm")
            kv = mm([(kvn, 0, w_kv, D, 0)], out_dtype=BF16, name="kv_proj")
        x, hm, s1 = _ffn_fwd(x, h, stream, 2 * l, W['mix_norm'][l], f"l{l}_ffn1")
        mem_kv = mm([(mem_h, 0, (W['w_mem_kv'], l), D, 0)], name=f"l{l}_mem_kv")
        w_in, w_out, tok_w = mixer_w(l)
        proj = mm([(hm, 0, w_in, D, 0)], name=f"l{l}_mix_in")
        if l < n_a:
            y_tok = gmlp_fwd(proj, W['a_v_norm'][l], W['a_w_spatial'][l], b_mats[l], f"l{l}_gmlp")
        else:
            y_tok = sb_fwd(proj, kv, f"l{l}_sb")
        q_cb = tok_w // MEM_W
        y_mem = memattn_fwd(proj, q_cb, mem_kv, f"l{l}_memattn")
        x_mid = x
        x, h2 = mm([(y_tok, 0, w_out, GM_W, 0), (y_mem, 0, w_out, MEM_W, GM_W // MEM_W)], res=x,
                   norm_next=W['ffn2_norm'][l], name=f"l{l}_mix_out")
        sm = (x_mid, hm, mem_kv, proj, y_tok, y_mem, q_cb)
        next_gain = W['ffn1_norm'][l + 1] if l + 1 < depth else None
        x, h, s2 = _ffn_fwd(x, h2, stream, 2 * l + 1, next_gain, f"l{l}_ffn2")
        saved.append((s1, sm, s2))

    loss, dx, d_final = final_loss(x, W['final_norm'], target, "loss_head")
    G['final_norm'] = d_final.reshape(-1)

    per_layer = {n: [None] * depth for n in ['ffn1_norm', 'mix_norm', 'ffn2_norm', 'w_mem_kv']}
    per_a = {n: [None] * n_a for n in ['a_w_in', 'a_v_norm', 'a_w_spatial', 'a_b_spatial', 'a_w_out']}
    per_b = {n: [None] * (depth - n_a) for n in ['b_w_in', 'b_w_out']}
    d_mem_h = None
    dk = dv = None
    for l in reversed(range(depth)):
        s1, sm, s2 = saved[l]
        dx, dg = _ffn_bwd(dx, s2, W['ffn2_norm'][l], sink, 2 * l + 1, f"l{l}_ffn2")
        per_layer['ffn2_norm'][l] = dg.reshape(-1)

        x_mid, hm, mem_kv, proj, y_tok, y_mem, q_cb = sm
        is_a = l < n_a
        w_in, w_out, tok_w = mixer_w(l)
        dy = mm([(dx[1], 0, w_out, D, 0)], tb=True, name=f"l{l}_mix_out_bwd")
        d_w_out = jnp.concatenate([mm_tn(y_tok, dx[1], name=f"l{l}_dwout_tok"),
                                   mm_tn(y_mem, dx[1], name=f"l{l}_dwout_mem")], axis=0)
        dq_mem, d_mem_kv = memattn_bwd(proj, q_cb, mem_kv, dy, GM_W // MEM_W, f"l{l}_memattn_bwd")
        if is_a:
            d_tok, d_ws, d_bs, d_vg = gmlp_bwd(proj, dy, W['a_v_norm'][l], W['a_w_spatial'][l], b_mats[l],
                                               f"l{l}_gmlp_bwd")
            per_a['a_w_spatial'][l], per_a['a_b_spatial'][l] = d_ws, d_bs[:, :, 0]
            per_a['a_v_norm'][l], per_a['a_w_out'][l] = d_vg.reshape(-1), d_w_out
        else:
            d_tok, dk, dv = sb_bwd(proj, kv, dy, y_tok, dk, dv, f"l{l}_sb_bwd")
            per_b['b_w_out'][l - n_a] = d_w_out
        d_w_in = jnp.concatenate([mm_tn(hm, d_tok, name=f"l{l}_dwin_tok"),
                                  mm_tn(hm, dq_mem, name=f"l{l}_dwin_mem")], axis=1)
        (per_a['a_w_in'] if is_a else per_b['b_w_in'])[l if is_a else l - n_a] = d_w_in
        dx, dg = mm([(d_tok, 0, w_in, tok_w, 0), (dq_mem, 0, w_in, MEM_W, tok_w // MEM_W)], tb=True, res=dx[0],
                    norm_bwd=(x_mid, W['mix_norm'][l]), name=f"l{l}_mix_in_bwd")
        per_layer['mix_norm'][l] = dg.reshape(-1)
        per_layer['w_mem_kv'][l] = mm_tn(mem_h, d_mem_kv, name=f"l{l}_dw_mem_kv")
        d_mem_h = mm([(d_mem_kv, 0, (W['w_mem_kv'], l), 2 * MEM_W, 0)], tb=True, res=d_mem_h,
                     name=f"l{l}_mem_kv_bwd")

        dx, dg = _ffn_bwd(dx, s1, W['ffn1_norm'][l], sink, 2 * l, f"l{l}_ffn1")
        per_layer['ffn1_norm'][l] = dg.reshape(-1)
        if l == n_a:
            G['w_kv'] = jnp.concatenate([mm_tn(kvn, dk, name="dw_k"), mm_tn(kvn, dv, name="dw_v")], axis=1)
            dx, dg = mm([(dk, 0, w_kv, GM_W, 0), (dv, 0, w_kv, GM_W, 1)], tb=True, res=dx[0],
                        norm_bwd=(x_kv, W['kv_norm']), name="kv_proj_bwd")
            G['kv_norm'] = dg.reshape(-1)

    G['mem_norm'] = rms_gain_grad(mem, d_mem_h, "mem_norm_bwd").reshape(-1)
    for d in (per_layer, per_a, per_b):
        G.update(d)
    return loss, dx[0], G


def kernel(x, mem, ffn1_norm, ffn1_w_gate, ffn1_w_up, ffn1_w_down, mix_norm, ffn2_norm, ffn2_w_gate, ffn2_w_up, ffn2_w_down, mem_norm, w_mem_kv, a_w_in, a_v_norm, a_w_spatial, a_b_spatial, a_w_out, kv_norm, w_kv, b_w_in, b_w_out, final_norm, loss_target, m_ffn1_norm, m_ffn1_w_gate, m_ffn1_w_up, m_ffn1_w_down, m_mix_norm, m_ffn2_norm, m_ffn2_w_gate, m_ffn2_w_up, m_ffn2_w_down, m_mem_norm, m_w_mem_kv, m_a_w_in, m_a_v_norm, m_a_w_spatial, m_a_b_spatial, m_a_w_out, m_kv_norm, m_w_kv, m_b_w_in, m_b_w_out, m_final_norm, v_ffn1_norm, v_ffn1_w_gate, v_ffn1_w_up, v_ffn1_w_down, v_mix_norm, v_ffn2_norm, v_ffn2_w_gate, v_ffn2_w_up, v_ffn2_w_down, v_mem_norm, v_w_mem_kv, v_a_w_in, v_a_v_norm, v_a_w_spatial, v_a_b_spatial, v_a_w_out, v_kv_norm, v_w_kv, v_b_w_in, v_b_w_out, v_final_norm):
    weights = dict(zip(W_NAMES, [ffn1_norm, ffn1_w_gate, ffn1_w_up, ffn1_w_down, mix_norm, ffn2_norm, ffn2_w_gate,
                                 ffn2_w_up, ffn2_w_down, mem_norm, w_mem_kv, a_w_in, a_v_norm, a_w_spatial,
                                 a_b_spatial, a_w_out, kv_norm, w_kv, b_w_in, b_w_out, final_norm]))
    m_in = dict(zip(W_NAMES, [m_ffn1_norm, m_ffn1_w_gate, m_ffn1_w_up, m_ffn1_w_down, m_mix_norm, m_ffn2_norm,
                              m_ffn2_w_gate, m_ffn2_w_up, m_ffn2_w_down, m_mem_norm, m_w_mem_kv, m_a_w_in,
                              m_a_v_norm, m_a_w_spatial, m_a_b_spatial, m_a_w_out, m_kv_norm, m_w_kv, m_b_w_in,
                              m_b_w_out, m_final_norm]))
    v_in = dict(zip(W_NAMES, [v_ffn1_norm, v_ffn1_w_gate, v_ffn1_w_up, v_ffn1_w_down, v_mix_norm, v_ffn2_norm,
                              v_ffn2_w_gate, v_ffn2_w_up, v_ffn2_w_down, v_mem_norm, v_w_mem_kv, v_a_w_in,
                              v_a_v_norm, v_a_w_spatial, v_a_b_spatial, v_a_w_out, v_kv_norm, v_w_kv, v_b_w_in,
                              v_b_w_out, v_final_norm]))

    stream = WeightStream({n: weights[n] for n in SHARD_AXIS})
    W = {n: weights[n] for n in W_NAMES if n not in SHARD_AXIS}
    W.update(stream.rest())
    sink = GradSink()
    loss, dx, grads = local_step(x[0], mem[0], W, stream, sink, loss_target[0])
    total = sink.reduce(grads, {n: weights[n].shape for n in SHARD_AXIS})
    loss = lax.psum(loss[0, 0], ("x", "y", "c"))

    deltas, new_m, new_v = {}, {}, {}
    for n in W_NAMES:
        deltas[n], new_m[n], new_v[n] = adamw(weights[n], total[n], m_in[n], v_in[n], "adamw_" + n)
    return (loss, dx[None], *[total[n] for n in W_NAMES], *[deltas[n] for n in W_NAMES],
            *[new_m[n] for n in W_NAMES], *[new_v[n] for n in W_NAMES])
```

```python
import functools
import math

import jax
import jax.numpy as jnp
from jax import lax
from jax.experimental import pallas as pl
from jax.experimental.pallas import tpu as pltpu

F32 = jnp.float32
BF16 = jnp.bfloat16
EPS = 1e-6
LANES = 128
ROW_TILE = 512
COL_TILE = 1408
PACK_ROWS = 256
PART_ROWS = 16
VMEM_LIMIT = 56 * 1024 * 1024

W_NAMES = ['ffn1_norm', 'ffn1_w_gate', 'ffn1_w_up', 'ffn1_w_down', 'mix_norm', 'ffn2_norm', 'ffn2_w_gate',
           'ffn2_w_up', 'ffn2_w_down', 'mem_norm', 'w_mem_kv', 'a_w_in', 'a_v_norm', 'a_w_spatial',
           'a_b_spatial', 'a_w_out', 'kv_norm', 'w_kv', 'b_w_in', 'b_w_out', 'final_norm']
SHARD_AXIS = {'ffn1_w_gate': 2, 'ffn1_w_up': 2, 'ffn1_w_down': 1, 'ffn2_w_gate': 2, 'ffn2_w_up': 2,
              'ffn2_w_down': 1, 'w_mem_kv': 1, 'a_w_in': 2, 'a_v_norm': 1, 'a_w_out': 1, 'w_kv': 1,
              'b_w_in': 1, 'b_w_out': 1}
N_CHIPS = 4
PACK_W = 1024

MEM_HEADS = 4
MEM_W = 256
HEAD_DIM = 64
GM_W = 768
GM_GROUPS = 6
GM_CHUNK = 128
CHUNK = 64
SB_BLOCK = 128
SB_Q_CHUNK = 1024
SB_DEAD = -104.0
SB_PAIRS = 2
SB_COLS = SB_PAIRS * LANES

ADAM_LR, ADAM_B1, ADAM_B2, ADAM_EPS, ADAM_WD, ADAM_STEP = 0.001, 0.9, 0.999, 1e-08, 0.01, 10

MESH = pl.DeviceIdType.MESH


def _cparams(sem=None):
    return pltpu.CompilerParams(dimension_semantics=sem, vmem_limit_bytes=VMEM_LIMIT)


class Hosted:
    def __init__(self, arrays, n_out, sems, make, fresh=()):
        self.arrays, self.n_out, self.sems, self.make, self.fresh = arrays, n_out, sems, make, list(fresh)


def _call(body, *, grid, in_specs, out_specs, out_shape, ins, semantics, name, hosted=None):
    if hosted is None:
        return pl.pallas_call(body, grid=grid, in_specs=in_specs, out_specs=out_specs, out_shape=out_shape,
                              compiler_params=_cparams(semantics), name=name)(*ins)
    n_in, n_out = len(ins), len(out_shape)
    n_h, n_alias = len(hosted.arrays), hosted.n_out
    n_ho = n_alias + len(hosted.fresh)

    def hosting_body(*refs):
        base_in, h_in = refs[:n_in], refs[n_in:n_in + n_h]
        base_out = refs[n_in + n_h:n_in + n_h + n_out]
        h_out = refs[n_in + n_h + n_out:n_in + n_h + n_out + n_ho]
        sems = refs[n_in + n_h + n_out + n_ho:]
        copies, waits = hosted.make(h_in, h_out, *sems)
        first = last = None
        for axis, extent in enumerate(grid):
            at_start, at_end = pl.program_id(axis) == 0, pl.program_id(axis) == extent - 1
            first = at_start if first is None else jnp.logical_and(first, at_start)
            last = at_end if last is None else jnp.logical_and(last, at_end)

        @pl.when(first)
        def _():
            for cp in copies:
                cp.start()

        body(*base_in, *base_out)

        @pl.when(last)
        def _():
            for wait in waits:
                wait()

    any_spec = pl.BlockSpec(memory_space=pl.ANY)
    outs = pl.pallas_call(
        hosting_body, grid=grid, in_specs=list(in_specs) + [any_spec] * n_h,
        out_specs=list(out_specs) + [any_spec] * n_ho,
        out_shape=(list(out_shape) + [jax.ShapeDtypeStruct(a.shape, a.dtype) for a in hosted.arrays[:n_alias]]
                   + hosted.fresh),
        scratch_shapes=[pltpu.SemaphoreType.DMA((n,)) for n in hosted.sems],
        input_output_aliases={n_in + k: n_out + k for k in range(n_alias)},
        compiler_params=_cparams(("arbitrary",) * len(grid)), name=name)(*ins, *hosted.arrays)
    return outs[:n_out], outs[n_out:]


def _tile(n, target, mult=LANES):
    best = None
    for t in range(mult, min(n, target) + 1, mult):
        if n % t == 0:
            best = t
    return best if best is not None else n


def _dot(a, b):
    return jnp.dot(a, b, preferred_element_type=F32)


def _dot_nt(a, b):
    return lax.dot_general(a, b, (((1,), (1,)), ((), ())), preferred_element_type=F32)


def _dot_tn(a, b):
    return lax.dot_general(a, b, (((0,), (0,)), ((), ())), preferred_element_type=F32)


def _bf(v):
    return v.astype(BF16)


def rms_fwd(x, gain, name):
    S, D = x.shape
    tm = _tile(S, ROW_TILE, 8)

    def body(x_ref, g_ref, o_ref):
        xf = x_ref[...]
        r = lax.rsqrt(jnp.mean(xf * xf, axis=-1, keepdims=True) + EPS)
        o_ref[...] = ((xf * r) * g_ref[...]).astype(o_ref.dtype)

    return pl.pallas_call(
        body, grid=(S // tm,),
        in_specs=[pl.BlockSpec((tm, D), lambda i: (i, 0)), pl.BlockSpec((1, D), lambda i: (0, 0))],
        out_specs=pl.BlockSpec((tm, D), lambda i: (i, 0)),
        out_shape=jax.ShapeDtypeStruct((S, D), BF16),
        compiler_params=_cparams(("parallel",)), name=name)(x, gain.reshape(1, D))


def rms_gain_grad(x, dh, name):
    S, D = x.shape
    tm = _tile(S, ROW_TILE, 8)

    def body(x_ref, dh_ref, dg_ref):
        xf = x_ref[...]
        r = lax.rsqrt(jnp.mean(xf * xf, axis=-1, keepdims=True) + EPS)

        @pl.when(pl.program_id(0) == 0)
        def _():
            dg_ref[...] = jnp.zeros_like(dg_ref)

        dg_ref[...] += jnp.sum(dh_ref[...] * (xf * r), axis=0, keepdims=True)

    row = pl.BlockSpec((tm, D), lambda i: (i, 0))
    return pl.pallas_call(
        body, grid=(S // tm,), in_specs=[row, row], out_specs=pl.BlockSpec((1, D), lambda i: (0, 0)),
        out_shape=jax.ShapeDtypeStruct((1, D), F32),
        compiler_params=_cparams(("arbitrary",)), name=name)(x, dh)


def final_loss(x, gain, target, name):
    S, D = x.shape
    tm = _tile(S, ROW_TILE, 16)

    def body(x_ref, g_ref, t_ref, loss_ref, dx_ref, dxb_ref, dg_ref):
        xf = x_ref[...]
        r = lax.rsqrt(jnp.mean(xf * xf, axis=-1, keepdims=True) + EPS)
        xhat = xf * r
        g = g_ref[...]
        diff = xhat * g - t_ref[...]
        dy = diff * (1.0 / D)
        dxhat = dy * g
        m = jnp.mean(dxhat * xhat, axis=-1, keepdims=True)
        dx = r * (dxhat - xhat * m)
        dx_ref[...] = dx
        dxb_ref[...] = _bf(dx)

        @pl.when(pl.program_id(0) == 0)
        def _():
            dg_ref[...] = jnp.zeros_like(dg_ref)
            loss_ref[...] = jnp.zeros_like(loss_ref)

        dg_ref[...] += jnp.sum(dy * xhat, axis=0, keepdims=True)
        per_tok = jnp.sum(diff * diff, axis=-1, keepdims=True) * (0.5 / D)
        loss_ref[...] += jnp.sum(per_tok, axis=0, keepdims=True)

    row = pl.BlockSpec((tm, D), lambda i: (i, 0))
    vec = pl.BlockSpec((1, D), lambda i: (0, 0))
    one = pl.BlockSpec((1, 1), lambda i: (0, 0))
    loss, dx, dxb, dg = pl.pallas_call(
        body, grid=(S // tm,), in_specs=[row, vec, row], out_specs=[one, row, row, vec],
        out_shape=[jax.ShapeDtypeStruct((1, 1), F32), jax.ShapeDtypeStruct((S, D), F32),
                   jax.ShapeDtypeStruct((S, D), BF16), jax.ShapeDtypeStruct((1, D), F32)],
        compiler_params=_cparams(("arbitrary",)), name=name)(x, gain.reshape(1, D), target)
    return loss, (dx, dxb), dg


def _wspec(w, blk, idx):
    _, l = w
    if l is None:
        return pl.BlockSpec(blk, idx)
    return pl.BlockSpec((None,) + blk, lambda *g: (l,) + idx(*g))


def mm(pairs, *, tb=False, res=None, scale=1.0, out_dtype=F32, norm_bwd=None, norm_next=None, hosted=None, name):
    M = pairs[0][0].shape[0]
    N = pairs[0][2][0].shape[-2 if tb else -1]
    tm = _tile(M, ROW_TILE, 16)
    tn = N if (norm_bwd is not None or norm_next is not None) else _tile(N, COL_TILE)
    n_p = len(pairs)
    has_res = res is not None
    n_in = 2 * n_p + has_res + (2 if norm_bwd is not None else 0) + (norm_next is not None)

    def body(*refs):
        acc = None
        for p in range(n_p):
            a = _bf(refs[2 * p][...])
            b = _bf(refs[2 * p + 1][...])
            d = _dot_nt(a, b) if tb else _dot(a, b)
            acc = d if acc is None else acc + d
        if scale != 1.0:
            acc = acc * scale
        if norm_bwd is None:
            if has_res:
                acc = acc + refs[2 * n_p][...]
            refs[n_in][...] = acc.astype(refs[n_in].dtype)
            if norm_next is not None:
                r = lax.rsqrt(jnp.mean(acc * acc, axis=-1, keepdims=True) + EPS)
                refs[n_in + 1][...] = _bf((acc * r) * refs[n_in - 1][...])
            return
        x_ref, g_ref = refs[n_in - 2], refs[n_in - 1]
        dx_ref, dxb_ref, dg_ref = refs[n_in:]
        xf = x_ref[...]
        r = lax.rsqrt(jnp.mean(xf * xf, axis=-1, keepdims=True) + EPS)
        xhat = xf * r
        dxhat = acc * g_ref[...]
        m = jnp.mean(dxhat * xhat, axis=-1, keepdims=True)
        dx = r * (dxhat - xhat * m)
        if has_res:
            dx = dx + refs[2 * n_p][...]
        dx_ref[...] = dx
        dxb_ref[...] = _bf(dx)

        @pl.when(pl.program_id(1) == 0)
        def _():
            dg_ref[...] = jnp.zeros_like(dg_ref)

        dg_ref[...] += jnp.sum(acc * xhat, axis=0, keepdims=True)

    ins, in_specs = [], []
    for a, cb, w, K, kb in pairs:
        ins += [a, w[0]]
        in_specs.append(pl.BlockSpec((tm, K), functools.partial(lambda j, i, cb: (i, cb), cb=cb)))
        if tb:
            in_specs.append(_wspec(w, (tn, K), functools.partial(lambda j, i, kb: (j, kb), kb=kb)))
        else:
            in_specs.append(_wspec(w, (K, tn), functools.partial(lambda j, i, kb: (kb, j), kb=kb)))
    tile = pl.BlockSpec((tm, tn), lambda j, i: (i, j))
    if has_res:
        ins.append(res)
        in_specs.append(tile)
    if norm_bwd is None:
        out_shape = [jax.ShapeDtypeStruct((M, N), out_dtype)]
        if norm_next is not None:
            ins.append(norm_next.reshape(1, N))
            in_specs.append(pl.BlockSpec((1, N), lambda j, i: (0, 0)))
            out_shape.append(jax.ShapeDtypeStruct((M, N), BF16))
        out = _call(body, grid=(N // tn, M // tm), in_specs=in_specs, out_specs=[tile] * len(out_shape),
                    out_shape=out_shape, ins=ins, semantics=("parallel", "parallel"), name=name, hosted=hosted)
        base = out if hosted is None else out[0]
        value = base[0] if norm_next is None else (base[0], base[1])
        return value if hosted is None else (value, out[1])
    x, gain = norm_bwd
    vec = pl.BlockSpec((1, N), lambda j, i: (0, 0))
    out = _call(
        body, grid=(1, M // tm), in_specs=in_specs + [tile, vec], out_specs=[tile, tile, vec],
        out_shape=[jax.ShapeDtypeStruct((M, N), F32), jax.ShapeDtypeStruct((M, N), BF16),
                   jax.ShapeDtypeStruct((1, N), F32)], ins=ins + [x, gain.reshape(1, N)],
        semantics=("arbitrary", "arbitrary"), name=name, hosted=hosted)
    dx, dxb, dg = out if hosted is None else out[0]
    return ((dx, dxb), dg) if hosted is None else (((dx, dxb), dg), out[1])


def mm_tn(a, b, *, hosted=None, name):
    S, Ka = a.shape
    Nb = b.shape[1]
    ts = _tile(S, 2 * ROW_TILE, 8)
    tk = _tile(Ka, COL_TILE)
    tn = _tile(Nb, COL_TILE)

    def body(a_ref, b_ref, o_ref):
        s = pl.program_id(2)

        @pl.when(s == 0)
        def _():
            o_ref[...] = jnp.zeros_like(o_ref)

        o_ref[...] += _dot_tn(_bf(a_ref[...]), _bf(b_ref[...]))

    out = _call(
        body, grid=(Ka // tk, Nb // tn, S // ts),
        in_specs=[pl.BlockSpec((ts, tk), lambda k, n, s: (s, k)),
                  pl.BlockSpec((ts, tn), lambda k, n, s: (s, n))],
        out_specs=[pl.BlockSpec((tk, tn), lambda k, n, s: (k, n))],
        out_shape=[jax.ShapeDtypeStruct((Ka, Nb), F32)], ins=[a, b],
        semantics=("parallel", "parallel", "arbitrary"), name=name, hosted=hosted)
    return out[0] if hosted is None else (out[0][0], out[1])


def mm_tn_pair(a, b1, b2, *, hosted=None, name):
    S, Ka = a.shape
    Nb = b1.shape[1]
    ts = _tile(S, 2 * ROW_TILE, 8)
    tk = _tile(Ka, COL_TILE)
    tn = _tile(Nb, COL_TILE)

    def body(a_ref, b1_ref, b2_ref, o1_ref, o2_ref):
        @pl.when(pl.program_id(2) == 0)
        def _():
            o1_ref[...] = jnp.zeros_like(o1_ref)
            o2_ref[...] = jnp.zeros_like(o2_ref)

        av = _bf(a_ref[...])
        o1_ref[...] += _dot_tn(av, _bf(b1_ref[...]))
        o2_ref[...] += _dot_tn(av, _bf(b2_ref[...]))

    b_spec = pl.BlockSpec((ts, tn), lambda k, n, s: (s, n))
    o_spec = pl.BlockSpec((tk, tn), lambda k, n, s: (k, n))
    sh = jax.ShapeDtypeStruct((Ka, Nb), F32)
    out = _call(
        body, grid=(Ka // tk, Nb // tn, S // ts),
        in_specs=[pl.BlockSpec((ts, tk), lambda k, n, s: (s, k)), b_spec, b_spec],
        out_specs=[o_spec, o_spec], out_shape=[sh, sh], ins=[a, b1, b2],
        semantics=("parallel", "parallel", "arbitrary"), name=name, hosted=hosted)
    return tuple(out) if hosted is None else (tuple(out[0]), out[1])


def _sigmoid(x):
    return 0.5 * jnp.tanh(0.5 * x) + 0.5


def ffn_up(h, wg, wu, name, hosted=None):
    S, D = h.shape
    Fd = wg[0].shape[-1]
    tm = _tile(S, 2 * ROW_TILE, 16)
    tn = _tile(Fd, COL_TILE)

    def body(h_ref, wg_ref, wu_ref, g_ref, u_ref, a_ref):
        hv = h_ref[...]
        g = _dot(hv, wg_ref[...])
        u = _dot(hv, wu_ref[...])
        g_ref[...] = _bf(g)
        u_ref[...] = _bf(u)
        a_ref[...] = _bf(g * _sigmoid(g) * u)

    blk = pl.BlockSpec((tm, tn), lambda j, i: (i, j))
    sh = jax.ShapeDtypeStruct((S, Fd), BF16)
    return _call(
        body, grid=(Fd // tn, S // tm),
        in_specs=[pl.BlockSpec((tm, D), lambda j, i: (i, 0)), _wspec(wg, (D, tn), lambda j, i: (0, j)),
                  _wspec(wu, (D, tn), lambda j, i: (0, j))],
        out_specs=[blk, blk, blk], out_shape=[sh, sh, sh], ins=[h, wg[0], wu[0]],
        semantics=("parallel", "parallel"), name=name, hosted=hosted)


def ffn_down_bwd(dxo, wd, g, u, a, name, hosted=None):
    S, D = dxo.shape
    Fd = wd[0].shape[-2]
    tm = _tile(S, ROW_TILE, 8)
    tn = _tile(Fd, COL_TILE)

    def body(dx_ref, wd_ref, g_ref, u_ref, a_ref, dg_ref, du_ref, dwd_ref):
        dx = _bf(dx_ref[...])
        da = _dot_nt(dx, wd_ref[...]) * 0.5
        g = g_ref[...].astype(F32)
        u = u_ref[...].astype(F32)
        sg = _sigmoid(g)
        dg_ref[...] = _bf(da * u * (sg * (1.0 + g * (1.0 - sg))))
        du_ref[...] = _bf(da * (g * sg))

        @pl.when(pl.program_id(1) == 0)
        def _():
            dwd_ref[...] = jnp.zeros_like(dwd_ref)

        dwd_ref[...] += _dot_tn(a_ref[...], dx) * 0.5

    blk = pl.BlockSpec((tm, tn), lambda j, i: (i, j))
    sh = jax.ShapeDtypeStruct((S, Fd), BF16)
    return _call(
        body, grid=(Fd // tn, S // tm),
        in_specs=[pl.BlockSpec((tm, D), lambda j, i: (i, 0)), _wspec(wd, (tn, D), lambda j, i: (j, 0)),
                  blk, blk, blk],
        out_specs=[blk, blk, pl.BlockSpec((tn, D), lambda j, i: (j, 0))],
        out_shape=[sh, sh, jax.ShapeDtypeStruct((Fd, D), F32)], ins=[dxo, wd[0], g, u, a],
        semantics=("parallel", "arbitrary"), name=name, hosted=hosted)


_GELU_C = math.sqrt(2.0 / math.pi)


def _gelu(p):
    return 0.5 * p * (1.0 + jnp.tanh(_GELU_C * (p + 0.044715 * (p * p * p))))


def _gelu_grad(p):
    th = jnp.tanh(_GELU_C * (p + 0.044715 * (p * p * p)))
    return 0.5 * (1.0 + th) + 0.5 * p * (1.0 - th * th) * (_GELU_C * (1.0 + 3.0 * 0.044715 * (p * p)))


def _chunk_mask():
    t = lax.broadcasted_iota(jnp.int32, (GM_CHUNK, GM_CHUNK), 0)
    s = lax.broadcasted_iota(jnp.int32, (GM_CHUNK, GM_CHUNK), 1)
    return (s // CHUNK) <= (t // CHUNK)


def gmlp_fwd(proj, v_gain, w_s, b_mat, name):
    S = proj.shape[0]
    tw = _tile(S, ROW_TILE, GM_CHUNK)
    n_win = tw // GM_CHUNK

    def body(p_ref, gain_ref, w_ref, b_ref, y_ref):
        mask = _chunk_mask()
        v = _gelu(p_ref[:, GM_W:])
        r = lax.rsqrt(jnp.mean(v * v, axis=-1, keepdims=True) + EPS)
        vn = _bf(v * r * gain_ref[...])
        for g in range(GM_GROUPS):
            wm = _bf(jnp.where(mask, w_ref[g], 0.0))
            cs = slice(g * GM_CHUNK, (g + 1) * GM_CHUNK)
            for w in range(n_win):
                rs = slice(w * GM_CHUNK, (w + 1) * GM_CHUNK)
                mixed = _dot(wm, vn[rs, cs]) + b_ref[g]
                y_ref[rs, cs] = _bf(_gelu(p_ref[rs, cs]) * mixed)

    return pl.pallas_call(
        body, grid=(S // tw,),
        in_specs=[pl.BlockSpec((tw, 2 * GM_W), lambda i: (i, 0)), pl.BlockSpec((1, GM_W), lambda i: (0, 0)),
                  pl.BlockSpec((GM_GROUPS, GM_CHUNK, GM_CHUNK), lambda i: (0, 0, 0)),
                  pl.BlockSpec((GM_GROUPS, GM_CHUNK, GM_CHUNK), lambda i: (0, 0, 0))],
        out_specs=pl.BlockSpec((tw, GM_W), lambda i: (i, 0)),
        out_shape=jax.ShapeDtypeStruct((S, GM_W), BF16),
        compiler_params=_cparams(("parallel",)), name=name)(proj, v_gain.reshape(1, GM_W), w_s, b_mat)


def gmlp_bwd(proj, dy, v_gain, w_s, b_mat, name):
    S = proj.shape[0]
    tw = _tile(S, ROW_TILE, GM_CHUNK)
    n_win = tw // GM_CHUNK

    def body(p_ref, dy_ref, gain_ref, w_ref, b_ref, dp_ref, dw_ref, db_ref, dgain_ref, dvn_ref):
        step = pl.program_id(0)

        @pl.when(step == 0)
        def _():
            dw_ref[...] = jnp.zeros_like(dw_ref)
            db_ref[...] = jnp.zeros_like(db_ref)
            dgain_ref[...] = jnp.zeros_like(dgain_ref)

        mask = _chunk_mask()
        pv = p_ref[:, GM_W:]
        v = _gelu(pv)
        r = lax.rsqrt(jnp.mean(v * v, axis=-1, keepdims=True) + EPS)
        vhat = v * r
        gain = gain_ref[...]
        vn = _bf(vhat * gain)
        for g in range(GM_GROUPS):
            wm = _bf(jnp.where(mask, w_ref[g], 0.0))
            cs = slice(g * GM_CHUNK, (g + 1) * GM_CHUNK)
            dw_acc = jnp.zeros((GM_CHUNK, GM_CHUNK), F32)
            db_acc = jnp.zeros((GM_CHUNK, GM_CHUNK), F32)
            for w in range(n_win):
                rs = slice(w * GM_CHUNK, (w + 1) * GM_CHUNK)
                pu = p_ref[rs, cs]
                u = _gelu(pu)
                vn_blk = vn[rs, cs]
                mixed = _dot(wm, vn_blk) + b_ref[g]
                dyb = dy_ref[rs, cs]
                dp_ref[rs, cs] = _bf(dyb * mixed * _gelu_grad(pu))
                dmix = dyb * u
                db_acc = db_acc + dmix
                dmix_b = _bf(dmix)
                dw_acc = dw_acc + _dot_nt(dmix_b, vn_blk)
                dvn_ref[rs, cs] = _dot_tn(wm, dmix_b)
            dw_ref[g] += jnp.where(mask, dw_acc, 0.0)
            db_ref[g] += jnp.broadcast_to(jnp.sum(db_acc, axis=-1, keepdims=True), (GM_CHUNK, GM_CHUNK))
        dvn = dvn_ref[...]
        dgain_ref[...] += jnp.sum(dvn * vhat, axis=0, keepdims=True)
        dvhat = dvn * gain
        m = jnp.mean(dvhat * vhat, axis=-1, keepdims=True)
        dv = r * (dvhat - vhat * m)
        dp_ref[:, GM_W:] = _bf(dv * _gelu_grad(pv))

    sq = pl.BlockSpec((GM_GROUPS, GM_CHUNK, GM_CHUNK), lambda i: (0, 0, 0))
    sq_sh = jax.ShapeDtypeStruct((GM_GROUPS, GM_CHUNK, GM_CHUNK), F32)
    return pl.pallas_call(
        body, grid=(S // tw,),
        in_specs=[pl.BlockSpec((tw, 2 * GM_W), lambda i: (i, 0)), pl.BlockSpec((tw, GM_W), lambda i: (i, 0)),
                  pl.BlockSpec((1, GM_W), lambda i: (0, 0)), sq, sq],
        out_specs=[pl.BlockSpec((tw, 2 * GM_W), lambda i: (i, 0)), sq, sq,
                   pl.BlockSpec((1, GM_W), lambda i: (0, 0))],
        out_shape=[jax.ShapeDtypeStruct((S, 2 * GM_W), BF16), sq_sh, sq_sh,
                   jax.ShapeDtypeStruct((1, GM_W), F32)],
        scratch_shapes=[pltpu.VMEM((tw, GM_W), F32)],
        compiler_params=_cparams(("arbitrary",)), name=name)(proj, dy, v_gain.reshape(1, GM_W), w_s, b_mat)


def _head_mask(h, width):
    lane = lax.broadcasted_iota(jnp.int32, (1, width), 1)
    return (lane >= HEAD_DIM * h) & (lane < HEAD_DIM * (h + 1))


def _mem_probs(q, k, h):
    kh = jnp.where(_head_mask(h, MEM_W), k, jnp.zeros_like(k))
    s = _dot_nt(q, kh) * (HEAD_DIM ** -0.5)
    s = s - jnp.max(s, axis=-1, keepdims=True)
    e = jnp.exp(s)
    return e * (1.0 / jnp.sum(e, axis=-1, keepdims=True)), kh


def memattn_fwd(proj, q_cb, mem_kv, name):
    S = proj.shape[0]
    NM = mem_kv.shape[0]
    tm = _tile(S, ROW_TILE, 8)

    def body(q_ref, kv_ref, o_ref):
        q = _bf(q_ref[...])
        k = _bf(kv_ref[:, :MEM_W])
        v = _bf(kv_ref[:, MEM_W:])
        acc = jnp.zeros((tm, MEM_W), F32)
        for h in range(MEM_HEADS):
            p, _ = _mem_probs(q, k, h)
            vh = jnp.where(_head_mask(h, MEM_W), v, jnp.zeros_like(v))
            acc = acc + _dot(_bf(p), vh)
        o_ref[...] = _bf(acc)

    return pl.pallas_call(
        body, grid=(S // tm,),
        in_specs=[pl.BlockSpec((tm, MEM_W), lambda i: (i, q_cb)), pl.BlockSpec((NM, 2 * MEM_W), lambda i: (0, 0))],
        out_specs=pl.BlockSpec((tm, MEM_W), lambda i: (i, 0)),
        out_shape=jax.ShapeDtypeStruct((S, MEM_W), BF16),
        compiler_params=_cparams(("parallel",)), name=name)(proj, mem_kv)


def memattn_bwd(proj, q_cb, mem_kv, dy, dy_cb, name):
    S = proj.shape[0]
    NM = mem_kv.shape[0]
    tm = _tile(S, ROW_TILE, 8)

    def body(q_ref, kv_ref, do_ref, dq_ref, dkv_ref):
        @pl.when(pl.program_id(0) == 0)
        def _():
            dkv_ref[...] = jnp.zeros_like(dkv_ref)

        q = _bf(q_ref[...])
        k = _bf(kv_ref[:, :MEM_W])
        v = _bf(kv_ref[:, MEM_W:])
        do = _bf(do_ref[...])
        dq = jnp.zeros((tm, MEM_W), F32)
        dk = jnp.zeros((NM, MEM_W), F32)
        dv = jnp.zeros((NM, MEM_W), F32)
        for h in range(MEM_HEADS):
            hm = _head_mask(h, MEM_W)
            p, kh = _mem_probs(q, k, h)
            vh = jnp.where(hm, v, jnp.zeros_like(v))
            dp = _dot_nt(do, vh)
            ds = _bf(p * (dp - jnp.sum(p * dp, axis=-1, keepdims=True)) * (HEAD_DIM ** -0.5))
            dq = dq + _dot(ds, kh)
            dk = dk + jnp.where(hm, _dot_tn(ds, q), 0.0)
            dv = dv + jnp.where(hm, _dot_tn(_bf(p), do), 0.0)
        dq_ref[...] = _bf(dq)
        dkv_ref[:, :MEM_W] += dk
        dkv_ref[:, MEM_W:] += dv

    return pl.pallas_call(
        body, grid=(S // tm,),
        in_specs=[pl.BlockSpec((tm, MEM_W), lambda i: (i, q_cb)), pl.BlockSpec((NM, 2 * MEM_W), lambda i: (0, 0)),
                  pl.BlockSpec((tm, MEM_W), lambda i: (i, dy_cb))],
        out_specs=[pl.BlockSpec((tm, MEM_W), lambda i: (i, 0)), pl.BlockSpec((NM, 2 * MEM_W), lambda i: (0, 0))],
        out_shape=[jax.ShapeDtypeStruct((S, MEM_W), BF16), jax.ShapeDtypeStruct((NM, 2 * MEM_W), F32)],
        compiler_params=_cparams(("arbitrary",)), name=name)(proj, mem_kv, dy)


def _split_bf(v):
    hi = _bf(v)
    return hi, _bf(v - hi.astype(F32))


def _sb_scores(qp, k2, dpos, row, col, tri):
    z = _dot_nt(qp, k2) * (HEAD_DIM ** -0.5)
    t = jnp.log(1.0 + jnp.exp(-jnp.abs(z)))
    lb = jnp.minimum(z, 0.0) - t
    lom = jnp.where(_sb_mask(dpos, row, col), -jnp.maximum(z, 0.0) - t, 0.0)
    l_hi, l_lo = _split_bf(lom)
    insuf = _dot(l_hi, tri) + _dot(l_lo, tri)
    return lb + insuf, jnp.sum(lom, axis=1, keepdims=True), lb


def _sb_mask(dpos, row, col):
    return (col - row) < dpos


def _sb_weights(s, run, dpos, row, col):
    return jnp.where(_sb_mask(dpos, row, col), jnp.exp(s + run), 0.0)


def _sb_alive(runs):
    top = runs[0]
    for r in runs[1:]:
        top = jnp.maximum(top, r)
    return jnp.max(top) > SB_DEAD


def _sb_consts():
    row = lax.broadcasted_iota(jnp.int32, (2 * SB_BLOCK, SB_BLOCK), 0) % SB_BLOCK
    col = lax.broadcasted_iota(jnp.int32, (2 * SB_BLOCK, SB_BLOCK), 1)
    r = lax.broadcasted_iota(jnp.int32, (SB_BLOCK, SB_BLOCK), 0)
    c = lax.broadcasted_iota(jnp.int32, (SB_BLOCK, SB_BLOCK), 1)
    return row, col, _bf(jnp.where(r > c, 1.0, 0.0)), _bf(jnp.where(r >= c, 1.0, 0.0))


def _stack_heads(x2, masks):
    return jnp.concatenate([_bf(jnp.where(hm, x2, jnp.zeros_like(x2))) for hm in masks], axis=0)


def _side_by_side(x):
    return jnp.concatenate([x[:SB_BLOCK], x[SB_BLOCK:]], axis=1)


def sb_fwd(proj, kv, name):
    S = proj.shape[0]
    n_g = GM_W // SB_COLS
    tq = _tile(S, SB_Q_CHUNK, SB_BLOCK)
    nqb = tq // SB_BLOCK

    def body(q_ref, k_ref, v_ref, o_ref, s_ref):
        chunk = pl.program_id(1)
        row, col, tri, _ = _sb_consts()
        masks = [_head_mask(h, LANES) for h in range(2)]
        pairs = [slice(p * LANES, (p + 1) * LANES) for p in range(SB_PAIRS)]

        def q_loop(qi, _):
            i = chunk * nqb + qi
            rows = pl.ds(pl.multiple_of(qi * SB_BLOCK, SB_BLOCK), SB_BLOCK)
            qps = [_stack_heads(q_ref[rows, cs], masks) for cs in pairs]

            def look_ahead(t):
                keys = pl.ds(pl.multiple_of(jnp.maximum(i - t, 0) * SB_BLOCK, SB_BLOCK), SB_BLOCK)
                sums = []
                for p, cs in enumerate(pairs):
                    s, lom_sum, _ = _sb_scores(qps[p], k_ref[keys, cs], t * SB_BLOCK, row, col, tri)
                    s_ref[p] = s
                    sums.append(lom_sum)
                return tuple(sums)

            def k_step(carry):
                t, accs, runs, sums, _ = carry
                keys = pl.ds(pl.multiple_of((i - t) * SB_BLOCK, SB_BLOCK), SB_BLOCK)
                new_accs, new_runs = [], []
                for p, cs in enumerate(pairs):
                    a = _sb_weights(s_ref[p], runs[p], t * SB_BLOCK, row, col)
                    a_hi, a_lo = _split_bf(_side_by_side(a))
                    vp = _stack_heads(v_ref[keys, cs], masks)
                    new_accs.append(accs[p] + _dot(a_hi, vp) + _dot(a_lo, vp))
                    new_runs.append(runs[p] + sums[p])
                return t + 1, tuple(new_accs), tuple(new_runs), look_ahead(t + 1), _sb_alive(new_runs)

            zero = jnp.zeros((2 * SB_BLOCK, 1), F32)
            _, accs, _, _, _ = lax.while_loop(
                lambda carry: jnp.logical_and(carry[0] <= i, carry[4]), k_step,
                (jnp.int32(0), (jnp.zeros((SB_BLOCK, LANES), F32),) * SB_PAIRS, (zero,) * SB_PAIRS,
                 look_ahead(jnp.int32(0)), jnp.bool_(True)))
            for p, cs in enumerate(pairs):
                o_ref[rows, cs] = accs[p]
            return 0

        lax.fori_loop(0, nqb, q_loop, 0)

    return pl.pallas_call(
        body, grid=(n_g, S // tq),
        in_specs=[pl.BlockSpec((tq, SB_COLS), lambda g, c: (c, g)),
                  pl.BlockSpec((S, SB_COLS), lambda g, c: (0, g)),
                  pl.BlockSpec((S, SB_COLS), lambda g, c: (0, n_g + g))],
        out_specs=pl.BlockSpec((tq, SB_COLS), lambda g, c: (c, g)),
        out_shape=jax.ShapeDtypeStruct((S, GM_W), F32),
        scratch_shapes=[pltpu.VMEM((SB_PAIRS, 2 * SB_BLOCK, LANES), F32)],
        compiler_params=_cparams(("parallel", "parallel")), name=name)(proj, kv, kv)


def sb_bwd(proj, kv, dy, out, dk_init, dv_init, name):
    S = proj.shape[0]
    n_g = GM_W // SB_COLS
    tq = _tile(S, SB_Q_CHUNK, SB_BLOCK)
    nqb = tq // SB_BLOCK
    n_chunks = S // tq
    has_init = dk_init is not None
    scale = HEAD_DIM ** -0.5

    def body(*refs):
        s_ref, beta_ref, da_ref = refs[-3:]
        if has_init:
            q_ref, k_ref, v_ref, do_ref, out_ref, dki_ref, dvi_ref, dq_ref, dko_ref, dvo_ref, dk_acc, dv_acc = refs[:-3]
        else:
            q_ref, k_ref, v_ref, do_ref, out_ref, dq_ref, dko_ref, dvo_ref, dk_acc, dv_acc = refs[:-3]
        chunk = pl.program_id(1)
        cols = pl.ds(pl.multiple_of(pl.program_id(0) * SB_COLS, SB_COLS), SB_COLS)

        @pl.when(chunk == 0)
        def _():
            if has_init:
                pltpu.sync_copy(dki_ref.at[:, cols], dk_acc)
                pltpu.sync_copy(dvi_ref.at[:, cols], dv_acc)
            else:
                dk_acc[...] = jnp.zeros_like(dk_acc)
                dv_acc[...] = jnp.zeros_like(dv_acc)

        row, col, tri, tri_inc = _sb_consts()
        masks = [_head_mask(h, LANES) for h in range(2)]
        pairs = [slice(p * LANES, (p + 1) * LANES) for p in range(SB_PAIRS)]

        def q_loop(qi, _):
            i = chunk * nqb + qi
            rows = pl.ds(pl.multiple_of(qi * SB_BLOCK, SB_BLOCK), SB_BLOCK)
            qps = [_stack_heads(q_ref[rows, cs], masks) for cs in pairs]
            dops = [_stack_heads(do_ref[rows, cs], masks) for cs in pairs]
            e_tots = [jnp.sum(dop.astype(F32) * jnp.concatenate([out_ref[rows, cs]] * 2, axis=0), axis=1,
                              keepdims=True) for dop, cs in zip(dops, pairs)]

            def look_ahead(t):
                keys = pl.ds(pl.multiple_of(jnp.maximum(i - t, 0) * SB_BLOCK, SB_BLOCK), SB_BLOCK)
                sums = []
                for p, cs in enumerate(pairs):
                    s, lom_sum, lb = _sb_scores(qps[p], k_ref[keys, cs], t * SB_BLOCK, row, col, tri)
                    s_ref[p] = s
                    beta_ref[p] = jnp.exp(lb)
                    da_ref[p] = _dot_nt(dops[p], v_ref[keys, cs])
                    sums.append(lom_sum)
                return tuple(sums)

            def k_step(carry):
                t, dqs, runs, e_runs, sums, _ = carry
                keys = pl.ds(pl.multiple_of((i - t) * SB_BLOCK, SB_BLOCK), SB_BLOCK)
                new_dqs, new_runs, new_e_runs = [], [], []
                for p, cs in enumerate(pairs):
                    a = _sb_weights(s_ref[p], runs[p], t * SB_BLOCK, row, col)
                    e = a * da_ref[p]
                    e_hi, e_lo = _split_bf(e)
                    before = e_tots[p] - e_runs[p] - (_dot(e_hi, tri_inc) + _dot(e_lo, tri_inc))
                    beta = beta_ref[p]
                    dz = jnp.where(_sb_mask(t * SB_BLOCK, row, col), e * (1.0 - beta) - before * beta, 0.0)
                    dz = _bf(dz * scale)
                    new_dqs.append(dqs[p] + _dot(_side_by_side(dz), _stack_heads(k_ref[keys, cs], masks)))
                    dk_acc[keys, cs] += _dot_tn(dz, qps[p])
                    dv_acc[keys, cs] += _dot_tn(_bf(a), dops[p])
                    new_runs.append(runs[p] + sums[p])
                    new_e_runs.append(e_runs[p] + jnp.sum(e, axis=1, keepdims=True))
                return (t + 1, tuple(new_dqs), tuple(new_runs), tuple(new_e_runs), look_ahead(t + 1),
                        _sb_alive(new_runs))

            zero = jnp.zeros((2 * SB_BLOCK, 1), F32)
            zeros = (zero,) * SB_PAIRS
            _, dqs, _, _, _, _ = lax.while_loop(
                lambda carry: jnp.logical_and(carry[0] <= i, carry[5]), k_step,
                (jnp.int32(0), (jnp.zeros((SB_BLOCK, LANES), F32),) * SB_PAIRS, zeros, zeros,
                 look_ahead(jnp.int32(0)), jnp.bool_(True)))
            for p, cs in enumerate(pairs):
                dq_ref[rows, cs] = dqs[p]
            return 0

        lax.fori_loop(0, nqb, q_loop, 0)

        @pl.when(chunk == n_chunks - 1)
        def _():
            pltpu.sync_copy(dk_acc, dko_ref.at[:, cols])
            pltpu.sync_copy(dv_acc, dvo_ref.at[:, cols])

    qspec = pl.BlockSpec((tq, SB_COLS), lambda g, c: (c, g))
    kspec = pl.BlockSpec((S, SB_COLS), lambda g, c: (0, g))
    in_specs = [qspec, kspec, pl.BlockSpec((S, SB_COLS), lambda g, c: (0, n_g + g)), qspec, qspec]
    ins = [proj, kv, kv, dy, out]
    if has_init:
        in_specs += [_ANY, _ANY]
        ins += [dk_init, dv_init]
    sh = jax.ShapeDtypeStruct((S, GM_W), F32)
    return pl.pallas_call(
        body, grid=(n_g, n_chunks), in_specs=in_specs, out_specs=[qspec, _ANY, _ANY],
        out_shape=[sh, sh, sh],
        scratch_shapes=[pltpu.VMEM((S, SB_COLS), F32), pltpu.VMEM((S, SB_COLS), F32)]
        + [pltpu.VMEM((SB_PAIRS, 2 * SB_BLOCK, LANES), F32)] * 3,
        compiler_params=_cparams(("arbitrary", "arbitrary")), name=name)(*ins)


def adamw(w, g, m, v, name):
    shape = w.shape
    C = shape[-1] if w.ndim > 1 else shape[0]
    R = w.size // C
    tr = _tile(R, ROW_TILE, 8)

    def body(w_ref, g_ref, m_ref, v_ref, d_ref, nm_ref, nv_ref):
        gv = g_ref[...]
        m2 = ADAM_B1 * m_ref[...] + (1.0 - ADAM_B1) * gv
        v2 = ADAM_B2 * v_ref[...] + (1.0 - ADAM_B2) * (gv * gv)
        m_hat = m2 / (1.0 - ADAM_B1 ** ADAM_STEP)
        v_hat = v2 / (1.0 - ADAM_B2 ** ADAM_STEP)
        d_ref[...] = -ADAM_LR * (m_hat / (jnp.sqrt(v_hat) + ADAM_EPS) + ADAM_WD * w_ref[...])
        nm_ref[...] = m2
        nv_ref[...] = v2

    blk = pl.BlockSpec((tr, C), lambda i: (i, 0))
    sh = jax.ShapeDtypeStruct((R, C), F32)
    outs = pl.pallas_call(
        body, grid=(R // tr,), in_specs=[blk] * 4, out_specs=[blk] * 3, out_shape=[sh] * 3,
        compiler_params=_cparams(("parallel",)), name=name)(
            w.reshape(R, C), g.reshape(R, C), m.reshape(R, C), v.reshape(R, C))
    return tuple(o.reshape(shape) for o in outs)


def _place():
    return lax.axis_index("x"), lax.axis_index("y"), lax.axis_index("c")


def _other_chips(x, y):
    return [(1 - x, y), (x, 1 - y), (1 - x, 1 - y)]


_ANY = pl.BlockSpec(memory_space=pl.ANY)
LOCAL_CHUNKS = 8
SEG_LOCAL_CHUNKS = 4


def _remote(src, dst, send_sems, recv_sems, k, to):
    return pltpu.make_async_remote_copy(src_ref=src, dst_ref=dst, send_sem=send_sems.at[k], recv_sem=recv_sems.at[k],
                                        device_id=to, device_id_type=MESH)


def _gather_stage(segs, n_gath, stage):
    def make(ins, outs, send_sems, recv_sems, *local_sems):
        x, y, c = _place()
        sibling = (x, y, 1 - c)
        chips = _other_chips(x, y)
        copies, waits = [], []
        for s, (b, r0, n) in enumerate(segs):
            half = n // 2

            def piece(px, py, pc, b=b, r0=r0, half=half):
                return outs[b].at[2 * px + py, pl.ds(r0 + pc * half, half), :]

            for j, chip in enumerate(chips):
                if stage == 0:
                    mine = ins[n_gath + b].at[pl.ds(r0 + c * half, half), :]
                    cp = _remote(mine, piece(x, y, c), send_sems, recv_sems, 3 * s + j, (*chip, c))
                    landed, sender = piece(*chip, c), (*chip, c)
                else:
                    cp = _remote(piece(*chip, c), piece(*chip, c), send_sems, recv_sems, 3 * s + j, sibling)
                    landed, sender = piece(*chip, 1 - c), sibling
                copies.append(cp)
                waits.append(cp.wait_send)
                waits.append(_remote(landed, landed, send_sems, recv_sems, 3 * s + j, sender).wait_recv)
            if stage == 0:
                rows = n // SEG_LOCAL_CHUNKS
                for q in range(SEG_LOCAL_CHUNKS):
                    r = pl.ds(r0 + q * rows, rows)
                    cp = pltpu.make_async_copy(ins[n_gath + b].at[r, :], outs[b].at[2 * x + y, r, :],
                                               local_sems[0].at[SEG_LOCAL_CHUNKS * s + q])
                    copies.append(cp)
                    waits.append(cp.wait)
        return copies, waits

    return make


def _gather_sems(segs, stage):
    return [3 * len(segs), 3 * len(segs)] + ([SEG_LOCAL_CHUNKS * len(segs)] if stage == 0 else [])


def all_gather_segments(bufs, segs, name):
    nb = len(bufs)
    n0, n1 = len(_gather_sems(segs, 0)), len(_gather_sems(segs, 1))

    def body(*refs):
        in_refs, out_refs = refs[:nb], refs[nb:2 * nb]
        sems = refs[2 * nb:]
        for stage, stage_sems in ((0, sems[:n0]), (1, sems[n0:n0 + n1])):
            copies, waits = _gather_stage(segs, nb, stage)(list(out_refs) + list(in_refs), out_refs, *stage_sems)
            for cp in copies:
                cp.start()
            for wait in waits:
                wait()

    return pl.pallas_call(
        body, in_specs=[_ANY] * nb, out_specs=[_ANY] * nb,
        out_shape=[jax.ShapeDtypeStruct((N_CHIPS,) + b.shape, b.dtype) for b in bufs],
        scratch_shapes=[pltpu.SemaphoreType.DMA((n,)) for n in _gather_sems(segs, 0) + _gather_sems(segs, 1)],
        name=name)(*bufs)


def hosted_gather(gathered, bufs, segs, stage):
    arrays = list(gathered) + (list(bufs) if stage == 0 else [])
    return Hosted(arrays, len(gathered), _gather_sems(segs, stage), _gather_stage(segs, len(gathered), stage))


def _exchange(step, arrays, name):
    n_h, n_alias, n_fresh = len(step.arrays), step.n_out, len(step.fresh)

    def body(*refs):
        h_in = refs[:n_h]
        h_out = refs[n_h:n_h + n_alias + n_fresh]
        copies, waits = step.make(h_in, h_out, *refs[n_h + n_alias + n_fresh:])
        for cp in copies:
            cp.start()
        for wait in waits:
            wait()

    return pl.pallas_call(
        body, in_specs=[_ANY] * n_h, out_specs=[_ANY] * (n_alias + n_fresh),
        out_shape=[jax.ShapeDtypeStruct(a.shape, a.dtype) for a in step.arrays[:n_alias]] + step.fresh,
        scratch_shapes=[pltpu.SemaphoreType.DMA((n,)) for n in step.sems],
        input_output_aliases={k: k for k in range(n_alias)}, name=name)(*arrays)


def swap_step(gs):
    nb = len(gs)

    def make(ins, outs, send_sems, recv_sems):
        x, y, c = _place()
        copies = []
        for b in range(nb):
            half = gs[b].shape[1] // 2
            for k in range(N_CHIPS):
                copies.append(_remote(ins[b].at[k, pl.ds((1 - c) * half, half), :], outs[b].at[k], send_sems,
                                      recv_sems, N_CHIPS * b + k, (x, y, 1 - c)))
        return copies, [cp.wait for cp in copies]

    fresh = [jax.ShapeDtypeStruct((N_CHIPS, g.shape[1] // 2, g.shape[2]), g.dtype) for g in gs]
    return Hosted(list(gs), 0, [N_CHIPS * nb, N_CHIPS * nb], make, fresh)


def _core_index():
    return lax.axis_index("c").astype(jnp.int32).reshape(1)


def add_cores(g, theirs, name):
    n, Ph, Wd = theirs.shape
    tr = _tile(Ph, ROW_TILE, 16)
    steps = Ph // tr

    def body(c_ref, g_ref, t_ref, o_ref):
        o_ref[...] = _bf(g_ref[...] + t_ref[...])

    blk = pl.BlockSpec((None, tr, Wd), lambda k, i, c_ref: (k, i, 0))
    return pl.pallas_call(
        body,
        grid_spec=pltpu.PrefetchScalarGridSpec(
            num_scalar_prefetch=1, grid=(n, steps),
            in_specs=[pl.BlockSpec((None, tr, Wd), lambda k, i, c_ref: (k, c_ref[0] * steps + i, 0)), blk],
            out_specs=blk),
        out_shape=jax.ShapeDtypeStruct((n, Ph, Wd), BF16),
        compiler_params=_cparams(("parallel", "parallel")), name=name)(_core_index(), g, theirs)


def add_chips(slots, name):
    n, Ph, Wd = slots.shape
    tr = _tile(Ph, ROW_TILE, 16)
    steps = Ph // tr

    def body(c_ref, *refs):
        acc = refs[0][...].astype(F32)
        for k in range(1, n):
            acc = acc + refs[k][...].astype(F32)
        refs[n][...] = acc

    in_specs = [pl.BlockSpec((None, tr, Wd), functools.partial(lambda i, c_ref, k: (k, i, 0), k=k)) for k in range(n)]
    return pl.pallas_call(
        body,
        grid_spec=pltpu.PrefetchScalarGridSpec(
            num_scalar_prefetch=1, grid=(steps,), in_specs=in_specs,
            out_specs=pl.BlockSpec((tr, Wd), lambda i, c_ref: (c_ref[0] * steps + i, 0))),
        out_shape=jax.ShapeDtypeStruct((2 * Ph, Wd), F32),
        compiler_params=_cparams(("parallel",)), name=name)(_core_index(), *([slots] * n))


def _whole_tile_chunks(rows, most, tile_rows):
    return max(n for n in range(1, most + 1) if rows % (n * tile_rows) == 0)


def scatter_step(hs):
    nb = len(hs)
    chunks = [_whole_tile_chunks(h.shape[1], LOCAL_CHUNKS, 16) for h in hs]

    def make(ins, outs, send_sems, recv_sems, local_sems):
        x, y, c = _place()
        me = 2 * x + y
        copies, waits = [], []
        for b in range(nb):
            for j, (cx, cy) in enumerate(_other_chips(x, y)):
                cp = _remote(ins[b].at[2 * cx + cy], outs[b].at[me], send_sems, recv_sems, 3 * b + j, (cx, cy, c))
                copies.append(cp)
                waits.append(cp.wait_send)
                landed = outs[b].at[2 * cx + cy]
                waits.append(_remote(landed, landed, send_sems, recv_sems, 3 * b + j, (cx, cy, c)).wait_recv)
            rows = hs[b].shape[1] // chunks[b]
            for q in range(chunks[b]):
                r = pl.ds(q * rows, rows)
                cp = pltpu.make_async_copy(ins[b].at[me, r, :], outs[b].at[me, r, :],
                                           local_sems.at[sum(chunks[:b]) + q])
                copies.append(cp)
                waits.append(cp.wait)
        return copies, waits

    fresh = [jax.ShapeDtypeStruct(h.shape, h.dtype) for h in hs]
    return Hosted(list(hs), 0, [3 * nb, 3 * nb, sum(chunks)], make, fresh)


JOIN_CHUNKS = 4


def join_step(ts):
    nb = len(ts)
    n = JOIN_CHUNKS

    def make(ins, outs, send_sems, recv_sems):
        x, y, c = _place()
        copies, waits = [], []
        for b in range(nb):
            half = ts[b].shape[0] // 2
            rows = half // n
            for k in range(n):
                part = outs[b].at[pl.ds(c * half + k * rows, rows), :]
                cp = _remote(part, part, send_sems, recv_sems, n * b + k, (x, y, 1 - c))
                copies.append(cp)
                waits.append(cp.wait_send)
                landed = outs[b].at[pl.ds((1 - c) * half + k * rows, rows), :]
                waits.append(_remote(landed, landed, send_sems, recv_sems, n * b + k, (x, y, 1 - c)).wait_recv)
        return copies, waits

    return Hosted(list(ts), nb, [n * nb, n * nb], make)


WIDE = ['ffn1_w_gate', 'ffn1_w_up', 'ffn2_w_gate', 'ffn2_w_up']


def _rows_of(p):
    if p.shape[-1] == PACK_W and (p.size // PACK_W) % PART_ROWS == 0:
        return p.reshape(-1, PACK_W)
    flat = p.reshape(-1)
    rows = -(-flat.shape[0] // (PACK_W * PART_ROWS)) * PART_ROWS
    return jnp.pad(flat, (0, rows * PACK_W - flat.shape[0])).reshape(rows, PACK_W)


def _pack(parts):
    buf = jnp.concatenate([_rows_of(p) for p in parts], axis=0)
    rows = buf.shape[0]
    total = -(-rows // PACK_ROWS) * PACK_ROWS
    return jnp.pad(buf, ((0, total - rows), (0, 0)))


def _unpack(buf, shapes):
    outs, r = [], 0
    lead = buf.shape[:-2]
    for shp in shapes:
        n = math.prod(shp)
        rows = -(-n // (PACK_W * PART_ROWS)) * PART_ROWS
        blk = buf[..., r:r + rows, :]
        if n != rows * PACK_W:
            blk = blk.reshape(lead + (rows * PACK_W,))[..., :n]
        outs.append(blk.reshape(lead + tuple(shp)))
        r += rows
    return outs


class WeightStream:
    def __init__(self, shards):
        depth = shards['ffn1_w_gate'].shape[0]
        order = [(which, l) for l in range(depth) for which in ('ffn1', 'ffn2')]
        self.count = len(order)
        gate_up = [shards[f'{which}_w_{part}'][l].astype(BF16) for which, l in order for part in ('gate', 'up')]
        downs = [shards[f'{which}_w_down'][l].astype(BF16) for which, l in order]
        self.d, self.cols = gate_up[0].shape
        assert downs[0].shape == (self.cols, PACK_W)
        self.rest_names = [n for n in W_NAMES if n in SHARD_AXIS and n not in WIDE and not n.endswith('_w_down')]
        rest = [lax.bitcast_convert_type(shards[n], BF16) if n == 'a_v_norm' else shards[n].astype(BF16)
                for n in self.rest_names]
        self.rest_shapes = [p.shape for p in rest]
        self.bufs = [jnp.concatenate(gate_up, axis=0), _pack(downs + rest)]
        self.rest_row0 = self.count * self.cols
        first = self.segments(0) + [(1, self.rest_row0, self.bufs[1].shape[0] - self.rest_row0)]
        self.gathered = all_gather_segments(self.bufs, first, "all_gather_first")

    def segments(self, i):
        return [(0, 2 * self.d * i, 2 * self.d), (1, self.cols * i, self.cols)]

    def rest(self):
        out = {}
        for n, g in zip(self.rest_names, _unpack(self.gathered[1][:, self.rest_row0:], self.rest_shapes)):
            if n == 'a_v_norm':
                g = lax.bitcast_convert_type(g, F32)
            out[n] = jnp.concatenate([g[k] for k in range(N_CHIPS)], axis=SHARD_AXIS[n])
        return out

    def ffn(self, i):
        wide, narrow = self.gathered
        r = 2 * self.d * i
        gate = jnp.concatenate([wide[k, r:r + self.d] for k in range(N_CHIPS)], axis=1)
        up = jnp.concatenate([wide[k, r + self.d:r + 2 * self.d] for k in range(N_CHIPS)], axis=1)
        down = narrow[:, self.cols * i:self.cols * (i + 1)].reshape(N_CHIPS * self.cols, PACK_W)
        return (gate, None), (up, None), (down, None)

    def hosted(self, i, stage):
        if i >= self.count:
            return None
        return hosted_gather(self.gathered, self.bufs, self.segments(i), stage)

    def absorb(self, gathered):
        self.gathered = list(gathered)


class GradSink:
    def __init__(self):
        self.queue = []
        self.totals = {}
        self.held = {}

    def add(self, i, d_wg, d_wu, d_wd, ride):
        cols = d_wg.shape[1] // N_CHIPS
        wide = jnp.stack([jnp.concatenate([d_wg[:, k * cols:(k + 1) * cols], d_wu[:, k * cols:(k + 1) * cols]],
                                          axis=0) for k in range(N_CHIPS)], axis=0)
        bufs = [wide, d_wd.reshape(N_CHIPS, cols, d_wd.shape[1])]
        if ride:
            self.queue.append([0, bufs, i])
        else:
            self.held[i] = bufs

    def step(self):
        if not self.queue:
            return None
        stage, bufs, _ = self.queue[0]
        return (swap_step, scatter_step, join_step)[stage](bufs)

    def absorb(self, results):
        entry = self.queue[0]
        stage, bufs, i = entry
        if stage == 0:
            entry[1] = [add_cores(g, t, f"ffn{i}_add_cores_{b}") for b, (g, t) in enumerate(zip(bufs, results))]
        elif stage == 1:
            entry[1] = [add_chips(s, f"ffn{i}_add_chips_{b}") for b, s in enumerate(results)]
        else:
            self.totals[i] = list(results)
            self.queue.pop(0)
        entry[0] = stage + 1

    def reduce(self, grads, shard_shapes):
        assert not self.queue
        narrow = [n for n in W_NAMES if n in SHARD_AXIS and n in grads]
        repl = [n for n in W_NAMES if n not in SHARD_AXIS]
        full = {n: (jnp.stack(g, axis=0) if isinstance(g, list) else g) for n, g in grads.items()}
        blocks = []
        for k in range(N_CHIPS):
            parts = [jnp.split(full[n], N_CHIPS, axis=SHARD_AXIS[n])[k] for n in narrow]
            parts += [full[n] for n in repl]
            blocks.append(_pack(parts))
        held = sorted(self.held)
        gs = [b for i in held for b in self.held[i]] + [jnp.stack(blocks, axis=0)]
        theirs = _exchange(swap_step(gs), gs, "grads_swap_halves")
        sums = [add_cores(g, t, f"grads_add_cores_{b}") for b, (g, t) in enumerate(zip(gs, theirs))]
        slots = _exchange(scatter_step(sums), sums, "grads_scatter")
        halves = [add_chips(s, f"grads_add_chips_{b}") for b, s in enumerate(slots)]
        totals = _exchange(join_step(halves), halves, "grads_join_halves")
        for n, i in enumerate(held):
            self.totals[i] = totals[2 * n:2 * n + 2]
        shapes = [shard_shapes[n] for n in narrow] + [full[n].shape for n in repl]
        out = dict(zip(narrow + repl, _unpack(totals[-1], shapes)))
        d = self.totals[0][0].shape[0] // 2
        for w, which in enumerate(('ffn1', 'ffn2')):
            mine = [self.totals[i] for i in sorted(self.totals) if i % 2 == w]
            out[f'{which}_w_gate'] = jnp.stack([t[0][:d] for t in mine], axis=0)
            out[f'{which}_w_up'] = jnp.stack([t[0][d:] for t in mine], axis=0)
            out[f'{which}_w_down'] = jnp.stack([t[1] for t in mine], axis=0)
        return out


def _ffn_fwd(x, h, stream, i, next_gain, tag):
    wg, wu, wd = stream.ffn(i)
    ici = stream.hosted(i + 1, 0)
    if ici is None:
        g, u, a = ffn_up(h, wg, wu, tag + "_up")
        out = mm([(a, 0, wd, a.shape[1], 0)], res=x, scale=0.5, norm_next=next_gain, name=tag + "_down")
    else:
        (g, u, a), gathered = ffn_up(h, wg, wu, tag + "_up", hosted=ici)
        stream.absorb(gathered)
        out, gathered = mm([(a, 0, wd, a.shape[1], 0)], res=x, scale=0.5, norm_next=next_gain,
                           name=tag + "_down", hosted=stream.hosted(i + 1, 1))
        stream.absorb(gathered)
    xo, h_next = (out, None) if next_gain is None else out
    return xo, h_next, (x, h, g, u, a, wg, wu, wd)


def _carrying(sink, kernel, *args, **kwargs):
    step = sink.step()
    if step is None:
        return kernel(*args, **kwargs)
    out, results = kernel(*args, hosted=step, **kwargs)
    sink.absorb(results)
    return out


def _ffn_bwd(dxo, saved, gain, sink, i, tag):
    x, h, g, u, a, wg, wu, wd = saved
    Fd = a.shape[1]
    dxo, dxo_b = dxo
    dgp, du, d_wd = _carrying(sink, ffn_down_bwd, dxo_b, wd, g, u, a, tag + "_down_bwd")
    d_wg, d_wu = _carrying(sink, mm_tn_pair, h, dgp, du, name=tag + "_dwgu")
    sink.add(i, d_wg, d_wu, d_wd, ride=i > 0)
    return _carrying(sink, mm, [(dgp, 0, wg, Fd, 0), (du, 0, wu, Fd, 0)], tb=True, res=dxo, norm_bwd=(x, gain),
                     name=tag + "_up_bwd")


def local_step(x, mem, W, stream, sink, target):
    D = x.shape[1]
    depth = W['ffn1_norm'].shape[0]
    n_a = W['a_w_in'].shape[0]
    G = {}
    mem_h = rms_fwd(mem, W['mem_norm'], "mem_norm")
    b_mats = jnp.broadcast_to(W['a_b_spatial'][..., None], W['a_b_spatial'].shape + (GM_CHUNK,))
    w_kv = (W['w_kv'], None)

    def mixer_w(l):
        if l < n_a:
            return (W['a_w_in'], l), (W['a_w_out'], l), 2 * GM_W
        return (W['b_w_in'], l - n_a), (W['b_w_out'], l - n_a), GM_W

    saved = []
    kv = kvn = x_kv = None
    h = rms_fwd(x, W['ffn1_norm'][0], "l0_ffn1_norm")
    for l in range(depth):
        if l == n_a:
            x_kv = x
            kvn = rms_fwd(x, W['kv_norm'], "kv_norm")
            kv = mm([(kvn, 0, w_kv, D, 0)], out_dtype=BF16, name="kv_proj")
        x, hm, s1 = _ffn_fwd(x, h, stream, 2 * l, W['mix_norm'][l], f"l{l}_ffn1")
        mem_kv = mm([(mem_h, 0, (W['w_mem_kv'], l), D, 0)], name=f"l{l}_mem_kv")
        w_in, w_out, tok_w = mixer_w(l)
        proj = mm([(hm, 0, w_in, D, 0)], name=f"l{l}_mix_in")
        if l < n_a:
            y_tok = gmlp_fwd(proj, W['a_v_norm'][l], W['a_w_spatial'][l], b_mats[l], f"l{l}_gmlp")
        else:
            y_tok = sb_fwd(proj, kv, f"l{l}_sb")
        q_cb = tok_w // MEM_W
        y_mem = memattn_fwd(proj, q_cb, mem_kv, f"l{l}_memattn")
        x_mid = x
        x, h2 = mm([(y_tok, 0, w_out, GM_W, 0), (y_mem, 0, w_out, MEM_W, GM_W // MEM_W)], res=x,
                   norm_next=W['ffn2_norm'][l], name=f"l{l}_mix_out")
        sm = (x_mid, hm, mem_kv, proj, y_tok, y_mem, q_cb)
        next_gain = W['ffn1_norm'][l + 1] if l + 1 < depth else None
        x, h, s2 = _ffn_fwd(x, h2, stream, 2 * l + 1, next_gain, f"l{l}_ffn2")
        saved.append((s1, sm, s2))

    loss, dx, d_final = final_loss(x, W['final_norm'], target, "loss_head")
    G['final_norm'] = d_final.reshape(-1)

    per_layer = {n: [None] * depth for n in ['ffn1_norm', 'mix_norm', 'ffn2_norm', 'w_mem_kv']}
    per_a = {n: [None] * n_a for n in ['a_w_in', 'a_v_norm', 'a_w_spatial', 'a_b_spatial', 'a_w_out']}
    per_b = {n: [None] * (depth - n_a) for n in ['b_w_in', 'b_w_out']}
    d_mem_h = None
    dk = dv = None
    for l in reversed(range(depth)):
        s1, sm, s2 = saved[l]
        dx, dg = _ffn_bwd(dx, s2, W['ffn2_norm'][l], sink, 2 * l + 1, f"l{l}_ffn2")
        per_layer['ffn2_norm'][l] = dg.reshape(-1)

        x_mid, hm, mem_kv, proj, y_tok, y_mem, q_cb = sm
        is_a = l < n_a
        w_in, w_out, tok_w = mixer_w(l)
        dy = mm([(dx[1], 0, w_out, D, 0)], tb=True, name=f"l{l}_mix_out_bwd")
        d_w_out = jnp.concatenate([mm_tn(y_tok, dx[1], name=f"l{l}_dwout_tok"),
                                   mm_tn(y_mem, dx[1], name=f"l{l}_dwout_mem")], axis=0)
        dq_mem, d_mem_kv = memattn_bwd(proj, q_cb, mem_kv, dy, GM_W // MEM_W, f"l{l}_memattn_bwd")
        if is_a:
            d_tok, d_ws, d_bs, d_vg = gmlp_bwd(proj, dy, W['a_v_norm'][l], W['a_w_spatial'][l], b_mats[l],
                                               f"l{l}_gmlp_bwd")
            per_a['a_w_spatial'][l], per_a['a_b_spatial'][l] = d_ws, d_bs[:, :, 0]
            per_a['a_v_norm'][l], per_a['a_w_out'][l] = d_vg.reshape(-1), d_w_out
        else:
            d_tok, dk, dv = sb_bwd(proj, kv, dy, y_tok, dk, dv, f"l{l}_sb_bwd")
            per_b['b_w_out'][l - n_a] = d_w_out
        d_w_in = jnp.concatenate([mm_tn(hm, d_tok, name=f"l{l}_dwin_tok"),
                                  mm_tn(hm, dq_mem, name=f"l{l}_dwin_mem")], axis=1)
        (per_a['a_w_in'] if is_a else per_b['b_w_in'])[l if is_a else l - n_a] = d_w_in
        dx, dg = mm([(d_tok, 0, w_in, tok_w, 0), (dq_mem, 0, w_in, MEM_W, tok_w // MEM_W)], tb=True, res=dx[0],
                    norm_bwd=(x_mid, W['mix_norm'][l]), name=f"l{l}_mix_in_bwd")
        per_layer['mix_norm'][l] = dg.reshape(-1)
        per_layer['w_mem_kv'][l] = mm_tn(mem_h, d_mem_kv, name=f"l{l}_dw_mem_kv")
        d_mem_h = mm([(d_mem_kv, 0, (W['w_mem_kv'], l), 2 * MEM_W, 0)], tb=True, res=d_mem_h,
                     name=f"l{l}_mem_kv_bwd")

        dx, dg = _ffn_bwd(dx, s1, W['ffn1_norm'][l], sink, 2 * l, f"l{l}_ffn1")
        per_layer['ffn1_norm'][l] = dg.reshape(-1)
        if l == n_a:
            G['w_kv'] = jnp.concatenate([mm_tn(kvn, dk, name="dw_k"), mm_tn(kvn, dv, name="dw_v")], axis=1)
            dx, dg = mm([(dk, 0, w_kv, GM_W, 0), (dv, 0, w_kv, GM_W, 1)], tb=True, res=dx[0],
                        norm_bwd=(x_kv, W['kv_norm']), name="kv_proj_bwd")
            G['kv_norm'] = dg.reshape(-1)

    G['mem_norm'] = rms_gain_grad(mem, d_mem_h, "mem_norm_bwd").reshape(-1)
    for d in (per_layer, per_a, per_b):
        G.update(d)
    return loss, dx[0], G


def kernel(x, mem, ffn1_norm, ffn1_w_gate, ffn1_w_up, ffn1_w_down, mix_norm, ffn2_norm, ffn2_w_gate, ffn2_w_up, ffn2_w_down, mem_norm, w_mem_kv, a_w_in, a_v_norm, a_w_spatial, a_b_spatial, a_w_out, kv_norm, w_kv, b_w_in, b_w_out, final_norm, loss_target, m_ffn1_norm, m_ffn1_w_gate, m_ffn1_w_up, m_ffn1_w_down, m_mix_norm, m_ffn2_norm, m_ffn2_w_gate, m_ffn2_w_up, m_ffn2_w_down, m_mem_norm, m_w_mem_kv, m_a_w_in, m_a_v_norm, m_a_w_spatial, m_a_b_spatial, m_a_w_out, m_kv_norm, m_w_kv, m_b_w_in, m_b_w_out, m_final_norm, v_ffn1_norm, v_ffn1_w_gate, v_ffn1_w_up, v_ffn1_w_down, v_mix_norm, v_ffn2_norm, v_ffn2_w_gate, v_ffn2_w_up, v_ffn2_w_down, v_mem_norm, v_w_mem_kv, v_a_w_in, v_a_v_norm, v_a_w_spatial, v_a_b_spatial, v_a_w_out, v_kv_norm, v_w_kv, v_b_w_in, v_b_w_out, v_final_norm):
    weights = dict(zip(W_NAMES, [ffn1_norm, ffn1_w_gate, ffn1_w_up, ffn1_w_down, mix_norm, ffn2_norm, ffn2_w_gate,
                                 ffn2_w_up, ffn2_w_down, mem_norm, w_mem_kv, a_w_in, a_v_norm, a_w_spatial,
                                 a_b_spatial, a_w_out, kv_norm, w_kv, b_w_in, b_w_out, final_norm]))
    m_in = dict(zip(W_NAMES, [m_ffn1_norm, m_ffn1_w_gate, m_ffn1_w_up, m_ffn1_w_down, m_mix_norm, m_ffn2_norm,
                              m_ffn2_w_gate, m_ffn2_w_up, m_ffn2_w_down, m_mem_norm, m_w_mem_kv, m_a_w_in,
                              m_a_v_norm, m_a_w_spatial, m_a_b_spatial, m_a_w_out, m_kv_norm, m_w_kv, m_b_w_in,
                              m_b_w_out, m_final_norm]))
    v_in = dict(zip(W_NAMES, [v_ffn1_norm, v_ffn1_w_gate, v_ffn1_w_up, v_ffn1_w_down, v_mix_norm, v_ffn2_norm,
                              v_ffn2_w_gate, v_ffn2_w_up, v_ffn2_w_down, v_mem_norm, v_w_mem_kv, v_a_w_in,
                              v_a_v_norm, v_a_w_spatial, v_a_b_spatial, v_a_w_out, v_kv_norm, v_w_kv, v_b_w_in,
                              v_b_w_out, v_final_norm]))

    stream = WeightStream({n: weights[n] for n in SHARD_AXIS})
    W = {n: weights[n] for n in W_NAMES if n not in SHARD_AXIS}
    W.update(stream.rest())
    sink = GradSink()
    loss, dx, grads = local_step(x[0], mem[0], W, stream, sink, loss_target[0])
    total = sink.reduce(grads, {n: weights[n].shape for n in SHARD_AXIS})
    loss = lax.psum(loss[0, 0], ("x", "y", "c"))

    deltas, new_m, new_v = {}, {}, {}
    for n in W_NAMES:
        deltas[n], new_m[n], new_v[n] = adamw(weights[n], total[n], m_in[n], v_in[n], "adamw_" + n)
    return (loss, dx[None], *[total[n] for n in W_NAMES], *[deltas[n] for n in W_NAMES],
            *[new_m[n] for n in W_NAMES], *[new_v[n] for n in W_NAMES])
```

```python
import functools
import math

import jax
import jax.numpy as jnp
from jax import lax
from jax.experimental import pallas as pl
from jax.experimental.pallas import tpu as pltpu

F32 = jnp.float32
BF16 = jnp.bfloat16
EPS = 1e-6
LANES = 128
ROW_TILE = 512
COL_TILE = 1408
PACK_ROWS = 256
PART_ROWS = 16
VMEM_LIMIT = 56 * 1024 * 1024

W_NAMES = ['ffn1_norm', 'ffn1_w_gate', 'ffn1_w_up', 'ffn1_w_down', 'mix_norm', 'ffn2_norm', 'ffn2_w_gate',
           'ffn2_w_up', 'ffn2_w_down', 'mem_norm', 'w_mem_kv', 'a_w_in', 'a_v_norm', 'a_w_spatial',
           'a_b_spatial', 'a_w_out', 'kv_norm', 'w_kv', 'b_w_in', 'b_w_out', 'final_norm']
SHARD_AXIS = {'ffn1_w_gate': 2, 'ffn1_w_up': 2, 'ffn1_w_down': 1, 'ffn2_w_gate': 2, 'ffn2_w_up': 2,
              'ffn2_w_down': 1, 'w_mem_kv': 1, 'a_w_in': 2, 'a_v_norm': 1, 'a_w_out': 1, 'w_kv': 1,
              'b_w_in': 1, 'b_w_out': 1}
N_CHIPS = 4
PACK_W = 1024

MEM_HEADS = 4
MEM_W = 256
HEAD_DIM = 64
GM_W = 768
GM_GROUPS = 6
GM_CHUNK = 128
CHUNK = 64
SB_BLOCK = 128
SB_Q_CHUNK = 1024
SB_DEAD = -104.0
SB_PAIRS = 1
SB_COLS = SB_PAIRS * LANES

ADAM_LR, ADAM_B1, ADAM_B2, ADAM_EPS, ADAM_WD, ADAM_STEP = 0.001, 0.9, 0.999, 1e-08, 0.01, 10

MESH = pl.DeviceIdType.MESH


def _cparams(sem=None):
    return pltpu.CompilerParams(dimension_semantics=sem, vmem_limit_bytes=VMEM_LIMIT)


class Hosted:
    def __init__(self, arrays, n_out, sems, make, fresh=()):
        self.arrays, self.n_out, self.sems, self.make, self.fresh = arrays, n_out, sems, make, list(fresh)


def _call(body, *, grid, in_specs, out_specs, out_shape, ins, semantics, name, hosted=None):
    if hosted is None:
        return pl.pallas_call(body, grid=grid, in_specs=in_specs, out_specs=out_specs, out_shape=out_shape,
                              compiler_params=_cparams(semantics), name=name)(*ins)
    n_in, n_out = len(ins), len(out_shape)
    n_h, n_alias = len(hosted.arrays), hosted.n_out
    n_ho = n_alias + len(hosted.fresh)

    def hosting_body(*refs):
        base_in, h_in = refs[:n_in], refs[n_in:n_in + n_h]
        base_out = refs[n_in + n_h:n_in + n_h + n_out]
        h_out = refs[n_in + n_h + n_out:n_in + n_h + n_out + n_ho]
        sems = refs[n_in + n_h + n_out + n_ho:]
        copies, waits = hosted.make(h_in, h_out, *sems)
        first = last = None
        for axis, extent in enumerate(grid):
            at_start, at_end = pl.program_id(axis) == 0, pl.program_id(axis) == extent - 1
            first = at_start if first is None else jnp.logical_and(first, at_start)
            last = at_end if last is None else jnp.logical_and(last, at_end)

        @pl.when(first)
        def _():
            for cp in copies:
                cp.start()

        body(*base_in, *base_out)

        @pl.when(last)
        def _():
            for wait in waits:
                wait()

    any_spec = pl.BlockSpec(memory_space=pl.ANY)
    outs = pl.pallas_call(
        hosting_body, grid=grid, in_specs=list(in_specs) + [any_spec] * n_h,
        out_specs=list(out_specs) + [any_spec] * n_ho,
        out_shape=(list(out_shape) + [jax.ShapeDtypeStruct(a.shape, a.dtype) for a in hosted.arrays[:n_alias]]
                   + hosted.fresh),
        scratch_shapes=[pltpu.SemaphoreType.DMA((n,)) for n in hosted.sems],
        input_output_aliases={n_in + k: n_out + k for k in range(n_alias)},
        compiler_params=_cparams(("arbitrary",) * len(grid)), name=name)(*ins, *hosted.arrays)
    return outs[:n_out], outs[n_out:]


def _tile(n, target, mult=LANES):
    best = None
    for t in range(mult, min(n, target) + 1, mult):
        if n % t == 0:
            best = t
    return best if best is not None else n


def _dot(a, b):
    return jnp.dot(a, b, preferred_element_type=F32)


def _dot_nt(a, b):
    return lax.dot_general(a, b, (((1,), (1,)), ((), ())), preferred_element_type=F32)


def _dot_tn(a, b):
    return lax.dot_general(a, b, (((0,), (0,)), ((), ())), preferred_element_type=F32)


def _bf(v):
    return v.astype(BF16)


def rms_fwd(x, gain, name):
    S, D = x.shape
    tm = _tile(S, ROW_TILE, 8)

    def body(x_ref, g_ref, o_ref):
        xf = x_ref[...]
        r = lax.rsqrt(jnp.mean(xf * xf, axis=-1, keepdims=True) + EPS)
        o_ref[...] = ((xf * r) * g_ref[...]).astype(o_ref.dtype)

    return pl.pallas_call(
        body, grid=(S // tm,),
        in_specs=[pl.BlockSpec((tm, D), lambda i: (i, 0)), pl.BlockSpec((1, D), lambda i: (0, 0))],
        out_specs=pl.BlockSpec((tm, D), lambda i: (i, 0)),
        out_shape=jax.ShapeDtypeStruct((S, D), BF16),
        compiler_params=_cparams(("parallel",)), name=name)(x, gain.reshape(1, D))


def rms_gain_grad(x, dh, name):
    S, D = x.shape
    tm = _tile(S, ROW_TILE, 8)

    def body(x_ref, dh_ref, dg_ref):
        xf = x_ref[...]
        r = lax.rsqrt(jnp.mean(xf * xf, axis=-1, keepdims=True) + EPS)

        @pl.when(pl.program_id(0) == 0)
        def _():
            dg_ref[...] = jnp.zeros_like(dg_ref)

        dg_ref[...] += jnp.sum(dh_ref[...] * (xf * r), axis=0, keepdims=True)

    row = pl.BlockSpec((tm, D), lambda i: (i, 0))
    return pl.pallas_call(
        body, grid=(S // tm,), in_specs=[row, row], out_specs=pl.BlockSpec((1, D), lambda i: (0, 0)),
        out_shape=jax.ShapeDtypeStruct((1, D), F32),
        compiler_params=_cparams(("arbitrary",)), name=name)(x, dh)


def final_loss(x, gain, target, name):
    S, D = x.shape
    tm = _tile(S, ROW_TILE, 16)

    def body(x_ref, g_ref, t_ref, loss_ref, dx_ref, dxb_ref, dg_ref):
        xf = x_ref[...]
        r = lax.rsqrt(jnp.mean(xf * xf, axis=-1, keepdims=True) + EPS)
        xhat = xf * r
        g = g_ref[...]
        diff = xhat * g - t_ref[...]
        dy = diff * (1.0 / D)
        dxhat = dy * g
        m = jnp.mean(dxhat * xhat, axis=-1, keepdims=True)
        dx = r * (dxhat - xhat * m)
        dx_ref[...] = dx
        dxb_ref[...] = _bf(dx)

        @pl.when(pl.program_id(0) == 0)
        def _():
            dg_ref[...] = jnp.zeros_like(dg_ref)
            loss_ref[...] = jnp.zeros_like(loss_ref)

        dg_ref[...] += jnp.sum(dy * xhat, axis=0, keepdims=True)
        per_tok = jnp.sum(diff * diff, axis=-1, keepdims=True) * (0.5 / D)
        loss_ref[...] += jnp.sum(per_tok, axis=0, keepdims=True)

    row = pl.BlockSpec((tm, D), lambda i: (i, 0))
    vec = pl.BlockSpec((1, D), lambda i: (0, 0))
    one = pl.BlockSpec((1, 1), lambda i: (0, 0))
    loss, dx, dxb, dg = pl.pallas_call(
        body, grid=(S // tm,), in_specs=[row, vec, row], out_specs=[one, row, row, vec],
        out_shape=[jax.ShapeDtypeStruct((1, 1), F32), jax.ShapeDtypeStruct((S, D), F32),
                   jax.ShapeDtypeStruct((S, D), BF16), jax.ShapeDtypeStruct((1, D), F32)],
        compiler_params=_cparams(("arbitrary",)), name=name)(x, gain.reshape(1, D), target)
    return loss, (dx, dxb), dg


def _wspec(w, blk, idx):
    _, l = w
    if l is None:
        return pl.BlockSpec(blk, idx)
    return pl.BlockSpec((None,) + blk, lambda *g: (l,) + idx(*g))


def mm(pairs, *, tb=False, res=None, scale=1.0, out_dtype=F32, norm_bwd=None, norm_next=None, hosted=None, name):
    M = pairs[0][0].shape[0]
    N = pairs[0][2][0].shape[-2 if tb else -1]
    tm = _tile(M, ROW_TILE, 16)
    tn = N if (norm_bwd is not None or norm_next is not None) else _tile(N, COL_TILE)
    n_p = len(pairs)
    has_res = res is not None
    n_in = 2 * n_p + has_res + (2 if norm_bwd is not None else 0) + (norm_next is not None)

    def body(*refs):
        acc = None
        for p in range(n_p):
            a = _bf(refs[2 * p][...])
            b = _bf(refs[2 * p + 1][...])
            d = _dot_nt(a, b) if tb else _dot(a, b)
            acc = d if acc is None else acc + d
        if scale != 1.0:
            acc = acc * scale
        if norm_bwd is None:
            if has_res:
                acc = acc + refs[2 * n_p][...]
            refs[n_in][...] = acc.astype(refs[n_in].dtype)
            if norm_next is not None:
                r = lax.rsqrt(jnp.mean(acc * acc, axis=-1, keepdims=True) + EPS)
                refs[n_in + 1][...] = _bf((acc * r) * refs[n_in - 1][...])
            return
        x_ref, g_ref = refs[n_in - 2], refs[n_in - 1]
        dx_ref, dxb_ref, dg_ref = refs[n_in:]
        xf = x_ref[...]
        r = lax.rsqrt(jnp.mean(xf * xf, axis=-1, keepdims=True) + EPS)
        xhat = xf * r
        dxhat = acc * g_ref[...]
        m = jnp.mean(dxhat * xhat, axis=-1, keepdims=True)
        dx = r * (dxhat - xhat * m)
        if has_res:
            dx = dx + refs[2 * n_p][...]
        dx_ref[...] = dx
        dxb_ref[...] = _bf(dx)

        @pl.when(pl.program_id(1) == 0)
        def _():
            dg_ref[...] = jnp.zeros_like(dg_ref)

        dg_ref[...] += jnp.sum(acc * xhat, axis=0, keepdims=True)

    ins, in_specs = [], []
    for a, cb, w, K, kb in pairs:
        ins += [a, w[0]]
        in_specs.append(pl.BlockSpec((tm, K), functools.partial(lambda j, i, cb: (i, cb), cb=cb)))
        if tb:
            in_specs.append(_wspec(w, (tn, K), functools.partial(lambda j, i, kb: (j, kb), kb=kb)))
        else:
            in_specs.append(_wspec(w, (K, tn), functools.partial(lambda j, i, kb: (kb, j), kb=kb)))
    tile = pl.BlockSpec((tm, tn), lambda j, i: (i, j))
    if has_res:
        ins.append(res)
        in_specs.append(tile)
    if norm_bwd is None:
        out_shape = [jax.ShapeDtypeStruct((M, N), out_dtype)]
        if norm_next is not None:
            ins.append(norm_next.reshape(1, N))
            in_specs.append(pl.BlockSpec((1, N), lambda j, i: (0, 0)))
            out_shape.append(jax.ShapeDtypeStruct((M, N), BF16))
        out = _call(body, grid=(N // tn, M // tm), in_specs=in_specs, out_specs=[tile] * len(out_shape),
                    out_shape=out_shape, ins=ins, semantics=("parallel", "parallel"), name=name, hosted=hosted)
        base = out if hosted is None else out[0]
        value = base[0] if norm_next is None else (base[0], base[1])
        return value if hosted is None else (value, out[1])
    x, gain = norm_bwd
    vec = pl.BlockSpec((1, N), lambda j, i: (0, 0))
    out = _call(
        body, grid=(1, M // tm), in_specs=in_specs + [tile, vec], out_specs=[tile, tile, vec],
        out_shape=[jax.ShapeDtypeStruct((M, N), F32), jax.ShapeDtypeStruct((M, N), BF16),
                   jax.ShapeDtypeStruct((1, N), F32)], ins=ins + [x, gain.reshape(1, N)],
        semantics=("arbitrary", "arbitrary"), name=name, hosted=hosted)
    dx, dxb, dg = out if hosted is None else out[0]
    return ((dx, dxb), dg) if hosted is None else (((dx, dxb), dg), out[1])


def mm_tn(a, b, *, hosted=None, name):
    S, Ka = a.shape
    Nb = b.shape[1]
    ts = _tile(S, 2 * ROW_TILE, 8)
    tk = _tile(Ka, COL_TILE)
    tn = _tile(Nb, COL_TILE)

    def body(a_ref, b_ref, o_ref):
        s = pl.program_id(2)

        @pl.when(s == 0)
        def _():
            o_ref[...] = jnp.zeros_like(o_ref)

        o_ref[...] += _dot_tn(_bf(a_ref[...]), _bf(b_ref[...]))

    out = _call(
        body, grid=(Ka // tk, Nb // tn, S // ts),
        in_specs=[pl.BlockSpec((ts, tk), lambda k, n, s: (s, k)),
                  pl.BlockSpec((ts, tn), lambda k, n, s: (s, n))],
        out_specs=[pl.BlockSpec((tk, tn), lambda k, n, s: (k, n))],
        out_shape=[jax.ShapeDtypeStruct((Ka, Nb), F32)], ins=[a, b],
        semantics=("parallel", "parallel", "arbitrary"), name=name, hosted=hosted)
    return out[0] if hosted is None else (out[0][0], out[1])


def mm_tn_pair(a, b1, b2, *, hosted=None, name):
    S, Ka = a.shape
    Nb = b1.shape[1]
    ts = _tile(S, 2 * ROW_TILE, 8)
    tk = _tile(Ka, COL_TILE)
    tn = _tile(Nb, COL_TILE)

    def body(a_ref, b1_ref, b2_ref, o1_ref, o2_ref):
        @pl.when(pl.program_id(2) == 0)
        def _():
            o1_ref[...] = jnp.zeros_like(o1_ref)
            o2_ref[...] = jnp.zeros_like(o2_ref)

        av = _bf(a_ref[...])
        o1_ref[...] += _dot_tn(av, _bf(b1_ref[...]))
        o2_ref[...] += _dot_tn(av, _bf(b2_ref[...]))

    b_spec = pl.BlockSpec((ts, tn), lambda k, n, s: (s, n))
    o_spec = pl.BlockSpec((tk, tn), lambda k, n, s: (k, n))
    sh = jax.ShapeDtypeStruct((Ka, Nb), F32)
    out = _call(
        body, grid=(Ka // tk, Nb // tn, S // ts),
        in_specs=[pl.BlockSpec((ts, tk), lambda k, n, s: (s, k)), b_spec, b_spec],
        out_specs=[o_spec, o_spec], out_shape=[sh, sh], ins=[a, b1, b2],
        semantics=("parallel", "parallel", "arbitrary"), name=name, hosted=hosted)
    return tuple(out) if hosted is None else (tuple(out[0]), out[1])


def _sigmoid(x):
    return 0.5 * jnp.tanh(0.5 * x) + 0.5


def ffn_up(h, wg, wu, name, hosted=None):
    S, D = h.shape
    Fd = wg[0].shape[-1]
    tm = _tile(S, ROW_TILE, 8)
    tn = _tile(Fd, COL_TILE)

    def body(h_ref, wg_ref, wu_ref, g_ref, u_ref, a_ref):
        hv = h_ref[...]
        g = _dot(hv, wg_ref[...])
        u = _dot(hv, wu_ref[...])
        g_ref[...] = _bf(g)
        u_ref[...] = _bf(u)
        a_ref[...] = _bf(g * _sigmoid(g) * u)

    blk = pl.BlockSpec((tm, tn), lambda j, i: (i, j))
    sh = jax.ShapeDtypeStruct((S, Fd), BF16)
    return _call(
        body, grid=(Fd // tn, S // tm),
        in_specs=[pl.BlockSpec((tm, D), lambda j, i: (i, 0)), _wspec(wg, (D, tn), lambda j, i: (0, j)),
                  _wspec(wu, (D, tn), lambda j, i: (0, j))],
        out_specs=[blk, blk, blk], out_shape=[sh, sh, sh], ins=[h, wg[0], wu[0]],
        semantics=("parallel", "parallel"), name=name, hosted=hosted)


def ffn_down_bwd(dxo, wd, g, u, a, name, hosted=None):
    S, D = dxo.shape
    Fd = wd[0].shape[-2]
    tm = _tile(S, ROW_TILE, 8)
    tn = _tile(Fd, COL_TILE)

    def body(dx_ref, wd_ref, g_ref, u_ref, a_ref, dg_ref, du_ref, dwd_ref):
        dx = _bf(dx_ref[...])
        da = _dot_nt(dx, wd_ref[...]) * 0.5
        g = g_ref[...].astype(F32)
        u = u_ref[...].astype(F32)
        sg = _sigmoid(g)
        dg_ref[...] = _bf(da * u * (sg * (1.0 + g * (1.0 - sg))))
        du_ref[...] = _bf(da * (g * sg))

        @pl.when(pl.program_id(1) == 0)
        def _():
            dwd_ref[...] = jnp.zeros_like(dwd_ref)

        dwd_ref[...] += _dot_tn(a_ref[...], dx) * 0.5

    blk = pl.BlockSpec((tm, tn), lambda j, i: (i, j))
    sh = jax.ShapeDtypeStruct((S, Fd), BF16)
    return _call(
        body, grid=(Fd // tn, S // tm),
        in_specs=[pl.BlockSpec((tm, D), lambda j, i: (i, 0)), _wspec(wd, (tn, D), lambda j, i: (j, 0)),
                  blk, blk, blk],
        out_specs=[blk, blk, pl.BlockSpec((tn, D), lambda j, i: (j, 0))],
        out_shape=[sh, sh, jax.ShapeDtypeStruct((Fd, D), F32)], ins=[dxo, wd[0], g, u, a],
        semantics=("parallel", "arbitrary"), name=name, hosted=hosted)


_GELU_C = math.sqrt(2.0 / math.pi)


def _gelu(p):
    return 0.5 * p * (1.0 + jnp.tanh(_GELU_C * (p + 0.044715 * (p * p * p))))


def _gelu_grad(p):
    th = jnp.tanh(_GELU_C * (p + 0.044715 * (p * p * p)))
    return 0.5 * (1.0 + th) + 0.5 * p * (1.0 - th * th) * (_GELU_C * (1.0 + 3.0 * 0.044715 * (p * p)))


def _chunk_mask():
    t = lax.broadcasted_iota(jnp.int32, (GM_CHUNK, GM_CHUNK), 0)
    s = lax.broadcasted_iota(jnp.int32, (GM_CHUNK, GM_CHUNK), 1)
    return (s // CHUNK) <= (t // CHUNK)


def gmlp_fwd(proj, v_gain, w_s, b_mat, name):
    S = proj.shape[0]
    tw = _tile(S, ROW_TILE, GM_CHUNK)
    n_win = tw // GM_CHUNK

    def body(p_ref, gain_ref, w_ref, b_ref, y_ref):
        mask = _chunk_mask()
        v = _gelu(p_ref[:, GM_W:])
        r = lax.rsqrt(jnp.mean(v * v, axis=-1, keepdims=True) + EPS)
        vn = _bf(v * r * gain_ref[...])
        for g in range(GM_GROUPS):
            wm = _bf(jnp.where(mask, w_ref[g], 0.0))
            cs = slice(g * GM_CHUNK, (g + 1) * GM_CHUNK)
            for w in range(n_win):
                rs = slice(w * GM_CHUNK, (w + 1) * GM_CHUNK)
                mixed = _dot(wm, vn[rs, cs]) + b_ref[g]
                y_ref[rs, cs] = _bf(_gelu(p_ref[rs, cs]) * mixed)

    return pl.pallas_call(
        body, grid=(S // tw,),
        in_specs=[pl.BlockSpec((tw, 2 * GM_W), lambda i: (i, 0)), pl.BlockSpec((1, GM_W), lambda i: (0, 0)),
                  pl.BlockSpec((GM_GROUPS, GM_CHUNK, GM_CHUNK), lambda i: (0, 0, 0)),
                  pl.BlockSpec((GM_GROUPS, GM_CHUNK, GM_CHUNK), lambda i: (0, 0, 0))],
        out_specs=pl.BlockSpec((tw, GM_W), lambda i: (i, 0)),
        out_shape=jax.ShapeDtypeStruct((S, GM_W), BF16),
        compiler_params=_cparams(("parallel",)), name=name)(proj, v_gain.reshape(1, GM_W), w_s, b_mat)


def gmlp_bwd(proj, dy, v_gain, w_s, b_mat, name):
    S = proj.shape[0]
    tw = _tile(S, ROW_TILE, GM_CHUNK)
    n_win = tw // GM_CHUNK

    def body(p_ref, dy_ref, gain_ref, w_ref, b_ref, dp_ref, dw_ref, db_ref, dgain_ref, dvn_ref):
        step = pl.program_id(0)

        @pl.when(step == 0)
        def _():
            dw_ref[...] = jnp.zeros_like(dw_ref)
            db_ref[...] = jnp.zeros_like(db_ref)
            dgain_ref[...] = jnp.zeros_like(dgain_ref)

        mask = _chunk_mask()
        pv = p_ref[:, GM_W:]
        v = _gelu(pv)
        r = lax.rsqrt(jnp.mean(v * v, axis=-1, keepdims=True) + EPS)
        vhat = v * r
        gain = gain_ref[...]
        vn = _bf(vhat * gain)
        for g in range(GM_GROUPS):
            wm = _bf(jnp.where(mask, w_ref[g], 0.0))
            cs = slice(g * GM_CHUNK, (g + 1) * GM_CHUNK)
            dw_acc = jnp.zeros((GM_CHUNK, GM_CHUNK), F32)
            db_acc = jnp.zeros((GM_CHUNK, GM_CHUNK), F32)
            for w in range(n_win):
                rs = slice(w * GM_CHUNK, (w + 1) * GM_CHUNK)
                pu = p_ref[rs, cs]
                u = _gelu(pu)
                vn_blk = vn[rs, cs]
                mixed = _dot(wm, vn_blk) + b_ref[g]
                dyb = dy_ref[rs, cs]
                dp_ref[rs, cs] = _bf(dyb * mixed * _gelu_grad(pu))
                dmix = dyb * u
                db_acc = db_acc + dmix
                dmix_b = _bf(dmix)
                dw_acc = dw_acc + _dot_nt(dmix_b, vn_blk)
                dvn_ref[rs, cs] = _dot_tn(wm, dmix_b)
            dw_ref[g] += jnp.where(mask, dw_acc, 0.0)
            db_ref[g] += jnp.broadcast_to(jnp.sum(db_acc, axis=-1, keepdims=True), (GM_CHUNK, GM_CHUNK))
        dvn = dvn_ref[...]
        dgain_ref[...] += jnp.sum(dvn * vhat, axis=0, keepdims=True)
        dvhat = dvn * gain
        m = jnp.mean(dvhat * vhat, axis=-1, keepdims=True)
        dv = r * (dvhat - vhat * m)
        dp_ref[:, GM_W:] = _bf(dv * _gelu_grad(pv))

    sq = pl.BlockSpec((GM_GROUPS, GM_CHUNK, GM_CHUNK), lambda i: (0, 0, 0))
    sq_sh = jax.ShapeDtypeStruct((GM_GROUPS, GM_CHUNK, GM_CHUNK), F32)
    return pl.pallas_call(
        body, grid=(S // tw,),
        in_specs=[pl.BlockSpec((tw, 2 * GM_W), lambda i: (i, 0)), pl.BlockSpec((tw, GM_W), lambda i: (i, 0)),
                  pl.BlockSpec((1, GM_W), lambda i: (0, 0)), sq, sq],
        out_specs=[pl.BlockSpec((tw, 2 * GM_W), lambda i: (i, 0)), sq, sq,
                   pl.BlockSpec((1, GM_W), lambda i: (0, 0))],
        out_shape=[jax.ShapeDtypeStruct((S, 2 * GM_W), BF16), sq_sh, sq_sh,
                   jax.ShapeDtypeStruct((1, GM_W), F32)],
        scratch_shapes=[pltpu.VMEM((tw, GM_W), F32)],
        compiler_params=_cparams(("arbitrary",)), name=name)(proj, dy, v_gain.reshape(1, GM_W), w_s, b_mat)


def _head_mask(h, width):
    lane = lax.broadcasted_iota(jnp.int32, (1, width), 1)
    return (lane >= HEAD_DIM * h) & (lane < HEAD_DIM * (h + 1))


def _mem_probs(q, k, h):
    kh = jnp.where(_head_mask(h, MEM_W), k, jnp.zeros_like(k))
    s = _dot_nt(q, kh) * (HEAD_DIM ** -0.5)
    s = s - jnp.max(s, axis=-1, keepdims=True)
    e = jnp.exp(s)
    return e * (1.0 / jnp.sum(e, axis=-1, keepdims=True)), kh


def memattn_fwd(proj, q_cb, mem_kv, name):
    S = proj.shape[0]
    NM = mem_kv.shape[0]
    tm = _tile(S, ROW_TILE, 8)

    def body(q_ref, kv_ref, o_ref):
        q = _bf(q_ref[...])
        k = _bf(kv_ref[:, :MEM_W])
        v = _bf(kv_ref[:, MEM_W:])
        acc = jnp.zeros((tm, MEM_W), F32)
        for h in range(MEM_HEADS):
            p, _ = _mem_probs(q, k, h)
            vh = jnp.where(_head_mask(h, MEM_W), v, jnp.zeros_like(v))
            acc = acc + _dot(_bf(p), vh)
        o_ref[...] = _bf(acc)

    return pl.pallas_call(
        body, grid=(S // tm,),
        in_specs=[pl.BlockSpec((tm, MEM_W), lambda i: (i, q_cb)), pl.BlockSpec((NM, 2 * MEM_W), lambda i: (0, 0))],
        out_specs=pl.BlockSpec((tm, MEM_W), lambda i: (i, 0)),
        out_shape=jax.ShapeDtypeStruct((S, MEM_W), BF16),
        compiler_params=_cparams(("parallel",)), name=name)(proj, mem_kv)


def memattn_bwd(proj, q_cb, mem_kv, dy, dy_cb, name):
    S = proj.shape[0]
    NM = mem_kv.shape[0]
    tm = _tile(S, ROW_TILE, 8)

    def body(q_ref, kv_ref, do_ref, dq_ref, dkv_ref):
        @pl.when(pl.program_id(0) == 0)
        def _():
            dkv_ref[...] = jnp.zeros_like(dkv_ref)

        q = _bf(q_ref[...])
        k = _bf(kv_ref[:, :MEM_W])
        v = _bf(kv_ref[:, MEM_W:])
        do = _bf(do_ref[...])
        dq = jnp.zeros((tm, MEM_W), F32)
        dk = jnp.zeros((NM, MEM_W), F32)
        dv = jnp.zeros((NM, MEM_W), F32)
        for h in range(MEM_HEADS):
            hm = _head_mask(h, MEM_W)
            p, kh = _mem_probs(q, k, h)
            vh = jnp.where(hm, v, jnp.zeros_like(v))
            dp = _dot_nt(do, vh)
            ds = _bf(p * (dp - jnp.sum(p * dp, axis=-1, keepdims=True)) * (HEAD_DIM ** -0.5))
            dq = dq + _dot(ds, kh)
            dk = dk + jnp.where(hm, _dot_tn(ds, q), 0.0)
            dv = dv + jnp.where(hm, _dot_tn(_bf(p), do), 0.0)
        dq_ref[...] = _bf(dq)
        dkv_ref[:, :MEM_W] += dk
        dkv_ref[:, MEM_W:] += dv

    return pl.pallas_call(
        body, grid=(S // tm,),
        in_specs=[pl.BlockSpec((tm, MEM_W), lambda i: (i, q_cb)), pl.BlockSpec((NM, 2 * MEM_W), lambda i: (0, 0)),
                  pl.BlockSpec((tm, MEM_W), lambda i: (i, dy_cb))],
        out_specs=[pl.BlockSpec((tm, MEM_W), lambda i: (i, 0)), pl.BlockSpec((NM, 2 * MEM_W), lambda i: (0, 0))],
        out_shape=[jax.ShapeDtypeStruct((S, MEM_W), BF16), jax.ShapeDtypeStruct((NM, 2 * MEM_W), F32)],
        compiler_params=_cparams(("arbitrary",)), name=name)(proj, mem_kv, dy)


def _split_bf(v):
    hi = _bf(v)
    return hi, _bf(v - hi.astype(F32))


def _sb_scores(qp, k2, dpos, row, col, tri):
    z = _dot_nt(qp, k2) * (HEAD_DIM ** -0.5)
    t = jnp.log(1.0 + jnp.exp(-jnp.abs(z)))
    lb = jnp.minimum(z, 0.0) - t
    lom = jnp.where(_sb_mask(dpos, row, col), -jnp.maximum(z, 0.0) - t, 0.0)
    l_hi, l_lo = _split_bf(lom)
    insuf = _dot(l_hi, tri) + _dot(l_lo, tri)
    return lb + insuf, jnp.sum(lom, axis=1, keepdims=True), lb


def _sb_mask(dpos, row, col):
    return (col - row) < dpos


def _sb_weights(s, run, dpos, row, col):
    return jnp.where(_sb_mask(dpos, row, col), jnp.exp(s + run), 0.0)


def _sb_alive(runs):
    top = runs[0]
    for r in runs[1:]:
        top = jnp.maximum(top, r)
    return jnp.max(top) > SB_DEAD


def _sb_consts():
    row = lax.broadcasted_iota(jnp.int32, (2 * SB_BLOCK, SB_BLOCK), 0) % SB_BLOCK
    col = lax.broadcasted_iota(jnp.int32, (2 * SB_BLOCK, SB_BLOCK), 1)
    r = lax.broadcasted_iota(jnp.int32, (SB_BLOCK, SB_BLOCK), 0)
    c = lax.broadcasted_iota(jnp.int32, (SB_BLOCK, SB_BLOCK), 1)
    return row, col, _bf(jnp.where(r > c, 1.0, 0.0)), _bf(jnp.where(r >= c, 1.0, 0.0))


def _stack_heads(x2, masks):
    return jnp.concatenate([_bf(jnp.where(hm, x2, jnp.zeros_like(x2))) for hm in masks], axis=0)


def _side_by_side(x):
    return jnp.concatenate([x[:SB_BLOCK], x[SB_BLOCK:]], axis=1)


def sb_fwd(proj, kv, name):
    S = proj.shape[0]
    n_g = GM_W // SB_COLS
    tq = _tile(S, SB_Q_CHUNK, SB_BLOCK)
    nqb = tq // SB_BLOCK

    def body(q_ref, k_ref, v_ref, o_ref, s_ref):
        chunk = pl.program_id(1)
        row, col, tri, _ = _sb_consts()
        masks = [_head_mask(h, LANES) for h in range(2)]
        pairs = [slice(p * LANES, (p + 1) * LANES) for p in range(SB_PAIRS)]

        def q_loop(qi, _):
            i = chunk * nqb + qi
            rows = pl.ds(pl.multiple_of(qi * SB_BLOCK, SB_BLOCK), SB_BLOCK)
            qps = [_stack_heads(q_ref[rows, cs], masks) for cs in pairs]

            def look_ahead(t):
                keys = pl.ds(pl.multiple_of(jnp.maximum(i - t, 0) * SB_BLOCK, SB_BLOCK), SB_BLOCK)
                sums = []
                for p, cs in enumerate(pairs):
                    s, lom_sum, _ = _sb_scores(qps[p], k_ref[keys, cs], t * SB_BLOCK, row, col, tri)
                    s_ref[p] = s
                    sums.append(lom_sum)
                return tuple(sums)

            def k_step(carry):
                t, accs, runs, sums, _ = carry
                keys = pl.ds(pl.multiple_of((i - t) * SB_BLOCK, SB_BLOCK), SB_BLOCK)
                new_accs, new_runs = [], []
                for p, cs in enumerate(pairs):
                    a = _sb_weights(s_ref[p], runs[p], t * SB_BLOCK, row, col)
                    a_hi, a_lo = _split_bf(_side_by_side(a))
                    vp = _stack_heads(v_ref[keys, cs], masks)
                    new_accs.append(accs[p] + _dot(a_hi, vp) + _dot(a_lo, vp))
                    new_runs.append(runs[p] + sums[p])
                return t + 1, tuple(new_accs), tuple(new_runs), look_ahead(t + 1), _sb_alive(new_runs)

            zero = jnp.zeros((2 * SB_BLOCK, 1), F32)
            _, accs, _, _, _ = lax.while_loop(
                lambda carry: jnp.logical_and(carry[0] <= i, carry[4]), k_step,
                (jnp.int32(0), (jnp.zeros((SB_BLOCK, LANES), F32),) * SB_PAIRS, (zero,) * SB_PAIRS,
                 look_ahead(jnp.int32(0)), jnp.bool_(True)))
            for p, cs in enumerate(pairs):
                o_ref[rows, cs] = accs[p]
            return 0

        lax.fori_loop(0, nqb, q_loop, 0)

    return pl.pallas_call(
        body, grid=(n_g, S // tq),
        in_specs=[pl.BlockSpec((tq, SB_COLS), lambda g, c: (c, g)),
                  pl.BlockSpec((S, SB_COLS), lambda g, c: (0, g)),
                  pl.BlockSpec((S, SB_COLS), lambda g, c: (0, n_g + g))],
        out_specs=pl.BlockSpec((tq, SB_COLS), lambda g, c: (c, g)),
        out_shape=jax.ShapeDtypeStruct((S, GM_W), F32),
        scratch_shapes=[pltpu.VMEM((SB_PAIRS, 2 * SB_BLOCK, LANES), F32)],
        compiler_params=_cparams(("parallel", "parallel")), name=name)(proj, kv, kv)


def sb_bwd(proj, kv, dy, out, dk_init, dv_init, name):
    S = proj.shape[0]
    n_g = GM_W // SB_COLS
    tq = _tile(S, SB_Q_CHUNK, SB_BLOCK)
    nqb = tq // SB_BLOCK
    n_chunks = S // tq
    has_init = dk_init is not None
    scale = HEAD_DIM ** -0.5

    def body(*refs):
        s_ref, beta_ref, da_ref = refs[-3:]
        if has_init:
            q_ref, k_ref, v_ref, do_ref, out_ref, dki_ref, dvi_ref, dq_ref, dko_ref, dvo_ref, dk_acc, dv_acc = refs[:-3]
        else:
            q_ref, k_ref, v_ref, do_ref, out_ref, dq_ref, dko_ref, dvo_ref, dk_acc, dv_acc = refs[:-3]
        chunk = pl.program_id(1)
        cols = pl.ds(pl.multiple_of(pl.program_id(0) * SB_COLS, SB_COLS), SB_COLS)

        @pl.when(chunk == 0)
        def _():
            if has_init:
                pltpu.sync_copy(dki_ref.at[:, cols], dk_acc)
                pltpu.sync_copy(dvi_ref.at[:, cols], dv_acc)
            else:
                dk_acc[...] = jnp.zeros_like(dk_acc)
                dv_acc[...] = jnp.zeros_like(dv_acc)

        row, col, tri, tri_inc = _sb_consts()
        masks = [_head_mask(h, LANES) for h in range(2)]
        pairs = [slice(p * LANES, (p + 1) * LANES) for p in range(SB_PAIRS)]

        def q_loop(qi, _):
            i = chunk * nqb + qi
            rows = pl.ds(pl.multiple_of(qi * SB_BLOCK, SB_BLOCK), SB_BLOCK)
            qps = [_stack_heads(q_ref[rows, cs], masks) for cs in pairs]
            dops = [_stack_heads(do_ref[rows, cs], masks) for cs in pairs]
            e_tots = [jnp.sum(dop.astype(F32) * jnp.concatenate([out_ref[rows, cs]] * 2, axis=0), axis=1,
                              keepdims=True) for dop, cs in zip(dops, pairs)]

            def look_ahead(t):
                keys = pl.ds(pl.multiple_of(jnp.maximum(i - t, 0) * SB_BLOCK, SB_BLOCK), SB_BLOCK)
                sums = []
                for p, cs in enumerate(pairs):
                    s, lom_sum, lb = _sb_scores(qps[p], k_ref[keys, cs], t * SB_BLOCK, row, col, tri)
                    s_ref[p] = s
                    beta_ref[p] = jnp.exp(lb)
                    da_ref[p] = _dot_nt(dops[p], v_ref[keys, cs])
                    sums.append(lom_sum)
                return tuple(sums)

            def k_step(carry):
                t, dqs, runs, e_runs, sums, _ = carry
                keys = pl.ds(pl.multiple_of((i - t) * SB_BLOCK, SB_BLOCK), SB_BLOCK)
                new_dqs, new_runs, new_e_runs = [], [], []
                for p, cs in enumerate(pairs):
                    a = _sb_weights(s_ref[p], runs[p], t * SB_BLOCK, row, col)
                    e = a * da_ref[p]
                    e_hi, e_lo = _split_bf(e)
                    before = e_tots[p] - e_runs[p] - (_dot(e_hi, tri_inc) + _dot(e_lo, tri_inc))
                    beta = beta_ref[p]
                    dz = jnp.where(_sb_mask(t * SB_BLOCK, row, col), e * (1.0 - beta) - before * beta, 0.0)
                    dz = _bf(dz * scale)
                    new_dqs.append(dqs[p] + _dot(_side_by_side(dz), _stack_heads(k_ref[keys, cs], masks)))
                    dk_acc[keys, cs] += _dot_tn(dz, qps[p])
                    dv_acc[keys, cs] += _dot_tn(_bf(a), dops[p])
                    new_runs.append(runs[p] + sums[p])
                    new_e_runs.append(e_runs[p] + jnp.sum(e, axis=1, keepdims=True))
                return (t + 1, tuple(new_dqs), tuple(new_runs), tuple(new_e_runs), look_ahead(t + 1),
                        _sb_alive(new_runs))

            zero = jnp.zeros((2 * SB_BLOCK, 1), F32)
            zeros = (zero,) * SB_PAIRS
            _, dqs, _, _, _, _ = lax.while_loop(
                lambda carry: jnp.logical_and(carry[0] <= i, carry[5]), k_step,
                (jnp.int32(0), (jnp.zeros((SB_BLOCK, LANES), F32),) * SB_PAIRS, zeros, zeros,
                 look_ahead(jnp.int32(0)), jnp.bool_(True)))
            for p, cs in enumerate(pairs):
                dq_ref[rows, cs] = dqs[p]
            return 0

        lax.fori_loop(0, nqb, q_loop, 0)

        @pl.when(chunk == n_chunks - 1)
        def _():
            pltpu.sync_copy(dk_acc, dko_ref.at[:, cols])
            pltpu.sync_copy(dv_acc, dvo_ref.at[:, cols])

    qspec = pl.BlockSpec((tq, SB_COLS), lambda g, c: (c, g))
    kspec = pl.BlockSpec((S, SB_COLS), lambda g, c: (0, g))
    in_specs = [qspec, kspec, pl.BlockSpec((S, SB_COLS), lambda g, c: (0, n_g + g)), qspec, qspec]
    ins = [proj, kv, kv, dy, out]
    if has_init:
        in_specs += [_ANY, _ANY]
        ins += [dk_init, dv_init]
    sh = jax.ShapeDtypeStruct((S, GM_W), F32)
    return pl.pallas_call(
        body, grid=(n_g, n_chunks), in_specs=in_specs, out_specs=[qspec, _ANY, _ANY],
        out_shape=[sh, sh, sh],
        scratch_shapes=[pltpu.VMEM((S, SB_COLS), F32), pltpu.VMEM((S, SB_COLS), F32)]
        + [pltpu.VMEM((SB_PAIRS, 2 * SB_BLOCK, LANES), F32)] * 3,
        compiler_params=_cparams(("arbitrary", "arbitrary")), name=name)(*ins)


def adamw(w, g, m, v, name):
    shape = w.shape
    C = shape[-1] if w.ndim > 1 else shape[0]
    R = w.size // C
    tr = _tile(R, ROW_TILE, 8)

    def body(w_ref, g_ref, m_ref, v_ref, d_ref, nm_ref, nv_ref):
        gv = g_ref[...]
        m2 = ADAM_B1 * m_ref[...] + (1.0 - ADAM_B1) * gv
        v2 = ADAM_B2 * v_ref[...] + (1.0 - ADAM_B2) * (gv * gv)
        m_hat = m2 / (1.0 - ADAM_B1 ** ADAM_STEP)
        v_hat = v2 / (1.0 - ADAM_B2 ** ADAM_STEP)
        d_ref[...] = -ADAM_LR * (m_hat / (jnp.sqrt(v_hat) + ADAM_EPS) + ADAM_WD * w_ref[...])
        nm_ref[...] = m2
        nv_ref[...] = v2

    blk = pl.BlockSpec((tr, C), lambda i: (i, 0))
    sh = jax.ShapeDtypeStruct((R, C), F32)
    outs = pl.pallas_call(
        body, grid=(R // tr,), in_specs=[blk] * 4, out_specs=[blk] * 3, out_shape=[sh] * 3,
        compiler_params=_cparams(("parallel",)), name=name)(
            w.reshape(R, C), g.reshape(R, C), m.reshape(R, C), v.reshape(R, C))
    return tuple(o.reshape(shape) for o in outs)


def _place():
    return lax.axis_index("x"), lax.axis_index("y"), lax.axis_index("c")


def _other_chips(x, y):
    return [(1 - x, y), (x, 1 - y), (1 - x, 1 - y)]


_ANY = pl.BlockSpec(memory_space=pl.ANY)
LOCAL_CHUNKS = 8
SEG_LOCAL_CHUNKS = 4


def _remote(src, dst, send_sems, recv_sems, k, to):
    return pltpu.make_async_remote_copy(src_ref=src, dst_ref=dst, send_sem=send_sems.at[k], recv_sem=recv_sems.at[k],
                                        device_id=to, device_id_type=MESH)


def _gather_stage(segs, n_gath, stage):
    def make(ins, outs, send_sems, recv_sems, *local_sems):
        x, y, c = _place()
        sibling = (x, y, 1 - c)
        chips = _other_chips(x, y)
        copies, waits = [], []
        for s, (b, r0, n) in enumerate(segs):
            half = n // 2

            def piece(px, py, pc, b=b, r0=r0, half=half):
                return outs[b].at[2 * px + py, pl.ds(r0 + pc * half, half), :]

            for j, chip in enumerate(chips):
                if stage == 0:
                    mine = ins[n_gath + b].at[pl.ds(r0 + c * half, half), :]
                    cp = _remote(mine, piece(x, y, c), send_sems, recv_sems, 3 * s + j, (*chip, c))
                    landed, sender = piece(*chip, c), (*chip, c)
                else:
                    cp = _remote(piece(*chip, c), piece(*chip, c), send_sems, recv_sems, 3 * s + j, sibling)
                    landed, sender = piece(*chip, 1 - c), sibling
                copies.append(cp)
                waits.append(cp.wait_send)
                waits.append(_remote(landed, landed, send_sems, recv_sems, 3 * s + j, sender).wait_recv)
            if stage == 0:
                rows = n // SEG_LOCAL_CHUNKS
                for q in range(SEG_LOCAL_CHUNKS):
                    r = pl.ds(r0 + q * rows, rows)
                    cp = pltpu.make_async_copy(ins[n_gath + b].at[r, :], outs[b].at[2 * x + y, r, :],
                                               local_sems[0].at[SEG_LOCAL_CHUNKS * s + q])
                    copies.append(cp)
                    waits.append(cp.wait)
        return copies, waits

    return make


def _gather_sems(segs, stage):
    return [3 * len(segs), 3 * len(segs)] + ([SEG_LOCAL_CHUNKS * len(segs)] if stage == 0 else [])


def all_gather_segments(bufs, segs, name):
    nb = len(bufs)
    n0, n1 = len(_gather_sems(segs, 0)), len(_gather_sems(segs, 1))

    def body(*refs):
        in_refs, out_refs = refs[:nb], refs[nb:2 * nb]
        sems = refs[2 * nb:]
        for stage, stage_sems in ((0, sems[:n0]), (1, sems[n0:n0 + n1])):
            copies, waits = _gather_stage(segs, nb, stage)(list(out_refs) + list(in_refs), out_refs, *stage_sems)
            for cp in copies:
                cp.start()
            for wait in waits:
                wait()

    return pl.pallas_call(
        body, in_specs=[_ANY] * nb, out_specs=[_ANY] * nb,
        out_shape=[jax.ShapeDtypeStruct((N_CHIPS,) + b.shape, b.dtype) for b in bufs],
        scratch_shapes=[pltpu.SemaphoreType.DMA((n,)) for n in _gather_sems(segs, 0) + _gather_sems(segs, 1)],
        name=name)(*bufs)


def hosted_gather(gathered, bufs, segs, stage):
    arrays = list(gathered) + (list(bufs) if stage == 0 else [])
    return Hosted(arrays, len(gathered), _gather_sems(segs, stage), _gather_stage(segs, len(gathered), stage))


def _exchange(step, arrays, name):
    n_h, n_alias, n_fresh = len(step.arrays), step.n_out, len(step.fresh)

    def body(*refs):
        h_in = refs[:n_h]
        h_out = refs[n_h:n_h + n_alias + n_fresh]
        copies, waits = step.make(h_in, h_out, *refs[n_h + n_alias + n_fresh:])
        for cp in copies:
            cp.start()
        for wait in waits:
            wait()

    return pl.pallas_call(
        body, in_specs=[_ANY] * n_h, out_specs=[_ANY] * (n_alias + n_fresh),
        out_shape=[jax.ShapeDtypeStruct(a.shape, a.dtype) for a in step.arrays[:n_alias]] + step.fresh,
        scratch_shapes=[pltpu.SemaphoreType.DMA((n,)) for n in step.sems],
        input_output_aliases={k: k for k in range(n_alias)}, name=name)(*arrays)


def swap_step(gs):
    nb = len(gs)

    def make(ins, outs, send_sems, recv_sems):
        x, y, c = _place()
        copies = []
        for b in range(nb):
            half = gs[b].shape[1] // 2
            for k in range(N_CHIPS):
                copies.append(_remote(ins[b].at[k, pl.ds((1 - c) * half, half), :], outs[b].at[k], send_sems,
                                      recv_sems, N_CHIPS * b + k, (x, y, 1 - c)))
        return copies, [cp.wait for cp in copies]

    fresh = [jax.ShapeDtypeStruct((N_CHIPS, g.shape[1] // 2, g.shape[2]), g.dtype) for g in gs]
    return Hosted(list(gs), 0, [N_CHIPS * nb, N_CHIPS * nb], make, fresh)


def _core_index():
    return lax.axis_index("c").astype(jnp.int32).reshape(1)


def add_cores(g, theirs, name):
    n, Ph, Wd = theirs.shape
    tr = _tile(Ph, ROW_TILE, 16)
    steps = Ph // tr

    def body(c_ref, g_ref, t_ref, o_ref):
        o_ref[...] = _bf(g_ref[...] + t_ref[...])

    blk = pl.BlockSpec((None, tr, Wd), lambda k, i, c_ref: (k, i, 0))
    return pl.pallas_call(
        body,
        grid_spec=pltpu.PrefetchScalarGridSpec(
            num_scalar_prefetch=1, grid=(n, steps),
            in_specs=[pl.BlockSpec((None, tr, Wd), lambda k, i, c_ref: (k, c_ref[0] * steps + i, 0)), blk],
            out_specs=blk),
        out_shape=jax.ShapeDtypeStruct((n, Ph, Wd), BF16),
        compiler_params=_cparams(("parallel", "parallel")), name=name)(_core_index(), g, theirs)


def add_chips(slots, name):
    n, Ph, Wd = slots.shape
    tr = _tile(Ph, ROW_TILE, 16)
    steps = Ph // tr

    def body(c_ref, *refs):
        acc = refs[0][...].astype(F32)
        for k in range(1, n):
            acc = acc + refs[k][...].astype(F32)
        refs[n][...] = acc

    in_specs = [pl.BlockSpec((None, tr, Wd), functools.partial(lambda i, c_ref, k: (k, i, 0), k=k)) for k in range(n)]
    return pl.pallas_call(
        body,
        grid_spec=pltpu.PrefetchScalarGridSpec(
            num_scalar_prefetch=1, grid=(steps,), in_specs=in_specs,
            out_specs=pl.BlockSpec((tr, Wd), lambda i, c_ref: (c_ref[0] * steps + i, 0))),
        out_shape=jax.ShapeDtypeStruct((2 * Ph, Wd), F32),
        compiler_params=_cparams(("parallel",)), name=name)(_core_index(), *([slots] * n))


def _whole_tile_chunks(rows, most, tile_rows):
    return max(n for n in range(1, most + 1) if rows % (n * tile_rows) == 0)


def scatter_step(hs):
    nb = len(hs)
    chunks = [_whole_tile_chunks(h.shape[1], LOCAL_CHUNKS, 16) for h in hs]

    def make(ins, outs, send_sems, recv_sems, local_sems):
        x, y, c = _place()
        me = 2 * x + y
        copies, waits = [], []
        for b in range(nb):
            for j, (cx, cy) in enumerate(_other_chips(x, y)):
                cp = _remote(ins[b].at[2 * cx + cy], outs[b].at[me], send_sems, recv_sems, 3 * b + j, (cx, cy, c))
                copies.append(cp)
                waits.append(cp.wait_send)
                landed = outs[b].at[2 * cx + cy]
                waits.append(_remote(landed, landed, send_sems, recv_sems, 3 * b + j, (cx, cy, c)).wait_recv)
            rows = hs[b].shape[1] // chunks[b]
            for q in range(chunks[b]):
                r = pl.ds(q * rows, rows)
                cp = pltpu.make_async_copy(ins[b].at[me, r, :], outs[b].at[me, r, :],
                                           local_sems.at[sum(chunks[:b]) + q])
                copies.append(cp)
                waits.append(cp.wait)
        return copies, waits

    fresh = [jax.ShapeDtypeStruct(h.shape, h.dtype) for h in hs]
    return Hosted(list(hs), 0, [3 * nb, 3 * nb, sum(chunks)], make, fresh)


JOIN_CHUNKS = 4


def join_step(ts):
    nb = len(ts)
    n = JOIN_CHUNKS

    def make(ins, outs, send_sems, recv_sems):
        x, y, c = _place()
        copies, waits = [], []
        for b in range(nb):
            half = ts[b].shape[0] // 2
            rows = half // n
            for k in range(n):
                part = outs[b].at[pl.ds(c * half + k * rows, rows), :]
                cp = _remote(part, part, send_sems, recv_sems, n * b + k, (x, y, 1 - c))
                copies.append(cp)
                waits.append(cp.wait_send)
                landed = outs[b].at[pl.ds((1 - c) * half + k * rows, rows), :]
                waits.append(_remote(landed, landed, send_sems, recv_sems, n * b + k, (x, y, 1 - c)).wait_recv)
        return copies, waits

    return Hosted(list(ts), nb, [n * nb, n * nb], make)


WIDE = ['ffn1_w_gate', 'ffn1_w_up', 'ffn2_w_gate', 'ffn2_w_up']


def _rows_of(p):
    if p.shape[-1] == PACK_W and (p.size // PACK_W) % PART_ROWS == 0:
        return p.reshape(-1, PACK_W)
    flat = p.reshape(-1)
    rows = -(-flat.shape[0] // (PACK_W * PART_ROWS)) * PART_ROWS
    return jnp.pad(flat, (0, rows * PACK_W - flat.shape[0])).reshape(rows, PACK_W)


def _pack(parts):
    buf = jnp.concatenate([_rows_of(p) for p in parts], axis=0)
    rows = buf.shape[0]
    total = -(-rows // PACK_ROWS) * PACK_ROWS
    return jnp.pad(buf, ((0, total - rows), (0, 0)))


def _unpack(buf, shapes):
    outs, r = [], 0
    lead = buf.shape[:-2]
    for shp in shapes:
        n = math.prod(shp)
        rows = -(-n // (PACK_W * PART_ROWS)) * PART_ROWS
        blk = buf[..., r:r + rows, :]
        if n != rows * PACK_W:
            blk = blk.reshape(lead + (rows * PACK_W,))[..., :n]
        outs.append(blk.reshape(lead + tuple(shp)))
        r += rows
    return outs


class WeightStream:
    def __init__(self, shards):
        depth = shards['ffn1_w_gate'].shape[0]
        order = [(which, l) for l in range(depth) for which in ('ffn1', 'ffn2')]
        self.count = len(order)
        gate_up = [shards[f'{which}_w_{part}'][l].astype(BF16) for which, l in order for part in ('gate', 'up')]
        downs = [shards[f'{which}_w_down'][l].astype(BF16) for which, l in order]
        self.d, self.cols = gate_up[0].shape
        assert downs[0].shape == (self.cols, PACK_W)
        self.rest_names = [n for n in W_NAMES if n in SHARD_AXIS and n not in WIDE and not n.endswith('_w_down')]
        rest = [lax.bitcast_convert_type(shards[n], BF16) if n == 'a_v_norm' else shards[n].astype(BF16)
                for n in self.rest_names]
        self.rest_shapes = [p.shape for p in rest]
        self.bufs = [jnp.concatenate(gate_up, axis=0), _pack(downs + rest)]
        self.rest_row0 = self.count * self.cols
        first = self.segments(0) + [(1, self.rest_row0, self.bufs[1].shape[0] - self.rest_row0)]
        self.gathered = all_gather_segments(self.bufs, first, "all_gather_first")

    def segments(self, i):
        return [(0, 2 * self.d * i, 2 * self.d), (1, self.cols * i, self.cols)]

    def rest(self):
        out = {}
        for n, g in zip(self.rest_names, _unpack(self.gathered[1][:, self.rest_row0:], self.rest_shapes)):
            if n == 'a_v_norm':
                g = lax.bitcast_convert_type(g, F32)
            out[n] = jnp.concatenate([g[k] for k in range(N_CHIPS)], axis=SHARD_AXIS[n])
        return out

    def ffn(self, i):
        wide, narrow = self.gathered
        r = 2 * self.d * i
        gate = jnp.concatenate([wide[k, r:r + self.d] for k in range(N_CHIPS)], axis=1)
        up = jnp.concatenate([wide[k, r + self.d:r + 2 * self.d] for k in range(N_CHIPS)], axis=1)
        down = narrow[:, self.cols * i:self.cols * (i + 1)].reshape(N_CHIPS * self.cols, PACK_W)
        return (gate, None), (up, None), (down, None)

    def hosted(self, i, stage):
        if i >= self.count:
            return None
        return hosted_gather(self.gathered, self.bufs, self.segments(i), stage)

    def absorb(self, gathered):
        self.gathered = list(gathered)


class GradSink:
    def __init__(self):
        self.queue = []
        self.totals = {}
        self.held = {}

    def add(self, i, d_wg, d_wu, d_wd, ride):
        cols = d_wg.shape[1] // N_CHIPS
        wide = jnp.stack([jnp.concatenate([d_wg[:, k * cols:(k + 1) * cols], d_wu[:, k * cols:(k + 1) * cols]],
                                          axis=0) for k in range(N_CHIPS)], axis=0)
        bufs = [wide, d_wd.reshape(N_CHIPS, cols, d_wd.shape[1])]
        if ride:
            self.queue.append([0, bufs, i])
        else:
            self.held[i] = bufs

    def step(self):
        if not self.queue:
            return None
        stage, bufs, _ = self.queue[0]
        return (swap_step, scatter_step, join_step)[stage](bufs)

    def absorb(self, results):
        entry = self.queue[0]
        stage, bufs, i = entry
        if stage == 0:
            entry[1] = [add_cores(g, t, f"ffn{i}_add_cores_{b}") for b, (g, t) in enumerate(zip(bufs, results))]
        elif stage == 1:
            entry[1] = [add_chips(s, f"ffn{i}_add_chips_{b}") for b, s in enumerate(results)]
        else:
            self.totals[i] = list(results)
            self.queue.pop(0)
        entry[0] = stage + 1

    def reduce(self, grads, shard_shapes):
        assert not self.queue
        narrow = [n for n in W_NAMES if n in SHARD_AXIS and n in grads]
        repl = [n for n in W_NAMES if n not in SHARD_AXIS]
        full = {n: (jnp.stack(g, axis=0) if isinstance(g, list) else g) for n, g in grads.items()}
        blocks = []
        for k in range(N_CHIPS):
            parts = [jnp.split(full[n], N_CHIPS, axis=SHARD_AXIS[n])[k] for n in narrow]
            parts += [full[n] for n in repl]
            blocks.append(_pack(parts))
        held = sorted(self.held)
        gs = [b for i in held for b in self.held[i]] + [jnp.stack(blocks, axis=0)]
        theirs = _exchange(swap_step(gs), gs, "grads_swap_halves")
        sums = [add_cores(g, t, f"grads_add_cores_{b}") for b, (g, t) in enumerate(zip(gs, theirs))]
        slots = _exchange(scatter_step(sums), sums, "grads_scatter")
        halves = [add_chips(s, f"grads_add_chips_{b}") for b, s in enumerate(slots)]
        totals = _exchange(join_step(halves), halves, "grads_join_halves")
        for n, i in enumerate(held):
            self.totals[i] = totals[2 * n:2 * n + 2]
        shapes = [shard_shapes[n] for n in narrow] + [full[n].shape for n in repl]
        out = dict(zip(narrow + repl, _unpack(totals[-1], shapes)))
        d = self.totals[0][0].shape[0] // 2
        for w, which in enumerate(('ffn1', 'ffn2')):
            mine = [self.totals[i] for i in sorted(self.totals) if i % 2 == w]
            out[f'{which}_w_gate'] = jnp.stack([t[0][:d] for t in mine], axis=0)
            out[f'{which}_w_up'] = jnp.stack([t[0][d:] for t in mine], axis=0)
            out[f'{which}_w_down'] = jnp.stack([t[1] for t in mine], axis=0)
        return out


def _ffn_fwd(x, h, stream, i, next_gain, tag):
    wg, wu, wd = stream.ffn(i)
    ici = stream.hosted(i + 1, 0)
    if ici is None:
        g, u, a = ffn_up(h, wg, wu, tag + "_up")
        out = mm([(a, 0, wd, a.shape[1], 0)], res=x, scale=0.5, norm_next=next_gain, name=tag + "_down")
    else:
        (g, u, a), gathered = ffn_up(h, wg, wu, tag + "_up", hosted=ici)
        stream.absorb(gathered)
        out, gathered = mm([(a, 0, wd, a.shape[1], 0)], res=x, scale=0.5, norm_next=next_gain,
                           name=tag + "_down", hosted=stream.hosted(i + 1, 1))
        stream.absorb(gathered)
    xo, h_next = (out, None) if next_gain is None else out
    return xo, h_next, (x, h, g, u, a, wg, wu, wd)


def _carrying(sink, kernel, *args, **kwargs):
    step = sink.step()
    if step is None:
        return kernel(*args, **kwargs)
    out, results = kernel(*args, hosted=step, **kwargs)
    sink.absorb(results)
    return out


def _ffn_bwd(dxo, saved, gain, sink, i, tag):
    x, h, g, u, a, wg, wu, wd = saved
    Fd = a.shape[1]
    dxo, dxo_b = dxo
    dgp, du, d_wd = _carrying(sink, ffn_down_bwd, dxo_b, wd, g, u, a, tag + "_down_bwd")
    d_wg, d_wu = _carrying(sink, mm_tn_pair, h, dgp, du, name=tag + "_dwgu")
    sink.add(i, d_wg, d_wu, d_wd, ride=i > 0)
    return _carrying(sink, mm, [(dgp, 0, wg, Fd, 0), (du, 0, wu, Fd, 0)], tb=True, res=dxo, norm_bwd=(x, gain),
                     name=tag + "_up_bwd")


def local_step(x, mem, W, stream, sink, target):
    D = x.shape[1]
    depth = W['ffn1_norm'].shape[0]
    n_a = W['a_w_in'].shape[0]
    G = {}
    mem_h = rms_fwd(mem, W['mem_norm'], "mem_norm")
    b_mats = jnp.broadcast_to(W['a_b_spatial'][..., None], W['a_b_spatial'].shape + (GM_CHUNK,))
    w_kv = (W['w_kv'], None)

    def mixer_w(l):
        if l < n_a:
            return (W['a_w_in'], l), (W['a_w_out'], l), 2 * GM_W
        return (W['b_w_in'], l - n_a), (W['b_w_out'], l - n_a), GM_W

    saved = []
    kv = kvn = x_kv = None
    h = rms_fwd(x, W['ffn1_norm'][0], "l0_ffn1_norm")
    for l in range(depth):
        if l == n_a:
            x_kv = x
            kvn = rms_fwd(x, W['kv_norm'], "kv_norm")
            kv = mm([(kvn, 0, w_kv, D, 0)], out_dtype=BF16, name="kv_proj")
        x, hm, s1 = _ffn_fwd(x, h, stream, 2 * l, W['mix_norm'][l], f"l{l}_ffn1")
        mem_kv = mm([(mem_h, 0, (W['w_mem_kv'], l), D, 0)], name=f"l{l}_mem_kv")
        w_in, w_out, tok_w = mixer_w(l)
        proj = mm([(hm, 0, w_in, D, 0)], name=f"l{l}_mix_in")
        if l < n_a:
            y_tok = gmlp_fwd(proj, W['a_v_norm'][l], W['a_w_spatial'][l], b_mats[l], f"l{l}_gmlp")
        else:
            y_tok = sb_fwd(proj, kv, f"l{l}_sb")
        q_cb = tok_w // MEM_W
        y_mem = memattn_fwd(proj, q_cb, mem_kv, f"l{l}_memattn")
        x_mid = x
        x, h2 = mm([(y_tok, 0, w_out, GM_W, 0), (y_mem, 0, w_out, MEM_W, GM_W // MEM_W)], res=x,
                   norm_next=W['ffn2_norm'][l], name=f"l{l}_mix_out")
        sm = (x_mid, hm, mem_kv, proj, y_tok, y_mem, q_cb)
        next_gain = W['ffn1_norm'][l + 1] if l + 1 < depth else None
        x, h, s2 = _ffn_fwd(x, h2, stream, 2 * l + 1, next_gain, f"l{l}_ffn2")
        saved.append((s1, sm, s2))

    loss, dx, d_final = final_loss(x, W['final_norm'], target, "loss_head")
    G['final_norm'] = d_final.reshape(-1)

    per_layer = {n: [None] * depth for n in ['ffn1_norm', 'mix_norm', 'ffn2_norm', 'w_mem_kv']}
    per_a = {n: [None] * n_a for n in ['a_w_in', 'a_v_norm', 'a_w_spatial', 'a_b_spatial', 'a_w_out']}
    per_b = {n: [None] * (depth - n_a) for n in ['b_w_in', 'b_w_out']}
    d_mem_h = None
    dk = dv = None
    for l in reversed(range(depth)):
        s1, sm, s2 = saved[l]
        dx, dg = _ffn_bwd(dx, s2, W['ffn2_norm'][l], sink, 2 * l + 1, f"l{l}_ffn2")
        per_layer['ffn2_norm'][l] = dg.reshape(-1)

        x_mid, hm, mem_kv, proj, y_tok, y_mem, q_cb = sm
        is_a = l < n_a
        w_in, w_out, tok_w = mixer_w(l)
        dy = mm([(dx[1], 0, w_out, D, 0)], tb=True, name=f"l{l}_mix_out_bwd")
        d_w_out = jnp.concatenate([mm_tn(y_tok, dx[1], name=f"l{l}_dwout_tok"),
                                   mm_tn(y_mem, dx[1], name=f"l{l}_dwout_mem")], axis=0)
        dq_mem, d_mem_kv = memattn_bwd(proj, q_cb, mem_kv, dy, GM_W // MEM_W, f"l{l}_memattn_bwd")
        if is_a:
            d_tok, d_ws, d_bs, d_vg = gmlp_bwd(proj, dy, W['a_v_norm'][l], W['a_w_spatial'][l], b_mats[l],
                                               f"l{l}_gmlp_bwd")
            per_a['a_w_spatial'][l], per_a['a_b_spatial'][l] = d_ws, d_bs[:, :, 0]
            per_a['a_v_norm'][l], per_a['a_w_out'][l] = d_vg.reshape(-1), d_w_out
        else:
            d_tok, dk, dv = sb_bwd(proj, kv, dy, y_tok, dk, dv, f"l{l}_sb_bwd")
            per_b['b_w_out'][l - n_a] = d_w_out
        d_w_in = jnp.concatenate([mm_tn(hm, d_tok, name=f"l{l}_dwin_tok"),
                                  mm_tn(hm, dq_mem, name=f"l{l}_dwin_mem")], axis=1)
        (per_a['a_w_in'] if is_a else per_b['b_w_in'])[l if is_a else l - n_a] = d_w_in
        dx, dg = mm([(d_tok, 0, w_in, tok_w, 0), (dq_mem, 0, w_in, MEM_W, tok_w // MEM_W)], tb=True, res=dx[0],
                    norm_bwd=(x_mid, W['mix_norm'][l]), name=f"l{l}_mix_in_bwd")
        per_layer['mix_norm'][l] = dg.reshape(-1)
        per_layer['w_mem_kv'][l] = mm_tn(mem_h, d_mem_kv, name=f"l{l}_dw_mem_kv")
        d_mem_h = mm([(d_mem_kv, 0, (W['w_mem_kv'], l), 2 * MEM_W, 0)], tb=True, res=d_mem_h,
                     name=f"l{l}_mem_kv_bwd")

        dx, dg = _ffn_bwd(dx, s1, W['ffn1_norm'][l], sink, 2 * l, f"l{l}_ffn1")
        per_layer['ffn1_norm'][l] = dg.reshape(-1)
        if l == n_a:
            G['w_kv'] = jnp.concatenate([mm_tn(kvn, dk, name="dw_k"), mm_tn(kvn, dv, name="dw_v")], axis=1)
            dx, dg = mm([(dk, 0, w_kv, GM_W, 0), (dv, 0, w_kv, GM_W, 1)], tb=True, res=dx[0],
                        norm_bwd=(x_kv, W['kv_norm']), name="kv_proj_bwd")
            G['kv_norm'] = dg.reshape(-1)

    G['mem_norm'] = rms_gain_grad(mem, d_mem_h, "mem_norm_bwd").reshape(-1)
    for d in (per_layer, per_a, per_b):
        G.update(d)
    return loss, dx[0], G


def kernel(x, mem, ffn1_norm, ffn1_w_gate, ffn1_w_up, ffn1_w_down, mix_norm, ffn2_norm, ffn2_w_gate, ffn2_w_up, ffn2_w_down, mem_norm, w_mem_kv, a_w_in, a_v_norm, a_w_spatial, a_b_spatial, a_w_out, kv_norm, w_kv, b_w_in, b_w_out, final_norm, loss_target, m_ffn1_norm, m_ffn1_w_gate, m_ffn1_w_up, m_ffn1_w_down, m_mix_norm, m_ffn2_norm, m_ffn2_w_gate, m_ffn2_w_up, m_ffn2_w_down, m_mem_norm, m_w_mem_kv, m_a_w_in, m_a_v_norm, m_a_w_spatial, m_a_b_spatial, m_a_w_out, m_kv_norm, m_w_kv, m_b_w_in, m_b_w_out, m_final_norm, v_ffn1_norm, v_ffn1_w_gate, v_ffn1_w_up, v_ffn1_w_down, v_mix_norm, v_ffn2_norm, v_ffn2_w_gate, v_ffn2_w_up, v_ffn2_w_down, v_mem_norm, v_w_mem_kv, v_a_w_in, v_a_v_norm, v_a_w_spatial, v_a_b_spatial, v_a_w_out, v_kv_norm, v_w_kv, v_b_w_in, v_b_w_out, v_final_norm):
    weights = dict(zip(W_NAMES, [ffn1_norm, ffn1_w_gate, ffn1_w_up, ffn1_w_down, mix_norm, ffn2_norm, ffn2_w_gate,
                                 ffn2_w_up, ffn2_w_down, mem_norm, w_mem_kv, a_w_in, a_v_norm, a_w_spatial,
                                 a_b_spatial, a_w_out, kv_norm, w_kv, b_w_in, b_w_out, final_norm]))
    m_in = dict(zip(W_NAMES, [m_ffn1_norm, m_ffn1_w_gate, m_ffn1_w_up, m_ffn1_w_down, m_mix_norm, m_ffn2_norm,
                              m_ffn2_w_gate, m_ffn2_w_up, m_ffn2_w_down, m_mem_norm, m_w_mem_kv, m_a_w_in,
                              m_a_v_norm, m_a_w_spatial, m_a_b_spatial, m_a_w_out, m_kv_norm, m_w_kv, m_b_w_in,
                              m_b_w_out, m_final_norm]))
    v_in = dict(zip(W_NAMES, [v_ffn1_norm, v_ffn1_w_gate, v_ffn1_w_up, v_ffn1_w_down, v_mix_norm, v_ffn2_norm,
                              v_ffn2_w_gate, v_ffn2_w_up, v_ffn2_w_down, v_mem_norm, v_w_mem_kv, v_a_w_in,
                              v_a_v_norm, v_a_w_spatial, v_a_b_spatial, v_a_w_out, v_kv_norm, v_w_kv, v_b_w_in,
                              v_b_w_out, v_final_norm]))

    stream = WeightStream({n: weights[n] for n in SHARD_AXIS})
    W = {n: weights[n] for n in W_NAMES if n not in SHARD_AXIS}
    W.update(stream.rest())
    sink = GradSink()
    loss, dx, grads = local_step(x[0], mem[0], W, stream, sink, loss_target[0])
    total = sink.reduce(grads, {n: weights[n].shape for n in SHARD_AXIS})
    loss = lax.psum(loss[0, 0], ("x", "y", "c"))

    deltas, new_m, new_v = {}, {}, {}
    for n in W_NAMES:
        deltas[n], new_m[n], new_v[n] = adamw(weights[n], total[n], m_in[n], v_in[n], "adamw_" + n)
    return (loss, dx[None], *[total[n] for n in W_NAMES], *[deltas[n] for n in W_NAMES],
            *[new_m[n] for n in W_NAMES], *[new_v[n] for n in W_NAMES])
```
